```python
import jax, jax.numpy as jnp
from jax import lax
import numpy as np

D_MODEL = 1024
BATCH = 16
SEQ = 256
DEPTH = 2
DEC_BATCH = 2
DEC_SEQ = 1024
PAST_LEN = 512

GRID_W = 64
EPS = 1e-6
A_HEADS = 4
A_HEAD_DIM = 64
A_WIDTH = A_HEADS * A_HEAD_DIM
CHUNK = 128
F_GROUPS = 4
F_GROUP_DIM = 64
F_WIDTH = F_GROUPS * F_GROUP_DIM
MLA_HEADS = 4
Q_LORA_RANK = 256
KV_LORA_RANK = 128
QK_NOPE_DIM = 128
QK_ROPE_DIM = 64
V_HEAD_DIM = 128
MLA_WIDTH = MLA_HEADS * V_HEAD_DIM
ROPE_AXIS_DIM = QK_ROPE_DIM // 2
ROPE_BASE = 10000.0
Q_BLOCK = 128
MIX_WIDTH = A_WIDTH + F_WIDTH + MLA_WIDTH
IN_COLS = 2 * A_WIDTH + F_WIDTH + Q_LORA_RANK + KV_LORA_RANK + QK_ROPE_DIM
N_EXPERTS = 32
TOP_K = 4
D_FF = D_MODEL
SWIGLU_LIMIT = 7.0
SWIGLU_ALPHA = 1.702
MOE_BLOCK = 128

kernel_name = 'hymba_gmlp_fnet_mla_moe_diffusion_step'


def rmsnorm(x, g):
    xf = x.astype(jnp.float32)
    y = xf * lax.rsqrt(jnp.mean(xf * xf, axis=-1, keepdims=True) + EPS)
    return (y * g.astype(jnp.float32)).astype(x.dtype)


def group_layernorm(v, g):
    vf = v.astype(jnp.float32)
    mu = jnp.mean(vf, axis=-1, keepdims=True)
    var = jnp.mean(jnp.square(vf - mu), axis=-1, keepdims=True)
    y = (vf - mu) * lax.rsqrt(var + EPS)
    return (y * g.reshape(A_HEADS, A_HEAD_DIM).astype(jnp.float32)).astype(v.dtype)


def modulation(cond, w_ada, b_ada):
    m = jax.nn.silu(cond) @ w_ada + b_ada
    return jnp.split(m[:, None, :], 6, axis=-1)


def chunk_gating_mlp(z, g_sgu, w_s, b_s):
    B, T, _ = z.shape
    z = jax.nn.gelu(z)
    u, v = z[..., :A_WIDTH], z[..., A_WIDTH:]
    v = group_layernorm(v.reshape(B, T, A_HEADS, A_HEAD_DIM), g_sgu)
    v = v.reshape(B, T // CHUNK, CHUNK, A_HEADS, A_HEAD_DIM)
    s = jnp.einsum('gpq,bnqgd->bnpgd', w_s, v) + b_s.T[:, :, None]
    return u * s.reshape(B, T, A_WIDTH)


def fourier_mix(z):
    B, T, _ = z.shape
    zf = z.astype(jnp.float32).reshape(B, T, F_GROUPS, F_GROUP_DIM)
    y = jnp.fft.fft2(zf, axes=(1, 3), norm='ortho').real
    return y.reshape(B, T, F_WIDTH).astype(z.dtype)


def _rotate(x, ang):
    n = x.shape[-1] // 2
    cos = jnp.cos(ang)[None, :, None, :].astype(x.dtype)
    sin = jnp.sin(ang)[None, :, None, :].astype(x.dtype)
    x1, x2 = x[..., :n], x[..., n:]
    return jnp.concatenate([x1 * cos - x2 * sin, x2 * cos + x1 * sin], axis=-1)


def axial_rope(x):
    T = x.shape[1]
    rows = T // GRID_W
    pos_r = jnp.repeat(jnp.arange(rows, dtype=jnp.float32), GRID_W)
    pos_c = jnp.tile(jnp.arange(GRID_W, dtype=jnp.float32), rows)
    n = ROPE_AXIS_DIM // 2
    inv_freq = jnp.power(ROPE_BASE, -jnp.arange(n, dtype=jnp.float32) / n)
    xr = _rotate(x[..., :ROPE_AXIS_DIM], pos_r[:, None] * inv_freq)
    xc = _rotate(x[..., ROPE_AXIS_DIM:], pos_c[:, None] * inv_freq)
    return jnp.concatenate([xr, xc], axis=-1)


def block_attention(q, k, v):
    B, Tq, H, Dk = q.shape
    Dv = v.shape[-1]
    scale = Dk ** -0.5
    qb = q.reshape(B, Tq // Q_BLOCK, Q_BLOCK, H, Dk).transpose(1, 0, 2, 3, 4)

    def attend_block(qblk):
        s = jnp.einsum('bqhd,bkhd->bhqk', qblk, k, preferred_element_type=jnp.float32) * scale
        p = jax.nn.softmax(s, axis=-1).astype(v.dtype)
        return jnp.einsum('bhqk,bkhd->bqhd', p, v)

    o = lax.map(attend_block, qb)
    return o.transpose(1, 0, 2, 3, 4).reshape(B, Tq, H * Dv)


def mla_expand(ckv, k_rope, w_ukv):
    B, S, _ = ckv.shape
    kv = (ckv @ w_ukv).reshape(B, S, MLA_HEADS, QK_NOPE_DIM + V_HEAD_DIM)
    k_r = jnp.broadcast_to(k_rope[:, :, None, :], (B, S, MLA_HEADS, QK_ROPE_DIM))
    k = jnp.concatenate([kv[..., :QK_NOPE_DIM], k_r], axis=-1)
    return k, kv[..., QK_NOPE_DIM:]


def token_mixers(h, lw, ctx_ckv=None, ctx_krope=None):
    B, T, _ = h.shape
    z = h @ lw['w_in']
    o1 = 2 * A_WIDTH
    o2 = o1 + F_WIDTH
    o3 = o2 + Q_LORA_RANK
    o4 = o3 + KV_LORA_RANK
    za, zf, zq, zkv, zkr = z[..., :o1], z[..., o1:o2], z[..., o2:o3], z[..., o3:o4], z[..., o4:]
    y_a = chunk_gating_mlp(za, lw['g_sgu'], lw['w_spatial'], lw['b_spatial'])
    y_f = fourier_mix(zf)
    q = (rmsnorm(zq, lw['g_q']) @ lw['w_uq']).reshape(B, T, MLA_HEADS, QK_NOPE_DIM + QK_ROPE_DIM)
    q_nope, q_rope = q[..., :QK_NOPE_DIM], q[..., QK_NOPE_DIM:]
    ckv = rmsnorm(zkv, lw['g_kv'])
    if ctx_ckv is None:
        k, v = mla_expand(ckv, zkr, lw['w_ukv'])
        q = jnp.concatenate([q_nope, q_rope], axis=-1)
        new_ctx = (ckv, zkr)
    else:
        q = jnp.concatenate([q_nope, axial_rope(q_rope)], axis=-1)
        k_rope_lat = axial_rope(zkr[:, :, None, :])[:, :, 0, :]
        k, v = mla_expand(jnp.concatenate([ctx_ckv, ckv], axis=1),
                          jnp.concatenate([ctx_krope, k_rope_lat], axis=1), lw['w_ukv'])
        new_ctx = None
    y_c = block_attention(q, k, v)
    y = jnp.concatenate([y_a, y_f, y_c], axis=-1) @ lw['w_out']
    return y, new_ctx


def clamped_swiglu(gu):
    g, l = gu[..., :D_FF], gu[..., D_FF:]
    g = jnp.minimum(g, SWIGLU_LIMIT)
    l = jnp.clip(l, -SWIGLU_LIMIT, SWIGLU_LIMIT)
    return g * jax.nn.sigmoid(SWIGLU_ALPHA * g) * (l + 1.0)


def moe_ffn(h, lw):
    B, T, D = h.shape
    xt = h.reshape(-1, D)
    N = xt.shape[0]
    logits = (xt @ lw['w_router'] + lw['b_router']).astype(jnp.float32)
    top_logit, top_idx = lax.top_k(logits, TOP_K)
    top_w = jax.nn.softmax(top_logit, axis=-1).astype(xt.dtype)
    n_assign = N * TOP_K
    flat_e = top_idx.reshape(-1)
    flat_tok = jnp.repeat(jnp.arange(N, dtype=jnp.int32), TOP_K)
    flat_w = top_w.reshape(-1)
    order = jnp.argsort(flat_e)
    e_sorted = flat_e[order]
    counts = jnp.bincount(flat_e, length=N_EXPERTS)
    padded = (counts + MOE_BLOCK - 1) // MOE_BLOCK * MOE_BLOCK
    pad_end = jnp.cumsum(padded)
    pad_start = pad_end - padded
    start = jnp.cumsum(counts) - counts
    dest = pad_start[e_sorted] + jnp.arange(n_assign) - start[e_sorted]
    n_blocks = -(-n_assign // MOE_BLOCK) + N_EXPERTS
    n_rows = n_blocks * MOE_BLOCK
    row_tok = jnp.zeros((n_rows,), jnp.int32).at[dest].set(flat_tok[order])
    row_w = jnp.zeros((n_rows,), xt.dtype).at[dest].set(flat_w[order])
    block_e = jnp.minimum(jnp.searchsorted(pad_end, jnp.arange(n_blocks) * MOE_BLOCK, side='right'),
                          N_EXPERTS - 1)
    xs = xt[row_tok].reshape(n_blocks, MOE_BLOCK, D)
    w_gu, b_gu, w_dn, b_dn = lw['w_gate_up'], lw['b_gate_up'], lw['w_down'], lw['b_down']

    def expert_block(args):
        xb, e = args
        gu = xb @ w_gu[e] + b_gu[e]
        return clamped_swiglu(gu) @ w_dn[e] + b_dn[e]

    ys = lax.map(expert_block, (xs, block_e)).reshape(n_rows, D)
    y = jax.ops.segment_sum(ys * row_w[:, None], row_tok, num_segments=N)
    return y.reshape(B, T, D)


def trunk_layer(x, mod, lw, ctx_ckv=None, ctx_krope=None):
    sh1, sc1, g1, sh2, sc2, g2 = mod
    h = rmsnorm(x, lw['g_pre_mix']) * (1.0 + sc1) + sh1
    y, new_ctx = token_mixers(h, lw, ctx_ckv, ctx_krope)
    x = x + g1 * rmsnorm(y, lw['g_post_mix'])
    h = rmsnorm(x, lw['g_pre_ffn']) * (1.0 + sc2) + sh2
    x = x + g2 * rmsnorm(moe_ffn(h, lw), lw['g_post_ffn'])
    return x, new_ctx


def setup_inputs(seed: int = 0) -> dict:
    key = jax.random.key(seed)
    ks = jax.random.split(key, 32)
    nrm = jax.random.normal
    D = D_MODEL
    return {
        'x_prompt': nrm(ks[0], (BATCH, SEQ, D), jnp.float32),
        'x_sample': nrm(ks[1], (DEC_BATCH, DEC_SEQ, D), jnp.float32),
        'cache_ckv': nrm(ks[2], (DEC_BATCH, DEPTH, PAST_LEN, KV_LORA_RANK), jnp.float32),
        'cache_krope': nrm(ks[3], (DEC_BATCH, DEPTH, PAST_LEN, QK_ROPE_DIM), jnp.float32),
        'c': nrm(ks[4], (DEC_BATCH, D), jnp.float32),
        'c_ctx': nrm(ks[5], (D,), jnp.float32),
        'w_ada': nrm(ks[6], (DEPTH, D, 6 * D), jnp.float32) * (0.5 * D ** -0.5),
        'b_ada': nrm(ks[7], (DEPTH, 6 * D), jnp.float32) * 0.02,
        'g_pre_mix': 1.0 + 0.05 * nrm(ks[8], (DEPTH, D), jnp.float32),
        'g_post_mix': 1.0 + 0.05 * nrm(ks[9], (DEPTH, D), jnp.float32),
        'g_pre_ffn': 1.0 + 0.05 * nrm(ks[10], (DEPTH, D), jnp.float32),
        'g_post_ffn': 1.0 + 0.05 * nrm(ks[11], (DEPTH, D), jnp.float32),
        'w_in': nrm(ks[12], (DEPTH, D, IN_COLS), jnp.float32) * D ** -0.5,
        'g_sgu': 1.0 + 0.05 * nrm(ks[13], (DEPTH, A_WIDTH), jnp.float32),
        'w_spatial': nrm(ks[14], (DEPTH, A_HEADS, CHUNK, CHUNK), jnp.float32) * CHUNK ** -0.5,
        'b_spatial': 1.0 + 0.02 * nrm(ks[15], (DEPTH, A_HEADS, CHUNK), jnp.float32),
        'g_q': 1.0 + 0.05 * nrm(ks[16], (DEPTH, Q_LORA_RANK), jnp.float32),
        'w_uq': nrm(ks[17], (DEPTH, Q_LORA_RANK, MLA_HEADS * (QK_NOPE_DIM + QK_ROPE_DIM)), jnp.float32) * Q_LORA_RANK ** -0.5,
        'g_kv': 1.0 + 0.05 * nrm(ks[18], (DEPTH, KV_LORA_RANK), jnp.float32),
        'w_ukv': nrm(ks[19], (DEPTH, KV_LORA_RANK, MLA_HEADS * (QK_NOPE_DIM + V_HEAD_DIM)), jnp.float32) * KV_LORA_RANK ** -0.5,
        'w_out': nrm(ks[20], (DEPTH, MIX_WIDTH, D), jnp.float32) * MIX_WIDTH ** -0.5,
        'w_router': nrm(ks[21], (DEPTH, D, N_EXPERTS), jnp.float32) * D ** -0.5,
        'b_router': nrm(ks[22], (DEPTH, N_EXPERTS), jnp.float32) * 0.01,
        'w_gate_up': nrm(ks[23], (DEPTH, N_EXPERTS, D, 2 * D_FF), jnp.float32) * D ** -0.5,
        'b_gate_up': nrm(ks[24], (DEPTH, N_EXPERTS, 2 * D_FF), jnp.float32) * 0.02,
        'w_down': nrm(ks[25], (DEPTH, N_EXPERTS, D_FF, D), jnp.float32) * D_FF ** -0.5,
        'b_down': nrm(ks[26], (DEPTH, N_EXPERTS, D), jnp.float32) * 0.02,
    }


def reference(x_prompt, x_sample, cache_ckv, cache_krope, c, c_ctx, w_ada, b_ada,
              g_pre_mix, g_post_mix, g_pre_ffn, g_post_ffn, w_in, g_sgu, w_spatial, b_spatial,
              g_q, w_uq, g_kv, w_ukv, w_out, w_router, b_router, w_gate_up, b_gate_up,
              w_down, b_down):
    x_ctx = x_prompt
    x_lat = x_sample
    ckv_layers = []
    krope_layers = []
    for i in range(DEPTH):
        lw = dict(w_in=w_in[i], g_sgu=g_sgu[i], w_spatial=w_spatial[i], b_spatial=b_spatial[i],
                  g_q=g_q[i], w_uq=w_uq[i], g_kv=g_kv[i], w_ukv=w_ukv[i], w_out=w_out[i],
                  g_pre_mix=g_pre_mix[i], g_post_mix=g_post_mix[i],
                  g_pre_ffn=g_pre_ffn[i], g_post_ffn=g_post_ffn[i],
                  w_router=w_router[i], b_router=b_router[i],
                  w_gate_up=w_gate_up[i], b_gate_up=b_gate_up[i],
                  w_down=w_down[i], b_down=b_down[i])
        mod_ctx = modulation(c_ctx[None, :], w_ada[i], b_ada[i])
        mod_lat = modulation(c, w_ada[i], b_ada[i])
        x_ctx, ctx_kv = trunk_layer(x_ctx, mod_ctx, lw)
        ckv_layers.append(ctx_kv[0])
        krope_layers.append(ctx_kv[1])
        x_lat, _ = trunk_layer(x_lat, mod_lat, lw, cache_ckv[:, i], cache_krope[:, i])
    new_ckv = jnp.stack(ckv_layers, axis=1)
    new_krope = jnp.stack(krope_layers, axis=1)
    return (x_ctx, x_lat, new_ckv, new_krope)
```

```python
import functools
import math

import jax
import jax.numpy as jnp
import numpy as np
from jax import lax
from jax.experimental import pallas as pl
from jax.experimental.pallas import tpu as pltpu

F32 = jnp.float32
BF16 = jnp.bfloat16

D = 1024
N_CTX_B, CTX_T = 16, 256
N_LAT_B, LAT_T = 2, 1024
PAST = 512
N_CTX = N_CTX_B * CTX_T
N_LAT = N_LAT_B * LAT_T
N_TOK = N_CTX + N_LAT
DEPTH = 2
GRID_W = 64
EPS = 1e-6
A_HEADS, A_HEAD_DIM, A_WIDTH, CHUNK = 4, 64, 256, 128
F_GROUPS, F_GROUP_DIM, F_WIDTH = 4, 64, 256
HEADS, Q_LORA, KV_LORA, NOPE, ROPE, V_DIM = 4, 256, 128, 128, 64, 128
HEAD_PAD = 256
IN_PAD = 1280
N_EXPERTS, TOP_K, D_FF = 32, 4, 1024
SWIGLU_LIMIT, SWIGLU_ALPHA = 7.0, 1.702
ROPE_BASE = 10000.0

TB = 256
QB = 256
RB = 256
LANES = 128
N_ROWS = N_TOK * TOP_K + N_EXPERTS * RB
N_BLOCKS = N_ROWS // RB
NEG = -3.0e38
VMEM_LIMIT = 56 * 1024 * 1024


def _rms(x, g):
    return x * lax.rsqrt(jnp.mean(x * x, axis=-1, keepdims=True) + EPS) * g


def _split_dot(v, m):
    hi = v.astype(BF16)
    lo = (v - hi.astype(F32)).astype(BF16)
    return (jnp.dot(hi, m, preferred_element_type=F32)
            + jnp.dot(lo, m, preferred_element_type=F32))


def _dot_nt(a, b):
    return lax.dot_general(a, b, (((1,), (1,)), ((), ())), preferred_element_type=F32)


def _mod_kernel(ct_ref, w_ref, b_ref, o_ref):
    ct = ct_ref[...]
    s = ct * jax.nn.sigmoid(ct)
    w = w_ref[...]
    o_ref[...] = jnp.zeros(o_ref.shape, F32)
    for r in range(3):
        o_ref[r:r + 1, :] = jnp.sum(w * s[:, r:r + 1], axis=0, keepdims=True) + b_ref[...]


def _mod_call(cond_t, w_ada, b_ada):
    cb = 512
    return pl.pallas_call(
        _mod_kernel,
        grid=(DEPTH, 6 * D // cb),
        in_specs=[
            pl.BlockSpec((D, 8), lambda l, j: (0, 0)),
            pl.BlockSpec((None, D, cb), lambda l, j: (l, 0, j)),
            pl.BlockSpec((None, 1, cb), lambda l, j: (l, 0, j)),
        ],
        out_specs=pl.BlockSpec((None, 8, cb), lambda l, j: (l, 0, j)),
        out_shape=jax.ShapeDtypeStruct((DEPTH, 8, 6 * D), F32),
        compiler_params=pltpu.CompilerParams(dimension_semantics=("parallel", "parallel")),
        name="modulation",
    )(cond_t, w_ada, b_ada)


def _mod_index(i):
    first_lat = N_CTX // TB
    return jnp.where(i < first_lat, 0, 1 + (i - first_lat) // (LAT_T // TB))


def _swap_halves(x, lane):
    w = x.shape[-1]
    fwd = pltpu.roll(x, w - 16, 1)
    bwd = pltpu.roll(x, 16, 1)
    return jnp.where((lane & 31) < 16, fwd, bwd)


def _pre_kernel(x_ref, mod_ref, gpre_ref, win_ref, gsgu_ref, gmat_ref, gq_ref, wuq_ref,
                gkv_ref, wukv_ref, cos_ref, sin_ref,
                u_ref, vn_ref, zf_ref, q_ref, kv_ref, kr_ref, ckv_ref, zkr_ref):
    i = pl.program_id(0)
    is_lat = i >= N_CTX // TB
    x = x_ref[...]
    h = _rms(x, gpre_ref[...]) * (1.0 + mod_ref[1:2, :]) + mod_ref[0:1, :]
    z = jnp.dot(h.astype(BF16), win_ref[...], preferred_element_type=F32)

    ga = jax.nn.gelu(z[:, :2 * A_WIDTH])
    u_ref[...] = ga[:, :A_WIDTH]
    v = ga[:, A_WIDTH:]
    gmat = gmat_ref[...]
    dv = v - _split_dot(v, gmat)
    var = _split_dot(dv * dv, gmat)
    vn_ref[...] = (dv * lax.rsqrt(var + EPS) * gsgu_ref[...]).astype(BF16)

    zf_ref[...] = z[:, 512:768].astype(BF16)

    cos = jnp.where(is_lat, cos_ref[...], 1.0)
    sin = jnp.where(is_lat, sin_ref[...], 0.0)
    lane = lax.broadcasted_iota(jnp.int32, (TB, LANES), 1)

    qn = _rms(z[:, 768:1024], gq_ref[...])
    scale = (NOPE + ROPE) ** -0.5
    q = jnp.dot(qn.astype(BF16), wuq_ref[...], preferred_element_type=F32) * scale
    for hd in range(HEADS):
        base = hd * HEAD_PAD
        q_ref[:, base:base + NOPE] = q[:, base:base + NOPE].astype(BF16)
        qr = q[:, base + NOPE:base + HEAD_PAD]
        q_ref[:, base + NOPE:base + HEAD_PAD] = (qr * cos + _swap_halves(qr, lane) * sin).astype(BF16)

    ckv = _rms(z[:, 1024:1152], gkv_ref[...])
    ckv_ref[...] = ckv
    kv_ref[...] = jnp.dot(ckv.astype(BF16), wukv_ref[...], preferred_element_type=F32).astype(BF16)

    zkr = z[:, 1152:1280]
    zkr_ref[...] = zkr
    kr_ref[...] = (zkr * cos + _swap_halves(zkr, lane) * sin).astype(BF16)


def _pre_call(x, mods, g_pre, w_in, g_sgu, gmat, g_q, w_uq, g_kv, w_ukv, cos_t, sin_t):
    nb = N_TOK // TB
    first_lat = N_CTX // TB
    pos_blocks = LAT_T // TB

    def tok(width):
        return pl.BlockSpec((TB, width), lambda i: (i, 0))

    def full(shape):
        return pl.BlockSpec(shape, lambda i: (0,) * len(shape))

    def rope_map(i):
        return (jnp.where(i >= first_lat, (i - first_lat) % pos_blocks, 0), 0)

    out_shape = (
        jax.ShapeDtypeStruct((N_TOK, A_WIDTH), F32),
        jax.ShapeDtypeStruct((N_TOK, A_WIDTH), BF16),
        jax.ShapeDtypeStruct((N_TOK, F_WIDTH), BF16),
        jax.ShapeDtypeStruct((N_TOK, HEADS * HEAD_PAD), BF16),
        jax.ShapeDtypeStruct((N_TOK, HEADS * (NOPE + V_DIM)), BF16),
        jax.ShapeDtypeStruct((N_TOK, LANES), BF16),
        jax.ShapeDtypeStruct((N_TOK, KV_LORA), F32),
        jax.ShapeDtypeStruct((N_TOK, LANES), F32),
    )
    return pl.pallas_call(
        _pre_kernel,
        grid=(nb,),
        in_specs=[
            tok(D),
            pl.BlockSpec((None, 6, D), lambda i: (_mod_index(i), 0, 0)),
            full((1, D)), full((D, IN_PAD)), full((1, A_WIDTH)), full((A_WIDTH, A_WIDTH)),
            full((1, Q_LORA)), full((Q_LORA, HEADS * HEAD_PAD)),
            full((1, KV_LORA)), full((KV_LORA, HEADS * (NOPE + V_DIM))),
            pl.BlockSpec((TB, LANES), rope_map), pl.BlockSpec((TB, LANES), rope_map),
        ],
        out_specs=(tok(A_WIDTH), tok(A_WIDTH), tok(F_WIDTH), tok(HEADS * HEAD_PAD),
                   tok(HEADS * (NOPE + V_DIM)), tok(LANES), tok(KV_LORA), tok(LANES)),
        out_shape=out_shape,
        compiler_params=pltpu.CompilerParams(dimension_semantics=("parallel",),
                                             vmem_limit_bytes=VMEM_LIMIT),
        name="pre_mix",
    )(x, mods, g_pre, w_in, g_sgu, gmat, g_q, w_uq, g_kv, w_ukv, cos_t, sin_t)


def _mix_kernel(*refs, has_cache):
    if has_cache:
        (u_ref, vn_ref, zf_ref, q_ref, kv_ref, kr_ref, ws_ref, bs_ref, bdc_ref, bds_ref,
         ct_ref, st_ref, cckv_ref, ckr_ref, wukv_ref, o_ref) = refs
    else:
        (u_ref, vn_ref, zf_ref, q_ref, kv_ref, kr_ref, ws_ref, bs_ref, bdc_ref, bds_ref,
         ct_ref, st_ref, o_ref) = refs

    lane = lax.broadcasted_iota(jnp.int32, (CHUNK, A_WIDTH), 1)
    for c in range(QB // CHUNK):
        rows = slice(c * CHUNK, (c + 1) * CHUNK)
        vch = vn_ref[rows, :]
        s = bs_ref[...]
        for g in range(A_HEADS):
            sg = jnp.dot(ws_ref[g], vch, preferred_element_type=F32)
            in_head = (lane >= g * A_HEAD_DIM) & (lane < (g + 1) * A_HEAD_DIM)
            s = s + jnp.where(in_head, sg, 0.0)
        o_ref[rows, 0:A_WIDTH] = (u_ref[rows, :] * s).astype(BF16)

    zf = zf_ref[...]
    zc = jnp.dot(zf, bdc_ref[...], preferred_element_type=F32).astype(BF16)
    zs = jnp.dot(zf, bds_ref[...], preferred_element_type=F32).astype(BF16)
    yf = (jnp.dot(ct_ref[...], zc, preferred_element_type=F32)
          - jnp.dot(st_ref[...], zs, preferred_element_type=F32))
    o_ref[:, A_WIDTH:A_WIDTH + F_WIDTH] = yf.astype(BF16)

    kr = kr_ref[...]
    if has_cache:
        kvc = jnp.dot(cckv_ref[...].astype(BF16), wukv_ref[...],
                      preferred_element_type=F32).astype(BF16)
        krc = ckr_ref[...]
    for hd in range(HEADS):
        qh = q_ref[:, hd * HEAD_PAD:(hd + 1) * HEAD_PAD]
        kb = hd * (NOPE + V_DIM)
        kh = jnp.concatenate([kv_ref[:, kb:kb + NOPE], kr], axis=1)
        vh = kv_ref[:, kb + NOPE:kb + NOPE + V_DIM]
        s = _dot_nt(qh, kh)
        m = jnp.max(s, axis=-1, keepdims=True)
        if has_cache:
            khc = jnp.concatenate([kvc[:, kb:kb + NOPE], krc], axis=1)
            vhc = kvc[:, kb + NOPE:kb + NOPE + V_DIM]
            sc = _dot_nt(qh, khc)
            m = jnp.maximum(m, jnp.max(sc, axis=-1, keepdims=True))
        e = jnp.exp(s - m)
        den = jnp.sum(e, axis=-1, keepdims=True)
        o = jnp.dot(e.astype(BF16), vh, preferred_element_type=F32)
        if has_cache:
            ec = jnp.exp(sc - m)
            den = den + jnp.sum(ec, axis=-1, keepdims=True)
            o = o + jnp.dot(ec.astype(BF16), vhc, preferred_element_type=F32)
        ob = A_WIDTH + F_WIDTH + hd * V_DIM
        o_ref[:, ob:ob + V_DIM] = (o * (1.0 / den)).astype(BF16)


def _mix_call(pre, consts, seq_t, n_batch, tok_off, cache=None):
    u, vn, zf, q, kv, kr = pre
    ws, bs_full, bdc, bds, ct, st = consts
    nq = seq_t // QB
    qoff = tok_off // QB
    soff = tok_off // seq_t

    def qrow(width):
        return pl.BlockSpec((QB, width), lambda b, j: (qoff + b * nq + j, 0))

    def srow(width):
        return pl.BlockSpec((seq_t, width), lambda b, j: (soff + b, 0))

    def full(shape):
        return pl.BlockSpec(shape, lambda b, j: (0,) * len(shape))

    in_specs = [qrow(A_WIDTH), qrow(A_WIDTH), srow(F_WIDTH), qrow(HEADS * HEAD_PAD),
                srow(HEADS * (NOPE + V_DIM)), srow(LANES),
                full((A_HEADS, CHUNK, CHUNK)), full((CHUNK, A_WIDTH)),
                full((F_WIDTH, F_WIDTH)), full((F_WIDTH, F_WIDTH)),
                pl.BlockSpec((QB, seq_t), lambda b, j: (j, 0)),
                pl.BlockSpec((QB, seq_t), lambda b, j: (j, 0))]
    args = [u, vn, zf, q, kv, kr, ws, bs_full, bdc, bds, ct, st]
    if cache is not None:
        cckv, ckr, wukv = cache
        in_specs += [pl.BlockSpec((None, PAST, KV_LORA), lambda b, j: (b, 0, 0)),
                     pl.BlockSpec((None, PAST, LANES), lambda b, j: (b, 0, 0)),
                     full((KV_LORA, HEADS * (NOPE + V_DIM)))]
        args += [cckv, ckr, wukv]
    return pl.pallas_call(
        functools.partial(_mix_kernel, has_cache=cache is not None),
        grid=(n_batch, nq),
        in_specs=in_specs,
        out_specs=pl.BlockSpec((QB, D), lambda b, j: (b * nq + j, 0)),
        out_shape=jax.ShapeDtypeStruct((n_batch * seq_t, D), BF16),
        compiler_params=pltpu.CompilerParams(dimension_semantics=("parallel", "parallel"),
                                             vmem_limit_bytes=VMEM_LIMIT),
        name="mix_lat" if cache is not None else "mix_ctx",
    )(*args)


def _post_kernel(y_ref, x_ref, mod_ref, wout_ref, gpost_ref, gffn_ref, wr_ref, br_ref, tri_ref,
                 x1_ref, h2_ref, ri_ref, rw_ref, cnt_ref, carry_ref):
    i = pl.program_id(0)

    @pl.when(i == 0)
    def _():
        carry_ref[...] = jnp.zeros(carry_ref.shape, F32)

    y = jnp.dot(y_ref[...], wout_ref[...], preferred_element_type=F32)
    x1 = x_ref[...] + mod_ref[2:3, :] * _rms(y, gpost_ref[...])
    x1_ref[...] = x1
    h2 = _rms(x1, gffn_ref[...]) * (1.0 + mod_ref[4:5, :]) + mod_ref[3:4, :]
    h2_ref[...] = h2.astype(BF16)

    wr = wr_ref[...]
    wr_hi = wr.astype(BF16)
    wr_lo = (wr - wr_hi.astype(F32)).astype(BF16)
    h_hi = h2.astype(BF16)
    h_lo = (h2 - h_hi.astype(F32)).astype(BF16)
    logits = (jnp.dot(h_hi, wr_hi, preferred_element_type=F32)
              + jnp.dot(h_lo, wr_hi, preferred_element_type=F32)
              + jnp.dot(h_hi, wr_lo, preferred_element_type=F32)) + br_ref[...]

    lane = lax.broadcasted_iota(jnp.int32, (TB, LANES), 1)
    lane_f = lane.astype(F32)
    work = logits
    idx, val = [], []
    for _ in range(TOP_K):
        m = jnp.max(work, axis=-1, keepdims=True)
        ik = jnp.min(jnp.where(work == m, lane_f, float(LANES)), axis=-1, keepdims=True)
        idx.append(ik)
        val.append(m)
        work = jnp.where(lane_f == ik, NEG, work)
    ex = [jnp.exp(v - val[0]) for v in val]
    den = ex[0] + ex[1] + ex[2] + ex[3]

    onehot = jnp.zeros((TB, LANES), F32)
    for k in range(TOP_K):
        onehot = onehot + jnp.where(lane_f == idx[k], 1.0, 0.0)
    before = jnp.dot(tri_ref[...], onehot.astype(BF16), preferred_element_type=F32) + carry_ref[0:1, :]
    ri = jnp.zeros((TB, LANES), F32)
    rw = jnp.zeros((TB, LANES), F32)
    for k in range(TOP_K):
        rank_k = jnp.sum(jnp.where(lane_f == idx[k], before, 0.0), axis=-1, keepdims=True)
        ri = ri + jnp.where(lane == k, idx[k], 0.0) + jnp.where(lane == TOP_K + k, rank_k, 0.0)
        rw = rw + jnp.where(lane == k, ex[k] / den, 0.0)
    ri_ref[...] = ri.astype(jnp.int32)
    rw_ref[...] = rw
    total = carry_ref[0:1, :] + jnp.sum(onehot, axis=0, keepdims=True)
    carry_ref[...] = jnp.broadcast_to(total, carry_ref.shape)
    cnt_ref[...] = jnp.broadcast_to(total, cnt_ref.shape).astype(jnp.int32)


def _post_call(ymix, x, mods, w_out, g_post, g_ffn, w_r, b_r, tri):
    nb = N_TOK // TB

    def tok(width):
        return pl.BlockSpec((TB, width), lambda i: (i, 0))

    def full(shape):
        return pl.BlockSpec(shape, lambda i: (0,) * len(shape))

    return pl.pallas_call(
        _post_kernel,
        grid=(nb,),
        in_specs=[tok(D), tok(D), pl.BlockSpec((None, 6, D), lambda i: (_mod_index(i), 0, 0)),
                  full((D, D)), full((1, D)), full((1, D)), full((D, LANES)), full((1, LANES)),
                  full((TB, TB))],
        out_specs=(tok(D), tok(D), tok(LANES), tok(LANES), full((8, LANES))),
        out_shape=(jax.ShapeDtypeStruct((N_TOK, D), F32),
                   jax.ShapeDtypeStruct((N_TOK, D), BF16),
                   jax.ShapeDtypeStruct((N_TOK, LANES), jnp.int32),
                   jax.ShapeDtypeStruct((N_TOK, LANES), F32),
                   jax.ShapeDtypeStruct((8, LANES), jnp.int32)),
        scratch_shapes=[pltpu.VMEM((8, LANES), F32)],
        compiler_params=pltpu.CompilerParams(dimension_semantics=("arbitrary",),
                                             vmem_limit_bytes=VMEM_LIMIT),
        name="post_mix_router",
    )(ymix, x, mods, w_out, g_post, g_ffn, w_r, b_r, tri)


def _moe_kernel(be_ref, nu_ref, xs_ref, wgu_ref, bgu_ref, wdn_ref, bdn_ref, o_ref, wgu_s, wdn_s):
    b = pl.program_id(0)
    prev = be_ref[jnp.maximum(b - 1, 0)]
    changed = jnp.logical_or(b == 0, be_ref[b] != prev)
    used = b < nu_ref[0]

    @pl.when(changed)
    def _():
        step = 128

        def body(c, carry):
            r = pl.multiple_of(c * step, step)
            wgu_s[pl.ds(r, step), :] = wgu_ref[pl.ds(r, step), :].astype(BF16)
            wdn_s[pl.ds(r, step), :] = wdn_ref[pl.ds(r, step), :].astype(BF16)
            return carry

        lax.fori_loop(0, D // step, body, 0)

    @pl.when(used)
    def _():
        gu = jnp.dot(xs_ref[...], wgu_s[...], preferred_element_type=F32) + bgu_ref[...]
        g = jnp.minimum(gu[:, :D_FF], SWIGLU_LIMIT)
        l = jnp.clip(gu[:, D_FF:], -SWIGLU_LIMIT, SWIGLU_LIMIT)
        a = g * jax.nn.sigmoid(SWIGLU_ALPHA * g) * (l + 1.0)
        o_ref[...] = jnp.dot(a.astype(BF16), wdn_s[...], preferred_element_type=F32) + bdn_ref[...]

    @pl.when(jnp.logical_not(used))
    def _():
        o_ref[...] = jnp.zeros(o_ref.shape, F32)


def _moe_call(layer, block_e, n_used, xs, w_gu, b_gu, w_dn, b_dn):
    grid_spec = pltpu.PrefetchScalarGridSpec(
        num_scalar_prefetch=2,
        grid=(N_BLOCKS,),
        in_specs=[
            pl.BlockSpec((RB, D), lambda b, be, nu: (b, 0)),
            pl.BlockSpec((None, None, D, 2 * D_FF), lambda b, be, nu: (layer, be[b], 0, 0)),
            pl.BlockSpec((None, None, 1, 2 * D_FF), lambda b, be, nu: (layer, be[b], 0, 0)),
            pl.BlockSpec((None, None, D_FF, D), lambda b, be, nu: (layer, be[b], 0, 0)),
            pl.BlockSpec((None, None, 1, D), lambda b, be, nu: (layer, be[b], 0, 0)),
        ],
        out_specs=pl.BlockSpec((RB, D), lambda b, be, nu: (b, 0)),
        scratch_shapes=[pltpu.VMEM((D, 2 * D_FF), BF16), pltpu.VMEM((D_FF, D), BF16)],
    )
    return pl.pallas_call(
        _moe_kernel,
        grid_spec=grid_spec,
        out_shape=jax.ShapeDtypeStruct((N_ROWS, D), F32),
        compiler_params=pltpu.CompilerParams(dimension_semantics=("arbitrary",),
                                             vmem_limit_bytes=VMEM_LIMIT),
        name="moe_experts",
    )(block_e, n_used, xs, w_gu, b_gu, w_dn, b_dn)


def _final_kernel(x1_ref, ys_ref, rw_ref, mod_ref, g_ref, o_ref):
    rw = rw_ref[...]
    y = ys_ref[0] * rw[:, 0:1]
    for k in range(1, TOP_K):
        y = y + ys_ref[k] * rw[:, k:k + 1]
    o_ref[...] = x1_ref[...] + mod_ref[5:6, :] * _rms(y, g_ref[...])


def _final_call(x1, ysg, rw, mods, g_post_ffn):
    nb = N_TOK // TB
    return pl.pallas_call(
        _final_kernel,
        grid=(nb,),
        in_specs=[pl.BlockSpec((TB, D), lambda i: (i, 0)),
                  pl.BlockSpec((TOP_K, TB, D), lambda i: (0, i, 0)),
                  pl.BlockSpec((TB, LANES), lambda i: (i, 0)),
                  pl.BlockSpec((None, 6, D), lambda i: (_mod_index(i), 0, 0)),
                  pl.BlockSpec((1, D), lambda i: (0, 0))],
        out_specs=pl.BlockSpec((TB, D), lambda i: (i, 0)),
        out_shape=jax.ShapeDtypeStruct((N_TOK, D), F32),
        compiler_params=pltpu.CompilerParams(dimension_semantics=("parallel",),
                                             vmem_limit_bytes=VMEM_LIMIT),
        name="combine_residual",
    )(x1, ysg, rw, mods, g_post_ffn)


def _dft_tables(t):
    j = np.arange(t, dtype=np.int64)
    ang = 2.0 * np.pi * ((j[:, None] * j[None, :]) % t) / t
    return (np.cos(ang) / math.sqrt(t)).astype(np.float32), (np.sin(ang) / math.sqrt(t)).astype(np.float32)


def _channel_dft():
    c = np.arange(F_GROUP_DIM, dtype=np.int64)
    ang = 2.0 * np.pi * ((c[:, None] * c[None, :]) % F_GROUP_DIM) / F_GROUP_DIM
    eye = np.eye(F_GROUPS)
    bdc = np.kron(eye, np.cos(ang)) / math.sqrt(F_GROUP_DIM)
    bds = np.kron(eye, np.sin(ang)) / math.sqrt(F_GROUP_DIM)
    return bdc.astype(np.float32), bds.astype(np.float32)


def _rope_tables():
    pos = np.arange(LAT_T)
    n = ROPE // 4
    inv_freq = np.power(np.float32(ROPE_BASE), -np.arange(n, dtype=np.float32) / np.float32(n))
    ang_r = (pos // GRID_W).astype(np.float32)[:, None] * inv_freq
    ang_c = (pos % GRID_W).astype(np.float32)[:, None] * inv_freq
    cos = np.concatenate([np.cos(ang_r), np.cos(ang_r), np.cos(ang_c), np.cos(ang_c),
                          np.ones((LAT_T, LANES - ROPE))], axis=1)
    sin = np.concatenate([-np.sin(ang_r), np.sin(ang_r), -np.sin(ang_c), np.sin(ang_c),
                          np.zeros((LAT_T, LANES - ROPE))], axis=1)
    return cos.astype(np.float32), sin.astype(np.float32)


def kernel(x_prompt, x_sample, cache_ckv, cache_krope, c, c_ctx, w_ada, b_ada, g_pre_mix, g_post_mix, g_pre_ffn, g_post_ffn, w_in, g_sgu, w_spatial, b_spatial, g_q, w_uq, g_kv, w_ukv, w_out, w_router, b_router, w_gate_up, b_gate_up, w_down, b_down):
    x = jnp.concatenate([x_prompt.reshape(N_CTX, D), x_sample.reshape(N_LAT, D)], axis=0)

    cond = jnp.concatenate([c_ctx[None, :], c, jnp.zeros((5, D), F32)], axis=0)
    mods = _mod_call(cond.T, w_ada, b_ada.reshape(DEPTH, 1, 6 * D))
    mods = mods[:, :3].reshape(DEPTH, 3, 6, D)

    gmat = jnp.asarray(np.kron(np.eye(A_HEADS), np.full((A_HEAD_DIM, A_HEAD_DIM), 1.0 / A_HEAD_DIM)),
                       dtype=BF16)
    bdc_np, bds_np = _channel_dft()
    bdc, bds = jnp.asarray(bdc_np).astype(BF16), jnp.asarray(bds_np).astype(BF16)
    dft = {}
    for t in (CTX_T, LAT_T):
        ct_np, st_np = _dft_tables(t)
        dft[t] = (jnp.asarray(ct_np).astype(BF16), jnp.asarray(st_np).astype(BF16))
    cos_np, sin_np = _rope_tables()
    cos_t, sin_t = jnp.asarray(cos_np), jnp.asarray(sin_np)
    tri = jnp.asarray(np.tril(np.ones((TB, TB), np.float32), k=-1)).astype(BF16)
    tok_ids = jnp.repeat(jnp.arange(N_TOK, dtype=jnp.int32), TOP_K)

    b_gu = b_gate_up.reshape(DEPTH, N_EXPERTS, 1, 2 * D_FF)
    b_dn = b_down.reshape(DEPTH, N_EXPERTS, 1, D)

    ckv_layers, krope_layers = [], []
    for i in range(DEPTH):
        w_in_p = jnp.pad(w_in[i], ((0, 0), (0, IN_PAD - w_in.shape[-1]))).astype(BF16)
        w_uq_p = jnp.pad(w_uq[i].reshape(Q_LORA, HEADS, NOPE + ROPE),
                         ((0, 0), (0, 0), (0, HEAD_PAD - NOPE - ROPE))).reshape(Q_LORA, HEADS * HEAD_PAD)
        w_uq_p = w_uq_p.astype(BF16)
        w_ukv_b = w_ukv[i].astype(BF16)
        pre = _pre_call(x, mods[i], g_pre_mix[i][None, :], w_in_p, g_sgu[i][None, :], gmat,
                        g_q[i][None, :], w_uq_p, g_kv[i][None, :], w_ukv_b, cos_t, sin_t)
        u, vn, zf, q, kv, kr, ckv, zkr = pre
        ckv_layers.append(ckv[:N_CTX].reshape(N_CTX_B, CTX_T, KV_LORA))
        krope_layers.append(zkr[:N_CTX, :ROPE].reshape(N_CTX_B, CTX_T, ROPE))

        ws = w_spatial[i].astype(BF16)
        bs_full = jnp.repeat(b_spatial[i].T, A_HEAD_DIM, axis=1)
        mix_in = (u, vn, zf, q, kv, kr)
        y_ctx = _mix_call(mix_in, (ws, bs_full, bdc, bds) + dft[CTX_T], CTX_T, N_CTX_B, 0)
        ckr_p = jnp.pad(cache_krope[:, i], ((0, 0), (0, 0), (0, LANES - ROPE))).astype(BF16)
        y_lat = _mix_call(mix_in, (ws, bs_full, bdc, bds) + dft[LAT_T], LAT_T, N_LAT_B, N_CTX,
                          cache=(cache_ckv[:, i], ckr_p, w_ukv_b))
        ymix = jnp.concatenate([y_ctx, y_lat], axis=0)

        w_r = jnp.pad(w_router[i], ((0, 0), (0, LANES - N_EXPERTS)))
        b_r = jnp.pad(b_router[i], (0, LANES - N_EXPERTS), constant_values=NEG)[None, :]
        x1, h2, ri, rw, cnt = _post_call(ymix, x, mods[i], w_out[i].astype(BF16),
                                         g_post_mix[i][None, :], g_pre_ffn[i][None, :], w_r, b_r, tri)

        counts = cnt[0, :N_EXPERTS]
        padded = (counts + RB - 1) // RB * RB
        pad_end = jnp.cumsum(padded)
        pad_start = pad_end - padded
        top_idx = ri[:, :TOP_K]
        dest = pad_start[top_idx] + ri[:, TOP_K:2 * TOP_K]
        row_tok = jnp.zeros((N_ROWS,), jnp.int32).at[dest.reshape(-1)].set(tok_ids)
        n_used = (pad_end[-1] // RB).astype(jnp.int32)
        blk = jnp.arange(N_BLOCKS, dtype=jnp.int32)
        blk_row = jnp.minimum(blk, n_used - 1) * RB
        block_e = jnp.sum((pad_end[None, :] <= blk_row[:, None]).astype(jnp.int32), axis=1)
        block_e = jnp.minimum(block_e, N_EXPERTS - 1)
        xs = h2[row_tok]
        ys = _moe_call(i, block_e, n_used[None], xs, w_gate_up, b_gu, w_down, b_dn)
        ysg = ys[dest.T.reshape(-1)].reshape(TOP_K, N_TOK, D)
        x = _final_call(x1, ysg, rw, mods[i], g_post_ffn[i][None, :])

    y_prompt = x[:N_CTX].reshape(N_CTX_B, CTX_T, D)
    y_sample = x[N_CTX:].reshape(N_LAT_B, LAT_T, D)
    return (y_prompt, y_sample, jnp.stack(ckv_layers, axis=1), jnp.stack(krope_layers, axis=1))
```

```python
import functools
import math

import jax
import jax.numpy as jnp
import numpy as np
from jax import lax
from jax.experimental import pallas as pl
from jax.experimental.pallas import tpu as pltpu

F32 = jnp.float32
BF16 = jnp.bfloat16

D = 1024
N_CTX_B, CTX_T = 16, 256
N_LAT_B, LAT_T = 2, 1024
PAST = 512
N_CTX = N_CTX_B * CTX_T
N_LAT = N_LAT_B * LAT_T
N_TOK = N_CTX + N_LAT
DEPTH = 2
GRID_W = 64
EPS = 1e-6
A_HEADS, A_HEAD_DIM, A_WIDTH, CHUNK = 4, 64, 256, 128
F_GROUPS, F_GROUP_DIM, F_WIDTH = 4, 64, 256
HEADS, Q_LORA, KV_LORA, NOPE, ROPE, V_DIM = 4, 256, 128, 128, 64, 128
HEAD_PAD = 256
IN_PAD = 1280
N_EXPERTS, TOP_K, D_FF = 32, 4, 1024
SWIGLU_LIMIT, SWIGLU_ALPHA = 7.0, 1.702
ROPE_BASE = 10000.0

TB = 256
QB = 256
RB = 256
LANES = 128
N_BLOCKS = N_TOK * TOP_K // RB + N_EXPERTS
N_ROWS = N_BLOCKS * RB
NEG = -3.0e38
VMEM_LIMIT = 56 * 1024 * 1024


def _rms(x, g):
    return x * lax.rsqrt(jnp.mean(x * x, axis=-1, keepdims=True) + EPS) * g


def _split_dot(v, m):
    hi = v.astype(BF16)
    lo = (v - hi.astype(F32)).astype(BF16)
    return (jnp.dot(hi, m, preferred_element_type=F32)
            + jnp.dot(lo, m, preferred_element_type=F32))


def _dot_nt(a, b):
    return lax.dot_general(a, b, (((1,), (1,)), ((), ())), preferred_element_type=F32)


def _mod_kernel(ct_ref, w_ref, b_ref, o_ref):
    ct = ct_ref[...]
    s = ct * jax.nn.sigmoid(ct)
    w = w_ref[...]
    o_ref[...] = jnp.zeros(o_ref.shape, F32)
    for r in range(3):
        o_ref[r:r + 1, :] = jnp.sum(w * s[:, r:r + 1], axis=0, keepdims=True) + b_ref[...]


def _mod_call(cond_t, w_ada, b_ada):
    cb = 512
    return pl.pallas_call(
        _mod_kernel,
        grid=(DEPTH, 6 * D // cb),
        in_specs=[
            pl.BlockSpec((D, 8), lambda l, j: (0, 0)),
            pl.BlockSpec((None, D, cb), lambda l, j: (l, 0, j)),
            pl.BlockSpec((None, 1, cb), lambda l, j: (l, 0, j)),
        ],
        out_specs=pl.BlockSpec((None, 8, cb), lambda l, j: (l, 0, j)),
        out_shape=jax.ShapeDtypeStruct((DEPTH, 8, 6 * D), F32),
        compiler_params=pltpu.CompilerParams(dimension_semantics=("parallel", "parallel")),
        name="modulation",
    )(cond_t, w_ada, b_ada)


def _mod_index(i):
    first_lat = N_CTX // TB
    return jnp.where(i < first_lat, 0, 1 + (i - first_lat) // (LAT_T // TB))


def _swap_halves(x, lane):
    w = x.shape[-1]
    fwd = pltpu.roll(x, w - 16, 1)
    bwd = pltpu.roll(x, 16, 1)
    return jnp.where((lane & 31) < 16, fwd, bwd)


def _pre_kernel(x_ref, mod_ref, gpre_ref, win_ref, gsgu_ref, gmat_ref, gq_ref, wuq_ref,
                gkv_ref, wukv_ref, cos_ref, sin_ref,
                u_ref, vn_ref, zf_ref, q_ref, kv_ref, kr_ref, ckv_ref, zkr_ref):
    i = pl.program_id(0)
    is_lat = i >= N_CTX // TB
    x = x_ref[...]
    h = _rms(x, gpre_ref[...]) * (1.0 + mod_ref[1:2, :]) + mod_ref[0:1, :]
    z = jnp.dot(h.astype(BF16), win_ref[...], preferred_element_type=F32)

    ga = jax.nn.gelu(z[:, :2 * A_WIDTH])
    u_ref[...] = ga[:, :A_WIDTH]
    v = ga[:, A_WIDTH:]
    gmat = gmat_ref[...]
    dv = v - _split_dot(v, gmat)
    var = _split_dot(dv * dv, gmat)
    vn_ref[...] = (dv * lax.rsqrt(var + EPS) * gsgu_ref[...]).astype(BF16)

    zf_ref[...] = z[:, 512:768].astype(BF16)

    cos = jnp.where(is_lat, cos_ref[...], 1.0)
    sin = jnp.where(is_lat, sin_ref[...], 0.0)
    lane = lax.broadcasted_iota(jnp.int32, (TB, LANES), 1)

    qn = _rms(z[:, 768:1024], gq_ref[...])
    scale = (NOPE + ROPE) ** -0.5
    q = jnp.dot(qn.astype(BF16), wuq_ref[...], preferred_element_type=F32) * scale
    for hd in range(HEADS):
        base = hd * HEAD_PAD
        q_ref[:, base:base + NOPE] = q[:, base:base + NOPE].astype(BF16)
        qr = q[:, base + NOPE:base + HEAD_PAD]
        q_ref[:, base + NOPE:base + HEAD_PAD] = (qr * cos + _swap_halves(qr, lane) * sin).astype(BF16)

    ckv = _rms(z[:, 1024:1152], gkv_ref[...])
    ckv_ref[...] = ckv
    kv_ref[...] = jnp.dot(ckv.astype(BF16), wukv_ref[...], preferred_element_type=F32).astype(BF16)

    zkr = z[:, 1152:1280]
    zkr_ref[...] = zkr
    kr_ref[...] = (zkr * cos + _swap_halves(zkr, lane) * sin).astype(BF16)


def _pre_call(x, mods, g_pre, w_in, g_sgu, gmat, g_q, w_uq, g_kv, w_ukv, cos_t, sin_t):
    nb = N_TOK // TB
    first_lat = N_CTX // TB
    pos_blocks = LAT_T // TB

    def tok(width):
        return pl.BlockSpec((TB, width), lambda i: (i, 0))

    def full(shape):
        return pl.BlockSpec(shape, lambda i: (0,) * len(shape))

    def rope_map(i):
        return (jnp.where(i >= first_lat, (i - first_lat) % pos_blocks, 0), 0)

    out_shape = (
        jax.ShapeDtypeStruct((N_TOK, A_WIDTH), F32),
        jax.ShapeDtypeStruct((N_TOK, A_WIDTH), BF16),
        jax.ShapeDtypeStruct((N_TOK, F_WIDTH), BF16),
        jax.ShapeDtypeStruct((N_TOK, HEADS * HEAD_PAD), BF16),
        jax.ShapeDtypeStruct((N_TOK, HEADS * (NOPE + V_DIM)), BF16),
        jax.ShapeDtypeStruct((N_TOK, LANES), BF16),
        jax.ShapeDtypeStruct((N_TOK, KV_LORA), F32),
        jax.ShapeDtypeStruct((N_TOK, LANES), F32),
    )
    return pl.pallas_call(
        _pre_kernel,
        grid=(nb,),
        in_specs=[
            tok(D),
            pl.BlockSpec((None, 6, D), lambda i: (_mod_index(i), 0, 0)),
            full((1, D)), full((D, IN_PAD)), full((1, A_WIDTH)), full((A_WIDTH, A_WIDTH)),
            full((1, Q_LORA)), full((Q_LORA, HEADS * HEAD_PAD)),
            full((1, KV_LORA)), full((KV_LORA, HEADS * (NOPE + V_DIM))),
            pl.BlockSpec((TB, LANES), rope_map), pl.BlockSpec((TB, LANES), rope_map),
        ],
        out_specs=(tok(A_WIDTH), tok(A_WIDTH), tok(F_WIDTH), tok(HEADS * HEAD_PAD),
                   tok(HEADS * (NOPE + V_DIM)), tok(LANES), tok(KV_LORA), tok(LANES)),
        out_shape=out_shape,
        compiler_params=pltpu.CompilerParams(dimension_semantics=("parallel",),
                                             vmem_limit_bytes=VMEM_LIMIT),
        name="pre_mix",
    )(x, mods, g_pre, w_in, g_sgu, gmat, g_q, w_uq, g_kv, w_ukv, cos_t, sin_t)


def _mix_kernel(*refs, has_cache):
    if has_cache:
        (u_ref, vn_ref, zf_ref, q_ref, kv_ref, kr_ref, ws_ref, bs_ref, bdc_ref, bds_ref,
         ct_ref, st_ref, cckv_ref, ckr_ref, wukv_ref, o_ref) = refs
    else:
        (u_ref, vn_ref, zf_ref, q_ref, kv_ref, kr_ref, ws_ref, bs_ref, bdc_ref, bds_ref,
         ct_ref, st_ref, o_ref) = refs

    lane = lax.broadcasted_iota(jnp.int32, (CHUNK, A_WIDTH), 1)
    for c in range(QB // CHUNK):
        rows = slice(c * CHUNK, (c + 1) * CHUNK)
        vch = vn_ref[rows, :]
        s = bs_ref[...]
        for g in range(A_HEADS):
            sg = jnp.dot(ws_ref[g], vch, preferred_element_type=F32)
            in_head = (lane >= g * A_HEAD_DIM) & (lane < (g + 1) * A_HEAD_DIM)
            s = s + jnp.where(in_head, sg, 0.0)
        o_ref[rows, 0:A_WIDTH] = (u_ref[rows, :] * s).astype(BF16)

    zf = zf_ref[...]
    zc = jnp.dot(zf, bdc_ref[...], preferred_element_type=F32).astype(BF16)
    zs = jnp.dot(zf, bds_ref[...], preferred_element_type=F32).astype(BF16)
    yf = (jnp.dot(ct_ref[...], zc, preferred_element_type=F32)
          - jnp.dot(st_ref[...], zs, preferred_element_type=F32))
    o_ref[:, A_WIDTH:A_WIDTH + F_WIDTH] = yf.astype(BF16)

    kr = kr_ref[...]
    if has_cache:
        kvc = jnp.dot(cckv_ref[...].astype(BF16), wukv_ref[...],
                      preferred_element_type=F32).astype(BF16)
        krc = ckr_ref[...]
    for hd in range(HEADS):
        qh = q_ref[:, hd * HEAD_PAD:(hd + 1) * HEAD_PAD]
        kb = hd * (NOPE + V_DIM)
        kh = jnp.concatenate([kv_ref[:, kb:kb + NOPE], kr], axis=1)
        vh = kv_ref[:, kb + NOPE:kb + NOPE + V_DIM]
        s = _dot_nt(qh, kh)
        m = jnp.max(s, axis=-1, keepdims=True)
        if has_cache:
            khc = jnp.concatenate([kvc[:, kb:kb + NOPE], krc], axis=1)
            vhc = kvc[:, kb + NOPE:kb + NOPE + V_DIM]
            sc = _dot_nt(qh, khc)
            m = jnp.maximum(m, jnp.max(sc, axis=-1, keepdims=True))
        e = jnp.exp(s - m)
        den = jnp.sum(e, axis=-1, keepdims=True)
        o = jnp.dot(e.astype(BF16), vh, preferred_element_type=F32)
        if has_cache:
            ec = jnp.exp(sc - m)
            den = den + jnp.sum(ec, axis=-1, keepdims=True)
            o = o + jnp.dot(ec.astype(BF16), vhc, preferred_element_type=F32)
        ob = A_WIDTH + F_WIDTH + hd * V_DIM
        o_ref[:, ob:ob + V_DIM] = (o * (1.0 / den)).astype(BF16)


def _mix_call(pre, consts, seq_t, n_batch, tok_off, cache=None):
    u, vn, zf, q, kv, kr = pre
    ws, bs_full, bdc, bds, ct, st = consts
    nq = seq_t // QB
    qoff = tok_off // QB
    soff = tok_off // seq_t

    def qrow(width):
        return pl.BlockSpec((QB, width), lambda b, j: (qoff + b * nq + j, 0))

    def srow(width):
        return pl.BlockSpec((seq_t, width), lambda b, j: (soff + b, 0))

    def full(shape):
        return pl.BlockSpec(shape, lambda b, j: (0,) * len(shape))

    in_specs = [qrow(A_WIDTH), qrow(A_WIDTH), srow(F_WIDTH), qrow(HEADS * HEAD_PAD),
                srow(HEADS * (NOPE + V_DIM)), srow(LANES),
                full((A_HEADS, CHUNK, CHUNK)), full((CHUNK, A_WIDTH)),
                full((F_WIDTH, F_WIDTH)), full((F_WIDTH, F_WIDTH)),
                pl.BlockSpec((QB, seq_t), lambda b, j: (j, 0)),
                pl.BlockSpec((QB, seq_t), lambda b, j: (j, 0))]
    args = [u, vn, zf, q, kv, kr, ws, bs_full, bdc, bds, ct, st]
    if cache is not None:
        cckv, ckr, wukv = cache
        in_specs += [pl.BlockSpec((None, PAST, KV_LORA), lambda b, j: (b, 0, 0)),
                     pl.BlockSpec((None, PAST, LANES), lambda b, j: (b, 0, 0)),
                     full((KV_LORA, HEADS * (NOPE + V_DIM)))]
        args += [cckv, ckr, wukv]
    return pl.pallas_call(
        functools.partial(_mix_kernel, has_cache=cache is not None),
        grid=(n_batch, nq),
        in_specs=in_specs,
        out_specs=pl.BlockSpec((QB, D), lambda b, j: (b * nq + j, 0)),
        out_shape=jax.ShapeDtypeStruct((n_batch * seq_t, D), BF16),
        compiler_params=pltpu.CompilerParams(dimension_semantics=("parallel", "parallel"),
                                             vmem_limit_bytes=VMEM_LIMIT),
        name="mix_lat" if cache is not None else "mix_ctx",
    )(*args)


def _post_kernel(y_ref, x_ref, mod_ref, wout_ref, gpost_ref, gffn_ref, wr_ref, br_ref, tri_ref,
                 x1_ref, h2_ref, ri_ref, rw_ref, cnt_ref, carry_ref):
    i = pl.program_id(0)

    @pl.when(i == 0)
    def _():
        carry_ref[...] = jnp.zeros(carry_ref.shape, F32)

    y = jnp.dot(y_ref[...], wout_ref[...], preferred_element_type=F32)
    x1 = x_ref[...] + mod_ref[2:3, :] * _rms(y, gpost_ref[...])
    x1_ref[...] = x1
    h2 = _rms(x1, gffn_ref[...]) * (1.0 + mod_ref[4:5, :]) + mod_ref[3:4, :]
    h2_ref[...] = h2

    wr = wr_ref[...]
    wr_hi = wr.astype(BF16)
    wr_lo = (wr - wr_hi.astype(F32)).astype(BF16)
    h_hi = h2.astype(BF16)
    h_lo = (h2 - h_hi.astype(F32)).astype(BF16)
    logits = (jnp.dot(h_hi, wr_hi, preferred_element_type=F32)
              + jnp.dot(h_lo, wr_hi, preferred_element_type=F32)
              + jnp.dot(h_hi, wr_lo, preferred_element_type=F32)) + br_ref[...]

    lane = lax.broadcasted_iota(jnp.int32, (TB, LANES), 1)
    lane_f = lane.astype(F32)
    work = logits
    idx, val = [], []
    for _ in range(TOP_K):
        m = jnp.max(work, axis=-1, keepdims=True)
        ik = jnp.min(jnp.where(work == m, lane_f, float(LANES)), axis=-1, keepdims=True)
        idx.append(ik)
        val.append(m)
        work = jnp.where(lane_f == ik, NEG, work)
    ex = [jnp.exp(v - val[0]) for v in val]
    den = ex[0] + ex[1] + ex[2] + ex[3]

    onehot = jnp.zeros((TB, LANES), F32)
    for k in range(TOP_K):
        onehot = onehot + jnp.where(lane_f == idx[k], 1.0, 0.0)
    before = jnp.dot(tri_ref[...], onehot.astype(BF16), preferred_element_type=F32) + carry_ref[0:1, :]
    ri = jnp.zeros((TB, LANES), F32)
    rw = jnp.zeros((TB, LANES), F32)
    for k in range(TOP_K):
        rank_k = jnp.sum(jnp.where(lane_f == idx[k], before, 0.0), axis=-1, keepdims=True)
        ri = ri + jnp.where(lane == k, idx[k], 0.0) + jnp.where(lane == TOP_K + k, rank_k, 0.0)
        rw = rw + jnp.where(lane == k, ex[k] / den, 0.0)
    ri_ref[...] = ri.astype(jnp.int32)
    rw_ref[...] = rw
    total = carry_ref[0:1, :] + jnp.sum(onehot, axis=0, keepdims=True)
    carry_ref[...] = jnp.broadcast_to(total, carry_ref.shape)
    cnt_ref[...] = jnp.broadcast_to(total, cnt_ref.shape).astype(jnp.int32)


def _post_call(ymix, x, mods, w_out, g_post, g_ffn, w_r, b_r, tri):
    nb = N_TOK // TB

    def tok(width):
        return pl.BlockSpec((TB, width), lambda i: (i, 0))

    def full(shape):
        return pl.BlockSpec(shape, lambda i: (0,) * len(shape))

    return pl.pallas_call(
        _post_kernel,
        grid=(nb,),
        in_specs=[tok(D), tok(D), pl.BlockSpec((None, 6, D), lambda i: (_mod_index(i), 0, 0)),
                  full((D, D)), full((1, D)), full((1, D)), full((D, LANES)), full((1, LANES)),
                  full((TB, TB))],
        out_specs=(tok(D), tok(D), tok(LANES), tok(LANES), full((8, LANES))),
        out_shape=(jax.ShapeDtypeStruct((N_TOK, D), F32),
                   jax.ShapeDtypeStruct((N_TOK, D), F32),
                   jax.ShapeDtypeStruct((N_TOK, LANES), jnp.int32),
                   jax.ShapeDtypeStruct((N_TOK, LANES), F32),
                   jax.ShapeDtypeStruct((8, LANES), jnp.int32)),
        scratch_shapes=[pltpu.VMEM((8, LANES), F32)],
        compiler_params=pltpu.CompilerParams(dimension_semantics=("arbitrary",),
                                             vmem_limit_bytes=VMEM_LIMIT),
        name="post_mix_router",
    )(ymix, x, mods, w_out, g_post, g_ffn, w_r, b_r, tri)


def _dispatch_kernel(zrow_ref, nu_ref, dest_ref, h2_ref, xs_ref, zero_ref, sem, zsem):
    i = pl.program_id(0)

    def zero_fill(row):
        return pltpu.make_async_copy(zero_ref, xs_ref.at[pl.ds(pl.multiple_of(row, 8), RB)], zsem)

    @pl.when(i == 0)
    def _():
        zero_ref[...] = jnp.zeros(zero_ref.shape, F32)
        for e in range(N_EXPERTS):
            zero_fill(zrow_ref[e]).start()
        for e in range(N_EXPERTS):
            zero_fill(zrow_ref[e]).wait()

        def tail_start(b, carry):
            zero_fill(b * RB).start()
            return carry

        def tail_wait(b, carry):
            zero_fill(b * RB).wait()
            return carry

        lax.fori_loop(nu_ref[0], N_BLOCKS, tail_start, 0)
        lax.fori_loop(nu_ref[0], N_BLOCKS, tail_wait, 0)

    def body(t, carry):
        for k in range(TOP_K):
            pltpu.make_async_copy(h2_ref.at[pl.ds(t, 1)], xs_ref.at[pl.ds(dest_ref[k, t], 1)],
                                  sem).start(priority=k % 2)
        return carry

    lax.fori_loop(0, TB, body, 0, unroll=8)
    for k in range(TOP_K):
        pltpu.make_async_copy(h2_ref, xs_ref.at[pl.ds(0, TB)], sem).wait()


def _dispatch_call(zrow, n_used, dest_t, h2):
    nb = N_TOK // TB
    grid_spec = pltpu.PrefetchScalarGridSpec(
        num_scalar_prefetch=2,
        grid=(nb,),
        in_specs=[
            pl.BlockSpec((None, TOP_K, TB), lambda i, z, n: (i, 0, 0), memory_space=pltpu.SMEM),
            pl.BlockSpec((TB, D), lambda i, z, n: (i, 0)),
        ],
        out_specs=pl.BlockSpec(memory_space=pl.ANY),
        scratch_shapes=[pltpu.VMEM((RB, D), F32), pltpu.SemaphoreType.DMA, pltpu.SemaphoreType.DMA],
    )
    return pl.pallas_call(
        _dispatch_kernel,
        grid_spec=grid_spec,
        out_shape=jax.ShapeDtypeStruct((N_ROWS, D), F32),
        compiler_params=pltpu.CompilerParams(dimension_semantics=("arbitrary",),
                                             vmem_limit_bytes=VMEM_LIMIT),
        name="dispatch_rows",
    )(zrow, n_used, dest_t, h2)


def _moe_kernel(be_ref, nu_ref, xs_ref, wgu_ref, bgu_ref, wdn_ref, bdn_ref, o_ref, wgu_s, wdn_s):
    b = pl.program_id(0)
    prev = be_ref[jnp.maximum(b - 1, 0)]
    changed = jnp.logical_or(b == 0, be_ref[b] != prev)
    used = b < nu_ref[0]

    @pl.when(changed)
    def _():
        step = 128

        def body(c, carry):
            r = pl.multiple_of(c * step, step)
            wgu_s[pl.ds(r, step), :] = wgu_ref[pl.ds(r, step), :].astype(BF16)
            wdn_s[pl.ds(r, step), :] = wdn_ref[pl.ds(r, step), :].astype(BF16)
            return carry

        lax.fori_loop(0, D // step, body, 0)

    @pl.when(used)
    def _():
        gu = jnp.dot(xs_ref[...].astype(BF16), wgu_s[...], preferred_element_type=F32) + bgu_ref[...]
        g = jnp.minimum(gu[:, :D_FF], SWIGLU_LIMIT)
        l = jnp.clip(gu[:, D_FF:], -SWIGLU_LIMIT, SWIGLU_LIMIT)
        a = g * jax.nn.sigmoid(SWIGLU_ALPHA * g) * (l + 1.0)
        o_ref[...] = jnp.dot(a.astype(BF16), wdn_s[...], preferred_element_type=F32) + bdn_ref[...]

    @pl.when(jnp.logical_not(used))
    def _():
        o_ref[...] = jnp.zeros(o_ref.shape, F32)


def _moe_call(layer, block_e, n_used, xs, w_gu, b_gu, w_dn, b_dn):
    def rows(b, be, nu):
        return (b, 0)

    def expert(b, be, nu):
        return (layer, be[b], 0, 0)

    grid_spec = pltpu.PrefetchScalarGridSpec(
        num_scalar_prefetch=2,
        grid=(N_BLOCKS,),
        in_specs=[
            pl.BlockSpec((RB, D), rows),
            pl.BlockSpec((None, None, D, 2 * D_FF), expert),
            pl.BlockSpec((None, None, 1, 2 * D_FF), expert),
            pl.BlockSpec((None, None, D_FF, D), expert),
            pl.BlockSpec((None, None, 1, D), expert),
        ],
        out_specs=pl.BlockSpec((RB, D), rows),
        scratch_shapes=[pltpu.VMEM((D, 2 * D_FF), BF16), pltpu.VMEM((D_FF, D), BF16)],
    )
    return pl.pallas_call(
        _moe_kernel,
        grid_spec=grid_spec,
        out_shape=jax.ShapeDtypeStruct((N_ROWS, D), F32),
        compiler_params=pltpu.CompilerParams(dimension_semantics=("arbitrary",),
                                             vmem_limit_bytes=VMEM_LIMIT),
        name="moe_experts",
    )(block_e, n_used, xs, w_gu, b_gu, w_dn, b_dn)


def _final_kernel(x1_ref, ys_ref, rw_ref, mod_ref, g_ref, o_ref):
    rw = rw_ref[...]
    y = ys_ref[0] * rw[:, 0:1]
    for k in range(1, TOP_K):
        y = y + ys_ref[k] * rw[:, k:k + 1]
    o_ref[...] = x1_ref[...] + mod_ref[5:6, :] * _rms(y, g_ref[...])


def _final_call(x1, ysg, rw, mods, g_post_ffn):
    nb = N_TOK // TB
    return pl.pallas_call(
        _final_kernel,
        grid=(nb,),
        in_specs=[pl.BlockSpec((TB, D), lambda i: (i, 0)),
                  pl.BlockSpec((TOP_K, TB, D), lambda i: (0, i, 0)),
                  pl.BlockSpec((TB, LANES), lambda i: (i, 0)),
                  pl.BlockSpec((None, 6, D), lambda i: (_mod_index(i), 0, 0)),
                  pl.BlockSpec((1, D), lambda i: (0, 0))],
        out_specs=pl.BlockSpec((TB, D), lambda i: (i, 0)),
        out_shape=jax.ShapeDtypeStruct((N_TOK, D), F32),
        compiler_params=pltpu.CompilerParams(dimension_semantics=("parallel",),
                                             vmem_limit_bytes=VMEM_LIMIT),
        name="combine_residual",
    )(x1, ysg, rw, mods, g_post_ffn)


def _dft_tables(t):
    j = np.arange(t, dtype=np.int64)
    ang = 2.0 * np.pi * ((j[:, None] * j[None, :]) % t) / t
    return (np.cos(ang) / math.sqrt(t)).astype(np.float32), (np.sin(ang) / math.sqrt(t)).astype(np.float32)


def _channel_dft():
    c = np.arange(F_GROUP_DIM, dtype=np.int64)
    ang = 2.0 * np.pi * ((c[:, None] * c[None, :]) % F_GROUP_DIM) / F_GROUP_DIM
    eye = np.eye(F_GROUPS)
    bdc = np.kron(eye, np.cos(ang)) / math.sqrt(F_GROUP_DIM)
    bds = np.kron(eye, np.sin(ang)) / math.sqrt(F_GROUP_DIM)
    return bdc.astype(np.float32), bds.astype(np.float32)


def _rope_tables():
    pos = np.arange(LAT_T)
    n = ROPE // 4
    inv_freq = np.power(np.float32(ROPE_BASE), -np.arange(n, dtype=np.float32) / np.float32(n))
    ang_r = (pos // GRID_W).astype(np.float32)[:, None] * inv_freq
    ang_c = (pos % GRID_W).astype(np.float32)[:, None] * inv_freq
    cos = np.concatenate([np.cos(ang_r), np.cos(ang_r), np.cos(ang_c), np.cos(ang_c),
                          np.ones((LAT_T, LANES - ROPE))], axis=1)
    sin = np.concatenate([-np.sin(ang_r), np.sin(ang_r), -np.sin(ang_c), np.sin(ang_c),
                          np.zeros((LAT_T, LANES - ROPE))], axis=1)
    return cos.astype(np.float32), sin.astype(np.float32)


def kernel(x_prompt, x_sample, cache_ckv, cache_krope, c, c_ctx, w_ada, b_ada, g_pre_mix, g_post_mix, g_pre_ffn, g_post_ffn, w_in, g_sgu, w_spatial, b_spatial, g_q, w_uq, g_kv, w_ukv, w_out, w_router, b_router, w_gate_up, b_gate_up, w_down, b_down):
    x = jnp.concatenate([x_prompt.reshape(N_CTX, D), x_sample.reshape(N_LAT, D)], axis=0)

    cond = jnp.concatenate([c_ctx[None, :], c, jnp.zeros((5, D), F32)], axis=0)
    mods = _mod_call(cond.T, w_ada, b_ada.reshape(DEPTH, 1, 6 * D))
    mods = mods[:, :3].reshape(DEPTH, 3, 6, D)

    gmat = jnp.asarray(np.kron(np.eye(A_HEADS), np.full((A_HEAD_DIM, A_HEAD_DIM), 1.0 / A_HEAD_DIM)),
                       dtype=BF16)
    bdc_np, bds_np = _channel_dft()
    bdc, bds = jnp.asarray(bdc_np).astype(BF16), jnp.asarray(bds_np).astype(BF16)
    dft = {}
    for t in (CTX_T, LAT_T):
        ct_np, st_np = _dft_tables(t)
        dft[t] = (jnp.asarray(ct_np).astype(BF16), jnp.asarray(st_np).astype(BF16))
    cos_np, sin_np = _rope_tables()
    cos_t, sin_t = jnp.asarray(cos_np), jnp.asarray(sin_np)
    tri = jnp.asarray(np.tril(np.ones((TB, TB), np.float32), k=-1)).astype(BF16)

    b_gu = b_gate_up.reshape(DEPTH, N_EXPERTS, 1, 2 * D_FF)
    b_dn = b_down.reshape(DEPTH, N_EXPERTS, 1, D)

    ckv_layers, krope_layers = [], []
    for i in range(DEPTH):
        w_in_p = jnp.pad(w_in[i], ((0, 0), (0, IN_PAD - w_in.shape[-1]))).astype(BF16)
        w_uq_p = jnp.pad(w_uq[i].reshape(Q_LORA, HEADS, NOPE + ROPE),
                         ((0, 0), (0, 0), (0, HEAD_PAD - NOPE - ROPE))).reshape(Q_LORA, HEADS * HEAD_PAD)
        w_uq_p = w_uq_p.astype(BF16)
        w_ukv_b = w_ukv[i].astype(BF16)
        pre = _pre_call(x, mods[i], g_pre_mix[i][None, :], w_in_p, g_sgu[i][None, :], gmat,
                        g_q[i][None, :], w_uq_p, g_kv[i][None, :], w_ukv_b, cos_t, sin_t)
        u, vn, zf, q, kv, kr, ckv, zkr = pre
        ckv_layers.append(ckv[:N_CTX].reshape(N_CTX_B, CTX_T, KV_LORA))
        krope_layers.append(zkr[:N_CTX, :ROPE].reshape(N_CTX_B, CTX_T, ROPE))

        ws = w_spatial[i].astype(BF16)
        bs_full = jnp.repeat(b_spatial[i].T, A_HEAD_DIM, axis=1)
        mix_in = (u, vn, zf, q, kv, kr)
        y_ctx = _mix_call(mix_in, (ws, bs_full, bdc, bds) + dft[CTX_T], CTX_T, N_CTX_B, 0)
        ckr_p = jnp.pad(cache_krope[:, i], ((0, 0), (0, 0), (0, LANES - ROPE))).astype(BF16)
        y_lat = _mix_call(mix_in, (ws, bs_full, bdc, bds) + dft[LAT_T], LAT_T, N_LAT_B, N_CTX,
                          cache=(cache_ckv[:, i], ckr_p, w_ukv_b))
        ymix = jnp.concatenate([y_ctx, y_lat], axis=0)

        w_r = jnp.pad(w_router[i], ((0, 0), (0, LANES - N_EXPERTS)))
        b_r = jnp.pad(b_router[i], (0, LANES - N_EXPERTS), constant_values=NEG)[None, :]
        x1, h2, ri, rw, cnt = _post_call(ymix, x, mods[i], w_out[i].astype(BF16),
                                         g_post_mix[i][None, :], g_pre_ffn[i][None, :], w_r, b_r, tri)

        counts = cnt[0, :N_EXPERTS]
        nblk = (counts + RB - 1) // RB
        blk_end = jnp.cumsum(nblk)
        blk_start = blk_end - nblk
        n_used = blk_end[-1:]
        dest = blk_start[ri[:, :TOP_K]] * RB + ri[:, TOP_K:2 * TOP_K]
        dest_t = dest.reshape(N_TOK // TB, TB, TOP_K).transpose(0, 2, 1)
        zrow = blk_start * RB + counts // 8 * 8
        xs = _dispatch_call(zrow, n_used, dest_t, h2)
        blk = jnp.minimum(jnp.arange(N_BLOCKS, dtype=jnp.int32), n_used - 1)
        block_e = jnp.sum((blk_end[None, :] <= blk[:, None]).astype(jnp.int32), axis=1)
        block_e = jnp.minimum(block_e, N_EXPERTS - 1)
        ys = _moe_call(i, block_e, n_used, xs, w_gate_up, b_gu, w_down, b_dn)
        ysg = ys[dest.T.reshape(-1)].reshape(TOP_K, N_TOK, D)
        x = _final_call(x1, ysg, rw, mods[i], g_post_ffn[i][None, :])

    y_prompt = x[:N_CTX].reshape(N_CTX_B, CTX_T, D)
    y_sample = x[N_CTX:].reshape(N_LAT_B, LAT_T, D)
    return (y_prompt, y_sample, jnp.stack(ckv_layers, axis=1), jnp.stack(krope_layers, axis=1))
```

```python
import functools
import math

import jax
import jax.numpy as jnp
import numpy as np
from jax import lax
from jax.experimental import pallas as pl
from jax.experimental.pallas import tpu as pltpu

F32 = jnp.float32
BF16 = jnp.bfloat16

D = 1024
N_CTX_B, CTX_T = 16, 256
N_LAT_B, LAT_T = 2, 1024
PAST = 512
N_CTX = N_CTX_B * CTX_T
N_LAT = N_LAT_B * LAT_T
N_TOK = N_CTX + N_LAT
DEPTH = 2
GRID_W = 64
EPS = 1e-6
A_HEADS, A_HEAD_DIM, A_WIDTH, CHUNK = 4, 64, 256, 128
F_GROUPS, F_GROUP_DIM, F_WIDTH = 4, 64, 256
HEADS, Q_LORA, KV_LORA, NOPE, ROPE, V_DIM = 4, 256, 128, 128, 64, 128
HEAD_PAD = 256
IN_PAD = 1280
N_EXPERTS, TOP_K, D_FF = 32, 4, 1024
SWIGLU_LIMIT, SWIGLU_ALPHA = 7.0, 1.702
ROPE_BASE = 10000.0

TB = 256
QB = 256
RB = 256
LANES = 128
N_BLOCKS = N_TOK * TOP_K // RB + N_EXPERTS
N_ROWS = N_BLOCKS * RB
NEG = -3.0e38
VMEM_LIMIT = 56 * 1024 * 1024


def _rms(x, g):
    return x * lax.rsqrt(jnp.mean(x * x, axis=-1, keepdims=True) + EPS) * g


def _split_dot(v, m):
    hi = v.astype(BF16)
    lo = (v - hi.astype(F32)).astype(BF16)
    return (jnp.dot(hi, m, preferred_element_type=F32)
            + jnp.dot(lo, m, preferred_element_type=F32))


def _dot_nt(a, b):
    return lax.dot_general(a, b, (((1,), (1,)), ((), ())), preferred_element_type=F32)


def _mod_kernel(ct_ref, w_ref, b_ref, o_ref):
    ct = ct_ref[...]
    s = ct * jax.nn.sigmoid(ct)
    w = w_ref[...]
    o_ref[...] = jnp.zeros(o_ref.shape, F32)
    for r in range(3):
        o_ref[r:r + 1, :] = jnp.sum(w * s[:, r:r + 1], axis=0, keepdims=True) + b_ref[...]


def _mod_call(cond_t, w_ada, b_ada):
    cb = 512
    return pl.pallas_call(
        _mod_kernel,
        grid=(DEPTH, 6 * D // cb),
        in_specs=[
            pl.BlockSpec((D, 8), lambda l, j: (0, 0)),
            pl.BlockSpec((None, D, cb), lambda l, j: (l, 0, j)),
            pl.BlockSpec((None, 1, cb), lambda l, j: (l, 0, j)),
        ],
        out_specs=pl.BlockSpec((None, 8, cb), lambda l, j: (l, 0, j)),
        out_shape=jax.ShapeDtypeStruct((DEPTH, 8, 6 * D), F32),
        compiler_params=pltpu.CompilerParams(dimension_semantics=("parallel", "parallel")),
        name="modulation",
    )(cond_t, w_ada, b_ada)


def _mod_index(i):
    first_lat = N_CTX // TB
    return jnp.where(i < first_lat, 0, 1 + (i - first_lat) // (LAT_T // TB))


def _swap_halves(x, lane):
    w = x.shape[-1]
    fwd = pltpu.roll(x, w - 16, 1)
    bwd = pltpu.roll(x, 16, 1)
    return jnp.where((lane & 31) < 16, fwd, bwd)


def _pre_kernel(x_ref, mod_ref, gpre_ref, win_ref, gsgu_ref, gmat_ref, gq_ref, wuq_ref,
                gkv_ref, wukv_ref, cos_ref, sin_ref,
                u_ref, vn_ref, zf_ref, q_ref, kv_ref, kr_ref, ckv_ref, zkr_ref):
    i = pl.program_id(0)
    is_lat = i >= N_CTX // TB
    x = x_ref[...]
    h = _rms(x, gpre_ref[...]) * (1.0 + mod_ref[1:2, :]) + mod_ref[0:1, :]
    z = jnp.dot(h.astype(BF16), win_ref[...], preferred_element_type=F32)

    ga = jax.nn.gelu(z[:, :2 * A_WIDTH])
    u_ref[...] = ga[:, :A_WIDTH]
    v = ga[:, A_WIDTH:]
    gmat = gmat_ref[...]
    dv = v - _split_dot(v, gmat)
    var = _split_dot(dv * dv, gmat)
    vn_ref[...] = (dv * lax.rsqrt(var + EPS) * gsgu_ref[...]).astype(BF16)

    zf_ref[...] = z[:, 512:768].astype(BF16)

    cos = jnp.where(is_lat, cos_ref[...], 1.0)
    sin = jnp.where(is_lat, sin_ref[...], 0.0)
    lane = lax.broadcasted_iota(jnp.int32, (TB, LANES), 1)

    qn = _rms(z[:, 768:1024], gq_ref[...])
    scale = (NOPE + ROPE) ** -0.5
    q = jnp.dot(qn.astype(BF16), wuq_ref[...], preferred_element_type=F32) * scale
    for hd in range(HEADS):
        base = hd * HEAD_PAD
        q_ref[:, base:base + NOPE] = q[:, base:base + NOPE].astype(BF16)
        qr = q[:, base + NOPE:base + HEAD_PAD]
        q_ref[:, base + NOPE:base + HEAD_PAD] = (qr * cos + _swap_halves(qr, lane) * sin).astype(BF16)

    ckv = _rms(z[:, 1024:1152], gkv_ref[...])
    ckv_ref[...] = ckv
    kv_ref[...] = jnp.dot(ckv.astype(BF16), wukv_ref[...], preferred_element_type=F32).astype(BF16)

    zkr = z[:, 1152:1280]
    zkr_ref[...] = zkr
    kr_ref[...] = (zkr * cos + _swap_halves(zkr, lane) * sin).astype(BF16)


def _pre_call(x, mods, g_pre, w_in, g_sgu, gmat, g_q, w_uq, g_kv, w_ukv, cos_t, sin_t):
    nb = N_TOK // TB
    first_lat = N_CTX // TB
    pos_blocks = LAT_T // TB

    def tok(width):
        return pl.BlockSpec((TB, width), lambda i: (i, 0))

    def full(shape):
        return pl.BlockSpec(shape, lambda i: (0,) * len(shape))

    def rope_map(i):
        return (jnp.where(i >= first_lat, (i - first_lat) % pos_blocks, 0), 0)

    out_shape = (
        jax.ShapeDtypeStruct((N_TOK, A_WIDTH), F32),
        jax.ShapeDtypeStruct((N_TOK, A_WIDTH), BF16),
        jax.ShapeDtypeStruct((N_TOK, F_WIDTH), BF16),
        jax.ShapeDtypeStruct((N_TOK, HEADS * HEAD_PAD), BF16),
        jax.ShapeDtypeStruct((N_TOK, HEADS * (NOPE + V_DIM)), BF16),
        jax.ShapeDtypeStruct((N_TOK, LANES), BF16),
        jax.ShapeDtypeStruct((N_TOK, KV_LORA), F32),
        jax.ShapeDtypeStruct((N_TOK, LANES), F32),
    )
    return pl.pallas_call(
        _pre_kernel,
        grid=(nb,),
        in_specs=[
            tok(D),
            pl.BlockSpec((None, 6, D), lambda i: (_mod_index(i), 0, 0)),
            full((1, D)), full((D, IN_PAD)), full((1, A_WIDTH)), full((A_WIDTH, A_WIDTH)),
            full((1, Q_LORA)), full((Q_LORA, HEADS * HEAD_PAD)),
            full((1, KV_LORA)), full((KV_LORA, HEADS * (NOPE + V_DIM))),
            pl.BlockSpec((TB, LANES), rope_map), pl.BlockSpec((TB, LANES), rope_map),
        ],
        out_specs=(tok(A_WIDTH), tok(A_WIDTH), tok(F_WIDTH), tok(HEADS * HEAD_PAD),
                   tok(HEADS * (NOPE + V_DIM)), tok(LANES), tok(KV_LORA), tok(LANES)),
        out_shape=out_shape,
        compiler_params=pltpu.CompilerParams(dimension_semantics=("parallel",),
                                             vmem_limit_bytes=VMEM_LIMIT),
        name="pre_mix",
    )(x, mods, g_pre, w_in, g_sgu, gmat, g_q, w_uq, g_kv, w_ukv, cos_t, sin_t)


def _mix_kernel(*refs, has_cache):
    if has_cache:
        (u_ref, vn_ref, zf_ref, q_ref, kv_ref, kr_ref, ws_ref, bs_ref, bdc_ref, bds_ref,
         ct_ref, st_ref, cckv_ref, ckr_ref, wukv_ref, o_ref) = refs
    else:
        (u_ref, vn_ref, zf_ref, q_ref, kv_ref, kr_ref, ws_ref, bs_ref, bdc_ref, bds_ref,
         ct_ref, st_ref, o_ref) = refs

    lane = lax.broadcasted_iota(jnp.int32, (CHUNK, A_WIDTH), 1)
    for c in range(QB // CHUNK):
        rows = slice(c * CHUNK, (c + 1) * CHUNK)
        vch = vn_ref[rows, :]
        s = bs_ref[...]
        for g in range(A_HEADS):
            sg = jnp.dot(ws_ref[g], vch, preferred_element_type=F32)
            in_head = (lane >= g * A_HEAD_DIM) & (lane < (g + 1) * A_HEAD_DIM)
            s = s + jnp.where(in_head, sg, 0.0)
        o_ref[rows, 0:A_WIDTH] = (u_ref[rows, :] * s).astype(BF16)

    zf = zf_ref[...]
    zc = jnp.dot(zf, bdc_ref[...], preferred_element_type=F32).astype(BF16)
    zs = jnp.dot(zf, bds_ref[...], preferred_element_type=F32).astype(BF16)
    yf = (jnp.dot(ct_ref[...], zc, preferred_element_type=F32)
          - jnp.dot(st_ref[...], zs, preferred_element_type=F32))
    o_ref[:, A_WIDTH:A_WIDTH + F_WIDTH] = yf.astype(BF16)

    kr = kr_ref[...]
    if has_cache:
        kvc = jnp.dot(cckv_ref[...].astype(BF16), wukv_ref[...],
                      preferred_element_type=F32).astype(BF16)
        krc = ckr_ref[...]
    for hd in range(HEADS):
        qh = q_ref[:, hd * HEAD_PAD:(hd + 1) * HEAD_PAD]
        kb = hd * (NOPE + V_DIM)
        kh = jnp.concatenate([kv_ref[:, kb:kb + NOPE], kr], axis=1)
        vh = kv_ref[:, kb + NOPE:kb + NOPE + V_DIM]
        s = _dot_nt(qh, kh)
        m = jnp.max(s, axis=-1, keepdims=True)
        if has_cache:
            khc = jnp.concatenate([kvc[:, kb:kb + NOPE], krc], axis=1)
            vhc = kvc[:, kb + NOPE:kb + NOPE + V_DIM]
            sc = _dot_nt(qh, khc)
            m = jnp.maximum(m, jnp.max(sc, axis=-1, keepdims=True))
        e = jnp.exp(s - m)
        den = jnp.sum(e, axis=-1, keepdims=True)
        o = jnp.dot(e.astype(BF16), vh, preferred_element_type=F32)
        if has_cache:
            ec = jnp.exp(sc - m)
            den = den + jnp.sum(ec, axis=-1, keepdims=True)
            o = o + jnp.dot(ec.astype(BF16), vhc, preferred_element_type=F32)
        ob = A_WIDTH + F_WIDTH + hd * V_DIM
        o_ref[:, ob:ob + V_DIM] = (o * (1.0 / den)).astype(BF16)


def _mix_call(pre, consts, seq_t, n_batch, tok_off, cache=None):
    u, vn, zf, q, kv, kr = pre
    ws, bs_full, bdc, bds, ct, st = consts
    nq = seq_t // QB
    qoff = tok_off // QB
    soff = tok_off // seq_t

    def qrow(width):
        return pl.BlockSpec((QB, width), lambda b, j: (qoff + b * nq + j, 0))

    def srow(width):
        return pl.BlockSpec((seq_t, width), lambda b, j: (soff + b, 0))

    def full(shape):
        return pl.BlockSpec(shape, lambda b, j: (0,) * len(shape))

    in_specs = [qrow(A_WIDTH), qrow(A_WIDTH), srow(F_WIDTH), qrow(HEADS * HEAD_PAD),
                srow(HEADS * (NOPE + V_DIM)), srow(LANES),
                full((A_HEADS, CHUNK, CHUNK)), full((CHUNK, A_WIDTH)),
                full((F_WIDTH, F_WIDTH)), full((F_WIDTH, F_WIDTH)),
                pl.BlockSpec((QB, seq_t), lambda b, j: (j, 0)),
                pl.BlockSpec((QB, seq_t), lambda b, j: (j, 0))]
    args = [u, vn, zf, q, kv, kr, ws, bs_full, bdc, bds, ct, st]
    if cache is not None:
        cckv, ckr, wukv = cache
        in_specs += [pl.BlockSpec((None, PAST, KV_LORA), lambda b, j: (b, 0, 0)),
                     pl.BlockSpec((None, PAST, LANES), lambda b, j: (b, 0, 0)),
                     full((KV_LORA, HEADS * (NOPE + V_DIM)))]
        args += [cckv, ckr, wukv]
    return pl.pallas_call(
        functools.partial(_mix_kernel, has_cache=cache is not None),
        grid=(n_batch, nq),
        in_specs=in_specs,
        out_specs=pl.BlockSpec((QB, D), lambda b, j: (b * nq + j, 0)),
        out_shape=jax.ShapeDtypeStruct((n_batch * seq_t, D), BF16),
        compiler_params=pltpu.CompilerParams(dimension_semantics=("parallel", "parallel"),
                                             vmem_limit_bytes=VMEM_LIMIT),
        name="mix_lat" if cache is not None else "mix_ctx",
    )(*args)


def _post_kernel(y_ref, x_ref, mod_ref, wout_ref, gpost_ref, gffn_ref, wr_ref, br_ref, tri_ref,
                 x1_ref, h2_ref, ri_ref, rw_ref, cnt_ref, carry_ref):
    i = pl.program_id(0)

    @pl.when(i == 0)
    def _():
        carry_ref[...] = jnp.zeros(carry_ref.shape, F32)

    y = jnp.dot(y_ref[...], wout_ref[...], preferred_element_type=F32)
    x1 = x_ref[...] + mod_ref[2:3, :] * _rms(y, gpost_ref[...])
    x1_ref[...] = x1
    h2 = _rms(x1, gffn_ref[...]) * (1.0 + mod_ref[4:5, :]) + mod_ref[3:4, :]
    h2_ref[...] = h2

    wr = wr_ref[...]
    wr_hi = wr.astype(BF16)
    wr_lo = (wr - wr_hi.astype(F32)).astype(BF16)
    h_hi = h2.astype(BF16)
    h_lo = (h2 - h_hi.astype(F32)).astype(BF16)
    logits = (jnp.dot(h_hi, wr_hi, preferred_element_type=F32)
              + jnp.dot(h_lo, wr_hi, preferred_element_type=F32)
              + jnp.dot(h_hi, wr_lo, preferred_element_type=F32)) + br_ref[...]

    lane = lax.broadcasted_iota(jnp.int32, (TB, LANES), 1)
    lane_f = lane.astype(F32)
    work = logits
    idx, val = [], []
    for _ in range(TOP_K):
        m = jnp.max(work, axis=-1, keepdims=True)
        ik = jnp.min(jnp.where(work == m, lane_f, float(LANES)), axis=-1, keepdims=True)
        idx.append(ik)
        val.append(m)
        work = jnp.where(lane_f == ik, NEG, work)
    ex = [jnp.exp(v - val[0]) for v in val]
    den = ex[0] + ex[1] + ex[2] + ex[3]

    onehot = jnp.zeros((TB, LANES), F32)
    for k in range(TOP_K):
        onehot = onehot + jnp.where(lane_f == idx[k], 1.0, 0.0)
    before = jnp.dot(tri_ref[...], onehot.astype(BF16), preferred_element_type=F32) + carry_ref[0:1, :]
    ri = jnp.zeros((TB, LANES), F32)
    rw = jnp.zeros((TB, LANES), F32)
    for k in range(TOP_K):
        rank_k = jnp.sum(jnp.where(lane_f == idx[k], before, 0.0), axis=-1, keepdims=True)
        ri = ri + jnp.where(lane == k, idx[k], 0.0) + jnp.where(lane == TOP_K + k, rank_k, 0.0)
        rw = rw + jnp.where(lane == k, ex[k] / den, 0.0)
    ri_ref[...] = ri.astype(jnp.int32)
    rw_ref[...] = rw
    total = carry_ref[0:1, :] + jnp.sum(onehot, axis=0, keepdims=True)
    carry_ref[...] = jnp.broadcast_to(total, carry_ref.shape)
    cnt_ref[...] = jnp.broadcast_to(total, cnt_ref.shape).astype(jnp.int32)


def _post_call(ymix, x, mods, w_out, g_post, g_ffn, w_r, b_r, tri):
    nb = N_TOK // TB

    def tok(width):
        return pl.BlockSpec((TB, width), lambda i: (i, 0))

    def full(shape):
        return pl.BlockSpec(shape, lambda i: (0,) * len(shape))

    return pl.pallas_call(
        _post_kernel,
        grid=(nb,),
        in_specs=[tok(D), tok(D), pl.BlockSpec((None, 6, D), lambda i: (_mod_index(i), 0, 0)),
                  full((D, D)), full((1, D)), full((1, D)), full((D, LANES)), full((1, LANES)),
                  full((TB, TB))],
        out_specs=(tok(D), tok(D), tok(LANES), tok(LANES), full((8, LANES))),
        out_shape=(jax.ShapeDtypeStruct((N_TOK, D), F32),
                   jax.ShapeDtypeStruct((N_TOK, D), F32),
                   jax.ShapeDtypeStruct((N_TOK, LANES), jnp.int32),
                   jax.ShapeDtypeStruct((N_TOK, LANES), F32),
                   jax.ShapeDtypeStruct((8, LANES), jnp.int32)),
        scratch_shapes=[pltpu.VMEM((8, LANES), F32)],
        compiler_params=pltpu.CompilerParams(dimension_semantics=("arbitrary",),
                                             vmem_limit_bytes=VMEM_LIMIT),
        name="post_mix_router",
    )(ymix, x, mods, w_out, g_post, g_ffn, w_r, b_r, tri)


def _dispatch_kernel(zrow_ref, nu_ref, dest_ref, h2_ref, xs_ref, zero_ref, sem, zsem):
    i = pl.program_id(0)

    def zero_fill(row):
        return pltpu.make_async_copy(zero_ref, xs_ref.at[pl.ds(pl.multiple_of(row, 8), RB)], zsem)

    @pl.when(i == 0)
    def _():
        zero_ref[...] = jnp.zeros(zero_ref.shape, F32)
        for e in range(N_EXPERTS):
            zero_fill(zrow_ref[e]).start()
        for e in range(N_EXPERTS):
            zero_fill(zrow_ref[e]).wait()

        def tail_start(b, carry):
            zero_fill(b * RB).start()
            return carry

        def tail_wait(b, carry):
            zero_fill(b * RB).wait()
            return carry

        lax.fori_loop(nu_ref[0], N_BLOCKS, tail_start, 0)
        lax.fori_loop(nu_ref[0], N_BLOCKS, tail_wait, 0)

    def body(t, carry):
        for k in range(TOP_K):
            pltpu.make_async_copy(h2_ref.at[pl.ds(t, 1)], xs_ref.at[pl.ds(dest_ref[k, t], 1)],
                                  sem).start(priority=k % 2)
        return carry

    lax.fori_loop(0, TB, body, 0, unroll=8)
    for k in range(TOP_K):
        pltpu.make_async_copy(h2_ref, xs_ref.at[pl.ds(0, TB)], sem).wait()


def _dispatch_call(zrow, n_used, dest_t, h2):
    nb = N_TOK // TB
    grid_spec = pltpu.PrefetchScalarGridSpec(
        num_scalar_prefetch=2,
        grid=(nb,),
        in_specs=[
            pl.BlockSpec((None, TOP_K, TB), lambda i, z, n: (i, 0, 0), memory_space=pltpu.SMEM),
            pl.BlockSpec((TB, D), lambda i, z, n: (i, 0)),
        ],
        out_specs=pl.BlockSpec(memory_space=pl.ANY),
        scratch_shapes=[pltpu.VMEM((RB, D), F32), pltpu.SemaphoreType.DMA, pltpu.SemaphoreType.DMA],
    )
    return pl.pallas_call(
        _dispatch_kernel,
        grid_spec=grid_spec,
        out_shape=jax.ShapeDtypeStruct((N_ROWS, D), F32),
        compiler_params=pltpu.CompilerParams(dimension_semantics=("arbitrary",),
                                             vmem_limit_bytes=VMEM_LIMIT),
        name="dispatch_rows",
    )(zrow, n_used, dest_t, h2)


def _moe_kernel(be_ref, first_ref, nxt_ref, slot_ref, nu_ref, xs_ref, wgu_hbm, bgu_ref, wdn_hbm, bdn_ref,
                o_ref, wgu_f, wdn_f, wgu_s, wdn_s, sems, *, layer):
    b = pl.program_id(0)
    used = b < nu_ref[0]

    def fetch(e, s):
        return (pltpu.make_async_copy(wgu_hbm.at[layer, e], wgu_f.at[s], sems.at[0, s]),
                pltpu.make_async_copy(wdn_hbm.at[layer, e], wdn_f.at[s], sems.at[1, s]))

    @pl.when(b == 0)
    def _():
        for cp in fetch(be_ref[0], 0):
            cp.start()

    @pl.when(first_ref[b] == 1)
    def _():
        s = slot_ref[b]
        for cp in fetch(be_ref[b], s):
            cp.wait()

        @pl.when(nxt_ref[b] >= 0)
        def _():
            for cp in fetch(nxt_ref[b], 1 - s):
                cp.start()

        step = 128

        def body(c, carry):
            r = pl.multiple_of(c * step, step)
            wgu_s[pl.ds(r, step), :] = wgu_f[s, pl.ds(r, step), :].astype(BF16)
            wdn_s[pl.ds(r, step), :] = wdn_f[s, pl.ds(r, step), :].astype(BF16)
            return carry

        lax.fori_loop(0, D // step, body, 0)

    @pl.when(used)
    def _():
        gu = jnp.dot(xs_ref[...].astype(BF16), wgu_s[...], preferred_element_type=F32) + bgu_ref[...]
        g = jnp.minimum(gu[:, :D_FF], SWIGLU_LIMIT)
        l = jnp.clip(gu[:, D_FF:], -SWIGLU_LIMIT, SWIGLU_LIMIT)
        a = g * jax.nn.sigmoid(SWIGLU_ALPHA * g) * (l + 1.0)
        o_ref[...] = jnp.dot(a.astype(BF16), wdn_s[...], preferred_element_type=F32) + bdn_ref[...]

    @pl.when(jnp.logical_not(used))
    def _():
        o_ref[...] = jnp.zeros(o_ref.shape, F32)


def _moe_call(layer, tables, xs, w_gu, b_gu, w_dn, b_dn):
    def rows_in(b, be, fi, nx, sl, nu):
        return (jnp.minimum(b, nu[0] - 1), 0)

    def rows_out(b, be, fi, nx, sl, nu):
        return (b, 0)

    def expert(b, be, fi, nx, sl, nu):
        return (layer, be[b], 0, 0)

    grid_spec = pltpu.PrefetchScalarGridSpec(
        num_scalar_prefetch=5,
        grid=(N_BLOCKS,),
        in_specs=[
            pl.BlockSpec((RB, D), rows_in),
            pl.BlockSpec(memory_space=pl.ANY),
            pl.BlockSpec((None, None, 1, 2 * D_FF), expert),
            pl.BlockSpec(memory_space=pl.ANY),
            pl.BlockSpec((None, None, 1, D), expert),
        ],
        out_specs=pl.BlockSpec((RB, D), rows_out),
        scratch_shapes=[pltpu.VMEM((2, D, 2 * D_FF), F32), pltpu.VMEM((2, D_FF, D), F32),
                        pltpu.VMEM((D, 2 * D_FF), BF16), pltpu.VMEM((D_FF, D), BF16),
                        pltpu.SemaphoreType.DMA((2, 2))],
    )
    return pl.pallas_call(
        functools.partial(_moe_kernel, layer=layer),
        grid_spec=grid_spec,
        out_shape=jax.ShapeDtypeStruct((N_ROWS, D), F32),
        compiler_params=pltpu.CompilerParams(dimension_semantics=("arbitrary",),
                                             vmem_limit_bytes=VMEM_LIMIT),
        name="moe_experts",
    )(*tables, xs, w_gu, b_gu, w_dn, b_dn)


def _final_kernel(x1_ref, ys_ref, rw_ref, mod_ref, g_ref, o_ref):
    rw = rw_ref[...]
    y = ys_ref[0] * rw[:, 0:1]
    for k in range(1, TOP_K):
        y = y + ys_ref[k] * rw[:, k:k + 1]
    o_ref[...] = x1_ref[...] + mod_ref[5:6, :] * _rms(y, g_ref[...])


def _final_call(x1, ysg, rw, mods, g_post_ffn):
    nb = N_TOK // TB
    return pl.pallas_call(
        _final_kernel,
        grid=(nb,),
        in_specs=[pl.BlockSpec((TB, D), lambda i: (i, 0)),
                  pl.BlockSpec((TOP_K, TB, D), lambda i: (0, i, 0)),
                  pl.BlockSpec((TB, LANES), lambda i: (i, 0)),
                  pl.BlockSpec((None, 6, D), lambda i: (_mod_index(i), 0, 0)),
                  pl.BlockSpec((1, D), lambda i: (0, 0))],
        out_specs=pl.BlockSpec((TB, D), lambda i: (i, 0)),
        out_shape=jax.ShapeDtypeStruct((N_TOK, D), F32),
        compiler_params=pltpu.CompilerParams(dimension_semantics=("parallel",),
                                             vmem_limit_bytes=VMEM_LIMIT),
        name="combine_residual",
    )(x1, ysg, rw, mods, g_post_ffn)


def _dft_tables(t):
    j = np.arange(t, dtype=np.int64)
    ang = 2.0 * np.pi * ((j[:, None] * j[None, :]) % t) / t
    return (np.cos(ang) / math.sqrt(t)).astype(np.float32), (np.sin(ang) / math.sqrt(t)).astype(np.float32)


def _channel_dft():
    c = np.arange(F_GROUP_DIM, dtype=np.int64)
    ang = 2.0 * np.pi * ((c[:, None] * c[None, :]) % F_GROUP_DIM) / F_GROUP_DIM
    eye = np.eye(F_GROUPS)
    bdc = np.kron(eye, np.cos(ang)) / math.sqrt(F_GROUP_DIM)
    bds = np.kron(eye, np.sin(ang)) / math.sqrt(F_GROUP_DIM)
    return bdc.astype(np.float32), bds.astype(np.float32)


def _rope_tables():
    pos = np.arange(LAT_T)
    n = ROPE // 4
    inv_freq = np.power(np.float32(ROPE_BASE), -np.arange(n, dtype=np.float32) / np.float32(n))
    ang_r = (pos // GRID_W).astype(np.float32)[:, None] * inv_freq
    ang_c = (pos % GRID_W).astype(np.float32)[:, None] * inv_freq
    cos = np.concatenate([np.cos(ang_r), np.cos(ang_r), np.cos(ang_c), np.cos(ang_c),
                          np.ones((LAT_T, LANES - ROPE))], axis=1)
    sin = np.concatenate([-np.sin(ang_r), np.sin(ang_r), -np.sin(ang_c), np.sin(ang_c),
                          np.zeros((LAT_T, LANES - ROPE))], axis=1)
    return cos.astype(np.float32), sin.astype(np.float32)


def kernel(x_prompt, x_sample, cache_ckv, cache_krope, c, c_ctx, w_ada, b_ada, g_pre_mix, g_post_mix, g_pre_ffn, g_post_ffn, w_in, g_sgu, w_spatial, b_spatial, g_q, w_uq, g_kv, w_ukv, w_out, w_router, b_router, w_gate_up, b_gate_up, w_down, b_down):
    x = jnp.concatenate([x_prompt.reshape(N_CTX, D), x_sample.reshape(N_LAT, D)], axis=0)

    cond = jnp.concatenate([c_ctx[None, :], c, jnp.zeros((5, D), F32)], axis=0)
    mods = _mod_call(cond.T, w_ada, b_ada.reshape(DEPTH, 1, 6 * D))
    mods = mods[:, :3].reshape(DEPTH, 3, 6, D)

    gmat = jnp.asarray(np.kron(np.eye(A_HEADS), np.full((A_HEAD_DIM, A_HEAD_DIM), 1.0 / A_HEAD_DIM)),
                       dtype=BF16)
    bdc_np, bds_np = _channel_dft()
    bdc, bds = jnp.asarray(bdc_np).astype(BF16), jnp.asarray(bds_np).astype(BF16)
    dft = {}
    for t in (CTX_T, LAT_T):
        ct_np, st_np = _dft_tables(t)
        dft[t] = (jnp.asarray(ct_np).astype(BF16), jnp.asarray(st_np).astype(BF16))
    cos_np, sin_np = _rope_tables()
    cos_t, sin_t = jnp.asarray(cos_np), jnp.asarray(sin_np)
    tri = jnp.asarray(np.tril(np.ones((TB, TB), np.float32), k=-1)).astype(BF16)

    b_gu = b_gate_up.reshape(DEPTH, N_EXPERTS, 1, 2 * D_FF)
    b_dn = b_down.reshape(DEPTH, N_EXPERTS, 1, D)

    ckv_layers, krope_layers = [], []
    for i in range(DEPTH):
        w_in_p = jnp.pad(w_in[i], ((0, 0), (0, IN_PAD - w_in.shape[-1]))).astype(BF16)
        w_uq_p = jnp.pad(w_uq[i].reshape(Q_LORA, HEADS, NOPE + ROPE),
                         ((0, 0), (0, 0), (0, HEAD_PAD - NOPE - ROPE))).reshape(Q_LORA, HEADS * HEAD_PAD)
        w_uq_p = w_uq_p.astype(BF16)
        w_ukv_b = w_ukv[i].astype(BF16)
        pre = _pre_call(x, mods[i], g_pre_mix[i][None, :], w_in_p, g_sgu[i][None, :], gmat,
                        g_q[i][None, :], w_uq_p, g_kv[i][None, :], w_ukv_b, cos_t, sin_t)
        u, vn, zf, q, kv, kr, ckv, zkr = pre
        ckv_layers.append(ckv[:N_CTX].reshape(N_CTX_B, CTX_T, KV_LORA))
        krope_layers.append(zkr[:N_CTX, :ROPE].reshape(N_CTX_B, CTX_T, ROPE))

        ws = w_spatial[i].astype(BF16)
        bs_full = jnp.repeat(b_spatial[i].T, A_HEAD_DIM, axis=1)
        mix_in = (u, vn, zf, q, kv, kr)
        y_ctx = _mix_call(mix_in, (ws, bs_full, bdc, bds) + dft[CTX_T], CTX_T, N_CTX_B, 0)
        ckr_p = jnp.pad(cache_krope[:, i], ((0, 0), (0, 0), (0, LANES - ROPE))).astype(BF16)
        y_lat = _mix_call(mix_in, (ws, bs_full, bdc, bds) + dft[LAT_T], LAT_T, N_LAT_B, N_CTX,
                          cache=(cache_ckv[:, i], ckr_p, w_ukv_b))
        ymix = jnp.concatenate([y_ctx, y_lat], axis=0)

        w_r = jnp.pad(w_router[i], ((0, 0), (0, LANES - N_EXPERTS)))
        b_r = jnp.pad(b_router[i], (0, LANES - N_EXPERTS), constant_values=NEG)[None, :]
        x1, h2, ri, rw, cnt = _post_call(ymix, x, mods[i], w_out[i].astype(BF16),
                                         g_post_mix[i][None, :], g_pre_ffn[i][None, :], w_r, b_r, tri)

        counts = cnt[0, :N_EXPERTS]
        nblk = (counts + RB - 1) // RB
        blk_end = jnp.cumsum(nblk)
        blk_start = blk_end - nblk
        n_used = blk_end[-1:]
        dest = blk_start[ri[:, :TOP_K]] * RB + ri[:, TOP_K:2 * TOP_K]
        dest_t = dest.reshape(N_TOK // TB, TB, TOP_K).transpose(0, 2, 1)
        zrow = blk_start * RB + counts // 8 * 8
        xs = _dispatch_call(zrow, n_used, dest_t, h2)
        blk_id = jnp.arange(N_BLOCKS, dtype=jnp.int32)
        blk = jnp.minimum(blk_id, n_used - 1)
        block_e = jnp.sum((blk_end[None, :] <= blk[:, None]).astype(jnp.int32), axis=1)
        block_e = jnp.minimum(block_e, N_EXPERTS - 1)
        eid = jnp.arange(N_EXPERTS, dtype=jnp.int32)
        owns = nblk > 0
        later = jnp.where((eid[None, :] > eid[:, None]) & owns[None, :], eid[None, :], N_EXPERTS)
        nxt_e = jnp.min(later, axis=1)
        nxt_e = jnp.where(nxt_e == N_EXPERTS, -1, nxt_e)
        run_id = jnp.cumsum(owns.astype(jnp.int32)) - 1
        first = ((blk_id == blk_start[block_e]) & (blk_id < n_used)).astype(jnp.int32)
        tables = (block_e, first, nxt_e[block_e], run_id[block_e] % 2, n_used)
        ys = _moe_call(i, tables, xs, w_gate_up, b_gu, w_down, b_dn)
        ysg = ys[dest.T.reshape(-1)].reshape(TOP_K, N_TOK, D)
        x = _final_call(x1, ysg, rw, mods[i], g_post_ffn[i][None, :])

    y_prompt = x[:N_CTX].reshape(N_CTX_B, CTX_T, D)
    y_sample = x[N_CTX:].reshape(N_LAT_B, LAT_T, D)
    return (y_prompt, y_sample, jnp.stack(ckv_layers, axis=1), jnp.stack(krope_layers, axis=1))
```

```python
import functools
import math

import jax
import jax.numpy as jnp
import numpy as np
from jax import lax
from jax.experimental import pallas as pl
from jax.experimental.pallas import tpu as pltpu

F32 = jnp.float32
BF16 = jnp.bfloat16

D = 1024
N_CTX_B, CTX_T = 16, 256
N_LAT_B, LAT_T = 2, 1024
PAST = 512
N_CTX = N_CTX_B * CTX_T
N_LAT = N_LAT_B * LAT_T
N_TOK = N_CTX + N_LAT
DEPTH = 2
GRID_W = 64
EPS = 1e-6
A_HEADS, A_HEAD_DIM, A_WIDTH, CHUNK = 4, 64, 256, 128
F_GROUPS, F_GROUP_DIM, F_WIDTH = 4, 64, 256
HEADS, Q_LORA, KV_LORA, NOPE, ROPE, V_DIM = 4, 256, 128, 128, 64, 128
HEAD_PAD = 256
IN_PAD = 1280
N_EXPERTS, TOP_K, D_FF = 32, 4, 1024
SWIGLU_LIMIT, SWIGLU_ALPHA = 7.0, 1.702
ROPE_BASE = 10000.0

TB = 256
QB = 256
RB = 256
LANES = 128
N_BLOCKS = N_TOK * TOP_K // RB + N_EXPERTS
N_ROWS = N_BLOCKS * RB
NEG = -3.0e38
VMEM_LIMIT = 56 * 1024 * 1024


def _rms(x, g):
    return x * lax.rsqrt(jnp.mean(x * x, axis=-1, keepdims=True) + EPS) * g


def _split_dot(v, m):
    hi = v.astype(BF16)
    lo = (v - hi.astype(F32)).astype(BF16)
    return (jnp.dot(hi, m, preferred_element_type=F32)
            + jnp.dot(lo, m, preferred_element_type=F32))


def _dot_nt(a, b):
    return lax.dot_general(a, b, (((1,), (1,)), ((), ())), preferred_element_type=F32)


def _mod_kernel(ct_ref, w_ref, b_ref, o_ref):
    ct = ct_ref[...]
    s = ct * jax.nn.sigmoid(ct)
    w = w_ref[...]
    o_ref[...] = jnp.zeros(o_ref.shape, F32)
    for r in range(3):
        o_ref[r:r + 1, :] = jnp.sum(w * s[:, r:r + 1], axis=0, keepdims=True) + b_ref[...]


def _mod_call(cond_t, w_ada, b_ada):
    cb = 512
    return pl.pallas_call(
        _mod_kernel,
        grid=(DEPTH, 6 * D // cb),
        in_specs=[
            pl.BlockSpec((D, 8), lambda l, j: (0, 0)),
            pl.BlockSpec((None, D, cb), lambda l, j: (l, 0, j)),
            pl.BlockSpec((None, 1, cb), lambda l, j: (l, 0, j)),
        ],
        out_specs=pl.BlockSpec((None, 8, cb), lambda l, j: (l, 0, j)),
        out_shape=jax.ShapeDtypeStruct((DEPTH, 8, 6 * D), F32),
        compiler_params=pltpu.CompilerParams(dimension_semantics=("parallel", "parallel")),
        name="modulation",
    )(cond_t, w_ada, b_ada)


def _mod_index(i):
    first_lat = N_CTX // TB
    return jnp.where(i < first_lat, 0, 1 + (i - first_lat) // (LAT_T // TB))


def _ctx_spec(width):
    return pl.BlockSpec((TB, width), lambda i: (jnp.minimum(i, N_CTX // TB - 1), 0))


def _lat_spec(width):
    return pl.BlockSpec((TB, width), lambda i: (jnp.maximum(i - N_CTX // TB, 0), 0))


def _pick(i, ctx_ref, lat_ref):
    return jnp.where(i >= N_CTX // TB, lat_ref[...], ctx_ref[...])


def _swap_halves(x, lane):
    w = x.shape[-1]
    fwd = pltpu.roll(x, w - 16, 1)
    bwd = pltpu.roll(x, 16, 1)
    return jnp.where((lane & 31) < 16, fwd, bwd)


def _pre_kernel(xc_ref, xl_ref, mod_ref, gpre_ref, win_ref, gsgu_ref, gmat_ref, gq_ref, wuq_ref,
                gkv_ref, wukv_ref, cos_ref, sin_ref,
                u_ref, vn_ref, zf_ref, q_ref, kv_ref, kr_ref, ckv_ref, zkr_ref):
    i = pl.program_id(0)
    is_lat = i >= N_CTX // TB
    x = _pick(i, xc_ref, xl_ref)
    h = _rms(x, gpre_ref[...]) * (1.0 + mod_ref[1:2, :]) + mod_ref[0:1, :]
    z = jnp.dot(h.astype(BF16), win_ref[...], preferred_element_type=F32)

    ga = jax.nn.gelu(z[:, :2 * A_WIDTH])
    u_ref[...] = ga[:, :A_WIDTH]
    v = ga[:, A_WIDTH:]
    gmat = gmat_ref[...]
    dv = v - _split_dot(v, gmat)
    var = _split_dot(dv * dv, gmat)
    vn_ref[...] = (dv * lax.rsqrt(var + EPS) * gsgu_ref[...]).astype(BF16)

    zf_ref[...] = z[:, 512:768].astype(BF16)

    cos = jnp.where(is_lat, cos_ref[...], 1.0)
    sin = jnp.where(is_lat, sin_ref[...], 0.0)
    lane = lax.broadcasted_iota(jnp.int32, (TB, LANES), 1)

    qn = _rms(z[:, 768:1024], gq_ref[...])
    scale = (NOPE + ROPE) ** -0.5
    q = jnp.dot(qn.astype(BF16), wuq_ref[...], preferred_element_type=F32) * scale
    for hd in range(HEADS):
        base = hd * HEAD_PAD
        q_ref[:, base:base + NOPE] = q[:, base:base + NOPE].astype(BF16)
        qr = q[:, base + NOPE:base + HEAD_PAD]
        q_ref[:, base + NOPE:base + HEAD_PAD] = (qr * cos + _swap_halves(qr, lane) * sin).astype(BF16)

    ckv = _rms(z[:, 1024:1152], gkv_ref[...])
    ckv_ref[...] = ckv
    kv_ref[...] = jnp.dot(ckv.astype(BF16), wukv_ref[...], preferred_element_type=F32).astype(BF16)

    zkr = z[:, 1152:1280]
    zkr_ref[...] = zkr
    kr_ref[...] = (zkr * cos + _swap_halves(zkr, lane) * sin).astype(BF16)


def _pre_call(xc, xl, mods, g_pre, w_in, g_sgu, gmat, g_q, w_uq, g_kv, w_ukv, cos_t, sin_t):
    nb = N_TOK // TB
    first_lat = N_CTX // TB
    pos_blocks = LAT_T // TB

    def tok(width):
        return pl.BlockSpec((TB, width), lambda i: (i, 0))

    def full(shape):
        return pl.BlockSpec(shape, lambda i: (0,) * len(shape))

    def rope_map(i):
        return (jnp.where(i >= first_lat, (i - first_lat) % pos_blocks, 0), 0)

    out_shape = (
        jax.ShapeDtypeStruct((N_TOK, A_WIDTH), F32),
        jax.ShapeDtypeStruct((N_TOK, A_WIDTH), BF16),
        jax.ShapeDtypeStruct((N_TOK, F_WIDTH), BF16),
        jax.ShapeDtypeStruct((N_TOK, HEADS * HEAD_PAD), BF16),
        jax.ShapeDtypeStruct((N_TOK, HEADS * (NOPE + V_DIM)), BF16),
        jax.ShapeDtypeStruct((N_TOK, LANES), BF16),
        jax.ShapeDtypeStruct((N_TOK, KV_LORA), F32),
        jax.ShapeDtypeStruct((N_TOK, LANES), F32),
    )
    return pl.pallas_call(
        _pre_kernel,
        grid=(nb,),
        in_specs=[
            _ctx_spec(D), _lat_spec(D),
            pl.BlockSpec((None, 6, D), lambda i: (_mod_index(i), 0, 0)),
            full((1, D)), full((D, IN_PAD)), full((1, A_WIDTH)), full((A_WIDTH, A_WIDTH)),
            full((1, Q_LORA)), full((Q_LORA, HEADS * HEAD_PAD)),
            full((1, KV_LORA)), full((KV_LORA, HEADS * (NOPE + V_DIM))),
            pl.BlockSpec((TB, LANES), rope_map), pl.BlockSpec((TB, LANES), rope_map),
        ],
        out_specs=(tok(A_WIDTH), tok(A_WIDTH), tok(F_WIDTH), tok(HEADS * HEAD_PAD),
                   tok(HEADS * (NOPE + V_DIM)), tok(LANES), tok(KV_LORA), tok(LANES)),
        out_shape=out_shape,
        compiler_params=pltpu.CompilerParams(dimension_semantics=("parallel",),
                                             vmem_limit_bytes=VMEM_LIMIT),
        name="pre_mix",
    )(xc, xl, mods, g_pre, w_in, g_sgu, gmat, g_q, w_uq, g_kv, w_ukv, cos_t, sin_t)


def _mix_kernel(*refs, has_cache):
    if has_cache:
        (u_ref, vn_ref, zf_ref, q_ref, kv_ref, kr_ref, ws_ref, bs_ref, bdc_ref, bds_ref,
         ct_ref, st_ref, cckv_ref, ckr_ref, wukv_ref, o_ref) = refs
    else:
        (u_ref, vn_ref, zf_ref, q_ref, kv_ref, kr_ref, ws_ref, bs_ref, bdc_ref, bds_ref,
         ct_ref, st_ref, o_ref) = refs

    lane = lax.broadcasted_iota(jnp.int32, (CHUNK, A_WIDTH), 1)
    for c in range(QB // CHUNK):
        rows = slice(c * CHUNK, (c + 1) * CHUNK)
        vch = vn_ref[rows, :]
        s = bs_ref[...]
        for g in range(A_HEADS):
            sg = jnp.dot(ws_ref[g], vch, preferred_element_type=F32)
            in_head = (lane >= g * A_HEAD_DIM) & (lane < (g + 1) * A_HEAD_DIM)
            s = s + jnp.where(in_head, sg, 0.0)
        o_ref[rows, 0:A_WIDTH] = (u_ref[rows, :] * s).astype(BF16)

    zf = zf_ref[...]
    zc = jnp.dot(zf, bdc_ref[...], preferred_element_type=F32).astype(BF16)
    zs = jnp.dot(zf, bds_ref[...], preferred_element_type=F32).astype(BF16)
    yf = (jnp.dot(ct_ref[...], zc, preferred_element_type=F32)
          - jnp.dot(st_ref[...], zs, preferred_element_type=F32))
    o_ref[:, A_WIDTH:A_WIDTH + F_WIDTH] = yf.astype(BF16)

    kr = kr_ref[...]
    if has_cache:
        kvc = jnp.dot(cckv_ref[...].astype(BF16), wukv_ref[...],
                      preferred_element_type=F32).astype(BF16)
        krc = ckr_ref[...]
    for hd in range(HEADS):
        qh = q_ref[:, hd * HEAD_PAD:(hd + 1) * HEAD_PAD]
        kb = hd * (NOPE + V_DIM)
        kh = jnp.concatenate([kv_ref[:, kb:kb + NOPE], kr], axis=1)
        vh = kv_ref[:, kb + NOPE:kb + NOPE + V_DIM]
        s = _dot_nt(qh, kh)
        m = jnp.max(s, axis=-1, keepdims=True)
        if has_cache:
            khc = jnp.concatenate([kvc[:, kb:kb + NOPE], krc], axis=1)
            vhc = kvc[:, kb + NOPE:kb + NOPE + V_DIM]
            sc = _dot_nt(qh, khc)
            m = jnp.maximum(m, jnp.max(sc, axis=-1, keepdims=True))
        e = jnp.exp(s - m)
        den = jnp.sum(e, axis=-1, keepdims=True)
        o = jnp.dot(e.astype(BF16), vh, preferred_element_type=F32)
        if has_cache:
            ec = jnp.exp(sc - m)
            den = den + jnp.sum(ec, axis=-1, keepdims=True)
            o = o + jnp.dot(ec.astype(BF16), vhc, preferred_element_type=F32)
        ob = A_WIDTH + F_WIDTH + hd * V_DIM
        o_ref[:, ob:ob + V_DIM] = (o * (1.0 / den)).astype(BF16)


def _mix_call(pre, consts, seq_t, n_batch, tok_off, cache=None):
    u, vn, zf, q, kv, kr = pre
    ws, bs_full, bdc, bds, ct, st = consts
    nq = seq_t // QB
    qoff = tok_off // QB
    soff = tok_off // seq_t

    def qrow(width):
        return pl.BlockSpec((QB, width), lambda b, j: (qoff + b * nq + j, 0))

    def srow(width):
        return pl.BlockSpec((seq_t, width), lambda b, j: (soff + b, 0))

    def full(shape):
        return pl.BlockSpec(shape, lambda b, j: (0,) * len(shape))

    in_specs = [qrow(A_WIDTH), qrow(A_WIDTH), srow(F_WIDTH), qrow(HEADS * HEAD_PAD),
                srow(HEADS * (NOPE + V_DIM)), srow(LANES),
                full((A_HEADS, CHUNK, CHUNK)), full((CHUNK, A_WIDTH)),
                full((F_WIDTH, F_WIDTH)), full((F_WIDTH, F_WIDTH)),
                pl.BlockSpec((QB, seq_t), lambda b, j: (j, 0)),
                pl.BlockSpec((QB, seq_t), lambda b, j: (j, 0))]
    args = [u, vn, zf, q, kv, kr, ws, bs_full, bdc, bds, ct, st]
    if cache is not None:
        cckv, ckr, wukv = cache
        in_specs += [pl.BlockSpec((None, PAST, KV_LORA), lambda b, j: (b, 0, 0)),
                     pl.BlockSpec((None, PAST, LANES), lambda b, j: (b, 0, 0)),
                     full((KV_LORA, HEADS * (NOPE + V_DIM)))]
        args += [cckv, ckr, wukv]
    return pl.pallas_call(
        functools.partial(_mix_kernel, has_cache=cache is not None),
        grid=(n_batch, nq),
        in_specs=in_specs,
        out_specs=pl.BlockSpec((QB, D), lambda b, j: (b * nq + j, 0)),
        out_shape=jax.ShapeDtypeStruct((n_batch * seq_t, D), BF16),
        compiler_params=pltpu.CompilerParams(dimension_semantics=("parallel", "parallel"),
                                             vmem_limit_bytes=VMEM_LIMIT),
        name="mix_lat" if cache is not None else "mix_ctx",
    )(*args)


def _post_kernel(yc_ref, yl_ref, xc_ref, xl_ref, mod_ref, wout_ref, gpost_ref, gffn_ref, wr_ref, br_ref,
                 tri_ref, x1_ref, h2_ref, tab_ref, rw_ref, cnt_ref, carry_ref):
    i = pl.program_id(0)

    @pl.when(i == 0)
    def _():
        carry_ref[...] = jnp.zeros(carry_ref.shape, F32)

    y = jnp.dot(_pick(i, yc_ref, yl_ref), wout_ref[...], preferred_element_type=F32)
    x1 = _pick(i, xc_ref, xl_ref) + mod_ref[2:3, :] * _rms(y, gpost_ref[...])
    x1_ref[...] = x1
    h2 = _rms(x1, gffn_ref[...]) * (1.0 + mod_ref[4:5, :]) + mod_ref[3:4, :]
    h2_ref[...] = h2

    wr = wr_ref[...]
    wr_hi = wr.astype(BF16)
    wr_lo = (wr - wr_hi.astype(F32)).astype(BF16)
    h_hi = h2.astype(BF16)
    h_lo = (h2 - h_hi.astype(F32)).astype(BF16)
    logits = (jnp.dot(h_hi, wr_hi, preferred_element_type=F32)
              + jnp.dot(h_lo, wr_hi, preferred_element_type=F32)
              + jnp.dot(h_hi, wr_lo, preferred_element_type=F32)) + br_ref[...]

    lane = lax.broadcasted_iota(jnp.int32, (TB, LANES), 1)
    lane_f = lane.astype(F32)
    work = logits
    idx, val = [], []
    for _ in range(TOP_K):
        m = jnp.max(work, axis=-1, keepdims=True)
        ik = jnp.min(jnp.where(work == m, lane_f, float(LANES)), axis=-1, keepdims=True)
        idx.append(ik)
        val.append(m)
        work = jnp.where(lane_f == ik, NEG, work)
    ex = [jnp.exp(v - val[0]) for v in val]
    den = ex[0] + ex[1] + ex[2] + ex[3]

    onehot = jnp.zeros((TB, LANES), F32)
    for k in range(TOP_K):
        onehot = onehot + jnp.where(lane_f == idx[k], 1.0, 0.0)
    before = jnp.dot(tri_ref[...], onehot.astype(BF16), preferred_element_type=F32) + carry_ref[0:1, :]
    ri = jnp.zeros((TB, LANES), F32)
    rw = jnp.zeros((TB, LANES), F32)
    for k in range(TOP_K):
        rank_k = jnp.sum(jnp.where(lane_f == idx[k], before, 0.0), axis=-1, keepdims=True)
        ri = ri + jnp.where(lane == k, idx[k], 0.0) + jnp.where(lane == TOP_K + k, rank_k, 0.0)
        rw = rw + jnp.where(lane == k, ex[k] / den, 0.0)
    tab_ref[...] = ri.T[0:2 * TOP_K, :].astype(jnp.int32)
    rw_ref[...] = rw
    total = carry_ref[0:1, :] + jnp.sum(onehot, axis=0, keepdims=True)
    carry_ref[...] = jnp.broadcast_to(total, carry_ref.shape)
    cnt_ref[...] = jnp.broadcast_to(total, cnt_ref.shape).astype(jnp.int32)


def _post_call(yc, yl, xc, xl, mods, w_out, g_post, g_ffn, w_r, b_r, tri):
    nb = N_TOK // TB

    def tok(width):
        return pl.BlockSpec((TB, width), lambda i: (i, 0))

    def full(shape):
        return pl.BlockSpec(shape, lambda i: (0,) * len(shape))

    return pl.pallas_call(
        _post_kernel,
        grid=(nb,),
        in_specs=[_ctx_spec(D), _lat_spec(D), _ctx_spec(D), _lat_spec(D),
                  pl.BlockSpec((None, 6, D), lambda i: (_mod_index(i), 0, 0)),
                  full((D, D)), full((1, D)), full((1, D)), full((D, LANES)), full((1, LANES)),
                  full((TB, TB))],
        out_specs=(tok(D), tok(D), pl.BlockSpec((None, 2 * TOP_K, TB), lambda i: (i, 0, 0)),
                   tok(LANES), full((8, LANES))),
        out_shape=(jax.ShapeDtypeStruct((N_TOK, D), F32),
                   jax.ShapeDtypeStruct((N_TOK, D), F32),
                   jax.ShapeDtypeStruct((nb, 2 * TOP_K, TB), jnp.int32),
                   jax.ShapeDtypeStruct((N_TOK, LANES), F32),
                   jax.ShapeDtypeStruct((8, LANES), jnp.int32)),
        scratch_shapes=[pltpu.VMEM((8, LANES), F32)],
        compiler_params=pltpu.CompilerParams(dimension_semantics=("arbitrary",),
                                             vmem_limit_bytes=VMEM_LIMIT),
        name="post_mix_router",
    )(yc, yl, xc, xl, mods, w_out, g_post, g_ffn, w_r, b_r, tri)


def _dispatch_kernel(zrow_ref, nu_ref, bs_ref, tab_ref, h2_ref, xs_ref, zero_ref, sem, zsem):
    i = pl.program_id(0)

    def zero_fill(row):
        return pltpu.make_async_copy(zero_ref, xs_ref.at[pl.ds(pl.multiple_of(row, 8), RB)], zsem)

    @pl.when(i == 0)
    def _():
        zero_ref[...] = jnp.zeros(zero_ref.shape, F32)
        for e in range(N_EXPERTS):
            zero_fill(zrow_ref[e]).start()
        for e in range(N_EXPERTS):
            zero_fill(zrow_ref[e]).wait()

        def tail_start(b, carry):
            zero_fill(b * RB).start()
            return carry

        def tail_wait(b, carry):
            zero_fill(b * RB).wait()
            return carry

        lax.fori_loop(nu_ref[0], N_BLOCKS, tail_start, 0)
        lax.fori_loop(nu_ref[0], N_BLOCKS, tail_wait, 0)

    def body(t, carry):
        for k in range(TOP_K):
            row = bs_ref[tab_ref[k, t]] * RB + tab_ref[TOP_K + k, t]
            pltpu.make_async_copy(h2_ref.at[pl.ds(t, 1)], xs_ref.at[pl.ds(row, 1)],
                                  sem).start(priority=k % 2)
        return carry

    lax.fori_loop(0, TB, body, 0, unroll=8)
    for k in range(TOP_K):
        pltpu.make_async_copy(h2_ref, xs_ref.at[pl.ds(0, TB)], sem).wait()


def _dispatch_call(zrow, n_used, blk_start, tab, h2):
    nb = N_TOK // TB
    grid_spec = pltpu.PrefetchScalarGridSpec(
        num_scalar_prefetch=3,
        grid=(nb,),
        in_specs=[
            pl.BlockSpec((None, 2 * TOP_K, TB), lambda i, z, n, s: (i, 0, 0), memory_space=pltpu.SMEM),
            pl.BlockSpec((TB, D), lambda i, z, n, s: (i, 0)),
        ],
        out_specs=pl.BlockSpec(memory_space=pl.ANY),
        scratch_shapes=[pltpu.VMEM((RB, D), F32), pltpu.SemaphoreType.DMA, pltpu.SemaphoreType.DMA],
    )
    return pl.pallas_call(
        _dispatch_kernel,
        grid_spec=grid_spec,
        out_shape=jax.ShapeDtypeStruct((N_ROWS, D), F32),
        compiler_params=pltpu.CompilerParams(dimension_semantics=("arbitrary",),
                                             vmem_limit_bytes=VMEM_LIMIT),
        name="dispatch_rows",
    )(zrow, n_used, blk_start, tab, h2)


def _moe_kernel(be_ref, first_ref, nxt_ref, slot_ref, nu_ref, xs_ref, wgu_hbm, bgu_ref, wdn_hbm, bdn_ref,
                o_ref, wgu_f, wdn_f, wgu_s, wdn_s, sems, *, layer):
    b = pl.program_id(0)
    used = b < nu_ref[0]

    def fetch(e, s):
        return (pltpu.make_async_copy(wgu_hbm.at[layer, e], wgu_f.at[s], sems.at[0, s]),
                pltpu.make_async_copy(wdn_hbm.at[layer, e], wdn_f.at[s], sems.at[1, s]))

    @pl.when(b == 0)
    def _():
        for cp in fetch(be_ref[0], 0):
            cp.start()

    @pl.when(first_ref[b] == 1)
    def _():
        s = slot_ref[b]
        for cp in fetch(be_ref[b], s):
            cp.wait()

        @pl.when(nxt_ref[b] >= 0)
        def _():
            for cp in fetch(nxt_ref[b], 1 - s):
                cp.start()

        step = 128

        def body(c, carry):
            r = pl.multiple_of(c * step, step)
            wgu_s[pl.ds(r, step), :] = wgu_f[s, pl.ds(r, step), :].astype(BF16)
            wdn_s[pl.ds(r, step), :] = wdn_f[s, pl.ds(r, step), :].astype(BF16)
            return carry

        lax.fori_loop(0, D // step, body, 0)

    @pl.when(used)
    def _():
        gu = jnp.dot(xs_ref[...].astype(BF16), wgu_s[...], preferred_element_type=F32) + bgu_ref[...]
        g = jnp.minimum(gu[:, :D_FF], SWIGLU_LIMIT)
        l = jnp.clip(gu[:, D_FF:], -SWIGLU_LIMIT, SWIGLU_LIMIT)
        a = g * jax.nn.sigmoid(SWIGLU_ALPHA * g) * (l + 1.0)
        o_ref[...] = jnp.dot(a.astype(BF16), wdn_s[...], preferred_element_type=F32) + bdn_ref[...]

    @pl.when(jnp.logical_not(used))
    def _():
        o_ref[...] = jnp.zeros(o_ref.shape, F32)


def _moe_call(layer, tables, xs, w_gu, b_gu, w_dn, b_dn):
    def rows_in(b, be, fi, nx, sl, nu):
        return (jnp.minimum(b, nu[0] - 1), 0)

    def rows_out(b, be, fi, nx, sl, nu):
        return (b, 0)

    def expert(b, be, fi, nx, sl, nu):
        return (layer, be[b], 0, 0)

    grid_spec = pltpu.PrefetchScalarGridSpec(
        num_scalar_prefetch=5,
        grid=(N_BLOCKS,),
        in_specs=[
            pl.BlockSpec((RB, D), rows_in),
            pl.BlockSpec(memory_space=pl.ANY),
            pl.BlockSpec((None, None, 1, 2 * D_FF), expert),
            pl.BlockSpec(memory_space=pl.ANY),
            pl.BlockSpec((None, None, 1, D), expert),
        ],
        out_specs=pl.BlockSpec((RB, D), rows_out),
        scratch_shapes=[pltpu.VMEM((2, D, 2 * D_FF), F32), pltpu.VMEM((2, D_FF, D), F32),
                        pltpu.VMEM((D, 2 * D_FF), BF16), pltpu.VMEM((D_FF, D), BF16),
                        pltpu.SemaphoreType.DMA((2, 2))],
    )
    return pl.pallas_call(
        functools.partial(_moe_kernel, layer=layer),
        grid_spec=grid_spec,
        out_shape=jax.ShapeDtypeStruct((N_ROWS, D), F32),
        compiler_params=pltpu.CompilerParams(dimension_semantics=("arbitrary",),
                                             vmem_limit_bytes=VMEM_LIMIT),
        name="moe_experts",
    )(*tables, xs, w_gu, b_gu, w_dn, b_dn)


def _final_kernel(x1_ref, ys_ref, rw_ref, mod_ref, g_ref, oc_ref, ol_ref):
    i = pl.program_id(0)
    rw = rw_ref[...]
    y = ys_ref[0] * rw[:, 0:1]
    for k in range(1, TOP_K):
        y = y + ys_ref[k] * rw[:, k:k + 1]
    x2 = x1_ref[...] + mod_ref[5:6, :] * _rms(y, g_ref[...])

    @pl.when(i < N_CTX // TB)
    def _():
        oc_ref[...] = x2

    @pl.when(i >= N_CTX // TB)
    def _():
        ol_ref[...] = x2


def _final_call(x1, ysg, rw, mods, g_post_ffn):
    nb = N_TOK // TB
    return pl.pallas_call(
        _final_kernel,
        grid=(nb,),
        in_specs=[pl.BlockSpec((TB, D), lambda i: (i, 0)),
                  pl.BlockSpec((TOP_K, TB, D), lambda i: (0, i, 0)),
                  pl.BlockSpec((TB, LANES), lambda i: (i, 0)),
                  pl.BlockSpec((None, 6, D), lambda i: (_mod_index(i), 0, 0)),
                  pl.BlockSpec((1, D), lambda i: (0, 0))],
        out_specs=(_ctx_spec(D), _lat_spec(D)),
        out_shape=(jax.ShapeDtypeStruct((N_CTX, D), F32), jax.ShapeDtypeStruct((N_LAT, D), F32)),
        compiler_params=pltpu.CompilerParams(dimension_semantics=("arbitrary",),
                                             vmem_limit_bytes=VMEM_LIMIT),
        name="combine_residual",
    )(x1, ysg, rw, mods, g_post_ffn)


def _dft_tables(t):
    j = np.arange(t, dtype=np.int64)
    ang = 2.0 * np.pi * ((j[:, None] * j[None, :]) % t) / t
    return (np.cos(ang) / math.sqrt(t)).astype(np.float32), (np.sin(ang) / math.sqrt(t)).astype(np.float32)


def _channel_dft():
    c = np.arange(F_GROUP_DIM, dtype=np.int64)
    ang = 2.0 * np.pi * ((c[:, None] * c[None, :]) % F_GROUP_DIM) / F_GROUP_DIM
    eye = np.eye(F_GROUPS)
    bdc = np.kron(eye, np.cos(ang)) / math.sqrt(F_GROUP_DIM)
    bds = np.kron(eye, np.sin(ang)) / math.sqrt(F_GROUP_DIM)
    return bdc.astype(np.float32), bds.astype(np.float32)


def _rope_tables():
    pos = np.arange(LAT_T)
    n = ROPE // 4
    inv_freq = np.power(np.float32(ROPE_BASE), -np.arange(n, dtype=np.float32) / np.float32(n))
    ang_r = (pos // GRID_W).astype(np.float32)[:, None] * inv_freq
    ang_c = (pos % GRID_W).astype(np.float32)[:, None] * inv_freq
    cos = np.concatenate([np.cos(ang_r), np.cos(ang_r), np.cos(ang_c), np.cos(ang_c),
                          np.ones((LAT_T, LANES - ROPE))], axis=1)
    sin = np.concatenate([-np.sin(ang_r), np.sin(ang_r), -np.sin(ang_c), np.sin(ang_c),
                          np.zeros((LAT_T, LANES - ROPE))], axis=1)
    return cos.astype(np.float32), sin.astype(np.float32)


def kernel(x_prompt, x_sample, cache_ckv, cache_krope, c, c_ctx, w_ada, b_ada, g_pre_mix, g_post_mix, g_pre_ffn, g_post_ffn, w_in, g_sgu, w_spatial, b_spatial, g_q, w_uq, g_kv, w_ukv, w_out, w_router, b_router, w_gate_up, b_gate_up, w_down, b_down):
    xc, xl = x_prompt.reshape(N_CTX, D), x_sample.reshape(N_LAT, D)

    cond = jnp.concatenate([c_ctx[None, :], c, jnp.zeros((5, D), F32)], axis=0)
    mods = _mod_call(cond.T, w_ada, b_ada.reshape(DEPTH, 1, 6 * D))
    mods = mods[:, :3].reshape(DEPTH, 3, 6, D)

    gmat = jnp.asarray(np.kron(np.eye(A_HEADS), np.full((A_HEAD_DIM, A_HEAD_DIM), 1.0 / A_HEAD_DIM)),
                       dtype=BF16)
    bdc_np, bds_np = _channel_dft()
    bdc, bds = jnp.asarray(bdc_np).astype(BF16), jnp.asarray(bds_np).astype(BF16)
    dft = {}
    for t in (CTX_T, LAT_T):
        ct_np, st_np = _dft_tables(t)
        dft[t] = (jnp.asarray(ct_np).astype(BF16), jnp.asarray(st_np).astype(BF16))
    cos_np, sin_np = _rope_tables()
    cos_t, sin_t = jnp.asarray(cos_np), jnp.asarray(sin_np)
    tri = jnp.asarray(np.tril(np.ones((TB, TB), np.float32), k=-1)).astype(BF16)

    b_gu = b_gate_up.reshape(DEPTH, N_EXPERTS, 1, 2 * D_FF)
    b_dn = b_down.reshape(DEPTH, N_EXPERTS, 1, D)

    ckv_layers, krope_layers = [], []
    for i in range(DEPTH):
        w_in_p = jnp.pad(w_in[i], ((0, 0), (0, IN_PAD - w_in.shape[-1]))).astype(BF16)
        w_uq_p = jnp.pad(w_uq[i].reshape(Q_LORA, HEADS, NOPE + ROPE),
                         ((0, 0), (0, 0), (0, HEAD_PAD - NOPE - ROPE))).reshape(Q_LORA, HEADS * HEAD_PAD)
        w_uq_p = w_uq_p.astype(BF16)
        w_ukv_b = w_ukv[i].astype(BF16)
        pre = _pre_call(xc, xl, mods[i], g_pre_mix[i][None, :], w_in_p, g_sgu[i][None, :], gmat,
                        g_q[i][None, :], w_uq_p, g_kv[i][None, :], w_ukv_b, cos_t, sin_t)
        u, vn, zf, q, kv, kr, ckv, zkr = pre
        ckv_layers.append(ckv[:N_CTX].reshape(N_CTX_B, CTX_T, KV_LORA))
        krope_layers.append(zkr[:N_CTX, :ROPE].reshape(N_CTX_B, CTX_T, ROPE))

        ws = w_spatial[i].astype(BF16)
        bs_full = jnp.repeat(b_spatial[i].T, A_HEAD_DIM, axis=1)
        mix_in = (u, vn, zf, q, kv, kr)
        y_ctx = _mix_call(mix_in, (ws, bs_full, bdc, bds) + dft[CTX_T], CTX_T, N_CTX_B, 0)
        ckr_p = jnp.pad(cache_krope[:, i], ((0, 0), (0, 0), (0, LANES - ROPE))).astype(BF16)
        y_lat = _mix_call(mix_in, (ws, bs_full, bdc, bds) + dft[LAT_T], LAT_T, N_LAT_B, N_CTX,
                          cache=(cache_ckv[:, i], ckr_p, w_ukv_b))

        w_r = jnp.pad(w_router[i], ((0, 0), (0, LANES - N_EXPERTS)))
        b_r = jnp.pad(b_router[i], (0, LANES - N_EXPERTS), constant_values=NEG)[None, :]
        x1, h2, tab, rw, cnt = _post_call(y_ctx, y_lat, xc, xl, mods[i], w_out[i].astype(BF16),
                                          g_post_mix[i][None, :], g_pre_ffn[i][None, :], w_r, b_r, tri)

        counts = cnt[0, :N_EXPERTS]
        nblk = (counts + RB - 1) // RB
        blk_end = jnp.cumsum(nblk)
        blk_start = blk_end - nblk
        n_used = blk_end[-1:]
        zrow = blk_start * RB + counts // 8 * 8
        xs = _dispatch_call(zrow, n_used, blk_start, tab, h2)
        blk_id = jnp.arange(N_BLOCKS, dtype=jnp.int32)
        blk = jnp.minimum(blk_id, n_used - 1)
        block_e = jnp.sum((blk_end[None, :] <= blk[:, None]).astype(jnp.int32), axis=1)
        block_e = jnp.minimum(block_e, N_EXPERTS - 1)
        eid = jnp.arange(N_EXPERTS, dtype=jnp.int32)
        owns = nblk > 0
        later = jnp.where((eid[None, :] > eid[:, None]) & owns[None, :], eid[None, :], N_EXPERTS)
        nxt_e = jnp.min(later, axis=1)
        nxt_e = jnp.where(nxt_e == N_EXPERTS, -1, nxt_e)
        run_id = jnp.cumsum(owns.astype(jnp.int32)) - 1
        first = ((blk_id == blk_start[block_e]) & (blk_id < n_used)).astype(jnp.int32)
        tables = (block_e, first, nxt_e[block_e], run_id[block_e] % 2, n_used)
        ys = _moe_call(i, tables, xs, w_gate_up, b_gu, w_down, b_dn)
        tab_k = tab.transpose(1, 0, 2).reshape(2 * TOP_K, N_TOK)
        dest = blk_start[tab_k[:TOP_K]] * RB + tab_k[TOP_K:]
        ysg = ys[dest.reshape(-1)].reshape(TOP_K, N_TOK, D)
        xc, xl = _final_call(x1, ysg, rw, mods[i], g_post_ffn[i][None, :])

    y_prompt = xc.reshape(N_CTX_B, CTX_T, D)
    y_sample = xl.reshape(N_LAT_B, LAT_T, D)
    return (y_prompt, y_sample, jnp.stack(ckv_layers, axis=1), jnp.stack(krope_layers, axis=1))
```

```python
import functools
import math

import jax
import jax.numpy as jnp
import numpy as np
from jax import lax
from jax.experimental import pallas as pl
from jax.experimental.pallas import tpu as pltpu

F32 = jnp.float32
BF16 = jnp.bfloat16

D = 1024
N_CTX_B, CTX_T = 16, 256
N_LAT_B, LAT_T = 2, 1024
PAST = 512
N_CTX = N_CTX_B * CTX_T
N_LAT = N_LAT_B * LAT_T
N_TOK = N_CTX + N_LAT
DEPTH = 2
GRID_W = 64
EPS = 1e-6
A_HEADS, A_HEAD_DIM, A_WIDTH, CHUNK = 4, 64, 256, 128
F_GROUPS, F_GROUP_DIM, F_WIDTH = 4, 64, 256
HEADS, Q_LORA, KV_LORA, NOPE, ROPE, V_DIM = 4, 256, 128, 128, 64, 128
HEAD_PAD = 256
IN_PAD = 1280
N_EXPERTS, TOP_K, D_FF = 32, 4, 1024
SWIGLU_LIMIT, SWIGLU_ALPHA = 7.0, 1.702
ROPE_BASE = 10000.0

TB = 256
QB = 256
RB = 256
LANES = 128
N_BLOCKS = N_TOK * TOP_K // RB + N_EXPERTS
N_ROWS = N_BLOCKS * RB
NEG = -3.0e38
VMEM_LIMIT = 56 * 1024 * 1024


def _rms(x, g):
    return x * lax.rsqrt(jnp.mean(x * x, axis=-1, keepdims=True) + EPS) * g


def _split_dot(v, m):
    hi = v.astype(BF16)
    lo = (v - hi.astype(F32)).astype(BF16)
    return (jnp.dot(hi, m, preferred_element_type=F32)
            + jnp.dot(lo, m, preferred_element_type=F32))


def _dot_nt(a, b):
    return lax.dot_general(a, b, (((1,), (1,)), ((), ())), preferred_element_type=F32)


def _mod_kernel(ct_ref, w_ref, b_ref, o_ref):
    ct = ct_ref[...]
    s = ct * jax.nn.sigmoid(ct)
    w = w_ref[...]
    o_ref[...] = jnp.zeros(o_ref.shape, F32)
    for r in range(3):
        o_ref[r:r + 1, :] = jnp.sum(w * s[:, r:r + 1], axis=0, keepdims=True) + b_ref[...]


def _mod_call(cond_t, w_ada, b_ada):
    cb = 512
    return pl.pallas_call(
        _mod_kernel,
        grid=(DEPTH, 6 * D // cb),
        in_specs=[
            pl.BlockSpec((D, 8), lambda l, j: (0, 0)),
            pl.BlockSpec((None, D, cb), lambda l, j: (l, 0, j)),
            pl.BlockSpec((None, 1, cb), lambda l, j: (l, 0, j)),
        ],
        out_specs=pl.BlockSpec((None, 8, cb), lambda l, j: (l, 0, j)),
        out_shape=jax.ShapeDtypeStruct((DEPTH, 8, 6 * D), F32),
        compiler_params=pltpu.CompilerParams(dimension_semantics=("parallel", "parallel")),
        name="modulation",
    )(cond_t, w_ada, b_ada)


def _mod_index(i):
    first_lat = N_CTX // TB
    return jnp.where(i < first_lat, 0, 1 + (i - first_lat) // (LAT_T // TB))


def _ctx_spec(width):
    return pl.BlockSpec((TB, width), lambda i: (jnp.minimum(i, N_CTX // TB - 1), 0))


def _lat_spec(width):
    return pl.BlockSpec((TB, width), lambda i: (jnp.maximum(i - N_CTX // TB, 0), 0))


def _pick(i, ctx_ref, lat_ref):
    return jnp.where(i >= N_CTX // TB, lat_ref[...], ctx_ref[...])


def _swap_halves(x, lane):
    w = x.shape[-1]
    fwd = pltpu.roll(x, w - 16, 1)
    bwd = pltpu.roll(x, 16, 1)
    return jnp.where((lane & 31) < 16, fwd, bwd)


def _pre_kernel(xc_ref, xl_ref, mod_ref, gpre_ref, win_ref, gsgu_ref, gmat_ref, gq_ref, wuq_ref,
                gkv_ref, wukv_ref, cos_ref, sin_ref,
                u_ref, vn_ref, zf_ref, q_ref, kv_ref, kr_ref, ckv_ref, zkr_ref):
    i = pl.program_id(0)
    is_lat = i >= N_CTX // TB
    x = _pick(i, xc_ref, xl_ref)
    h = _rms(x, gpre_ref[...]) * (1.0 + mod_ref[1:2, :]) + mod_ref[0:1, :]
    z = jnp.dot(h.astype(BF16), win_ref[...], preferred_element_type=F32)

    ga = jax.nn.gelu(z[:, :2 * A_WIDTH])
    u_ref[...] = ga[:, :A_WIDTH]
    v = ga[:, A_WIDTH:]
    gmat = gmat_ref[...]
    dv = v - _split_dot(v, gmat)
    var = _split_dot(dv * dv, gmat)
    vn_ref[...] = (dv * lax.rsqrt(var + EPS) * gsgu_ref[...]).astype(BF16)

    zf_ref[...] = z[:, 512:768].astype(BF16)

    cos = jnp.where(is_lat, cos_ref[...], 1.0)
    sin = jnp.where(is_lat, sin_ref[...], 0.0)
    lane = lax.broadcasted_iota(jnp.int32, (TB, LANES), 1)

    qn = _rms(z[:, 768:1024], gq_ref[...])
    scale = (NOPE + ROPE) ** -0.5
    q = jnp.dot(qn.astype(BF16), wuq_ref[...], preferred_element_type=F32) * scale
    for hd in range(HEADS):
        base = hd * HEAD_PAD
        q_ref[:, base:base + NOPE] = q[:, base:base + NOPE].astype(BF16)
        qr = q[:, base + NOPE:base + HEAD_PAD]
        q_ref[:, base + NOPE:base + HEAD_PAD] = (qr * cos + _swap_halves(qr, lane) * sin).astype(BF16)

    ckv = _rms(z[:, 1024:1152], gkv_ref[...])
    ckv_ref[...] = ckv
    kv_ref[...] = jnp.dot(ckv.astype(BF16), wukv_ref[...], preferred_element_type=F32).astype(BF16)

    zkr = z[:, 1152:1280]
    zkr_ref[...] = zkr
    kr_ref[...] = (zkr * cos + _swap_halves(zkr, lane) * sin).astype(BF16)


def _pre_call(xc, xl, mods, g_pre, w_in, g_sgu, gmat, g_q, w_uq, g_kv, w_ukv, cos_t, sin_t):
    nb = N_TOK // TB
    first_lat = N_CTX // TB
    pos_blocks = LAT_T // TB

    def tok(width):
        return pl.BlockSpec((TB, width), lambda i: (i, 0))

    def full(shape):
        return pl.BlockSpec(shape, lambda i: (0,) * len(shape))

    def rope_map(i):
        return (jnp.where(i >= first_lat, (i - first_lat) % pos_blocks, 0), 0)

    out_shape = (
        jax.ShapeDtypeStruct((N_TOK, A_WIDTH), F32),
        jax.ShapeDtypeStruct((N_TOK, A_WIDTH), BF16),
        jax.ShapeDtypeStruct((N_TOK, F_WIDTH), BF16),
        jax.ShapeDtypeStruct((N_TOK, HEADS * HEAD_PAD), BF16),
        jax.ShapeDtypeStruct((N_TOK, HEADS * (NOPE + V_DIM)), BF16),
        jax.ShapeDtypeStruct((N_TOK, LANES), BF16),
        jax.ShapeDtypeStruct((N_TOK, KV_LORA), F32),
        jax.ShapeDtypeStruct((N_TOK, LANES), F32),
    )
    return pl.pallas_call(
        _pre_kernel,
        grid=(nb,),
        in_specs=[
            _ctx_spec(D), _lat_spec(D),
            pl.BlockSpec((None, 6, D), lambda i: (_mod_index(i), 0, 0)),
            full((1, D)), full((D, IN_PAD)), full((1, A_WIDTH)), full((A_WIDTH, A_WIDTH)),
            full((1, Q_LORA)), full((Q_LORA, HEADS * HEAD_PAD)),
            full((1, KV_LORA)), full((KV_LORA, HEADS * (NOPE + V_DIM))),
            pl.BlockSpec((TB, LANES), rope_map), pl.BlockSpec((TB, LANES), rope_map),
        ],
        out_specs=(tok(A_WIDTH), tok(A_WIDTH), tok(F_WIDTH), tok(HEADS * HEAD_PAD),
                   tok(HEADS * (NOPE + V_DIM)), tok(LANES), tok(KV_LORA), tok(LANES)),
        out_shape=out_shape,
        compiler_params=pltpu.CompilerParams(dimension_semantics=("parallel",),
                                             vmem_limit_bytes=VMEM_LIMIT),
        name="pre_mix",
    )(xc, xl, mods, g_pre, w_in, g_sgu, gmat, g_q, w_uq, g_kv, w_ukv, cos_t, sin_t)


def _mix_kernel(*refs, has_cache):
    if has_cache:
        (u_ref, vn_ref, zf_ref, q_ref, kv_ref, kr_ref, ws_ref, bs_ref, bdc_ref, bds_ref,
         ct_ref, st_ref, cckv_ref, ckr_ref, wukv_ref, o_ref) = refs
    else:
        (u_ref, vn_ref, zf_ref, q_ref, kv_ref, kr_ref, ws_ref, bs_ref, bdc_ref, bds_ref,
         ct_ref, st_ref, o_ref) = refs

    lane = lax.broadcasted_iota(jnp.int32, (CHUNK, A_WIDTH), 1)
    for c in range(QB // CHUNK):
        rows = slice(c * CHUNK, (c + 1) * CHUNK)
        vch = vn_ref[rows, :]
        s = bs_ref[...]
        for g in range(A_HEADS):
            sg = jnp.dot(ws_ref[g], vch, preferred_element_type=F32)
            in_head = (lane >= g * A_HEAD_DIM) & (lane < (g + 1) * A_HEAD_DIM)
            s = s + jnp.where(in_head, sg, 0.0)
        o_ref[rows, 0:A_WIDTH] = (u_ref[rows, :] * s).astype(BF16)

    zf = zf_ref[...]
    zc = jnp.dot(zf, bdc_ref[...], preferred_element_type=F32).astype(BF16)
    zs = jnp.dot(zf, bds_ref[...], preferred_element_type=F32).astype(BF16)
    yf = (jnp.dot(ct_ref[...], zc, preferred_element_type=F32)
          - jnp.dot(st_ref[...], zs, preferred_element_type=F32))
    o_ref[:, A_WIDTH:A_WIDTH + F_WIDTH] = yf.astype(BF16)

    kr = kr_ref[...]
    if has_cache:
        kvc = jnp.dot(cckv_ref[...].astype(BF16), wukv_ref[...],
                      preferred_element_type=F32).astype(BF16)
        krc = ckr_ref[...]
    for hd in range(HEADS):
        qh = q_ref[:, hd * HEAD_PAD:(hd + 1) * HEAD_PAD]
        kb = hd * (NOPE + V_DIM)
        kh = jnp.concatenate([kv_ref[:, kb:kb + NOPE], kr], axis=1)
        vh = kv_ref[:, kb + NOPE:kb + NOPE + V_DIM]
        s = _dot_nt(qh, kh)
        m = jnp.max(s, axis=-1, keepdims=True)
        if has_cache:
            khc = jnp.concatenate([kvc[:, kb:kb + NOPE], krc], axis=1)
            vhc = kvc[:, kb + NOPE:kb + NOPE + V_DIM]
            sc = _dot_nt(qh, khc)
            m = jnp.maximum(m, jnp.max(sc, axis=-1, keepdims=True))
        e = jnp.exp(s - m)
        den = jnp.sum(e, axis=-1, keepdims=True)
        o = jnp.dot(e.astype(BF16), vh, preferred_element_type=F32)
        if has_cache:
            ec = jnp.exp(sc - m)
            den = den + jnp.sum(ec, axis=-1, keepdims=True)
            o = o + jnp.dot(ec.astype(BF16), vhc, preferred_element_type=F32)
        ob = A_WIDTH + F_WIDTH + hd * V_DIM
        o_ref[:, ob:ob + V_DIM] = (o * (1.0 / den)).astype(BF16)


def _mix_call(pre, consts, seq_t, n_batch, tok_off, cache=None):
    u, vn, zf, q, kv, kr = pre
    ws, bs_full, bdc, bds, ct, st = consts
    nq = seq_t // QB
    qoff = tok_off // QB
    soff = tok_off // seq_t

    def qrow(width):
        return pl.BlockSpec((QB, width), lambda b, j: (qoff + b * nq + j, 0))

    def srow(width):
        return pl.BlockSpec((seq_t, width), lambda b, j: (soff + b, 0))

    def full(shape):
        return pl.BlockSpec(shape, lambda b, j: (0,) * len(shape))

    in_specs = [qrow(A_WIDTH), qrow(A_WIDTH), srow(F_WIDTH), qrow(HEADS * HEAD_PAD),
                srow(HEADS * (NOPE + V_DIM)), srow(LANES),
                full((A_HEADS, CHUNK, CHUNK)), full((CHUNK, A_WIDTH)),
                full((F_WIDTH, F_WIDTH)), full((F_WIDTH, F_WIDTH)),
                pl.BlockSpec((QB, seq_t), lambda b, j: (j, 0)),
                pl.BlockSpec((QB, seq_t), lambda b, j: (j, 0))]
    args = [u, vn, zf, q, kv, kr, ws, bs_full, bdc, bds, ct, st]
    if cache is not None:
        cckv, ckr, wukv = cache
        in_specs += [pl.BlockSpec((None, PAST, KV_LORA), lambda b, j: (b, 0, 0)),
                     pl.BlockSpec((None, PAST, LANES), lambda b, j: (b, 0, 0)),
                     full((KV_LORA, HEADS * (NOPE + V_DIM)))]
        args += [cckv, ckr, wukv]
    return pl.pallas_call(
        functools.partial(_mix_kernel, has_cache=cache is not None),
        grid=(n_batch, nq),
        in_specs=in_specs,
        out_specs=pl.BlockSpec((QB, D), lambda b, j: (b * nq + j, 0)),
        out_shape=jax.ShapeDtypeStruct((n_batch * seq_t, D), BF16),
        compiler_params=pltpu.CompilerParams(dimension_semantics=("parallel", "parallel"),
                                             vmem_limit_bytes=VMEM_LIMIT),
        name="mix_lat" if cache is not None else "mix_ctx",
    )(*args)


def _post_kernel(yc_ref, yl_ref, xc_ref, xl_ref, mod_ref, wout_ref, gpost_ref, gffn_ref, wr_ref, br_ref,
                 tri_ref, x1_ref, h2_ref, tab_ref, rw_ref, cnt_ref, carry_ref):
    i = pl.program_id(0)

    @pl.when(i == 0)
    def _():
        carry_ref[...] = jnp.zeros(carry_ref.shape, F32)

    y = jnp.dot(_pick(i, yc_ref, yl_ref), wout_ref[...], preferred_element_type=F32)
    x1 = _pick(i, xc_ref, xl_ref) + mod_ref[2:3, :] * _rms(y, gpost_ref[...])
    x1_ref[...] = x1
    h2 = _rms(x1, gffn_ref[...]) * (1.0 + mod_ref[4:5, :]) + mod_ref[3:4, :]
    h2_ref[...] = h2

    wr = wr_ref[...]
    wr_hi = wr.astype(BF16)
    wr_lo = (wr - wr_hi.astype(F32)).astype(BF16)
    h_hi = h2.astype(BF16)
    h_lo = (h2 - h_hi.astype(F32)).astype(BF16)
    logits = (jnp.dot(h_hi, wr_hi, preferred_element_type=F32)
              + jnp.dot(h_lo, wr_hi, preferred_element_type=F32)
              + jnp.dot(h_hi, wr_lo, preferred_element_type=F32)) + br_ref[...]

    lane = lax.broadcasted_iota(jnp.int32, (TB, LANES), 1)
    lane_f = lane.astype(F32)
    work = logits
    idx, val = [], []
    for _ in range(TOP_K):
        m = jnp.max(work, axis=-1, keepdims=True)
        ik = jnp.min(jnp.where(work == m, lane_f, float(LANES)), axis=-1, keepdims=True)
        idx.append(ik)
        val.append(m)
        work = jnp.where(lane_f == ik, NEG, work)
    ex = [jnp.exp(v - val[0]) for v in val]
    den = ex[0] + ex[1] + ex[2] + ex[3]

    onehot = jnp.zeros((TB, LANES), F32)
    for k in range(TOP_K):
        onehot = onehot + jnp.where(lane_f == idx[k], 1.0, 0.0)
    before = jnp.dot(tri_ref[...], onehot.astype(BF16), preferred_element_type=F32) + carry_ref[0:1, :]
    ri = jnp.zeros((TB, LANES), F32)
    rw = jnp.zeros((TB, LANES), F32)
    for k in range(TOP_K):
        rank_k = jnp.sum(jnp.where(lane_f == idx[k], before, 0.0), axis=-1, keepdims=True)
        ri = ri + jnp.where(lane == k, idx[k], 0.0) + jnp.where(lane == TOP_K + k, rank_k, 0.0)
        rw = rw + jnp.where(lane == k, ex[k] / den, 0.0)
    tab_ref[...] = ri.T[0:2 * TOP_K, :].astype(jnp.int32)
    rw_ref[...] = rw
    total = carry_ref[0:1, :] + jnp.sum(onehot, axis=0, keepdims=True)
    carry_ref[...] = jnp.broadcast_to(total, carry_ref.shape)
    cnt_ref[...] = jnp.broadcast_to(total, cnt_ref.shape).astype(jnp.int32)


def _post_call(yc, yl, xc, xl, mods, w_out, g_post, g_ffn, w_r, b_r, tri):
    nb = N_TOK // TB

    def tok(width):
        return pl.BlockSpec((TB, width), lambda i: (i, 0))

    def full(shape):
        return pl.BlockSpec(shape, lambda i: (0,) * len(shape))

    return pl.pallas_call(
        _post_kernel,
        grid=(nb,),
        in_specs=[_ctx_spec(D), _lat_spec(D), _ctx_spec(D), _lat_spec(D),
                  pl.BlockSpec((None, 6, D), lambda i: (_mod_index(i), 0, 0)),
                  full((D, D)), full((1, D)), full((1, D)), full((D, LANES)), full((1, LANES)),
                  full((TB, TB))],
        out_specs=(tok(D), tok(D), pl.BlockSpec((None, 2 * TOP_K, TB), lambda i: (i, 0, 0)),
                   tok(LANES), full((8, LANES))),
        out_shape=(jax.ShapeDtypeStruct((N_TOK, D), F32),
                   jax.ShapeDtypeStruct((N_TOK, D), F32),
                   jax.ShapeDtypeStruct((nb, 2 * TOP_K, TB), jnp.int32),
                   jax.ShapeDtypeStruct((N_TOK, LANES), F32),
                   jax.ShapeDtypeStruct((8, LANES), jnp.int32)),
        scratch_shapes=[pltpu.VMEM((8, LANES), F32)],
        compiler_params=pltpu.CompilerParams(dimension_semantics=("arbitrary",),
                                             vmem_limit_bytes=VMEM_LIMIT),
        name="post_mix_router",
    )(yc, yl, xc, xl, mods, w_out, g_post, g_ffn, w_r, b_r, tri)


def _dest_kernel(bs_ref, tab_ref, o_ref):
    idx = tab_ref[:, 0:TOP_K, :]
    base = jnp.zeros(idx.shape, jnp.int32)
    for e in range(N_EXPERTS):
        base = jnp.where(idx == e, bs_ref[e] * RB, base)
    o_ref[...] = base + tab_ref[:, TOP_K:2 * TOP_K, :]


def _dest_call(blk_start, tab):
    nb = N_TOK // TB
    return pl.pallas_call(
        _dest_kernel,
        in_specs=[pl.BlockSpec(memory_space=pltpu.SMEM), pl.BlockSpec(memory_space=pltpu.VMEM)],
        out_specs=pl.BlockSpec(memory_space=pltpu.VMEM),
        out_shape=jax.ShapeDtypeStruct((nb, TOP_K, TB), jnp.int32),
        name="dest_rows",
    )(blk_start, tab)


def _dispatch_kernel(zrow_ref, nu_ref, dest_ref, h2_ref, xs_ref, zero_ref, sem, zsem):
    i = pl.program_id(0)

    def zero_fill(row):
        return pltpu.make_async_copy(zero_ref, xs_ref.at[pl.ds(pl.multiple_of(row, 8), RB)], zsem)

    @pl.when(i == 0)
    def _():
        zero_ref[...] = jnp.zeros(zero_ref.shape, F32)
        for e in range(N_EXPERTS):
            zero_fill(zrow_ref[e]).start()
        for e in range(N_EXPERTS):
            zero_fill(zrow_ref[e]).wait()

        def tail_start(b, carry):
            zero_fill(b * RB).start()
            return carry

        def tail_wait(b, carry):
            zero_fill(b * RB).wait()
            return carry

        lax.fori_loop(nu_ref[0], N_BLOCKS, tail_start, 0)
        lax.fori_loop(nu_ref[0], N_BLOCKS, tail_wait, 0)

    def body(t, carry):
        for k in range(TOP_K):
            pltpu.make_async_copy(h2_ref.at[pl.ds(t, 1)], xs_ref.at[pl.ds(dest_ref[k, t], 1)],
                                  sem).start(priority=k % 2)
        return carry

    lax.fori_loop(0, TB, body, 0, unroll=8)
    for k in range(TOP_K):
        pltpu.make_async_copy(h2_ref, xs_ref.at[pl.ds(0, TB)], sem).wait()


def _dispatch_call(zrow, n_used, dest, h2):
    nb = N_TOK // TB
    grid_spec = pltpu.PrefetchScalarGridSpec(
        num_scalar_prefetch=2,
        grid=(nb,),
        in_specs=[
            pl.BlockSpec((None, TOP_K, TB), lambda i, z, n: (i, 0, 0), memory_space=pltpu.SMEM),
            pl.BlockSpec((TB, D), lambda i, z, n: (i, 0)),
        ],
        out_specs=pl.BlockSpec(memory_space=pl.ANY),
        scratch_shapes=[pltpu.VMEM((RB, D), F32), pltpu.SemaphoreType.DMA, pltpu.SemaphoreType.DMA],
    )
    return pl.pallas_call(
        _dispatch_kernel,
        grid_spec=grid_spec,
        out_shape=jax.ShapeDtypeStruct((N_ROWS, D), F32),
        compiler_params=pltpu.CompilerParams(dimension_semantics=("arbitrary",),
                                             vmem_limit_bytes=VMEM_LIMIT),
        name="dispatch_rows",
    )(zrow, n_used, dest, h2)


def _moe_kernel(be_ref, first_ref, nxt_ref, slot_ref, nu_ref, xs_ref, wgu_hbm, bgu_ref, wdn_hbm, bdn_ref,
                o_ref, wgu_f, wdn_f, wgu_s, wdn_s, sems, *, layer):
    b = pl.program_id(0)
    used = b < nu_ref[0]

    def fetch(e, s):
        return (pltpu.make_async_copy(wgu_hbm.at[layer, e], wgu_f.at[s], sems.at[0, s]),
                pltpu.make_async_copy(wdn_hbm.at[layer, e], wdn_f.at[s], sems.at[1, s]))

    @pl.when(b == 0)
    def _():
        for cp in fetch(be_ref[0], 0):
            cp.start()

    @pl.when(first_ref[b] == 1)
    def _():
        s = slot_ref[b]
        for cp in fetch(be_ref[b], s):
            cp.wait()

        @pl.when(nxt_ref[b] >= 0)
        def _():
            for cp in fetch(nxt_ref[b], 1 - s):
                cp.start()

        step = 128

        def body(c, carry):
            r = pl.multiple_of(c * step, step)
            wgu_s[pl.ds(r, step), :] = wgu_f[s, pl.ds(r, step), :].astype(BF16)
            wdn_s[pl.ds(r, step), :] = wdn_f[s, pl.ds(r, step), :].astype(BF16)
            return carry

        lax.fori_loop(0, D // step, body, 0)

    @pl.when(used)
    def _():
        gu = jnp.dot(xs_ref[...].astype(BF16), wgu_s[...], preferred_element_type=F32) + bgu_ref[...]
        g = jnp.minimum(gu[:, :D_FF], SWIGLU_LIMIT)
        l = jnp.clip(gu[:, D_FF:], -SWIGLU_LIMIT, SWIGLU_LIMIT)
        a = g * jax.nn.sigmoid(SWIGLU_ALPHA * g) * (l + 1.0)
        o_ref[...] = jnp.dot(a.astype(BF16), wdn_s[...], preferred_element_type=F32) + bdn_ref[...]

    @pl.when(jnp.logical_not(used))
    def _():
        o_ref[...] = jnp.zeros(o_ref.shape, F32)


def _moe_call(layer, tables, xs, w_gu, b_gu, w_dn, b_dn):
    def rows_in(b, be, fi, nx, sl, nu):
        return (jnp.minimum(b, nu[0] - 1), 0)

    def rows_out(b, be, fi, nx, sl, nu):
        return (b, 0)

    def expert(b, be, fi, nx, sl, nu):
        return (layer, be[b], 0, 0)

    grid_spec = pltpu.PrefetchScalarGridSpec(
        num_scalar_prefetch=5,
        grid=(N_BLOCKS,),
        in_specs=[
            pl.BlockSpec((RB, D), rows_in),
            pl.BlockSpec(memory_space=pl.ANY),
            pl.BlockSpec((None, None, 1, 2 * D_FF), expert),
            pl.BlockSpec(memory_space=pl.ANY),
            pl.BlockSpec((None, None, 1, D), expert),
        ],
        out_specs=pl.BlockSpec((RB, D), rows_out),
        scratch_shapes=[pltpu.VMEM((2, D, 2 * D_FF), F32), pltpu.VMEM((2, D_FF, D), F32),
                        pltpu.VMEM((D, 2 * D_FF), BF16), pltpu.VMEM((D_FF, D), BF16),
                        pltpu.SemaphoreType.DMA((2, 2))],
    )
    return pl.pallas_call(
        functools.partial(_moe_kernel, layer=layer),
        grid_spec=grid_spec,
        out_shape=jax.ShapeDtypeStruct((N_ROWS, D), F32),
        compiler_params=pltpu.CompilerParams(dimension_semantics=("arbitrary",),
                                             vmem_limit_bytes=VMEM_LIMIT),
        name="moe_experts",
    )(*tables, xs, w_gu, b_gu, w_dn, b_dn)


def _final_kernel(x1_ref, ys_ref, rw_ref, mod_ref, g_ref, oc_ref, ol_ref):
    i = pl.program_id(0)
    rw = rw_ref[...]
    y = ys_ref[0] * rw[:, 0:1]
    for k in range(1, TOP_K):
        y = y + ys_ref[k] * rw[:, k:k + 1]
    x2 = x1_ref[...] + mod_ref[5:6, :] * _rms(y, g_ref[...])

    @pl.when(i < N_CTX // TB)
    def _():
        oc_ref[...] = x2

    @pl.when(i >= N_CTX // TB)
    def _():
        ol_ref[...] = x2


def _final_call(x1, ysg, rw, mods, g_post_ffn):
    nb = N_TOK // TB
    return pl.pallas_call(
        _final_kernel,
        grid=(nb,),
        in_specs=[pl.BlockSpec((TB, D), lambda i: (i, 0)),
                  pl.BlockSpec((TOP_K, TB, D), lambda i: (0, i, 0)),
                  pl.BlockSpec((TB, LANES), lambda i: (i, 0)),
                  pl.BlockSpec((None, 6, D), lambda i: (_mod_index(i), 0, 0)),
                  pl.BlockSpec((1, D), lambda i: (0, 0))],
        out_specs=(_ctx_spec(D), _lat_spec(D)),
        out_shape=(jax.ShapeDtypeStruct((N_CTX, D), F32), jax.ShapeDtypeStruct((N_LAT, D), F32)),
        compiler_params=pltpu.CompilerParams(dimension_semantics=("arbitrary",),
                                             vmem_limit_bytes=VMEM_LIMIT),
        name="combine_residual",
    )(x1, ysg, rw, mods, g_post_ffn)


def _dft_tables(t):
    j = np.arange(t, dtype=np.int64)
    ang = 2.0 * np.pi * ((j[:, None] * j[None, :]) % t) / t
    return (np.cos(ang) / math.sqrt(t)).astype(np.float32), (np.sin(ang) / math.sqrt(t)).astype(np.float32)


def _channel_dft():
    c = np.arange(F_GROUP_DIM, dtype=np.int64)
    ang = 2.0 * np.pi * ((c[:, None] * c[None, :]) % F_GROUP_DIM) / F_GROUP_DIM
    eye = np.eye(F_GROUPS)
    bdc = np.kron(eye, np.cos(ang)) / math.sqrt(F_GROUP_DIM)
    bds = np.kron(eye, np.sin(ang)) / math.sqrt(F_GROUP_DIM)
    return bdc.astype(np.float32), bds.astype(np.float32)


def _rope_tables():
    pos = np.arange(LAT_T)
    n = ROPE // 4
    inv_freq = np.power(np.float32(ROPE_BASE), -np.arange(n, dtype=np.float32) / np.float32(n))
    ang_r = (pos // GRID_W).astype(np.float32)[:, None] * inv_freq
    ang_c = (pos % GRID_W).astype(np.float32)[:, None] * inv_freq
    cos = np.concatenate([np.cos(ang_r), np.cos(ang_r), np.cos(ang_c), np.cos(ang_c),
                          np.ones((LAT_T, LANES - ROPE))], axis=1)
    sin = np.concatenate([-np.sin(ang_r), np.sin(ang_r), -np.sin(ang_c), np.sin(ang_c),
                          np.zeros((LAT_T, LANES - ROPE))], axis=1)
    return cos.astype(np.float32), sin.astype(np.float32)


def kernel(x_prompt, x_sample, cache_ckv, cache_krope, c, c_ctx, w_ada, b_ada, g_pre_mix, g_post_mix, g_pre_ffn, g_post_ffn, w_in, g_sgu, w_spatial, b_spatial, g_q, w_uq, g_kv, w_ukv, w_out, w_router, b_router, w_gate_up, b_gate_up, w_down, b_down):
    xc, xl = x_prompt.reshape(N_CTX, D), x_sample.reshape(N_LAT, D)

    cond = jnp.concatenate([c_ctx[None, :], c, jnp.zeros((5, D), F32)], axis=0)
    mods = _mod_call(cond.T, w_ada, b_ada.reshape(DEPTH, 1, 6 * D))
    mods = mods[:, :3].reshape(DEPTH, 3, 6, D)

    gmat = jnp.asarray(np.kron(np.eye(A_HEADS), np.full((A_HEAD_DIM, A_HEAD_DIM), 1.0 / A_HEAD_DIM)),
                       dtype=BF16)
    bdc_np, bds_np = _channel_dft()
    bdc, bds = jnp.asarray(bdc_np).astype(BF16), jnp.asarray(bds_np).astype(BF16)
    dft = {}
    for t in (CTX_T, LAT_T):
        ct_np, st_np = _dft_tables(t)
        dft[t] = (jnp.asarray(ct_np).astype(BF16), jnp.asarray(st_np).astype(BF16))
    cos_np, sin_np = _rope_tables()
    cos_t, sin_t = jnp.asarray(cos_np), jnp.asarray(sin_np)
    tri = jnp.asarray(np.tril(np.ones((TB, TB), np.float32), k=-1)).astype(BF16)

    b_gu = b_gate_up.reshape(DEPTH, N_EXPERTS, 1, 2 * D_FF)
    b_dn = b_down.reshape(DEPTH, N_EXPERTS, 1, D)

    ckv_layers, krope_layers = [], []
    for i in range(DEPTH):
        w_in_p = jnp.pad(w_in[i], ((0, 0), (0, IN_PAD - w_in.shape[-1]))).astype(BF16)
        w_uq_p = jnp.pad(w_uq[i].reshape(Q_LORA, HEADS, NOPE + ROPE),
                         ((0, 0), (0, 0), (0, HEAD_PAD - NOPE - ROPE))).reshape(Q_LORA, HEADS * HEAD_PAD)
        w_uq_p = w_uq_p.astype(BF16)
        w_ukv_b = w_ukv[i].astype(BF16)
        pre = _pre_call(xc, xl, mods[i], g_pre_mix[i][None, :], w_in_p, g_sgu[i][None, :], gmat,
                        g_q[i][None, :], w_uq_p, g_kv[i][None, :], w_ukv_b, cos_t, sin_t)
        u, vn, zf, q, kv, kr, ckv, zkr = pre
        ckv_layers.append(ckv[:N_CTX].reshape(N_CTX_B, CTX_T, KV_LORA))
        krope_layers.append(zkr[:N_CTX, :ROPE].reshape(N_CTX_B, CTX_T, ROPE))

        ws = w_spatial[i].astype(BF16)
        bs_full = jnp.repeat(b_spatial[i].T, A_HEAD_DIM, axis=1)
        mix_in = (u, vn, zf, q, kv, kr)
        y_ctx = _mix_call(mix_in, (ws, bs_full, bdc, bds) + dft[CTX_T], CTX_T, N_CTX_B, 0)
        ckr_p = jnp.pad(cache_krope[:, i], ((0, 0), (0, 0), (0, LANES - ROPE))).astype(BF16)
        y_lat = _mix_call(mix_in, (ws, bs_full, bdc, bds) + dft[LAT_T], LAT_T, N_LAT_B, N_CTX,
                          cache=(cache_ckv[:, i], ckr_p, w_ukv_b))

        w_r = jnp.pad(w_router[i], ((0, 0), (0, LANES - N_EXPERTS)))
        b_r = jnp.pad(b_router[i], (0, LANES - N_EXPERTS), constant_values=NEG)[None, :]
        x1, h2, tab, rw, cnt = _post_call(y_ctx, y_lat, xc, xl, mods[i], w_out[i].astype(BF16),
                                          g_post_mix[i][None, :], g_pre_ffn[i][None, :], w_r, b_r, tri)

        counts = cnt[0, :N_EXPERTS]
        nblk = (counts + RB - 1) // RB
        blk_end = jnp.cumsum(nblk)
        blk_start = blk_end - nblk
        n_used = blk_end[-1:]
        zrow = blk_start * RB + counts // 8 * 8
        dest = _dest_call(blk_start, tab)
        xs = _dispatch_call(zrow, n_used, dest, h2)
        blk_id = jnp.arange(N_BLOCKS, dtype=jnp.int32)
        blk = jnp.minimum(blk_id, n_used - 1)
        block_e = jnp.sum((blk_end[None, :] <= blk[:, None]).astype(jnp.int32), axis=1)
        block_e = jnp.minimum(block_e, N_EXPERTS - 1)
        eid = jnp.arange(N_EXPERTS, dtype=jnp.int32)
        owns = nblk > 0
        later = jnp.where((eid[None, :] > eid[:, None]) & owns[None, :], eid[None, :], N_EXPERTS)
        nxt_e = jnp.min(later, axis=1)
        nxt_e = jnp.where(nxt_e == N_EXPERTS, -1, nxt_e)
        run_id = jnp.cumsum(owns.astype(jnp.int32)) - 1
        first = ((blk_id == blk_start[block_e]) & (blk_id < n_used)).astype(jnp.int32)
        tables = (block_e, first, nxt_e[block_e], run_id[block_e] % 2, n_used)
        ys = _moe_call(i, tables, xs, w_gate_up, b_gu, w_down, b_dn)
        ysg = ys[dest.transpose(1, 0, 2).reshape(-1)].reshape(TOP_K, N_TOK, D)
        xc, xl = _final_call(x1, ysg, rw, mods[i], g_post_ffn[i][None, :])

    y_prompt = xc.reshape(N_CTX_B, CTX_T, D)
    y_sample = xl.reshape(N_LAT_B, LAT_T, D)
    return (y_prompt, y_sample, jnp.stack(ckv_layers, axis=1), jnp.stack(krope_layers, axis=1))
```

```python
import functools
import math

import jax
import jax.numpy as jnp
import numpy as np
from jax import lax
from jax.experimental import pallas as pl
from jax.experimental.pallas import tpu as pltpu

F32 = jnp.float32
BF16 = jnp.bfloat16

D = 1024
N_CTX_B, CTX_T = 16, 256
N_LAT_B, LAT_T = 2, 1024
PAST = 512
N_CTX = N_CTX_B * CTX_T
N_LAT = N_LAT_B * LAT_T
N_TOK = N_CTX + N_LAT
DEPTH = 2
GRID_W = 64
EPS = 1e-6
A_HEADS, A_HEAD_DIM, A_WIDTH, CHUNK = 4, 64, 256, 128
F_GROUPS, F_GROUP_DIM, F_WIDTH = 4, 64, 256
HEADS, Q_LORA, KV_LORA, NOPE, ROPE, V_DIM = 4, 256, 128, 128, 64, 128
HEAD_PAD = 256
IN_PAD = 1280
N_EXPERTS, TOP_K, D_FF = 32, 4, 1024
SWIGLU_LIMIT, SWIGLU_ALPHA = 7.0, 1.702
ROPE_BASE = 10000.0

TB = 256
QB = 256
RB = 256
LANES = 128
N_BLOCKS = N_TOK * TOP_K // RB + N_EXPERTS
N_ROWS = N_BLOCKS * RB
NEG = -3.0e38
VMEM_LIMIT = 56 * 1024 * 1024


def _rms(x, g):
    return x * lax.rsqrt(jnp.mean(x * x, axis=-1, keepdims=True) + EPS) * g


def _split_dot(v, m):
    hi = v.astype(BF16)
    lo = (v - hi.astype(F32)).astype(BF16)
    return (jnp.dot(hi, m, preferred_element_type=F32)
            + jnp.dot(lo, m, preferred_element_type=F32))


def _dot_nt(a, b):
    return lax.dot_general(a, b, (((1,), (1,)), ((), ())), preferred_element_type=F32)


def _mod_kernel(ct_ref, w_ref, b_ref, o_ref):
    ct = ct_ref[...]
    s = ct * jax.nn.sigmoid(ct)
    w = w_ref[...]
    o_ref[...] = jnp.zeros(o_ref.shape, F32)
    for r in range(3):
        o_ref[r:r + 1, :] = jnp.sum(w * s[:, r:r + 1], axis=0, keepdims=True) + b_ref[...]


def _mod_call(cond_t, w_ada, b_ada):
    cb = 512
    return pl.pallas_call(
        _mod_kernel,
        grid=(DEPTH, 6 * D // cb),
        in_specs=[
            pl.BlockSpec((D, 8), lambda l, j: (0, 0)),
            pl.BlockSpec((None, D, cb), lambda l, j: (l, 0, j)),
            pl.BlockSpec((None, 1, cb), lambda l, j: (l, 0, j)),
        ],
        out_specs=pl.BlockSpec((None, 8, cb), lambda l, j: (l, 0, j)),
        out_shape=jax.ShapeDtypeStruct((DEPTH, 8, 6 * D), F32),
        compiler_params=pltpu.CompilerParams(dimension_semantics=("parallel", "parallel")),
        name="modulation",
    )(cond_t, w_ada, b_ada)


def _mod_index(i):
    first_lat = N_CTX // TB
    return jnp.where(i < first_lat, 0, 1 + (i - first_lat) // (LAT_T // TB))


def _ctx_spec(width):
    return pl.BlockSpec((TB, width), lambda i: (jnp.minimum(i, N_CTX // TB - 1), 0))


def _lat_spec(width):
    return pl.BlockSpec((TB, width), lambda i: (jnp.maximum(i - N_CTX // TB, 0), 0))


def _pick(i, ctx_ref, lat_ref):
    return jnp.where(i >= N_CTX // TB, lat_ref[...], ctx_ref[...])


def _swap_halves(x, lane):
    w = x.shape[-1]
    fwd = pltpu.roll(x, w - 16, 1)
    bwd = pltpu.roll(x, 16, 1)
    return jnp.where((lane & 31) < 16, fwd, bwd)


def _pre_kernel(xc_ref, xl_ref, mod_ref, gpre_ref, win_ref, gsgu_ref, gmat_ref, gq_ref, wuq_ref,
                gkv_ref, wukv_ref, cos_ref, sin_ref,
                u_ref, vn_ref, zf_ref, q_ref, kv_ref, kr_ref, ckv_ref, zkr_ref):
    i = pl.program_id(0)
    is_lat = i >= N_CTX // TB
    x = _pick(i, xc_ref, xl_ref)
    h = _rms(x, gpre_ref[...]) * (1.0 + mod_ref[1:2, :]) + mod_ref[0:1, :]
    z = jnp.dot(h.astype(BF16), win_ref[...], preferred_element_type=F32)

    ga = jax.nn.gelu(z[:, :2 * A_WIDTH])
    u_ref[...] = ga[:, :A_WIDTH]
    v = ga[:, A_WIDTH:]
    gmat = gmat_ref[...]
    dv = v - _split_dot(v, gmat)
    var = _split_dot(dv * dv, gmat)
    vn_ref[...] = (dv * lax.rsqrt(var + EPS) * gsgu_ref[...]).astype(BF16)

    zf_ref[...] = z[:, 512:768].astype(BF16)

    cos = jnp.where(is_lat, cos_ref[...], 1.0)
    sin = jnp.where(is_lat, sin_ref[...], 0.0)
    lane = lax.broadcasted_iota(jnp.int32, (TB, LANES), 1)

    qn = _rms(z[:, 768:1024], gq_ref[...])
    scale = (NOPE + ROPE) ** -0.5
    q = jnp.dot(qn.astype(BF16), wuq_ref[...], preferred_element_type=F32) * scale
    for hd in range(HEADS):
        base = hd * HEAD_PAD
        q_ref[:, base:base + NOPE] = q[:, base:base + NOPE].astype(BF16)
        qr = q[:, base + NOPE:base + HEAD_PAD]
        q_ref[:, base + NOPE:base + HEAD_PAD] = (qr * cos + _swap_halves(qr, lane) * sin).astype(BF16)

    ckv = _rms(z[:, 1024:1152], gkv_ref[...])
    ckv_ref[...] = ckv
    kv_ref[...] = jnp.dot(ckv.astype(BF16), wukv_ref[...], preferred_element_type=F32).astype(BF16)

    zkr = z[:, 1152:1280]
    zkr_ref[...] = zkr
    kr_ref[...] = (zkr * cos + _swap_halves(zkr, lane) * sin).astype(BF16)


def _pre_call(xc, xl, mods, g_pre, w_in, g_sgu, gmat, g_q, w_uq, g_kv, w_ukv, cos_t, sin_t):
    nb = N_TOK // TB
    first_lat = N_CTX // TB
    pos_blocks = LAT_T // TB

    def tok(width):
        return pl.BlockSpec((TB, width), lambda i: (i, 0))

    def full(shape):
        return pl.BlockSpec(shape, lambda i: (0,) * len(shape))

    def rope_map(i):
        return (jnp.where(i >= first_lat, (i - first_lat) % pos_blocks, 0), 0)

    out_shape = (
        jax.ShapeDtypeStruct((N_TOK, A_WIDTH), F32),
        jax.ShapeDtypeStruct((N_TOK, A_WIDTH), BF16),
        jax.ShapeDtypeStruct((N_TOK, F_WIDTH), BF16),
        jax.ShapeDtypeStruct((N_TOK, HEADS * HEAD_PAD), BF16),
        jax.ShapeDtypeStruct((N_TOK, HEADS * (NOPE + V_DIM)), BF16),
        jax.ShapeDtypeStruct((N_TOK, LANES), BF16),
        jax.ShapeDtypeStruct((N_TOK, KV_LORA), F32),
        jax.ShapeDtypeStruct((N_TOK, LANES), F32),
    )
    return pl.pallas_call(
        _pre_kernel,
        grid=(nb,),
        in_specs=[
            _ctx_spec(D), _lat_spec(D),
            pl.BlockSpec((None, 6, D), lambda i: (_mod_index(i), 0, 0)),
            full((1, D)), full((D, IN_PAD)), full((1, A_WIDTH)), full((A_WIDTH, A_WIDTH)),
            full((1, Q_LORA)), full((Q_LORA, HEADS * HEAD_PAD)),
            full((1, KV_LORA)), full((KV_LORA, HEADS * (NOPE + V_DIM))),
            pl.BlockSpec((TB, LANES), rope_map), pl.BlockSpec((TB, LANES), rope_map),
        ],
        out_specs=(tok(A_WIDTH), tok(A_WIDTH), tok(F_WIDTH), tok(HEADS * HEAD_PAD),
                   tok(HEADS * (NOPE + V_DIM)), tok(LANES), tok(KV_LORA), tok(LANES)),
        out_shape=out_shape,
        compiler_params=pltpu.CompilerParams(dimension_semantics=("parallel",),
                                             vmem_limit_bytes=VMEM_LIMIT),
        name="pre_mix",
    )(xc, xl, mods, g_pre, w_in, g_sgu, gmat, g_q, w_uq, g_kv, w_ukv, cos_t, sin_t)


def _mix_kernel(*refs, has_cache):
    if has_cache:
        (u_ref, vn_ref, zf_ref, q_ref, kv_ref, kr_ref, ws_ref, bs_ref, bdc_ref, bds_ref,
         ct_ref, st_ref, cckv_ref, ckr_ref, wukv_ref, o_ref) = refs
    else:
        (u_ref, vn_ref, zf_ref, q_ref, kv_ref, kr_ref, ws_ref, bs_ref, bdc_ref, bds_ref,
         ct_ref, st_ref, o_ref) = refs

    lane = lax.broadcasted_iota(jnp.int32, (CHUNK, A_WIDTH), 1)
    for c in range(QB // CHUNK):
        rows = slice(c * CHUNK, (c + 1) * CHUNK)
        vch = vn_ref[rows, :]
        s = bs_ref[...]
        for g in range(A_HEADS):
            sg = jnp.dot(ws_ref[g], vch, preferred_element_type=F32)
            in_head = (lane >= g * A_HEAD_DIM) & (lane < (g + 1) * A_HEAD_DIM)
            s = s + jnp.where(in_head, sg, 0.0)
        o_ref[rows, 0:A_WIDTH] = (u_ref[rows, :] * s).astype(BF16)

    zf = zf_ref[...]
    zc = jnp.dot(zf, bdc_ref[...], preferred_element_type=F32).astype(BF16)
    zs = jnp.dot(zf, bds_ref[...], preferred_element_type=F32).astype(BF16)
    yf = (jnp.dot(ct_ref[...], zc, preferred_element_type=F32)
          - jnp.dot(st_ref[...], zs, preferred_element_type=F32))
    o_ref[:, A_WIDTH:A_WIDTH + F_WIDTH] = yf.astype(BF16)

    kr = kr_ref[...]
    if has_cache:
        kvc = jnp.dot(cckv_ref[...].astype(BF16), wukv_ref[...],
                      preferred_element_type=F32).astype(BF16)
        krc = ckr_ref[...]
    for hd in range(HEADS):
        qh = q_ref[:, hd * HEAD_PAD:(hd + 1) * HEAD_PAD]
        kb = hd * (NOPE + V_DIM)
        kh = jnp.concatenate([kv_ref[:, kb:kb + NOPE], kr], axis=1)
        vh = kv_ref[:, kb + NOPE:kb + NOPE + V_DIM]
        s = _dot_nt(qh, kh)
        m = jnp.max(s, axis=-1, keepdims=True)
        if has_cache:
            khc = jnp.concatenate([kvc[:, kb:kb + NOPE], krc], axis=1)
            vhc = kvc[:, kb + NOPE:kb + NOPE + V_DIM]
            sc = _dot_nt(qh, khc)
            m = jnp.maximum(m, jnp.max(sc, axis=-1, keepdims=True))
        e = jnp.exp(s - m)
        den = jnp.sum(e, axis=-1, keepdims=True)
        o = jnp.dot(e.astype(BF16), vh, preferred_element_type=F32)
        if has_cache:
            ec = jnp.exp(sc - m)
            den = den + jnp.sum(ec, axis=-1, keepdims=True)
            o = o + jnp.dot(ec.astype(BF16), vhc, preferred_element_type=F32)
        ob = A_WIDTH + F_WIDTH + hd * V_DIM
        o_ref[:, ob:ob + V_DIM] = (o * (1.0 / den)).astype(BF16)


def _mix_call(pre, consts, seq_t, n_batch, tok_off, cache=None):
    u, vn, zf, q, kv, kr = pre
    ws, bs_full, bdc, bds, ct, st = consts
    nq = seq_t // QB
    qoff = tok_off // QB
    soff = tok_off // seq_t

    def qrow(width):
        return pl.BlockSpec((QB, width), lambda b, j: (qoff + b * nq + j, 0))

    def srow(width):
        return pl.BlockSpec((seq_t, width), lambda b, j: (soff + b, 0))

    def full(shape):
        return pl.BlockSpec(shape, lambda b, j: (0,) * len(shape))

    in_specs = [qrow(A_WIDTH), qrow(A_WIDTH), srow(F_WIDTH), qrow(HEADS * HEAD_PAD),
                srow(HEADS * (NOPE + V_DIM)), srow(LANES),
                full((A_HEADS, CHUNK, CHUNK)), full((CHUNK, A_WIDTH)),
                full((F_WIDTH, F_WIDTH)), full((F_WIDTH, F_WIDTH)),
                pl.BlockSpec((QB, seq_t), lambda b, j: (j, 0)),
                pl.BlockSpec((QB, seq_t), lambda b, j: (j, 0))]
    args = [u, vn, zf, q, kv, kr, ws, bs_full, bdc, bds, ct, st]
    if cache is not None:
        cckv, ckr, wukv = cache
        in_specs += [pl.BlockSpec((None, PAST, KV_LORA), lambda b, j: (b, 0, 0)),
                     pl.BlockSpec((None, PAST, LANES), lambda b, j: (b, 0, 0)),
                     full((KV_LORA, HEADS * (NOPE + V_DIM)))]
        args += [cckv, ckr, wukv]
    return pl.pallas_call(
        functools.partial(_mix_kernel, has_cache=cache is not None),
        grid=(n_batch, nq),
        in_specs=in_specs,
        out_specs=pl.BlockSpec((QB, D), lambda b, j: (b * nq + j, 0)),
        out_shape=jax.ShapeDtypeStruct((n_batch * seq_t, D), BF16),
        compiler_params=pltpu.CompilerParams(dimension_semantics=("parallel", "parallel"),
                                             vmem_limit_bytes=VMEM_LIMIT),
        name="mix_lat" if cache is not None else "mix_ctx",
    )(*args)


def _post_kernel(yc_ref, yl_ref, xc_ref, xl_ref, mod_ref, wout_ref, gpost_ref, gffn_ref, wr_ref, br_ref,
                 tri_ref, x1_ref, h2_ref, tab_ref, rw_ref, cnt_ref, carry_ref):
    i = pl.program_id(0)

    @pl.when(i == 0)
    def _():
        carry_ref[...] = jnp.zeros(carry_ref.shape, F32)

    y = jnp.dot(_pick(i, yc_ref, yl_ref), wout_ref[...], preferred_element_type=F32)
    x1 = _pick(i, xc_ref, xl_ref) + mod_ref[2:3, :] * _rms(y, gpost_ref[...])
    x1_ref[...] = x1
    h2 = _rms(x1, gffn_ref[...]) * (1.0 + mod_ref[4:5, :]) + mod_ref[3:4, :]
    h2_ref[...] = h2

    wr = wr_ref[...]
    wr_hi = wr.astype(BF16)
    wr_lo = (wr - wr_hi.astype(F32)).astype(BF16)
    h_hi = h2.astype(BF16)
    h_lo = (h2 - h_hi.astype(F32)).astype(BF16)
    logits = (jnp.dot(h_hi, wr_hi, preferred_element_type=F32)
              + jnp.dot(h_lo, wr_hi, preferred_element_type=F32)
              + jnp.dot(h_hi, wr_lo, preferred_element_type=F32)) + br_ref[...]

    lane = lax.broadcasted_iota(jnp.int32, (TB, LANES), 1)
    lane_f = lane.astype(F32)
    work = logits
    idx, val = [], []
    for _ in range(TOP_K):
        m = jnp.max(work, axis=-1, keepdims=True)
        ik = jnp.min(jnp.where(work == m, lane_f, float(LANES)), axis=-1, keepdims=True)
        idx.append(ik)
        val.append(m)
        work = jnp.where(lane_f == ik, NEG, work)
    ex = [jnp.exp(v - val[0]) for v in val]
    den = ex[0] + ex[1] + ex[2] + ex[3]

    onehot = jnp.zeros((TB, LANES), F32)
    for k in range(TOP_K):
        onehot = onehot + jnp.where(lane_f == idx[k], 1.0, 0.0)
    before = jnp.dot(tri_ref[...], onehot.astype(BF16), preferred_element_type=F32) + carry_ref[0:1, :]
    ri = jnp.zeros((TB, LANES), F32)
    rw = jnp.zeros((TB, LANES), F32)
    for k in range(TOP_K):
        rank_k = jnp.sum(jnp.where(lane_f == idx[k], before, 0.0), axis=-1, keepdims=True)
        ri = ri + jnp.where(lane == k, idx[k], 0.0) + jnp.where(lane == TOP_K + k, rank_k, 0.0)
        rw = rw + jnp.where(lane == k, ex[k] / den, 0.0)
    tab_ref[...] = ri.T[0:2 * TOP_K, :].astype(jnp.int32)
    rw_ref[...] = rw
    total = carry_ref[0:1, :] + jnp.sum(onehot, axis=0, keepdims=True)
    carry_ref[...] = jnp.broadcast_to(total, carry_ref.shape)
    cnt_ref[...] = jnp.broadcast_to(total, cnt_ref.shape).astype(jnp.int32)


def _post_call(yc, yl, xc, xl, mods, w_out, g_post, g_ffn, w_r, b_r, tri):
    nb = N_TOK // TB

    def tok(width):
        return pl.BlockSpec((TB, width), lambda i: (i, 0))

    def full(shape):
        return pl.BlockSpec(shape, lambda i: (0,) * len(shape))

    return pl.pallas_call(
        _post_kernel,
        grid=(nb,),
        in_specs=[_ctx_spec(D), _lat_spec(D), _ctx_spec(D), _lat_spec(D),
                  pl.BlockSpec((None, 6, D), lambda i: (_mod_index(i), 0, 0)),
                  full((D, D)), full((1, D)), full((1, D)), full((D, LANES)), full((1, LANES)),
                  full((TB, TB))],
        out_specs=(tok(D), tok(D), pl.BlockSpec((None, 2 * TOP_K, TB), lambda i: (i, 0, 0)),
                   tok(LANES), full((8, LANES))),
        out_shape=(jax.ShapeDtypeStruct((N_TOK, D), F32),
                   jax.ShapeDtypeStruct((N_TOK, D), F32),
                   jax.ShapeDtypeStruct((nb, 2 * TOP_K, TB), jnp.int32),
                   jax.ShapeDtypeStruct((N_TOK, LANES), F32),
                   jax.ShapeDtypeStruct((8, LANES), jnp.int32)),
        scratch_shapes=[pltpu.VMEM((8, LANES), F32)],
        compiler_params=pltpu.CompilerParams(dimension_semantics=("arbitrary",),
                                             vmem_limit_bytes=VMEM_LIMIT),
        name="post_mix_router",
    )(yc, yl, xc, xl, mods, w_out, g_post, g_ffn, w_r, b_r, tri)


def _dest_kernel(bs_ref, tab_ref, o_ref):
    idx = tab_ref[:, 0:TOP_K, :]
    base = jnp.zeros(idx.shape, jnp.int32)
    for e in range(N_EXPERTS):
        base = jnp.where(idx == e, bs_ref[e] * RB, base)
    o_ref[...] = base + tab_ref[:, TOP_K:2 * TOP_K, :]


def _dest_call(blk_start, tab):
    nb = N_TOK // TB
    return pl.pallas_call(
        _dest_kernel,
        in_specs=[pl.BlockSpec(memory_space=pltpu.SMEM), pl.BlockSpec(memory_space=pltpu.VMEM)],
        out_specs=pl.BlockSpec(memory_space=pltpu.VMEM),
        out_shape=jax.ShapeDtypeStruct((nb, TOP_K, TB), jnp.int32),
        name="dest_rows",
    )(blk_start, tab)


def _dispatch_kernel(zrow_ref, nu_ref, dest_ref, h2_ref, xs_ref, zero_ref, sem, zsem):
    i = pl.program_id(0)

    def zero_fill(row):
        return pltpu.make_async_copy(zero_ref, xs_ref.at[pl.ds(pl.multiple_of(row, 8), RB)], zsem)

    @pl.when(i == 0)
    def _():
        zero_ref[...] = jnp.zeros(zero_ref.shape, F32)
        for e in range(N_EXPERTS):
            zero_fill(zrow_ref[e]).start()
        for e in range(N_EXPERTS):
            zero_fill(zrow_ref[e]).wait()

        def tail_start(b, carry):
            zero_fill(b * RB).start()
            return carry

        def tail_wait(b, carry):
            zero_fill(b * RB).wait()
            return carry

        lax.fori_loop(nu_ref[0], N_BLOCKS, tail_start, 0)
        lax.fori_loop(nu_ref[0], N_BLOCKS, tail_wait, 0)

    def body(t, carry):
        for k in range(TOP_K):
            pltpu.make_async_copy(h2_ref.at[pl.ds(t, 1)], xs_ref.at[pl.ds(dest_ref[k, t], 1)],
                                  sem).start(priority=k % 2)
        return carry

    lax.fori_loop(0, TB, body, 0, unroll=8)
    for k in range(TOP_K):
        pltpu.make_async_copy(h2_ref, xs_ref.at[pl.ds(0, TB)], sem).wait()


def _dispatch_call(zrow, n_used, dest, h2):
    nb = N_TOK // TB
    grid_spec = pltpu.PrefetchScalarGridSpec(
        num_scalar_prefetch=2,
        grid=(nb,),
        in_specs=[
            pl.BlockSpec((None, TOP_K, TB), lambda i, z, n: (i, 0, 0), memory_space=pltpu.SMEM),
            pl.BlockSpec((TB, D), lambda i, z, n: (i, 0)),
        ],
        out_specs=pl.BlockSpec(memory_space=pl.ANY),
        scratch_shapes=[pltpu.VMEM((RB, D), F32), pltpu.SemaphoreType.DMA, pltpu.SemaphoreType.DMA],
    )
    return pl.pallas_call(
        _dispatch_kernel,
        grid_spec=grid_spec,
        out_shape=jax.ShapeDtypeStruct((N_ROWS, D), F32),
        compiler_params=pltpu.CompilerParams(dimension_semantics=("arbitrary",),
                                             vmem_limit_bytes=VMEM_LIMIT),
        name="dispatch_rows",
    )(zrow, n_used, dest, h2)


def _moe_kernel(be_ref, first_ref, nxt_ref, slot_ref, nu_ref, xs_ref, wgu_hbm, bgu_ref, wdn_hbm, bdn_ref,
                o_ref, wgu_f, wdn_f, wgu_s, wdn_s, sems, *, layer):
    b = pl.program_id(0)
    used = b < nu_ref[0]

    def fetch(e, s):
        return (pltpu.make_async_copy(wgu_hbm.at[layer, e], wgu_f.at[s], sems.at[0, s]),
                pltpu.make_async_copy(wdn_hbm.at[layer, e], wdn_f.at[s], sems.at[1, s]))

    @pl.when(b == 0)
    def _():
        for cp in fetch(be_ref[0], 0):
            cp.start()

    @pl.when(first_ref[b] == 1)
    def _():
        s = slot_ref[b]
        for cp in fetch(be_ref[b], s):
            cp.wait()

        @pl.when(nxt_ref[b] >= 0)
        def _():
            for cp in fetch(nxt_ref[b], 1 - s):
                cp.start()

        step = 128

        def body(c, carry):
            r = pl.multiple_of(c * step, step)
            wgu_s[pl.ds(r, step), :] = wgu_f[s, pl.ds(r, step), :].astype(BF16)
            wdn_s[pl.ds(r, step), :] = wdn_f[s, pl.ds(r, step), :].astype(BF16)
            return carry

        lax.fori_loop(0, D // step, body, 0)

    @pl.when(used)
    def _():
        gu = jnp.dot(xs_ref[...].astype(BF16), wgu_s[...], preferred_element_type=F32) + bgu_ref[...]
        g = jnp.minimum(gu[:, :D_FF], SWIGLU_LIMIT)
        l = jnp.clip(gu[:, D_FF:], -SWIGLU_LIMIT, SWIGLU_LIMIT)
        a = g * jax.nn.sigmoid(SWIGLU_ALPHA * g) * (l + 1.0)
        y = jnp.dot(a.astype(BF16), wdn_s[...], preferred_element_type=F32) + bdn_ref[...]
        o_ref[...] = y.astype(o_ref.dtype)

    @pl.when(jnp.logical_not(used))
    def _():
        o_ref[...] = jnp.zeros(o_ref.shape, o_ref.dtype)


def _moe_call(layer, tables, xs, w_gu, b_gu, w_dn, b_dn):
    def rows_in(b, be, fi, nx, sl, nu):
        return (jnp.minimum(b, nu[0] - 1), 0)

    def rows_out(b, be, fi, nx, sl, nu):
        return (b, 0)

    def expert(b, be, fi, nx, sl, nu):
        return (layer, be[b], 0, 0)

    grid_spec = pltpu.PrefetchScalarGridSpec(
        num_scalar_prefetch=5,
        grid=(N_BLOCKS,),
        in_specs=[
            pl.BlockSpec((RB, D), rows_in),
            pl.BlockSpec(memory_space=pl.ANY),
            pl.BlockSpec((None, None, 1, 2 * D_FF), expert),
            pl.BlockSpec(memory_space=pl.ANY),
            pl.BlockSpec((None, None, 1, D), expert),
        ],
        out_specs=pl.BlockSpec((RB, D), rows_out),
        scratch_shapes=[pltpu.VMEM((2, D, 2 * D_FF), F32), pltpu.VMEM((2, D_FF, D), F32),
                        pltpu.VMEM((D, 2 * D_FF), BF16), pltpu.VMEM((D_FF, D), BF16),
                        pltpu.SemaphoreType.DMA((2, 2))],
    )
    return pl.pallas_call(
        functools.partial(_moe_kernel, layer=layer),
        grid_spec=grid_spec,
        out_shape=jax.ShapeDtypeStruct((N_ROWS, D), BF16),
        compiler_params=pltpu.CompilerParams(dimension_semantics=("arbitrary",),
                                             vmem_limit_bytes=VMEM_LIMIT),
        name="moe_experts",
    )(*tables, xs, w_gu, b_gu, w_dn, b_dn)


def _final_kernel(x1_ref, ys_ref, rw_ref, mod_ref, g_ref, oc_ref, ol_ref):
    i = pl.program_id(0)
    rw = rw_ref[...]
    y = ys_ref[0].astype(F32) * rw[:, 0:1]
    for k in range(1, TOP_K):
        y = y + ys_ref[k].astype(F32) * rw[:, k:k + 1]
    x2 = x1_ref[...] + mod_ref[5:6, :] * _rms(y, g_ref[...])

    @pl.when(i < N_CTX // TB)
    def _():
        oc_ref[...] = x2

    @pl.when(i >= N_CTX // TB)
    def _():
        ol_ref[...] = x2


def _final_call(x1, ysg, rw, mods, g_post_ffn):
    nb = N_TOK // TB
    return pl.pallas_call(
        _final_kernel,
        grid=(nb,),
        in_specs=[pl.BlockSpec((TB, D), lambda i: (i, 0)),
                  pl.BlockSpec((TOP_K, TB, D), lambda i: (0, i, 0)),
                  pl.BlockSpec((TB, LANES), lambda i: (i, 0)),
                  pl.BlockSpec((None, 6, D), lambda i: (_mod_index(i), 0, 0)),
                  pl.BlockSpec((1, D), lambda i: (0, 0))],
        out_specs=(_ctx_spec(D), _lat_spec(D)),
        out_shape=(jax.ShapeDtypeStruct((N_CTX, D), F32), jax.ShapeDtypeStruct((N_LAT, D), F32)),
        compiler_params=pltpu.CompilerParams(dimension_semantics=("arbitrary",),
                                             vmem_limit_bytes=VMEM_LIMIT),
        name="combine_residual",
    )(x1, ysg, rw, mods, g_post_ffn)


def _dft_tables(t):
    j = np.arange(t, dtype=np.int64)
    ang = 2.0 * np.pi * ((j[:, None] * j[None, :]) % t) / t
    return (np.cos(ang) / math.sqrt(t)).astype(np.float32), (np.sin(ang) / math.sqrt(t)).astype(np.float32)


def _channel_dft():
    c = np.arange(F_GROUP_DIM, dtype=np.int64)
    ang = 2.0 * np.pi * ((c[:, None] * c[None, :]) % F_GROUP_DIM) / F_GROUP_DIM
    eye = np.eye(F_GROUPS)
    bdc = np.kron(eye, np.cos(ang)) / math.sqrt(F_GROUP_DIM)
    bds = np.kron(eye, np.sin(ang)) / math.sqrt(F_GROUP_DIM)
    return bdc.astype(np.float32), bds.astype(np.float32)


def _rope_tables():
    pos = np.arange(LAT_T)
    n = ROPE // 4
    inv_freq = np.power(np.float32(ROPE_BASE), -np.arange(n, dtype=np.float32) / np.float32(n))
    ang_r = (pos // GRID_W).astype(np.float32)[:, None] * inv_freq
    ang_c = (pos % GRID_W).astype(np.float32)[:, None] * inv_freq
    cos = np.concatenate([np.cos(ang_r), np.cos(ang_r), np.cos(ang_c), np.cos(ang_c),
                          np.ones((LAT_T, LANES - ROPE))], axis=1)
    sin = np.concatenate([-np.sin(ang_r), np.sin(ang_r), -np.sin(ang_c), np.sin(ang_c),
                          np.zeros((LAT_T, LANES - ROPE))], axis=1)
    return cos.astype(np.float32), sin.astype(np.float32)


def kernel(x_prompt, x_sample, cache_ckv, cache_krope, c, c_ctx, w_ada, b_ada, g_pre_mix, g_post_mix, g_pre_ffn, g_post_ffn, w_in, g_sgu, w_spatial, b_spatial, g_q, w_uq, g_kv, w_ukv, w_out, w_router, b_router, w_gate_up, b_gate_up, w_down, b_down):
    xc, xl = x_prompt.reshape(N_CTX, D), x_sample.reshape(N_LAT, D)

    cond = jnp.concatenate([c_ctx[None, :], c, jnp.zeros((5, D), F32)], axis=0)
    mods = _mod_call(cond.T, w_ada, b_ada.reshape(DEPTH, 1, 6 * D))
    mods = mods[:, :3].reshape(DEPTH, 3, 6, D)

    gmat = jnp.asarray(np.kron(np.eye(A_HEADS), np.full((A_HEAD_DIM, A_HEAD_DIM), 1.0 / A_HEAD_DIM)),
                       dtype=BF16)
    bdc_np, bds_np = _channel_dft()
    bdc, bds = jnp.asarray(bdc_np).astype(BF16), jnp.asarray(bds_np).astype(BF16)
    dft = {}
    for t in (CTX_T, LAT_T):
        ct_np, st_np = _dft_tables(t)
        dft[t] = (jnp.asarray(ct_np).astype(BF16), jnp.asarray(st_np).astype(BF16))
    cos_np, sin_np = _rope_tables()
    cos_t, sin_t = jnp.asarray(cos_np), jnp.asarray(sin_np)
    tri = jnp.asarray(np.tril(np.ones((TB, TB), np.float32), k=-1)).astype(BF16)

    b_gu = b_gate_up.reshape(DEPTH, N_EXPERTS, 1, 2 * D_FF)
    b_dn = b_down.reshape(DEPTH, N_EXPERTS, 1, D)

    ckv_layers, krope_layers = [], []
    for i in range(DEPTH):
        w_in_p = jnp.pad(w_in[i], ((0, 0), (0, IN_PAD - w_in.shape[-1]))).astype(BF16)
        w_uq_p = jnp.pad(w_uq[i].reshape(Q_LORA, HEADS, NOPE + ROPE),
                         ((0, 0), (0, 0), (0, HEAD_PAD - NOPE - ROPE))).reshape(Q_LORA, HEADS * HEAD_PAD)
        w_uq_p = w_uq_p.astype(BF16)
        w_ukv_b = w_ukv[i].astype(BF16)
        pre = _pre_call(xc, xl, mods[i], g_pre_mix[i][None, :], w_in_p, g_sgu[i][None, :], gmat,
                        g_q[i][None, :], w_uq_p, g_kv[i][None, :], w_ukv_b, cos_t, sin_t)
        u, vn, zf, q, kv, kr, ckv, zkr = pre
        ckv_layers.append(ckv[:N_CTX].reshape(N_CTX_B, CTX_T, KV_LORA))
        krope_layers.append(zkr[:N_CTX, :ROPE].reshape(N_CTX_B, CTX_T, ROPE))

        ws = w_spatial[i].astype(BF16)
        bs_full = jnp.repeat(b_spatial[i].T, A_HEAD_DIM, axis=1)
        mix_in = (u, vn, zf, q, kv, kr)
        y_ctx = _mix_call(mix_in, (ws, bs_full, bdc, bds) + dft[CTX_T], CTX_T, N_CTX_B, 0)
        ckr_p = jnp.pad(cache_krope[:, i], ((0, 0), (0, 0), (0, LANES - ROPE))).astype(BF16)
        y_lat = _mix_call(mix_in, (ws, bs_full, bdc, bds) + dft[LAT_T], LAT_T, N_LAT_B, N_CTX,
                          cache=(cache_ckv[:, i], ckr_p, w_ukv_b))

        w_r = jnp.pad(w_router[i], ((0, 0), (0, LANES - N_EXPERTS)))
        b_r = jnp.pad(b_router[i], (0, LANES - N_EXPERTS), constant_values=NEG)[None, :]
        x1, h2, tab, rw, cnt = _post_call(y_ctx, y_lat, xc, xl, mods[i], w_out[i].astype(BF16),
                                          g_post_mix[i][None, :], g_pre_ffn[i][None, :], w_r, b_r, tri)

        counts = cnt[0, :N_EXPERTS]
        nblk = (counts + RB - 1) // RB
        blk_end = jnp.cumsum(nblk)
        blk_start = blk_end - nblk
        n_used = blk_end[-1:]
        zrow = blk_start * RB + counts // 8 * 8
        dest = _dest_call(blk_start, tab)
        xs = _dispatch_call(zrow, n_used, dest, h2)
        blk_id = jnp.arange(N_BLOCKS, dtype=jnp.int32)
        blk = jnp.minimum(blk_id, n_used - 1)
        block_e = jnp.sum((blk_end[None, :] <= blk[:, None]).astype(jnp.int32), axis=1)
        block_e = jnp.minimum(block_e, N_EXPERTS - 1)
        eid = jnp.arange(N_EXPERTS, dtype=jnp.int32)
        owns = nblk > 0
        later = jnp.where((eid[None, :] > eid[:, None]) & owns[None, :], eid[None, :], N_EXPERTS)
        nxt_e = jnp.min(later, axis=1)
        nxt_e = jnp.where(nxt_e == N_EXPERTS, -1, nxt_e)
        run_id = jnp.cumsum(owns.astype(jnp.int32)) - 1
        first = ((blk_id == blk_start[block_e]) & (blk_id < n_used)).astype(jnp.int32)
        tables = (block_e, first, nxt_e[block_e], run_id[block_e] % 2, n_used)
        ys = _moe_call(i, tables, xs, w_gate_up, b_gu, w_down, b_dn)
        ysg = ys[dest.transpose(1, 0, 2).reshape(-1)].reshape(TOP_K, N_TOK, D)
        xc, xl = _final_call(x1, ysg, rw, mods[i], g_post_ffn[i][None, :])

    y_prompt = xc.reshape(N_CTX_B, CTX_T, D)
    y_sample = xl.reshape(N_LAT_B, LAT_T, D)
    return (y_prompt, y_sample, jnp.stack(ckv_layers, axis=1), jnp.stack(krope_layers, axis=1))
```

```python
import functools
import math

import jax
import jax.numpy as jnp
import numpy as np
from jax import lax
from jax.experimental import pallas as pl
from jax.experimental.pallas import tpu as pltpu

F32 = jnp.float32
BF16 = jnp.bfloat16

D = 1024
N_CTX_B, CTX_T = 16, 256
N_LAT_B, LAT_T = 2, 1024
PAST = 512
N_CTX = N_CTX_B * CTX_T
N_LAT = N_LAT_B * LAT_T
N_TOK = N_CTX + N_LAT
DEPTH = 2
GRID_W = 64
EPS = 1e-6
A_HEADS, A_HEAD_DIM, A_WIDTH, CHUNK = 4, 64, 256, 128
F_GROUPS, F_GROUP_DIM, F_WIDTH = 4, 64, 256
HEADS, Q_LORA, KV_LORA, NOPE, ROPE, V_DIM = 4, 256, 128, 128, 64, 128
HEAD_PAD = 256
IN_PAD = 1280
N_EXPERTS, TOP_K, D_FF = 32, 4, 1024
SWIGLU_LIMIT, SWIGLU_ALPHA = 7.0, 1.702
ROPE_BASE = 10000.0

TB = 256
QB = 256
RB = 256
LANES = 128
SUB = D // LANES
N_BLOCKS = N_TOK * TOP_K // RB + N_EXPERTS
N_ROWS = N_BLOCKS * RB
NEG = -3.0e38
VMEM_LIMIT = 56 * 1024 * 1024


def _rms(x, g):
    return x * lax.rsqrt(jnp.mean(x * x, axis=-1, keepdims=True) + EPS) * g


def _split_dot(v, m):
    hi = v.astype(BF16)
    lo = (v - hi.astype(F32)).astype(BF16)
    return (jnp.dot(hi, m, preferred_element_type=F32)
            + jnp.dot(lo, m, preferred_element_type=F32))


def _dot_nt(a, b):
    return lax.dot_general(a, b, (((1,), (1,)), ((), ())), preferred_element_type=F32)


def _store_tiled(ref, x):
    rows = x.shape[0]
    for s in range(SUB):
        ref[pl.ds(s, rows, stride=SUB), :] = x[:, s * LANES:(s + 1) * LANES]


def _load_tiled(ref, rows):
    return jnp.concatenate([ref[pl.ds(s, rows, stride=SUB), :] for s in range(SUB)], axis=1)


def _tile_rows(ref, row, n_rows=1):
    return ref.at[pl.ds(pl.multiple_of(row * SUB, SUB), n_rows * SUB)]


def _mod_kernel(ct_ref, w_ref, b_ref, o_ref):
    ct = ct_ref[...]
    s = ct * jax.nn.sigmoid(ct)
    w = w_ref[...]
    o_ref[...] = jnp.zeros(o_ref.shape, F32)
    for r in range(3):
        o_ref[r:r + 1, :] = jnp.sum(w * s[:, r:r + 1], axis=0, keepdims=True) + b_ref[...]


def _mod_call(cond_t, w_ada, b_ada):
    cb = 512
    return pl.pallas_call(
        _mod_kernel,
        grid=(DEPTH, 6 * D // cb),
        in_specs=[
            pl.BlockSpec((D, 8), lambda l, j: (0, 0)),
            pl.BlockSpec((None, D, cb), lambda l, j: (l, 0, j)),
            pl.BlockSpec((None, 1, cb), lambda l, j: (l, 0, j)),
        ],
        out_specs=pl.BlockSpec((None, 8, cb), lambda l, j: (l, 0, j)),
        out_shape=jax.ShapeDtypeStruct((DEPTH, 8, 6 * D), F32),
        compiler_params=pltpu.CompilerParams(dimension_semantics=("parallel", "parallel")),
        name="modulation",
    )(cond_t, w_ada, b_ada)


def _mod_index(i):
    first_lat = N_CTX // TB
    return jnp.where(i < first_lat, 0, 1 + (i - first_lat) // (LAT_T // TB))


def _ctx_spec(width):
    return pl.BlockSpec((TB, width), lambda i: (jnp.minimum(i, N_CTX // TB - 1), 0))


def _lat_spec(width):
    return pl.BlockSpec((TB, width), lambda i: (jnp.maximum(i - N_CTX // TB, 0), 0))


def _pick(i, ctx_ref, lat_ref):
    return jnp.where(i >= N_CTX // TB, lat_ref[...], ctx_ref[...])


def _swap_halves(x, lane):
    w = x.shape[-1]
    fwd = pltpu.roll(x, w - 16, 1)
    bwd = pltpu.roll(x, 16, 1)
    return jnp.where((lane & 31) < 16, fwd, bwd)


def _pre_kernel(xc_ref, xl_ref, mod_ref, gpre_ref, win_ref, gsgu_ref, gmat_ref, gq_ref, wuq_ref,
                gkv_ref, wukv_ref, cos_ref, sin_ref,
                u_ref, vn_ref, zf_ref, q_ref, kv_ref, kr_ref, ckv_ref, zkr_ref):
    i = pl.program_id(0)
    is_lat = i >= N_CTX // TB
    x = _pick(i, xc_ref, xl_ref)
    h = _rms(x, gpre_ref[...]) * (1.0 + mod_ref[1:2, :]) + mod_ref[0:1, :]
    z = jnp.dot(h.astype(BF16), win_ref[...], preferred_element_type=F32)

    ga = jax.nn.gelu(z[:, :2 * A_WIDTH])
    u_ref[...] = ga[:, :A_WIDTH]
    v = ga[:, A_WIDTH:]
    gmat = gmat_ref[...]
    dv = v - _split_dot(v, gmat)
    var = _split_dot(dv * dv, gmat)
    vn_ref[...] = (dv * lax.rsqrt(var + EPS) * gsgu_ref[...]).astype(BF16)

    zf_ref[...] = z[:, 512:768].astype(BF16)

    cos = jnp.where(is_lat, cos_ref[...], 1.0)
    sin = jnp.where(is_lat, sin_ref[...], 0.0)
    lane = lax.broadcasted_iota(jnp.int32, (TB, LANES), 1)

    qn = _rms(z[:, 768:1024], gq_ref[...])
    scale = (NOPE + ROPE) ** -0.5
    q = jnp.dot(qn.astype(BF16), wuq_ref[...], preferred_element_type=F32) * scale
    for hd in range(HEADS):
        base = hd * HEAD_PAD
        q_ref[:, base:base + NOPE] = q[:, base:base + NOPE].astype(BF16)
        qr = q[:, base + NOPE:base + HEAD_PAD]
        q_ref[:, base + NOPE:base + HEAD_PAD] = (qr * cos + _swap_halves(qr, lane) * sin).astype(BF16)

    ckv = _rms(z[:, 1024:1152], gkv_ref[...])
    ckv_ref[...] = ckv
    kv_ref[...] = jnp.dot(ckv.astype(BF16), wukv_ref[...], preferred_element_type=F32).astype(BF16)

    zkr = z[:, 1152:1280]
    zkr_ref[...] = zkr
    kr_ref[...] = (zkr * cos + _swap_halves(zkr, lane) * sin).astype(BF16)


def _pre_call(xc, xl, mods, g_pre, w_in, g_sgu, gmat, g_q, w_uq, g_kv, w_ukv, cos_t, sin_t):
    nb = N_TOK // TB
    first_lat = N_CTX // TB
    pos_blocks = LAT_T // TB

    def tok(width):
        return pl.BlockSpec((TB, width), lambda i: (i, 0))

    def full(shape):
        return pl.BlockSpec(shape, lambda i: (0,) * len(shape))

    def rope_map(i):
        return (jnp.where(i >= first_lat, (i - first_lat) % pos_blocks, 0), 0)

    out_shape = (
        jax.ShapeDtypeStruct((N_TOK, A_WIDTH), F32),
        jax.ShapeDtypeStruct((N_TOK, A_WIDTH), BF16),
        jax.ShapeDtypeStruct((N_TOK, F_WIDTH), BF16),
        jax.ShapeDtypeStruct((N_TOK, HEADS * HEAD_PAD), BF16),
        jax.ShapeDtypeStruct((N_TOK, HEADS * (NOPE + V_DIM)), BF16),
        jax.ShapeDtypeStruct((N_TOK, LANES), BF16),
        jax.ShapeDtypeStruct((N_TOK, KV_LORA), F32),
        jax.ShapeDtypeStruct((N_TOK, LANES), F32),
    )
    return pl.pallas_call(
        _pre_kernel,
        grid=(nb,),
        in_specs=[
            _ctx_spec(D), _lat_spec(D),
            pl.BlockSpec((None, 6, D), lambda i: (_mod_index(i), 0, 0)),
            full((1, D)), full((D, IN_PAD)), full((1, A_WIDTH)), full((A_WIDTH, A_WIDTH)),
            full((1, Q_LORA)), full((Q_LORA, HEADS * HEAD_PAD)),
            full((1, KV_LORA)), full((KV_LORA, HEADS * (NOPE + V_DIM))),
            pl.BlockSpec((TB, LANES), rope_map), pl.BlockSpec((TB, LANES), rope_map),
        ],
        out_specs=(tok(A_WIDTH), tok(A_WIDTH), tok(F_WIDTH), tok(HEADS * HEAD_PAD),
                   tok(HEADS * (NOPE + V_DIM)), tok(LANES), tok(KV_LORA), tok(LANES)),
        out_shape=out_shape,
        compiler_params=pltpu.CompilerParams(dimension_semantics=("parallel",),
                                             vmem_limit_bytes=VMEM_LIMIT),
        name="pre_mix",
    )(xc, xl, mods, g_pre, w_in, g_sgu, gmat, g_q, w_uq, g_kv, w_ukv, cos_t, sin_t)


def _mix_kernel(*refs, has_cache):
    if has_cache:
        (u_ref, vn_ref, zf_ref, q_ref, kv_ref, kr_ref, ws_ref, bs_ref, bdc_ref, bds_ref,
         ct_ref, st_ref, cckv_ref, ckr_ref, wukv_ref, o_ref) = refs
    else:
        (u_ref, vn_ref, zf_ref, q_ref, kv_ref, kr_ref, ws_ref, bs_ref, bdc_ref, bds_ref,
         ct_ref, st_ref, o_ref) = refs

    lane = lax.broadcasted_iota(jnp.int32, (CHUNK, A_WIDTH), 1)
    for c in range(QB // CHUNK):
        rows = slice(c * CHUNK, (c + 1) * CHUNK)
        vch = vn_ref[rows, :]
        s = bs_ref[...]
        for g in range(A_HEADS):
            sg = jnp.dot(ws_ref[g], vch, preferred_element_type=F32)
            in_head = (lane >= g * A_HEAD_DIM) & (lane < (g + 1) * A_HEAD_DIM)
            s = s + jnp.where(in_head, sg, 0.0)
        o_ref[rows, 0:A_WIDTH] = (u_ref[rows, :] * s).astype(BF16)

    zf = zf_ref[...]
    zc = jnp.dot(zf, bdc_ref[...], preferred_element_type=F32).astype(BF16)
    zs = jnp.dot(zf, bds_ref[...], preferred_element_type=F32).astype(BF16)
    yf = (jnp.dot(ct_ref[...], zc, preferred_element_type=F32)
          - jnp.dot(st_ref[...], zs, preferred_element_type=F32))
    o_ref[:, A_WIDTH:A_WIDTH + F_WIDTH] = yf.astype(BF16)

    kr = kr_ref[...]
    if has_cache:
        kvc = jnp.dot(cckv_ref[...].astype(BF16), wukv_ref[...],
                      preferred_element_type=F32).astype(BF16)
        krc = ckr_ref[...]
    for hd in range(HEADS):
        qh = q_ref[:, hd * HEAD_PAD:(hd + 1) * HEAD_PAD]
        kb = hd * (NOPE + V_DIM)
        kh = jnp.concatenate([kv_ref[:, kb:kb + NOPE], kr], axis=1)
        vh = kv_ref[:, kb + NOPE:kb + NOPE + V_DIM]
        s = _dot_nt(qh, kh)
        m = jnp.max(s, axis=-1, keepdims=True)
        if has_cache:
            khc = jnp.concatenate([kvc[:, kb:kb + NOPE], krc], axis=1)
            vhc = kvc[:, kb + NOPE:kb + NOPE + V_DIM]
            sc = _dot_nt(qh, khc)
            m = jnp.maximum(m, jnp.max(sc, axis=-1, keepdims=True))
        e = jnp.exp(s - m)
        den = jnp.sum(e, axis=-1, keepdims=True)
        o = jnp.dot(e.astype(BF16), vh, preferred_element_type=F32)
        if has_cache:
            ec = jnp.exp(sc - m)
            den = den + jnp.sum(ec, axis=-1, keepdims=True)
            o = o + jnp.dot(ec.astype(BF16), vhc, preferred_element_type=F32)
        ob = A_WIDTH + F_WIDTH + hd * V_DIM
        o_ref[:, ob:ob + V_DIM] = (o * (1.0 / den)).astype(BF16)


def _mix_call(pre, consts, seq_t, n_batch, tok_off, cache=None):
    u, vn, zf, q, kv, kr = pre
    ws, bs_full, bdc, bds, ct, st = consts
    nq = seq_t // QB
    qoff = tok_off // QB
    soff = tok_off // seq_t

    def qrow(width):
        return pl.BlockSpec((QB, width), lambda b, j: (qoff + b * nq + j, 0))

    def srow(width):
        return pl.BlockSpec((seq_t, width), lambda b, j: (soff + b, 0))

    def full(shape):
        return pl.BlockSpec(shape, lambda b, j: (0,) * len(shape))

    in_specs = [qrow(A_WIDTH), qrow(A_WIDTH), srow(F_WIDTH), qrow(HEADS * HEAD_PAD),
                srow(HEADS * (NOPE + V_DIM)), srow(LANES),
                full((A_HEADS, CHUNK, CHUNK)), full((CHUNK, A_WIDTH)),
                full((F_WIDTH, F_WIDTH)), full((F_WIDTH, F_WIDTH)),
                pl.BlockSpec((QB, seq_t), lambda b, j: (j, 0)),
                pl.BlockSpec((QB, seq_t), lambda b, j: (j, 0))]
    args = [u, vn, zf, q, kv, kr, ws, bs_full, bdc, bds, ct, st]
    if cache is not None:
        cckv, ckr, wukv = cache
        in_specs += [pl.BlockSpec((None, PAST, KV_LORA), lambda b, j: (b, 0, 0)),
                     pl.BlockSpec((None, PAST, LANES), lambda b, j: (b, 0, 0)),
                     full((KV_LORA, HEADS * (NOPE + V_DIM)))]
        args += [cckv, ckr, wukv]
    return pl.pallas_call(
        functools.partial(_mix_kernel, has_cache=cache is not None),
        grid=(n_batch, nq),
        in_specs=in_specs,
        out_specs=pl.BlockSpec((QB, D), lambda b, j: (b * nq + j, 0)),
        out_shape=jax.ShapeDtypeStruct((n_batch * seq_t, D), BF16),
        compiler_params=pltpu.CompilerParams(dimension_semantics=("parallel", "parallel"),
                                             vmem_limit_bytes=VMEM_LIMIT),
        name="mix_lat" if cache is not None else "mix_ctx",
    )(*args)


def _post_kernel(yc_ref, yl_ref, xc_ref, xl_ref, mod_ref, wout_ref, gpost_ref, gffn_ref, wr_ref, br_ref,
                 tri_ref, x1_ref, h2_ref, tab_ref, rw_ref, cnt_ref, carry_ref):
    i = pl.program_id(0)

    @pl.when(i == 0)
    def _():
        carry_ref[...] = jnp.zeros(carry_ref.shape, F32)

    y = jnp.dot(_pick(i, yc_ref, yl_ref), wout_ref[...], preferred_element_type=F32)
    x1 = _pick(i, xc_ref, xl_ref) + mod_ref[2:3, :] * _rms(y, gpost_ref[...])
    x1_ref[...] = x1
    h2 = _rms(x1, gffn_ref[...]) * (1.0 + mod_ref[4:5, :]) + mod_ref[3:4, :]
    _store_tiled(h2_ref, h2)

    wr = wr_ref[...]
    wr_hi = wr.astype(BF16)
    wr_lo = (wr - wr_hi.astype(F32)).astype(BF16)
    h_hi = h2.astype(BF16)
    h_lo = (h2 - h_hi.astype(F32)).astype(BF16)
    logits = (jnp.dot(h_hi, wr_hi, preferred_element_type=F32)
              + jnp.dot(h_lo, wr_hi, preferred_element_type=F32)
              + jnp.dot(h_hi, wr_lo, preferred_element_type=F32)) + br_ref[...]

    lane = lax.broadcasted_iota(jnp.int32, (TB, LANES), 1)
    lane_f = lane.astype(F32)
    work = logits
    idx, val = [], []
    for _ in range(TOP_K):
        m = jnp.max(work, axis=-1, keepdims=True)
        ik = jnp.min(jnp.where(work == m, lane_f, float(LANES)), axis=-1, keepdims=True)
        idx.append(ik)
        val.append(m)
        work = jnp.where(lane_f == ik, NEG, work)
    ex = [jnp.exp(v - val[0]) for v in val]
    den = ex[0] + ex[1] + ex[2] + ex[3]

    onehot = jnp.zeros((TB, LANES), F32)
    for k in range(TOP_K):
        onehot = onehot + jnp.where(lane_f == idx[k], 1.0, 0.0)
    before = jnp.dot(tri_ref[...], onehot.astype(BF16), preferred_element_type=F32) + carry_ref[0:1, :]
    ri = jnp.zeros((TB, LANES), F32)
    rw = jnp.zeros((TB, LANES), F32)
    for k in range(TOP_K):
        rank_k = jnp.sum(jnp.where(lane_f == idx[k], before, 0.0), axis=-1, keepdims=True)
        ri = ri + jnp.where(lane == k, idx[k], 0.0) + jnp.where(lane == TOP_K + k, rank_k, 0.0)
        rw = rw + jnp.where(lane == k, ex[k] / den, 0.0)
    tab_ref[...] = ri.T[0:2 * TOP_K, :].astype(jnp.int32)
    rw_ref[...] = rw
    total = carry_ref[0:1, :] + jnp.sum(onehot, axis=0, keepdims=True)
    carry_ref[...] = jnp.broadcast_to(total, carry_ref.shape)
    cnt_ref[...] = jnp.broadcast_to(total, cnt_ref.shape).astype(jnp.int32)


def _post_call(yc, yl, xc, xl, mods, w_out, g_post, g_ffn, w_r, b_r, tri):
    nb = N_TOK // TB

    def tok(width):
        return pl.BlockSpec((TB, width), lambda i: (i, 0))

    def full(shape):
        return pl.BlockSpec(shape, lambda i: (0,) * len(shape))

    return pl.pallas_call(
        _post_kernel,
        grid=(nb,),
        in_specs=[_ctx_spec(D), _lat_spec(D), _ctx_spec(D), _lat_spec(D),
                  pl.BlockSpec((None, 6, D), lambda i: (_mod_index(i), 0, 0)),
                  full((D, D)), full((1, D)), full((1, D)), full((D, LANES)), full((1, LANES)),
                  full((TB, TB))],
        out_specs=(tok(D), pl.BlockSpec((TB * SUB, LANES), lambda i: (i, 0)),
                   pl.BlockSpec((None, 2 * TOP_K, TB), lambda i: (i, 0, 0)),
                   tok(LANES), full((8, LANES))),
        out_shape=(jax.ShapeDtypeStruct((N_TOK, D), F32),
                   jax.ShapeDtypeStruct((N_TOK * SUB, LANES), F32),
                   jax.ShapeDtypeStruct((nb, 2 * TOP_K, TB), jnp.int32),
                   jax.ShapeDtypeStruct((N_TOK, LANES), F32),
                   jax.ShapeDtypeStruct((8, LANES), jnp.int32)),
        scratch_shapes=[pltpu.VMEM((8, LANES), F32)],
        compiler_params=pltpu.CompilerParams(dimension_semantics=("arbitrary",),
                                             vmem_limit_bytes=VMEM_LIMIT),
        name="post_mix_router",
    )(yc, yl, xc, xl, mods, w_out, g_post, g_ffn, w_r, b_r, tri)


def _dest_kernel(bs_ref, tab_ref, o_ref):
    idx = tab_ref[:, 0:TOP_K, :]
    base = jnp.zeros(idx.shape, jnp.int32)
    for e in range(N_EXPERTS):
        base = jnp.where(idx == e, bs_ref[e] * RB, base)
    o_ref[...] = base + tab_ref[:, TOP_K:2 * TOP_K, :]


def _dest_call(blk_start, tab):
    nb = N_TOK // TB
    return pl.pallas_call(
        _dest_kernel,
        in_specs=[pl.BlockSpec(memory_space=pltpu.SMEM), pl.BlockSpec(memory_space=pltpu.VMEM)],
        out_specs=pl.BlockSpec(memory_space=pltpu.VMEM),
        out_shape=jax.ShapeDtypeStruct((nb, TOP_K, TB), jnp.int32),
        name="dest_rows",
    )(blk_start, tab)


def _dispatch_kernel(zrow_ref, nu_ref, dest_ref, h2_ref, xs_ref, zero_ref, sem, zsem):
    i = pl.program_id(0)

    def zero_fill(row):
        return pltpu.make_async_copy(zero_ref, _tile_rows(xs_ref, row, RB), zsem)

    @pl.when(i == 0)
    def _():
        zero_ref[...] = jnp.zeros(zero_ref.shape, F32)
        for e in range(N_EXPERTS):
            zero_fill(zrow_ref[e]).start()
        for e in range(N_EXPERTS):
            zero_fill(zrow_ref[e]).wait()

        def tail_start(b, carry):
            zero_fill(b * RB).start()
            return carry

        def tail_wait(b, carry):
            zero_fill(b * RB).wait()
            return carry

        lax.fori_loop(nu_ref[0], N_BLOCKS, tail_start, 0)
        lax.fori_loop(nu_ref[0], N_BLOCKS, tail_wait, 0)

    def body(t, carry):
        for k in range(TOP_K):
            pltpu.make_async_copy(_tile_rows(h2_ref, t), _tile_rows(xs_ref, dest_ref[k, t]),
                                  sem).start(priority=k % 2)
        return carry

    lax.fori_loop(0, TB, body, 0, unroll=8)
    for k in range(TOP_K):
        pltpu.make_async_copy(h2_ref, _tile_rows(xs_ref, 0, TB), sem).wait()


def _dispatch_call(zrow, n_used, dest, h2):
    nb = N_TOK // TB
    grid_spec = pltpu.PrefetchScalarGridSpec(
        num_scalar_prefetch=2,
        grid=(nb,),
        in_specs=[
            pl.BlockSpec((None, TOP_K, TB), lambda i, z, n: (i, 0, 0), memory_space=pltpu.SMEM),
            pl.BlockSpec((TB * SUB, LANES), lambda i, z, n: (i, 0)),
        ],
        out_specs=pl.BlockSpec(memory_space=pl.ANY),
        scratch_shapes=[pltpu.VMEM((RB * SUB, LANES), F32), pltpu.SemaphoreType.DMA,
                        pltpu.SemaphoreType.DMA],
    )
    return pl.pallas_call(
        _dispatch_kernel,
        grid_spec=grid_spec,
        out_shape=jax.ShapeDtypeStruct((N_ROWS * SUB, LANES), F32),
        compiler_params=pltpu.CompilerParams(dimension_semantics=("arbitrary",),
                                             vmem_limit_bytes=VMEM_LIMIT),
        name="dispatch_rows",
    )(zrow, n_used, dest, h2)


def _moe_kernel(be_ref, first_ref, nxt_ref, slot_ref, nu_ref, xs_ref, wgu_hbm, bgu_ref, wdn_hbm, bdn_ref,
                o_ref, wgu_f, wdn_f, wgu_s, wdn_s, sems, *, layer):
    b = pl.program_id(0)
    used = b < nu_ref[0]

    def fetch(e, s):
        return (pltpu.make_async_copy(wgu_hbm.at[layer, e], wgu_f.at[s], sems.at[0, s]),
                pltpu.make_async_copy(wdn_hbm.at[layer, e], wdn_f.at[s], sems.at[1, s]))

    @pl.when(b == 0)
    def _():
        for cp in fetch(be_ref[0], 0):
            cp.start()

    @pl.when(first_ref[b] == 1)
    def _():
        s = slot_ref[b]
        for cp in fetch(be_ref[b], s):
            cp.wait()

        @pl.when(nxt_ref[b] >= 0)
        def _():
            for cp in fetch(nxt_ref[b], 1 - s):
                cp.start()

        step = 128

        def body(c, carry):
            r = pl.multiple_of(c * step, step)
            wgu_s[pl.ds(r, step), :] = wgu_f[s, pl.ds(r, step), :].astype(BF16)
            wdn_s[pl.ds(r, step), :] = wdn_f[s, pl.ds(r, step), :].astype(BF16)
            return carry

        lax.fori_loop(0, D // step, body, 0)

    @pl.when(used)
    def _():
        x = _load_tiled(xs_ref, RB).astype(BF16)
        gu = jnp.dot(x, wgu_s[...], preferred_element_type=F32) + bgu_ref[...]
        g = jnp.minimum(gu[:, :D_FF], SWIGLU_LIMIT)
        l = jnp.clip(gu[:, D_FF:], -SWIGLU_LIMIT, SWIGLU_LIMIT)
        a = g * jax.nn.sigmoid(SWIGLU_ALPHA * g) * (l + 1.0)
        y = jnp.dot(a.astype(BF16), wdn_s[...], preferred_element_type=F32) + bdn_ref[...]
        o_ref[...] = y.astype(o_ref.dtype)

    @pl.when(jnp.logical_not(used))
    def _():
        o_ref[...] = jnp.zeros(o_ref.shape, o_ref.dtype)


def _moe_call(layer, tables, xs, w_gu, b_gu, w_dn, b_dn):
    def rows_in(b, be, fi, nx, sl, nu):
        return (jnp.minimum(b, nu[0] - 1), 0)

    def rows_out(b, be, fi, nx, sl, nu):
        return (b, 0)

    def expert(b, be, fi, nx, sl, nu):
        return (layer, be[b], 0, 0)

    grid_spec = pltpu.PrefetchScalarGridSpec(
        num_scalar_prefetch=5,
        grid=(N_BLOCKS,),
        in_specs=[
            pl.BlockSpec((RB * SUB, LANES), rows_in),
            pl.BlockSpec(memory_space=pl.ANY),
            pl.BlockSpec((None, None, 1, 2 * D_FF), expert),
            pl.BlockSpec(memory_space=pl.ANY),
            pl.BlockSpec((None, None, 1, D), expert),
        ],
        out_specs=pl.BlockSpec((RB, D), rows_out),
        scratch_shapes=[pltpu.VMEM((2, D, 2 * D_FF), F32), pltpu.VMEM((2, D_FF, D), F32),
                        pltpu.VMEM((D, 2 * D_FF), BF16), pltpu.VMEM((D_FF, D), BF16),
                        pltpu.SemaphoreType.DMA((2, 2))],
    )
    return pl.pallas_call(
        functools.partial(_moe_kernel, layer=layer),
        grid_spec=grid_spec,
        out_shape=jax.ShapeDtypeStruct((N_ROWS, D), BF16),
        compiler_params=pltpu.CompilerParams(dimension_semantics=("arbitrary",),
                                             vmem_limit_bytes=VMEM_LIMIT),
        name="moe_experts",
    )(*tables, xs, w_gu, b_gu, w_dn, b_dn)


def _final_kernel(x1_ref, ys_ref, rw_ref, mod_ref, g_ref, oc_ref, ol_ref):
    i = pl.program_id(0)
    rw = rw_ref[...]
    y = ys_ref[0].astype(F32) * rw[:, 0:1]
    for k in range(1, TOP_K):
        y = y + ys_ref[k].astype(F32) * rw[:, k:k + 1]
    x2 = x1_ref[...] + mod_ref[5:6, :] * _rms(y, g_ref[...])

    @pl.when(i < N_CTX // TB)
    def _():
        oc_ref[...] = x2

    @pl.when(i >= N_CTX // TB)
    def _():
        ol_ref[...] = x2


def _final_call(x1, ysg, rw, mods, g_post_ffn):
    nb = N_TOK // TB
    return pl.pallas_call(
        _final_kernel,
        grid=(nb,),
        in_specs=[pl.BlockSpec((TB, D), lambda i: (i, 0)),
                  pl.BlockSpec((TOP_K, TB, D), lambda i: (0, i, 0)),
                  pl.BlockSpec((TB, LANES), lambda i: (i, 0)),
                  pl.BlockSpec((None, 6, D), lambda i: (_mod_index(i), 0, 0)),
                  pl.BlockSpec((1, D), lambda i: (0, 0))],
        out_specs=(_ctx_spec(D), _lat_spec(D)),
        out_shape=(jax.ShapeDtypeStruct((N_CTX, D), F32), jax.ShapeDtypeStruct((N_LAT, D), F32)),
        compiler_params=pltpu.CompilerParams(dimension_semantics=("arbitrary",),
                                             vmem_limit_bytes=VMEM_LIMIT),
        name="combine_residual",
    )(x1, ysg, rw, mods, g_post_ffn)


def _dft_tables(t):
    j = np.arange(t, dtype=np.int64)
    ang = 2.0 * np.pi * ((j[:, None] * j[None, :]) % t) / t
    return (np.cos(ang) / math.sqrt(t)).astype(np.float32), (np.sin(ang) / math.sqrt(t)).astype(np.float32)


def _channel_dft():
    c = np.arange(F_GROUP_DIM, dtype=np.int64)
    ang = 2.0 * np.pi * ((c[:, None] * c[None, :]) % F_GROUP_DIM) / F_GROUP_DIM
    eye = np.eye(F_GROUPS)
    bdc = np.kron(eye, np.cos(ang)) / math.sqrt(F_GROUP_DIM)
    bds = np.kron(eye, np.sin(ang)) / math.sqrt(F_GROUP_DIM)
    return bdc.astype(np.float32), bds.astype(np.float32)


def _rope_tables():
    pos = np.arange(LAT_T)
    n = ROPE // 4
    inv_freq = np.power(np.float32(ROPE_BASE), -np.arange(n, dtype=np.float32) / np.float32(n))
    ang_r = (pos // GRID_W).astype(np.float32)[:, None] * inv_freq
    ang_c = (pos % GRID_W).astype(np.float32)[:, None] * inv_freq
    cos = np.concatenate([np.cos(ang_r), np.cos(ang_r), np.cos(ang_c), np.cos(ang_c),
                          np.ones((LAT_T, LANES - ROPE))], axis=1)
    sin = np.concatenate([-np.sin(ang_r), np.sin(ang_r), -np.sin(ang_c), np.sin(ang_c),
                          np.zeros((LAT_T, LANES - ROPE))], axis=1)
    return cos.astype(np.float32), sin.astype(np.float32)


def kernel(x_prompt, x_sample, cache_ckv, cache_krope, c, c_ctx, w_ada, b_ada, g_pre_mix, g_post_mix, g_pre_ffn, g_post_ffn, w_in, g_sgu, w_spatial, b_spatial, g_q, w_uq, g_kv, w_ukv, w_out, w_router, b_router, w_gate_up, b_gate_up, w_down, b_down):
    xc, xl = x_prompt.reshape(N_CTX, D), x_sample.reshape(N_LAT, D)

    cond = jnp.concatenate([c_ctx[None, :], c, jnp.zeros((5, D), F32)], axis=0)
    mods = _mod_call(cond.T, w_ada, b_ada.reshape(DEPTH, 1, 6 * D))
    mods = mods[:, :3].reshape(DEPTH, 3, 6, D)

    gmat = jnp.asarray(np.kron(np.eye(A_HEADS), np.full((A_HEAD_DIM, A_HEAD_DIM), 1.0 / A_HEAD_DIM)),
                       dtype=BF16)
    bdc_np, bds_np = _channel_dft()
    bdc, bds = jnp.asarray(bdc_np).astype(BF16), jnp.asarray(bds_np).astype(BF16)
    dft = {}
    for t in (CTX_T, LAT_T):
        ct_np, st_np = _dft_tables(t)
        dft[t] = (jnp.asarray(ct_np).astype(BF16), jnp.asarray(st_np).astype(BF16))
    cos_np, sin_np = _rope_tables()
    cos_t, sin_t = jnp.asarray(cos_np), jnp.asarray(sin_np)
    tri = jnp.asarray(np.tril(np.ones((TB, TB), np.float32), k=-1)).astype(BF16)

    b_gu = b_gate_up.reshape(DEPTH, N_EXPERTS, 1, 2 * D_FF)
    b_dn = b_down.reshape(DEPTH, N_EXPERTS, 1, D)

    ckv_layers, krope_layers = [], []
    for i in range(DEPTH):
        w_in_p = jnp.pad(w_in[i], ((0, 0), (0, IN_PAD - w_in.shape[-1]))).astype(BF16)
        w_uq_p = jnp.pad(w_uq[i].reshape(Q_LORA, HEADS, NOPE + ROPE),
                         ((0, 0), (0, 0), (0, HEAD_PAD - NOPE - ROPE))).reshape(Q_LORA, HEADS * HEAD_PAD)
        w_uq_p = w_uq_p.astype(BF16)
        w_ukv_b = w_ukv[i].astype(BF16)
        pre = _pre_call(xc, xl, mods[i], g_pre_mix[i][None, :], w_in_p, g_sgu[i][None, :], gmat,
                        g_q[i][None, :], w_uq_p, g_kv[i][None, :], w_ukv_b, cos_t, sin_t)
        u, vn, zf, q, kv, kr, ckv, zkr = pre
        ckv_layers.append(ckv[:N_CTX].reshape(N_CTX_B, CTX_T, KV_LORA))
        krope_layers.append(zkr[:N_CTX, :ROPE].reshape(N_CTX_B, CTX_T, ROPE))

        ws = w_spatial[i].astype(BF16)
        bs_full = jnp.repeat(b_spatial[i].T, A_HEAD_DIM, axis=1)
        mix_in = (u, vn, zf, q, kv, kr)
        y_ctx = _mix_call(mix_in, (ws, bs_full, bdc, bds) + dft[CTX_T], CTX_T, N_CTX_B, 0)
        ckr_p = jnp.pad(cache_krope[:, i], ((0, 0), (0, 0), (0, LANES - ROPE))).astype(BF16)
        y_lat = _mix_call(mix_in, (ws, bs_full, bdc, bds) + dft[LAT_T], LAT_T, N_LAT_B, N_CTX,
                          cache=(cache_ckv[:, i], ckr_p, w_ukv_b))

        w_r = jnp.pad(w_router[i], ((0, 0), (0, LANES - N_EXPERTS)))
        b_r = jnp.pad(b_router[i], (0, LANES - N_EXPERTS), constant_values=NEG)[None, :]
        x1, h2, tab, rw, cnt = _post_call(y_ctx, y_lat, xc, xl, mods[i], w_out[i].astype(BF16),
                                          g_post_mix[i][None, :], g_pre_ffn[i][None, :], w_r, b_r, tri)

        counts = cnt[0, :N_EXPERTS]
        nblk = (counts + RB - 1) // RB
        blk_end = jnp.cumsum(nblk)
        blk_start = blk_end - nblk
        n_used = blk_end[-1:]
        zrow = blk_start * RB + counts // 8 * 8
        dest = _dest_call(blk_start, tab)
        xs = _dispatch_call(zrow, n_used, dest, h2)
        blk_id = jnp.arange(N_BLOCKS, dtype=jnp.int32)
        blk = jnp.minimum(blk_id, n_used - 1)
        block_e = jnp.sum((blk_end[None, :] <= blk[:, None]).astype(jnp.int32), axis=1)
        block_e = jnp.minimum(block_e, N_EXPERTS - 1)
        eid = jnp.arange(N_EXPERTS, dtype=jnp.int32)
        owns = nblk > 0
        later = jnp.where((eid[None, :] > eid[:, None]) & owns[None, :], eid[None, :], N_EXPERTS)
        nxt_e = jnp.min(later, axis=1)
        nxt_e = jnp.where(nxt_e == N_EXPERTS, -1, nxt_e)
        run_id = jnp.cumsum(owns.astype(jnp.int32)) - 1
        first = ((blk_id == blk_start[block_e]) & (blk_id < n_used)).astype(jnp.int32)
        tables = (block_e, first, nxt_e[block_e], run_id[block_e] % 2, n_used)
        ys = _moe_call(i, tables, xs, w_gate_up, b_gu, w_down, b_dn)
        ysg = ys[dest.transpose(1, 0, 2).reshape(-1)].reshape(TOP_K, N_TOK, D)
        xc, xl = _final_call(x1, ysg, rw, mods[i], g_post_ffn[i][None, :])

    y_prompt = xc.reshape(N_CTX_B, CTX_T, D)
    y_sample = xl.reshape(N_LAT_B, LAT_T, D)
    return (y_prompt, y_sample, jnp.stack(ckv_layers, axis=1), jnp.stack(krope_layers, axis=1))
```

```python
import functools
import math

import jax
import jax.numpy as jnp
import numpy as np
from jax import lax
from jax.experimental import pallas as pl
from jax.experimental.pallas import tpu as pltpu

F32 = jnp.float32
BF16 = jnp.bfloat16

D = 1024
N_CTX_B, CTX_T = 16, 256
N_LAT_B, LAT_T = 2, 1024
PAST = 512
N_CTX = N_CTX_B * CTX_T
N_LAT = N_LAT_B * LAT_T
N_TOK = N_CTX + N_LAT
DEPTH = 2
GRID_W = 64
EPS = 1e-6
A_HEADS, A_HEAD_DIM, A_WIDTH, CHUNK = 4, 64, 256, 128
F_GROUPS, F_GROUP_DIM, F_WIDTH = 4, 64, 256
HEADS, Q_LORA, KV_LORA, NOPE, ROPE, V_DIM = 4, 256, 128, 128, 64, 128
HEAD_PAD = 256
IN_PAD = 1280
N_EXPERTS, TOP_K, D_FF = 32, 4, 1024
SWIGLU_LIMIT, SWIGLU_ALPHA = 7.0, 1.702
ROPE_BASE = 10000.0

TB = 256
QB = 256
RB = 256
LANES = 128
SUB = D // LANES
N_BLOCKS = N_TOK * TOP_K // RB + N_EXPERTS
N_ROWS = N_BLOCKS * RB
NEG = -3.0e38
VMEM_LIMIT = 56 * 1024 * 1024


def _rms(x, g):
    return x * lax.rsqrt(jnp.mean(x * x, axis=-1, keepdims=True) + EPS) * g


def _split_dot(v, m):
    hi = v.astype(BF16)
    lo = (v - hi.astype(F32)).astype(BF16)
    return (jnp.dot(hi, m, preferred_element_type=F32)
            + jnp.dot(lo, m, preferred_element_type=F32))


def _dot_nt(a, b):
    return lax.dot_general(a, b, (((1,), (1,)), ((), ())), preferred_element_type=F32)


def _store_tiled(ref, x):
    rows = x.shape[0]
    for s in range(SUB):
        ref[pl.ds(s, rows, stride=SUB), :] = x[:, s * LANES:(s + 1) * LANES]


def _load_tiled(ref, rows):
    return jnp.concatenate([ref[pl.ds(s, rows, stride=SUB), :] for s in range(SUB)], axis=1)


def _tile_rows(ref, row, n_rows=1):
    return ref.at[pl.ds(pl.multiple_of(row * SUB, SUB), n_rows * SUB)]


def _mod_kernel(ct_ref, w_ref, b_ref, o_ref):
    ct = ct_ref[...]
    s = ct * jax.nn.sigmoid(ct)
    w = w_ref[...]
    o_ref[...] = jnp.zeros(o_ref.shape, F32)
    for r in range(3):
        o_ref[r:r + 1, :] = jnp.sum(w * s[:, r:r + 1], axis=0, keepdims=True) + b_ref[...]


def _mod_call(cond_t, w_ada, b_ada):
    cb = 512
    return pl.pallas_call(
        _mod_kernel,
        grid=(DEPTH, 6 * D // cb),
        in_specs=[
            pl.BlockSpec((D, 8), lambda l, j: (0, 0)),
            pl.BlockSpec((None, D, cb), lambda l, j: (l, 0, j)),
            pl.BlockSpec((None, 1, cb), lambda l, j: (l, 0, j)),
        ],
        out_specs=pl.BlockSpec((None, 8, cb), lambda l, j: (l, 0, j)),
        out_shape=jax.ShapeDtypeStruct((DEPTH, 8, 6 * D), F32),
        compiler_params=pltpu.CompilerParams(dimension_semantics=("parallel", "parallel")),
        name="modulation",
    )(cond_t, w_ada, b_ada)


def _mod_index(i):
    first_lat = N_CTX // TB
    return jnp.where(i < first_lat, 0, 1 + (i - first_lat) // (LAT_T // TB))


def _ctx_spec(width):
    return pl.BlockSpec((TB, width), lambda i: (jnp.minimum(i, N_CTX // TB - 1), 0))


def _lat_spec(width):
    return pl.BlockSpec((TB, width), lambda i: (jnp.maximum(i - N_CTX // TB, 0), 0))


def _pick(i, ctx_ref, lat_ref):
    return jnp.where(i >= N_CTX // TB, lat_ref[...], ctx_ref[...])


def _swap_halves(x, lane):
    w = x.shape[-1]
    fwd = pltpu.roll(x, w - 16, 1)
    bwd = pltpu.roll(x, 16, 1)
    return jnp.where((lane & 31) < 16, fwd, bwd)


def _pre_kernel(xc_ref, xl_ref, mod_ref, gpre_ref, win_ref, gsgu_ref, gmat_ref, gq_ref, wuq_ref,
                gkv_ref, wukv_ref, cos_ref, sin_ref,
                u_ref, vn_ref, zf_ref, q_ref, kv_ref, kr_ref, ckv_ref, zkr_ref):
    i = pl.program_id(0)
    is_lat = i >= N_CTX // TB
    x = _pick(i, xc_ref, xl_ref)
    h = _rms(x, gpre_ref[...]) * (1.0 + mod_ref[1:2, :]) + mod_ref[0:1, :]
    z = jnp.dot(h.astype(BF16), win_ref[...], preferred_element_type=F32)

    ga = jax.nn.gelu(z[:, :2 * A_WIDTH])
    u_ref[...] = ga[:, :A_WIDTH]
    v = ga[:, A_WIDTH:]
    gmat = gmat_ref[...]
    dv = v - _split_dot(v, gmat)
    var = _split_dot(dv * dv, gmat)
    vn_ref[...] = (dv * lax.rsqrt(var + EPS) * gsgu_ref[...]).astype(BF16)

    zf_ref[...] = z[:, 512:768].astype(BF16)

    cos = jnp.where(is_lat, cos_ref[...], 1.0)
    sin = jnp.where(is_lat, sin_ref[...], 0.0)
    lane = lax.broadcasted_iota(jnp.int32, (TB, LANES), 1)

    qn = _rms(z[:, 768:1024], gq_ref[...])
    scale = (NOPE + ROPE) ** -0.5
    q = jnp.dot(qn.astype(BF16), wuq_ref[...], preferred_element_type=F32) * scale
    for hd in range(HEADS):
        base = hd * HEAD_PAD
        q_ref[:, base:base + NOPE] = q[:, base:base + NOPE].astype(BF16)
        qr = q[:, base + NOPE:base + HEAD_PAD]
        q_ref[:, base + NOPE:base + HEAD_PAD] = (qr * cos + _swap_halves(qr, lane) * sin).astype(BF16)

    ckv = _rms(z[:, 1024:1152], gkv_ref[...])
    ckv_ref[...] = ckv
    kv_ref[...] = jnp.dot(ckv.astype(BF16), wukv_ref[...], preferred_element_type=F32).astype(BF16)

    zkr = z[:, 1152:1280]
    zkr_ref[...] = zkr
    kr_ref[...] = (zkr * cos + _swap_halves(zkr, lane) * sin).astype(BF16)


def _pre_call(xc, xl, mods, g_pre, w_in, g_sgu, gmat, g_q, w_uq, g_kv, w_ukv, cos_t, sin_t):
    nb = N_TOK // TB
    first_lat = N_CTX // TB
    pos_blocks = LAT_T // TB

    def tok(width):
        return pl.BlockSpec((TB, width), lambda i: (i, 0))

    def full(shape):
        return pl.BlockSpec(shape, lambda i: (0,) * len(shape))

    def rope_map(i):
        return (jnp.where(i >= first_lat, (i - first_lat) % pos_blocks, 0), 0)

    out_shape = (
        jax.ShapeDtypeStruct((N_TOK, A_WIDTH), F32),
        jax.ShapeDtypeStruct((N_TOK, A_WIDTH), BF16),
        jax.ShapeDtypeStruct((N_TOK, F_WIDTH), BF16),
        jax.ShapeDtypeStruct((N_TOK, HEADS * HEAD_PAD), BF16),
        jax.ShapeDtypeStruct((N_TOK, HEADS * (NOPE + V_DIM)), BF16),
        jax.ShapeDtypeStruct((N_TOK, LANES), BF16),
        jax.ShapeDtypeStruct((N_TOK, KV_LORA), F32),
        jax.ShapeDtypeStruct((N_TOK, LANES), F32),
    )
    return pl.pallas_call(
        _pre_kernel,
        grid=(nb,),
        in_specs=[
            _ctx_spec(D), _lat_spec(D),
            pl.BlockSpec((None, 6, D), lambda i: (_mod_index(i), 0, 0)),
            full((1, D)), full((D, IN_PAD)), full((1, A_WIDTH)), full((A_WIDTH, A_WIDTH)),
            full((1, Q_LORA)), full((Q_LORA, HEADS * HEAD_PAD)),
            full((1, KV_LORA)), full((KV_LORA, HEADS * (NOPE + V_DIM))),
            pl.BlockSpec((TB, LANES), rope_map), pl.BlockSpec((TB, LANES), rope_map),
        ],
        out_specs=(tok(A_WIDTH), tok(A_WIDTH), tok(F_WIDTH), tok(HEADS * HEAD_PAD),
                   tok(HEADS * (NOPE + V_DIM)), tok(LANES), tok(KV_LORA), tok(LANES)),
        out_shape=out_shape,
        compiler_params=pltpu.CompilerParams(dimension_semantics=("parallel",),
                                             vmem_limit_bytes=VMEM_LIMIT),
        name="pre_mix",
    )(xc, xl, mods, g_pre, w_in, g_sgu, gmat, g_q, w_uq, g_kv, w_ukv, cos_t, sin_t)


def _mix_kernel(*refs, has_cache):
    if has_cache:
        (u_ref, vn_ref, zf_ref, q_ref, kv_ref, kr_ref, ws_ref, bs_ref, bdc_ref, bds_ref,
         ct_ref, st_ref, cckv_ref, ckr_ref, wukv_ref, o_ref) = refs
    else:
        (u_ref, vn_ref, zf_ref, q_ref, kv_ref, kr_ref, ws_ref, bs_ref, bdc_ref, bds_ref,
         ct_ref, st_ref, o_ref) = refs

    lane = lax.broadcasted_iota(jnp.int32, (CHUNK, A_WIDTH), 1)
    for c in range(QB // CHUNK):
        rows = slice(c * CHUNK, (c + 1) * CHUNK)
        vch = vn_ref[rows, :]
        s = bs_ref[...]
        for g in range(A_HEADS):
            sg = jnp.dot(ws_ref[g], vch, preferred_element_type=F32)
            in_head = (lane >= g * A_HEAD_DIM) & (lane < (g + 1) * A_HEAD_DIM)
            s = s + jnp.where(in_head, sg, 0.0)
        o_ref[rows, 0:A_WIDTH] = (u_ref[rows, :] * s).astype(BF16)

    zf = zf_ref[...]
    zc = jnp.dot(zf, bdc_ref[...], preferred_element_type=F32).astype(BF16)
    zs = jnp.dot(zf, bds_ref[...], preferred_element_type=F32).astype(BF16)
    yf = (jnp.dot(ct_ref[...], zc, preferred_element_type=F32)
          - jnp.dot(st_ref[...], zs, preferred_element_type=F32))
    o_ref[:, A_WIDTH:A_WIDTH + F_WIDTH] = yf.astype(BF16)

    kr = kr_ref[...]
    if has_cache:
        kvc = jnp.dot(cckv_ref[...].astype(BF16), wukv_ref[...],
                      preferred_element_type=F32).astype(BF16)
        krc = ckr_ref[...]
    for hd in range(HEADS):
        qh = q_ref[:, hd * HEAD_PAD:(hd + 1) * HEAD_PAD]
        kb = hd * (NOPE + V_DIM)
        kh = jnp.concatenate([kv_ref[:, kb:kb + NOPE], kr], axis=1)
        vh = kv_ref[:, kb + NOPE:kb + NOPE + V_DIM]
        s = _dot_nt(qh, kh)
        m = jnp.max(s, axis=-1, keepdims=True)
        if has_cache:
            khc = jnp.concatenate([kvc[:, kb:kb + NOPE], krc], axis=1)
            vhc = kvc[:, kb + NOPE:kb + NOPE + V_DIM]
            sc = _dot_nt(qh, khc)
            m = jnp.maximum(m, jnp.max(sc, axis=-1, keepdims=True))
        e = jnp.exp(s - m)
        den = jnp.sum(e, axis=-1, keepdims=True)
        o = jnp.dot(e.astype(BF16), vh, preferred_element_type=F32)
        if has_cache:
            ec = jnp.exp(sc - m)
            den = den + jnp.sum(ec, axis=-1, keepdims=True)
            o = o + jnp.dot(ec.astype(BF16), vhc, preferred_element_type=F32)
        ob = A_WIDTH + F_WIDTH + hd * V_DIM
        o_ref[:, ob:ob + V_DIM] = (o * (1.0 / den)).astype(BF16)


def _mix_call(pre, consts, seq_t, n_batch, tok_off, cache=None):
    u, vn, zf, q, kv, kr = pre
    ws, bs_full, bdc, bds, ct, st = consts
    nq = seq_t // QB
    qoff = tok_off // QB
    soff = tok_off // seq_t

    def qrow(width):
        return pl.BlockSpec((QB, width), lambda b, j: (qoff + b * nq + j, 0))

    def srow(width):
        return pl.BlockSpec((seq_t, width), lambda b, j: (soff + b, 0))

    def full(shape):
        return pl.BlockSpec(shape, lambda b, j: (0,) * len(shape))

    in_specs = [qrow(A_WIDTH), qrow(A_WIDTH), srow(F_WIDTH), qrow(HEADS * HEAD_PAD),
                srow(HEADS * (NOPE + V_DIM)), srow(LANES),
                full((A_HEADS, CHUNK, CHUNK)), full((CHUNK, A_WIDTH)),
                full((F_WIDTH, F_WIDTH)), full((F_WIDTH, F_WIDTH)),
                pl.BlockSpec((QB, seq_t), lambda b, j: (j, 0)),
                pl.BlockSpec((QB, seq_t), lambda b, j: (j, 0))]
    args = [u, vn, zf, q, kv, kr, ws, bs_full, bdc, bds, ct, st]
    if cache is not None:
        cckv, ckr, wukv = cache
        in_specs += [pl.BlockSpec((None, PAST, KV_LORA), lambda b, j: (b, 0, 0)),
                     pl.BlockSpec((None, PAST, LANES), lambda b, j: (b, 0, 0)),
                     full((KV_LORA, HEADS * (NOPE + V_DIM)))]
        args += [cckv, ckr, wukv]
    return pl.pallas_call(
        functools.partial(_mix_kernel, has_cache=cache is not None),
        grid=(n_batch, nq),
        in_specs=in_specs,
        out_specs=pl.BlockSpec((QB, D), lambda b, j: (b * nq + j, 0)),
        out_shape=jax.ShapeDtypeStruct((n_batch * seq_t, D), BF16),
        compiler_params=pltpu.CompilerParams(dimension_semantics=("parallel", "parallel"),
                                             vmem_limit_bytes=VMEM_LIMIT),
        name="mix_lat" if cache is not None else "mix_ctx",
    )(*args)


def _post_kernel(yc_ref, yl_ref, xc_ref, xl_ref, mod_ref, wout_ref, gpost_ref, gffn_ref, wr_ref, br_ref,
                 tri_ref, x1_ref, h2_ref, tab_ref, rw_ref, cnt_ref, carry_ref):
    i = pl.program_id(0)

    @pl.when(i == 0)
    def _():
        carry_ref[...] = jnp.zeros(carry_ref.shape, F32)

    y = jnp.dot(_pick(i, yc_ref, yl_ref), wout_ref[...], preferred_element_type=F32)
    x1 = _pick(i, xc_ref, xl_ref) + mod_ref[2:3, :] * _rms(y, gpost_ref[...])
    x1_ref[...] = x1
    h2 = _rms(x1, gffn_ref[...]) * (1.0 + mod_ref[4:5, :]) + mod_ref[3:4, :]
    _store_tiled(h2_ref, h2)

    wr = wr_ref[...]
    wr_hi = wr.astype(BF16)
    wr_lo = (wr - wr_hi.astype(F32)).astype(BF16)
    h_hi = h2.astype(BF16)
    h_lo = (h2 - h_hi.astype(F32)).astype(BF16)
    logits = (jnp.dot(h_hi, wr_hi, preferred_element_type=F32)
              + jnp.dot(h_lo, wr_hi, preferred_element_type=F32)
              + jnp.dot(h_hi, wr_lo, preferred_element_type=F32)) + br_ref[...]

    lane = lax.broadcasted_iota(jnp.int32, (TB, LANES), 1)
    lane_f = lane.astype(F32)
    work = logits
    idx, val = [], []
    for _ in range(TOP_K):
        m = jnp.max(work, axis=-1, keepdims=True)
        ik = jnp.min(jnp.where(work == m, lane_f, float(LANES)), axis=-1, keepdims=True)
        idx.append(ik)
        val.append(m)
        work = jnp.where(lane_f == ik, NEG, work)
    ex = [jnp.exp(v - val[0]) for v in val]
    den = ex[0] + ex[1] + ex[2] + ex[3]

    onehot = jnp.zeros((TB, LANES), F32)
    for k in range(TOP_K):
        onehot = onehot + jnp.where(lane_f == idx[k], 1.0, 0.0)
    before = jnp.dot(tri_ref[...], onehot.astype(BF16), preferred_element_type=F32) + carry_ref[0:1, :]
    ri = jnp.zeros((TB, LANES), F32)
    rw = jnp.zeros((TB, LANES), F32)
    for k in range(TOP_K):
        rank_k = jnp.sum(jnp.where(lane_f == idx[k], before, 0.0), axis=-1, keepdims=True)
        ri = ri + jnp.where(lane == k, idx[k], 0.0) + jnp.where(lane == TOP_K + k, rank_k, 0.0)
        rw = rw + jnp.where(lane == k, ex[k] / den, 0.0)
    tab_ref[...] = ri.T[0:2 * TOP_K, :].astype(jnp.int32)
    rw_ref[...] = rw
    total = carry_ref[0:1, :] + jnp.sum(onehot, axis=0, keepdims=True)
    carry_ref[...] = jnp.broadcast_to(total, carry_ref.shape)
    cnt_ref[...] = jnp.broadcast_to(total, cnt_ref.shape).astype(jnp.int32)


def _post_call(yc, yl, xc, xl, mods, w_out, g_post, g_ffn, w_r, b_r, tri):
    nb = N_TOK // TB

    def tok(width):
        return pl.BlockSpec((TB, width), lambda i: (i, 0))

    def full(shape):
        return pl.BlockSpec(shape, lambda i: (0,) * len(shape))

    return pl.pallas_call(
        _post_kernel,
        grid=(nb,),
        in_specs=[_ctx_spec(D), _lat_spec(D), _ctx_spec(D), _lat_spec(D),
                  pl.BlockSpec((None, 6, D), lambda i: (_mod_index(i), 0, 0)),
                  full((D, D)), full((1, D)), full((1, D)), full((D, LANES)), full((1, LANES)),
                  full((TB, TB))],
        out_specs=(tok(D), pl.BlockSpec((TB * SUB, LANES), lambda i: (i, 0)),
                   pl.BlockSpec((None, 2 * TOP_K, TB), lambda i: (i, 0, 0)),
                   tok(LANES), full((8, LANES))),
        out_shape=(jax.ShapeDtypeStruct((N_TOK, D), F32),
                   jax.ShapeDtypeStruct((N_TOK * SUB, LANES), F32),
                   jax.ShapeDtypeStruct((nb, 2 * TOP_K, TB), jnp.int32),
                   jax.ShapeDtypeStruct((N_TOK, LANES), F32),
                   jax.ShapeDtypeStruct((8, LANES), jnp.int32)),
        scratch_shapes=[pltpu.VMEM((8, LANES), F32)],
        compiler_params=pltpu.CompilerParams(dimension_semantics=("arbitrary",),
                                             vmem_limit_bytes=VMEM_LIMIT),
        name="post_mix_router",
    )(yc, yl, xc, xl, mods, w_out, g_post, g_ffn, w_r, b_r, tri)


def _dest_kernel(bs_ref, tab_ref, o_ref):
    idx = tab_ref[:, 0:TOP_K, :]
    base = jnp.zeros(idx.shape, jnp.int32)
    for e in range(N_EXPERTS):
        base = jnp.where(idx == e, bs_ref[e] * RB, base)
    o_ref[...] = base + tab_ref[:, TOP_K:2 * TOP_K, :]


def _dest_call(blk_start, tab):
    nb = N_TOK // TB
    return pl.pallas_call(
        _dest_kernel,
        in_specs=[pl.BlockSpec(memory_space=pltpu.SMEM), pl.BlockSpec(memory_space=pltpu.VMEM)],
        out_specs=pl.BlockSpec(memory_space=pltpu.VMEM),
        out_shape=jax.ShapeDtypeStruct((nb, TOP_K, TB), jnp.int32),
        name="dest_rows",
    )(blk_start, tab)


def _dispatch_kernel(zrow_ref, nu_ref, dest_ref, h2_ref, xs_ref, zero_ref, sem, zsem):
    i = pl.program_id(0)

    def zero_fill(row):
        return pltpu.make_async_copy(zero_ref, _tile_rows(xs_ref, row, RB), zsem)

    @pl.when(i == 0)
    def _():
        zero_ref[...] = jnp.zeros(zero_ref.shape, F32)
        for e in range(N_EXPERTS):
            zero_fill(zrow_ref[e]).start()
        for e in range(N_EXPERTS):
            zero_fill(zrow_ref[e]).wait()

        def tail_start(b, carry):
            zero_fill(b * RB).start()
            return carry

        def tail_wait(b, carry):
            zero_fill(b * RB).wait()
            return carry

        lax.fori_loop(nu_ref[0], N_BLOCKS, tail_start, 0)
        lax.fori_loop(nu_ref[0], N_BLOCKS, tail_wait, 0)

    def body(t, carry):
        for k in range(TOP_K):
            pltpu.make_async_copy(_tile_rows(h2_ref, t), _tile_rows(xs_ref, dest_ref[0, k * TB + t]),
                                  sem).start(priority=k % 2)
        return carry

    lax.fori_loop(0, TB, body, 0, unroll=8)
    for k in range(TOP_K):
        pltpu.make_async_copy(h2_ref, _tile_rows(xs_ref, 0, TB), sem).wait()


def _dispatch_call(zrow, n_used, dest, h2):
    nb = N_TOK // TB
    grid_spec = pltpu.PrefetchScalarGridSpec(
        num_scalar_prefetch=2,
        grid=(nb,),
        in_specs=[
            pl.BlockSpec((None, 1, TOP_K * TB), lambda i, z, n: (i, 0, 0), memory_space=pltpu.SMEM),
            pl.BlockSpec((TB * SUB, LANES), lambda i, z, n: (i, 0)),
        ],
        out_specs=pl.BlockSpec(memory_space=pl.ANY),
        scratch_shapes=[pltpu.VMEM((RB * SUB, LANES), F32), pltpu.SemaphoreType.DMA,
                        pltpu.SemaphoreType.DMA],
    )
    return pl.pallas_call(
        _dispatch_kernel,
        grid_spec=grid_spec,
        out_shape=jax.ShapeDtypeStruct((N_ROWS * SUB, LANES), F32),
        compiler_params=pltpu.CompilerParams(dimension_semantics=("arbitrary",),
                                             vmem_limit_bytes=VMEM_LIMIT),
        name="dispatch_rows",
    )(zrow, n_used, dest, h2)


def _moe_kernel(be_ref, first_ref, nxt_ref, slot_ref, nu_ref, xs_ref, wgu_hbm, bgu_ref, wdn_hbm, bdn_ref,
                o_ref, wgu_f, wdn_f, wgu_s, wdn_s, sems, *, layer):
    b = pl.program_id(0)
    used = b < nu_ref[0]

    def fetch(e, s):
        return (pltpu.make_async_copy(wgu_hbm.at[layer, e], wgu_f.at[s], sems.at[0, s]),
                pltpu.make_async_copy(wdn_hbm.at[layer, e], wdn_f.at[s], sems.at[1, s]))

    @pl.when(b == 0)
    def _():
        for cp in fetch(be_ref[0], 0):
            cp.start()

    @pl.when(first_ref[b] == 1)
    def _():
        s = slot_ref[b]
        for cp in fetch(be_ref[b], s):
            cp.wait()

        @pl.when(nxt_ref[b] >= 0)
        def _():
            for cp in fetch(nxt_ref[b], 1 - s):
                cp.start()

        step = 128

        def body(c, carry):
            r = pl.multiple_of(c * step, step)
            wgu_s[pl.ds(r, step), :] = wgu_f[s, pl.ds(r, step), :].astype(BF16)
            wdn_s[pl.ds(r, step), :] = wdn_f[s, pl.ds(r, step), :].astype(BF16)
            return carry

        lax.fori_loop(0, D // step, body, 0)

    @pl.when(used)
    def _():
        x = _load_tiled(xs_ref, RB).astype(BF16)
        gu = jnp.dot(x, wgu_s[...], preferred_element_type=F32) + bgu_ref[...]
        g = jnp.minimum(gu[:, :D_FF], SWIGLU_LIMIT)
        l = jnp.clip(gu[:, D_FF:], -SWIGLU_LIMIT, SWIGLU_LIMIT)
        a = g * jax.nn.sigmoid(SWIGLU_ALPHA * g) * (l + 1.0)
        y = jnp.dot(a.astype(BF16), wdn_s[...], preferred_element_type=F32) + bdn_ref[...]
        _store_tiled(o_ref, y)

    @pl.when(jnp.logical_not(used))
    def _():
        o_ref[...] = jnp.zeros(o_ref.shape, o_ref.dtype)


def _moe_call(layer, tables, xs, w_gu, b_gu, w_dn, b_dn):
    def rows_in(b, be, fi, nx, sl, nu):
        return (jnp.minimum(b, nu[0] - 1), 0)

    def rows_out(b, be, fi, nx, sl, nu):
        return (b, 0)

    def expert(b, be, fi, nx, sl, nu):
        return (layer, be[b], 0, 0)

    grid_spec = pltpu.PrefetchScalarGridSpec(
        num_scalar_prefetch=5,
        grid=(N_BLOCKS,),
        in_specs=[
            pl.BlockSpec((RB * SUB, LANES), rows_in),
            pl.BlockSpec(memory_space=pl.ANY),
            pl.BlockSpec((None, None, 1, 2 * D_FF), expert),
            pl.BlockSpec(memory_space=pl.ANY),
            pl.BlockSpec((None, None, 1, D), expert),
        ],
        out_specs=pl.BlockSpec((RB * SUB, LANES), rows_out),
        scratch_shapes=[pltpu.VMEM((2, D, 2 * D_FF), F32), pltpu.VMEM((2, D_FF, D), F32),
                        pltpu.VMEM((D, 2 * D_FF), BF16), pltpu.VMEM((D_FF, D), BF16),
                        pltpu.SemaphoreType.DMA((2, 2))],
    )
    return pl.pallas_call(
        functools.partial(_moe_kernel, layer=layer),
        grid_spec=grid_spec,
        out_shape=jax.ShapeDtypeStruct((N_ROWS * SUB, LANES), F32),
        compiler_params=pltpu.CompilerParams(dimension_semantics=("arbitrary",),
                                             vmem_limit_bytes=VMEM_LIMIT),
        name="moe_experts",
    )(*tables, xs, w_gu, b_gu, w_dn, b_dn)


def _final_kernel(dcur_ref, dnxt_ref, x1_ref, ys_ref, rw_ref, mod_ref, g_ref, oc_ref, ol_ref, buf, sems):
    i = pl.program_id(0)
    nb = pl.num_programs(0)
    slot = i % 2

    def gather(dest_ref, s):
        def body(t, carry):
            for k in range(TOP_K):
                pltpu.make_async_copy(_tile_rows(ys_ref, dest_ref[0, k * TB + t]),
                                      _tile_rows(buf.at[s], k * TB + t), sems.at[s]).start(priority=k % 2)
            return carry

        lax.fori_loop(0, TB, body, 0, unroll=8)

    @pl.when(i == 0)
    def _():
        gather(dcur_ref, 0)

    @pl.when(i + 1 < nb)
    def _():
        gather(dnxt_ref, 1 - slot)

    pltpu.make_async_copy(_tile_rows(ys_ref, 0, TOP_K * TB), buf.at[slot], sems.at[slot]).wait()

    rw = rw_ref[...]
    y = jnp.zeros((TB, D), F32)
    for k in range(TOP_K):
        yk = jnp.concatenate([buf[slot, pl.ds(k * TB * SUB + s, TB, stride=SUB), :] for s in range(SUB)],
                             axis=1)
        y = y + yk * rw[:, k:k + 1]
    x2 = x1_ref[...] + mod_ref[5:6, :] * _rms(y, g_ref[...])

    @pl.when(i < N_CTX // TB)
    def _():
        oc_ref[...] = x2

    @pl.when(i >= N_CTX // TB)
    def _():
        ol_ref[...] = x2


def _final_call(dest_flat, x1, ys, rw, mods, g_post_ffn):
    nb = N_TOK // TB
    return pl.pallas_call(
        _final_kernel,
        grid=(nb,),
        in_specs=[pl.BlockSpec((None, 1, TOP_K * TB), lambda i: (i, 0, 0), memory_space=pltpu.SMEM),
                  pl.BlockSpec((None, 1, TOP_K * TB), lambda i: (jnp.minimum(i + 1, nb - 1), 0, 0),
                               memory_space=pltpu.SMEM),
                  pl.BlockSpec((TB, D), lambda i: (i, 0)),
                  pl.BlockSpec(memory_space=pl.ANY),
                  pl.BlockSpec((TB, LANES), lambda i: (i, 0)),
                  pl.BlockSpec((None, 6, D), lambda i: (_mod_index(i), 0, 0)),
                  pl.BlockSpec((1, D), lambda i: (0, 0))],
        out_specs=(_ctx_spec(D), _lat_spec(D)),
        out_shape=(jax.ShapeDtypeStruct((N_CTX, D), F32), jax.ShapeDtypeStruct((N_LAT, D), F32)),
        scratch_shapes=[pltpu.VMEM((2, TOP_K * TB * SUB, LANES), F32), pltpu.SemaphoreType.DMA((2,))],
        compiler_params=pltpu.CompilerParams(dimension_semantics=("arbitrary",),
                                             vmem_limit_bytes=VMEM_LIMIT),
        name="combine_residual",
    )(dest_flat, dest_flat, x1, ys, rw, mods, g_post_ffn)


def _dft_tables(t):
    j = np.arange(t, dtype=np.int64)
    ang = 2.0 * np.pi * ((j[:, None] * j[None, :]) % t) / t
    return (np.cos(ang) / math.sqrt(t)).astype(np.float32), (np.sin(ang) / math.sqrt(t)).astype(np.float32)


def _channel_dft():
    c = np.arange(F_GROUP_DIM, dtype=np.int64)
    ang = 2.0 * np.pi * ((c[:, None] * c[None, :]) % F_GROUP_DIM) / F_GROUP_DIM
    eye = np.eye(F_GROUPS)
    bdc = np.kron(eye, np.cos(ang)) / math.sqrt(F_GROUP_DIM)
    bds = np.kron(eye, np.sin(ang)) / math.sqrt(F_GROUP_DIM)
    return bdc.astype(np.float32), bds.astype(np.float32)


def _rope_tables():
    pos = np.arange(LAT_T)
    n = ROPE // 4
    inv_freq = np.power(np.float32(ROPE_BASE), -np.arange(n, dtype=np.float32) / np.float32(n))
    ang_r = (pos // GRID_W).astype(np.float32)[:, None] * inv_freq
    ang_c = (pos % GRID_W).astype(np.float32)[:, None] * inv_freq
    cos = np.concatenate([np.cos(ang_r), np.cos(ang_r), np.cos(ang_c), np.cos(ang_c),
                          np.ones((LAT_T, LANES - ROPE))], axis=1)
    sin = np.concatenate([-np.sin(ang_r), np.sin(ang_r), -np.sin(ang_c), np.sin(ang_c),
                          np.zeros((LAT_T, LANES - ROPE))], axis=1)
    return cos.astype(np.float32), sin.astype(np.float32)


def kernel(x_prompt, x_sample, cache_ckv, cache_krope, c, c_ctx, w_ada, b_ada, g_pre_mix, g_post_mix, g_pre_ffn, g_post_ffn, w_in, g_sgu, w_spatial, b_spatial, g_q, w_uq, g_kv, w_ukv, w_out, w_router, b_router, w_gate_up, b_gate_up, w_down, b_down):
    xc, xl = x_prompt.reshape(N_CTX, D), x_sample.reshape(N_LAT, D)

    cond = jnp.concatenate([c_ctx[None, :], c, jnp.zeros((5, D), F32)], axis=0)
    mods = _mod_call(cond.T, w_ada, b_ada.reshape(DEPTH, 1, 6 * D))
    mods = mods[:, :3].reshape(DEPTH, 3, 6, D)

    gmat = jnp.asarray(np.kron(np.eye(A_HEADS), np.full((A_HEAD_DIM, A_HEAD_DIM), 1.0 / A_HEAD_DIM)),
                       dtype=BF16)
    bdc_np, bds_np = _channel_dft()
    bdc, bds = jnp.asarray(bdc_np).astype(BF16), jnp.asarray(bds_np).astype(BF16)
    dft = {}
    for t in (CTX_T, LAT_T):
        ct_np, st_np = _dft_tables(t)
        dft[t] = (jnp.asarray(ct_np).astype(BF16), jnp.asarray(st_np).astype(BF16))
    cos_np, sin_np = _rope_tables()
    cos_t, sin_t = jnp.asarray(cos_np), jnp.asarray(sin_np)
    tri = jnp.asarray(np.tril(np.ones((TB, TB), np.float32), k=-1)).astype(BF16)

    b_gu = b_gate_up.reshape(DEPTH, N_EXPERTS, 1, 2 * D_FF)
    b_dn = b_down.reshape(DEPTH, N_EXPERTS, 1, D)

    ckv_layers, krope_layers = [], []
    for i in range(DEPTH):
        w_in_p = jnp.pad(w_in[i], ((0, 0), (0, IN_PAD - w_in.shape[-1]))).astype(BF16)
        w_uq_p = jnp.pad(w_uq[i].reshape(Q_LORA, HEADS, NOPE + ROPE),
                         ((0, 0), (0, 0), (0, HEAD_PAD - NOPE - ROPE))).reshape(Q_LORA, HEADS * HEAD_PAD)
        w_uq_p = w_uq_p.astype(BF16)
        w_ukv_b = w_ukv[i].astype(BF16)
        pre = _pre_call(xc, xl, mods[i], g_pre_mix[i][None, :], w_in_p, g_sgu[i][None, :], gmat,
                        g_q[i][None, :], w_uq_p, g_kv[i][None, :], w_ukv_b, cos_t, sin_t)
        u, vn, zf, q, kv, kr, ckv, zkr = pre
        ckv_layers.append(ckv[:N_CTX].reshape(N_CTX_B, CTX_T, KV_LORA))
        krope_layers.append(zkr[:N_CTX, :ROPE].reshape(N_CTX_B, CTX_T, ROPE))

        ws = w_spatial[i].astype(BF16)
        bs_full = jnp.repeat(b_spatial[i].T, A_HEAD_DIM, axis=1)
        mix_in = (u, vn, zf, q, kv, kr)
        y_ctx = _mix_call(mix_in, (ws, bs_full, bdc, bds) + dft[CTX_T], CTX_T, N_CTX_B, 0)
        ckr_p = jnp.pad(cache_krope[:, i], ((0, 0), (0, 0), (0, LANES - ROPE))).astype(BF16)
        y_lat = _mix_call(mix_in, (ws, bs_full, bdc, bds) + dft[LAT_T], LAT_T, N_LAT_B, N_CTX,
                          cache=(cache_ckv[:, i], ckr_p, w_ukv_b))

        w_r = jnp.pad(w_router[i], ((0, 0), (0, LANES - N_EXPERTS)))
        b_r = jnp.pad(b_router[i], (0, LANES - N_EXPERTS), constant_values=NEG)[None, :]
        x1, h2, tab, rw, cnt = _post_call(y_ctx, y_lat, xc, xl, mods[i], w_out[i].astype(BF16),
                                          g_post_mix[i][None, :], g_pre_ffn[i][None, :], w_r, b_r, tri)

        counts = cnt[0, :N_EXPERTS]
        nblk = (counts + RB - 1) // RB
        blk_end = jnp.cumsum(nblk)
        blk_start = blk_end - nblk
        n_used = blk_end[-1:]
        zrow = blk_start * RB + counts // 8 * 8
        dest = _dest_call(blk_start, tab)
        dest_flat = dest.reshape(N_TOK // TB, 1, TOP_K * TB)
        xs = _dispatch_call(zrow, n_used, dest_flat, h2)
        blk_id = jnp.arange(N_BLOCKS, dtype=jnp.int32)
        blk = jnp.minimum(blk_id, n_used - 1)
        block_e = jnp.sum((blk_end[None, :] <= blk[:, None]).astype(jnp.int32), axis=1)
        block_e = jnp.minimum(block_e, N_EXPERTS - 1)
        eid = jnp.arange(N_EXPERTS, dtype=jnp.int32)
        owns = nblk > 0
        later = jnp.where((eid[None, :] > eid[:, None]) & owns[None, :], eid[None, :], N_EXPERTS)
        nxt_e = jnp.min(later, axis=1)
        nxt_e = jnp.where(nxt_e == N_EXPERTS, -1, nxt_e)
        run_id = jnp.cumsum(owns.astype(jnp.int32)) - 1
        first = ((blk_id == blk_start[block_e]) & (blk_id < n_used)).astype(jnp.int32)
        tables = (block_e, first, nxt_e[block_e], run_id[block_e] % 2, n_used)
        ys = _moe_call(i, tables, xs, w_gate_up, b_gu, w_down, b_dn)
        xc, xl = _final_call(dest_flat, x1, ys, rw, mods[i], g_post_ffn[i][None, :])

    y_prompt = xc.reshape(N_CTX_B, CTX_T, D)
    y_sample = xl.reshape(N_LAT_B, LAT_T, D)
    return (y_prompt, y_sample, jnp.stack(ckv_layers, axis=1), jnp.stack(krope_layers, axis=1))
```

```python
import functools
import math

import jax
import jax.numpy as jnp
import numpy as np
from jax import lax
from jax.experimental import pallas as pl
from jax.experimental.pallas import tpu as pltpu

F32 = jnp.float32
BF16 = jnp.bfloat16

D = 1024
N_CTX_B, CTX_T = 16, 256
N_LAT_B, LAT_T = 2, 1024
PAST = 512
N_CTX = N_CTX_B * CTX_T
N_LAT = N_LAT_B * LAT_T
N_TOK = N_CTX + N_LAT
DEPTH = 2
GRID_W = 64
EPS = 1e-6
A_HEADS, A_HEAD_DIM, A_WIDTH, CHUNK = 4, 64, 256, 128
F_GROUPS, F_GROUP_DIM, F_WIDTH = 4, 64, 256
HEADS, Q_LORA, KV_LORA, NOPE, ROPE, V_DIM = 4, 256, 128, 128, 64, 128
HEAD_PAD = 256
IN_PAD = 1280
N_EXPERTS, TOP_K, D_FF = 32, 4, 1024
SWIGLU_LIMIT, SWIGLU_ALPHA = 7.0, 1.702
ROPE_BASE = 10000.0

TB = 256
QB = 256
RB = 256
LANES = 128
SUB = D // LANES
N_BLOCKS = N_TOK * TOP_K // RB + N_EXPERTS
N_ROWS = N_BLOCKS * RB
NEG = -3.0e38
VMEM_LIMIT = 56 * 1024 * 1024


def _rms(x, g):
    return x * lax.rsqrt(jnp.mean(x * x, axis=-1, keepdims=True) + EPS) * g


def _split_dot(v, m):
    hi = v.astype(BF16)
    lo = (v - hi.astype(F32)).astype(BF16)
    return (jnp.dot(hi, m, preferred_element_type=F32)
            + jnp.dot(lo, m, preferred_element_type=F32))


def _dot_nt(a, b):
    return lax.dot_general(a, b, (((1,), (1,)), ((), ())), preferred_element_type=F32)


def _store_tiled(ref, x):
    rows = x.shape[0]
    for s in range(SUB):
        ref[pl.ds(s, rows, stride=SUB), :] = x[:, s * LANES:(s + 1) * LANES]


def _load_tiled(ref, rows):
    return jnp.concatenate([ref[pl.ds(s, rows, stride=SUB), :] for s in range(SUB)], axis=1)


def _tile_rows(ref, row, n_rows=1):
    return ref.at[pl.ds(pl.multiple_of(row * SUB, SUB), n_rows * SUB)]


def _mod_kernel(ct_ref, w_ref, b_ref, o_ref):
    ct = ct_ref[...]
    s = ct * jax.nn.sigmoid(ct)
    w = w_ref[...]
    o_ref[...] = jnp.zeros(o_ref.shape, F32)
    for r in range(3):
        o_ref[r:r + 1, :] = jnp.sum(w * s[:, r:r + 1], axis=0, keepdims=True) + b_ref[...]


def _mod_call(cond_t, w_ada, b_ada):
    cb = 512
    return pl.pallas_call(
        _mod_kernel,
        grid=(DEPTH, 6 * D // cb),
        in_specs=[
            pl.BlockSpec((D, 8), lambda l, j: (0, 0)),
            pl.BlockSpec((None, D, cb), lambda l, j: (l, 0, j)),
            pl.BlockSpec((None, 1, cb), lambda l, j: (l, 0, j)),
        ],
        out_specs=pl.BlockSpec((None, 8, cb), lambda l, j: (l, 0, j)),
        out_shape=jax.ShapeDtypeStruct((DEPTH, 8, 6 * D), F32),
        compiler_params=pltpu.CompilerParams(dimension_semantics=("parallel", "parallel")),
        name="modulation",
    )(cond_t, w_ada, b_ada)


def _mod_index(i):
    first_lat = N_CTX // TB
    return jnp.where(i < first_lat, 0, 1 + (i - first_lat) // (LAT_T // TB))


def _ctx_spec(width):
    return pl.BlockSpec((TB, width), lambda i: (jnp.minimum(i, N_CTX // TB - 1), 0))


def _lat_spec(width):
    return pl.BlockSpec((TB, width), lambda i: (jnp.maximum(i - N_CTX // TB, 0), 0))


def _pick(i, ctx_ref, lat_ref):
    return jnp.where(i >= N_CTX // TB, lat_ref[...], ctx_ref[...])


def _swap_halves(x, lane):
    w = x.shape[-1]
    fwd = pltpu.roll(x, w - 16, 1)
    bwd = pltpu.roll(x, 16, 1)
    return jnp.where((lane & 31) < 16, fwd, bwd)


def _pre_kernel(xc_ref, xl_ref, mod_ref, gpre_ref, win_ref, gsgu_ref, gmat_ref, gq_ref, wuq_ref,
                gkv_ref, wukv_ref, cos_ref, sin_ref,
                u_ref, vn_ref, zf_ref, q_ref, kv_ref, kr_ref, ckv_ref, zkr_ref):
    i = pl.program_id(0)
    is_lat = i >= N_CTX // TB
    x = _pick(i, xc_ref, xl_ref)
    h = _rms(x, gpre_ref[...]) * (1.0 + mod_ref[1:2, :]) + mod_ref[0:1, :]
    z = jnp.dot(h.astype(BF16), win_ref[...], preferred_element_type=F32)

    ga = jax.nn.gelu(z[:, :2 * A_WIDTH])
    u_ref[...] = ga[:, :A_WIDTH]
    v = ga[:, A_WIDTH:]
    gmat = gmat_ref[...]
    dv = v - _split_dot(v, gmat)
    var = _split_dot(dv * dv, gmat)
    vn_ref[...] = (dv * lax.rsqrt(var + EPS) * gsgu_ref[...]).astype(BF16)

    zf_ref[...] = z[:, 512:768].astype(BF16)

    cos = jnp.where(is_lat, cos_ref[...], 1.0)
    sin = jnp.where(is_lat, sin_ref[...], 0.0)
    lane = lax.broadcasted_iota(jnp.int32, (TB, LANES), 1)

    qn = _rms(z[:, 768:1024], gq_ref[...])
    scale = (NOPE + ROPE) ** -0.5
    q = jnp.dot(qn.astype(BF16), wuq_ref[...], preferred_element_type=F32) * scale
    for hd in range(HEADS):
        base = hd * HEAD_PAD
        q_ref[:, base:base + NOPE] = q[:, base:base + NOPE].astype(BF16)
        qr = q[:, base + NOPE:base + HEAD_PAD]
        q_ref[:, base + NOPE:base + HEAD_PAD] = (qr * cos + _swap_halves(qr, lane) * sin).astype(BF16)

    ckv = _rms(z[:, 1024:1152], gkv_ref[...])
    ckv_ref[...] = ckv
    kv_ref[...] = jnp.dot(ckv.astype(BF16), wukv_ref[...], preferred_element_type=F32).astype(BF16)

    zkr = z[:, 1152:1280]
    zkr_ref[...] = zkr
    kr_ref[...] = (zkr * cos + _swap_halves(zkr, lane) * sin).astype(BF16)


def _pre_call(xc, xl, mods, g_pre, w_in, g_sgu, gmat, g_q, w_uq, g_kv, w_ukv, cos_t, sin_t):
    nb = N_TOK // TB
    first_lat = N_CTX // TB
    pos_blocks = LAT_T // TB

    def tok(width):
        return pl.BlockSpec((TB, width), lambda i: (i, 0))

    def full(shape):
        return pl.BlockSpec(shape, lambda i: (0,) * len(shape))

    def rope_map(i):
        return (jnp.where(i >= first_lat, (i - first_lat) % pos_blocks, 0), 0)

    out_shape = (
        jax.ShapeDtypeStruct((N_TOK, A_WIDTH), F32),
        jax.ShapeDtypeStruct((N_TOK, A_WIDTH), BF16),
        jax.ShapeDtypeStruct((N_TOK, F_WIDTH), BF16),
        jax.ShapeDtypeStruct((N_TOK, HEADS * HEAD_PAD), BF16),
        jax.ShapeDtypeStruct((N_TOK, HEADS * (NOPE + V_DIM)), BF16),
        jax.ShapeDtypeStruct((N_TOK, LANES), BF16),
        jax.ShapeDtypeStruct((N_TOK, KV_LORA), F32),
        jax.ShapeDtypeStruct((N_TOK, LANES), F32),
    )
    return pl.pallas_call(
        _pre_kernel,
        grid=(nb,),
        in_specs=[
            _ctx_spec(D), _lat_spec(D),
            pl.BlockSpec((None, 6, D), lambda i: (_mod_index(i), 0, 0)),
            full((1, D)), full((D, IN_PAD)), full((1, A_WIDTH)), full((A_WIDTH, A_WIDTH)),
            full((1, Q_LORA)), full((Q_LORA, HEADS * HEAD_PAD)),
            full((1, KV_LORA)), full((KV_LORA, HEADS * (NOPE + V_DIM))),
            pl.BlockSpec((TB, LANES), rope_map), pl.BlockSpec((TB, LANES), rope_map),
        ],
        out_specs=(tok(A_WIDTH), tok(A_WIDTH), tok(F_WIDTH), tok(HEADS * HEAD_PAD),
                   tok(HEADS * (NOPE + V_DIM)), tok(LANES), tok(KV_LORA), tok(LANES)),
        out_shape=out_shape,
        compiler_params=pltpu.CompilerParams(dimension_semantics=("parallel",),
                                             vmem_limit_bytes=VMEM_LIMIT),
        name="pre_mix",
    )(xc, xl, mods, g_pre, w_in, g_sgu, gmat, g_q, w_uq, g_kv, w_ukv, cos_t, sin_t)


def _mix_kernel(*refs, has_cache):
    if has_cache:
        (u_ref, vn_ref, zf_ref, q_ref, kv_ref, kr_ref, ws_ref, bs_ref, bdc_ref, bds_ref,
         ct_ref, st_ref, cckv_ref, ckr_ref, wukv_ref, o_ref) = refs
    else:
        (u_ref, vn_ref, zf_ref, q_ref, kv_ref, kr_ref, ws_ref, bs_ref, bdc_ref, bds_ref,
         ct_ref, st_ref, o_ref) = refs

    lane = lax.broadcasted_iota(jnp.int32, (CHUNK, A_WIDTH), 1)
    for c in range(QB // CHUNK):
        rows = slice(c * CHUNK, (c + 1) * CHUNK)
        vch = vn_ref[rows, :]
        s = bs_ref[...]
        for g in range(A_HEADS):
            sg = jnp.dot(ws_ref[g], vch, preferred_element_type=F32)
            in_head = (lane >= g * A_HEAD_DIM) & (lane < (g + 1) * A_HEAD_DIM)
            s = s + jnp.where(in_head, sg, 0.0)
        o_ref[rows, 0:A_WIDTH] = (u_ref[rows, :] * s).astype(BF16)

    zf = zf_ref[...]
    zc = jnp.dot(zf, bdc_ref[...], preferred_element_type=F32).astype(BF16)
    zs = jnp.dot(zf, bds_ref[...], preferred_element_type=F32).astype(BF16)
    yf = (jnp.dot(ct_ref[...], zc, preferred_element_type=F32)
          - jnp.dot(st_ref[...], zs, preferred_element_type=F32))
    o_ref[:, A_WIDTH:A_WIDTH + F_WIDTH] = yf.astype(BF16)

    kr = kr_ref[...]
    if has_cache:
        kvc = jnp.dot(cckv_ref[...].astype(BF16), wukv_ref[...],
                      preferred_element_type=F32).astype(BF16)
        krc = ckr_ref[...]
    for hd in range(HEADS):
        qh = q_ref[:, hd * HEAD_PAD:(hd + 1) * HEAD_PAD]
        kb = hd * (NOPE + V_DIM)
        kh = jnp.concatenate([kv_ref[:, kb:kb + NOPE], kr], axis=1)
        vh = kv_ref[:, kb + NOPE:kb + NOPE + V_DIM]
        s = _dot_nt(qh, kh)
        m = jnp.max(s, axis=-1, keepdims=True)
        if has_cache:
            khc = jnp.concatenate([kvc[:, kb:kb + NOPE], krc], axis=1)
            vhc = kvc[:, kb + NOPE:kb + NOPE + V_DIM]
            sc = _dot_nt(qh, khc)
            m = jnp.maximum(m, jnp.max(sc, axis=-1, keepdims=True))
        e = jnp.exp(s - m)
        den = jnp.sum(e, axis=-1, keepdims=True)
        o = jnp.dot(e.astype(BF16), vh, preferred_element_type=F32)
        if has_cache:
            ec = jnp.exp(sc - m)
            den = den + jnp.sum(ec, axis=-1, keepdims=True)
            o = o + jnp.dot(ec.astype(BF16), vhc, preferred_element_type=F32)
        ob = A_WIDTH + F_WIDTH + hd * V_DIM
        o_ref[:, ob:ob + V_DIM] = (o * (1.0 / den)).astype(BF16)


def _mix_call(pre, consts, seq_t, n_batch, tok_off, cache=None):
    u, vn, zf, q, kv, kr = pre
    ws, bs_full, bdc, bds, ct, st = consts
    nq = seq_t // QB
    qoff = tok_off // QB
    soff = tok_off // seq_t

    def qrow(width):
        return pl.BlockSpec((QB, width), lambda b, j: (qoff + b * nq + j, 0))

    def srow(width):
        return pl.BlockSpec((seq_t, width), lambda b, j: (soff + b, 0))

    def full(shape):
        return pl.BlockSpec(shape, lambda b, j: (0,) * len(shape))

    in_specs = [qrow(A_WIDTH), qrow(A_WIDTH), srow(F_WIDTH), qrow(HEADS * HEAD_PAD),
                srow(HEADS * (NOPE + V_DIM)), srow(LANES),
                full((A_HEADS, CHUNK, CHUNK)), full((CHUNK, A_WIDTH)),
                full((F_WIDTH, F_WIDTH)), full((F_WIDTH, F_WIDTH)),
                pl.BlockSpec((QB, seq_t), lambda b, j: (j, 0)),
                pl.BlockSpec((QB, seq_t), lambda b, j: (j, 0))]
    args = [u, vn, zf, q, kv, kr, ws, bs_full, bdc, bds, ct, st]
    if cache is not None:
        cckv, ckr, wukv = cache
        in_specs += [pl.BlockSpec((None, PAST, KV_LORA), lambda b, j: (b, 0, 0)),
                     pl.BlockSpec((None, PAST, LANES), lambda b, j: (b, 0, 0)),
                     full((KV_LORA, HEADS * (NOPE + V_DIM)))]
        args += [cckv, ckr, wukv]
    return pl.pallas_call(
        functools.partial(_mix_kernel, has_cache=cache is not None),
        grid=(n_batch, nq),
        in_specs=in_specs,
        out_specs=pl.BlockSpec((QB, D), lambda b, j: (b * nq + j, 0)),
        out_shape=jax.ShapeDtypeStruct((n_batch * seq_t, D), BF16),
        compiler_params=pltpu.CompilerParams(dimension_semantics=("parallel", "parallel"),
                                             vmem_limit_bytes=VMEM_LIMIT),
        name="mix_lat" if cache is not None else "mix_ctx",
    )(*args)


def _post_kernel(yc_ref, yl_ref, xc_ref, xl_ref, mod_ref, wout_ref, gpost_ref, gffn_ref, wr_ref, br_ref,
                 tri_ref, x1_ref, h2_ref, tab_ref, rw_ref, cnt_ref, carry_ref):
    i = pl.program_id(0)

    @pl.when(i == 0)
    def _():
        carry_ref[...] = jnp.zeros(carry_ref.shape, F32)

    y = jnp.dot(_pick(i, yc_ref, yl_ref), wout_ref[...], preferred_element_type=F32)
    x1 = _pick(i, xc_ref, xl_ref) + mod_ref[2:3, :] * _rms(y, gpost_ref[...])
    x1_ref[...] = x1
    h2 = _rms(x1, gffn_ref[...]) * (1.0 + mod_ref[4:5, :]) + mod_ref[3:4, :]
    _store_tiled(h2_ref, h2)

    wr = wr_ref[...]
    wr_hi = wr.astype(BF16)
    wr_lo = (wr - wr_hi.astype(F32)).astype(BF16)
    h_hi = h2.astype(BF16)
    h_lo = (h2 - h_hi.astype(F32)).astype(BF16)
    logits = (jnp.dot(h_hi, wr_hi, preferred_element_type=F32)
              + jnp.dot(h_lo, wr_hi, preferred_element_type=F32)
              + jnp.dot(h_hi, wr_lo, preferred_element_type=F32)) + br_ref[...]

    lane = lax.broadcasted_iota(jnp.int32, (TB, LANES), 1)
    lane_f = lane.astype(F32)
    work = logits
    idx, val = [], []
    for _ in range(TOP_K):
        m = jnp.max(work, axis=-1, keepdims=True)
        ik = jnp.min(jnp.where(work == m, lane_f, float(LANES)), axis=-1, keepdims=True)
        idx.append(ik)
        val.append(m)
        work = jnp.where(lane_f == ik, NEG, work)
    ex = [jnp.exp(v - val[0]) for v in val]
    den = ex[0] + ex[1] + ex[2] + ex[3]

    onehot = jnp.zeros((TB, LANES), F32)
    for k in range(TOP_K):
        onehot = onehot + jnp.where(lane_f == idx[k], 1.0, 0.0)
    before = jnp.dot(tri_ref[...], onehot.astype(BF16), preferred_element_type=F32) + carry_ref[0:1, :]
    ri = jnp.zeros((TB, LANES), F32)
    rw = jnp.zeros((TB, LANES), F32)
    for k in range(TOP_K):
        rank_k = jnp.sum(jnp.where(lane_f == idx[k], before, 0.0), axis=-1, keepdims=True)
        ri = ri + jnp.where(lane == k, idx[k], 0.0) + jnp.where(lane == TOP_K + k, rank_k, 0.0)
        rw = rw + jnp.where(lane == k, ex[k] / den, 0.0)
    tab_ref[...] = ri.T[0:2 * TOP_K, :].astype(jnp.int32)
    rw_ref[...] = rw
    total = carry_ref[0:1, :] + jnp.sum(onehot, axis=0, keepdims=True)
    carry_ref[...] = jnp.broadcast_to(total, carry_ref.shape)
    cnt_ref[...] = jnp.broadcast_to(total, cnt_ref.shape).astype(jnp.int32)


def _post_call(yc, yl, xc, xl, mods, w_out, g_post, g_ffn, w_r, b_r, tri):
    nb = N_TOK // TB

    def tok(width):
        return pl.BlockSpec((TB, width), lambda i: (i, 0))

    def full(shape):
        return pl.BlockSpec(shape, lambda i: (0,) * len(shape))

    return pl.pallas_call(
        _post_kernel,
        grid=(nb,),
        in_specs=[_ctx_spec(D), _lat_spec(D), _ctx_spec(D), _lat_spec(D),
                  pl.BlockSpec((None, 6, D), lambda i: (_mod_index(i), 0, 0)),
                  full((D, D)), full((1, D)), full((1, D)), full((D, LANES)), full((1, LANES)),
                  full((TB, TB))],
        out_specs=(tok(D), pl.BlockSpec((TB * SUB, LANES), lambda i: (i, 0)),
                   pl.BlockSpec((None, 2 * TOP_K, TB), lambda i: (i, 0, 0)),
                   tok(LANES), full((8, LANES))),
        out_shape=(jax.ShapeDtypeStruct((N_TOK, D), F32),
                   jax.ShapeDtypeStruct((N_TOK * SUB, LANES), F32),
                   jax.ShapeDtypeStruct((nb, 2 * TOP_K, TB), jnp.int32),
                   jax.ShapeDtypeStruct((N_TOK, LANES), F32),
                   jax.ShapeDtypeStruct((8, LANES), jnp.int32)),
        scratch_shapes=[pltpu.VMEM((8, LANES), F32)],
        compiler_params=pltpu.CompilerParams(dimension_semantics=("arbitrary",),
                                             vmem_limit_bytes=VMEM_LIMIT),
        name="post_mix_router",
    )(yc, yl, xc, xl, mods, w_out, g_post, g_ffn, w_r, b_r, tri)


def _dest_kernel(bs_ref, tab_ref, o_ref):
    idx = tab_ref[:, 0:TOP_K, :]
    base = jnp.zeros(idx.shape, jnp.int32)
    for e in range(N_EXPERTS):
        base = jnp.where(idx == e, bs_ref[e] * RB, base)
    o_ref[...] = base + tab_ref[:, TOP_K:2 * TOP_K, :]


def _dest_call(blk_start, tab):
    nb = N_TOK // TB
    return pl.pallas_call(
        _dest_kernel,
        in_specs=[pl.BlockSpec(memory_space=pltpu.SMEM), pl.BlockSpec(memory_space=pltpu.VMEM)],
        out_specs=pl.BlockSpec(memory_space=pltpu.VMEM),
        out_shape=jax.ShapeDtypeStruct((nb, TOP_K, TB), jnp.int32),
        name="dest_rows",
    )(blk_start, tab)


def _dispatch_kernel(zrow_ref, nu_ref, dest_ref, h2_ref, xs_ref, zero_ref, sem, zsem):
    i = pl.program_id(0)

    def zero_fill(row):
        return pltpu.make_async_copy(zero_ref, _tile_rows(xs_ref, row, RB), zsem)

    @pl.when(i == 0)
    def _():
        zero_ref[...] = jnp.zeros(zero_ref.shape, F32)
        for e in range(N_EXPERTS):
            zero_fill(zrow_ref[e]).start()
        for e in range(N_EXPERTS):
            zero_fill(zrow_ref[e]).wait()

        def tail_start(b, carry):
            zero_fill(b * RB).start()
            return carry

        def tail_wait(b, carry):
            zero_fill(b * RB).wait()
            return carry

        lax.fori_loop(nu_ref[0], N_BLOCKS, tail_start, 0)
        lax.fori_loop(nu_ref[0], N_BLOCKS, tail_wait, 0)

    def body(t, carry):
        for k in range(TOP_K):
            pltpu.make_async_copy(_tile_rows(h2_ref, t), _tile_rows(xs_ref, dest_ref[0, k * TB + t]),
                                  sem).start(priority=k % 2)
        return carry

    lax.fori_loop(0, TB, body, 0, unroll=8)
    for k in range(TOP_K):
        pltpu.make_async_copy(h2_ref, _tile_rows(xs_ref, 0, TB), sem).wait()


def _dispatch_call(zrow, n_used, dest, h2):
    nb = N_TOK // TB
    grid_spec = pltpu.PrefetchScalarGridSpec(
        num_scalar_prefetch=2,
        grid=(nb,),
        in_specs=[
            pl.BlockSpec((None, 1, TOP_K * TB), lambda i, z, n: (i, 0, 0), memory_space=pltpu.SMEM),
            pl.BlockSpec((TB * SUB, LANES), lambda i, z, n: (i, 0)),
        ],
        out_specs=pl.BlockSpec(memory_space=pl.ANY),
        scratch_shapes=[pltpu.VMEM((RB * SUB, LANES), F32), pltpu.SemaphoreType.DMA,
                        pltpu.SemaphoreType.DMA],
    )
    return pl.pallas_call(
        _dispatch_kernel,
        grid_spec=grid_spec,
        out_shape=jax.ShapeDtypeStruct((N_ROWS * SUB, LANES), F32),
        compiler_params=pltpu.CompilerParams(dimension_semantics=("arbitrary",),
                                             vmem_limit_bytes=VMEM_LIMIT),
        name="dispatch_rows",
    )(zrow, n_used, dest, h2)


def _moe_kernel(be_ref, first_ref, nxt_ref, slot_ref, nu_ref, xs_ref, wgu_hbm, bgu_ref, wdn_hbm, bdn_ref,
                o_ref, wgu_f, wdn_f, wgu_s, wdn_s, sems, *, layer):
    b = pl.program_id(0)
    used = b < nu_ref[0]

    def fetch(e, s):
        return (pltpu.make_async_copy(wgu_hbm.at[layer, e], wgu_f.at[s], sems.at[0, s]),
                pltpu.make_async_copy(wdn_hbm.at[layer, e], wdn_f.at[s], sems.at[1, s]))

    @pl.when(b == 0)
    def _():
        for cp in fetch(be_ref[0], 0):
            cp.start(priority=1)

    @pl.when(first_ref[b] == 1)
    def _():
        s = slot_ref[b]
        for cp in fetch(be_ref[b], s):
            cp.wait()

        @pl.when(nxt_ref[b] >= 0)
        def _():
            for cp in fetch(nxt_ref[b], 1 - s):
                cp.start(priority=1)

        step = 128

        def body(c, carry):
            r = pl.multiple_of(c * step, step)
            wgu_s[pl.ds(r, step), :] = wgu_f[s, pl.ds(r, step), :].astype(BF16)
            wdn_s[pl.ds(r, step), :] = wdn_f[s, pl.ds(r, step), :].astype(BF16)
            return carry

        lax.fori_loop(0, D // step, body, 0)

    @pl.when(used)
    def _():
        x = _load_tiled(xs_ref, RB).astype(BF16)
        gu = jnp.dot(x, wgu_s[...], preferred_element_type=F32) + bgu_ref[...]
        g = jnp.minimum(gu[:, :D_FF], SWIGLU_LIMIT)
        l = jnp.clip(gu[:, D_FF:], -SWIGLU_LIMIT, SWIGLU_LIMIT)
        a = g * jax.nn.sigmoid(SWIGLU_ALPHA * g) * (l + 1.0)
        y = jnp.dot(a.astype(BF16), wdn_s[...], preferred_element_type=F32) + bdn_ref[...]
        _store_tiled(o_ref, y)

    @pl.when(jnp.logical_not(used))
    def _():
        o_ref[...] = jnp.zeros(o_ref.shape, o_ref.dtype)


def _moe_call(layer, tables, xs, w_gu, b_gu, w_dn, b_dn):
    def rows_in(b, be, fi, nx, sl, nu):
        return (jnp.minimum(b, nu[0] - 1), 0)

    def rows_out(b, be, fi, nx, sl, nu):
        return (b, 0)

    def expert(b, be, fi, nx, sl, nu):
        return (layer, be[b], 0, 0)

    grid_spec = pltpu.PrefetchScalarGridSpec(
        num_scalar_prefetch=5,
        grid=(N_BLOCKS,),
        in_specs=[
            pl.BlockSpec((RB * SUB, LANES), rows_in),
            pl.BlockSpec(memory_space=pl.ANY),
            pl.BlockSpec((None, None, 1, 2 * D_FF), expert),
            pl.BlockSpec(memory_space=pl.ANY),
            pl.BlockSpec((None, None, 1, D), expert),
        ],
        out_specs=pl.BlockSpec((RB * SUB, LANES), rows_out),
        scratch_shapes=[pltpu.VMEM((2, D, 2 * D_FF), F32), pltpu.VMEM((2, D_FF, D), F32),
                        pltpu.VMEM((D, 2 * D_FF), BF16), pltpu.VMEM((D_FF, D), BF16),
                        pltpu.SemaphoreType.DMA((2, 2))],
    )
    return pl.pallas_call(
        functools.partial(_moe_kernel, layer=layer),
        grid_spec=grid_spec,
        out_shape=jax.ShapeDtypeStruct((N_ROWS * SUB, LANES), F32),
        compiler_params=pltpu.CompilerParams(dimension_semantics=("arbitrary",),
                                             vmem_limit_bytes=VMEM_LIMIT),
        name="moe_experts",
    )(*tables, xs, w_gu, b_gu, w_dn, b_dn)


def _final_kernel(dcur_ref, dnxt_ref, x1_ref, ys_ref, rw_ref, mod_ref, g_ref, oc_ref, ol_ref, buf, sems):
    i = pl.program_id(0)
    nb = pl.num_programs(0)
    slot = i % 2

    def gather(dest_ref, s):
        def body(t, carry):
            for k in range(TOP_K):
                pltpu.make_async_copy(_tile_rows(ys_ref, dest_ref[0, k * TB + t]),
                                      _tile_rows(buf.at[s], k * TB + t), sems.at[s]).start(priority=k % 2)
            return carry

        lax.fori_loop(0, TB, body, 0, unroll=8)

    @pl.when(i == 0)
    def _():
        gather(dcur_ref, 0)

    @pl.when(i + 1 < nb)
    def _():
        gather(dnxt_ref, 1 - slot)

    pltpu.make_async_copy(_tile_rows(ys_ref, 0, TOP_K * TB), buf.at[slot], sems.at[slot]).wait()

    rw = rw_ref[...]
    y = jnp.zeros((TB, D), F32)
    for k in range(TOP_K):
        yk = jnp.concatenate([buf[slot, pl.ds(k * TB * SUB + s, TB, stride=SUB), :] for s in range(SUB)],
                             axis=1)
        y = y + yk * rw[:, k:k + 1]
    x2 = x1_ref[...] + mod_ref[5:6, :] * _rms(y, g_ref[...])

    @pl.when(i < N_CTX // TB)
    def _():
        oc_ref[...] = x2

    @pl.when(i >= N_CTX // TB)
    def _():
        ol_ref[...] = x2


def _final_call(dest_flat, x1, ys, rw, mods, g_post_ffn):
    nb = N_TOK // TB
    return pl.pallas_call(
        _final_kernel,
        grid=(nb,),
        in_specs=[pl.BlockSpec((None, 1, TOP_K * TB), lambda i: (i, 0, 0), memory_space=pltpu.SMEM),
                  pl.BlockSpec((None, 1, TOP_K * TB), lambda i: (jnp.minimum(i + 1, nb - 1), 0, 0),
                               memory_space=pltpu.SMEM),
                  pl.BlockSpec((TB, D), lambda i: (i, 0)),
                  pl.BlockSpec(memory_space=pl.ANY),
                  pl.BlockSpec((TB, LANES), lambda i: (i, 0)),
                  pl.BlockSpec((None, 6, D), lambda i: (_mod_index(i), 0, 0)),
                  pl.BlockSpec((1, D), lambda i: (0, 0))],
        out_specs=(_ctx_spec(D), _lat_spec(D)),
        out_shape=(jax.ShapeDtypeStruct((N_CTX, D), F32), jax.ShapeDtypeStruct((N_LAT, D), F32)),
        scratch_shapes=[pltpu.VMEM((2, TOP_K * TB * SUB, LANES), F32), pltpu.SemaphoreType.DMA((2,))],
        compiler_params=pltpu.CompilerParams(dimension_semantics=("arbitrary",),
                                             vmem_limit_bytes=VMEM_LIMIT),
        name="combine_residual",
    )(dest_flat, dest_flat, x1, ys, rw, mods, g_post_ffn)


def _dft_tables(t):
    j = np.arange(t, dtype=np.int64)
    ang = 2.0 * np.pi * ((j[:, None] * j[None, :]) % t) / t
    return (np.cos(ang) / math.sqrt(t)).astype(np.float32), (np.sin(ang) / math.sqrt(t)).astype(np.float32)


def _channel_dft():
    c = np.arange(F_GROUP_DIM, dtype=np.int64)
    ang = 2.0 * np.pi * ((c[:, None] * c[None, :]) % F_GROUP_DIM) / F_GROUP_DIM
    eye = np.eye(F_GROUPS)
    bdc = np.kron(eye, np.cos(ang)) / math.sqrt(F_GROUP_DIM)
    bds = np.kron(eye, np.sin(ang)) / math.sqrt(F_GROUP_DIM)
    return bdc.astype(np.float32), bds.astype(np.float32)


def _rope_tables():
    pos = np.arange(LAT_T)
    n = ROPE // 4
    inv_freq = np.power(np.float32(ROPE_BASE), -np.arange(n, dtype=np.float32) / np.float32(n))
    ang_r = (pos // GRID_W).astype(np.float32)[:, None] * inv_freq
    ang_c = (pos % GRID_W).astype(np.float32)[:, None] * inv_freq
    cos = np.concatenate([np.cos(ang_r), np.cos(ang_r), np.cos(ang_c), np.cos(ang_c),
                          np.ones((LAT_T, LANES - ROPE))], axis=1)
    sin = np.concatenate([-np.sin(ang_r), np.sin(ang_r), -np.sin(ang_c), np.sin(ang_c),
                          np.zeros((LAT_T, LANES - ROPE))], axis=1)
    return cos.astype(np.float32), sin.astype(np.float32)


def kernel(x_prompt, x_sample, cache_ckv, cache_krope, c, c_ctx, w_ada, b_ada, g_pre_mix, g_post_mix, g_pre_ffn, g_post_ffn, w_in, g_sgu, w_spatial, b_spatial, g_q, w_uq, g_kv, w_ukv, w_out, w_router, b_router, w_gate_up, b_gate_up, w_down, b_down):
    xc, xl = x_prompt.reshape(N_CTX, D), x_sample.reshape(N_LAT, D)

    cond = jnp.concatenate([c_ctx[None, :], c, jnp.zeros((5, D), F32)], axis=0)
    mods = _mod_call(cond.T, w_ada, b_ada.reshape(DEPTH, 1, 6 * D))
    mods = mods[:, :3].reshape(DEPTH, 3, 6, D)

    gmat = jnp.asarray(np.kron(np.eye(A_HEADS), np.full((A_HEAD_DIM, A_HEAD_DIM), 1.0 / A_HEAD_DIM)),
                       dtype=BF16)
    bdc_np, bds_np = _channel_dft()
    bdc, bds = jnp.asarray(bdc_np).astype(BF16), jnp.asarray(bds_np).astype(BF16)
    dft = {}
    for t in (CTX_T, LAT_T):
        ct_np, st_np = _dft_tables(t)
        dft[t] = (jnp.asarray(ct_np).astype(BF16), jnp.asarray(st_np).astype(BF16))
    cos_np, sin_np = _rope_tables()
    cos_t, sin_t = jnp.asarray(cos_np), jnp.asarray(sin_np)
    tri = jnp.asarray(np.tril(np.ones((TB, TB), np.float32), k=-1)).astype(BF16)

    b_gu = b_gate_up.reshape(DEPTH, N_EXPERTS, 1, 2 * D_FF)
    b_dn = b_down.reshape(DEPTH, N_EXPERTS, 1, D)

    ckv_layers, krope_layers = [], []
    for i in range(DEPTH):
        w_in_p = jnp.pad(w_in[i], ((0, 0), (0, IN_PAD - w_in.shape[-1]))).astype(BF16)
        w_uq_p = jnp.pad(w_uq[i].reshape(Q_LORA, HEADS, NOPE + ROPE),
                         ((0, 0), (0, 0), (0, HEAD_PAD - NOPE - ROPE))).reshape(Q_LORA, HEADS * HEAD_PAD)
        w_uq_p = w_uq_p.astype(BF16)
        w_ukv_b = w_ukv[i].astype(BF16)
        pre = _pre_call(xc, xl, mods[i], g_pre_mix[i][None, :], w_in_p, g_sgu[i][None, :], gmat,
                        g_q[i][None, :], w_uq_p, g_kv[i][None, :], w_ukv_b, cos_t, sin_t)
        u, vn, zf, q, kv, kr, ckv, zkr = pre
        ckv_layers.append(ckv[:N_CTX].reshape(N_CTX_B, CTX_T, KV_LORA))
        krope_layers.append(zkr[:N_CTX, :ROPE].reshape(N_CTX_B, CTX_T, ROPE))

        ws = w_spatial[i].astype(BF16)
        bs_full = jnp.repeat(b_spatial[i].T, A_HEAD_DIM, axis=1)
        mix_in = (u, vn, zf, q, kv, kr)
        y_ctx = _mix_call(mix_in, (ws, bs_full, bdc, bds) + dft[CTX_T], CTX_T, N_CTX_B, 0)
        ckr_p = jnp.pad(cache_krope[:, i], ((0, 0), (0, 0), (0, LANES - ROPE))).astype(BF16)
        y_lat = _mix_call(mix_in, (ws, bs_full, bdc, bds) + dft[LAT_T], LAT_T, N_LAT_B, N_CTX,
                          cache=(cache_ckv[:, i], ckr_p, w_ukv_b))

        w_r = jnp.pad(w_router[i], ((0, 0), (0, LANES - N_EXPERTS)))
        b_r = jnp.pad(b_router[i], (0, LANES - N_EXPERTS), constant_values=NEG)[None, :]
        x1, h2, tab, rw, cnt = _post_call(y_ctx, y_lat, xc, xl, mods[i], w_out[i].astype(BF16),
                                          g_post_mix[i][None, :], g_pre_ffn[i][None, :], w_r, b_r, tri)

        counts = cnt[0, :N_EXPERTS]
        nblk = (counts + RB - 1) // RB
        blk_end = jnp.cumsum(nblk)
        blk_start = blk_end - nblk
        n_used = blk_end[-1:]
        zrow = blk_start * RB + counts // 8 * 8
        dest = _dest_call(blk_start, tab)
        dest_flat = dest.reshape(N_TOK // TB, 1, TOP_K * TB)
        xs = _dispatch_call(zrow, n_used, dest_flat, h2)
        blk_id = jnp.arange(N_BLOCKS, dtype=jnp.int32)
        blk = jnp.minimum(blk_id, n_used - 1)
        block_e = jnp.sum((blk_end[None, :] <= blk[:, None]).astype(jnp.int32), axis=1)
        block_e = jnp.minimum(block_e, N_EXPERTS - 1)
        eid = jnp.arange(N_EXPERTS, dtype=jnp.int32)
        owns = nblk > 0
        later = jnp.where((eid[None, :] > eid[:, None]) & owns[None, :], eid[None, :], N_EXPERTS)
        nxt_e = jnp.min(later, axis=1)
        nxt_e = jnp.where(nxt_e == N_EXPERTS, -1, nxt_e)
        run_id = jnp.cumsum(owns.astype(jnp.int32)) - 1
        first = ((blk_id == blk_start[block_e]) & (blk_id < n_used)).astype(jnp.int32)
        tables = (block_e, first, nxt_e[block_e], run_id[block_e] % 2, n_used)
        ys = _moe_call(i, tables, xs, w_gate_up, b_gu, w_down, b_dn)
        xc, xl = _final_call(dest_flat, x1, ys, rw, mods[i], g_post_ffn[i][None, :])

    y_prompt = xc.reshape(N_CTX_B, CTX_T, D)
    y_sample = xl.reshape(N_LAT_B, LAT_T, D)
    return (y_prompt, y_sample, jnp.stack(ckv_layers, axis=1), jnp.stack(krope_layers, axis=1))
```

```python
import functools
import math

import jax
import jax.numpy as jnp
import numpy as np
from jax import lax
from jax.experimental import pallas as pl
from jax.experimental.pallas import tpu as pltpu

F32 = jnp.float32
BF16 = jnp.bfloat16

D = 1024
N_CTX_B, CTX_T = 16, 256
N_LAT_B, LAT_T = 2, 1024
PAST = 512
N_CTX = N_CTX_B * CTX_T
N_LAT = N_LAT_B * LAT_T
N_TOK = N_CTX + N_LAT
DEPTH = 2
GRID_W = 64
EPS = 1e-6
A_HEADS, A_HEAD_DIM, A_WIDTH, CHUNK = 4, 64, 256, 128
F_GROUPS, F_GROUP_DIM, F_WIDTH = 4, 64, 256
HEADS, Q_LORA, KV_LORA, NOPE, ROPE, V_DIM = 4, 256, 128, 128, 64, 128
HEAD_PAD = 256
IN_PAD = 1280
N_EXPERTS, TOP_K, D_FF = 32, 4, 1024
SWIGLU_LIMIT, SWIGLU_ALPHA = 7.0, 1.702
ROPE_BASE = 10000.0

TB = 256
QB = 256
RB = 256
LANES = 128
SUB = D // LANES
N_BLOCKS = N_TOK * TOP_K // RB + N_EXPERTS
N_ROWS = N_BLOCKS * RB
NEG = -3.0e38
VMEM_LIMIT = 56 * 1024 * 1024


def _rms(x, g):
    return x * lax.rsqrt(jnp.mean(x * x, axis=-1, keepdims=True) + EPS) * g


def _split_dot(v, m):
    hi = v.astype(BF16)
    lo = (v - hi.astype(F32)).astype(BF16)
    return (jnp.dot(hi, m, preferred_element_type=F32)
            + jnp.dot(lo, m, preferred_element_type=F32))


def _dot_nt(a, b):
    return lax.dot_general(a, b, (((1,), (1,)), ((), ())), preferred_element_type=F32)


def _store_tiled(ref, x):
    rows = x.shape[0]
    for s in range(SUB):
        ref[pl.ds(s, rows, stride=SUB), :] = x[:, s * LANES:(s + 1) * LANES]


def _load_tiled(ref, rows):
    return jnp.concatenate([ref[pl.ds(s, rows, stride=SUB), :] for s in range(SUB)], axis=1)


def _tile_rows(ref, row, n_rows=1):
    return ref.at[pl.ds(pl.multiple_of(row * SUB, SUB), n_rows * SUB)]


def _mod_kernel(ct_ref, w_ref, b_ref, o_ref):
    ct = ct_ref[...]
    s = ct * jax.nn.sigmoid(ct)
    w = w_ref[...]
    o_ref[...] = jnp.zeros(o_ref.shape, F32)
    for r in range(3):
        o_ref[r:r + 1, :] = jnp.sum(w * s[:, r:r + 1], axis=0, keepdims=True) + b_ref[...]


def _mod_call(cond_t, w_ada, b_ada):
    cb = 512
    return pl.pallas_call(
        _mod_kernel,
        grid=(DEPTH, 6 * D // cb),
        in_specs=[
            pl.BlockSpec((D, 8), lambda l, j: (0, 0)),
            pl.BlockSpec((None, D, cb), lambda l, j: (l, 0, j)),
            pl.BlockSpec((None, 1, cb), lambda l, j: (l, 0, j)),
        ],
        out_specs=pl.BlockSpec((None, 8, cb), lambda l, j: (l, 0, j)),
        out_shape=jax.ShapeDtypeStruct((DEPTH, 8, 6 * D), F32),
        compiler_params=pltpu.CompilerParams(dimension_semantics=("parallel", "parallel")),
        name="modulation",
    )(cond_t, w_ada, b_ada)


def _mod_index(i):
    first_lat = N_CTX // TB
    return jnp.where(i < first_lat, 0, 1 + (i - first_lat) // (LAT_T // TB))


def _ctx_spec(width):
    return pl.BlockSpec((TB, width), lambda i: (jnp.minimum(i, N_CTX // TB - 1), 0))


def _lat_spec(width):
    return pl.BlockSpec((TB, width), lambda i: (jnp.maximum(i - N_CTX // TB, 0), 0))


def _pick(i, ctx_ref, lat_ref):
    return jnp.where(i >= N_CTX // TB, lat_ref[...], ctx_ref[...])


def _swap_halves(x, lane):
    w = x.shape[-1]
    fwd = pltpu.roll(x, w - 16, 1)
    bwd = pltpu.roll(x, 16, 1)
    return jnp.where((lane & 31) < 16, fwd, bwd)


def _pre_kernel(xc_ref, xl_ref, mod_ref, gpre_ref, win_ref, gsgu_ref, gmat_ref, gq_ref, wuq_ref,
                gkv_ref, wukv_ref, cos_ref, sin_ref,
                u_ref, vn_ref, zf_ref, q_ref, kv_ref, kr_ref, ckv_ref, zkr_ref):
    i = pl.program_id(0)
    is_lat = i >= N_CTX // TB
    x = _pick(i, xc_ref, xl_ref)
    h = _rms(x, gpre_ref[...]) * (1.0 + mod_ref[1:2, :]) + mod_ref[0:1, :]
    z = jnp.dot(h.astype(BF16), win_ref[...], preferred_element_type=F32)

    ga = jax.nn.gelu(z[:, :2 * A_WIDTH])
    u_ref[...] = ga[:, :A_WIDTH]
    v = ga[:, A_WIDTH:]
    gmat = gmat_ref[...]
    dv = v - _split_dot(v, gmat)
    var = _split_dot(dv * dv, gmat)
    vn_ref[...] = (dv * lax.rsqrt(var + EPS) * gsgu_ref[...]).astype(BF16)

    zf_ref[...] = z[:, 512:768].astype(BF16)

    cos = jnp.where(is_lat, cos_ref[...], 1.0)
    sin = jnp.where(is_lat, sin_ref[...], 0.0)
    lane = lax.broadcasted_iota(jnp.int32, (TB, LANES), 1)

    qn = _rms(z[:, 768:1024], gq_ref[...])
    scale = (NOPE + ROPE) ** -0.5
    q = jnp.dot(qn.astype(BF16), wuq_ref[...], preferred_element_type=F32) * scale
    for hd in range(HEADS):
        base = hd * HEAD_PAD
        q_ref[:, base:base + NOPE] = q[:, base:base + NOPE].astype(BF16)
        qr = q[:, base + NOPE:base + HEAD_PAD]
        q_ref[:, base + NOPE:base + HEAD_PAD] = (qr * cos + _swap_halves(qr, lane) * sin).astype(BF16)

    ckv = _rms(z[:, 1024:1152], gkv_ref[...])
    ckv_ref[...] = ckv
    kv_ref[...] = jnp.dot(ckv.astype(BF16), wukv_ref[...], preferred_element_type=F32).astype(BF16)

    zkr = z[:, 1152:1280]
    zkr_ref[...] = zkr
    kr_ref[...] = (zkr * cos + _swap_halves(zkr, lane) * sin).astype(BF16)


def _pre_call(xc, xl, mods, g_pre, w_in, g_sgu, gmat, g_q, w_uq, g_kv, w_ukv, cos_t, sin_t):
    nb = N_TOK // TB
    first_lat = N_CTX // TB
    pos_blocks = LAT_T // TB

    def tok(width):
        return pl.BlockSpec((TB, width), lambda i: (i, 0))

    def full(shape):
        return pl.BlockSpec(shape, lambda i: (0,) * len(shape))

    def rope_map(i):
        return (jnp.where(i >= first_lat, (i - first_lat) % pos_blocks, 0), 0)

    out_shape = (
        jax.ShapeDtypeStruct((N_TOK, A_WIDTH), F32),
        jax.ShapeDtypeStruct((N_TOK, A_WIDTH), BF16),
        jax.ShapeDtypeStruct((N_TOK, F_WIDTH), BF16),
        jax.ShapeDtypeStruct((N_TOK, HEADS * HEAD_PAD), BF16),
        jax.ShapeDtypeStruct((N_TOK, HEADS * (NOPE + V_DIM)), BF16),
        jax.ShapeDtypeStruct((N_TOK, LANES), BF16),
        jax.ShapeDtypeStruct((N_TOK, KV_LORA), F32),
        jax.ShapeDtypeStruct((N_TOK, LANES), F32),
    )
    return pl.pallas_call(
        _pre_kernel,
        grid=(nb,),
        in_specs=[
            _ctx_spec(D), _lat_spec(D),
            pl.BlockSpec((None, 6, D), lambda i: (_mod_index(i), 0, 0)),
            full((1, D)), full((D, IN_PAD)), full((1, A_WIDTH)), full((A_WIDTH, A_WIDTH)),
            full((1, Q_LORA)), full((Q_LORA, HEADS * HEAD_PAD)),
            full((1, KV_LORA)), full((KV_LORA, HEADS * (NOPE + V_DIM))),
            pl.BlockSpec((TB, LANES), rope_map), pl.BlockSpec((TB, LANES), rope_map),
        ],
        out_specs=(tok(A_WIDTH), tok(A_WIDTH), tok(F_WIDTH), tok(HEADS * HEAD_PAD),
                   tok(HEADS * (NOPE + V_DIM)), tok(LANES), tok(KV_LORA), tok(LANES)),
        out_shape=out_shape,
        compiler_params=pltpu.CompilerParams(dimension_semantics=("parallel",),
                                             vmem_limit_bytes=VMEM_LIMIT),
        name="pre_mix",
    )(xc, xl, mods, g_pre, w_in, g_sgu, gmat, g_q, w_uq, g_kv, w_ukv, cos_t, sin_t)


def _mix_kernel(*refs, has_cache):
    if has_cache:
        (u_ref, vn_ref, zf_ref, q_ref, kv_ref, kr_ref, ws_ref, bs_ref, bdc_ref, bds_ref,
         ct_ref, st_ref, cckv_ref, ckr_ref, wukv_ref, o_ref) = refs
    else:
        (u_ref, vn_ref, zf_ref, q_ref, kv_ref, kr_ref, ws_ref, bs_ref, bdc_ref, bds_ref,
         ct_ref, st_ref, o_ref) = refs

    lane = lax.broadcasted_iota(jnp.int32, (CHUNK, A_WIDTH), 1)
    for c in range(QB // CHUNK):
        rows = slice(c * CHUNK, (c + 1) * CHUNK)
        vch = vn_ref[rows, :]
        s = bs_ref[...]
        for g in range(A_HEADS):
            sg = jnp.dot(ws_ref[g], vch, preferred_element_type=F32)
            in_head = (lane >= g * A_HEAD_DIM) & (lane < (g + 1) * A_HEAD_DIM)
            s = s + jnp.where(in_head, sg, 0.0)
        o_ref[rows, 0:A_WIDTH] = (u_ref[rows, :] * s).astype(BF16)

    zf = zf_ref[...]
    zc = jnp.dot(zf, bdc_ref[...], preferred_element_type=F32).astype(BF16)
    zs = jnp.dot(zf, bds_ref[...], preferred_element_type=F32).astype(BF16)
    yf = (jnp.dot(ct_ref[...], zc, preferred_element_type=F32)
          - jnp.dot(st_ref[...], zs, preferred_element_type=F32))
    o_ref[:, A_WIDTH:A_WIDTH + F_WIDTH] = yf.astype(BF16)

    kr = kr_ref[...]
    if has_cache:
        kvc = jnp.dot(cckv_ref[...].astype(BF16), wukv_ref[...],
                      preferred_element_type=F32).astype(BF16)
        krc = ckr_ref[...]
    for hd in range(HEADS):
        qh = q_ref[:, hd * HEAD_PAD:(hd + 1) * HEAD_PAD]
        kb = hd * (NOPE + V_DIM)
        kh = jnp.concatenate([kv_ref[:, kb:kb + NOPE], kr], axis=1)
        vh = kv_ref[:, kb + NOPE:kb + NOPE + V_DIM]
        s = _dot_nt(qh, kh)
        m = jnp.max(s, axis=-1, keepdims=True)
        if has_cache:
            khc = jnp.concatenate([kvc[:, kb:kb + NOPE], krc], axis=1)
            vhc = kvc[:, kb + NOPE:kb + NOPE + V_DIM]
            sc = _dot_nt(qh, khc)
            m = jnp.maximum(m, jnp.max(sc, axis=-1, keepdims=True))
        e = jnp.exp(s - m)
        den = jnp.sum(e, axis=-1, keepdims=True)
        o = jnp.dot(e.astype(BF16), vh, preferred_element_type=F32)
        if has_cache:
            ec = jnp.exp(sc - m)
            den = den + jnp.sum(ec, axis=-1, keepdims=True)
            o = o + jnp.dot(ec.astype(BF16), vhc, preferred_element_type=F32)
        ob = A_WIDTH + F_WIDTH + hd * V_DIM
        o_ref[:, ob:ob + V_DIM] = (o * (1.0 / den)).astype(BF16)


def _mix_call(pre, consts, seq_t, n_batch, tok_off, cache=None):
    u, vn, zf, q, kv, kr = pre
    ws, bs_full, bdc, bds, ct, st = consts
    nq = seq_t // QB
    qoff = tok_off // QB
    soff = tok_off // seq_t

    def qrow(width):
        return pl.BlockSpec((QB, width), lambda b, j: (qoff + b * nq + j, 0))

    def srow(width):
        return pl.BlockSpec((seq_t, width), lambda b, j: (soff + b, 0))

    def full(shape):
        return pl.BlockSpec(shape, lambda b, j: (0,) * len(shape))

    in_specs = [qrow(A_WIDTH), qrow(A_WIDTH), srow(F_WIDTH), qrow(HEADS * HEAD_PAD),
                srow(HEADS * (NOPE + V_DIM)), srow(LANES),
                full((A_HEADS, CHUNK, CHUNK)), full((CHUNK, A_WIDTH)),
                full((F_WIDTH, F_WIDTH)), full((F_WIDTH, F_WIDTH)),
                pl.BlockSpec((QB, seq_t), lambda b, j: (j, 0)),
                pl.BlockSpec((QB, seq_t), lambda b, j: (j, 0))]
    args = [u, vn, zf, q, kv, kr, ws, bs_full, bdc, bds, ct, st]
    if cache is not None:
        cckv, ckr, wukv = cache
        in_specs += [pl.BlockSpec((None, PAST, KV_LORA), lambda b, j: (b, 0, 0)),
                     pl.BlockSpec((None, PAST, LANES), lambda b, j: (b, 0, 0)),
                     full((KV_LORA, HEADS * (NOPE + V_DIM)))]
        args += [cckv, ckr, wukv]
    return pl.pallas_call(
        functools.partial(_mix_kernel, has_cache=cache is not None),
        grid=(n_batch, nq),
        in_specs=in_specs,
        out_specs=pl.BlockSpec((QB, D), lambda b, j: (b * nq + j, 0)),
        out_shape=jax.ShapeDtypeStruct((n_batch * seq_t, D), BF16),
        compiler_params=pltpu.CompilerParams(dimension_semantics=("parallel", "parallel"),
                                             vmem_limit_bytes=VMEM_LIMIT),
        name="mix_lat" if cache is not None else "mix_ctx",
    )(*args)


def _post_kernel(yc_ref, yl_ref, xc_ref, xl_ref, mod_ref, wout_ref, gpost_ref, gffn_ref, wr_ref, br_ref,
                 tri_ref, x1_ref, h2_ref, tab_ref, rw_ref, cnt_ref, carry_ref):
    i = pl.program_id(0)

    @pl.when(i == 0)
    def _():
        carry_ref[...] = jnp.zeros(carry_ref.shape, F32)

    y = jnp.dot(_pick(i, yc_ref, yl_ref), wout_ref[...], preferred_element_type=F32)
    x1 = _pick(i, xc_ref, xl_ref) + mod_ref[2:3, :] * _rms(y, gpost_ref[...])
    x1_ref[...] = x1
    h2 = _rms(x1, gffn_ref[...]) * (1.0 + mod_ref[4:5, :]) + mod_ref[3:4, :]
    _store_tiled(h2_ref, h2)

    wr = wr_ref[...]
    wr_hi = wr.astype(BF16)
    wr_lo = (wr - wr_hi.astype(F32)).astype(BF16)
    h_hi = h2.astype(BF16)
    h_lo = (h2 - h_hi.astype(F32)).astype(BF16)
    logits = (jnp.dot(h_hi, wr_hi, preferred_element_type=F32)
              + jnp.dot(h_lo, wr_hi, preferred_element_type=F32)
              + jnp.dot(h_hi, wr_lo, preferred_element_type=F32)) + br_ref[...]

    lane = lax.broadcasted_iota(jnp.int32, (TB, LANES), 1)
    lane_f = lane.astype(F32)
    work = logits
    idx, val = [], []
    for _ in range(TOP_K):
        m = jnp.max(work, axis=-1, keepdims=True)
        ik = jnp.min(jnp.where(work == m, lane_f, float(LANES)), axis=-1, keepdims=True)
        idx.append(ik)
        val.append(m)
        work = jnp.where(lane_f == ik, NEG, work)
    ex = [jnp.exp(v - val[0]) for v in val]
    den = ex[0] + ex[1] + ex[2] + ex[3]

    onehot = jnp.zeros((TB, LANES), F32)
    for k in range(TOP_K):
        onehot = onehot + jnp.where(lane_f == idx[k], 1.0, 0.0)
    before = jnp.dot(tri_ref[...], onehot.astype(BF16), preferred_element_type=F32) + carry_ref[0:1, :]
    ri = jnp.zeros((TB, LANES), F32)
    rw = jnp.zeros((TB, LANES), F32)
    for k in range(TOP_K):
        rank_k = jnp.sum(jnp.where(lane_f == idx[k], before, 0.0), axis=-1, keepdims=True)
        ri = ri + jnp.where(lane == k, idx[k], 0.0) + jnp.where(lane == TOP_K + k, rank_k, 0.0)
        rw = rw + jnp.where(lane == k, ex[k] / den, 0.0)
    tab_ref[...] = ri.T[0:2 * TOP_K, :].astype(jnp.int32)
    rw_ref[...] = rw
    total = carry_ref[0:1, :] + jnp.sum(onehot, axis=0, keepdims=True)
    carry_ref[...] = jnp.broadcast_to(total, carry_ref.shape)
    cnt_ref[...] = jnp.broadcast_to(total, cnt_ref.shape).astype(jnp.int32)


def _post_call(yc, yl, xc, xl, mods, w_out, g_post, g_ffn, w_r, b_r, tri):
    nb = N_TOK // TB

    def tok(width):
        return pl.BlockSpec((TB, width), lambda i: (i, 0))

    def full(shape):
        return pl.BlockSpec(shape, lambda i: (0,) * len(shape))

    return pl.pallas_call(
        _post_kernel,
        grid=(nb,),
        in_specs=[_ctx_spec(D), _lat_spec(D), _ctx_spec(D), _lat_spec(D),
                  pl.BlockSpec((None, 6, D), lambda i: (_mod_index(i), 0, 0)),
                  full((D, D)), full((1, D)), full((1, D)), full((D, LANES)), full((1, LANES)),
                  full((TB, TB))],
        out_specs=(tok(D), pl.BlockSpec((TB * SUB, LANES), lambda i: (i, 0)),
                   pl.BlockSpec((None, 2 * TOP_K, TB), lambda i: (i, 0, 0)),
                   tok(LANES), full((8, LANES))),
        out_shape=(jax.ShapeDtypeStruct((N_TOK, D), F32),
                   jax.ShapeDtypeStruct((N_TOK * SUB, LANES), F32),
                   jax.ShapeDtypeStruct((nb, 2 * TOP_K, TB), jnp.int32),
                   jax.ShapeDtypeStruct((N_TOK, LANES), F32),
                   jax.ShapeDtypeStruct((8, LANES), jnp.int32)),
        scratch_shapes=[pltpu.VMEM((8, LANES), F32)],
        compiler_params=pltpu.CompilerParams(dimension_semantics=("arbitrary",),
                                             vmem_limit_bytes=VMEM_LIMIT),
        name="post_mix_router",
    )(yc, yl, xc, xl, mods, w_out, g_post, g_ffn, w_r, b_r, tri)


def _dest_kernel(bs_ref, tab_ref, o_ref):
    idx = tab_ref[:, 0:TOP_K, :]
    base = jnp.zeros(idx.shape, jnp.int32)
    for e in range(N_EXPERTS):
        base = jnp.where(idx == e, bs_ref[e] * RB, base)
    o_ref[...] = base + tab_ref[:, TOP_K:2 * TOP_K, :]


def _dest_call(blk_start, tab):
    nb = N_TOK // TB
    return pl.pallas_call(
        _dest_kernel,
        in_specs=[pl.BlockSpec(memory_space=pltpu.SMEM), pl.BlockSpec(memory_space=pltpu.VMEM)],
        out_specs=pl.BlockSpec(memory_space=pltpu.VMEM),
        out_shape=jax.ShapeDtypeStruct((nb, TOP_K, TB), jnp.int32),
        name="dest_rows",
    )(blk_start, tab)


def _dispatch_kernel(zrow_ref, nu_ref, dest_ref, h2_ref, xs_ref, zero_ref, sem, zsem):
    i = pl.program_id(0)

    def zero_fill(row):
        return pltpu.make_async_copy(zero_ref, _tile_rows(xs_ref, row, RB), zsem)

    @pl.when(i == 0)
    def _():
        zero_ref[...] = jnp.zeros(zero_ref.shape, F32)
        for e in range(N_EXPERTS):
            zero_fill(zrow_ref[e]).start()
        for e in range(N_EXPERTS):
            zero_fill(zrow_ref[e]).wait()

        def tail_start(b, carry):
            zero_fill(b * RB).start()
            return carry

        def tail_wait(b, carry):
            zero_fill(b * RB).wait()
            return carry

        lax.fori_loop(nu_ref[0], N_BLOCKS, tail_start, 0)
        lax.fori_loop(nu_ref[0], N_BLOCKS, tail_wait, 0)

    def body(t, carry):
        for k in range(TOP_K):
            pltpu.make_async_copy(_tile_rows(h2_ref, t), _tile_rows(xs_ref, dest_ref[0, k * TB + t]),
                                  sem).start(priority=k % 2)
        return carry

    lax.fori_loop(0, TB, body, 0, unroll=8)
    for k in range(TOP_K):
        pltpu.make_async_copy(h2_ref, _tile_rows(xs_ref, 0, TB), sem).wait()


def _dispatch_call(zrow, n_used, dest, h2):
    nb = N_TOK // TB
    grid_spec = pltpu.PrefetchScalarGridSpec(
        num_scalar_prefetch=2,
        grid=(nb,),
        in_specs=[
            pl.BlockSpec((None, 1, TOP_K * TB), lambda i, z, n: (i, 0, 0), memory_space=pltpu.SMEM),
            pl.BlockSpec((TB * SUB, LANES), lambda i, z, n: (i, 0)),
        ],
        out_specs=pl.BlockSpec(memory_space=pl.ANY),
        scratch_shapes=[pltpu.VMEM((RB * SUB, LANES), F32), pltpu.SemaphoreType.DMA,
                        pltpu.SemaphoreType.DMA],
    )
    return pl.pallas_call(
        _dispatch_kernel,
        grid_spec=grid_spec,
        out_shape=jax.ShapeDtypeStruct((N_ROWS * SUB, LANES), F32),
        compiler_params=pltpu.CompilerParams(dimension_semantics=("arbitrary",),
                                             vmem_limit_bytes=VMEM_LIMIT),
        name="dispatch_rows",
    )(zrow, n_used, dest, h2)


def _moe_kernel(be_ref, first_ref, nxt_ref, slot_ref, nu_ref, xs_ref, wgu_hbm, bgu_ref, wdn_hbm, bdn_ref,
                o_ref, wgu_f, wdn_f, sems, *, layer):
    b = pl.program_id(0)
    used = b < nu_ref[0]

    def fetch(e, s):
        return (pltpu.make_async_copy(wgu_hbm.at[layer, e], wgu_f.at[s], sems.at[0, s]),
                pltpu.make_async_copy(wdn_hbm.at[layer, e], wdn_f.at[s], sems.at[1, s]))

    @pl.when(b == 0)
    def _():
        for cp in fetch(be_ref[0], 0):
            cp.start(priority=1)

    @pl.when(first_ref[b] == 1)
    def _():
        s = slot_ref[b]
        for cp in fetch(be_ref[b], s):
            cp.wait()

        @pl.when(nxt_ref[b] >= 0)
        def _():
            for cp in fetch(nxt_ref[b], 1 - s):
                cp.start(priority=1)

    @pl.when(used)
    def _():
        s = slot_ref[b]
        x = _load_tiled(xs_ref, RB).astype(BF16)
        gu = jnp.dot(x, wgu_f[s].astype(BF16), preferred_element_type=F32) + bgu_ref[...]
        g = jnp.minimum(gu[:, :D_FF], SWIGLU_LIMIT)
        l = jnp.clip(gu[:, D_FF:], -SWIGLU_LIMIT, SWIGLU_LIMIT)
        a = g * jax.nn.sigmoid(SWIGLU_ALPHA * g) * (l + 1.0)
        y = jnp.dot(a.astype(BF16), wdn_f[s].astype(BF16), preferred_element_type=F32) + bdn_ref[...]
        _store_tiled(o_ref, y)

    @pl.when(jnp.logical_not(used))
    def _():
        o_ref[...] = jnp.zeros(o_ref.shape, o_ref.dtype)


def _moe_call(layer, tables, xs, w_gu, b_gu, w_dn, b_dn):
    def rows_in(b, be, fi, nx, sl, nu):
        return (jnp.minimum(b, nu[0] - 1), 0)

    def rows_out(b, be, fi, nx, sl, nu):
        return (b, 0)

    def expert(b, be, fi, nx, sl, nu):
        return (layer, be[b], 0, 0)

    grid_spec = pltpu.PrefetchScalarGridSpec(
        num_scalar_prefetch=5,
        grid=(N_BLOCKS,),
        in_specs=[
            pl.BlockSpec((RB * SUB, LANES), rows_in),
            pl.BlockSpec(memory_space=pl.ANY),
            pl.BlockSpec((None, None, 1, 2 * D_FF), expert),
            pl.BlockSpec(memory_space=pl.ANY),
            pl.BlockSpec((None, None, 1, D), expert),
        ],
        out_specs=pl.BlockSpec((RB * SUB, LANES), rows_out),
        scratch_shapes=[pltpu.VMEM((2, D, 2 * D_FF), F32), pltpu.VMEM((2, D_FF, D), F32),
                        pltpu.SemaphoreType.DMA((2, 2))],
    )
    return pl.pallas_call(
        functools.partial(_moe_kernel, layer=layer),
        grid_spec=grid_spec,
        out_shape=jax.ShapeDtypeStruct((N_ROWS * SUB, LANES), F32),
        compiler_params=pltpu.CompilerParams(dimension_semantics=("arbitrary",),
                                             vmem_limit_bytes=VMEM_LIMIT),
        name="moe_experts",
    )(*tables, xs, w_gu, b_gu, w_dn, b_dn)


def _final_kernel(dcur_ref, dnxt_ref, x1_ref, ys_ref, rw_ref, mod_ref, g_ref, oc_ref, ol_ref, buf, sems):
    i = pl.program_id(0)
    nb = pl.num_programs(0)
    slot = i % 2

    def gather(dest_ref, s):
        def body(t, carry):
            for k in range(TOP_K):
                pltpu.make_async_copy(_tile_rows(ys_ref, dest_ref[0, k * TB + t]),
                                      _tile_rows(buf.at[s], k * TB + t), sems.at[s]).start(priority=k % 2)
            return carry

        lax.fori_loop(0, TB, body, 0, unroll=8)

    @pl.when(i == 0)
    def _():
        gather(dcur_ref, 0)

    @pl.when(i + 1 < nb)
    def _():
        gather(dnxt_ref, 1 - slot)

    pltpu.make_async_copy(_tile_rows(ys_ref, 0, TOP_K * TB), buf.at[slot], sems.at[slot]).wait()

    rw = rw_ref[...]
    y = jnp.zeros((TB, D), F32)
    for k in range(TOP_K):
        yk = jnp.concatenate([buf[slot, pl.ds(k * TB * SUB + s, TB, stride=SUB), :] for s in range(SUB)],
                             axis=1)
        y = y + yk * rw[:, k:k + 1]
    x2 = x1_ref[...] + mod_ref[5:6, :] * _rms(y, g_ref[...])

    @pl.when(i < N_CTX // TB)
    def _():
        oc_ref[...] = x2

    @pl.when(i >= N_CTX // TB)
    def _():
        ol_ref[...] = x2


def _final_call(dest_flat, x1, ys, rw, mods, g_post_ffn):
    nb = N_TOK // TB
    return pl.pallas_call(
        _final_kernel,
        grid=(nb,),
        in_specs=[pl.BlockSpec((None, 1, TOP_K * TB), lambda i: (i, 0, 0), memory_space=pltpu.SMEM),
                  pl.BlockSpec((None, 1, TOP_K * TB), lambda i: (jnp.minimum(i + 1, nb - 1), 0, 0),
                               memory_space=pltpu.SMEM),
                  pl.BlockSpec((TB, D), lambda i: (i, 0)),
                  pl.BlockSpec(memory_space=pl.ANY),
                  pl.BlockSpec((TB, LANES), lambda i: (i, 0)),
                  pl.BlockSpec((None, 6, D), lambda i: (_mod_index(i), 0, 0)),
                  pl.BlockSpec((1, D), lambda i: (0, 0))],
        out_specs=(_ctx_spec(D), _lat_spec(D)),
        out_shape=(jax.ShapeDtypeStruct((N_CTX, D), F32), jax.ShapeDtypeStruct((N_LAT, D), F32)),
        scratch_shapes=[pltpu.VMEM((2, TOP_K * TB * SUB, LANES), F32), pltpu.SemaphoreType.DMA((2,))],
        compiler_params=pltpu.CompilerParams(dimension_semantics=("arbitrary",),
                                             vmem_limit_bytes=VMEM_LIMIT),
        name="combine_residual",
    )(dest_flat, dest_flat, x1, ys, rw, mods, g_post_ffn)


def _dft_tables(t):
    j = np.arange(t, dtype=np.int64)
    ang = 2.0 * np.pi * ((j[:, None] * j[None, :]) % t) / t
    return (np.cos(ang) / math.sqrt(t)).astype(np.float32), (np.sin(ang) / math.sqrt(t)).astype(np.float32)


def _channel_dft():
    c = np.arange(F_GROUP_DIM, dtype=np.int64)
    ang = 2.0 * np.pi * ((c[:, None] * c[None, :]) % F_GROUP_DIM) / F_GROUP_DIM
    eye = np.eye(F_GROUPS)
    bdc = np.kron(eye, np.cos(ang)) / math.sqrt(F_GROUP_DIM)
    bds = np.kron(eye, np.sin(ang)) / math.sqrt(F_GROUP_DIM)
    return bdc.astype(np.float32), bds.astype(np.float32)


def _rope_tables():
    pos = np.arange(LAT_T)
    n = ROPE // 4
    inv_freq = np.power(np.float32(ROPE_BASE), -np.arange(n, dtype=np.float32) / np.float32(n))
    ang_r = (pos // GRID_W).astype(np.float32)[:, None] * inv_freq
    ang_c = (pos % GRID_W).astype(np.float32)[:, None] * inv_freq
    cos = np.concatenate([np.cos(ang_r), np.cos(ang_r), np.cos(ang_c), np.cos(ang_c),
                          np.ones((LAT_T, LANES - ROPE))], axis=1)
    sin = np.concatenate([-np.sin(ang_r), np.sin(ang_r), -np.sin(ang_c), np.sin(ang_c),
                          np.zeros((LAT_T, LANES - ROPE))], axis=1)
    return cos.astype(np.float32), sin.astype(np.float32)


def kernel(x_prompt, x_sample, cache_ckv, cache_krope, c, c_ctx, w_ada, b_ada, g_pre_mix, g_post_mix, g_pre_ffn, g_post_ffn, w_in, g_sgu, w_spatial, b_spatial, g_q, w_uq, g_kv, w_ukv, w_out, w_router, b_router, w_gate_up, b_gate_up, w_down, b_down):
    xc, xl = x_prompt.reshape(N_CTX, D), x_sample.reshape(N_LAT, D)

    cond = jnp.concatenate([c_ctx[None, :], c, jnp.zeros((5, D), F32)], axis=0)
    mods = _mod_call(cond.T, w_ada, b_ada.reshape(DEPTH, 1, 6 * D))
    mods = mods[:, :3].reshape(DEPTH, 3, 6, D)

    gmat = jnp.asarray(np.kron(np.eye(A_HEADS), np.full((A_HEAD_DIM, A_HEAD_DIM), 1.0 / A_HEAD_DIM)),
                       dtype=BF16)
    bdc_np, bds_np = _channel_dft()
    bdc, bds = jnp.asarray(bdc_np).astype(BF16), jnp.asarray(bds_np).astype(BF16)
    dft = {}
    for t in (CTX_T, LAT_T):
        ct_np, st_np = _dft_tables(t)
        dft[t] = (jnp.asarray(ct_np).astype(BF16), jnp.asarray(st_np).astype(BF16))
    cos_np, sin_np = _rope_tables()
    cos_t, sin_t = jnp.asarray(cos_np), jnp.asarray(sin_np)
    tri = jnp.asarray(np.tril(np.ones((TB, TB), np.float32), k=-1)).astype(BF16)

    b_gu = b_gate_up.reshape(DEPTH, N_EXPERTS, 1, 2 * D_FF)
    b_dn = b_down.reshape(DEPTH, N_EXPERTS, 1, D)

    ckv_layers, krope_layers = [], []
    for i in range(DEPTH):
        w_in_p = jnp.pad(w_in[i], ((0, 0), (0, IN_PAD - w_in.shape[-1]))).astype(BF16)
        w_uq_p = jnp.pad(w_uq[i].reshape(Q_LORA, HEADS, NOPE + ROPE),
                         ((0, 0), (0, 0), (0, HEAD_PAD - NOPE - ROPE))).reshape(Q_LORA, HEADS * HEAD_PAD)
        w_uq_p = w_uq_p.astype(BF16)
        w_ukv_b = w_ukv[i].astype(BF16)
        pre = _pre_call(xc, xl, mods[i], g_pre_mix[i][None, :], w_in_p, g_sgu[i][None, :], gmat,
                        g_q[i][None, :], w_uq_p, g_kv[i][None, :], w_ukv_b, cos_t, sin_t)
        u, vn, zf, q, kv, kr, ckv, zkr = pre
        ckv_layers.append(ckv[:N_CTX].reshape(N_CTX_B, CTX_T, KV_LORA))
        krope_layers.append(zkr[:N_CTX, :ROPE].reshape(N_CTX_B, CTX_T, ROPE))

        ws = w_spatial[i].astype(BF16)
        bs_full = jnp.repeat(b_spatial[i].T, A_HEAD_DIM, axis=1)
        mix_in = (u, vn, zf, q, kv, kr)
        y_ctx = _mix_call(mix_in, (ws, bs_full, bdc, bds) + dft[CTX_T], CTX_T, N_CTX_B, 0)
        ckr_p = jnp.pad(cache_krope[:, i], ((0, 0), (0, 0), (0, LANES - ROPE))).astype(BF16)
        y_lat = _mix_call(mix_in, (ws, bs_full, bdc, bds) + dft[LAT_T], LAT_T, N_LAT_B, N_CTX,
                          cache=(cache_ckv[:, i], ckr_p, w_ukv_b))

        w_r = jnp.pad(w_router[i], ((0, 0), (0, LANES - N_EXPERTS)))
        b_r = jnp.pad(b_router[i], (0, LANES - N_EXPERTS), constant_values=NEG)[None, :]
        x1, h2, tab, rw, cnt = _post_call(y_ctx, y_lat, xc, xl, mods[i], w_out[i].astype(BF16),
                                          g_post_mix[i][None, :], g_pre_ffn[i][None, :], w_r, b_r, tri)

        counts = cnt[0, :N_EXPERTS]
        nblk = (counts + RB - 1) // RB
        blk_end = jnp.cumsum(nblk)
        blk_start = blk_end - nblk
        n_used = blk_end[-1:]
        zrow = blk_start * RB + counts // 8 * 8
        dest = _dest_call(blk_start, tab)
        dest_flat = dest.reshape(N_TOK // TB, 1, TOP_K * TB)
        xs = _dispatch_call(zrow, n_used, dest_flat, h2)
        blk_id = jnp.arange(N_BLOCKS, dtype=jnp.int32)
        blk = jnp.minimum(blk_id, n_used - 1)
        block_e = jnp.sum((blk_end[None, :] <= blk[:, None]).astype(jnp.int32), axis=1)
        block_e = jnp.minimum(block_e, N_EXPERTS - 1)
        eid = jnp.arange(N_EXPERTS, dtype=jnp.int32)
        owns = nblk > 0
        later = jnp.where((eid[None, :] > eid[:, None]) & owns[None, :], eid[None, :], N_EXPERTS)
        nxt_e = jnp.min(later, axis=1)
        nxt_e = jnp.where(nxt_e == N_EXPERTS, -1, nxt_e)
        run_id = jnp.cumsum(owns.astype(jnp.int32)) - 1
        first = ((blk_id == blk_start[block_e]) & (blk_id < n_used)).astype(jnp.int32)
        tables = (block_e, first, nxt_e[block_e], run_id[block_e] % 2, n_used)
        ys = _moe_call(i, tables, xs, w_gate_up, b_gu, w_down, b_dn)
        xc, xl = _final_call(dest_flat, x1, ys, rw, mods[i], g_post_ffn[i][None, :])

    y_prompt = xc.reshape(N_CTX_B, CTX_T, D)
    y_sample = xl.reshape(N_LAT_B, LAT_T, D)
    return (y_prompt, y_sample, jnp.stack(ckv_layers, axis=1), jnp.stack(krope_layers, axis=1))
```

```python
import functools
import math

import jax
import jax.numpy as jnp
import numpy as np
from jax import lax
from jax.experimental import pallas as pl
from jax.experimental.pallas import tpu as pltpu

F32 = jnp.float32
BF16 = jnp.bfloat16

D = 1024
N_CTX_B, CTX_T = 16, 256
N_LAT_B, LAT_T = 2, 1024
PAST = 512
N_CTX = N_CTX_B * CTX_T
N_LAT = N_LAT_B * LAT_T
N_TOK = N_CTX + N_LAT
DEPTH = 2
GRID_W = 64
EPS = 1e-6
A_HEADS, A_HEAD_DIM, A_WIDTH, CHUNK = 4, 64, 256, 128
F_GROUPS, F_GROUP_DIM, F_WIDTH = 4, 64, 256
HEADS, Q_LORA, KV_LORA, NOPE, ROPE, V_DIM = 4, 256, 128, 128, 64, 128
HEAD_PAD = 256
IN_PAD = 1280
N_EXPERTS, TOP_K, D_FF = 32, 4, 1024
SWIGLU_LIMIT, SWIGLU_ALPHA = 7.0, 1.702
ROPE_BASE = 10000.0

TB = 512
QB = 256
RB = 256
LANES = 128
SUB = D // LANES
N_BLOCKS = N_TOK * TOP_K // RB + N_EXPERTS
N_ROWS = N_BLOCKS * RB
NEG = -3.0e38
VMEM_LIMIT = 56 * 1024 * 1024


def _rms(x, g):
    return x * lax.rsqrt(jnp.mean(x * x, axis=-1, keepdims=True) + EPS) * g


def _split_dot(v, m):
    hi = v.astype(BF16)
    lo = (v - hi.astype(F32)).astype(BF16)
    return (jnp.dot(hi, m, preferred_element_type=F32)
            + jnp.dot(lo, m, preferred_element_type=F32))


def _dot_nt(a, b):
    return lax.dot_general(a, b, (((1,), (1,)), ((), ())), preferred_element_type=F32)


def _store_tiled(ref, x):
    rows = x.shape[0]
    for s in range(SUB):
        ref[pl.ds(s, rows, stride=SUB), :] = x[:, s * LANES:(s + 1) * LANES]


def _load_tiled(ref, rows):
    return jnp.concatenate([ref[pl.ds(s, rows, stride=SUB), :] for s in range(SUB)], axis=1)


def _tile_rows(ref, row, n_rows=1):
    return ref.at[pl.ds(pl.multiple_of(row * SUB, SUB), n_rows * SUB)]


def _mod_kernel(ct_ref, w_ref, b_ref, o_ref):
    ct = ct_ref[...]
    s = ct * jax.nn.sigmoid(ct)
    w = w_ref[...]
    o_ref[...] = jnp.zeros(o_ref.shape, F32)
    for r in range(3):
        o_ref[r:r + 1, :] = jnp.sum(w * s[:, r:r + 1], axis=0, keepdims=True) + b_ref[...]


def _mod_call(cond_t, w_ada, b_ada):
    cb = 512
    return pl.pallas_call(
        _mod_kernel,
        grid=(DEPTH, 6 * D // cb),
        in_specs=[
            pl.BlockSpec((D, 8), lambda l, j: (0, 0)),
            pl.BlockSpec((None, D, cb), lambda l, j: (l, 0, j)),
            pl.BlockSpec((None, 1, cb), lambda l, j: (l, 0, j)),
        ],
        out_specs=pl.BlockSpec((None, 8, cb), lambda l, j: (l, 0, j)),
        out_shape=jax.ShapeDtypeStruct((DEPTH, 8, 6 * D), F32),
        compiler_params=pltpu.CompilerParams(dimension_semantics=("parallel", "parallel")),
        name="modulation",
    )(cond_t, w_ada, b_ada)


def _mod_index(i):
    first_lat = N_CTX // TB
    return jnp.where(i < first_lat, 0, 1 + (i - first_lat) // (LAT_T // TB))


def _ctx_spec(width):
    return pl.BlockSpec((TB, width), lambda i: (jnp.minimum(i, N_CTX // TB - 1), 0))


def _lat_spec(width):
    return pl.BlockSpec((TB, width), lambda i: (jnp.maximum(i - N_CTX // TB, 0), 0))


def _pick(i, ctx_ref, lat_ref):
    return jnp.where(i >= N_CTX // TB, lat_ref[...], ctx_ref[...])


def _swap_halves(x, lane):
    w = x.shape[-1]
    fwd = pltpu.roll(x, w - 16, 1)
    bwd = pltpu.roll(x, 16, 1)
    return jnp.where((lane & 31) < 16, fwd, bwd)


def _pre_kernel(xc_ref, xl_ref, mod_ref, gpre_ref, win_ref, gsgu_ref, gmat_ref, gq_ref, wuq_ref,
                gkv_ref, wukv_ref, cos_ref, sin_ref,
                u_ref, vn_ref, zf_ref, q_ref, kv_ref, kr_ref, ckv_ref, zkr_ref):
    i = pl.program_id(0)
    is_lat = i >= N_CTX // TB
    x = _pick(i, xc_ref, xl_ref)
    h = _rms(x, gpre_ref[...]) * (1.0 + mod_ref[1:2, :]) + mod_ref[0:1, :]
    z = jnp.dot(h.astype(BF16), win_ref[...], preferred_element_type=F32)

    ga = jax.nn.gelu(z[:, :2 * A_WIDTH])
    u_ref[...] = ga[:, :A_WIDTH]
    v = ga[:, A_WIDTH:]
    gmat = gmat_ref[...]
    dv = v - _split_dot(v, gmat)
    var = _split_dot(dv * dv, gmat)
    vn_ref[...] = (dv * lax.rsqrt(var + EPS) * gsgu_ref[...]).astype(BF16)

    zf_ref[...] = z[:, 512:768].astype(BF16)

    cos = jnp.where(is_lat, cos_ref[...], 1.0)
    sin = jnp.where(is_lat, sin_ref[...], 0.0)
    lane = lax.broadcasted_iota(jnp.int32, (TB, LANES), 1)

    qn = _rms(z[:, 768:1024], gq_ref[...])
    scale = (NOPE + ROPE) ** -0.5
    q = jnp.dot(qn.astype(BF16), wuq_ref[...], preferred_element_type=F32) * scale
    for hd in range(HEADS):
        base = hd * HEAD_PAD
        q_ref[:, base:base + NOPE] = q[:, base:base + NOPE].astype(BF16)
        qr = q[:, base + NOPE:base + HEAD_PAD]
        q_ref[:, base + NOPE:base + HEAD_PAD] = (qr * cos + _swap_halves(qr, lane) * sin).astype(BF16)

    ckv = _rms(z[:, 1024:1152], gkv_ref[...])
    ckv_ref[...] = ckv
    kv_ref[...] = jnp.dot(ckv.astype(BF16), wukv_ref[...], preferred_element_type=F32).astype(BF16)

    zkr = z[:, 1152:1280]
    zkr_ref[...] = zkr
    kr_ref[...] = (zkr * cos + _swap_halves(zkr, lane) * sin).astype(BF16)


def _pre_call(xc, xl, mods, g_pre, w_in, g_sgu, gmat, g_q, w_uq, g_kv, w_ukv, cos_t, sin_t):
    nb = N_TOK // TB
    first_lat = N_CTX // TB
    pos_blocks = LAT_T // TB

    def tok(width):
        return pl.BlockSpec((TB, width), lambda i: (i, 0))

    def full(shape):
        return pl.BlockSpec(shape, lambda i: (0,) * len(shape))

    def rope_map(i):
        return (jnp.where(i >= first_lat, (i - first_lat) % pos_blocks, 0), 0)

    out_shape = (
        jax.ShapeDtypeStruct((N_TOK, A_WIDTH), F32),
        jax.ShapeDtypeStruct((N_TOK, A_WIDTH), BF16),
        jax.ShapeDtypeStruct((N_TOK, F_WIDTH), BF16),
        jax.ShapeDtypeStruct((N_TOK, HEADS * HEAD_PAD), BF16),
        jax.ShapeDtypeStruct((N_TOK, HEADS * (NOPE + V_DIM)), BF16),
        jax.ShapeDtypeStruct((N_TOK, LANES), BF16),
        jax.ShapeDtypeStruct((N_TOK, KV_LORA), F32),
        jax.ShapeDtypeStruct((N_TOK, LANES), F32),
    )
    return pl.pallas_call(
        _pre_kernel,
        grid=(nb,),
        in_specs=[
            _ctx_spec(D), _lat_spec(D),
            pl.BlockSpec((None, 6, D), lambda i: (_mod_index(i), 0, 0)),
            full((1, D)), full((D, IN_PAD)), full((1, A_WIDTH)), full((A_WIDTH, A_WIDTH)),
            full((1, Q_LORA)), full((Q_LORA, HEADS * HEAD_PAD)),
            full((1, KV_LORA)), full((KV_LORA, HEADS * (NOPE + V_DIM))),
            pl.BlockSpec((TB, LANES), rope_map), pl.BlockSpec((TB, LANES), rope_map),
        ],
        out_specs=(tok(A_WIDTH), tok(A_WIDTH), tok(F_WIDTH), tok(HEADS * HEAD_PAD),
                   tok(HEADS * (NOPE + V_DIM)), tok(LANES), tok(KV_LORA), tok(LANES)),
        out_shape=out_shape,
        compiler_params=pltpu.CompilerParams(dimension_semantics=("parallel",),
                                             vmem_limit_bytes=VMEM_LIMIT),
        name="pre_mix",
    )(xc, xl, mods, g_pre, w_in, g_sgu, gmat, g_q, w_uq, g_kv, w_ukv, cos_t, sin_t)


def _mix_kernel(*refs, has_cache):
    if has_cache:
        (u_ref, vn_ref, zf_ref, q_ref, kv_ref, kr_ref, ws_ref, bs_ref, bdc_ref, bds_ref,
         ct_ref, st_ref, cckv_ref, ckr_ref, wukv_ref, o_ref) = refs
    else:
        (u_ref, vn_ref, zf_ref, q_ref, kv_ref, kr_ref, ws_ref, bs_ref, bdc_ref, bds_ref,
         ct_ref, st_ref, o_ref) = refs

    lane = lax.broadcasted_iota(jnp.int32, (CHUNK, A_WIDTH), 1)
    for c in range(QB // CHUNK):
        rows = slice(c * CHUNK, (c + 1) * CHUNK)
        vch = vn_ref[rows, :]
        s = bs_ref[...]
        for g in range(A_HEADS):
            sg = jnp.dot(ws_ref[g], vch, preferred_element_type=F32)
            in_head = (lane >= g * A_HEAD_DIM) & (lane < (g + 1) * A_HEAD_DIM)
            s = s + jnp.where(in_head, sg, 0.0)
        o_ref[rows, 0:A_WIDTH] = (u_ref[rows, :] * s).astype(BF16)

    zf = zf_ref[...]
    zc = jnp.dot(zf, bdc_ref[...], preferred_element_type=F32).astype(BF16)
    zs = jnp.dot(zf, bds_ref[...], preferred_element_type=F32).astype(BF16)
    yf = (jnp.dot(ct_ref[...], zc, preferred_element_type=F32)
          - jnp.dot(st_ref[...], zs, preferred_element_type=F32))
    o_ref[:, A_WIDTH:A_WIDTH + F_WIDTH] = yf.astype(BF16)

    kr = kr_ref[...]
    if has_cache:
        kvc = jnp.dot(cckv_ref[...].astype(BF16), wukv_ref[...],
                      preferred_element_type=F32).astype(BF16)
        krc = ckr_ref[...]
    for hd in range(HEADS):
        qh = q_ref[:, hd * HEAD_PAD:(hd + 1) * HEAD_PAD]
        kb = hd * (NOPE + V_DIM)
        kh = jnp.concatenate([kv_ref[:, kb:kb + NOPE], kr], axis=1)
        vh = kv_ref[:, kb + NOPE:kb + NOPE + V_DIM]
        s = _dot_nt(qh, kh)
        m = jnp.max(s, axis=-1, keepdims=True)
        if has_cache:
            khc = jnp.concatenate([kvc[:, kb:kb + NOPE], krc], axis=1)
            vhc = kvc[:, kb + NOPE:kb + NOPE + V_DIM]
            sc = _dot_nt(qh, khc)
            m = jnp.maximum(m, jnp.max(sc, axis=-1, keepdims=True))
        e = jnp.exp(s - m)
        den = jnp.sum(e, axis=-1, keepdims=True)
        o = jnp.dot(e.astype(BF16), vh, preferred_element_type=F32)
        if has_cache:
            ec = jnp.exp(sc - m)
            den = den + jnp.sum(ec, axis=-1, keepdims=True)
            o = o + jnp.dot(ec.astype(BF16), vhc, preferred_element_type=F32)
        ob = A_WIDTH + F_WIDTH + hd * V_DIM
        o_ref[:, ob:ob + V_DIM] = (o * (1.0 / den)).astype(BF16)


def _mix_call(pre, consts, seq_t, n_batch, tok_off, cache=None):
    u, vn, zf, q, kv, kr = pre
    ws, bs_full, bdc, bds, ct, st = consts
    nq = seq_t // QB
    qoff = tok_off // QB
    soff = tok_off // seq_t

    def qrow(width):
        return pl.BlockSpec((QB, width), lambda b, j: (qoff + b * nq + j, 0))

    def srow(width):
        return pl.BlockSpec((seq_t, width), lambda b, j: (soff + b, 0))

    def full(shape):
        return pl.BlockSpec(shape, lambda b, j: (0,) * len(shape))

    in_specs = [qrow(A_WIDTH), qrow(A_WIDTH), srow(F_WIDTH), qrow(HEADS * HEAD_PAD),
                srow(HEADS * (NOPE + V_DIM)), srow(LANES),
                full((A_HEADS, CHUNK, CHUNK)), full((CHUNK, A_WIDTH)),
                full((F_WIDTH, F_WIDTH)), full((F_WIDTH, F_WIDTH)),
                pl.BlockSpec((QB, seq_t), lambda b, j: (j, 0)),
                pl.BlockSpec((QB, seq_t), lambda b, j: (j, 0))]
    args = [u, vn, zf, q, kv, kr, ws, bs_full, bdc, bds, ct, st]
    if cache is not None:
        cckv, ckr, wukv = cache
        in_specs += [pl.BlockSpec((None, PAST, KV_LORA), lambda b, j: (b, 0, 0)),
                     pl.BlockSpec((None, PAST, LANES), lambda b, j: (b, 0, 0)),
                     full((KV_LORA, HEADS * (NOPE + V_DIM)))]
        args += [cckv, ckr, wukv]
    return pl.pallas_call(
        functools.partial(_mix_kernel, has_cache=cache is not None),
        grid=(n_batch, nq),
        in_specs=in_specs,
        out_specs=pl.BlockSpec((QB, D), lambda b, j: (b * nq + j, 0)),
        out_shape=jax.ShapeDtypeStruct((n_batch * seq_t, D), BF16),
        compiler_params=pltpu.CompilerParams(dimension_semantics=("parallel", "parallel"),
                                             vmem_limit_bytes=VMEM_LIMIT),
        name="mix_lat" if cache is not None else "mix_ctx",
    )(*args)


def _post_kernel(yc_ref, yl_ref, xc_ref, xl_ref, mod_ref, wout_ref, gpost_ref, gffn_ref, wr_ref, br_ref,
                 tri_ref, x1_ref, h2_ref, tab_ref, rw_ref, cnt_ref, carry_ref):
    i = pl.program_id(0)

    @pl.when(i == 0)
    def _():
        carry_ref[...] = jnp.zeros(carry_ref.shape, F32)

    y = jnp.dot(_pick(i, yc_ref, yl_ref), wout_ref[...], preferred_element_type=F32)
    x1 = _pick(i, xc_ref, xl_ref) + mod_ref[2:3, :] * _rms(y, gpost_ref[...])
    x1_ref[...] = x1
    h2 = _rms(x1, gffn_ref[...]) * (1.0 + mod_ref[4:5, :]) + mod_ref[3:4, :]
    _store_tiled(h2_ref, h2)

    wr = wr_ref[...]
    wr_hi = wr.astype(BF16)
    wr_lo = (wr - wr_hi.astype(F32)).astype(BF16)
    h_hi = h2.astype(BF16)
    h_lo = (h2 - h_hi.astype(F32)).astype(BF16)
    logits = (jnp.dot(h_hi, wr_hi, preferred_element_type=F32)
              + jnp.dot(h_lo, wr_hi, preferred_element_type=F32)
              + jnp.dot(h_hi, wr_lo, preferred_element_type=F32)) + br_ref[...]

    lane = lax.broadcasted_iota(jnp.int32, (TB, LANES), 1)
    lane_f = lane.astype(F32)
    work = logits
    idx, val = [], []
    for _ in range(TOP_K):
        m = jnp.max(work, axis=-1, keepdims=True)
        ik = jnp.min(jnp.where(work == m, lane_f, float(LANES)), axis=-1, keepdims=True)
        idx.append(ik)
        val.append(m)
        work = jnp.where(lane_f == ik, NEG, work)
    ex = [jnp.exp(v - val[0]) for v in val]
    den = ex[0] + ex[1] + ex[2] + ex[3]

    onehot = jnp.zeros((TB, LANES), F32)
    for k in range(TOP_K):
        onehot = onehot + jnp.where(lane_f == idx[k], 1.0, 0.0)
    before = jnp.dot(tri_ref[...], onehot.astype(BF16), preferred_element_type=F32) + carry_ref[0:1, :]
    ri = jnp.zeros((TB, LANES), F32)
    rw = jnp.zeros((TB, LANES), F32)
    for k in range(TOP_K):
        rank_k = jnp.sum(jnp.where(lane_f == idx[k], before, 0.0), axis=-1, keepdims=True)
        ri = ri + jnp.where(lane == k, idx[k], 0.0) + jnp.where(lane == TOP_K + k, rank_k, 0.0)
        rw = rw + jnp.where(lane == k, ex[k] / den, 0.0)
    tab_ref[...] = ri.T[0:2 * TOP_K, :].astype(jnp.int32)
    rw_ref[...] = rw
    total = carry_ref[0:1, :] + jnp.sum(onehot, axis=0, keepdims=True)
    carry_ref[...] = jnp.broadcast_to(total, carry_ref.shape)
    cnt_ref[...] = jnp.broadcast_to(total, cnt_ref.shape).astype(jnp.int32)


def _post_call(yc, yl, xc, xl, mods, w_out, g_post, g_ffn, w_r, b_r, tri):
    nb = N_TOK // TB

    def tok(width):
        return pl.BlockSpec((TB, width), lambda i: (i, 0))

    def full(shape):
        return pl.BlockSpec(shape, lambda i: (0,) * len(shape))

    return pl.pallas_call(
        _post_kernel,
        grid=(nb,),
        in_specs=[_ctx_spec(D), _lat_spec(D), _ctx_spec(D), _lat_spec(D),
                  pl.BlockSpec((None, 6, D), lambda i: (_mod_index(i), 0, 0)),
                  full((D, D)), full((1, D)), full((1, D)), full((D, LANES)), full((1, LANES)),
                  full((TB, TB))],
        out_specs=(tok(D), pl.BlockSpec((TB * SUB, LANES), lambda i: (i, 0)),
                   pl.BlockSpec((None, 2 * TOP_K, TB), lambda i: (i, 0, 0)),
                   tok(LANES), full((8, LANES))),
        out_shape=(jax.ShapeDtypeStruct((N_TOK, D), F32),
                   jax.ShapeDtypeStruct((N_TOK * SUB, LANES), F32),
                   jax.ShapeDtypeStruct((nb, 2 * TOP_K, TB), jnp.int32),
                   jax.ShapeDtypeStruct((N_TOK, LANES), F32),
                   jax.ShapeDtypeStruct((8, LANES), jnp.int32)),
        scratch_shapes=[pltpu.VMEM((8, LANES), F32)],
        compiler_params=pltpu.CompilerParams(dimension_semantics=("arbitrary",),
                                             vmem_limit_bytes=VMEM_LIMIT),
        name="post_mix_router",
    )(yc, yl, xc, xl, mods, w_out, g_post, g_ffn, w_r, b_r, tri)


def _dest_kernel(bs_ref, tab_ref, o_ref):
    idx = tab_ref[:, 0:TOP_K, :]
    base = jnp.zeros(idx.shape, jnp.int32)
    for e in range(N_EXPERTS):
        base = jnp.where(idx == e, bs_ref[e] * RB, base)
    o_ref[...] = base + tab_ref[:, TOP_K:2 * TOP_K, :]


def _dest_call(blk_start, tab):
    nb = N_TOK // TB
    return pl.pallas_call(
        _dest_kernel,
        in_specs=[pl.BlockSpec(memory_space=pltpu.SMEM), pl.BlockSpec(memory_space=pltpu.VMEM)],
        out_specs=pl.BlockSpec(memory_space=pltpu.VMEM),
        out_shape=jax.ShapeDtypeStruct((nb, TOP_K, TB), jnp.int32),
        name="dest_rows",
    )(blk_start, tab)


def _dispatch_kernel(zrow_ref, nu_ref, dest_ref, h2_ref, xs_ref, zero_ref, sem, zsem):
    i = pl.program_id(0)

    def zero_fill(row):
        return pltpu.make_async_copy(zero_ref, _tile_rows(xs_ref, row, RB), zsem)

    @pl.when(i == 0)
    def _():
        zero_ref[...] = jnp.zeros(zero_ref.shape, F32)
        for e in range(N_EXPERTS):
            zero_fill(zrow_ref[e]).start()
        for e in range(N_EXPERTS):
            zero_fill(zrow_ref[e]).wait()

        def tail_start(b, carry):
            zero_fill(b * RB).start()
            return carry

        def tail_wait(b, carry):
            zero_fill(b * RB).wait()
            return carry

        lax.fori_loop(nu_ref[0], N_BLOCKS, tail_start, 0)
        lax.fori_loop(nu_ref[0], N_BLOCKS, tail_wait, 0)

    def body(t, carry):
        for k in range(TOP_K):
            pltpu.make_async_copy(_tile_rows(h2_ref, t), _tile_rows(xs_ref, dest_ref[0, k * TB + t]),
                                  sem).start(priority=k % 2)
        return carry

    lax.fori_loop(0, TB, body, 0, unroll=8)
    for k in range(TOP_K):
        pltpu.make_async_copy(h2_ref, _tile_rows(xs_ref, 0, TB), sem).wait()


def _dispatch_call(zrow, n_used, dest, h2):
    nb = N_TOK // TB
    grid_spec = pltpu.PrefetchScalarGridSpec(
        num_scalar_prefetch=2,
        grid=(nb,),
        in_specs=[
            pl.BlockSpec((None, 1, TOP_K * TB), lambda i, z, n: (i, 0, 0), memory_space=pltpu.SMEM),
            pl.BlockSpec((TB * SUB, LANES), lambda i, z, n: (i, 0)),
        ],
        out_specs=pl.BlockSpec(memory_space=pl.ANY),
        scratch_shapes=[pltpu.VMEM((RB * SUB, LANES), F32), pltpu.SemaphoreType.DMA,
                        pltpu.SemaphoreType.DMA],
    )
    return pl.pallas_call(
        _dispatch_kernel,
        grid_spec=grid_spec,
        out_shape=jax.ShapeDtypeStruct((N_ROWS * SUB, LANES), F32),
        compiler_params=pltpu.CompilerParams(dimension_semantics=("arbitrary",),
                                             vmem_limit_bytes=VMEM_LIMIT),
        name="dispatch_rows",
    )(zrow, n_used, dest, h2)


def _moe_kernel(be_ref, first_ref, nxt_ref, slot_ref, nu_ref, xs_ref, wgu_hbm, bgu_ref, wdn_hbm, bdn_ref,
                o_ref, wgu_f, wdn_f, sems, *, layer):
    b = pl.program_id(0)
    used = b < nu_ref[0]

    def fetch(e, s):
        return (pltpu.make_async_copy(wgu_hbm.at[layer, e], wgu_f.at[s], sems.at[0, s]),
                pltpu.make_async_copy(wdn_hbm.at[layer, e], wdn_f.at[s], sems.at[1, s]))

    @pl.when(b == 0)
    def _():
        for cp in fetch(be_ref[0], 0):
            cp.start(priority=1)

    @pl.when(first_ref[b] == 1)
    def _():
        s = slot_ref[b]
        for cp in fetch(be_ref[b], s):
            cp.wait()

        @pl.when(nxt_ref[b] >= 0)
        def _():
            for cp in fetch(nxt_ref[b], 1 - s):
                cp.start(priority=1)

    @pl.when(used)
    def _():
        s = slot_ref[b]
        x = _load_tiled(xs_ref, RB).astype(BF16)
        gu = jnp.dot(x, wgu_f[s].astype(BF16), preferred_element_type=F32) + bgu_ref[...]
        g = jnp.minimum(gu[:, :D_FF], SWIGLU_LIMIT)
        l = jnp.clip(gu[:, D_FF:], -SWIGLU_LIMIT, SWIGLU_LIMIT)
        a = g * jax.nn.sigmoid(SWIGLU_ALPHA * g) * (l + 1.0)
        y = jnp.dot(a.astype(BF16), wdn_f[s].astype(BF16), preferred_element_type=F32) + bdn_ref[...]
        _store_tiled(o_ref, y)

    @pl.when(jnp.logical_not(used))
    def _():
        o_ref[...] = jnp.zeros(o_ref.shape, o_ref.dtype)


def _moe_call(layer, tables, xs, w_gu, b_gu, w_dn, b_dn):
    def rows_in(b, be, fi, nx, sl, nu):
        return (jnp.minimum(b, nu[0] - 1), 0)

    def rows_out(b, be, fi, nx, sl, nu):
        return (b, 0)

    def expert(b, be, fi, nx, sl, nu):
        return (layer, be[b], 0, 0)

    grid_spec = pltpu.PrefetchScalarGridSpec(
        num_scalar_prefetch=5,
        grid=(N_BLOCKS,),
        in_specs=[
            pl.BlockSpec((RB * SUB, LANES), rows_in),
            pl.BlockSpec(memory_space=pl.ANY),
            pl.BlockSpec((None, None, 1, 2 * D_FF), expert),
            pl.BlockSpec(memory_space=pl.ANY),
            pl.BlockSpec((None, None, 1, D), expert),
        ],
        out_specs=pl.BlockSpec((RB * SUB, LANES), rows_out),
        scratch_shapes=[pltpu.VMEM((2, D, 2 * D_FF), F32), pltpu.VMEM((2, D_FF, D), F32),
                        pltpu.SemaphoreType.DMA((2, 2))],
    )
    return pl.pallas_call(
        functools.partial(_moe_kernel, layer=layer),
        grid_spec=grid_spec,
        out_shape=jax.ShapeDtypeStruct((N_ROWS * SUB, LANES), F32),
        compiler_params=pltpu.CompilerParams(dimension_semantics=("arbitrary",),
                                             vmem_limit_bytes=VMEM_LIMIT),
        name="moe_experts",
    )(*tables, xs, w_gu, b_gu, w_dn, b_dn)


def _final_kernel(dcur_ref, dnxt_ref, x1_ref, ys_ref, rw_ref, mod_ref, g_ref, oc_ref, ol_ref, buf, sems):
    i = pl.program_id(0)
    nb = pl.num_programs(0)
    slot = i % 2

    def gather(dest_ref, s):
        def body(t, carry):
            for k in range(TOP_K):
                pltpu.make_async_copy(_tile_rows(ys_ref, dest_ref[0, k * TB + t]),
                                      _tile_rows(buf.at[s], k * TB + t), sems.at[s]).start(priority=k % 2)
            return carry

        lax.fori_loop(0, TB, body, 0, unroll=8)

    @pl.when(i == 0)
    def _():
        gather(dcur_ref, 0)

    @pl.when(i + 1 < nb)
    def _():
        gather(dnxt_ref, 1 - slot)

    pltpu.make_async_copy(_tile_rows(ys_ref, 0, TOP_K * TB), buf.at[slot], sems.at[slot]).wait()

    rw = rw_ref[...]
    y = jnp.zeros((TB, D), F32)
    for k in range(TOP_K):
        yk = jnp.concatenate([buf[slot, pl.ds(k * TB * SUB + s, TB, stride=SUB), :] for s in range(SUB)],
                             axis=1)
        y = y + yk * rw[:, k:k + 1]
    x2 = x1_ref[...] + mod_ref[5:6, :] * _rms(y, g_ref[...])

    @pl.when(i < N_CTX // TB)
    def _():
        oc_ref[...] = x2

    @pl.when(i >= N_CTX // TB)
    def _():
        ol_ref[...] = x2


def _final_call(dest_flat, x1, ys, rw, mods, g_post_ffn):
    nb = N_TOK // TB
    return pl.pallas_call(
        _final_kernel,
        grid=(nb,),
        in_specs=[pl.BlockSpec((None, 1, TOP_K * TB), lambda i: (i, 0, 0), memory_space=pltpu.SMEM),
                  pl.BlockSpec((None, 1, TOP_K * TB), lambda i: (jnp.minimum(i + 1, nb - 1), 0, 0),
                               memory_space=pltpu.SMEM),
                  pl.BlockSpec((TB, D), lambda i: (i, 0)),
                  pl.BlockSpec(memory_space=pl.ANY),
                  pl.BlockSpec((TB, LANES), lambda i: (i, 0)),
                  pl.BlockSpec((None, 6, D), lambda i: (_mod_index(i), 0, 0)),
                  pl.BlockSpec((1, D), lambda i: (0, 0))],
        out_specs=(_ctx_spec(D), _lat_spec(D)),
        out_shape=(jax.ShapeDtypeStruct((N_CTX, D), F32), jax.ShapeDtypeStruct((N_LAT, D), F32)),
        scratch_shapes=[pltpu.VMEM((2, TOP_K * TB * SUB, LANES), F32), pltpu.SemaphoreType.DMA((2,))],
        compiler_params=pltpu.CompilerParams(dimension_semantics=("arbitrary",),
                                             vmem_limit_bytes=VMEM_LIMIT),
        name="combine_residual",
    )(dest_flat, dest_flat, x1, ys, rw, mods, g_post_ffn)


def _dft_tables(t):
    j = np.arange(t, dtype=np.int64)
    ang = 2.0 * np.pi * ((j[:, None] * j[None, :]) % t) / t
    return (np.cos(ang) / math.sqrt(t)).astype(np.float32), (np.sin(ang) / math.sqrt(t)).astype(np.float32)


def _channel_dft():
    c = np.arange(F_GROUP_DIM, dtype=np.int64)
    ang = 2.0 * np.pi * ((c[:, None] * c[None, :]) % F_GROUP_DIM) / F_GROUP_DIM
    eye = np.eye(F_GROUPS)
    bdc = np.kron(eye, np.cos(ang)) / math.sqrt(F_GROUP_DIM)
    bds = np.kron(eye, np.sin(ang)) / math.sqrt(F_GROUP_DIM)
    return bdc.astype(np.float32), bds.astype(np.float32)


def _rope_tables():
    pos = np.arange(LAT_T)
    n = ROPE // 4
    inv_freq = np.power(np.float32(ROPE_BASE), -np.arange(n, dtype=np.float32) / np.float32(n))
    ang_r = (pos // GRID_W).astype(np.float32)[:, None] * inv_freq
    ang_c = (pos % GRID_W).astype(np.float32)[:, None] * inv_freq
    cos = np.concatenate([np.cos(ang_r), np.cos(ang_r), np.cos(ang_c), np.cos(ang_c),
                          np.ones((LAT_T, LANES - ROPE))], axis=1)
    sin = np.concatenate([-np.sin(ang_r), np.sin(ang_r), -np.sin(ang_c), np.sin(ang_c),
                          np.zeros((LAT_T, LANES - ROPE))], axis=1)
    return cos.astype(np.float32), sin.astype(np.float32)


def kernel(x_prompt, x_sample, cache_ckv, cache_krope, c, c_ctx, w_ada, b_ada, g_pre_mix, g_post_mix, g_pre_ffn, g_post_ffn, w_in, g_sgu, w_spatial, b_spatial, g_q, w_uq, g_kv, w_ukv, w_out, w_router, b_router, w_gate_up, b_gate_up, w_down, b_down):
    xc, xl = x_prompt.reshape(N_CTX, D), x_sample.reshape(N_LAT, D)

    cond = jnp.concatenate([c_ctx[None, :], c, jnp.zeros((5, D), F32)], axis=0)
    mods = _mod_call(cond.T, w_ada, b_ada.reshape(DEPTH, 1, 6 * D))
    mods = mods[:, :3].reshape(DEPTH, 3, 6, D)

    gmat = jnp.asarray(np.kron(np.eye(A_HEADS), np.full((A_HEAD_DIM, A_HEAD_DIM), 1.0 / A_HEAD_DIM)),
                       dtype=BF16)
    bdc_np, bds_np = _channel_dft()
    bdc, bds = jnp.asarray(bdc_np).astype(BF16), jnp.asarray(bds_np).astype(BF16)
    dft = {}
    for t in (CTX_T, LAT_T):
        ct_np, st_np = _dft_tables(t)
        dft[t] = (jnp.asarray(ct_np).astype(BF16), jnp.asarray(st_np).astype(BF16))
    cos_np, sin_np = _rope_tables()
    cos_t, sin_t = jnp.asarray(cos_np), jnp.asarray(sin_np)
    tri = jnp.asarray(np.tril(np.ones((TB, TB), np.float32), k=-1)).astype(BF16)

    b_gu = b_gate_up.reshape(DEPTH, N_EXPERTS, 1, 2 * D_FF)
    b_dn = b_down.reshape(DEPTH, N_EXPERTS, 1, D)

    ckv_layers, krope_layers = [], []
    for i in range(DEPTH):
        w_in_p = jnp.pad(w_in[i], ((0, 0), (0, IN_PAD - w_in.shape[-1]))).astype(BF16)
        w_uq_p = jnp.pad(w_uq[i].reshape(Q_LORA, HEADS, NOPE + ROPE),
                         ((0, 0), (0, 0), (0, HEAD_PAD - NOPE - ROPE))).reshape(Q_LORA, HEADS * HEAD_PAD)
        w_uq_p = w_uq_p.astype(BF16)
        w_ukv_b = w_ukv[i].astype(BF16)
        pre = _pre_call(xc, xl, mods[i], g_pre_mix[i][None, :], w_in_p, g_sgu[i][None, :], gmat,
                        g_q[i][None, :], w_uq_p, g_kv[i][None, :], w_ukv_b, cos_t, sin_t)
        u, vn, zf, q, kv, kr, ckv, zkr = pre
        ckv_layers.append(ckv[:N_CTX].reshape(N_CTX_B, CTX_T, KV_LORA))
        krope_layers.append(zkr[:N_CTX, :ROPE].reshape(N_CTX_B, CTX_T, ROPE))

        ws = w_spatial[i].astype(BF16)
        bs_full = jnp.repeat(b_spatial[i].T, A_HEAD_DIM, axis=1)
        mix_in = (u, vn, zf, q, kv, kr)
        y_ctx = _mix_call(mix_in, (ws, bs_full, bdc, bds) + dft[CTX_T], CTX_T, N_CTX_B, 0)
        ckr_p = jnp.pad(cache_krope[:, i], ((0, 0), (0, 0), (0, LANES - ROPE))).astype(BF16)
        y_lat = _mix_call(mix_in, (ws, bs_full, bdc, bds) + dft[LAT_T], LAT_T, N_LAT_B, N_CTX,
                          cache=(cache_ckv[:, i], ckr_p, w_ukv_b))

        w_r = jnp.pad(w_router[i], ((0, 0), (0, LANES - N_EXPERTS)))
        b_r = jnp.pad(b_router[i], (0, LANES - N_EXPERTS), constant_values=NEG)[None, :]
        x1, h2, tab, rw, cnt = _post_call(y_ctx, y_lat, xc, xl, mods[i], w_out[i].astype(BF16),
                                          g_post_mix[i][None, :], g_pre_ffn[i][None, :], w_r, b_r, tri)

        counts = cnt[0, :N_EXPERTS]
        nblk = (counts + RB - 1) // RB
        blk_end = jnp.cumsum(nblk)
        blk_start = blk_end - nblk
        n_used = blk_end[-1:]
        zrow = blk_start * RB + counts // 8 * 8
        dest = _dest_call(blk_start, tab)
        dest_flat = dest.reshape(N_TOK // TB, 1, TOP_K * TB)
        xs = _dispatch_call(zrow, n_used, dest_flat, h2)
        blk_id = jnp.arange(N_BLOCKS, dtype=jnp.int32)
        blk = jnp.minimum(blk_id, n_used - 1)
        block_e = jnp.sum((blk_end[None, :] <= blk[:, None]).astype(jnp.int32), axis=1)
        block_e = jnp.minimum(block_e, N_EXPERTS - 1)
        eid = jnp.arange(N_EXPERTS, dtype=jnp.int32)
        owns = nblk > 0
        later = jnp.where((eid[None, :] > eid[:, None]) & owns[None, :], eid[None, :], N_EXPERTS)
        nxt_e = jnp.min(later, axis=1)
        nxt_e = jnp.where(nxt_e == N_EXPERTS, -1, nxt_e)
        run_id = jnp.cumsum(owns.astype(jnp.int32)) - 1
        first = ((blk_id == blk_start[block_e]) & (blk_id < n_used)).astype(jnp.int32)
        tables = (block_e, first, nxt_e[block_e], run_id[block_e] % 2, n_used)
        ys = _moe_call(i, tables, xs, w_gate_up, b_gu, w_down, b_dn)
        xc, xl = _final_call(dest_flat, x1, ys, rw, mods[i], g_post_ffn[i][None, :])

    y_prompt = xc.reshape(N_CTX_B, CTX_T, D)
    y_sample = xl.reshape(N_LAT_B, LAT_T, D)
    return (y_prompt, y_sample, jnp.stack(ckv_layers, axis=1), jnp.stack(krope_layers, axis=1))
```

```python
import functools
import math

import jax
import jax.numpy as jnp
import numpy as np
from jax import lax
from jax.experimental import pallas as pl
from jax.experimental.pallas import tpu as pltpu

F32 = jnp.float32
BF16 = jnp.bfloat16

D = 1024
N_CTX_B, CTX_T = 16, 256
N_LAT_B, LAT_T = 2, 1024
PAST = 512
N_CTX = N_CTX_B * CTX_T
N_LAT = N_LAT_B * LAT_T
N_TOK = N_CTX + N_LAT
DEPTH = 2
GRID_W = 64
EPS = 1e-6
A_HEADS, A_HEAD_DIM, A_WIDTH, CHUNK = 4, 64, 256, 128
F_GROUPS, F_GROUP_DIM, F_WIDTH = 4, 64, 256
HEADS, Q_LORA, KV_LORA, NOPE, ROPE, V_DIM = 4, 256, 128, 128, 64, 128
HEAD_PAD = 256
IN_PAD = 1280
N_EXPERTS, TOP_K, D_FF = 32, 4, 1024
SWIGLU_LIMIT, SWIGLU_ALPHA = 7.0, 1.702
ROPE_BASE = 10000.0

TB = 512
QB = 256
RB = 256
W_SLOTS = 3
LANES = 128
SUB = D // LANES
N_BLOCKS = N_TOK * TOP_K // RB + N_EXPERTS
N_ROWS = N_BLOCKS * RB
NEG = -3.0e38
VMEM_LIMIT = 56 * 1024 * 1024


def _rms(x, g):
    return x * lax.rsqrt(jnp.mean(x * x, axis=-1, keepdims=True) + EPS) * g


def _split_dot(v, m):
    hi = v.astype(BF16)
    lo = (v - hi.astype(F32)).astype(BF16)
    return (jnp.dot(hi, m, preferred_element_type=F32)
            + jnp.dot(lo, m, preferred_element_type=F32))


def _dot_nt(a, b):
    return lax.dot_general(a, b, (((1,), (1,)), ((), ())), preferred_element_type=F32)


def _store_tiled(ref, x):
    rows = x.shape[0]
    for s in range(SUB):
        ref[pl.ds(s, rows, stride=SUB), :] = x[:, s * LANES:(s + 1) * LANES]


def _load_tiled(ref, rows):
    return jnp.concatenate([ref[pl.ds(s, rows, stride=SUB), :] for s in range(SUB)], axis=1)


def _tile_rows(ref, row, n_rows=1):
    return ref.at[pl.ds(pl.multiple_of(row * SUB, SUB), n_rows * SUB)]


def _mod_kernel(ct_ref, w_ref, b_ref, o_ref):
    ct = ct_ref[...]
    s = ct * jax.nn.sigmoid(ct)
    w = w_ref[...]
    o_ref[...] = jnp.zeros(o_ref.shape, F32)
    for r in range(3):
        o_ref[r:r + 1, :] = jnp.sum(w * s[:, r:r + 1], axis=0, keepdims=True) + b_ref[...]


def _mod_call(cond_t, w_ada, b_ada):
    cb = 2048
    return pl.pallas_call(
        _mod_kernel,
        grid=(DEPTH, 6 * D // cb),
        in_specs=[
            pl.BlockSpec((D, 8), lambda l, j: (0, 0)),
            pl.BlockSpec((None, D, cb), lambda l, j: (l, 0, j)),
            pl.BlockSpec((None, 1, cb), lambda l, j: (l, 0, j)),
        ],
        out_specs=pl.BlockSpec((None, 8, cb), lambda l, j: (l, 0, j)),
        out_shape=jax.ShapeDtypeStruct((DEPTH, 8, 6 * D), F32),
        compiler_params=pltpu.CompilerParams(dimension_semantics=("parallel", "parallel")),
        name="modulation",
    )(cond_t, w_ada, b_ada)


def _mod_index(i):
    first_lat = N_CTX // TB
    return jnp.where(i < first_lat, 0, 1 + (i - first_lat) // (LAT_T // TB))


def _ctx_spec(width):
    return pl.BlockSpec((TB, width), lambda i: (jnp.minimum(i, N_CTX // TB - 1), 0))


def _lat_spec(width):
    return pl.BlockSpec((TB, width), lambda i: (jnp.maximum(i - N_CTX // TB, 0), 0))


def _pick(i, ctx_ref, lat_ref):
    return jnp.where(i >= N_CTX // TB, lat_ref[...], ctx_ref[...])


def _swap_halves(x, lane):
    w = x.shape[-1]
    fwd = pltpu.roll(x, w - 16, 1)
    bwd = pltpu.roll(x, 16, 1)
    return jnp.where((lane & 31) < 16, fwd, bwd)


def _pre_kernel(xc_ref, xl_ref, mod_ref, gpre_ref, win_ref, gsgu_ref, gmat_ref, gq_ref, wuq_ref,
                gkv_ref, wukv_ref, cos_ref, sin_ref,
                u_ref, vn_ref, zf_ref, q_ref, kv_ref, kr_ref, ckv_ref, zkr_ref):
    i = pl.program_id(0)
    is_lat = i >= N_CTX // TB
    x = _pick(i, xc_ref, xl_ref)
    h = _rms(x, gpre_ref[...]) * (1.0 + mod_ref[1:2, :]) + mod_ref[0:1, :]
    z = jnp.dot(h.astype(BF16), win_ref[...], preferred_element_type=F32)

    ga = jax.nn.gelu(z[:, :2 * A_WIDTH])
    u_ref[...] = ga[:, :A_WIDTH]
    v = ga[:, A_WIDTH:]
    gmat = gmat_ref[...]
    dv = v - _split_dot(v, gmat)
    var = _split_dot(dv * dv, gmat)
    vn_ref[...] = (dv * lax.rsqrt(var + EPS) * gsgu_ref[...]).astype(BF16)

    zf_ref[...] = z[:, 512:768].astype(BF16)

    cos = jnp.where(is_lat, cos_ref[...], 1.0)
    sin = jnp.where(is_lat, sin_ref[...], 0.0)
    lane = lax.broadcasted_iota(jnp.int32, (TB, LANES), 1)

    qn = _rms(z[:, 768:1024], gq_ref[...])
    scale = (NOPE + ROPE) ** -0.5
    q = jnp.dot(qn.astype(BF16), wuq_ref[...], preferred_element_type=F32) * scale
    for hd in range(HEADS):
        base = hd * HEAD_PAD
        q_ref[:, base:base + NOPE] = q[:, base:base + NOPE].astype(BF16)
        qr = q[:, base + NOPE:base + HEAD_PAD]
        q_ref[:, base + NOPE:base + HEAD_PAD] = (qr * cos + _swap_halves(qr, lane) * sin).astype(BF16)

    ckv = _rms(z[:, 1024:1152], gkv_ref[...])
    ckv_ref[...] = ckv
    kv_ref[...] = jnp.dot(ckv.astype(BF16), wukv_ref[...], preferred_element_type=F32).astype(BF16)

    zkr = z[:, 1152:1280]
    zkr_ref[...] = zkr
    kr_ref[...] = (zkr * cos + _swap_halves(zkr, lane) * sin).astype(BF16)


def _pre_call(xc, xl, mods, g_pre, w_in, g_sgu, gmat, g_q, w_uq, g_kv, w_ukv, cos_t, sin_t):
    nb = N_TOK // TB
    first_lat = N_CTX // TB
    pos_blocks = LAT_T // TB

    def tok(width):
        return pl.BlockSpec((TB, width), lambda i: (i, 0))

    def full(shape):
        return pl.BlockSpec(shape, lambda i: (0,) * len(shape))

    def rope_map(i):
        return (jnp.where(i >= first_lat, (i - first_lat) % pos_blocks, 0), 0)

    out_shape = (
        jax.ShapeDtypeStruct((N_TOK, A_WIDTH), F32),
        jax.ShapeDtypeStruct((N_TOK, A_WIDTH), BF16),
        jax.ShapeDtypeStruct((N_TOK, F_WIDTH), BF16),
        jax.ShapeDtypeStruct((N_TOK, HEADS * HEAD_PAD), BF16),
        jax.ShapeDtypeStruct((N_TOK, HEADS * (NOPE + V_DIM)), BF16),
        jax.ShapeDtypeStruct((N_TOK, LANES), BF16),
        jax.ShapeDtypeStruct((N_TOK, KV_LORA), F32),
        jax.ShapeDtypeStruct((N_TOK, LANES), F32),
    )
    return pl.pallas_call(
        _pre_kernel,
        grid=(nb,),
        in_specs=[
            _ctx_spec(D), _lat_spec(D),
            pl.BlockSpec((None, 6, D), lambda i: (_mod_index(i), 0, 0)),
            full((1, D)), full((D, IN_PAD)), full((1, A_WIDTH)), full((A_WIDTH, A_WIDTH)),
            full((1, Q_LORA)), full((Q_LORA, HEADS * HEAD_PAD)),
            full((1, KV_LORA)), full((KV_LORA, HEADS * (NOPE + V_DIM))),
            pl.BlockSpec((TB, LANES), rope_map), pl.BlockSpec((TB, LANES), rope_map),
        ],
        out_specs=(tok(A_WIDTH), tok(A_WIDTH), tok(F_WIDTH), tok(HEADS * HEAD_PAD),
                   tok(HEADS * (NOPE + V_DIM)), tok(LANES), tok(KV_LORA), tok(LANES)),
        out_shape=out_shape,
        compiler_params=pltpu.CompilerParams(dimension_semantics=("parallel",),
                                             vmem_limit_bytes=VMEM_LIMIT),
        name="pre_mix",
    )(xc, xl, mods, g_pre, w_in, g_sgu, gmat, g_q, w_uq, g_kv, w_ukv, cos_t, sin_t)


def _mix_kernel(*refs, has_cache):
    if has_cache:
        (u_ref, vn_ref, zf_ref, q_ref, kv_ref, kr_ref, ws_ref, bs_ref, bdc_ref, bds_ref,
         ct_ref, st_ref, cckv_ref, ckr_ref, wukv_ref, o_ref) = refs
    else:
        (u_ref, vn_ref, zf_ref, q_ref, kv_ref, kr_ref, ws_ref, bs_ref, bdc_ref, bds_ref,
         ct_ref, st_ref, o_ref) = refs

    lane = lax.broadcasted_iota(jnp.int32, (CHUNK, A_WIDTH), 1)
    for c in range(QB // CHUNK):
        rows = slice(c * CHUNK, (c + 1) * CHUNK)
        vch = vn_ref[rows, :]
        s = bs_ref[...]
        for g in range(A_HEADS):
            sg = jnp.dot(ws_ref[g], vch, preferred_element_type=F32)
            in_head = (lane >= g * A_HEAD_DIM) & (lane < (g + 1) * A_HEAD_DIM)
            s = s + jnp.where(in_head, sg, 0.0)
        o_ref[rows, 0:A_WIDTH] = (u_ref[rows, :] * s).astype(BF16)

    zf = zf_ref[...]
    zc = jnp.dot(zf, bdc_ref[...], preferred_element_type=F32).astype(BF16)
    zs = jnp.dot(zf, bds_ref[...], preferred_element_type=F32).astype(BF16)
    yf = (jnp.dot(ct_ref[...], zc, preferred_element_type=F32)
          - jnp.dot(st_ref[...], zs, preferred_element_type=F32))
    o_ref[:, A_WIDTH:A_WIDTH + F_WIDTH] = yf.astype(BF16)

    kr = kr_ref[...]
    if has_cache:
        kvc = jnp.dot(cckv_ref[...].astype(BF16), wukv_ref[...],
                      preferred_element_type=F32).astype(BF16)
        krc = ckr_ref[...]
    for hd in range(HEADS):
        qh = q_ref[:, hd * HEAD_PAD:(hd + 1) * HEAD_PAD]
        kb = hd * (NOPE + V_DIM)
        kh = jnp.concatenate([kv_ref[:, kb:kb + NOPE], kr], axis=1)
        vh = kv_ref[:, kb + NOPE:kb + NOPE + V_DIM]
        s = _dot_nt(qh, kh)
        m = jnp.max(s, axis=-1, keepdims=True)
        if has_cache:
            khc = jnp.concatenate([kvc[:, kb:kb + NOPE], krc], axis=1)
            vhc = kvc[:, kb + NOPE:kb + NOPE + V_DIM]
            sc = _dot_nt(qh, khc)
            m = jnp.maximum(m, jnp.max(sc, axis=-1, keepdims=True))
        e = jnp.exp(s - m)
        den = jnp.sum(e, axis=-1, keepdims=True)
        o = jnp.dot(e.astype(BF16), vh, preferred_element_type=F32)
        if has_cache:
            ec = jnp.exp(sc - m)
            den = den + jnp.sum(ec, axis=-1, keepdims=True)
            o = o + jnp.dot(ec.astype(BF16), vhc, preferred_element_type=F32)
        ob = A_WIDTH + F_WIDTH + hd * V_DIM
        o_ref[:, ob:ob + V_DIM] = (o * (1.0 / den)).astype(BF16)


def _mix_call(pre, consts, seq_t, n_batch, tok_off, cache=None):
    u, vn, zf, q, kv, kr = pre
    ws, bs_full, bdc, bds, ct, st = consts
    nq = seq_t // QB
    qoff = tok_off // QB
    soff = tok_off // seq_t

    def qrow(width):
        return pl.BlockSpec((QB, width), lambda b, j: (qoff + b * nq + j, 0))

    def srow(width):
        return pl.BlockSpec((seq_t, width), lambda b, j: (soff + b, 0))

    def full(shape):
        return pl.BlockSpec(shape, lambda b, j: (0,) * len(shape))

    in_specs = [qrow(A_WIDTH), qrow(A_WIDTH), srow(F_WIDTH), qrow(HEADS * HEAD_PAD),
                srow(HEADS * (NOPE + V_DIM)), srow(LANES),
                full((A_HEADS, CHUNK, CHUNK)), full((CHUNK, A_WIDTH)),
                full((F_WIDTH, F_WIDTH)), full((F_WIDTH, F_WIDTH)),
                pl.BlockSpec((QB, seq_t), lambda b, j: (j, 0)),
                pl.BlockSpec((QB, seq_t), lambda b, j: (j, 0))]
    args = [u, vn, zf, q, kv, kr, ws, bs_full, bdc, bds, ct, st]
    if cache is not None:
        cckv, ckr, wukv = cache
        in_specs += [pl.BlockSpec((None, PAST, KV_LORA), lambda b, j: (b, 0, 0)),
                     pl.BlockSpec((None, PAST, LANES), lambda b, j: (b, 0, 0)),
                     full((KV_LORA, HEADS * (NOPE + V_DIM)))]
        args += [cckv, ckr, wukv]
    return pl.pallas_call(
        functools.partial(_mix_kernel, has_cache=cache is not None),
        grid=(n_batch, nq),
        in_specs=in_specs,
        out_specs=pl.BlockSpec((QB, D), lambda b, j: (b * nq + j, 0)),
        out_shape=jax.ShapeDtypeStruct((n_batch * seq_t, D), BF16),
        compiler_params=pltpu.CompilerParams(dimension_semantics=("parallel", "parallel"),
                                             vmem_limit_bytes=VMEM_LIMIT),
        name="mix_lat" if cache is not None else "mix_ctx",
    )(*args)


def _post_kernel(yc_ref, yl_ref, xc_ref, xl_ref, mod_ref, wout_ref, gpost_ref, gffn_ref, wr_ref, br_ref,
                 tri_ref, x1_ref, h2_ref, tab_ref, rw_ref, cnt_ref, carry_ref):
    i = pl.program_id(0)

    @pl.when(i == 0)
    def _():
        carry_ref[...] = jnp.zeros(carry_ref.shape, F32)

    y = jnp.dot(_pick(i, yc_ref, yl_ref), wout_ref[...], preferred_element_type=F32)
    x1 = _pick(i, xc_ref, xl_ref) + mod_ref[2:3, :] * _rms(y, gpost_ref[...])
    x1_ref[...] = x1
    h2 = _rms(x1, gffn_ref[...]) * (1.0 + mod_ref[4:5, :]) + mod_ref[3:4, :]
    _store_tiled(h2_ref, h2)

    wr = wr_ref[...]
    wr_hi = wr.astype(BF16)
    wr_lo = (wr - wr_hi.astype(F32)).astype(BF16)
    h_hi = h2.astype(BF16)
    h_lo = (h2 - h_hi.astype(F32)).astype(BF16)
    logits = (jnp.dot(h_hi, wr_hi, preferred_element_type=F32)
              + jnp.dot(h_lo, wr_hi, preferred_element_type=F32)
              + jnp.dot(h_hi, wr_lo, preferred_element_type=F32)) + br_ref[...]

    lane = lax.broadcasted_iota(jnp.int32, (TB, LANES), 1)
    lane_f = lane.astype(F32)
    work = logits
    idx, val = [], []
    for _ in range(TOP_K):
        m = jnp.max(work, axis=-1, keepdims=True)
        ik = jnp.min(jnp.where(work == m, lane_f, float(LANES)), axis=-1, keepdims=True)
        idx.append(ik)
        val.append(m)
        work = jnp.where(lane_f == ik, NEG, work)
    ex = [jnp.exp(v - val[0]) for v in val]
    den = ex[0] + ex[1] + ex[2] + ex[3]

    onehot = jnp.zeros((TB, LANES), F32)
    for k in range(TOP_K):
        onehot = onehot + jnp.where(lane_f == idx[k], 1.0, 0.0)
    before = jnp.dot(tri_ref[...], onehot.astype(BF16), preferred_element_type=F32) + carry_ref[0:1, :]
    ri = jnp.zeros((TB, LANES), F32)
    rw = jnp.zeros((TB, LANES), F32)
    for k in range(TOP_K):
        rank_k = jnp.sum(jnp.where(lane_f == idx[k], before, 0.0), axis=-1, keepdims=True)
        ri = ri + jnp.where(lane == k, idx[k], 0.0) + jnp.where(lane == TOP_K + k, rank_k, 0.0)
        rw = rw + jnp.where(lane == k, ex[k] / den, 0.0)
    tab_ref[...] = ri.T[0:2 * TOP_K, :].astype(jnp.int32)
    rw_ref[...] = rw
    total = carry_ref[0:1, :] + jnp.sum(onehot, axis=0, keepdims=True)
    carry_ref[...] = jnp.broadcast_to(total, carry_ref.shape)
    cnt_ref[...] = jnp.broadcast_to(total, cnt_ref.shape).astype(jnp.int32)


def _post_call(yc, yl, xc, xl, mods, w_out, g_post, g_ffn, w_r, b_r, tri):
    nb = N_TOK // TB

    def tok(width):
        return pl.BlockSpec((TB, width), lambda i: (i, 0))

    def full(shape):
        return pl.BlockSpec(shape, lambda i: (0,) * len(shape))

    return pl.pallas_call(
        _post_kernel,
        grid=(nb,),
        in_specs=[_ctx_spec(D), _lat_spec(D), _ctx_spec(D), _lat_spec(D),
                  pl.BlockSpec((None, 6, D), lambda i: (_mod_index(i), 0, 0)),
                  full((D, D)), full((1, D)), full((1, D)), full((D, LANES)), full((1, LANES)),
                  full((TB, TB))],
        out_specs=(tok(D), pl.BlockSpec((TB * SUB, LANES), lambda i: (i, 0)),
                   pl.BlockSpec((None, 2 * TOP_K, TB), lambda i: (i, 0, 0)),
                   tok(LANES), full((8, LANES))),
        out_shape=(jax.ShapeDtypeStruct((N_TOK, D), F32),
                   jax.ShapeDtypeStruct((N_TOK * SUB, LANES), F32),
                   jax.ShapeDtypeStruct((nb, 2 * TOP_K, TB), jnp.int32),
                   jax.ShapeDtypeStruct((N_TOK, LANES), F32),
                   jax.ShapeDtypeStruct((8, LANES), jnp.int32)),
        scratch_shapes=[pltpu.VMEM((8, LANES), F32)],
        compiler_params=pltpu.CompilerParams(dimension_semantics=("arbitrary",),
                                             vmem_limit_bytes=VMEM_LIMIT),
        name="post_mix_router",
    )(yc, yl, xc, xl, mods, w_out, g_post, g_ffn, w_r, b_r, tri)


def _dest_kernel(bs_ref, tab_ref, o_ref):
    idx = tab_ref[:, 0:TOP_K, :]
    base = jnp.zeros(idx.shape, jnp.int32)
    for e in range(N_EXPERTS):
        base = jnp.where(idx == e, bs_ref[e] * RB, base)
    o_ref[...] = base + tab_ref[:, TOP_K:2 * TOP_K, :]


def _dest_call(blk_start, tab):
    nb = N_TOK // TB
    return pl.pallas_call(
        _dest_kernel,
        in_specs=[pl.BlockSpec(memory_space=pltpu.SMEM), pl.BlockSpec(memory_space=pltpu.VMEM)],
        out_specs=pl.BlockSpec(memory_space=pltpu.VMEM),
        out_shape=jax.ShapeDtypeStruct((nb, TOP_K, TB), jnp.int32),
        name="dest_rows",
    )(blk_start, tab)


def _dispatch_kernel(zrow_ref, nu_ref, dest_ref, h2_ref, xs_ref, zero_ref, sem, zsem):
    i = pl.program_id(0)

    def zero_fill(row):
        return pltpu.make_async_copy(zero_ref, _tile_rows(xs_ref, row, RB), zsem)

    @pl.when(i == 0)
    def _():
        zero_ref[...] = jnp.zeros(zero_ref.shape, F32)
        for e in range(N_EXPERTS):
            zero_fill(zrow_ref[e]).start()
        for e in range(N_EXPERTS):
            zero_fill(zrow_ref[e]).wait()

        def tail_start(b, carry):
            zero_fill(b * RB).start()
            return carry

        def tail_wait(b, carry):
            zero_fill(b * RB).wait()
            return carry

        lax.fori_loop(nu_ref[0], N_BLOCKS, tail_start, 0)
        lax.fori_loop(nu_ref[0], N_BLOCKS, tail_wait, 0)

    def body(t, carry):
        for k in range(TOP_K):
            pltpu.make_async_copy(_tile_rows(h2_ref, t), _tile_rows(xs_ref, dest_ref[0, k * TB + t]),
                                  sem).start(priority=k % 2)
        return carry

    lax.fori_loop(0, TB, body, 0, unroll=8)
    for k in range(TOP_K):
        pltpu.make_async_copy(h2_ref, _tile_rows(xs_ref, 0, TB), sem).wait()


def _dispatch_call(zrow, n_used, dest, h2):
    nb = N_TOK // TB
    grid_spec = pltpu.PrefetchScalarGridSpec(
        num_scalar_prefetch=2,
        grid=(nb,),
        in_specs=[
            pl.BlockSpec((None, 1, TOP_K * TB), lambda i, z, n: (i, 0, 0), memory_space=pltpu.SMEM),
            pl.BlockSpec((TB * SUB, LANES), lambda i, z, n: (i, 0)),
        ],
        out_specs=pl.BlockSpec(memory_space=pl.ANY),
        scratch_shapes=[pltpu.VMEM((RB * SUB, LANES), F32), pltpu.SemaphoreType.DMA,
                        pltpu.SemaphoreType.DMA],
    )
    return pl.pallas_call(
        _dispatch_kernel,
        grid_spec=grid_spec,
        out_shape=jax.ShapeDtypeStruct((N_ROWS * SUB, LANES), F32),
        compiler_params=pltpu.CompilerParams(dimension_semantics=("arbitrary",),
                                             vmem_limit_bytes=VMEM_LIMIT),
        name="dispatch_rows",
    )(zrow, n_used, dest, h2)


def _moe_kernel(be_ref, first_ref, nxt_ref, nxt2_ref, slot_ref, nu_ref, xs_ref, wgu_hbm, bgu_ref, wdn_hbm,
                bdn_ref, o_ref, wgu_f, wdn_f, sems, *, layer):
    b = pl.program_id(0)
    used = b < nu_ref[0]

    def fetch(e, s):
        return (pltpu.make_async_copy(wgu_hbm.at[layer, e], wgu_f.at[s], sems.at[0, s]),
                pltpu.make_async_copy(wdn_hbm.at[layer, e], wdn_f.at[s], sems.at[1, s]))

    @pl.when(b == 0)
    def _():
        for cp in fetch(be_ref[0], 0):
            cp.start(priority=1)

        @pl.when(nxt_ref[0] >= 0)
        def _():
            for cp in fetch(nxt_ref[0], 1):
                cp.start(priority=1)

    @pl.when(first_ref[b] == 1)
    def _():
        s = slot_ref[b]
        for cp in fetch(be_ref[b], s):
            cp.wait()

        @pl.when(nxt2_ref[b] >= 0)
        def _():
            for cp in fetch(nxt2_ref[b], (s + 2) % W_SLOTS):
                cp.start(priority=1)

    @pl.when(used)
    def _():
        s = slot_ref[b]
        x = _load_tiled(xs_ref, RB).astype(BF16)
        gu = jnp.dot(x, wgu_f[s].astype(BF16), preferred_element_type=F32) + bgu_ref[...]
        g = jnp.minimum(gu[:, :D_FF], SWIGLU_LIMIT)
        l = jnp.clip(gu[:, D_FF:], -SWIGLU_LIMIT, SWIGLU_LIMIT)
        a = g * jax.nn.sigmoid(SWIGLU_ALPHA * g) * (l + 1.0)
        y = jnp.dot(a.astype(BF16), wdn_f[s].astype(BF16), preferred_element_type=F32) + bdn_ref[...]
        _store_tiled(o_ref, y)

    @pl.when(jnp.logical_not(used))
    def _():
        o_ref[...] = jnp.zeros(o_ref.shape, o_ref.dtype)


def _moe_call(layer, tables, xs, w_gu, b_gu, w_dn, b_dn):
    def rows_in(b, be, fi, nx, n2, sl, nu):
        return (jnp.minimum(b, nu[0] - 1), 0)

    def rows_out(b, be, fi, nx, n2, sl, nu):
        return (b, 0)

    def expert(b, be, fi, nx, n2, sl, nu):
        return (layer, be[b], 0, 0)

    grid_spec = pltpu.PrefetchScalarGridSpec(
        num_scalar_prefetch=6,
        grid=(N_BLOCKS,),
        in_specs=[
            pl.BlockSpec((RB * SUB, LANES), rows_in),
            pl.BlockSpec(memory_space=pl.ANY),
            pl.BlockSpec((None, None, 1, 2 * D_FF), expert),
            pl.BlockSpec(memory_space=pl.ANY),
            pl.BlockSpec((None, None, 1, D), expert),
        ],
        out_specs=pl.BlockSpec((RB * SUB, LANES), rows_out),
        scratch_shapes=[pltpu.VMEM((W_SLOTS, D, 2 * D_FF), F32), pltpu.VMEM((W_SLOTS, D_FF, D), F32),
                        pltpu.SemaphoreType.DMA((2, W_SLOTS))],
    )
    return pl.pallas_call(
        functools.partial(_moe_kernel, layer=layer),
        grid_spec=grid_spec,
        out_shape=jax.ShapeDtypeStruct((N_ROWS * SUB, LANES), F32),
        compiler_params=pltpu.CompilerParams(dimension_semantics=("arbitrary",),
                                             vmem_limit_bytes=VMEM_LIMIT),
        name="moe_experts",
    )(*tables, xs, w_gu, b_gu, w_dn, b_dn)


def _final_kernel(dcur_ref, dnxt_ref, x1_ref, ys_ref, rw_ref, mod_ref, g_ref, oc_ref, ol_ref, buf, sems):
    i = pl.program_id(0)
    nb = pl.num_programs(0)
    slot = i % 2

    def gather(dest_ref, s):
        def body(t, carry):
            for k in range(TOP_K):
                pltpu.make_async_copy(_tile_rows(ys_ref, dest_ref[0, k * TB + t]),
                                      _tile_rows(buf.at[s], k * TB + t), sems.at[s]).start(priority=k % 2)
            return carry

        lax.fori_loop(0, TB, body, 0, unroll=8)

    @pl.when(i == 0)
    def _():
        gather(dcur_ref, 0)

    @pl.when(i + 1 < nb)
    def _():
        gather(dnxt_ref, 1 - slot)

    pltpu.make_async_copy(_tile_rows(ys_ref, 0, TOP_K * TB), buf.at[slot], sems.at[slot]).wait()

    rw = rw_ref[...]
    y = jnp.zeros((TB, D), F32)
    for k in range(TOP_K):
        yk = jnp.concatenate([buf[slot, pl.ds(k * TB * SUB + s, TB, stride=SUB), :] for s in range(SUB)],
                             axis=1)
        y = y + yk * rw[:, k:k + 1]
    x2 = x1_ref[...] + mod_ref[5:6, :] * _rms(y, g_ref[...])

    @pl.when(i < N_CTX // TB)
    def _():
        oc_ref[...] = x2

    @pl.when(i >= N_CTX // TB)
    def _():
        ol_ref[...] = x2


def _final_call(dest_flat, x1, ys, rw, mods, g_post_ffn):
    nb = N_TOK // TB
    return pl.pallas_call(
        _final_kernel,
        grid=(nb,),
        in_specs=[pl.BlockSpec((None, 1, TOP_K * TB), lambda i: (i, 0, 0), memory_space=pltpu.SMEM),
                  pl.BlockSpec((None, 1, TOP_K * TB), lambda i: (jnp.minimum(i + 1, nb - 1), 0, 0),
                               memory_space=pltpu.SMEM),
                  pl.BlockSpec((TB, D), lambda i: (i, 0)),
                  pl.BlockSpec(memory_space=pl.ANY),
                  pl.BlockSpec((TB, LANES), lambda i: (i, 0)),
                  pl.BlockSpec((None, 6, D), lambda i: (_mod_index(i), 0, 0)),
                  pl.BlockSpec((1, D), lambda i: (0, 0))],
        out_specs=(_ctx_spec(D), _lat_spec(D)),
        out_shape=(jax.ShapeDtypeStruct((N_CTX, D), F32), jax.ShapeDtypeStruct((N_LAT, D), F32)),
        scratch_shapes=[pltpu.VMEM((2, TOP_K * TB * SUB, LANES), F32), pltpu.SemaphoreType.DMA((2,))],
        compiler_params=pltpu.CompilerParams(dimension_semantics=("arbitrary",),
                                             vmem_limit_bytes=VMEM_LIMIT),
        name="combine_residual",
    )(dest_flat, dest_flat, x1, ys, rw, mods, g_post_ffn)


def _dft_tables(t):
    j = np.arange(t, dtype=np.int64)
    ang = 2.0 * np.pi * ((j[:, None] * j[None, :]) % t) / t
    return (np.cos(ang) / math.sqrt(t)).astype(np.float32), (np.sin(ang) / math.sqrt(t)).astype(np.float32)


def _channel_dft():
    c = np.arange(F_GROUP_DIM, dtype=np.int64)
    ang = 2.0 * np.pi * ((c[:, None] * c[None, :]) % F_GROUP_DIM) / F_GROUP_DIM
    eye = np.eye(F_GROUPS)
    bdc = np.kron(eye, np.cos(ang)) / math.sqrt(F_GROUP_DIM)
    bds = np.kron(eye, np.sin(ang)) / math.sqrt(F_GROUP_DIM)
    return bdc.astype(np.float32), bds.astype(np.float32)


def _rope_tables():
    pos = np.arange(LAT_T)
    n = ROPE // 4
    inv_freq = np.power(np.float32(ROPE_BASE), -np.arange(n, dtype=np.float32) / np.float32(n))
    ang_r = (pos // GRID_W).astype(np.float32)[:, None] * inv_freq
    ang_c = (pos % GRID_W).astype(np.float32)[:, None] * inv_freq
    cos = np.concatenate([np.cos(ang_r), np.cos(ang_r), np.cos(ang_c), np.cos(ang_c),
                          np.ones((LAT_T, LANES - ROPE))], axis=1)
    sin = np.concatenate([-np.sin(ang_r), np.sin(ang_r), -np.sin(ang_c), np.sin(ang_c),
                          np.zeros((LAT_T, LANES - ROPE))], axis=1)
    return cos.astype(np.float32), sin.astype(np.float32)


def kernel(x_prompt, x_sample, cache_ckv, cache_krope, c, c_ctx, w_ada, b_ada, g_pre_mix, g_post_mix, g_pre_ffn, g_post_ffn, w_in, g_sgu, w_spatial, b_spatial, g_q, w_uq, g_kv, w_ukv, w_out, w_router, b_router, w_gate_up, b_gate_up, w_down, b_down):
    xc, xl = x_prompt.reshape(N_CTX, D), x_sample.reshape(N_LAT, D)

    cond = jnp.concatenate([c_ctx[None, :], c, jnp.zeros((5, D), F32)], axis=0)
    mods = _mod_call(cond.T, w_ada, b_ada.reshape(DEPTH, 1, 6 * D))
    mods = mods[:, :3].reshape(DEPTH, 3, 6, D)

    gmat = jnp.asarray(np.kron(np.eye(A_HEADS), np.full((A_HEAD_DIM, A_HEAD_DIM), 1.0 / A_HEAD_DIM)),
                       dtype=BF16)
    bdc_np, bds_np = _channel_dft()
    bdc, bds = jnp.asarray(bdc_np).astype(BF16), jnp.asarray(bds_np).astype(BF16)
    dft = {}
    for t in (CTX_T, LAT_T):
        ct_np, st_np = _dft_tables(t)
        dft[t] = (jnp.asarray(ct_np).astype(BF16), jnp.asarray(st_np).astype(BF16))
    cos_np, sin_np = _rope_tables()
    cos_t, sin_t = jnp.asarray(cos_np), jnp.asarray(sin_np)
    tri = jnp.asarray(np.tril(np.ones((TB, TB), np.float32), k=-1)).astype(BF16)

    b_gu = b_gate_up.reshape(DEPTH, N_EXPERTS, 1, 2 * D_FF)
    b_dn = b_down.reshape(DEPTH, N_EXPERTS, 1, D)

    ckv_layers, krope_layers = [], []
    for i in range(DEPTH):
        w_in_p = jnp.pad(w_in[i], ((0, 0), (0, IN_PAD - w_in.shape[-1]))).astype(BF16)
        w_uq_p = jnp.pad(w_uq[i].reshape(Q_LORA, HEADS, NOPE + ROPE),
                         ((0, 0), (0, 0), (0, HEAD_PAD - NOPE - ROPE))).reshape(Q_LORA, HEADS * HEAD_PAD)
        w_uq_p = w_uq_p.astype(BF16)
        w_ukv_b = w_ukv[i].astype(BF16)
        pre = _pre_call(xc, xl, mods[i], g_pre_mix[i][None, :], w_in_p, g_sgu[i][None, :], gmat,
                        g_q[i][None, :], w_uq_p, g_kv[i][None, :], w_ukv_b, cos_t, sin_t)
        u, vn, zf, q, kv, kr, ckv, zkr = pre
        ckv_layers.append(ckv[:N_CTX].reshape(N_CTX_B, CTX_T, KV_LORA))
        krope_layers.append(zkr[:N_CTX, :ROPE].reshape(N_CTX_B, CTX_T, ROPE))

        ws = w_spatial[i].astype(BF16)
        bs_full = jnp.repeat(b_spatial[i].T, A_HEAD_DIM, axis=1)
        mix_in = (u, vn, zf, q, kv, kr)
        y_ctx = _mix_call(mix_in, (ws, bs_full, bdc, bds) + dft[CTX_T], CTX_T, N_CTX_B, 0)
        ckr_p = jnp.pad(cache_krope[:, i], ((0, 0), (0, 0), (0, LANES - ROPE))).astype(BF16)
        y_lat = _mix_call(mix_in, (ws, bs_full, bdc, bds) + dft[LAT_T], LAT_T, N_LAT_B, N_CTX,
                          cache=(cache_ckv[:, i], ckr_p, w_ukv_b))

        w_r = jnp.pad(w_router[i], ((0, 0), (0, LANES - N_EXPERTS)))
        b_r = jnp.pad(b_router[i], (0, LANES - N_EXPERTS), constant_values=NEG)[None, :]
        x1, h2, tab, rw, cnt = _post_call(y_ctx, y_lat, xc, xl, mods[i], w_out[i].astype(BF16),
                                          g_post_mix[i][None, :], g_pre_ffn[i][None, :], w_r, b_r, tri)

        counts = cnt[0, :N_EXPERTS]
        nblk = (counts + RB - 1) // RB
        blk_end = jnp.cumsum(nblk)
        blk_start = blk_end - nblk
        n_used = blk_end[-1:]
        zrow = blk_start * RB + counts // 8 * 8
        dest = _dest_call(blk_start, tab)
        dest_flat = dest.reshape(N_TOK // TB, 1, TOP_K * TB)
        xs = _dispatch_call(zrow, n_used, dest_flat, h2)
        blk_id = jnp.arange(N_BLOCKS, dtype=jnp.int32)
        blk = jnp.minimum(blk_id, n_used - 1)
        block_e = jnp.sum((blk_end[None, :] <= blk[:, None]).astype(jnp.int32), axis=1)
        block_e = jnp.minimum(block_e, N_EXPERTS - 1)
        eid = jnp.arange(N_EXPERTS, dtype=jnp.int32)
        owns = nblk > 0
        later = jnp.where((eid[None, :] > eid[:, None]) & owns[None, :], eid[None, :], N_EXPERTS)
        nxt_e = jnp.min(later, axis=1)
        nxt_e = jnp.where(nxt_e == N_EXPERTS, -1, nxt_e)
        run_id = jnp.cumsum(owns.astype(jnp.int32)) - 1
        first = ((blk_id == blk_start[block_e]) & (blk_id < n_used)).astype(jnp.int32)
        nxt2_e = jnp.where(nxt_e >= 0, nxt_e[jnp.maximum(nxt_e, 0)], -1)
        tables = (block_e, first, nxt_e[block_e], nxt2_e[block_e], run_id[block_e] % W_SLOTS, n_used)
        ys = _moe_call(i, tables, xs, w_gate_up, b_gu, w_down, b_dn)
        xc, xl = _final_call(dest_flat, x1, ys, rw, mods[i], g_post_ffn[i][None, :])

    y_prompt = xc.reshape(N_CTX_B, CTX_T, D)
    y_sample = xl.reshape(N_LAT_B, LAT_T, D)
    return (y_prompt, y_sample, jnp.stack(ckv_layers, axis=1), jnp.stack(krope_layers, axis=1))
```

```python
import functools
import math

import jax
import jax.numpy as jnp
import numpy as np
from jax import lax
from jax.experimental import pallas as pl
from jax.experimental.pallas import tpu as pltpu

F32 = jnp.float32
BF16 = jnp.bfloat16

D = 1024
N_CTX_B, CTX_T = 16, 256
N_LAT_B, LAT_T = 2, 1024
PAST = 512
N_CTX = N_CTX_B * CTX_T
N_LAT = N_LAT_B * LAT_T
N_TOK = N_CTX + N_LAT
DEPTH = 2
GRID_W = 64
EPS = 1e-6
A_HEADS, A_HEAD_DIM, A_WIDTH, CHUNK = 4, 64, 256, 128
F_GROUPS, F_GROUP_DIM, F_WIDTH = 4, 64, 256
HEADS, Q_LORA, KV_LORA, NOPE, ROPE, V_DIM = 4, 256, 128, 128, 64, 128
HEAD_PAD = 256
IN_PAD = 1280
N_EXPERTS, TOP_K, D_FF = 32, 4, 1024
SWIGLU_LIMIT, SWIGLU_ALPHA = 7.0, 1.702
ROPE_BASE = 10000.0

TB = 512
QB = 256
RB = 256
W_SLOTS = 3
LANES = 128
SUB = D // LANES
N_BLOCKS = N_TOK * TOP_K // RB + N_EXPERTS
N_ROWS = N_BLOCKS * RB
NEG = -3.0e38
VMEM_LIMIT = 56 * 1024 * 1024


def _rms(x, g):
    return x * lax.rsqrt(jnp.mean(x * x, axis=-1, keepdims=True) + EPS) * g


def _split_dot(v, m):
    hi = v.astype(BF16)
    lo = (v - hi.astype(F32)).astype(BF16)
    return (jnp.dot(hi, m, preferred_element_type=F32)
            + jnp.dot(lo, m, preferred_element_type=F32))


def _dot_nt(a, b):
    return lax.dot_general(a, b, (((1,), (1,)), ((), ())), preferred_element_type=F32)


def _store_tiled(ref, x):
    rows = x.shape[0]
    for s in range(SUB):
        ref[pl.ds(s, rows, stride=SUB), :] = x[:, s * LANES:(s + 1) * LANES]


def _load_tiled(ref, rows):
    return jnp.concatenate([ref[pl.ds(s, rows, stride=SUB), :] for s in range(SUB)], axis=1)


def _tile_rows(ref, row, n_rows=1):
    return ref.at[pl.ds(pl.multiple_of(row * SUB, SUB), n_rows * SUB)]


def _mod_kernel(ct_ref, w_ref, b_ref, o_ref):
    ct = ct_ref[...]
    s = ct * jax.nn.sigmoid(ct)
    w = w_ref[...]
    o_ref[...] = jnp.zeros(o_ref.shape, F32)
    for r in range(3):
        o_ref[r:r + 1, :] = jnp.sum(w * s[:, r:r + 1], axis=0, keepdims=True) + b_ref[...]


def _mod_call(cond_t, w_ada, b_ada):
    cb = 2048
    return pl.pallas_call(
        _mod_kernel,
        grid=(DEPTH, 6 * D // cb),
        in_specs=[
            pl.BlockSpec((D, 8), lambda l, j: (0, 0)),
            pl.BlockSpec((None, D, cb), lambda l, j: (l, 0, j)),
            pl.BlockSpec((None, 1, cb), lambda l, j: (l, 0, j)),
        ],
        out_specs=pl.BlockSpec((None, 8, cb), lambda l, j: (l, 0, j)),
        out_shape=jax.ShapeDtypeStruct((DEPTH, 8, 6 * D), F32),
        compiler_params=pltpu.CompilerParams(dimension_semantics=("parallel", "parallel")),
        name="modulation",
    )(cond_t, w_ada, b_ada)


def _mod_index(i):
    first_lat = N_CTX // TB
    return jnp.where(i < first_lat, 0, 1 + (i - first_lat) // (LAT_T // TB))


def _ctx_spec(width):
    return pl.BlockSpec((TB, width), lambda i: (jnp.minimum(i, N_CTX // TB - 1), 0))


def _lat_spec(width):
    return pl.BlockSpec((TB, width), lambda i: (jnp.maximum(i - N_CTX // TB, 0), 0))


def _pick(i, ctx_ref, lat_ref):
    return jnp.where(i >= N_CTX // TB, lat_ref[...], ctx_ref[...])


def _layer_spec(layer, shape):
    zeros = (0,) * len(shape)
    return pl.BlockSpec((None,) + tuple(shape), lambda *_: (layer,) + zeros)


def _mod_spec(layer):
    return pl.BlockSpec((None, None, 6, D), lambda i: (layer, _mod_index(i), 0, 0))


def _swap_halves(x, lane):
    w = x.shape[-1]
    fwd = pltpu.roll(x, w - 16, 1)
    bwd = pltpu.roll(x, 16, 1)
    return jnp.where((lane & 31) < 16, fwd, bwd)


def _pre_kernel(xc_ref, xl_ref, mod_ref, gpre_ref, win_ref, gsgu_ref, gmat_ref, gq_ref, wuq_ref,
                gkv_ref, wukv_ref, cos_ref, sin_ref,
                u_ref, vn_ref, zf_ref, q_ref, kv_ref, kr_ref, ckv_ref, zkr_ref):
    i = pl.program_id(0)
    is_lat = i >= N_CTX // TB
    x = _pick(i, xc_ref, xl_ref)
    h = _rms(x, gpre_ref[...]) * (1.0 + mod_ref[1:2, :]) + mod_ref[0:1, :]
    z = jnp.dot(h.astype(BF16), win_ref[...], preferred_element_type=F32)

    ga = jax.nn.gelu(z[:, :2 * A_WIDTH])
    u_ref[...] = ga[:, :A_WIDTH]
    v = ga[:, A_WIDTH:]
    gmat = gmat_ref[...]
    dv = v - _split_dot(v, gmat)
    var = _split_dot(dv * dv, gmat)
    vn_ref[...] = (dv * lax.rsqrt(var + EPS) * gsgu_ref[...]).astype(BF16)

    zf_ref[...] = z[:, 512:768].astype(BF16)

    cos = jnp.where(is_lat, cos_ref[...], 1.0)
    sin = jnp.where(is_lat, sin_ref[...], 0.0)
    lane = lax.broadcasted_iota(jnp.int32, (TB, LANES), 1)

    qn = _rms(z[:, 768:1024], gq_ref[...])
    scale = (NOPE + ROPE) ** -0.5
    q = jnp.dot(qn.astype(BF16), wuq_ref[...], preferred_element_type=F32) * scale
    for hd in range(HEADS):
        base = hd * HEAD_PAD
        q_ref[:, base:base + NOPE] = q[:, base:base + NOPE].astype(BF16)
        qr = q[:, base + NOPE:base + HEAD_PAD]
        q_ref[:, base + NOPE:base + HEAD_PAD] = (qr * cos + _swap_halves(qr, lane) * sin).astype(BF16)

    ckv = _rms(z[:, 1024:1152], gkv_ref[...])
    ckv_ref[...] = ckv
    kv_ref[...] = jnp.dot(ckv.astype(BF16), wukv_ref[...], preferred_element_type=F32).astype(BF16)

    zkr = z[:, 1152:1280]
    zkr_ref[...] = zkr
    kr_ref[...] = (zkr * cos + _swap_halves(zkr, lane) * sin).astype(BF16)


def _pre_call(layer, xc, xl, mods, g_pre, w_in, g_sgu, gmat, g_q, w_uq, g_kv, w_ukv, cos_t, sin_t):
    nb = N_TOK // TB
    lay = functools.partial(_layer_spec, layer)
    first_lat = N_CTX // TB
    pos_blocks = LAT_T // TB

    def tok(width):
        return pl.BlockSpec((TB, width), lambda i: (i, 0))

    def full(shape):
        return pl.BlockSpec(shape, lambda i: (0,) * len(shape))

    def rope_map(i):
        return (jnp.where(i >= first_lat, (i - first_lat) % pos_blocks, 0), 0)

    out_shape = (
        jax.ShapeDtypeStruct((N_TOK, A_WIDTH), F32),
        jax.ShapeDtypeStruct((N_TOK, A_WIDTH), BF16),
        jax.ShapeDtypeStruct((N_TOK, F_WIDTH), BF16),
        jax.ShapeDtypeStruct((N_TOK, HEADS * HEAD_PAD), BF16),
        jax.ShapeDtypeStruct((N_TOK, HEADS * (NOPE + V_DIM)), BF16),
        jax.ShapeDtypeStruct((N_TOK, LANES), BF16),
        jax.ShapeDtypeStruct((N_TOK, KV_LORA), F32),
        jax.ShapeDtypeStruct((N_TOK, LANES), F32),
    )
    return pl.pallas_call(
        _pre_kernel,
        grid=(nb,),
        in_specs=[
            _ctx_spec(D), _lat_spec(D), _mod_spec(layer),
            lay((1, D)), lay((D, IN_PAD)), lay((1, A_WIDTH)), full((A_WIDTH, A_WIDTH)),
            lay((1, Q_LORA)), lay((Q_LORA, HEADS * HEAD_PAD)),
            lay((1, KV_LORA)), lay((KV_LORA, HEADS * (NOPE + V_DIM))),
            pl.BlockSpec((TB, LANES), rope_map), pl.BlockSpec((TB, LANES), rope_map),
        ],
        out_specs=(tok(A_WIDTH), tok(A_WIDTH), tok(F_WIDTH), tok(HEADS * HEAD_PAD),
                   tok(HEADS * (NOPE + V_DIM)), tok(LANES), tok(KV_LORA), tok(LANES)),
        out_shape=out_shape,
        compiler_params=pltpu.CompilerParams(dimension_semantics=("parallel",),
                                             vmem_limit_bytes=VMEM_LIMIT),
        name="pre_mix",
    )(xc, xl, mods, g_pre, w_in, g_sgu, gmat, g_q, w_uq, g_kv, w_ukv, cos_t, sin_t)


def _mix_kernel(*refs, has_cache):
    if has_cache:
        (u_ref, vn_ref, zf_ref, q_ref, kv_ref, kr_ref, ws_ref, bs_ref, bdc_ref, bds_ref,
         ct_ref, st_ref, cckv_ref, ckr_ref, wukv_ref, o_ref) = refs
    else:
        (u_ref, vn_ref, zf_ref, q_ref, kv_ref, kr_ref, ws_ref, bs_ref, bdc_ref, bds_ref,
         ct_ref, st_ref, o_ref) = refs

    lane = lax.broadcasted_iota(jnp.int32, (CHUNK, A_WIDTH), 1)
    for c in range(QB // CHUNK):
        rows = slice(c * CHUNK, (c + 1) * CHUNK)
        vch = vn_ref[rows, :]
        s = bs_ref[...]
        for g in range(A_HEADS):
            sg = jnp.dot(ws_ref[g], vch, preferred_element_type=F32)
            in_head = (lane >= g * A_HEAD_DIM) & (lane < (g + 1) * A_HEAD_DIM)
            s = s + jnp.where(in_head, sg, 0.0)
        o_ref[rows, 0:A_WIDTH] = (u_ref[rows, :] * s).astype(BF16)

    zf = zf_ref[...]
    zc = jnp.dot(zf, bdc_ref[...], preferred_element_type=F32).astype(BF16)
    zs = jnp.dot(zf, bds_ref[...], preferred_element_type=F32).astype(BF16)
    yf = (jnp.dot(ct_ref[...], zc, preferred_element_type=F32)
          - jnp.dot(st_ref[...], zs, preferred_element_type=F32))
    o_ref[:, A_WIDTH:A_WIDTH + F_WIDTH] = yf.astype(BF16)

    kr = kr_ref[...]
    if has_cache:
        kvc = jnp.dot(cckv_ref[...].astype(BF16), wukv_ref[...],
                      preferred_element_type=F32).astype(BF16)
        krc = ckr_ref[...]
    for hd in range(HEADS):
        qh = q_ref[:, hd * HEAD_PAD:(hd + 1) * HEAD_PAD]
        kb = hd * (NOPE + V_DIM)
        kh = jnp.concatenate([kv_ref[:, kb:kb + NOPE], kr], axis=1)
        vh = kv_ref[:, kb + NOPE:kb + NOPE + V_DIM]
        s = _dot_nt(qh, kh)
        m = jnp.max(s, axis=-1, keepdims=True)
        if has_cache:
            khc = jnp.concatenate([kvc[:, kb:kb + NOPE], krc], axis=1)
            vhc = kvc[:, kb + NOPE:kb + NOPE + V_DIM]
            sc = _dot_nt(qh, khc)
            m = jnp.maximum(m, jnp.max(sc, axis=-1, keepdims=True))
        e = jnp.exp(s - m)
        den = jnp.sum(e, axis=-1, keepdims=True)
        o = jnp.dot(e.astype(BF16), vh, preferred_element_type=F32)
        if has_cache:
            ec = jnp.exp(sc - m)
            den = den + jnp.sum(ec, axis=-1, keepdims=True)
            o = o + jnp.dot(ec.astype(BF16), vhc, preferred_element_type=F32)
        ob = A_WIDTH + F_WIDTH + hd * V_DIM
        o_ref[:, ob:ob + V_DIM] = (o * (1.0 / den)).astype(BF16)


def _mix_call(layer, pre, consts, seq_t, n_batch, tok_off, cache=None):
    u, vn, zf, q, kv, kr = pre
    ws, bs_full, bdc, bds, ct, st = consts
    lay = functools.partial(_layer_spec, layer)
    nq = seq_t // QB
    qoff = tok_off // QB
    soff = tok_off // seq_t

    def qrow(width):
        return pl.BlockSpec((QB, width), lambda b, j: (qoff + b * nq + j, 0))

    def srow(width):
        return pl.BlockSpec((seq_t, width), lambda b, j: (soff + b, 0))

    def full(shape):
        return pl.BlockSpec(shape, lambda b, j: (0,) * len(shape))

    in_specs = [qrow(A_WIDTH), qrow(A_WIDTH), srow(F_WIDTH), qrow(HEADS * HEAD_PAD),
                srow(HEADS * (NOPE + V_DIM)), srow(LANES),
                lay((A_HEADS, CHUNK, CHUNK)), lay((CHUNK, A_WIDTH)),
                full((F_WIDTH, F_WIDTH)), full((F_WIDTH, F_WIDTH)),
                pl.BlockSpec((QB, seq_t), lambda b, j: (j, 0)),
                pl.BlockSpec((QB, seq_t), lambda b, j: (j, 0))]
    args = [u, vn, zf, q, kv, kr, ws, bs_full, bdc, bds, ct, st]
    if cache is not None:
        cckv, ckr, wukv = cache
        in_specs += [pl.BlockSpec((None, None, PAST, KV_LORA), lambda b, j: (b, layer, 0, 0)),
                     pl.BlockSpec((None, None, PAST, LANES), lambda b, j: (b, layer, 0, 0)),
                     lay((KV_LORA, HEADS * (NOPE + V_DIM)))]
        args += [cckv, ckr, wukv]
    return pl.pallas_call(
        functools.partial(_mix_kernel, has_cache=cache is not None),
        grid=(n_batch, nq),
        in_specs=in_specs,
        out_specs=pl.BlockSpec((QB, D), lambda b, j: (b * nq + j, 0)),
        out_shape=jax.ShapeDtypeStruct((n_batch * seq_t, D), BF16),
        compiler_params=pltpu.CompilerParams(dimension_semantics=("parallel", "parallel"),
                                             vmem_limit_bytes=VMEM_LIMIT),
        name="mix_lat" if cache is not None else "mix_ctx",
    )(*args)


def _post_kernel(yc_ref, yl_ref, xc_ref, xl_ref, mod_ref, wout_ref, gpost_ref, gffn_ref, wr_ref, br_ref,
                 tri_ref, x1_ref, h2_ref, tab_ref, rw_ref, cnt_ref, carry_ref):
    i = pl.program_id(0)

    @pl.when(i == 0)
    def _():
        carry_ref[...] = jnp.zeros(carry_ref.shape, F32)

    y = jnp.dot(_pick(i, yc_ref, yl_ref), wout_ref[...], preferred_element_type=F32)
    x1 = _pick(i, xc_ref, xl_ref) + mod_ref[2:3, :] * _rms(y, gpost_ref[...])
    x1_ref[...] = x1
    h2 = _rms(x1, gffn_ref[...]) * (1.0 + mod_ref[4:5, :]) + mod_ref[3:4, :]
    _store_tiled(h2_ref, h2)

    wr = wr_ref[...]
    wr_hi = wr.astype(BF16)
    wr_lo = (wr - wr_hi.astype(F32)).astype(BF16)
    h_hi = h2.astype(BF16)
    h_lo = (h2 - h_hi.astype(F32)).astype(BF16)
    logits = (jnp.dot(h_hi, wr_hi, preferred_element_type=F32)
              + jnp.dot(h_lo, wr_hi, preferred_element_type=F32)
              + jnp.dot(h_hi, wr_lo, preferred_element_type=F32)) + br_ref[...]

    lane = lax.broadcasted_iota(jnp.int32, (TB, LANES), 1)
    lane_f = lane.astype(F32)
    work = logits
    idx, val = [], []
    for _ in range(TOP_K):
        m = jnp.max(work, axis=-1, keepdims=True)
        ik = jnp.min(jnp.where(work == m, lane_f, float(LANES)), axis=-1, keepdims=True)
        idx.append(ik)
        val.append(m)
        work = jnp.where(lane_f == ik, NEG, work)
    ex = [jnp.exp(v - val[0]) for v in val]
    den = ex[0] + ex[1] + ex[2] + ex[3]

    onehot = jnp.zeros((TB, LANES), F32)
    for k in range(TOP_K):
        onehot = onehot + jnp.where(lane_f == idx[k], 1.0, 0.0)
    before = jnp.dot(tri_ref[...], onehot.astype(BF16), preferred_element_type=F32) + carry_ref[0:1, :]
    ri = jnp.zeros((TB, LANES), F32)
    rw = jnp.zeros((TB, LANES), F32)
    for k in range(TOP_K):
        rank_k = jnp.sum(jnp.where(lane_f == idx[k], before, 0.0), axis=-1, keepdims=True)
        ri = ri + jnp.where(lane == k, idx[k], 0.0) + jnp.where(lane == TOP_K + k, rank_k, 0.0)
        rw = rw + jnp.where(lane == k, ex[k] / den, 0.0)
    tab_ref[...] = ri.T[0:2 * TOP_K, :].astype(jnp.int32)
    rw_ref[...] = rw
    total = carry_ref[0:1, :] + jnp.sum(onehot, axis=0, keepdims=True)
    carry_ref[...] = jnp.broadcast_to(total, carry_ref.shape)
    cnt_ref[...] = jnp.broadcast_to(total, cnt_ref.shape).astype(jnp.int32)


def _post_call(layer, yc, yl, xc, xl, mods, w_out, g_post, g_ffn, w_r, b_r, tri):
    nb = N_TOK // TB
    lay = functools.partial(_layer_spec, layer)

    def tok(width):
        return pl.BlockSpec((TB, width), lambda i: (i, 0))

    def full(shape):
        return pl.BlockSpec(shape, lambda i: (0,) * len(shape))

    return pl.pallas_call(
        _post_kernel,
        grid=(nb,),
        in_specs=[_ctx_spec(D), _lat_spec(D), _ctx_spec(D), _lat_spec(D), _mod_spec(layer),
                  lay((D, D)), lay((1, D)), lay((1, D)), lay((D, LANES)), lay((1, LANES)),
                  full((TB, TB))],
        out_specs=(tok(D), pl.BlockSpec((TB * SUB, LANES), lambda i: (i, 0)),
                   pl.BlockSpec((None, 2 * TOP_K, TB), lambda i: (i, 0, 0)),
                   tok(LANES), full((8, LANES))),
        out_shape=(jax.ShapeDtypeStruct((N_TOK, D), F32),
                   jax.ShapeDtypeStruct((N_TOK * SUB, LANES), F32),
                   jax.ShapeDtypeStruct((nb, 2 * TOP_K, TB), jnp.int32),
                   jax.ShapeDtypeStruct((N_TOK, LANES), F32),
                   jax.ShapeDtypeStruct((8, LANES), jnp.int32)),
        scratch_shapes=[pltpu.VMEM((8, LANES), F32)],
        compiler_params=pltpu.CompilerParams(dimension_semantics=("arbitrary",),
                                             vmem_limit_bytes=VMEM_LIMIT),
        name="post_mix_router",
    )(yc, yl, xc, xl, mods, w_out, g_post, g_ffn, w_r, b_r, tri)


def _plan_kernel(cnt_ref, tab_ref, dest_ref, zrow_ref, nu_ref, be_ref, first_ref, nxt_ref, nxt2_ref,
                 slot_ref, start_s, next_s):
    nxt = jnp.int32(-1)
    for e in reversed(range(N_EXPERTS)):
        next_s[e] = nxt
        nxt = jnp.where(cnt_ref[0, e] > 0, jnp.int32(e), nxt)

    start = jnp.int32(0)
    run = jnp.int32(0)
    last = jnp.int32(0)
    for e in range(N_EXPERTS):
        c = cnt_ref[0, e]
        n_blk = lax.shift_right_logical(c + (RB - 1), RB.bit_length() - 1)
        start_s[e] = start
        zrow_ref[e] = start * RB + lax.shift_left(lax.shift_right_logical(c, 3), 3)
        n1 = next_s[e]
        n2 = jnp.where(n1 >= 0, next_s[jnp.maximum(n1, 0)], jnp.int32(-1))
        slot = lax.rem(run, W_SLOTS)

        def fill(j, carry, e=e, n1=n1, n2=n2, slot=slot, start=start):
            be_ref[j] = jnp.int32(e)
            first_ref[j] = (j == start).astype(jnp.int32)
            nxt_ref[j] = n1
            nxt2_ref[j] = n2
            slot_ref[j] = slot
            return carry

        lax.fori_loop(start, start + n_blk, fill, 0)
        owns = n_blk > 0
        last = jnp.where(owns, jnp.int32(e), last)
        run = run + owns.astype(jnp.int32)
        start = start + n_blk
    nu_ref[0] = start

    def tail(j, carry):
        be_ref[j] = last
        first_ref[j] = jnp.int32(0)
        nxt_ref[j] = jnp.int32(-1)
        nxt2_ref[j] = jnp.int32(-1)
        slot_ref[j] = jnp.int32(0)
        return carry

    lax.fori_loop(start, N_BLOCKS, tail, 0)

    idx = tab_ref[:, 0:TOP_K, :]
    base = jnp.zeros(idx.shape, jnp.int32)
    for e in range(N_EXPERTS):
        base = jnp.where(idx == e, start_s[e] * RB, base)
    dest_ref[...] = base + tab_ref[:, TOP_K:2 * TOP_K, :]


def _plan_call(cnt, tab):
    nb = N_TOK // TB
    smem = pl.BlockSpec(memory_space=pltpu.SMEM)
    blocks = jax.ShapeDtypeStruct((N_BLOCKS,), jnp.int32)
    return pl.pallas_call(
        _plan_kernel,
        in_specs=[smem, pl.BlockSpec(memory_space=pltpu.VMEM)],
        out_specs=(pl.BlockSpec(memory_space=pltpu.VMEM), smem, smem, smem, smem, smem, smem, smem),
        out_shape=(jax.ShapeDtypeStruct((nb, TOP_K, TB), jnp.int32),
                   jax.ShapeDtypeStruct((N_EXPERTS,), jnp.int32),
                   jax.ShapeDtypeStruct((1,), jnp.int32),
                   blocks, blocks, blocks, blocks, blocks),
        scratch_shapes=[pltpu.SMEM((N_EXPERTS,), jnp.int32), pltpu.SMEM((N_EXPERTS,), jnp.int32)],
        name="plan_rows",
    )(cnt, tab)


def _dispatch_kernel(zrow_ref, nu_ref, dest_ref, h2_ref, xs_ref, zero_ref, sem, zsem):
    i = pl.program_id(0)

    def zero_fill(row):
        return pltpu.make_async_copy(zero_ref, _tile_rows(xs_ref, row, RB), zsem)

    @pl.when(i == 0)
    def _():
        zero_ref[...] = jnp.zeros(zero_ref.shape, F32)
        for e in range(N_EXPERTS):
            zero_fill(zrow_ref[e]).start()
        for e in range(N_EXPERTS):
            zero_fill(zrow_ref[e]).wait()

        def tail_start(b, carry):
            zero_fill(b * RB).start()
            return carry

        def tail_wait(b, carry):
            zero_fill(b * RB).wait()
            return carry

        lax.fori_loop(nu_ref[0], N_BLOCKS, tail_start, 0)
        lax.fori_loop(nu_ref[0], N_BLOCKS, tail_wait, 0)

    def body(t, carry):
        for k in range(TOP_K):
            pltpu.make_async_copy(_tile_rows(h2_ref, t), _tile_rows(xs_ref, dest_ref[0, k * TB + t]),
                                  sem).start(priority=k % 2)
        return carry

    lax.fori_loop(0, TB, body, 0, unroll=8)
    for k in range(TOP_K):
        pltpu.make_async_copy(h2_ref, _tile_rows(xs_ref, 0, TB), sem).wait()


def _dispatch_call(zrow, n_used, dest, h2):
    nb = N_TOK // TB
    grid_spec = pltpu.PrefetchScalarGridSpec(
        num_scalar_prefetch=2,
        grid=(nb,),
        in_specs=[
            pl.BlockSpec((None, 1, TOP_K * TB), lambda i, z, n: (i, 0, 0), memory_space=pltpu.SMEM),
            pl.BlockSpec((TB * SUB, LANES), lambda i, z, n: (i, 0)),
        ],
        out_specs=pl.BlockSpec(memory_space=pl.ANY),
        scratch_shapes=[pltpu.VMEM((RB * SUB, LANES), F32), pltpu.SemaphoreType.DMA,
                        pltpu.SemaphoreType.DMA],
    )
    return pl.pallas_call(
        _dispatch_kernel,
        grid_spec=grid_spec,
        out_shape=jax.ShapeDtypeStruct((N_ROWS * SUB, LANES), F32),
        compiler_params=pltpu.CompilerParams(dimension_semantics=("arbitrary",),
                                             vmem_limit_bytes=VMEM_LIMIT),
        name="dispatch_rows",
    )(zrow, n_used, dest, h2)


def _moe_kernel(be_ref, first_ref, nxt_ref, nxt2_ref, slot_ref, nu_ref, xs_ref, wgu_hbm, bgu_ref, wdn_hbm,
                bdn_ref, o_ref, wgu_f, wdn_f, sems, *, layer):
    b = pl.program_id(0)
    used = b < nu_ref[0]

    def fetch(e, s):
        return (pltpu.make_async_copy(wgu_hbm.at[layer, e], wgu_f.at[s], sems.at[0, s]),
                pltpu.make_async_copy(wdn_hbm.at[layer, e], wdn_f.at[s], sems.at[1, s]))

    @pl.when(b == 0)
    def _():
        for cp in fetch(be_ref[0], 0):
            cp.start(priority=1)

        @pl.when(nxt_ref[0] >= 0)
        def _():
            for cp in fetch(nxt_ref[0], 1):
                cp.start(priority=1)

    @pl.when(first_ref[b] == 1)
    def _():
        s = slot_ref[b]
        for cp in fetch(be_ref[b], s):
            cp.wait()

        @pl.when(nxt2_ref[b] >= 0)
        def _():
            for cp in fetch(nxt2_ref[b], (s + 2) % W_SLOTS):
                cp.start(priority=1)

    @pl.when(used)
    def _():
        s = slot_ref[b]
        x = _load_tiled(xs_ref, RB).astype(BF16)
        gu = jnp.dot(x, wgu_f[s].astype(BF16), preferred_element_type=F32) + bgu_ref[...]
        g = jnp.minimum(gu[:, :D_FF], SWIGLU_LIMIT)
        l = jnp.clip(gu[:, D_FF:], -SWIGLU_LIMIT, SWIGLU_LIMIT)
        a = g * jax.nn.sigmoid(SWIGLU_ALPHA * g) * (l + 1.0)
        y = jnp.dot(a.astype(BF16), wdn_f[s].astype(BF16), preferred_element_type=F32) + bdn_ref[...]
        _store_tiled(o_ref, y)

    @pl.when(jnp.logical_not(used))
    def _():
        o_ref[...] = jnp.zeros(o_ref.shape, o_ref.dtype)


def _moe_call(layer, tables, xs, w_gu, b_gu, w_dn, b_dn):
    def rows_in(b, be, fi, nx, n2, sl, nu):
        return (jnp.minimum(b, nu[0] - 1), 0)

    def rows_out(b, be, fi, nx, n2, sl, nu):
        return (b, 0)

    def expert(b, be, fi, nx, n2, sl, nu):
        return (layer, be[b], 0, 0)

    grid_spec = pltpu.PrefetchScalarGridSpec(
        num_scalar_prefetch=6,
        grid=(N_BLOCKS,),
        in_specs=[
            pl.BlockSpec((RB * SUB, LANES), rows_in),
            pl.BlockSpec(memory_space=pl.ANY),
            pl.BlockSpec((None, None, 1, 2 * D_FF), expert),
            pl.BlockSpec(memory_space=pl.ANY),
            pl.BlockSpec((None, None, 1, D), expert),
        ],
        out_specs=pl.BlockSpec((RB * SUB, LANES), rows_out),
        scratch_shapes=[pltpu.VMEM((W_SLOTS, D, 2 * D_FF), F32), pltpu.VMEM((W_SLOTS, D_FF, D), F32),
                        pltpu.SemaphoreType.DMA((2, W_SLOTS))],
    )
    return pl.pallas_call(
        functools.partial(_moe_kernel, layer=layer),
        grid_spec=grid_spec,
        out_shape=jax.ShapeDtypeStruct((N_ROWS * SUB, LANES), F32),
        compiler_params=pltpu.CompilerParams(dimension_semantics=("arbitrary",),
                                             vmem_limit_bytes=VMEM_LIMIT),
        name="moe_experts",
    )(*tables, xs, w_gu, b_gu, w_dn, b_dn)


def _final_kernel(dcur_ref, dnxt_ref, x1_ref, ys_ref, rw_ref, mod_ref, g_ref, oc_ref, ol_ref, buf, sems):
    i = pl.program_id(0)
    nb = pl.num_programs(0)
    slot = i % 2

    def gather(dest_ref, s):
        def body(t, carry):
            for k in range(TOP_K):
                pltpu.make_async_copy(_tile_rows(ys_ref, dest_ref[0, k * TB + t]),
                                      _tile_rows(buf.at[s], k * TB + t), sems.at[s]).start(priority=k % 2)
            return carry

        lax.fori_loop(0, TB, body, 0, unroll=8)

    @pl.when(i == 0)
    def _():
        gather(dcur_ref, 0)

    @pl.when(i + 1 < nb)
    def _():
        gather(dnxt_ref, 1 - slot)

    pltpu.make_async_copy(_tile_rows(ys_ref, 0, TOP_K * TB), buf.at[slot], sems.at[slot]).wait()

    rw = rw_ref[...]
    y = jnp.zeros((TB, D), F32)
    for k in range(TOP_K):
        yk = jnp.concatenate([buf[slot, pl.ds(k * TB * SUB + s, TB, stride=SUB), :] for s in range(SUB)],
                             axis=1)
        y = y + yk * rw[:, k:k + 1]
    x2 = x1_ref[...] + mod_ref[5:6, :] * _rms(y, g_ref[...])

    @pl.when(i < N_CTX // TB)
    def _():
        oc_ref[...] = x2

    @pl.when(i >= N_CTX // TB)
    def _():
        ol_ref[...] = x2


def _final_call(layer, dest_flat, x1, ys, rw, mods, g_post_ffn):
    nb = N_TOK // TB
    return pl.pallas_call(
        _final_kernel,
        grid=(nb,),
        in_specs=[pl.BlockSpec((None, 1, TOP_K * TB), lambda i: (i, 0, 0), memory_space=pltpu.SMEM),
                  pl.BlockSpec((None, 1, TOP_K * TB), lambda i: (jnp.minimum(i + 1, nb - 1), 0, 0),
                               memory_space=pltpu.SMEM),
                  pl.BlockSpec((TB, D), lambda i: (i, 0)),
                  pl.BlockSpec(memory_space=pl.ANY),
                  pl.BlockSpec((TB, LANES), lambda i: (i, 0)),
                  _mod_spec(layer), _layer_spec(layer, (1, D))],
        out_specs=(_ctx_spec(D), _lat_spec(D)),
        out_shape=(jax.ShapeDtypeStruct((N_CTX, D), F32), jax.ShapeDtypeStruct((N_LAT, D), F32)),
        scratch_shapes=[pltpu.VMEM((2, TOP_K * TB * SUB, LANES), F32), pltpu.SemaphoreType.DMA((2,))],
        compiler_params=pltpu.CompilerParams(dimension_semantics=("arbitrary",),
                                             vmem_limit_bytes=VMEM_LIMIT),
        name="combine_residual",
    )(dest_flat, dest_flat, x1, ys, rw, mods, g_post_ffn)


def _dft_tables(t):
    j = np.arange(t, dtype=np.int64)
    ang = 2.0 * np.pi * ((j[:, None] * j[None, :]) % t) / t
    return (np.cos(ang) / math.sqrt(t)).astype(np.float32), (np.sin(ang) / math.sqrt(t)).astype(np.float32)


def _channel_dft():
    c = np.arange(F_GROUP_DIM, dtype=np.int64)
    ang = 2.0 * np.pi * ((c[:, None] * c[None, :]) % F_GROUP_DIM) / F_GROUP_DIM
    eye = np.eye(F_GROUPS)
    bdc = np.kron(eye, np.cos(ang)) / math.sqrt(F_GROUP_DIM)
    bds = np.kron(eye, np.sin(ang)) / math.sqrt(F_GROUP_DIM)
    return bdc.astype(np.float32), bds.astype(np.float32)


def _rope_tables():
    pos = np.arange(LAT_T)
    n = ROPE // 4
    inv_freq = np.power(np.float32(ROPE_BASE), -np.arange(n, dtype=np.float32) / np.float32(n))
    ang_r = (pos // GRID_W).astype(np.float32)[:, None] * inv_freq
    ang_c = (pos % GRID_W).astype(np.float32)[:, None] * inv_freq
    cos = np.concatenate([np.cos(ang_r), np.cos(ang_r), np.cos(ang_c), np.cos(ang_c),
                          np.ones((LAT_T, LANES - ROPE))], axis=1)
    sin = np.concatenate([-np.sin(ang_r), np.sin(ang_r), -np.sin(ang_c), np.sin(ang_c),
                          np.zeros((LAT_T, LANES - ROPE))], axis=1)
    return cos.astype(np.float32), sin.astype(np.float32)


def kernel(x_prompt, x_sample, cache_ckv, cache_krope, c, c_ctx, w_ada, b_ada, g_pre_mix, g_post_mix, g_pre_ffn, g_post_ffn, w_in, g_sgu, w_spatial, b_spatial, g_q, w_uq, g_kv, w_ukv, w_out, w_router, b_router, w_gate_up, b_gate_up, w_down, b_down):
    xc, xl = x_prompt.reshape(N_CTX, D), x_sample.reshape(N_LAT, D)

    cond = jnp.concatenate([c_ctx[None, :], c, jnp.zeros((5, D), F32)], axis=0)
    mods = _mod_call(cond.T, w_ada, b_ada.reshape(DEPTH, 1, 6 * D))
    mods = mods[:, :3].reshape(DEPTH, 3, 6, D)

    gmat = jnp.asarray(np.kron(np.eye(A_HEADS), np.full((A_HEAD_DIM, A_HEAD_DIM), 1.0 / A_HEAD_DIM)),
                       dtype=BF16)
    bdc_np, bds_np = _channel_dft()
    bdc, bds = jnp.asarray(bdc_np).astype(BF16), jnp.asarray(bds_np).astype(BF16)
    dft = {}
    for t in (CTX_T, LAT_T):
        ct_np, st_np = _dft_tables(t)
        dft[t] = (jnp.asarray(ct_np).astype(BF16), jnp.asarray(st_np).astype(BF16))
    cos_np, sin_np = _rope_tables()
    cos_t, sin_t = jnp.asarray(cos_np), jnp.asarray(sin_np)
    tri = jnp.asarray(np.tril(np.ones((TB, TB), np.float32), k=-1)).astype(BF16)

    b_gu = b_gate_up.reshape(DEPTH, N_EXPERTS, 1, 2 * D_FF)
    b_dn = b_down.reshape(DEPTH, N_EXPERTS, 1, D)
    w_in_p = jnp.pad(w_in, ((0, 0), (0, 0), (0, IN_PAD - w_in.shape[-1]))).astype(BF16)
    w_uq_p = jnp.pad(w_uq.reshape(DEPTH, Q_LORA, HEADS, NOPE + ROPE),
                     ((0, 0), (0, 0), (0, 0), (0, HEAD_PAD - NOPE - ROPE)))
    w_uq_p = w_uq_p.reshape(DEPTH, Q_LORA, HEADS * HEAD_PAD).astype(BF16)
    w_ukv_b = w_ukv.astype(BF16)
    ws = w_spatial.astype(BF16)
    bs_full = jnp.repeat(jnp.swapaxes(b_spatial, 1, 2), A_HEAD_DIM, axis=2)
    ckr_p = jnp.pad(cache_krope, ((0, 0), (0, 0), (0, 0), (0, LANES - ROPE))).astype(BF16)
    w_out_b = w_out.astype(BF16)
    w_r = jnp.pad(w_router, ((0, 0), (0, 0), (0, LANES - N_EXPERTS)))
    b_r = jnp.pad(b_router, ((0, 0), (0, LANES - N_EXPERTS)), constant_values=NEG)[:, None, :]
    row = lambda g: g[:, None, :]

    ckv_layers, krope_layers = [], []
    for i in range(DEPTH):
        pre = _pre_call(i, xc, xl, mods, row(g_pre_mix), w_in_p, row(g_sgu), gmat,
                        row(g_q), w_uq_p, row(g_kv), w_ukv_b, cos_t, sin_t)
        u, vn, zf, q, kv, kr, ckv, zkr = pre
        ckv_layers.append(ckv[:N_CTX].reshape(N_CTX_B, CTX_T, KV_LORA))
        krope_layers.append(zkr[:N_CTX, :ROPE].reshape(N_CTX_B, CTX_T, ROPE))

        mix_in = (u, vn, zf, q, kv, kr)
        y_ctx = _mix_call(i, mix_in, (ws, bs_full, bdc, bds) + dft[CTX_T], CTX_T, N_CTX_B, 0)
        y_lat = _mix_call(i, mix_in, (ws, bs_full, bdc, bds) + dft[LAT_T], LAT_T, N_LAT_B, N_CTX,
                          cache=(cache_ckv, ckr_p, w_ukv_b))

        x1, h2, tab, rw, cnt = _post_call(i, y_ctx, y_lat, xc, xl, mods, w_out_b,
                                          row(g_post_mix), row(g_pre_ffn), w_r, b_r, tri)

        dest, zrow, n_used, block_e, first, nxt, nxt2, slot = _plan_call(cnt, tab)
        dest_flat = dest.reshape(N_TOK // TB, 1, TOP_K * TB)
        xs = _dispatch_call(zrow, n_used, dest_flat, h2)
        ys = _moe_call(i, (block_e, first, nxt, nxt2, slot, n_used), xs, w_gate_up, b_gu, w_down, b_dn)
        xc, xl = _final_call(i, dest_flat, x1, ys, rw, mods, row(g_post_ffn))

    y_prompt = xc.reshape(N_CTX_B, CTX_T, D)
    y_sample = xl.reshape(N_LAT_B, LAT_T, D)
    return (y_prompt, y_sample, jnp.stack(ckv_layers, axis=1), jnp.stack(krope_layers, axis=1))
```

```python
import functools
import math

import jax
import jax.numpy as jnp
import numpy as np
from jax import lax
from jax.experimental import pallas as pl
from jax.experimental.pallas import tpu as pltpu

F32 = jnp.float32
BF16 = jnp.bfloat16

D = 1024
N_CTX_B, CTX_T = 16, 256
N_LAT_B, LAT_T = 2, 1024
PAST = 512
N_CTX = N_CTX_B * CTX_T
N_LAT = N_LAT_B * LAT_T
N_TOK = N_CTX + N_LAT
DEPTH = 2
GRID_W = 64
EPS = 1e-6
A_HEADS, A_HEAD_DIM, A_WIDTH, CHUNK = 4, 64, 256, 128
F_GROUPS, F_GROUP_DIM, F_WIDTH = 4, 64, 256
HEADS, Q_LORA, KV_LORA, NOPE, ROPE, V_DIM = 4, 256, 128, 128, 64, 128
HEAD_PAD = 256
IN_PAD = 1280
N_EXPERTS, TOP_K, D_FF = 32, 4, 1024
SWIGLU_LIMIT, SWIGLU_ALPHA = 7.0, 1.702
ROPE_BASE = 10000.0

TB = 512
QB = 256
RB = 256
W_SLOTS = 3
LANES = 128
SUB = D // LANES
N_BLOCKS = N_TOK * TOP_K // RB + N_EXPERTS
N_ROWS = N_BLOCKS * RB
NEG = -3.0e38
VMEM_LIMIT = 56 * 1024 * 1024


def _rms(x, g):
    return x * lax.rsqrt(jnp.mean(x * x, axis=-1, keepdims=True) + EPS) * g


def _split_dot(v, m):
    hi = v.astype(BF16)
    lo = (v - hi.astype(F32)).astype(BF16)
    return (jnp.dot(hi, m, preferred_element_type=F32)
            + jnp.dot(lo, m, preferred_element_type=F32))


def _dot_nt(a, b):
    return lax.dot_general(a, b, (((1,), (1,)), ((), ())), preferred_element_type=F32)


def _store_tiled(ref, x):
    rows = x.shape[0]
    for s in range(SUB):
        ref[pl.ds(s, rows, stride=SUB), :] = x[:, s * LANES:(s + 1) * LANES]


def _load_tiled(ref, rows):
    return jnp.concatenate([ref[pl.ds(s, rows, stride=SUB), :] for s in range(SUB)], axis=1)


def _tile_rows(ref, row, n_rows=1):
    return ref.at[pl.ds(pl.multiple_of(row * SUB, SUB), n_rows * SUB)]


def _mod_kernel(ct_ref, w_ref, b_ref, o_ref):
    ct = ct_ref[...]
    s = ct * jax.nn.sigmoid(ct)
    w = w_ref[...]
    o_ref[...] = jnp.zeros(o_ref.shape, F32)
    for r in range(3):
        o_ref[r:r + 1, :] = jnp.sum(w * s[:, r:r + 1], axis=0, keepdims=True) + b_ref[...]


def _mod_call(cond_t, w_ada, b_ada):
    cb = 2048
    return pl.pallas_call(
        _mod_kernel,
        grid=(DEPTH, 6 * D // cb),
        in_specs=[
            pl.BlockSpec((D, 8), lambda l, j: (0, 0)),
            pl.BlockSpec((None, D, cb), lambda l, j: (l, 0, j)),
            pl.BlockSpec((None, 1, cb), lambda l, j: (l, 0, j)),
        ],
        out_specs=pl.BlockSpec((None, 8, cb), lambda l, j: (l, 0, j)),
        out_shape=jax.ShapeDtypeStruct((DEPTH, 8, 6 * D), F32),
        compiler_params=pltpu.CompilerParams(dimension_semantics=("parallel", "parallel")),
        name="modulation",
    )(cond_t, w_ada, b_ada)


def _mod_index(i):
    first_lat = N_CTX // TB
    return jnp.where(i < first_lat, 0, 1 + (i - first_lat) // (LAT_T // TB))


def _ctx_spec(width):
    return pl.BlockSpec((TB, width), lambda i: (jnp.minimum(i, N_CTX // TB - 1), 0))


def _lat_spec(width):
    return pl.BlockSpec((TB, width), lambda i: (jnp.maximum(i - N_CTX // TB, 0), 0))


def _pick(i, ctx_ref, lat_ref):
    return jnp.where(i >= N_CTX // TB, lat_ref[...], ctx_ref[...])


def _layer_spec(layer, shape):
    zeros = (0,) * len(shape)
    return pl.BlockSpec((None,) + tuple(shape), lambda *_: (layer,) + zeros)


def _mod_spec(layer):
    return pl.BlockSpec((None, None, 6, D), lambda i: (layer, _mod_index(i), 0, 0))


def _swap_halves(x, lane):
    w = x.shape[-1]
    fwd = pltpu.roll(x, w - 16, 1)
    bwd = pltpu.roll(x, 16, 1)
    return jnp.where((lane & 31) < 16, fwd, bwd)


def _pre_kernel(xc_ref, xl_ref, *refs):
    i = pl.program_id(0)
    _pre_body(i, _pick(i, xc_ref, xl_ref), *refs)


def _pre_body(i, x, mod_ref, gpre_ref, win_ref, gsgu_ref, gmat_ref, gq_ref, wuq_ref,
              gkv_ref, wukv_ref, cos_ref, sin_ref,
              u_ref, vn_ref, zf_ref, q_ref, kv_ref, kr_ref, ckv_ref, zkr_ref):
    is_lat = i >= N_CTX // TB
    h = _rms(x, gpre_ref[...]) * (1.0 + mod_ref[1:2, :]) + mod_ref[0:1, :]
    z = jnp.dot(h.astype(BF16), win_ref[...], preferred_element_type=F32)

    ga = jax.nn.gelu(z[:, :2 * A_WIDTH])
    u_ref[...] = ga[:, :A_WIDTH]
    v = ga[:, A_WIDTH:]
    gmat = gmat_ref[...]
    dv = v - _split_dot(v, gmat)
    var = _split_dot(dv * dv, gmat)
    vn_ref[...] = (dv * lax.rsqrt(var + EPS) * gsgu_ref[...]).astype(BF16)

    zf_ref[...] = z[:, 512:768].astype(BF16)

    cos = jnp.where(is_lat, cos_ref[...], 1.0)
    sin = jnp.where(is_lat, sin_ref[...], 0.0)
    lane = lax.broadcasted_iota(jnp.int32, (TB, LANES), 1)

    qn = _rms(z[:, 768:1024], gq_ref[...])
    scale = (NOPE + ROPE) ** -0.5
    q = jnp.dot(qn.astype(BF16), wuq_ref[...], preferred_element_type=F32) * scale
    for hd in range(HEADS):
        base = hd * HEAD_PAD
        q_ref[:, base:base + NOPE] = q[:, base:base + NOPE].astype(BF16)
        qr = q[:, base + NOPE:base + HEAD_PAD]
        q_ref[:, base + NOPE:base + HEAD_PAD] = (qr * cos + _swap_halves(qr, lane) * sin).astype(BF16)

    ckv = _rms(z[:, 1024:1152], gkv_ref[...])
    ckv_ref[...] = ckv
    kv_ref[...] = jnp.dot(ckv.astype(BF16), wukv_ref[...], preferred_element_type=F32).astype(BF16)

    zkr = z[:, 1152:1280]
    zkr_ref[...] = zkr
    kr_ref[...] = (zkr * cos + _swap_halves(zkr, lane) * sin).astype(BF16)


def _pre_specs(layer):
    lay = functools.partial(_layer_spec, layer)
    first_lat = N_CTX // TB
    pos_blocks = LAT_T // TB

    def tok(width):
        return pl.BlockSpec((TB, width), lambda i: (i, 0))

    def full(shape):
        return pl.BlockSpec(shape, lambda i: (0,) * len(shape))

    def rope_map(i):
        return (jnp.where(i >= first_lat, (i - first_lat) % pos_blocks, 0), 0)

    in_specs = [
        _mod_spec(layer),
        lay((1, D)), lay((D, IN_PAD)), lay((1, A_WIDTH)), full((A_WIDTH, A_WIDTH)),
        lay((1, Q_LORA)), lay((Q_LORA, HEADS * HEAD_PAD)),
        lay((1, KV_LORA)), lay((KV_LORA, HEADS * (NOPE + V_DIM))),
        pl.BlockSpec((TB, LANES), rope_map), pl.BlockSpec((TB, LANES), rope_map),
    ]
    out_specs = (tok(A_WIDTH), tok(A_WIDTH), tok(F_WIDTH), tok(HEADS * HEAD_PAD),
                 tok(HEADS * (NOPE + V_DIM)), tok(LANES), tok(KV_LORA), tok(LANES))
    out_shape = (
        jax.ShapeDtypeStruct((N_TOK, A_WIDTH), F32),
        jax.ShapeDtypeStruct((N_TOK, A_WIDTH), BF16),
        jax.ShapeDtypeStruct((N_TOK, F_WIDTH), BF16),
        jax.ShapeDtypeStruct((N_TOK, HEADS * HEAD_PAD), BF16),
        jax.ShapeDtypeStruct((N_TOK, HEADS * (NOPE + V_DIM)), BF16),
        jax.ShapeDtypeStruct((N_TOK, LANES), BF16),
        jax.ShapeDtypeStruct((N_TOK, KV_LORA), F32),
        jax.ShapeDtypeStruct((N_TOK, LANES), F32),
    )
    return in_specs, out_specs, out_shape


def _pre_call(layer, xc, xl, pre_params):
    in_specs, out_specs, out_shape = _pre_specs(layer)
    return pl.pallas_call(
        _pre_kernel,
        grid=(N_TOK // TB,),
        in_specs=[_ctx_spec(D), _lat_spec(D)] + in_specs,
        out_specs=out_specs,
        out_shape=out_shape,
        compiler_params=pltpu.CompilerParams(dimension_semantics=("parallel",),
                                             vmem_limit_bytes=VMEM_LIMIT),
        name="pre_mix",
    )(xc, xl, *pre_params)


def _mix_kernel(*refs, has_cache):
    if has_cache:
        (u_ref, vn_ref, zf_ref, q_ref, kv_ref, kr_ref, ws_ref, bs_ref, bdc_ref, bds_ref,
         ct_ref, st_ref, cckv_ref, ckr_ref, wukv_ref, o_ref) = refs
    else:
        (u_ref, vn_ref, zf_ref, q_ref, kv_ref, kr_ref, ws_ref, bs_ref, bdc_ref, bds_ref,
         ct_ref, st_ref, o_ref) = refs

    lane = lax.broadcasted_iota(jnp.int32, (CHUNK, A_WIDTH), 1)
    for c in range(QB // CHUNK):
        rows = slice(c * CHUNK, (c + 1) * CHUNK)
        vch = vn_ref[rows, :]
        s = bs_ref[...]
        for g in range(A_HEADS):
            sg = jnp.dot(ws_ref[g], vch, preferred_element_type=F32)
            in_head = (lane >= g * A_HEAD_DIM) & (lane < (g + 1) * A_HEAD_DIM)
            s = s + jnp.where(in_head, sg, 0.0)
        o_ref[rows, 0:A_WIDTH] = (u_ref[rows, :] * s).astype(BF16)

    zf = zf_ref[...]
    zc = jnp.dot(zf, bdc_ref[...], preferred_element_type=F32).astype(BF16)
    zs = jnp.dot(zf, bds_ref[...], preferred_element_type=F32).astype(BF16)
    yf = (jnp.dot(ct_ref[...], zc, preferred_element_type=F32)
          - jnp.dot(st_ref[...], zs, preferred_element_type=F32))
    o_ref[:, A_WIDTH:A_WIDTH + F_WIDTH] = yf.astype(BF16)

    kr = kr_ref[...]
    if has_cache:
        kvc = jnp.dot(cckv_ref[...].astype(BF16), wukv_ref[...],
                      preferred_element_type=F32).astype(BF16)
        krc = ckr_ref[...]
    for hd in range(HEADS):
        qh = q_ref[:, hd * HEAD_PAD:(hd + 1) * HEAD_PAD]
        kb = hd * (NOPE + V_DIM)
        kh = jnp.concatenate([kv_ref[:, kb:kb + NOPE], kr], axis=1)
        vh = kv_ref[:, kb + NOPE:kb + NOPE + V_DIM]
        s = _dot_nt(qh, kh)
        m = jnp.max(s, axis=-1, keepdims=True)
        if has_cache:
            khc = jnp.concatenate([kvc[:, kb:kb + NOPE], krc], axis=1)
            vhc = kvc[:, kb + NOPE:kb + NOPE + V_DIM]
            sc = _dot_nt(qh, khc)
            m = jnp.maximum(m, jnp.max(sc, axis=-1, keepdims=True))
        e = jnp.exp(s - m)
        den = jnp.sum(e, axis=-1, keepdims=True)
        o = jnp.dot(e.astype(BF16), vh, preferred_element_type=F32)
        if has_cache:
            ec = jnp.exp(sc - m)
            den = den + jnp.sum(ec, axis=-1, keepdims=True)
            o = o + jnp.dot(ec.astype(BF16), vhc, preferred_element_type=F32)
        ob = A_WIDTH + F_WIDTH + hd * V_DIM
        o_ref[:, ob:ob + V_DIM] = (o * (1.0 / den)).astype(BF16)


def _mix_call(layer, pre, consts, seq_t, n_batch, tok_off, cache=None):
    u, vn, zf, q, kv, kr = pre
    ws, bs_full, bdc, bds, ct, st = consts
    lay = functools.partial(_layer_spec, layer)
    nq = seq_t // QB
    qoff = tok_off // QB
    soff = tok_off // seq_t

    def qrow(width):
        return pl.BlockSpec((QB, width), lambda b, j: (qoff + b * nq + j, 0))

    def srow(width):
        return pl.BlockSpec((seq_t, width), lambda b, j: (soff + b, 0))

    def full(shape):
        return pl.BlockSpec(shape, lambda b, j: (0,) * len(shape))

    in_specs = [qrow(A_WIDTH), qrow(A_WIDTH), srow(F_WIDTH), qrow(HEADS * HEAD_PAD),
                srow(HEADS * (NOPE + V_DIM)), srow(LANES),
                lay((A_HEADS, CHUNK, CHUNK)), lay((CHUNK, A_WIDTH)),
                full((F_WIDTH, F_WIDTH)), full((F_WIDTH, F_WIDTH)),
                pl.BlockSpec((QB, seq_t), lambda b, j: (j, 0)),
                pl.BlockSpec((QB, seq_t), lambda b, j: (j, 0))]
    args = [u, vn, zf, q, kv, kr, ws, bs_full, bdc, bds, ct, st]
    if cache is not None:
        cckv, ckr, wukv = cache
        in_specs += [pl.BlockSpec((None, None, PAST, KV_LORA), lambda b, j: (b, layer, 0, 0)),
                     pl.BlockSpec((None, None, PAST, LANES), lambda b, j: (b, layer, 0, 0)),
                     lay((KV_LORA, HEADS * (NOPE + V_DIM)))]
        args += [cckv, ckr, wukv]
    return pl.pallas_call(
        functools.partial(_mix_kernel, has_cache=cache is not None),
        grid=(n_batch, nq),
        in_specs=in_specs,
        out_specs=pl.BlockSpec((QB, D), lambda b, j: (b * nq + j, 0)),
        out_shape=jax.ShapeDtypeStruct((n_batch * seq_t, D), BF16),
        compiler_params=pltpu.CompilerParams(dimension_semantics=("parallel", "parallel"),
                                             vmem_limit_bytes=VMEM_LIMIT),
        name="mix_lat" if cache is not None else "mix_ctx",
    )(*args)


def _post_kernel(yc_ref, yl_ref, xc_ref, xl_ref, mod_ref, wout_ref, gpost_ref, gffn_ref, wr_ref, br_ref,
                 tri_ref, x1_ref, h2_ref, tab_ref, rw_ref, cnt_ref, carry_ref):
    i = pl.program_id(0)

    @pl.when(i == 0)
    def _():
        carry_ref[...] = jnp.zeros(carry_ref.shape, F32)

    y = jnp.dot(_pick(i, yc_ref, yl_ref), wout_ref[...], preferred_element_type=F32)
    x1 = _pick(i, xc_ref, xl_ref) + mod_ref[2:3, :] * _rms(y, gpost_ref[...])
    x1_ref[...] = x1
    h2 = _rms(x1, gffn_ref[...]) * (1.0 + mod_ref[4:5, :]) + mod_ref[3:4, :]
    _store_tiled(h2_ref, h2)

    wr = wr_ref[...]
    wr_hi = wr.astype(BF16)
    wr_lo = (wr - wr_hi.astype(F32)).astype(BF16)
    h_hi = h2.astype(BF16)
    h_lo = (h2 - h_hi.astype(F32)).astype(BF16)
    logits = (jnp.dot(h_hi, wr_hi, preferred_element_type=F32)
              + jnp.dot(h_lo, wr_hi, preferred_element_type=F32)
              + jnp.dot(h_hi, wr_lo, preferred_element_type=F32)) + br_ref[...]

    lane = lax.broadcasted_iota(jnp.int32, (TB, LANES), 1)
    lane_f = lane.astype(F32)
    work = logits
    idx, val = [], []
    for _ in range(TOP_K):
        m = jnp.max(work, axis=-1, keepdims=True)
        ik = jnp.min(jnp.where(work == m, lane_f, float(LANES)), axis=-1, keepdims=True)
        idx.append(ik)
        val.append(m)
        work = jnp.where(lane_f == ik, NEG, work)
    ex = [jnp.exp(v - val[0]) for v in val]
    den = ex[0] + ex[1] + ex[2] + ex[3]

    onehot = jnp.zeros((TB, LANES), F32)
    for k in range(TOP_K):
        onehot = onehot + jnp.where(lane_f == idx[k], 1.0, 0.0)
    before = jnp.dot(tri_ref[...], onehot.astype(BF16), preferred_element_type=F32) + carry_ref[0:1, :]
    ri = jnp.zeros((TB, LANES), F32)
    rw = jnp.zeros((TB, LANES), F32)
    for k in range(TOP_K):
        rank_k = jnp.sum(jnp.where(lane_f == idx[k], before, 0.0), axis=-1, keepdims=True)
        ri = ri + jnp.where(lane == k, idx[k], 0.0) + jnp.where(lane == TOP_K + k, rank_k, 0.0)
        rw = rw + jnp.where(lane == k, ex[k] / den, 0.0)
    tab_ref[...] = ri.T[0:2 * TOP_K, :].astype(jnp.int32)
    rw_ref[...] = rw
    total = carry_ref[0:1, :] + jnp.sum(onehot, axis=0, keepdims=True)
    carry_ref[...] = jnp.broadcast_to(total, carry_ref.shape)
    cnt_ref[...] = jnp.broadcast_to(total, cnt_ref.shape).astype(jnp.int32)


def _post_call(layer, yc, yl, xc, xl, mods, w_out, g_post, g_ffn, w_r, b_r, tri):
    nb = N_TOK // TB
    lay = functools.partial(_layer_spec, layer)

    def tok(width):
        return pl.BlockSpec((TB, width), lambda i: (i, 0))

    def full(shape):
        return pl.BlockSpec(shape, lambda i: (0,) * len(shape))

    return pl.pallas_call(
        _post_kernel,
        grid=(nb,),
        in_specs=[_ctx_spec(D), _lat_spec(D), _ctx_spec(D), _lat_spec(D), _mod_spec(layer),
                  lay((D, D)), lay((1, D)), lay((1, D)), lay((D, LANES)), lay((1, LANES)),
                  full((TB, TB))],
        out_specs=(tok(D), pl.BlockSpec((TB * SUB, LANES), lambda i: (i, 0)),
                   pl.BlockSpec((None, 2 * TOP_K, TB), lambda i: (i, 0, 0)),
                   tok(LANES), full((8, LANES))),
        out_shape=(jax.ShapeDtypeStruct((N_TOK, D), F32),
                   jax.ShapeDtypeStruct((N_TOK * SUB, LANES), F32),
                   jax.ShapeDtypeStruct((nb, 2 * TOP_K, TB), jnp.int32),
                   jax.ShapeDtypeStruct((N_TOK, LANES), F32),
                   jax.ShapeDtypeStruct((8, LANES), jnp.int32)),
        scratch_shapes=[pltpu.VMEM((8, LANES), F32)],
        compiler_params=pltpu.CompilerParams(dimension_semantics=("arbitrary",),
                                             vmem_limit_bytes=VMEM_LIMIT),
        name="post_mix_router",
    )(yc, yl, xc, xl, mods, w_out, g_post, g_ffn, w_r, b_r, tri)


def _plan_kernel(cnt_ref, tab_ref, dest_ref, zrow_ref, nu_ref, be_ref, first_ref, nxt_ref, nxt2_ref,
                 slot_ref, start_s, next_s):
    nxt = jnp.int32(-1)
    for e in reversed(range(N_EXPERTS)):
        next_s[e] = nxt
        nxt = jnp.where(cnt_ref[0, e] > 0, jnp.int32(e), nxt)

    start = jnp.int32(0)
    run = jnp.int32(0)
    last = jnp.int32(0)
    for e in range(N_EXPERTS):
        c = cnt_ref[0, e]
        n_blk = lax.shift_right_logical(c + (RB - 1), RB.bit_length() - 1)
        start_s[e] = start
        zrow_ref[e] = start * RB + lax.shift_left(lax.shift_right_logical(c, 3), 3)
        n1 = next_s[e]
        n2 = jnp.where(n1 >= 0, next_s[jnp.maximum(n1, 0)], jnp.int32(-1))
        slot = lax.rem(run, W_SLOTS)

        def fill(j, carry, e=e, n1=n1, n2=n2, slot=slot, start=start):
            be_ref[j] = jnp.int32(e)
            first_ref[j] = (j == start).astype(jnp.int32)
            nxt_ref[j] = n1
            nxt2_ref[j] = n2
            slot_ref[j] = slot
            return carry

        lax.fori_loop(start, start + n_blk, fill, 0)
        owns = n_blk > 0
        last = jnp.where(owns, jnp.int32(e), last)
        run = run + owns.astype(jnp.int32)
        start = start + n_blk
    nu_ref[0] = start

    def tail(j, carry):
        be_ref[j] = last
        first_ref[j] = jnp.int32(0)
        nxt_ref[j] = jnp.int32(-1)
        nxt2_ref[j] = jnp.int32(-1)
        slot_ref[j] = jnp.int32(0)
        return carry

    lax.fori_loop(start, N_BLOCKS, tail, 0)

    idx = tab_ref[:, 0:TOP_K, :]
    base = jnp.zeros(idx.shape, jnp.int32)
    for e in range(N_EXPERTS):
        base = jnp.where(idx == e, start_s[e] * RB, base)
    dest_ref[...] = base + tab_ref[:, TOP_K:2 * TOP_K, :]


def _plan_call(cnt, tab):
    nb = N_TOK // TB
    smem = pl.BlockSpec(memory_space=pltpu.SMEM)
    blocks = jax.ShapeDtypeStruct((N_BLOCKS,), jnp.int32)
    return pl.pallas_call(
        _plan_kernel,
        in_specs=[smem, pl.BlockSpec(memory_space=pltpu.VMEM)],
        out_specs=(pl.BlockSpec(memory_space=pltpu.VMEM), smem, smem, smem, smem, smem, smem, smem),
        out_shape=(jax.ShapeDtypeStruct((nb, TOP_K, TB), jnp.int32),
                   jax.ShapeDtypeStruct((N_EXPERTS,), jnp.int32),
                   jax.ShapeDtypeStruct((1,), jnp.int32),
                   blocks, blocks, blocks, blocks, blocks),
        scratch_shapes=[pltpu.SMEM((N_EXPERTS,), jnp.int32), pltpu.SMEM((N_EXPERTS,), jnp.int32)],
        name="plan_rows",
    )(cnt, tab)


def _dispatch_kernel(zrow_ref, nu_ref, dest_ref, h2_ref, xs_ref, zero_ref, sem, zsem):
    i = pl.program_id(0)

    def zero_fill(row):
        return pltpu.make_async_copy(zero_ref, _tile_rows(xs_ref, row, RB), zsem)

    @pl.when(i == 0)
    def _():
        zero_ref[...] = jnp.zeros(zero_ref.shape, F32)
        for e in range(N_EXPERTS):
            zero_fill(zrow_ref[e]).start()
        for e in range(N_EXPERTS):
            zero_fill(zrow_ref[e]).wait()

        def tail_start(b, carry):
            zero_fill(b * RB).start()
            return carry

        def tail_wait(b, carry):
            zero_fill(b * RB).wait()
            return carry

        lax.fori_loop(nu_ref[0], N_BLOCKS, tail_start, 0)
        lax.fori_loop(nu_ref[0], N_BLOCKS, tail_wait, 0)

    def body(t, carry):
        for k in range(TOP_K):
            pltpu.make_async_copy(_tile_rows(h2_ref, t), _tile_rows(xs_ref, dest_ref[0, k * TB + t]),
                                  sem).start(priority=k % 2)
        return carry

    lax.fori_loop(0, TB, body, 0, unroll=8)
    for k in range(TOP_K):
        pltpu.make_async_copy(h2_ref, _tile_rows(xs_ref, 0, TB), sem).wait()


def _dispatch_call(zrow, n_used, dest, h2):
    nb = N_TOK // TB
    grid_spec = pltpu.PrefetchScalarGridSpec(
        num_scalar_prefetch=2,
        grid=(nb,),
        in_specs=[
            pl.BlockSpec((None, 1, TOP_K * TB), lambda i, z, n: (i, 0, 0), memory_space=pltpu.SMEM),
            pl.BlockSpec((TB * SUB, LANES), lambda i, z, n: (i, 0)),
        ],
        out_specs=pl.BlockSpec(memory_space=pl.ANY),
        scratch_shapes=[pltpu.VMEM((RB * SUB, LANES), F32), pltpu.SemaphoreType.DMA,
                        pltpu.SemaphoreType.DMA],
    )
    return pl.pallas_call(
        _dispatch_kernel,
        grid_spec=grid_spec,
        out_shape=jax.ShapeDtypeStruct((N_ROWS * SUB, LANES), F32),
        compiler_params=pltpu.CompilerParams(dimension_semantics=("arbitrary",),
                                             vmem_limit_bytes=VMEM_LIMIT),
        name="dispatch_rows",
    )(zrow, n_used, dest, h2)


def _moe_kernel(be_ref, first_ref, nxt_ref, nxt2_ref, slot_ref, nu_ref, xs_ref, wgu_hbm, bgu_ref, wdn_hbm,
                bdn_ref, o_ref, wgu_f, wdn_f, sems, *, layer):
    b = pl.program_id(0)
    used = b < nu_ref[0]

    def fetch(e, s):
        return (pltpu.make_async_copy(wgu_hbm.at[layer, e], wgu_f.at[s], sems.at[0, s]),
                pltpu.make_async_copy(wdn_hbm.at[layer, e], wdn_f.at[s], sems.at[1, s]))

    @pl.when(b == 0)
    def _():
        for cp in fetch(be_ref[0], 0):
            cp.start(priority=1)

        @pl.when(nxt_ref[0] >= 0)
        def _():
            for cp in fetch(nxt_ref[0], 1):
                cp.start(priority=1)

    @pl.when(first_ref[b] == 1)
    def _():
        s = slot_ref[b]
        for cp in fetch(be_ref[b], s):
            cp.wait()

        @pl.when(nxt2_ref[b] >= 0)
        def _():
            for cp in fetch(nxt2_ref[b], (s + 2) % W_SLOTS):
                cp.start(priority=1)

    @pl.when(used)
    def _():
        s = slot_ref[b]
        x = _load_tiled(xs_ref, RB).astype(BF16)
        gu = jnp.dot(x, wgu_f[s].astype(BF16), preferred_element_type=F32) + bgu_ref[...]
        g = jnp.minimum(gu[:, :D_FF], SWIGLU_LIMIT)
        l = jnp.clip(gu[:, D_FF:], -SWIGLU_LIMIT, SWIGLU_LIMIT)
        a = g * jax.nn.sigmoid(SWIGLU_ALPHA * g) * (l + 1.0)
        y = jnp.dot(a.astype(BF16), wdn_f[s].astype(BF16), preferred_element_type=F32) + bdn_ref[...]
        _store_tiled(o_ref, y)

    @pl.when(jnp.logical_not(used))
    def _():
        o_ref[...] = jnp.zeros(o_ref.shape, o_ref.dtype)


def _moe_call(layer, tables, xs, w_gu, b_gu, w_dn, b_dn):
    def rows_in(b, be, fi, nx, n2, sl, nu):
        return (jnp.minimum(b, nu[0] - 1), 0)

    def rows_out(b, be, fi, nx, n2, sl, nu):
        return (b, 0)

    def expert(b, be, fi, nx, n2, sl, nu):
        return (layer, be[b], 0, 0)

    grid_spec = pltpu.PrefetchScalarGridSpec(
        num_scalar_prefetch=6,
        grid=(N_BLOCKS,),
        in_specs=[
            pl.BlockSpec((RB * SUB, LANES), rows_in),
            pl.BlockSpec(memory_space=pl.ANY),
            pl.BlockSpec((None, None, 1, 2 * D_FF), expert),
            pl.BlockSpec(memory_space=pl.ANY),
            pl.BlockSpec((None, None, 1, D), expert),
        ],
        out_specs=pl.BlockSpec((RB * SUB, LANES), rows_out),
        scratch_shapes=[pltpu.VMEM((W_SLOTS, D, 2 * D_FF), F32), pltpu.VMEM((W_SLOTS, D_FF, D), F32),
                        pltpu.SemaphoreType.DMA((2, W_SLOTS))],
    )
    return pl.pallas_call(
        functools.partial(_moe_kernel, layer=layer),
        grid_spec=grid_spec,
        out_shape=jax.ShapeDtypeStruct((N_ROWS * SUB, LANES), F32),
        compiler_params=pltpu.CompilerParams(dimension_semantics=("arbitrary",),
                                             vmem_limit_bytes=VMEM_LIMIT),
        name="moe_experts",
    )(*tables, xs, w_gu, b_gu, w_dn, b_dn)


def _final_kernel(*refs):
    _combine_body(pl.program_id(0), *refs)


def _final_pre_kernel(*refs):
    n_comb_in, n_pre_in = 7, 11
    comb_in, refs = refs[:n_comb_in], refs[n_comb_in:]
    pre_in, refs = refs[:n_pre_in], refs[n_pre_in:]
    (oc_ref, ol_ref), pre_out, (buf, sems) = refs[:2], refs[2:10], refs[10:]
    i = pl.program_id(0)
    x2 = _combine_body(i, *comb_in, oc_ref, ol_ref, buf, sems)
    _pre_body(i, x2, *pre_in, *pre_out)


def _combine_body(i, dcur_ref, dnxt_ref, x1_ref, ys_ref, rw_ref, mod_ref, g_ref, oc_ref, ol_ref, buf, sems):
    nb = pl.num_programs(0)
    slot = i % 2

    def gather(dest_ref, s):
        def body(t, carry):
            for k in range(TOP_K):
                pltpu.make_async_copy(_tile_rows(ys_ref, dest_ref[0, k * TB + t]),
                                      _tile_rows(buf.at[s], k * TB + t), sems.at[s]).start(priority=k % 2)
            return carry

        lax.fori_loop(0, TB, body, 0, unroll=8)

    @pl.when(i == 0)
    def _():
        gather(dcur_ref, 0)

    @pl.when(i + 1 < nb)
    def _():
        gather(dnxt_ref, 1 - slot)

    pltpu.make_async_copy(_tile_rows(ys_ref, 0, TOP_K * TB), buf.at[slot], sems.at[slot]).wait()

    rw = rw_ref[...]
    y = jnp.zeros((TB, D), F32)
    for k in range(TOP_K):
        yk = jnp.concatenate([buf[slot, pl.ds(k * TB * SUB + s, TB, stride=SUB), :] for s in range(SUB)],
                             axis=1)
        y = y + yk * rw[:, k:k + 1]
    x2 = x1_ref[...] + mod_ref[5:6, :] * _rms(y, g_ref[...])

    @pl.when(i < N_CTX // TB)
    def _():
        oc_ref[...] = x2

    @pl.when(i >= N_CTX // TB)
    def _():
        ol_ref[...] = x2

    return x2


def _final_call(layer, dest_flat, x1, ys, rw, mods, g_post_ffn, next_pre_params=None):
    nb = N_TOK // TB
    in_specs = [pl.BlockSpec((None, 1, TOP_K * TB), lambda i: (i, 0, 0), memory_space=pltpu.SMEM),
                pl.BlockSpec((None, 1, TOP_K * TB), lambda i: (jnp.minimum(i + 1, nb - 1), 0, 0),
                             memory_space=pltpu.SMEM),
                pl.BlockSpec((TB, D), lambda i: (i, 0)),
                pl.BlockSpec(memory_space=pl.ANY),
                pl.BlockSpec((TB, LANES), lambda i: (i, 0)),
                _mod_spec(layer), _layer_spec(layer, (1, D))]
    out_specs = (_ctx_spec(D), _lat_spec(D))
    out_shape = (jax.ShapeDtypeStruct((N_CTX, D), F32), jax.ShapeDtypeStruct((N_LAT, D), F32))
    args = (dest_flat, dest_flat, x1, ys, rw, mods, g_post_ffn)
    body, name = _final_kernel, "combine_residual"
    if next_pre_params is not None:
        pre_in, pre_out, pre_shape = _pre_specs(layer + 1)
        in_specs, out_specs, out_shape = in_specs + pre_in, out_specs + pre_out, out_shape + pre_shape
        args = args + tuple(next_pre_params)
        body, name = _final_pre_kernel, "combine_pre_mix"
    outs = pl.pallas_call(
        body,
        grid=(nb,),
        in_specs=in_specs,
        out_specs=out_specs,
        out_shape=out_shape,
        scratch_shapes=[pltpu.VMEM((2, TOP_K * TB * SUB, LANES), F32), pltpu.SemaphoreType.DMA((2,))],
        compiler_params=pltpu.CompilerParams(dimension_semantics=("arbitrary",),
                                             vmem_limit_bytes=VMEM_LIMIT),
        name=name,
    )(*args)
    return outs[0], outs[1], outs[2:]


def _dft_tables(t):
    j = np.arange(t, dtype=np.int64)
    ang = 2.0 * np.pi * ((j[:, None] * j[None, :]) % t) / t
    return (np.cos(ang) / math.sqrt(t)).astype(np.float32), (np.sin(ang) / math.sqrt(t)).astype(np.float32)


def _channel_dft():
    c = np.arange(F_GROUP_DIM, dtype=np.int64)
    ang = 2.0 * np.pi * ((c[:, None] * c[None, :]) % F_GROUP_DIM) / F_GROUP_DIM
    eye = np.eye(F_GROUPS)
    bdc = np.kron(eye, np.cos(ang)) / math.sqrt(F_GROUP_DIM)
    bds = np.kron(eye, np.sin(ang)) / math.sqrt(F_GROUP_DIM)
    return bdc.astype(np.float32), bds.astype(np.float32)


def _rope_tables():
    pos = np.arange(LAT_T)
    n = ROPE // 4
    inv_freq = np.power(np.float32(ROPE_BASE), -np.arange(n, dtype=np.float32) / np.float32(n))
    ang_r = (pos // GRID_W).astype(np.float32)[:, None] * inv_freq
    ang_c = (pos % GRID_W).astype(np.float32)[:, None] * inv_freq
    cos = np.concatenate([np.cos(ang_r), np.cos(ang_r), np.cos(ang_c), np.cos(ang_c),
                          np.ones((LAT_T, LANES - ROPE))], axis=1)
    sin = np.concatenate([-np.sin(ang_r), np.sin(ang_r), -np.sin(ang_c), np.sin(ang_c),
                          np.zeros((LAT_T, LANES - ROPE))], axis=1)
    return cos.astype(np.float32), sin.astype(np.float32)


def kernel(x_prompt, x_sample, cache_ckv, cache_krope, c, c_ctx, w_ada, b_ada, g_pre_mix, g_post_mix, g_pre_ffn, g_post_ffn, w_in, g_sgu, w_spatial, b_spatial, g_q, w_uq, g_kv, w_ukv, w_out, w_router, b_router, w_gate_up, b_gate_up, w_down, b_down):
    xc, xl = x_prompt.reshape(N_CTX, D), x_sample.reshape(N_LAT, D)

    cond = jnp.concatenate([c_ctx[None, :], c, jnp.zeros((5, D), F32)], axis=0)
    mods = _mod_call(cond.T, w_ada, b_ada.reshape(DEPTH, 1, 6 * D))
    mods = mods[:, :3].reshape(DEPTH, 3, 6, D)

    gmat = jnp.asarray(np.kron(np.eye(A_HEADS), np.full((A_HEAD_DIM, A_HEAD_DIM), 1.0 / A_HEAD_DIM)),
                       dtype=BF16)
    bdc_np, bds_np = _channel_dft()
    bdc, bds = jnp.asarray(bdc_np).astype(BF16), jnp.asarray(bds_np).astype(BF16)
    dft = {}
    for t in (CTX_T, LAT_T):
        ct_np, st_np = _dft_tables(t)
        dft[t] = (jnp.asarray(ct_np).astype(BF16), jnp.asarray(st_np).astype(BF16))
    cos_np, sin_np = _rope_tables()
    cos_t, sin_t = jnp.asarray(cos_np), jnp.asarray(sin_np)
    tri = jnp.asarray(np.tril(np.ones((TB, TB), np.float32), k=-1)).astype(BF16)

    b_gu = b_gate_up.reshape(DEPTH, N_EXPERTS, 1, 2 * D_FF)
    b_dn = b_down.reshape(DEPTH, N_EXPERTS, 1, D)
    w_in_p = jnp.pad(w_in, ((0, 0), (0, 0), (0, IN_PAD - w_in.shape[-1]))).astype(BF16)
    w_uq_p = jnp.pad(w_uq.reshape(DEPTH, Q_LORA, HEADS, NOPE + ROPE),
                     ((0, 0), (0, 0), (0, 0), (0, HEAD_PAD - NOPE - ROPE)))
    w_uq_p = w_uq_p.reshape(DEPTH, Q_LORA, HEADS * HEAD_PAD).astype(BF16)
    w_ukv_b = w_ukv.astype(BF16)
    ws = w_spatial.astype(BF16)
    bs_full = jnp.repeat(jnp.swapaxes(b_spatial, 1, 2), A_HEAD_DIM, axis=2)
    ckr_p = jnp.pad(cache_krope, ((0, 0), (0, 0), (0, 0), (0, LANES - ROPE))).astype(BF16)
    w_out_b = w_out.astype(BF16)
    w_r = jnp.pad(w_router, ((0, 0), (0, 0), (0, LANES - N_EXPERTS)))
    b_r = jnp.pad(b_router, ((0, 0), (0, LANES - N_EXPERTS)), constant_values=NEG)[:, None, :]
    row = lambda g: g[:, None, :]

    pre_params = (mods, row(g_pre_mix), w_in_p, row(g_sgu), gmat, row(g_q), w_uq_p, row(g_kv), w_ukv_b,
                  cos_t, sin_t)

    ckv_layers, krope_layers = [], []
    pre = _pre_call(0, xc, xl, pre_params)
    for i in range(DEPTH):
        u, vn, zf, q, kv, kr, ckv, zkr = pre
        ckv_layers.append(ckv[:N_CTX].reshape(N_CTX_B, CTX_T, KV_LORA))
        krope_layers.append(zkr[:N_CTX, :ROPE].reshape(N_CTX_B, CTX_T, ROPE))

        mix_in = (u, vn, zf, q, kv, kr)
        y_ctx = _mix_call(i, mix_in, (ws, bs_full, bdc, bds) + dft[CTX_T], CTX_T, N_CTX_B, 0)
        y_lat = _mix_call(i, mix_in, (ws, bs_full, bdc, bds) + dft[LAT_T], LAT_T, N_LAT_B, N_CTX,
                          cache=(cache_ckv, ckr_p, w_ukv_b))

        x1, h2, tab, rw, cnt = _post_call(i, y_ctx, y_lat, xc, xl, mods, w_out_b,
                                          row(g_post_mix), row(g_pre_ffn), w_r, b_r, tri)

        dest, zrow, n_used, block_e, first, nxt, nxt2, slot = _plan_call(cnt, tab)
        dest_flat = dest.reshape(N_TOK // TB, 1, TOP_K * TB)
        xs = _dispatch_call(zrow, n_used, dest_flat, h2)
        ys = _moe_call(i, (block_e, first, nxt, nxt2, slot, n_used), xs, w_gate_up, b_gu, w_down, b_dn)
        xc, xl, pre = _final_call(i, dest_flat, x1, ys, rw, mods, row(g_post_ffn),
                                  next_pre_params=pre_params if i + 1 < DEPTH else None)

    y_prompt = xc.reshape(N_CTX_B, CTX_T, D)
    y_sample = xl.reshape(N_LAT_B, LAT_T, D)
    return (y_prompt, y_sample, jnp.stack(ckv_layers, axis=1), jnp.stack(krope_layers, axis=1))
```

```python
import functools
import math

import jax
import jax.numpy as jnp
import numpy as np
from jax import lax
from jax.experimental import pallas as pl
from jax.experimental.pallas import tpu as pltpu

F32 = jnp.float32
BF16 = jnp.bfloat16

D = 1024
N_CTX_B, CTX_T = 16, 256
N_LAT_B, LAT_T = 2, 1024
PAST = 512
N_CTX = N_CTX_B * CTX_T
N_LAT = N_LAT_B * LAT_T
N_TOK = N_CTX + N_LAT
DEPTH = 2
GRID_W = 64
EPS = 1e-6
A_HEADS, A_HEAD_DIM, A_WIDTH, CHUNK = 4, 64, 256, 128
F_GROUPS, F_GROUP_DIM, F_WIDTH = 4, 64, 256
HEADS, Q_LORA, KV_LORA, NOPE, ROPE, V_DIM = 4, 256, 128, 128, 64, 128
HEAD_PAD = 256
IN_PAD = 1280
N_EXPERTS, TOP_K, D_FF = 32, 4, 1024
SWIGLU_LIMIT, SWIGLU_ALPHA = 7.0, 1.702
ROPE_BASE = 10000.0

TB = 512
QB = 256
RB = 256
W_SLOTS = 3
GATHER_CHUNKS = 8
LANES = 128
SUB = D // LANES
N_BLOCKS = N_TOK * TOP_K // RB + N_EXPERTS
N_ROWS = N_BLOCKS * RB
NEG = -3.0e38
VMEM_LIMIT = 56 * 1024 * 1024


def _rms(x, g):
    return x * lax.rsqrt(jnp.mean(x * x, axis=-1, keepdims=True) + EPS) * g


def _split_dot(v, m):
    hi = v.astype(BF16)
    lo = (v - hi.astype(F32)).astype(BF16)
    return (jnp.dot(hi, m, preferred_element_type=F32)
            + jnp.dot(lo, m, preferred_element_type=F32))


def _dot_nt(a, b):
    return lax.dot_general(a, b, (((1,), (1,)), ((), ())), preferred_element_type=F32)


def _store_tiled(ref, x):
    rows = x.shape[0]
    for s in range(SUB):
        ref[pl.ds(s, rows, stride=SUB), :] = x[:, s * LANES:(s + 1) * LANES]


def _load_tiled(ref, rows):
    return jnp.concatenate([ref[pl.ds(s, rows, stride=SUB), :] for s in range(SUB)], axis=1)


def _tile_rows(ref, row, n_rows=1):
    return ref.at[pl.ds(pl.multiple_of(row * SUB, SUB), n_rows * SUB)]


def _mod_kernel(ct_ref, w_ref, b_ref, o_ref):
    ct = ct_ref[...]
    s = ct * jax.nn.sigmoid(ct)
    w = w_ref[...]
    o_ref[...] = jnp.zeros(o_ref.shape, F32)
    for r in range(3):
        o_ref[r:r + 1, :] = jnp.sum(w * s[:, r:r + 1], axis=0, keepdims=True) + b_ref[...]


def _mod_call(cond_t, w_ada, b_ada):
    cb = 2048
    return pl.pallas_call(
        _mod_kernel,
        grid=(DEPTH, 6 * D // cb),
        in_specs=[
            pl.BlockSpec((D, 8), lambda l, j: (0, 0)),
            pl.BlockSpec((None, D, cb), lambda l, j: (l, 0, j)),
            pl.BlockSpec((None, 1, cb), lambda l, j: (l, 0, j)),
        ],
        out_specs=pl.BlockSpec((None, 8, cb), lambda l, j: (l, 0, j)),
        out_shape=jax.ShapeDtypeStruct((DEPTH, 8, 6 * D), F32),
        compiler_params=pltpu.CompilerParams(dimension_semantics=("parallel", "parallel")),
        name="modulation",
    )(cond_t, w_ada, b_ada)


def _mod_index(i):
    first_lat = N_CTX // TB
    return jnp.where(i < first_lat, 0, 1 + (i - first_lat) // (LAT_T // TB))


def _ctx_spec(width):
    return pl.BlockSpec((TB, width), lambda i: (jnp.minimum(i, N_CTX // TB - 1), 0))


def _lat_spec(width):
    return pl.BlockSpec((TB, width), lambda i: (jnp.maximum(i - N_CTX // TB, 0), 0))


def _pick(i, ctx_ref, lat_ref):
    return jnp.where(i >= N_CTX // TB, lat_ref[...], ctx_ref[...])


def _layer_spec(layer, shape):
    zeros = (0,) * len(shape)
    return pl.BlockSpec((None,) + tuple(shape), lambda *_: (layer,) + zeros)


def _mod_spec(layer):
    return pl.BlockSpec((None, None, 6, D), lambda i: (layer, _mod_index(i), 0, 0))


def _swap_halves(x, lane):
    w = x.shape[-1]
    fwd = pltpu.roll(x, w - 16, 1)
    bwd = pltpu.roll(x, 16, 1)
    return jnp.where((lane & 31) < 16, fwd, bwd)


def _pre_kernel(xc_ref, xl_ref, *refs):
    i = pl.program_id(0)
    _pre_body(i, _pick(i, xc_ref, xl_ref), *refs)


def _pre_body(i, x, mod_ref, gpre_ref, win_ref, gsgu_ref, gmat_ref, gq_ref, wuq_ref,
              gkv_ref, wukv_ref, cos_ref, sin_ref,
              u_ref, vn_ref, zf_ref, q_ref, kv_ref, kr_ref, ckv_ref, zkr_ref, between=lambda: None):
    is_lat = i >= N_CTX // TB
    h = _rms(x, gpre_ref[...]) * (1.0 + mod_ref[1:2, :]) + mod_ref[0:1, :]
    z = jnp.dot(h.astype(BF16), win_ref[...], preferred_element_type=F32)
    between()

    ga = jax.nn.gelu(z[:, :2 * A_WIDTH])
    u_ref[...] = ga[:, :A_WIDTH]
    v = ga[:, A_WIDTH:]
    gmat = gmat_ref[...]
    dv = v - _split_dot(v, gmat)
    var = _split_dot(dv * dv, gmat)
    vn_ref[...] = (dv * lax.rsqrt(var + EPS) * gsgu_ref[...]).astype(BF16)
    between()

    zf_ref[...] = z[:, 512:768].astype(BF16)
    between()

    cos = jnp.where(is_lat, cos_ref[...], 1.0)
    sin = jnp.where(is_lat, sin_ref[...], 0.0)
    lane = lax.broadcasted_iota(jnp.int32, (TB, LANES), 1)

    qn = _rms(z[:, 768:1024], gq_ref[...])
    scale = (NOPE + ROPE) ** -0.5
    q = jnp.dot(qn.astype(BF16), wuq_ref[...], preferred_element_type=F32) * scale
    for hd in range(HEADS):
        base = hd * HEAD_PAD
        q_ref[:, base:base + NOPE] = q[:, base:base + NOPE].astype(BF16)
        qr = q[:, base + NOPE:base + HEAD_PAD]
        q_ref[:, base + NOPE:base + HEAD_PAD] = (qr * cos + _swap_halves(qr, lane) * sin).astype(BF16)
    between()

    ckv = _rms(z[:, 1024:1152], gkv_ref[...])
    ckv_ref[...] = ckv
    kv_ref[...] = jnp.dot(ckv.astype(BF16), wukv_ref[...], preferred_element_type=F32).astype(BF16)
    between()

    zkr = z[:, 1152:1280]
    zkr_ref[...] = zkr
    kr_ref[...] = (zkr * cos + _swap_halves(zkr, lane) * sin).astype(BF16)
    between()


def _pre_specs(layer):
    lay = functools.partial(_layer_spec, layer)
    first_lat = N_CTX // TB
    pos_blocks = LAT_T // TB

    def tok(width):
        return pl.BlockSpec((TB, width), lambda i: (i, 0))

    def full(shape):
        return pl.BlockSpec(shape, lambda i: (0,) * len(shape))

    def rope_map(i):
        return (jnp.where(i >= first_lat, (i - first_lat) % pos_blocks, 0), 0)

    in_specs = [
        _mod_spec(layer),
        lay((1, D)), lay((D, IN_PAD)), lay((1, A_WIDTH)), full((A_WIDTH, A_WIDTH)),
        lay((1, Q_LORA)), lay((Q_LORA, HEADS * HEAD_PAD)),
        lay((1, KV_LORA)), lay((KV_LORA, HEADS * (NOPE + V_DIM))),
        pl.BlockSpec((TB, LANES), rope_map), pl.BlockSpec((TB, LANES), rope_map),
    ]
    out_specs = (tok(A_WIDTH), tok(A_WIDTH), tok(F_WIDTH), tok(HEADS * HEAD_PAD),
                 tok(HEADS * (NOPE + V_DIM)), tok(LANES), tok(KV_LORA), tok(LANES))
    out_shape = (
        jax.ShapeDtypeStruct((N_TOK, A_WIDTH), F32),
        jax.ShapeDtypeStruct((N_TOK, A_WIDTH), BF16),
        jax.ShapeDtypeStruct((N_TOK, F_WIDTH), BF16),
        jax.ShapeDtypeStruct((N_TOK, HEADS * HEAD_PAD), BF16),
        jax.ShapeDtypeStruct((N_TOK, HEADS * (NOPE + V_DIM)), BF16),
        jax.ShapeDtypeStruct((N_TOK, LANES), BF16),
        jax.ShapeDtypeStruct((N_TOK, KV_LORA), F32),
        jax.ShapeDtypeStruct((N_TOK, LANES), F32),
    )
    return in_specs, out_specs, out_shape


def _pre_call(layer, xc, xl, pre_params):
    in_specs, out_specs, out_shape = _pre_specs(layer)
    return pl.pallas_call(
        _pre_kernel,
        grid=(N_TOK // TB,),
        in_specs=[_ctx_spec(D), _lat_spec(D)] + in_specs,
        out_specs=out_specs,
        out_shape=out_shape,
        compiler_params=pltpu.CompilerParams(dimension_semantics=("parallel",),
                                             vmem_limit_bytes=VMEM_LIMIT),
        name="pre_mix",
    )(xc, xl, *pre_params)


def _mix_kernel(*refs, has_cache):
    if has_cache:
        (u_ref, vn_ref, zf_ref, q_ref, kv_ref, kr_ref, ws_ref, bs_ref, bdc_ref, bds_ref,
         ct_ref, st_ref, cckv_ref, ckr_ref, wukv_ref, o_ref) = refs
    else:
        (u_ref, vn_ref, zf_ref, q_ref, kv_ref, kr_ref, ws_ref, bs_ref, bdc_ref, bds_ref,
         ct_ref, st_ref, o_ref) = refs

    lane = lax.broadcasted_iota(jnp.int32, (CHUNK, A_WIDTH), 1)
    for c in range(QB // CHUNK):
        rows = slice(c * CHUNK, (c + 1) * CHUNK)
        vch = vn_ref[rows, :]
        s = bs_ref[...]
        for g in range(A_HEADS):
            sg = jnp.dot(ws_ref[g], vch, preferred_element_type=F32)
            in_head = (lane >= g * A_HEAD_DIM) & (lane < (g + 1) * A_HEAD_DIM)
            s = s + jnp.where(in_head, sg, 0.0)
        o_ref[rows, 0:A_WIDTH] = (u_ref[rows, :] * s).astype(BF16)

    zf = zf_ref[...]
    zc = jnp.dot(zf, bdc_ref[...], preferred_element_type=F32).astype(BF16)
    zs = jnp.dot(zf, bds_ref[...], preferred_element_type=F32).astype(BF16)
    yf = (jnp.dot(ct_ref[...], zc, preferred_element_type=F32)
          - jnp.dot(st_ref[...], zs, preferred_element_type=F32))
    o_ref[:, A_WIDTH:A_WIDTH + F_WIDTH] = yf.astype(BF16)

    kr = kr_ref[...]
    if has_cache:
        kvc = jnp.dot(cckv_ref[...].astype(BF16), wukv_ref[...],
                      preferred_element_type=F32).astype(BF16)
        krc = ckr_ref[...]
    for hd in range(HEADS):
        qh = q_ref[:, hd * HEAD_PAD:(hd + 1) * HEAD_PAD]
        kb = hd * (NOPE + V_DIM)
        kh = jnp.concatenate([kv_ref[:, kb:kb + NOPE], kr], axis=1)
        vh = kv_ref[:, kb + NOPE:kb + NOPE + V_DIM]
        s = _dot_nt(qh, kh)
        m = jnp.max(s, axis=-1, keepdims=True)
        if has_cache:
            khc = jnp.concatenate([kvc[:, kb:kb + NOPE], krc], axis=1)
            vhc = kvc[:, kb + NOPE:kb + NOPE + V_DIM]
            sc = _dot_nt(qh, khc)
            m = jnp.maximum(m, jnp.max(sc, axis=-1, keepdims=True))
        e = jnp.exp(s - m)
        den = jnp.sum(e, axis=-1, keepdims=True)
        o = jnp.dot(e.astype(BF16), vh, preferred_element_type=F32)
        if has_cache:
            ec = jnp.exp(sc - m)
            den = den + jnp.sum(ec, axis=-1, keepdims=True)
            o = o + jnp.dot(ec.astype(BF16), vhc, preferred_element_type=F32)
        ob = A_WIDTH + F_WIDTH + hd * V_DIM
        o_ref[:, ob:ob + V_DIM] = (o * (1.0 / den)).astype(BF16)


def _mix_call(layer, pre, consts, seq_t, n_batch, tok_off, cache=None):
    u, vn, zf, q, kv, kr = pre
    ws, bs_full, bdc, bds, ct, st = consts
    lay = functools.partial(_layer_spec, layer)
    nq = seq_t // QB
    qoff = tok_off // QB
    soff = tok_off // seq_t

    def qrow(width):
        return pl.BlockSpec((QB, width), lambda b, j: (qoff + b * nq + j, 0))

    def srow(width):
        return pl.BlockSpec((seq_t, width), lambda b, j: (soff + b, 0))

    def full(shape):
        return pl.BlockSpec(shape, lambda b, j: (0,) * len(shape))

    in_specs = [qrow(A_WIDTH), qrow(A_WIDTH), srow(F_WIDTH), qrow(HEADS * HEAD_PAD),
                srow(HEADS * (NOPE + V_DIM)), srow(LANES),
                lay((A_HEADS, CHUNK, CHUNK)), lay((CHUNK, A_WIDTH)),
                full((F_WIDTH, F_WIDTH)), full((F_WIDTH, F_WIDTH)),
                pl.BlockSpec((QB, seq_t), lambda b, j: (j, 0)),
                pl.BlockSpec((QB, seq_t), lambda b, j: (j, 0))]
    args = [u, vn, zf, q, kv, kr, ws, bs_full, bdc, bds, ct, st]
    if cache is not None:
        cckv, ckr, wukv = cache
        in_specs += [pl.BlockSpec((None, None, PAST, KV_LORA), lambda b, j: (b, layer, 0, 0)),
                     pl.BlockSpec((None, None, PAST, LANES), lambda b, j: (b, layer, 0, 0)),
                     lay((KV_LORA, HEADS * (NOPE + V_DIM)))]
        args += [cckv, ckr, wukv]
    return pl.pallas_call(
        functools.partial(_mix_kernel, has_cache=cache is not None),
        grid=(n_batch, nq),
        in_specs=in_specs,
        out_specs=pl.BlockSpec((QB, D), lambda b, j: (b * nq + j, 0)),
        out_shape=jax.ShapeDtypeStruct((n_batch * seq_t, D), BF16),
        compiler_params=pltpu.CompilerParams(dimension_semantics=("parallel", "parallel"),
                                             vmem_limit_bytes=VMEM_LIMIT),
        name="mix_lat" if cache is not None else "mix_ctx",
    )(*args)


def _post_kernel(yc_ref, yl_ref, xc_ref, xl_ref, mod_ref, wout_ref, gpost_ref, gffn_ref, wr_ref, br_ref,
                 tri_ref, x1_ref, h2_ref, tab_ref, rw_ref, cnt_ref, carry_ref):
    i = pl.program_id(0)

    @pl.when(i == 0)
    def _():
        carry_ref[...] = jnp.zeros(carry_ref.shape, F32)

    y = jnp.dot(_pick(i, yc_ref, yl_ref), wout_ref[...], preferred_element_type=F32)
    x1 = _pick(i, xc_ref, xl_ref) + mod_ref[2:3, :] * _rms(y, gpost_ref[...])
    x1_ref[...] = x1
    h2 = _rms(x1, gffn_ref[...]) * (1.0 + mod_ref[4:5, :]) + mod_ref[3:4, :]
    _store_tiled(h2_ref, h2)

    wr = wr_ref[...]
    wr_hi = wr.astype(BF16)
    wr_lo = (wr - wr_hi.astype(F32)).astype(BF16)
    h_hi = h2.astype(BF16)
    h_lo = (h2 - h_hi.astype(F32)).astype(BF16)
    logits = (jnp.dot(h_hi, wr_hi, preferred_element_type=F32)
              + jnp.dot(h_lo, wr_hi, preferred_element_type=F32)
              + jnp.dot(h_hi, wr_lo, preferred_element_type=F32)) + br_ref[...]

    lane = lax.broadcasted_iota(jnp.int32, (TB, LANES), 1)
    lane_f = lane.astype(F32)
    work = logits
    idx, val = [], []
    for _ in range(TOP_K):
        m = jnp.max(work, axis=-1, keepdims=True)
        ik = jnp.min(jnp.where(work == m, lane_f, float(LANES)), axis=-1, keepdims=True)
        idx.append(ik)
        val.append(m)
        work = jnp.where(lane_f == ik, NEG, work)
    ex = [jnp.exp(v - val[0]) for v in val]
    den = ex[0] + ex[1] + ex[2] + ex[3]

    onehot = jnp.zeros((TB, LANES), F32)
    for k in range(TOP_K):
        onehot = onehot + jnp.where(lane_f == idx[k], 1.0, 0.0)
    before = jnp.dot(tri_ref[...], onehot.astype(BF16), preferred_element_type=F32) + carry_ref[0:1, :]
    ri = jnp.zeros((TB, LANES), F32)
    rw = jnp.zeros((TB, LANES), F32)
    for k in range(TOP_K):
        rank_k = jnp.sum(jnp.where(lane_f == idx[k], before, 0.0), axis=-1, keepdims=True)
        ri = ri + jnp.where(lane == k, idx[k], 0.0) + jnp.where(lane == TOP_K + k, rank_k, 0.0)
        rw = rw + jnp.where(lane == k, ex[k] / den, 0.0)
    tab_ref[...] = ri.T[0:2 * TOP_K, :].astype(jnp.int32)
    rw_ref[...] = rw
    total = carry_ref[0:1, :] + jnp.sum(onehot, axis=0, keepdims=True)
    carry_ref[...] = jnp.broadcast_to(total, carry_ref.shape)
    cnt_ref[...] = jnp.broadcast_to(total, cnt_ref.shape).astype(jnp.int32)


def _post_call(layer, yc, yl, xc, xl, mods, w_out, g_post, g_ffn, w_r, b_r, tri):
    nb = N_TOK // TB
    lay = functools.partial(_layer_spec, layer)

    def tok(width):
        return pl.BlockSpec((TB, width), lambda i: (i, 0))

    def full(shape):
        return pl.BlockSpec(shape, lambda i: (0,) * len(shape))

    return pl.pallas_call(
        _post_kernel,
        grid=(nb,),
        in_specs=[_ctx_spec(D), _lat_spec(D), _ctx_spec(D), _lat_spec(D), _mod_spec(layer),
                  lay((D, D)), lay((1, D)), lay((1, D)), lay((D, LANES)), lay((1, LANES)),
                  full((TB, TB))],
        out_specs=(tok(D), pl.BlockSpec((TB * SUB, LANES), lambda i: (i, 0)),
                   pl.BlockSpec((None, 2 * TOP_K, TB), lambda i: (i, 0, 0)),
                   tok(LANES), full((8, LANES))),
        out_shape=(jax.ShapeDtypeStruct((N_TOK, D), F32),
                   jax.ShapeDtypeStruct((N_TOK * SUB, LANES), F32),
                   jax.ShapeDtypeStruct((nb, 2 * TOP_K, TB), jnp.int32),
                   jax.ShapeDtypeStruct((N_TOK, LANES), F32),
                   jax.ShapeDtypeStruct((8, LANES), jnp.int32)),
        scratch_shapes=[pltpu.VMEM((8, LANES), F32)],
        compiler_params=pltpu.CompilerParams(dimension_semantics=("arbitrary",),
                                             vmem_limit_bytes=VMEM_LIMIT),
        name="post_mix_router",
    )(yc, yl, xc, xl, mods, w_out, g_post, g_ffn, w_r, b_r, tri)


def _plan_kernel(cnt_ref, tab_ref, dest_ref, zrow_ref, nu_ref, be_ref, first_ref, nxt_ref, nxt2_ref,
                 slot_ref, start_s, next_s):
    nxt = jnp.int32(-1)
    for e in reversed(range(N_EXPERTS)):
        next_s[e] = nxt
        nxt = jnp.where(cnt_ref[0, e] > 0, jnp.int32(e), nxt)

    start = jnp.int32(0)
    run = jnp.int32(0)
    last = jnp.int32(0)
    for e in range(N_EXPERTS):
        c = cnt_ref[0, e]
        n_blk = lax.shift_right_logical(c + (RB - 1), RB.bit_length() - 1)
        start_s[e] = start
        zrow_ref[e] = start * RB + lax.shift_left(lax.shift_right_logical(c, 3), 3)
        n1 = next_s[e]
        n2 = jnp.where(n1 >= 0, next_s[jnp.maximum(n1, 0)], jnp.int32(-1))
        slot = lax.rem(run, W_SLOTS)

        def fill(j, carry, e=e, n1=n1, n2=n2, slot=slot, start=start):
            be_ref[j] = jnp.int32(e)
            first_ref[j] = (j == start).astype(jnp.int32)
            nxt_ref[j] = n1
            nxt2_ref[j] = n2
            slot_ref[j] = slot
            return carry

        lax.fori_loop(start, start + n_blk, fill, 0)
        owns = n_blk > 0
        last = jnp.where(owns, jnp.int32(e), last)
        run = run + owns.astype(jnp.int32)
        start = start + n_blk
    nu_ref[0] = start

    def tail(j, carry):
        be_ref[j] = last
        first_ref[j] = jnp.int32(0)
        nxt_ref[j] = jnp.int32(-1)
        nxt2_ref[j] = jnp.int32(-1)
        slot_ref[j] = jnp.int32(0)
        return carry

    lax.fori_loop(start, N_BLOCKS, tail, 0)

    idx = tab_ref[:, 0:TOP_K, :]
    base = jnp.zeros(idx.shape, jnp.int32)
    for e in range(N_EXPERTS):
        base = jnp.where(idx == e, start_s[e] * RB, base)
    dest_ref[...] = base + tab_ref[:, TOP_K:2 * TOP_K, :]


def _plan_call(cnt, tab):
    nb = N_TOK // TB
    smem = pl.BlockSpec(memory_space=pltpu.SMEM)
    blocks = jax.ShapeDtypeStruct((N_BLOCKS,), jnp.int32)
    return pl.pallas_call(
        _plan_kernel,
        in_specs=[smem, pl.BlockSpec(memory_space=pltpu.VMEM)],
        out_specs=(pl.BlockSpec(memory_space=pltpu.VMEM), smem, smem, smem, smem, smem, smem, smem),
        out_shape=(jax.ShapeDtypeStruct((nb, TOP_K, TB), jnp.int32),
                   jax.ShapeDtypeStruct((N_EXPERTS,), jnp.int32),
                   jax.ShapeDtypeStruct((1,), jnp.int32),
                   blocks, blocks, blocks, blocks, blocks),
        scratch_shapes=[pltpu.SMEM((N_EXPERTS,), jnp.int32), pltpu.SMEM((N_EXPERTS,), jnp.int32)],
        name="plan_rows",
    )(cnt, tab)


def _dispatch_kernel(zrow_ref, nu_ref, dest_ref, h2_ref, xs_ref, zero_ref, sem, zsem):
    i = pl.program_id(0)

    def zero_fill(row):
        return pltpu.make_async_copy(zero_ref, _tile_rows(xs_ref, row, RB), zsem)

    @pl.when(i == 0)
    def _():
        zero_ref[...] = jnp.zeros(zero_ref.shape, F32)
        for e in range(N_EXPERTS):
            zero_fill(zrow_ref[e]).start()
        for e in range(N_EXPERTS):
            zero_fill(zrow_ref[e]).wait()

        def tail_start(b, carry):
            zero_fill(b * RB).start()
            return carry

        def tail_wait(b, carry):
            zero_fill(b * RB).wait()
            return carry

        lax.fori_loop(nu_ref[0], N_BLOCKS, tail_start, 0)
        lax.fori_loop(nu_ref[0], N_BLOCKS, tail_wait, 0)

    def body(t, carry):
        for k in range(TOP_K):
            pltpu.make_async_copy(_tile_rows(h2_ref, t), _tile_rows(xs_ref, dest_ref[0, k * TB + t]),
                                  sem).start(priority=k % 2)
        return carry

    lax.fori_loop(0, TB, body, 0, unroll=8)
    for k in range(TOP_K):
        pltpu.make_async_copy(h2_ref, _tile_rows(xs_ref, 0, TB), sem).wait()


def _dispatch_call(zrow, n_used, dest, h2):
    nb = N_TOK // TB
    grid_spec = pltpu.PrefetchScalarGridSpec(
        num_scalar_prefetch=2,
        grid=(nb,),
        in_specs=[
            pl.BlockSpec((None, 1, TOP_K * TB), lambda i, z, n: (i, 0, 0), memory_space=pltpu.SMEM),
            pl.BlockSpec((TB * SUB, LANES), lambda i, z, n: (i, 0)),
        ],
        out_specs=pl.BlockSpec(memory_space=pl.ANY),
        scratch_shapes=[pltpu.VMEM((RB * SUB, LANES), F32), pltpu.SemaphoreType.DMA,
                        pltpu.SemaphoreType.DMA],
    )
    return pl.pallas_call(
        _dispatch_kernel,
        grid_spec=grid_spec,
        out_shape=jax.ShapeDtypeStruct((N_ROWS * SUB, LANES), F32),
        compiler_params=pltpu.CompilerParams(dimension_semantics=("arbitrary",),
                                             vmem_limit_bytes=VMEM_LIMIT),
        name="dispatch_rows",
    )(zrow, n_used, dest, h2)


def _moe_kernel(be_ref, first_ref, nxt_ref, nxt2_ref, slot_ref, nu_ref, xs_ref, wgu_hbm, bgu_ref, wdn_hbm,
                bdn_ref, o_ref, wgu_f, wdn_f, sems, *, layer):
    b = pl.program_id(0)
    used = b < nu_ref[0]

    def fetch(e, s):
        return (pltpu.make_async_copy(wgu_hbm.at[layer, e], wgu_f.at[s], sems.at[0, s]),
                pltpu.make_async_copy(wdn_hbm.at[layer, e], wdn_f.at[s], sems.at[1, s]))

    @pl.when(b == 0)
    def _():
        for cp in fetch(be_ref[0], 0):
            cp.start(priority=1)

        @pl.when(nxt_ref[0] >= 0)
        def _():
            for cp in fetch(nxt_ref[0], 1):
                cp.start(priority=1)

    @pl.when(first_ref[b] == 1)
    def _():
        s = slot_ref[b]
        for cp in fetch(be_ref[b], s):
            cp.wait()

        @pl.when(nxt2_ref[b] >= 0)
        def _():
            for cp in fetch(nxt2_ref[b], (s + 2) % W_SLOTS):
                cp.start(priority=1)

    @pl.when(used)
    def _():
        s = slot_ref[b]
        x = _load_tiled(xs_ref, RB).astype(BF16)
        gu = jnp.dot(x, wgu_f[s].astype(BF16), preferred_element_type=F32) + bgu_ref[...]
        g = jnp.minimum(gu[:, :D_FF], SWIGLU_LIMIT)
        l = jnp.clip(gu[:, D_FF:], -SWIGLU_LIMIT, SWIGLU_LIMIT)
        a = g * jax.nn.sigmoid(SWIGLU_ALPHA * g) * (l + 1.0)
        y = jnp.dot(a.astype(BF16), wdn_f[s].astype(BF16), preferred_element_type=F32) + bdn_ref[...]
        _store_tiled(o_ref, y)

    @pl.when(jnp.logical_not(used))
    def _():
        o_ref[...] = jnp.zeros(o_ref.shape, o_ref.dtype)


def _moe_call(layer, tables, xs, w_gu, b_gu, w_dn, b_dn):
    def rows_in(b, be, fi, nx, n2, sl, nu):
        return (jnp.minimum(b, nu[0] - 1), 0)

    def rows_out(b, be, fi, nx, n2, sl, nu):
        return (b, 0)

    def expert(b, be, fi, nx, n2, sl, nu):
        return (layer, be[b], 0, 0)

    grid_spec = pltpu.PrefetchScalarGridSpec(
        num_scalar_prefetch=6,
        grid=(N_BLOCKS,),
        in_specs=[
            pl.BlockSpec((RB * SUB, LANES), rows_in),
            pl.BlockSpec(memory_space=pl.ANY),
            pl.BlockSpec((None, None, 1, 2 * D_FF), expert),
            pl.BlockSpec(memory_space=pl.ANY),
            pl.BlockSpec((None, None, 1, D), expert),
        ],
        out_specs=pl.BlockSpec((RB * SUB, LANES), rows_out),
        scratch_shapes=[pltpu.VMEM((W_SLOTS, D, 2 * D_FF), F32), pltpu.VMEM((W_SLOTS, D_FF, D), F32),
                        pltpu.SemaphoreType.DMA((2, W_SLOTS))],
    )
    return pl.pallas_call(
        functools.partial(_moe_kernel, layer=layer),
        grid_spec=grid_spec,
        out_shape=jax.ShapeDtypeStruct((N_ROWS * SUB, LANES), F32),
        compiler_params=pltpu.CompilerParams(dimension_semantics=("arbitrary",),
                                             vmem_limit_bytes=VMEM_LIMIT),
        name="moe_experts",
    )(*tables, xs, w_gu, b_gu, w_dn, b_dn)


def _final_kernel(*refs):
    _combine_body(pl.program_id(0), *refs)


def _final_pre_kernel(*refs):
    n_comb_in, n_pre_in = 7, 11
    comb_in, refs = refs[:n_comb_in], refs[n_comb_in:]
    pre_in, refs = refs[:n_pre_in], refs[n_pre_in:]
    (oc_ref, ol_ref), pre_out, (buf, sems) = refs[:2], refs[2:10], refs[10:]
    i = pl.program_id(0)
    x2, issue_next = _combine_body(i, *comb_in, oc_ref, ol_ref, buf, sems, spread=True)
    _pre_body(i, x2, *pre_in, *pre_out, between=issue_next)
    for _ in range(GATHER_CHUNKS):
        issue_next()


def _combine_body(i, dcur_ref, dnxt_ref, x1_ref, ys_ref, rw_ref, mod_ref, g_ref, oc_ref, ol_ref, buf, sems,
                  spread=False):
    nb = pl.num_programs(0)
    slot = i % 2
    chunk = TB // GATHER_CHUNKS

    def gather(dest_ref, s, first_chunk=0, n_chunks=GATHER_CHUNKS):
        def body(t, carry):
            for k in range(TOP_K):
                pltpu.make_async_copy(_tile_rows(ys_ref, dest_ref[0, k * TB + t]),
                                      _tile_rows(buf.at[s], k * TB + t), sems.at[s]).start(priority=k % 2)
            return carry

        lax.fori_loop(first_chunk * chunk, (first_chunk + n_chunks) * chunk, body, 0, unroll=8)

    pending = list(range(GATHER_CHUNKS))

    def issue_next():
        if pending:
            g = pending.pop(0)

            @pl.when(i + 1 < nb)
            def _():
                gather(dnxt_ref, 1 - slot, g, 1)

    @pl.when(i == 0)
    def _():
        gather(dcur_ref, 0)

    for _ in range(1 if spread else GATHER_CHUNKS):
        issue_next()

    pltpu.make_async_copy(_tile_rows(ys_ref, 0, TOP_K * TB), buf.at[slot], sems.at[slot]).wait()

    rw = rw_ref[...]
    y = jnp.zeros((TB, D), F32)
    for k in range(TOP_K):
        yk = jnp.concatenate([buf[slot, pl.ds(k * TB * SUB + s, TB, stride=SUB), :] for s in range(SUB)],
                             axis=1)
        y = y + yk * rw[:, k:k + 1]
    x2 = x1_ref[...] + mod_ref[5:6, :] * _rms(y, g_ref[...])

    @pl.when(i < N_CTX // TB)
    def _():
        oc_ref[...] = x2

    @pl.when(i >= N_CTX // TB)
    def _():
        ol_ref[...] = x2

    issue_next()
    return x2, issue_next


def _final_call(layer, dest_flat, x1, ys, rw, mods, g_post_ffn, next_pre_params=None):
    nb = N_TOK // TB
    in_specs = [pl.BlockSpec((None, 1, TOP_K * TB), lambda i: (i, 0, 0), memory_space=pltpu.SMEM),
                pl.BlockSpec((None, 1, TOP_K * TB), lambda i: (jnp.minimum(i + 1, nb - 1), 0, 0),
                             memory_space=pltpu.SMEM),
                pl.BlockSpec((TB, D), lambda i: (i, 0)),
                pl.BlockSpec(memory_space=pl.ANY),
                pl.BlockSpec((TB, LANES), lambda i: (i, 0)),
                _mod_spec(layer), _layer_spec(layer, (1, D))]
    out_specs = (_ctx_spec(D), _lat_spec(D))
    out_shape = (jax.ShapeDtypeStruct((N_CTX, D), F32), jax.ShapeDtypeStruct((N_LAT, D), F32))
    args = (dest_flat, dest_flat, x1, ys, rw, mods, g_post_ffn)
    body, name = _final_kernel, "combine_residual"
    if next_pre_params is not None:
        pre_in, pre_out, pre_shape = _pre_specs(layer + 1)
        in_specs, out_specs, out_shape = in_specs + pre_in, out_specs + pre_out, out_shape + pre_shape
        args = args + tuple(next_pre_params)
        body, name = _final_pre_kernel, "combine_pre_mix"
    outs = pl.pallas_call(
        body,
        grid=(nb,),
        in_specs=in_specs,
        out_specs=out_specs,
        out_shape=out_shape,
        scratch_shapes=[pltpu.VMEM((2, TOP_K * TB * SUB, LANES), F32), pltpu.SemaphoreType.DMA((2,))],
        compiler_params=pltpu.CompilerParams(dimension_semantics=("arbitrary",),
                                             vmem_limit_bytes=VMEM_LIMIT),
        name=name,
    )(*args)
    return outs[0], outs[1], outs[2:]


def _dft_tables(t):
    j = np.arange(t, dtype=np.int64)
    ang = 2.0 * np.pi * ((j[:, None] * j[None, :]) % t) / t
    return (np.cos(ang) / math.sqrt(t)).astype(np.float32), (np.sin(ang) / math.sqrt(t)).astype(np.float32)


def _channel_dft():
    c = np.arange(F_GROUP_DIM, dtype=np.int64)
    ang = 2.0 * np.pi * ((c[:, None] * c[None, :]) % F_GROUP_DIM) / F_GROUP_DIM
    eye = np.eye(F_GROUPS)
    bdc = np.kron(eye, np.cos(ang)) / math.sqrt(F_GROUP_DIM)
    bds = np.kron(eye, np.sin(ang)) / math.sqrt(F_GROUP_DIM)
    return bdc.astype(np.float32), bds.astype(np.float32)


def _rope_tables():
    pos = np.arange(LAT_T)
    n = ROPE // 4
    inv_freq = np.power(np.float32(ROPE_BASE), -np.arange(n, dtype=np.float32) / np.float32(n))
    ang_r = (pos // GRID_W).astype(np.float32)[:, None] * inv_freq
    ang_c = (pos % GRID_W).astype(np.float32)[:, None] * inv_freq
    cos = np.concatenate([np.cos(ang_r), np.cos(ang_r), np.cos(ang_c), np.cos(ang_c),
                          np.ones((LAT_T, LANES - ROPE))], axis=1)
    sin = np.concatenate([-np.sin(ang_r), np.sin(ang_r), -np.sin(ang_c), np.sin(ang_c),
                          np.zeros((LAT_T, LANES - ROPE))], axis=1)
    return cos.astype(np.float32), sin.astype(np.float32)


def kernel(x_prompt, x_sample, cache_ckv, cache_krope, c, c_ctx, w_ada, b_ada, g_pre_mix, g_post_mix, g_pre_ffn, g_post_ffn, w_in, g_sgu, w_spatial, b_spatial, g_q, w_uq, g_kv, w_ukv, w_out, w_router, b_router, w_gate_up, b_gate_up, w_down, b_down):
    xc, xl = x_prompt.reshape(N_CTX, D), x_sample.reshape(N_LAT, D)

    cond = jnp.concatenate([c_ctx[None, :], c, jnp.zeros((5, D), F32)], axis=0)
    mods = _mod_call(cond.T, w_ada, b_ada.reshape(DEPTH, 1, 6 * D))
    mods = mods[:, :3].reshape(DEPTH, 3, 6, D)

    gmat = jnp.asarray(np.kron(np.eye(A_HEADS), np.full((A_HEAD_DIM, A_HEAD_DIM), 1.0 / A_HEAD_DIM)),
                       dtype=BF16)
    bdc_np, bds_np = _channel_dft()
    bdc, bds = jnp.asarray(bdc_np).astype(BF16), jnp.asarray(bds_np).astype(BF16)
    dft = {}
    for t in (CTX_T, LAT_T):
        ct_np, st_np = _dft_tables(t)
        dft[t] = (jnp.asarray(ct_np).astype(BF16), jnp.asarray(st_np).astype(BF16))
    cos_np, sin_np = _rope_tables()
    cos_t, sin_t = jnp.asarray(cos_np), jnp.asarray(sin_np)
    tri = jnp.asarray(np.tril(np.ones((TB, TB), np.float32), k=-1)).astype(BF16)

    b_gu = b_gate_up.reshape(DEPTH, N_EXPERTS, 1, 2 * D_FF)
    b_dn = b_down.reshape(DEPTH, N_EXPERTS, 1, D)
    w_in_p = jnp.pad(w_in, ((0, 0), (0, 0), (0, IN_PAD - w_in.shape[-1]))).astype(BF16)
    w_uq_p = jnp.pad(w_uq.reshape(DEPTH, Q_LORA, HEADS, NOPE + ROPE),
                     ((0, 0), (0, 0), (0, 0), (0, HEAD_PAD - NOPE - ROPE)))
    w_uq_p = w_uq_p.reshape(DEPTH, Q_LORA, HEADS * HEAD_PAD).astype(BF16)
    w_ukv_b = w_ukv.astype(BF16)
    ws = w_spatial.astype(BF16)
    bs_full = jnp.repeat(jnp.swapaxes(b_spatial, 1, 2), A_HEAD_DIM, axis=2)
    ckr_p = jnp.pad(cache_krope, ((0, 0), (0, 0), (0, 0), (0, LANES - ROPE))).astype(BF16)
    w_out_b = w_out.astype(BF16)
    w_r = jnp.pad(w_router, ((0, 0), (0, 0), (0, LANES - N_EXPERTS)))
    b_r = jnp.pad(b_router, ((0, 0), (0, LANES - N_EXPERTS)), constant_values=NEG)[:, None, :]
    row = lambda g: g[:, None, :]

    pre_params = (mods, row(g_pre_mix), w_in_p, row(g_sgu), gmat, row(g_q), w_uq_p, row(g_kv), w_ukv_b,
                  cos_t, sin_t)

    ckv_layers, krope_layers = [], []
    pre = _pre_call(0, xc, xl, pre_params)
    for i in range(DEPTH):
        u, vn, zf, q, kv, kr, ckv, zkr = pre
        ckv_layers.append(ckv[:N_CTX].reshape(N_CTX_B, CTX_T, KV_LORA))
        krope_layers.append(zkr[:N_CTX, :ROPE].reshape(N_CTX_B, CTX_T, ROPE))

        mix_in = (u, vn, zf, q, kv, kr)
        y_ctx = _mix_call(i, mix_in, (ws, bs_full, bdc, bds) + dft[CTX_T], CTX_T, N_CTX_B, 0)
        y_lat = _mix_call(i, mix_in, (ws, bs_full, bdc, bds) + dft[LAT_T], LAT_T, N_LAT_B, N_CTX,
                          cache=(cache_ckv, ckr_p, w_ukv_b))

        x1, h2, tab, rw, cnt = _post_call(i, y_ctx, y_lat, xc, xl, mods, w_out_b,
                                          row(g_post_mix), row(g_pre_ffn), w_r, b_r, tri)

        dest, zrow, n_used, block_e, first, nxt, nxt2, slot = _plan_call(cnt, tab)
        dest_flat = dest.reshape(N_TOK // TB, 1, TOP_K * TB)
        xs = _dispatch_call(zrow, n_used, dest_flat, h2)
        ys = _moe_call(i, (block_e, first, nxt, nxt2, slot, n_used), xs, w_gate_up, b_gu, w_down, b_dn)
        xc, xl, pre = _final_call(i, dest_flat, x1, ys, rw, mods, row(g_post_ffn),
                                  next_pre_params=pre_params if i + 1 < DEPTH else None)

    y_prompt = xc.reshape(N_CTX_B, CTX_T, D)
    y_sample = xl.reshape(N_LAT_B, LAT_T, D)
    return (y_prompt, y_sample, jnp.stack(ckv_layers, axis=1), jnp.stack(krope_layers, axis=1))
```

```python
import functools
import math

import jax
import jax.numpy as jnp
import numpy as np
from jax import lax
from jax.experimental import pallas as pl
from jax.experimental.pallas import tpu as pltpu

F32 = jnp.float32
BF16 = jnp.bfloat16

D = 1024
N_CTX_B, CTX_T = 16, 256
N_LAT_B, LAT_T = 2, 1024
PAST = 512
N_CTX = N_CTX_B * CTX_T
N_LAT = N_LAT_B * LAT_T
N_TOK = N_CTX + N_LAT
DEPTH = 2
GRID_W = 64
EPS = 1e-6
A_HEADS, A_HEAD_DIM, A_WIDTH, CHUNK = 4, 64, 256, 128
F_GROUPS, F_GROUP_DIM, F_WIDTH = 4, 64, 256
HEADS, Q_LORA, KV_LORA, NOPE, ROPE, V_DIM = 4, 256, 128, 128, 64, 128
HEAD_PAD = 256
IN_PAD = 1280
N_EXPERTS, TOP_K, D_FF = 32, 4, 1024
SWIGLU_LIMIT, SWIGLU_ALPHA = 7.0, 1.702
ROPE_BASE = 10000.0

TB = 512
QB = 256
RB = 256
W_SLOTS = 3
LANES = 128
SUB = D // LANES
N_BLOCKS = N_TOK * TOP_K // RB + N_EXPERTS
N_ROWS = N_BLOCKS * RB
NEG = -3.0e38
VMEM_LIMIT = 56 * 1024 * 1024


def _rms(x, g):
    return x * lax.rsqrt(jnp.mean(x * x, axis=-1, keepdims=True) + EPS) * g


def _split_dot(v, m):
    hi = v.astype(BF16)
    lo = (v - hi.astype(F32)).astype(BF16)
    return (jnp.dot(hi, m, preferred_element_type=F32)
            + jnp.dot(lo, m, preferred_element_type=F32))


def _dot_nt(a, b):
    return lax.dot_general(a, b, (((1,), (1,)), ((), ())), preferred_element_type=F32)


def _store_tiled(ref, x):
    rows = x.shape[0]
    for s in range(SUB):
        ref[pl.ds(s, rows, stride=SUB), :] = x[:, s * LANES:(s + 1) * LANES]


def _load_tiled(ref, rows):
    return jnp.concatenate([ref[pl.ds(s, rows, stride=SUB), :] for s in range(SUB)], axis=1)


def _tile_rows(ref, row, n_rows=1):
    return ref.at[pl.ds(pl.multiple_of(row * SUB, SUB), n_rows * SUB)]


def _mod_kernel(ct_ref, w_ref, b_ref, o_ref):
    ct = ct_ref[...]
    s = ct * jax.nn.sigmoid(ct)
    w = w_ref[...]
    o_ref[...] = jnp.zeros(o_ref.shape, F32)
    for r in range(3):
        o_ref[r:r + 1, :] = jnp.sum(w * s[:, r:r + 1], axis=0, keepdims=True) + b_ref[...]


def _mod_call(cond_t, w_ada, b_ada):
    cb = 2048
    return pl.pallas_call(
        _mod_kernel,
        grid=(DEPTH, 6 * D // cb),
        in_specs=[
            pl.BlockSpec((D, 8), lambda l, j: (0, 0)),
            pl.BlockSpec((None, D, cb), lambda l, j: (l, 0, j)),
            pl.BlockSpec((None, 1, cb), lambda l, j: (l, 0, j)),
        ],
        out_specs=pl.BlockSpec((None, 8, cb), lambda l, j: (l, 0, j)),
        out_shape=jax.ShapeDtypeStruct((DEPTH, 8, 6 * D), F32),
        compiler_params=pltpu.CompilerParams(dimension_semantics=("parallel", "parallel")),
        name="modulation",
    )(cond_t, w_ada, b_ada)


def _mod_index(i):
    first_lat = N_CTX // TB
    return jnp.where(i < first_lat, 0, 1 + (i - first_lat) // (LAT_T // TB))


def _ctx_spec(width):
    return pl.BlockSpec((TB, width), lambda i: (jnp.minimum(i, N_CTX // TB - 1), 0))


def _lat_spec(width):
    return pl.BlockSpec((TB, width), lambda i: (jnp.maximum(i - N_CTX // TB, 0), 0))


def _pick(i, ctx_ref, lat_ref):
    return jnp.where(i >= N_CTX // TB, lat_ref[...], ctx_ref[...])


def _layer_spec(layer, shape):
    zeros = (0,) * len(shape)
    return pl.BlockSpec((None,) + tuple(shape), lambda *_: (layer,) + zeros)


def _mod_spec(layer):
    return pl.BlockSpec((None, None, 6, D), lambda i: (layer, _mod_index(i), 0, 0))


def _swap_halves(x, lane):
    w = x.shape[-1]
    fwd = pltpu.roll(x, w - 16, 1)
    bwd = pltpu.roll(x, 16, 1)
    return jnp.where((lane & 31) < 16, fwd, bwd)


def _pre_kernel(xc_ref, xl_ref, *refs):
    i = pl.program_id(0)
    _pre_body(i, _pick(i, xc_ref, xl_ref), *refs)


def _pre_body(i, x, mod_ref, gpre_ref, win_ref, gsgu_ref, gmat_ref, gq_ref, wuq_ref,
              gkv_ref, wukv_ref, cos_ref, sin_ref,
              u_ref, vn_ref, zf_ref, q_ref, kv_ref, kr_ref, ckv_ref, zkr_ref):
    is_lat = i >= N_CTX // TB
    h = _rms(x, gpre_ref[...]) * (1.0 + mod_ref[1:2, :]) + mod_ref[0:1, :]
    z = jnp.dot(h.astype(BF16), win_ref[...], preferred_element_type=F32)

    ga = jax.nn.gelu(z[:, :2 * A_WIDTH])
    u_ref[...] = ga[:, :A_WIDTH]
    v = ga[:, A_WIDTH:]
    gmat = gmat_ref[...]
    dv = v - _split_dot(v, gmat)
    var = _split_dot(dv * dv, gmat)
    vn_ref[...] = (dv * lax.rsqrt(var + EPS) * gsgu_ref[...]).astype(BF16)

    zf_ref[...] = z[:, 512:768].astype(BF16)

    cos = jnp.where(is_lat, cos_ref[...], 1.0)
    sin = jnp.where(is_lat, sin_ref[...], 0.0)
    lane = lax.broadcasted_iota(jnp.int32, (TB, LANES), 1)

    qn = _rms(z[:, 768:1024], gq_ref[...])
    scale = (NOPE + ROPE) ** -0.5
    q = jnp.dot(qn.astype(BF16), wuq_ref[...], preferred_element_type=F32) * scale
    for hd in range(HEADS):
        base = hd * HEAD_PAD
        q_ref[:, base:base + NOPE] = q[:, base:base + NOPE].astype(BF16)
        qr = q[:, base + NOPE:base + HEAD_PAD]
        q_ref[:, base + NOPE:base + HEAD_PAD] = (qr * cos + _swap_halves(qr, lane) * sin).astype(BF16)

    ckv = _rms(z[:, 1024:1152], gkv_ref[...])
    ckv_ref[...] = ckv
    kv_ref[...] = jnp.dot(ckv.astype(BF16), wukv_ref[...], preferred_element_type=F32).astype(BF16)

    zkr = z[:, 1152:1280]
    zkr_ref[...] = zkr
    kr_ref[...] = (zkr * cos + _swap_halves(zkr, lane) * sin).astype(BF16)


def _pre_specs(layer):
    lay = functools.partial(_layer_spec, layer)
    first_lat = N_CTX // TB
    pos_blocks = LAT_T // TB

    def tok(width):
        return pl.BlockSpec((TB, width), lambda i: (i, 0))

    def full(shape):
        return pl.BlockSpec(shape, lambda i: (0,) * len(shape))

    def rope_map(i):
        return (jnp.where(i >= first_lat, (i - first_lat) % pos_blocks, 0), 0)

    in_specs = [
        _mod_spec(layer),
        lay((1, D)), lay((D, IN_PAD)), lay((1, A_WIDTH)), full((A_WIDTH, A_WIDTH)),
        lay((1, Q_LORA)), lay((Q_LORA, HEADS * HEAD_PAD)),
        lay((1, KV_LORA)), lay((KV_LORA, HEADS * (NOPE + V_DIM))),
        pl.BlockSpec((TB, LANES), rope_map), pl.BlockSpec((TB, LANES), rope_map),
    ]
    out_specs = (tok(A_WIDTH), tok(A_WIDTH), tok(F_WIDTH), tok(HEADS * HEAD_PAD),
                 tok(HEADS * (NOPE + V_DIM)), tok(LANES), tok(KV_LORA), tok(LANES))
    out_shape = (
        jax.ShapeDtypeStruct((N_TOK, A_WIDTH), F32),
        jax.ShapeDtypeStruct((N_TOK, A_WIDTH), BF16),
        jax.ShapeDtypeStruct((N_TOK, F_WIDTH), BF16),
        jax.ShapeDtypeStruct((N_TOK, HEADS * HEAD_PAD), BF16),
        jax.ShapeDtypeStruct((N_TOK, HEADS * (NOPE + V_DIM)), BF16),
        jax.ShapeDtypeStruct((N_TOK, LANES), BF16),
        jax.ShapeDtypeStruct((N_TOK, KV_LORA), F32),
        jax.ShapeDtypeStruct((N_TOK, LANES), F32),
    )
    return in_specs, out_specs, out_shape


def _pre_call(layer, xc, xl, pre_params):
    in_specs, out_specs, out_shape = _pre_specs(layer)
    return pl.pallas_call(
        _pre_kernel,
        grid=(N_TOK // TB,),
        in_specs=[_ctx_spec(D), _lat_spec(D)] + in_specs,
        out_specs=out_specs,
        out_shape=out_shape,
        compiler_params=pltpu.CompilerParams(dimension_semantics=("parallel",),
                                             vmem_limit_bytes=VMEM_LIMIT),
        name="pre_mix",
    )(xc, xl, *pre_params)


def _mix_kernel(*refs, has_cache):
    if has_cache:
        (u_ref, vn_ref, zf_ref, q_ref, kv_ref, kr_ref, ws_ref, bs_ref, bdc_ref, bds_ref,
         ct_ref, st_ref, cckv_ref, ckr_ref, wukv_ref, o_ref) = refs
    else:
        (u_ref, vn_ref, zf_ref, q_ref, kv_ref, kr_ref, ws_ref, bs_ref, bdc_ref, bds_ref,
         ct_ref, st_ref, o_ref) = refs

    lane = lax.broadcasted_iota(jnp.int32, (CHUNK, A_WIDTH), 1)
    for c in range(QB // CHUNK):
        rows = slice(c * CHUNK, (c + 1) * CHUNK)
        vch = vn_ref[rows, :]
        s = bs_ref[...]
        for g in range(A_HEADS):
            sg = jnp.dot(ws_ref[g], vch, preferred_element_type=F32)
            in_head = (lane >= g * A_HEAD_DIM) & (lane < (g + 1) * A_HEAD_DIM)
            s = s + jnp.where(in_head, sg, 0.0)
        o_ref[rows, 0:A_WIDTH] = (u_ref[rows, :] * s).astype(BF16)

    zf = zf_ref[...]
    zc = jnp.dot(zf, bdc_ref[...], preferred_element_type=F32).astype(BF16)
    zs = jnp.dot(zf, bds_ref[...], preferred_element_type=F32).astype(BF16)
    yf = (jnp.dot(ct_ref[...], zc, preferred_element_type=F32)
          - jnp.dot(st_ref[...], zs, preferred_element_type=F32))
    o_ref[:, A_WIDTH:A_WIDTH + F_WIDTH] = yf.astype(BF16)

    kr = kr_ref[...]
    if has_cache:
        kvc = jnp.dot(cckv_ref[...].astype(BF16), wukv_ref[...],
                      preferred_element_type=F32).astype(BF16)
        krc = ckr_ref[...]
    for hd in range(HEADS):
        qh = q_ref[:, hd * HEAD_PAD:(hd + 1) * HEAD_PAD]
        kb = hd * (NOPE + V_DIM)
        kh = jnp.concatenate([kv_ref[:, kb:kb + NOPE], kr], axis=1)
        vh = kv_ref[:, kb + NOPE:kb + NOPE + V_DIM]
        s = _dot_nt(qh, kh)
        m = jnp.max(s, axis=-1, keepdims=True)
        if has_cache:
            khc = jnp.concatenate([kvc[:, kb:kb + NOPE], krc], axis=1)
            vhc = kvc[:, kb + NOPE:kb + NOPE + V_DIM]
            sc = _dot_nt(qh, khc)
            m = jnp.maximum(m, jnp.max(sc, axis=-1, keepdims=True))
        e = jnp.exp(s - m)
        den = jnp.sum(e, axis=-1, keepdims=True)
        o = jnp.dot(e.astype(BF16), vh, preferred_element_type=F32)
        if has_cache:
            ec = jnp.exp(sc - m)
            den = den + jnp.sum(ec, axis=-1, keepdims=True)
            o = o + jnp.dot(ec.astype(BF16), vhc, preferred_element_type=F32)
        ob = A_WIDTH + F_WIDTH + hd * V_DIM
        o_ref[:, ob:ob + V_DIM] = (o * (1.0 / den)).astype(BF16)


def _mix_call(layer, pre, consts, seq_t, n_batch, tok_off, cache=None):
    u, vn, zf, q, kv, kr = pre
    ws, bs_full, bdc, bds, ct, st = consts
    lay = functools.partial(_layer_spec, layer)
    nq = seq_t // QB
    qoff = tok_off // QB
    soff = tok_off // seq_t

    def qrow(width):
        return pl.BlockSpec((QB, width), lambda b, j: (qoff + b * nq + j, 0))

    def srow(width):
        return pl.BlockSpec((seq_t, width), lambda b, j: (soff + b, 0))

    def full(shape):
        return pl.BlockSpec(shape, lambda b, j: (0,) * len(shape))

    in_specs = [qrow(A_WIDTH), qrow(A_WIDTH), srow(F_WIDTH), qrow(HEADS * HEAD_PAD),
                srow(HEADS * (NOPE + V_DIM)), srow(LANES),
                lay((A_HEADS, CHUNK, CHUNK)), lay((CHUNK, A_WIDTH)),
                full((F_WIDTH, F_WIDTH)), full((F_WIDTH, F_WIDTH)),
                pl.BlockSpec((QB, seq_t), lambda b, j: (j, 0)),
                pl.BlockSpec((QB, seq_t), lambda b, j: (j, 0))]
    args = [u, vn, zf, q, kv, kr, ws, bs_full, bdc, bds, ct, st]
    if cache is not None:
        cckv, ckr, wukv = cache
        in_specs += [pl.BlockSpec((None, None, PAST, KV_LORA), lambda b, j: (b, layer, 0, 0)),
                     pl.BlockSpec((None, None, PAST, LANES), lambda b, j: (b, layer, 0, 0)),
                     lay((KV_LORA, HEADS * (NOPE + V_DIM)))]
        args += [cckv, ckr, wukv]
    return pl.pallas_call(
        functools.partial(_mix_kernel, has_cache=cache is not None),
        grid=(n_batch, nq),
        in_specs=in_specs,
        out_specs=pl.BlockSpec((QB, D), lambda b, j: (b * nq + j, 0)),
        out_shape=jax.ShapeDtypeStruct((n_batch * seq_t, D), BF16),
        compiler_params=pltpu.CompilerParams(dimension_semantics=("parallel", "parallel"),
                                             vmem_limit_bytes=VMEM_LIMIT),
        name="mix_lat" if cache is not None else "mix_ctx",
    )(*args)


def _post_kernel(yc_ref, yl_ref, xc_ref, xl_ref, mod_ref, wout_ref, gpost_ref, gffn_ref, wr_ref, br_ref,
                 tri_ref, x1_ref, h2_ref, tab_ref, rw_ref, cnt_ref, carry_ref):
    i = pl.program_id(0)

    @pl.when(i == 0)
    def _():
        carry_ref[...] = jnp.zeros(carry_ref.shape, F32)

    y = jnp.dot(_pick(i, yc_ref, yl_ref), wout_ref[...], preferred_element_type=F32)
    x1 = _pick(i, xc_ref, xl_ref) + mod_ref[2:3, :] * _rms(y, gpost_ref[...])
    x1_ref[...] = x1
    h2 = _rms(x1, gffn_ref[...]) * (1.0 + mod_ref[4:5, :]) + mod_ref[3:4, :]
    _store_tiled(h2_ref, h2)

    wr = wr_ref[...]
    wr_hi = wr.astype(BF16)
    wr_lo = (wr - wr_hi.astype(F32)).astype(BF16)
    h_hi = h2.astype(BF16)
    h_lo = (h2 - h_hi.astype(F32)).astype(BF16)
    logits = (jnp.dot(h_hi, wr_hi, preferred_element_type=F32)
              + jnp.dot(h_lo, wr_hi, preferred_element_type=F32)
              + jnp.dot(h_hi, wr_lo, preferred_element_type=F32)) + br_ref[...]

    lane = lax.broadcasted_iota(jnp.int32, (TB, LANES), 1)
    lane_f = lane.astype(F32)
    work = logits
    idx, val = [], []
    for _ in range(TOP_K):
        m = jnp.max(work, axis=-1, keepdims=True)
        ik = jnp.min(jnp.where(work == m, lane_f, float(LANES)), axis=-1, keepdims=True)
        idx.append(ik)
        val.append(m)
        work = jnp.where(lane_f == ik, NEG, work)
    ex = [jnp.exp(v - val[0]) for v in val]
    den = ex[0] + ex[1] + ex[2] + ex[3]

    onehot = jnp.zeros((TB, LANES), F32)
    for k in range(TOP_K):
        onehot = onehot + jnp.where(lane_f == idx[k], 1.0, 0.0)
    before = jnp.dot(tri_ref[...], onehot.astype(BF16), preferred_element_type=F32) + carry_ref[0:1, :]
    ri = jnp.zeros((TB, LANES), F32)
    rw = jnp.zeros((TB, LANES), F32)
    for k in range(TOP_K):
        rank_k = jnp.sum(jnp.where(lane_f == idx[k], before, 0.0), axis=-1, keepdims=True)
        ri = ri + jnp.where(lane == k, idx[k], 0.0) + jnp.where(lane == TOP_K + k, rank_k, 0.0)
        rw = rw + jnp.where(lane == k, ex[k] / den, 0.0)
    tab_ref[...] = ri.T[0:2 * TOP_K, :].astype(jnp.int32)
    rw_ref[...] = rw
    total = carry_ref[0:1, :] + jnp.sum(onehot, axis=0, keepdims=True)
    carry_ref[...] = jnp.broadcast_to(total, carry_ref.shape)
    cnt_ref[...] = jnp.broadcast_to(total, cnt_ref.shape).astype(jnp.int32)


def _post_call(layer, yc, yl, xc, xl, mods, w_out, g_post, g_ffn, w_r, b_r, tri):
    nb = N_TOK // TB
    lay = functools.partial(_layer_spec, layer)

    def tok(width):
        return pl.BlockSpec((TB, width), lambda i: (i, 0))

    def full(shape):
        return pl.BlockSpec(shape, lambda i: (0,) * len(shape))

    return pl.pallas_call(
        _post_kernel,
        grid=(nb,),
        in_specs=[_ctx_spec(D), _lat_spec(D), _ctx_spec(D), _lat_spec(D), _mod_spec(layer),
                  lay((D, D)), lay((1, D)), lay((1, D)), lay((D, LANES)), lay((1, LANES)),
                  full((TB, TB))],
        out_specs=(tok(D), pl.BlockSpec((TB * SUB, LANES), lambda i: (i, 0)),
                   pl.BlockSpec((None, 2 * TOP_K, TB), lambda i: (i, 0, 0)),
                   tok(LANES), full((8, LANES))),
        out_shape=(jax.ShapeDtypeStruct((N_TOK, D), F32),
                   jax.ShapeDtypeStruct((N_TOK * SUB, LANES), F32),
                   jax.ShapeDtypeStruct((nb, 2 * TOP_K, TB), jnp.int32),
                   jax.ShapeDtypeStruct((N_TOK, LANES), F32),
                   jax.ShapeDtypeStruct((8, LANES), jnp.int32)),
        scratch_shapes=[pltpu.VMEM((8, LANES), F32)],
        compiler_params=pltpu.CompilerParams(dimension_semantics=("arbitrary",),
                                             vmem_limit_bytes=VMEM_LIMIT),
        name="post_mix_router",
    )(yc, yl, xc, xl, mods, w_out, g_post, g_ffn, w_r, b_r, tri)


def _plan_kernel(cnt_ref, tab_ref, dest_ref, zrow_ref, nu_ref, be_ref, first_ref, nxt_ref, nxt2_ref,
                 slot_ref, start_s, next_s):
    nxt = jnp.int32(-1)
    for e in reversed(range(N_EXPERTS)):
        next_s[e] = nxt
        nxt = jnp.where(cnt_ref[0, e] > 0, jnp.int32(e), nxt)

    start = jnp.int32(0)
    run = jnp.int32(0)
    last = jnp.int32(0)
    for e in range(N_EXPERTS):
        c = cnt_ref[0, e]
        n_blk = lax.shift_right_logical(c + (RB - 1), RB.bit_length() - 1)
        start_s[e] = start
        zrow_ref[e] = start * RB + lax.shift_left(lax.shift_right_logical(c, 3), 3)
        n1 = next_s[e]
        n2 = jnp.where(n1 >= 0, next_s[jnp.maximum(n1, 0)], jnp.int32(-1))
        slot = lax.rem(run, W_SLOTS)

        def fill(j, carry, e=e, n1=n1, n2=n2, slot=slot, start=start):
            be_ref[j] = jnp.int32(e)
            first_ref[j] = (j == start).astype(jnp.int32)
            nxt_ref[j] = n1
            nxt2_ref[j] = n2
            slot_ref[j] = slot
            return carry

        lax.fori_loop(start, start + n_blk, fill, 0)
        owns = n_blk > 0
        last = jnp.where(owns, jnp.int32(e), last)
        run = run + owns.astype(jnp.int32)
        start = start + n_blk
    nu_ref[0] = start

    def tail(j, carry):
        be_ref[j] = last
        first_ref[j] = jnp.int32(0)
        nxt_ref[j] = jnp.int32(-1)
        nxt2_ref[j] = jnp.int32(-1)
        slot_ref[j] = jnp.int32(0)
        return carry

    lax.fori_loop(start, N_BLOCKS, tail, 0)

    idx = tab_ref[:, 0:TOP_K, :]
    base = jnp.zeros(idx.shape, jnp.int32)
    for e in range(N_EXPERTS):
        base = jnp.where(idx == e, start_s[e] * RB, base)
    dest_ref[...] = base + tab_ref[:, TOP_K:2 * TOP_K, :]


def _plan_call(cnt, tab):
    nb = N_TOK // TB
    smem = pl.BlockSpec(memory_space=pltpu.SMEM)
    blocks = jax.ShapeDtypeStruct((N_BLOCKS,), jnp.int32)
    return pl.pallas_call(
        _plan_kernel,
        in_specs=[smem, pl.BlockSpec(memory_space=pltpu.VMEM)],
        out_specs=(pl.BlockSpec(memory_space=pltpu.VMEM), smem, smem, smem, smem, smem, smem, smem),
        out_shape=(jax.ShapeDtypeStruct((nb, TOP_K, TB), jnp.int32),
                   jax.ShapeDtypeStruct((N_EXPERTS,), jnp.int32),
                   jax.ShapeDtypeStruct((1,), jnp.int32),
                   blocks, blocks, blocks, blocks, blocks),
        scratch_shapes=[pltpu.SMEM((N_EXPERTS,), jnp.int32), pltpu.SMEM((N_EXPERTS,), jnp.int32)],
        name="plan_rows",
    )(cnt, tab)


def _dispatch_kernel(zrow_ref, nu_ref, dest_ref, h2_ref, h2_hbm, xs_ref, zero_ref, sem, zsem):
    i = pl.program_id(0)

    def zero_fill(row):
        return pltpu.make_async_copy(zero_ref, _tile_rows(xs_ref, row, RB), zsem)

    @pl.when(i == 0)
    def _():
        zero_ref[...] = jnp.zeros(zero_ref.shape, F32)
        for e in range(N_EXPERTS):
            zero_fill(zrow_ref[e]).start()
        for e in range(N_EXPERTS):
            zero_fill(zrow_ref[e]).wait()

        def tail_start(b, carry):
            zero_fill(b * RB).start()
            return carry

        def tail_wait(b, carry):
            zero_fill(b * RB).wait()
            return carry

        lax.fori_loop(nu_ref[0], N_BLOCKS, tail_start, 0)
        lax.fori_loop(nu_ref[0], N_BLOCKS, tail_wait, 0)

    def body(t, carry):
        for k in range(TOP_K):
            src = _tile_rows(h2_ref, t) if k < TOP_K // 2 else _tile_rows(h2_hbm, i * TB + t)
            pltpu.make_async_copy(src, _tile_rows(xs_ref, dest_ref[0, k * TB + t]),
                                  sem).start(priority=k % 2)
        return carry

    lax.fori_loop(0, TB, body, 0, unroll=8)
    for k in range(TOP_K):
        pltpu.make_async_copy(h2_ref, _tile_rows(xs_ref, 0, TB), sem).wait()


def _dispatch_call(zrow, n_used, dest, h2):
    nb = N_TOK // TB
    grid_spec = pltpu.PrefetchScalarGridSpec(
        num_scalar_prefetch=2,
        grid=(nb,),
        in_specs=[
            pl.BlockSpec((None, 1, TOP_K * TB), lambda i, z, n: (i, 0, 0), memory_space=pltpu.SMEM),
            pl.BlockSpec((TB * SUB, LANES), lambda i, z, n: (i, 0)),
            pl.BlockSpec(memory_space=pl.ANY),
        ],
        out_specs=pl.BlockSpec(memory_space=pl.ANY),
        scratch_shapes=[pltpu.VMEM((RB * SUB, LANES), F32), pltpu.SemaphoreType.DMA,
                        pltpu.SemaphoreType.DMA],
    )
    return pl.pallas_call(
        _dispatch_kernel,
        grid_spec=grid_spec,
        out_shape=jax.ShapeDtypeStruct((N_ROWS * SUB, LANES), F32),
        compiler_params=pltpu.CompilerParams(dimension_semantics=("arbitrary",),
                                             vmem_limit_bytes=VMEM_LIMIT),
        name="dispatch_rows",
    )(zrow, n_used, dest, h2, h2)


def _moe_kernel(be_ref, first_ref, nxt_ref, nxt2_ref, slot_ref, nu_ref, xs_ref, wgu_hbm, bgu_ref, wdn_hbm,
                bdn_ref, o_ref, wgu_f, wdn_f, sems, *, layer):
    b = pl.program_id(0)
    used = b < nu_ref[0]

    def fetch(e, s):
        return (pltpu.make_async_copy(wgu_hbm.at[layer, e], wgu_f.at[s], sems.at[0, s]),
                pltpu.make_async_copy(wdn_hbm.at[layer, e], wdn_f.at[s], sems.at[1, s]))

    @pl.when(b == 0)
    def _():
        for cp in fetch(be_ref[0], 0):
            cp.start(priority=1)

        @pl.when(nxt_ref[0] >= 0)
        def _():
            for cp in fetch(nxt_ref[0], 1):
                cp.start(priority=1)

    @pl.when(first_ref[b] == 1)
    def _():
        s = slot_ref[b]
        for cp in fetch(be_ref[b], s):
            cp.wait()

        @pl.when(nxt2_ref[b] >= 0)
        def _():
            for cp in fetch(nxt2_ref[b], (s + 2) % W_SLOTS):
                cp.start(priority=1)

    @pl.when(used)
    def _():
        s = slot_ref[b]
        x = _load_tiled(xs_ref, RB).astype(BF16)
        gu = jnp.dot(x, wgu_f[s].astype(BF16), preferred_element_type=F32) + bgu_ref[...]
        g = jnp.minimum(gu[:, :D_FF], SWIGLU_LIMIT)
        l = jnp.clip(gu[:, D_FF:], -SWIGLU_LIMIT, SWIGLU_LIMIT)
        a = g * jax.nn.sigmoid(SWIGLU_ALPHA * g) * (l + 1.0)
        y = jnp.dot(a.astype(BF16), wdn_f[s].astype(BF16), preferred_element_type=F32) + bdn_ref[...]
        _store_tiled(o_ref, y)

    @pl.when(jnp.logical_not(used))
    def _():
        o_ref[...] = jnp.zeros(o_ref.shape, o_ref.dtype)


def _moe_call(layer, tables, xs, w_gu, b_gu, w_dn, b_dn):
    def rows_in(b, be, fi, nx, n2, sl, nu):
        return (jnp.minimum(b, nu[0] - 1), 0)

    def rows_out(b, be, fi, nx, n2, sl, nu):
        return (b, 0)

    def expert(b, be, fi, nx, n2, sl, nu):
        return (layer, be[b], 0, 0)

    grid_spec = pltpu.PrefetchScalarGridSpec(
        num_scalar_prefetch=6,
        grid=(N_BLOCKS,),
        in_specs=[
            pl.BlockSpec((RB * SUB, LANES), rows_in),
            pl.BlockSpec(memory_space=pl.ANY),
            pl.BlockSpec((None, None, 1, 2 * D_FF), expert),
            pl.BlockSpec(memory_space=pl.ANY),
            pl.BlockSpec((None, None, 1, D), expert),
        ],
        out_specs=pl.BlockSpec((RB * SUB, LANES), rows_out),
        scratch_shapes=[pltpu.VMEM((W_SLOTS, D, 2 * D_FF), F32), pltpu.VMEM((W_SLOTS, D_FF, D), F32),
                        pltpu.SemaphoreType.DMA((2, W_SLOTS))],
    )
    return pl.pallas_call(
        functools.partial(_moe_kernel, layer=layer),
        grid_spec=grid_spec,
        out_shape=jax.ShapeDtypeStruct((N_ROWS * SUB, LANES), F32),
        compiler_params=pltpu.CompilerParams(dimension_semantics=("arbitrary",),
                                             vmem_limit_bytes=VMEM_LIMIT),
        name="moe_experts",
    )(*tables, xs, w_gu, b_gu, w_dn, b_dn)


def _final_kernel(*refs):
    _combine_body(pl.program_id(0), *refs)


def _combine_body(i, dcur_ref, dnxt_ref, x1_ref, ys_ref, rw_ref, mod_ref, g_ref, oc_ref, ol_ref, buf, sems):
    nb = pl.num_programs(0)
    slot = i % 2

    def gather(dest_ref, s):
        def body(t, carry):
            for k in range(TOP_K):
                pltpu.make_async_copy(_tile_rows(ys_ref, dest_ref[0, k * TB + t]),
                                      _tile_rows(buf.at[s], k * TB + t), sems.at[s]).start(priority=k % 2)
            return carry

        lax.fori_loop(0, TB, body, 0, unroll=8)

    @pl.when(i == 0)
    def _():
        gather(dcur_ref, 0)

    @pl.when(i + 1 < nb)
    def _():
        gather(dnxt_ref, 1 - slot)

    pltpu.make_async_copy(_tile_rows(ys_ref, 0, TOP_K * TB), buf.at[slot], sems.at[slot]).wait()

    rw = rw_ref[...]
    y = jnp.zeros((TB, D), F32)
    for k in range(TOP_K):
        yk = jnp.concatenate([buf[slot, pl.ds(k * TB * SUB + s, TB, stride=SUB), :] for s in range(SUB)],
                             axis=1)
        y = y + yk * rw[:, k:k + 1]
    x2 = x1_ref[...] + mod_ref[5:6, :] * _rms(y, g_ref[...])

    @pl.when(i < N_CTX // TB)
    def _():
        oc_ref[...] = x2

    @pl.when(i >= N_CTX // TB)
    def _():
        ol_ref[...] = x2


def _final_call(layer, dest_flat, x1, ys, rw, mods, g_post_ffn):
    nb = N_TOK // TB
    return pl.pallas_call(
        _final_kernel,
        grid=(nb,),
        in_specs=[pl.BlockSpec((None, 1, TOP_K * TB), lambda i: (i, 0, 0), memory_space=pltpu.SMEM),
                  pl.BlockSpec((None, 1, TOP_K * TB), lambda i: (jnp.minimum(i + 1, nb - 1), 0, 0),
                               memory_space=pltpu.SMEM),
                  pl.BlockSpec((TB, D), lambda i: (i, 0)),
                  pl.BlockSpec(memory_space=pl.ANY),
                  pl.BlockSpec((TB, LANES), lambda i: (i, 0)),
                  _mod_spec(layer), _layer_spec(layer, (1, D))],
        out_specs=(_ctx_spec(D), _lat_spec(D)),
        out_shape=(jax.ShapeDtypeStruct((N_CTX, D), F32), jax.ShapeDtypeStruct((N_LAT, D), F32)),
        scratch_shapes=[pltpu.VMEM((2, TOP_K * TB * SUB, LANES), F32), pltpu.SemaphoreType.DMA((2,))],
        compiler_params=pltpu.CompilerParams(dimension_semantics=("arbitrary",),
                                             vmem_limit_bytes=VMEM_LIMIT),
        name="combine_residual",
    )(dest_flat, dest_flat, x1, ys, rw, mods, g_post_ffn)


def _dft_tables(t):
    j = np.arange(t, dtype=np.int64)
    ang = 2.0 * np.pi * ((j[:, None] * j[None, :]) % t) / t
    return (np.cos(ang) / math.sqrt(t)).astype(np.float32), (np.sin(ang) / math.sqrt(t)).astype(np.float32)


def _channel_dft():
    c = np.arange(F_GROUP_DIM, dtype=np.int64)
    ang = 2.0 * np.pi * ((c[:, None] * c[None, :]) % F_GROUP_DIM) / F_GROUP_DIM
    eye = np.eye(F_GROUPS)
    bdc = np.kron(eye, np.cos(ang)) / math.sqrt(F_GROUP_DIM)
    bds = np.kron(eye, np.sin(ang)) / math.sqrt(F_GROUP_DIM)
    return bdc.astype(np.float32), bds.astype(np.float32)


def _rope_tables():
    pos = np.arange(LAT_T)
    n = ROPE // 4
    inv_freq = np.power(np.float32(ROPE_BASE), -np.arange(n, dtype=np.float32) / np.float32(n))
    ang_r = (pos // GRID_W).astype(np.float32)[:, None] * inv_freq
    ang_c = (pos % GRID_W).astype(np.float32)[:, None] * inv_freq
    cos = np.concatenate([np.cos(ang_r), np.cos(ang_r), np.cos(ang_c), np.cos(ang_c),
                          np.ones((LAT_T, LANES - ROPE))], axis=1)
    sin = np.concatenate([-np.sin(ang_r), np.sin(ang_r), -np.sin(ang_c), np.sin(ang_c),
                          np.zeros((LAT_T, LANES - ROPE))], axis=1)
    return cos.astype(np.float32), sin.astype(np.float32)


def kernel(x_prompt, x_sample, cache_ckv, cache_krope, c, c_ctx, w_ada, b_ada, g_pre_mix, g_post_mix, g_pre_ffn, g_post_ffn, w_in, g_sgu, w_spatial, b_spatial, g_q, w_uq, g_kv, w_ukv, w_out, w_router, b_router, w_gate_up, b_gate_up, w_down, b_down):
    xc, xl = x_prompt.reshape(N_CTX, D), x_sample.reshape(N_LAT, D)

    cond = jnp.concatenate([c_ctx[None, :], c, jnp.zeros((5, D), F32)], axis=0)
    mods = _mod_call(cond.T, w_ada, b_ada.reshape(DEPTH, 1, 6 * D))
    mods = mods[:, :3].reshape(DEPTH, 3, 6, D)

    gmat = jnp.asarray(np.kron(np.eye(A_HEADS), np.full((A_HEAD_DIM, A_HEAD_DIM), 1.0 / A_HEAD_DIM)),
                       dtype=BF16)
    bdc_np, bds_np = _channel_dft()
    bdc, bds = jnp.asarray(bdc_np).astype(BF16), jnp.asarray(bds_np).astype(BF16)
    dft = {}
    for t in (CTX_T, LAT_T):
        ct_np, st_np = _dft_tables(t)
        dft[t] = (jnp.asarray(ct_np).astype(BF16), jnp.asarray(st_np).astype(BF16))
    cos_np, sin_np = _rope_tables()
    cos_t, sin_t = jnp.asarray(cos_np), jnp.asarray(sin_np)
    tri = jnp.asarray(np.tril(np.ones((TB, TB), np.float32), k=-1)).astype(BF16)

    b_gu = b_gate_up.reshape(DEPTH, N_EXPERTS, 1, 2 * D_FF)
    b_dn = b_down.reshape(DEPTH, N_EXPERTS, 1, D)
    w_in_p = jnp.pad(w_in, ((0, 0), (0, 0), (0, IN_PAD - w_in.shape[-1]))).astype(BF16)
    w_uq_p = jnp.pad(w_uq.reshape(DEPTH, Q_LORA, HEADS, NOPE + ROPE),
                     ((0, 0), (0, 0), (0, 0), (0, HEAD_PAD - NOPE - ROPE)))
    w_uq_p = w_uq_p.reshape(DEPTH, Q_LORA, HEADS * HEAD_PAD).astype(BF16)
    w_ukv_b = w_ukv.astype(BF16)
    ws = w_spatial.astype(BF16)
    bs_full = jnp.repeat(jnp.swapaxes(b_spatial, 1, 2), A_HEAD_DIM, axis=2)
    ckr_p = jnp.pad(cache_krope, ((0, 0), (0, 0), (0, 0), (0, LANES - ROPE))).astype(BF16)
    w_out_b = w_out.astype(BF16)
    w_r = jnp.pad(w_router, ((0, 0), (0, 0), (0, LANES - N_EXPERTS)))
    b_r = jnp.pad(b_router, ((0, 0), (0, LANES - N_EXPERTS)), constant_values=NEG)[:, None, :]
    row = lambda g: g[:, None, :]

    pre_params = (mods, row(g_pre_mix), w_in_p, row(g_sgu), gmat, row(g_q), w_uq_p, row(g_kv), w_ukv_b,
                  cos_t, sin_t)

    ckv_layers, krope_layers = [], []
    for i in range(DEPTH):
        u, vn, zf, q, kv, kr, ckv, zkr = _pre_call(i, xc, xl, pre_params)
        ckv_layers.append(ckv[:N_CTX].reshape(N_CTX_B, CTX_T, KV_LORA))
        krope_layers.append(zkr[:N_CTX, :ROPE].reshape(N_CTX_B, CTX_T, ROPE))

        mix_in = (u, vn, zf, q, kv, kr)
        y_ctx = _mix_call(i, mix_in, (ws, bs_full, bdc, bds) + dft[CTX_T], CTX_T, N_CTX_B, 0)
        y_lat = _mix_call(i, mix_in, (ws, bs_full, bdc, bds) + dft[LAT_T], LAT_T, N_LAT_B, N_CTX,
                          cache=(cache_ckv, ckr_p, w_ukv_b))

        x1, h2, tab, rw, cnt = _post_call(i, y_ctx, y_lat, xc, xl, mods, w_out_b,
                                          row(g_post_mix), row(g_pre_ffn), w_r, b_r, tri)

        dest, zrow, n_used, block_e, first, nxt, nxt2, slot = _plan_call(cnt, tab)
        dest_flat = dest.reshape(N_TOK // TB, 1, TOP_K * TB)
        xs = _dispatch_call(zrow, n_used, dest_flat, h2)
        ys = _moe_call(i, (block_e, first, nxt, nxt2, slot, n_used), xs, w_gate_up, b_gu, w_down, b_dn)
        xc, xl = _final_call(i, dest_flat, x1, ys, rw, mods, row(g_post_ffn))

    y_prompt = xc.reshape(N_CTX_B, CTX_T, D)
    y_sample = xl.reshape(N_LAT_B, LAT_T, D)
    return (y_prompt, y_sample, jnp.stack(ckv_layers, axis=1), jnp.stack(krope_layers, axis=1))
```

```python
import functools
import math

import jax
import jax.numpy as jnp
import numpy as np
from jax import lax
from jax.experimental import pallas as pl
from jax.experimental.pallas import tpu as pltpu

F32 = jnp.float32
BF16 = jnp.bfloat16

D = 1024
N_CTX_B, CTX_T = 16, 256
N_LAT_B, LAT_T = 2, 1024
PAST = 512
N_CTX = N_CTX_B * CTX_T
N_LAT = N_LAT_B * LAT_T
N_TOK = N_CTX + N_LAT
DEPTH = 2
GRID_W = 64
EPS = 1e-6
A_HEADS, A_HEAD_DIM, A_WIDTH, CHUNK = 4, 64, 256, 128
F_GROUPS, F_GROUP_DIM, F_WIDTH = 4, 64, 256
HEADS, Q_LORA, KV_LORA, NOPE, ROPE, V_DIM = 4, 256, 128, 128, 64, 128
HEAD_PAD = 256
IN_PAD = 1280
N_EXPERTS, TOP_K, D_FF = 32, 4, 1024
SWIGLU_LIMIT, SWIGLU_ALPHA = 7.0, 1.702
ROPE_BASE = 10000.0

TB = 512
QB = 256
RB = 256
W_SLOTS = 3
LANES = 128
SUB = D // LANES
N_BLOCKS = N_TOK * TOP_K // RB + N_EXPERTS
N_ROWS = N_BLOCKS * RB
NEG = -3.0e38
VMEM_LIMIT = 56 * 1024 * 1024


def _rms(x, g):
    return x * lax.rsqrt(jnp.mean(x * x, axis=-1, keepdims=True) + EPS) * g


def _split_dot(v, m):
    hi = v.astype(BF16)
    lo = (v - hi.astype(F32)).astype(BF16)
    return (jnp.dot(hi, m, preferred_element_type=F32)
            + jnp.dot(lo, m, preferred_element_type=F32))


def _dot_nt(a, b):
    return lax.dot_general(a, b, (((1,), (1,)), ((), ())), preferred_element_type=F32)


def _store_tiled(ref, x):
    rows = x.shape[0]
    for s in range(SUB):
        ref[pl.ds(s, rows, stride=SUB), :] = x[:, s * LANES:(s + 1) * LANES]


def _load_tiled(ref, rows):
    return jnp.concatenate([ref[pl.ds(s, rows, stride=SUB), :] for s in range(SUB)], axis=1)


def _tile_rows(ref, row, n_rows=1):
    return ref.at[pl.ds(pl.multiple_of(row * SUB, SUB), n_rows * SUB)]


def _mod_kernel(ct_ref, w_ref, b_ref, o_ref):
    ct = ct_ref[...]
    s = ct * jax.nn.sigmoid(ct)
    w = w_ref[...]
    o_ref[...] = jnp.zeros(o_ref.shape, F32)
    for r in range(3):
        o_ref[r:r + 1, :] = jnp.sum(w * s[:, r:r + 1], axis=0, keepdims=True) + b_ref[...]


def _mod_call(cond_t, w_ada, b_ada):
    cb = 2048
    return pl.pallas_call(
        _mod_kernel,
        grid=(DEPTH, 6 * D // cb),
        in_specs=[
            pl.BlockSpec((D, 8), lambda l, j: (0, 0)),
            pl.BlockSpec((None, D, cb), lambda l, j: (l, 0, j)),
            pl.BlockSpec((None, 1, cb), lambda l, j: (l, 0, j)),
        ],
        out_specs=pl.BlockSpec((None, 8, cb), lambda l, j: (l, 0, j)),
        out_shape=jax.ShapeDtypeStruct((DEPTH, 8, 6 * D), F32),
        compiler_params=pltpu.CompilerParams(dimension_semantics=("parallel", "parallel")),
        name="modulation",
    )(cond_t, w_ada, b_ada)


def _mod_index(i):
    first_lat = N_CTX // TB
    return jnp.where(i < first_lat, 0, 1 + (i - first_lat) // (LAT_T // TB))


def _ctx_spec(width):
    return pl.BlockSpec((TB, width), lambda i: (jnp.minimum(i, N_CTX // TB - 1), 0))


def _lat_spec(width):
    return pl.BlockSpec((TB, width), lambda i: (jnp.maximum(i - N_CTX // TB, 0), 0))


def _pick(i, ctx_ref, lat_ref):
    return jnp.where(i >= N_CTX // TB, lat_ref[...], ctx_ref[...])


def _layer_spec(layer, shape):
    zeros = (0,) * len(shape)
    return pl.BlockSpec((None,) + tuple(shape), lambda *_: (layer,) + zeros)


def _mod_spec(layer):
    return pl.BlockSpec((None, None, 6, D), lambda i: (layer, _mod_index(i), 0, 0))


def _swap_halves(x, lane):
    w = x.shape[-1]
    fwd = pltpu.roll(x, w - 16, 1)
    bwd = pltpu.roll(x, 16, 1)
    return jnp.where((lane & 31) < 16, fwd, bwd)


def _pre_kernel(xc_ref, xl_ref, *refs):
    i = pl.program_id(0)
    _pre_body(i, _pick(i, xc_ref, xl_ref), *refs)


def _pre_body(i, x, mod_ref, gpre_ref, win_ref, gsgu_ref, gmat_ref, gq_ref, wuq_ref,
              gkv_ref, wukv_ref, cos_ref, sin_ref,
              u_ref, vn_ref, zf_ref, q_ref, kv_ref, kr_ref, ckv_ref, zkr_ref):
    is_lat = i >= N_CTX // TB
    h = _rms(x, gpre_ref[...]) * (1.0 + mod_ref[1:2, :]) + mod_ref[0:1, :]
    z = jnp.dot(h.astype(BF16), win_ref[...], preferred_element_type=F32)

    ga = jax.nn.gelu(z[:, :2 * A_WIDTH])
    u_ref[...] = ga[:, :A_WIDTH]
    v = ga[:, A_WIDTH:]
    gmat = gmat_ref[...]
    dv = v - _split_dot(v, gmat)
    var = _split_dot(dv * dv, gmat)
    vn_ref[...] = (dv * lax.rsqrt(var + EPS) * gsgu_ref[...]).astype(BF16)

    zf_ref[...] = z[:, 512:768].astype(BF16)

    cos = jnp.where(is_lat, cos_ref[...], 1.0)
    sin = jnp.where(is_lat, sin_ref[...], 0.0)
    lane = lax.broadcasted_iota(jnp.int32, (TB, LANES), 1)

    qn = _rms(z[:, 768:1024], gq_ref[...])
    scale = (NOPE + ROPE) ** -0.5
    q = jnp.dot(qn.astype(BF16), wuq_ref[...], preferred_element_type=F32) * scale
    for hd in range(HEADS):
        base = hd * HEAD_PAD
        q_ref[:, base:base + NOPE] = q[:, base:base + NOPE].astype(BF16)
        qr = q[:, base + NOPE:base + HEAD_PAD]
        q_ref[:, base + NOPE:base + HEAD_PAD] = (qr * cos + _swap_halves(qr, lane) * sin).astype(BF16)

    ckv = _rms(z[:, 1024:1152], gkv_ref[...])
    ckv_ref[...] = ckv
    kv_ref[...] = jnp.dot(ckv.astype(BF16), wukv_ref[...], preferred_element_type=F32).astype(BF16)

    zkr = z[:, 1152:1280]
    zkr_ref[...] = zkr
    kr_ref[...] = (zkr * cos + _swap_halves(zkr, lane) * sin).astype(BF16)


def _pre_specs(layer):
    lay = functools.partial(_layer_spec, layer)
    first_lat = N_CTX // TB
    pos_blocks = LAT_T // TB

    def tok(width):
        return pl.BlockSpec((TB, width), lambda i: (i, 0))

    def full(shape):
        return pl.BlockSpec(shape, lambda i: (0,) * len(shape))

    def rope_map(i):
        return (jnp.where(i >= first_lat, (i - first_lat) % pos_blocks, 0), 0)

    in_specs = [
        _mod_spec(layer),
        lay((1, D)), lay((D, IN_PAD)), lay((1, A_WIDTH)), full((A_WIDTH, A_WIDTH)),
        lay((1, Q_LORA)), lay((Q_LORA, HEADS * HEAD_PAD)),
        lay((1, KV_LORA)), lay((KV_LORA, HEADS * (NOPE + V_DIM))),
        pl.BlockSpec((TB, LANES), rope_map), pl.BlockSpec((TB, LANES), rope_map),
    ]
    out_specs = (tok(A_WIDTH), tok(A_WIDTH), tok(F_WIDTH), tok(HEADS * HEAD_PAD),
                 tok(HEADS * (NOPE + V_DIM)), tok(LANES), tok(KV_LORA), tok(LANES))
    out_shape = (
        jax.ShapeDtypeStruct((N_TOK, A_WIDTH), F32),
        jax.ShapeDtypeStruct((N_TOK, A_WIDTH), BF16),
        jax.ShapeDtypeStruct((N_TOK, F_WIDTH), BF16),
        jax.ShapeDtypeStruct((N_TOK, HEADS * HEAD_PAD), BF16),
        jax.ShapeDtypeStruct((N_TOK, HEADS * (NOPE + V_DIM)), BF16),
        jax.ShapeDtypeStruct((N_TOK, LANES), BF16),
        jax.ShapeDtypeStruct((N_TOK, KV_LORA), F32),
        jax.ShapeDtypeStruct((N_TOK, LANES), F32),
    )
    return in_specs, out_specs, out_shape


def _pre_call(layer, xc, xl, pre_params):
    in_specs, out_specs, out_shape = _pre_specs(layer)
    return pl.pallas_call(
        _pre_kernel,
        grid=(N_TOK // TB,),
        in_specs=[_ctx_spec(D), _lat_spec(D)] + in_specs,
        out_specs=out_specs,
        out_shape=out_shape,
        compiler_params=pltpu.CompilerParams(dimension_semantics=("parallel",),
                                             vmem_limit_bytes=VMEM_LIMIT),
        name="pre_mix",
    )(xc, xl, *pre_params)


def _mix_kernel(*refs, has_cache):
    if has_cache:
        (u_ref, vn_ref, zf_ref, q_ref, kv_ref, kr_ref, ws_ref, bs_ref, bdc_ref, bds_ref,
         ct_ref, st_ref, cckv_ref, ckr_ref, wukv_ref, o_ref) = refs
    else:
        (u_ref, vn_ref, zf_ref, q_ref, kv_ref, kr_ref, ws_ref, bs_ref, bdc_ref, bds_ref,
         ct_ref, st_ref, o_ref) = refs

    lane = lax.broadcasted_iota(jnp.int32, (CHUNK, A_WIDTH), 1)
    for c in range(QB // CHUNK):
        rows = slice(c * CHUNK, (c + 1) * CHUNK)
        vch = vn_ref[rows, :]
        s = bs_ref[...]
        for g in range(A_HEADS):
            sg = jnp.dot(ws_ref[g], vch, preferred_element_type=F32)
            in_head = (lane >= g * A_HEAD_DIM) & (lane < (g + 1) * A_HEAD_DIM)
            s = s + jnp.where(in_head, sg, 0.0)
        o_ref[rows, 0:A_WIDTH] = (u_ref[rows, :] * s).astype(BF16)

    zf = zf_ref[...]
    zc = jnp.dot(zf, bdc_ref[...], preferred_element_type=F32).astype(BF16)
    zs = jnp.dot(zf, bds_ref[...], preferred_element_type=F32).astype(BF16)
    yf = (jnp.dot(ct_ref[...], zc, preferred_element_type=F32)
          - jnp.dot(st_ref[...], zs, preferred_element_type=F32))
    o_ref[:, A_WIDTH:A_WIDTH + F_WIDTH] = yf.astype(BF16)

    kr = kr_ref[...]
    if has_cache:
        kvc = jnp.dot(cckv_ref[...].astype(BF16), wukv_ref[...],
                      preferred_element_type=F32).astype(BF16)
        krc = ckr_ref[...]
    for hd in range(HEADS):
        qh = q_ref[:, hd * HEAD_PAD:(hd + 1) * HEAD_PAD]
        kb = hd * (NOPE + V_DIM)
        kh = jnp.concatenate([kv_ref[:, kb:kb + NOPE], kr], axis=1)
        vh = kv_ref[:, kb + NOPE:kb + NOPE + V_DIM]
        s = _dot_nt(qh, kh)
        m = jnp.max(s, axis=-1, keepdims=True)
        if has_cache:
            khc = jnp.concatenate([kvc[:, kb:kb + NOPE], krc], axis=1)
            vhc = kvc[:, kb + NOPE:kb + NOPE + V_DIM]
            sc = _dot_nt(qh, khc)
            m = jnp.maximum(m, jnp.max(sc, axis=-1, keepdims=True))
        e = jnp.exp(s - m)
        den = jnp.sum(e, axis=-1, keepdims=True)
        o = jnp.dot(e.astype(BF16), vh, preferred_element_type=F32)
        if has_cache:
            ec = jnp.exp(sc - m)
            den = den + jnp.sum(ec, axis=-1, keepdims=True)
            o = o + jnp.dot(ec.astype(BF16), vhc, preferred_element_type=F32)
        ob = A_WIDTH + F_WIDTH + hd * V_DIM
        o_ref[:, ob:ob + V_DIM] = (o * (1.0 / den)).astype(BF16)


def _mix_call(layer, pre, consts, seq_t, n_batch, tok_off, cache=None):
    u, vn, zf, q, kv, kr = pre
    ws, bs_full, bdc, bds, ct, st = consts
    lay = functools.partial(_layer_spec, layer)
    nq = seq_t // QB
    qoff = tok_off // QB
    soff = tok_off // seq_t

    def qrow(width):
        return pl.BlockSpec((QB, width), lambda b, j: (qoff + b * nq + j, 0))

    def srow(width):
        return pl.BlockSpec((seq_t, width), lambda b, j: (soff + b, 0))

    def full(shape):
        return pl.BlockSpec(shape, lambda b, j: (0,) * len(shape))

    in_specs = [qrow(A_WIDTH), qrow(A_WIDTH), srow(F_WIDTH), qrow(HEADS * HEAD_PAD),
                srow(HEADS * (NOPE + V_DIM)), srow(LANES),
                lay((A_HEADS, CHUNK, CHUNK)), lay((CHUNK, A_WIDTH)),
                full((F_WIDTH, F_WIDTH)), full((F_WIDTH, F_WIDTH)),
                pl.BlockSpec((QB, seq_t), lambda b, j: (j, 0)),
                pl.BlockSpec((QB, seq_t), lambda b, j: (j, 0))]
    args = [u, vn, zf, q, kv, kr, ws, bs_full, bdc, bds, ct, st]
    if cache is not None:
        cckv, ckr, wukv = cache
        in_specs += [pl.BlockSpec((None, None, PAST, KV_LORA), lambda b, j: (b, layer, 0, 0)),
                     pl.BlockSpec((None, None, PAST, LANES), lambda b, j: (b, layer, 0, 0)),
                     lay((KV_LORA, HEADS * (NOPE + V_DIM)))]
        args += [cckv, ckr, wukv]
    return pl.pallas_call(
        functools.partial(_mix_kernel, has_cache=cache is not None),
        grid=(n_batch, nq),
        in_specs=in_specs,
        out_specs=pl.BlockSpec((QB, D), lambda b, j: (b * nq + j, 0)),
        out_shape=jax.ShapeDtypeStruct((n_batch * seq_t, D), BF16),
        compiler_params=pltpu.CompilerParams(dimension_semantics=("parallel", "parallel"),
                                             vmem_limit_bytes=VMEM_LIMIT),
        name="mix_lat" if cache is not None else "mix_ctx",
    )(*args)


def _post_kernel(yc_ref, yl_ref, xc_ref, xl_ref, mod_ref, wout_ref, gpost_ref, gffn_ref, wr_ref, br_ref,
                 tri_ref, x1_ref, h2_ref, tab_ref, rw_ref, cnt_ref, carry_ref):
    i = pl.program_id(0)

    @pl.when(i == 0)
    def _():
        carry_ref[...] = jnp.zeros(carry_ref.shape, F32)

    y = jnp.dot(_pick(i, yc_ref, yl_ref), wout_ref[...], preferred_element_type=F32)
    x1 = _pick(i, xc_ref, xl_ref) + mod_ref[2:3, :] * _rms(y, gpost_ref[...])
    x1_ref[...] = x1
    h2 = _rms(x1, gffn_ref[...]) * (1.0 + mod_ref[4:5, :]) + mod_ref[3:4, :]
    _store_tiled(h2_ref, h2)

    wr = wr_ref[...]
    wr_hi = wr.astype(BF16)
    wr_lo = (wr - wr_hi.astype(F32)).astype(BF16)
    h_hi = h2.astype(BF16)
    h_lo = (h2 - h_hi.astype(F32)).astype(BF16)
    logits = (jnp.dot(h_hi, wr_hi, preferred_element_type=F32)
              + jnp.dot(h_lo, wr_hi, preferred_element_type=F32)
              + jnp.dot(h_hi, wr_lo, preferred_element_type=F32)) + br_ref[...]

    lane = lax.broadcasted_iota(jnp.int32, (TB, LANES), 1)
    lane_f = lane.astype(F32)
    work = logits
    idx, val = [], []
    for _ in range(TOP_K):
        m = jnp.max(work, axis=-1, keepdims=True)
        ik = jnp.min(jnp.where(work == m, lane_f, float(LANES)), axis=-1, keepdims=True)
        idx.append(ik)
        val.append(m)
        work = jnp.where(lane_f == ik, NEG, work)
    ex = [jnp.exp(v - val[0]) for v in val]
    den = ex[0] + ex[1] + ex[2] + ex[3]

    onehot = jnp.zeros((TB, LANES), F32)
    for k in range(TOP_K):
        onehot = onehot + jnp.where(lane_f == idx[k], 1.0, 0.0)
    before = jnp.dot(tri_ref[...], onehot.astype(BF16), preferred_element_type=F32) + carry_ref[0:1, :]
    ri = jnp.zeros((TB, LANES), F32)
    rw = jnp.zeros((TB, LANES), F32)
    for k in range(TOP_K):
        rank_k = jnp.sum(jnp.where(lane_f == idx[k], before, 0.0), axis=-1, keepdims=True)
        ri = ri + jnp.where(lane == k, idx[k], 0.0) + jnp.where(lane == TOP_K + k, rank_k, 0.0)
        rw = rw + jnp.where(lane == k, ex[k] / den, 0.0)
    tab_ref[...] = ri.T[0:2 * TOP_K, :].astype(jnp.int32)
    rw_ref[...] = rw
    total = carry_ref[0:1, :] + jnp.sum(onehot, axis=0, keepdims=True)
    carry_ref[...] = jnp.broadcast_to(total, carry_ref.shape)
    cnt_ref[...] = jnp.broadcast_to(total, cnt_ref.shape).astype(jnp.int32)


def _post_call(layer, yc, yl, xc, xl, mods, w_out, g_post, g_ffn, w_r, b_r, tri):
    nb = N_TOK // TB
    lay = functools.partial(_layer_spec, layer)

    def tok(width):
        return pl.BlockSpec((TB, width), lambda i: (i, 0))

    def full(shape):
        return pl.BlockSpec(shape, lambda i: (0,) * len(shape))

    return pl.pallas_call(
        _post_kernel,
        grid=(nb,),
        in_specs=[_ctx_spec(D), _lat_spec(D), _ctx_spec(D), _lat_spec(D), _mod_spec(layer),
                  lay((D, D)), lay((1, D)), lay((1, D)), lay((D, LANES)), lay((1, LANES)),
                  full((TB, TB))],
        out_specs=(tok(D), pl.BlockSpec((TB * SUB, LANES), lambda i: (i, 0)),
                   pl.BlockSpec((None, 2 * TOP_K, TB), lambda i: (i, 0, 0)),
                   tok(LANES), full((8, LANES))),
        out_shape=(jax.ShapeDtypeStruct((N_TOK, D), F32),
                   jax.ShapeDtypeStruct((N_TOK * SUB, LANES), F32),
                   jax.ShapeDtypeStruct((nb, 2 * TOP_K, TB), jnp.int32),
                   jax.ShapeDtypeStruct((N_TOK, LANES), F32),
                   jax.ShapeDtypeStruct((8, LANES), jnp.int32)),
        scratch_shapes=[pltpu.VMEM((8, LANES), F32)],
        compiler_params=pltpu.CompilerParams(dimension_semantics=("arbitrary",),
                                             vmem_limit_bytes=VMEM_LIMIT),
        name="post_mix_router",
    )(yc, yl, xc, xl, mods, w_out, g_post, g_ffn, w_r, b_r, tri)


def _plan_kernel(cnt_ref, tab_ref, dest_ref, cend_ref, pend_ref, nu_ref, be_ref, first_ref, half_ref, nxt_ref,
                 nxt2_ref, slot_ref, start_s, next_s):
    nxt = jnp.int32(-1)
    for e in reversed(range(N_EXPERTS)):
        next_s[e] = nxt
        nxt = jnp.where(cnt_ref[0, e] > 0, jnp.int32(e), nxt)

    start = jnp.int32(0)
    run = jnp.int32(0)
    last = jnp.int32(0)
    for e in range(N_EXPERTS):
        c = cnt_ref[0, e]
        n_blk = lax.shift_right_logical(c + (RB - 1), RB.bit_length() - 1)
        start_s[e] = start
        cend_ref[e] = start * RB + c
        pend_ref[e] = (start + n_blk) * RB
        few = (c - (n_blk - 1) * RB) <= RB // 2
        n1 = next_s[e]
        n2 = jnp.where(n1 >= 0, next_s[jnp.maximum(n1, 0)], jnp.int32(-1))
        slot = lax.rem(run, W_SLOTS)

        def fill(j, carry, e=e, n1=n1, n2=n2, slot=slot, start=start, n_blk=n_blk, few=few):
            be_ref[j] = jnp.int32(e)
            first_ref[j] = (j == start).astype(jnp.int32)
            half_ref[j] = jnp.logical_and(j == start + n_blk - 1, few).astype(jnp.int32)
            nxt_ref[j] = n1
            nxt2_ref[j] = n2
            slot_ref[j] = slot
            return carry

        lax.fori_loop(start, start + n_blk, fill, 0)
        owns = n_blk > 0
        last = jnp.where(owns, jnp.int32(e), last)
        run = run + owns.astype(jnp.int32)
        start = start + n_blk
    nu_ref[0] = start

    def tail(j, carry):
        be_ref[j] = last
        first_ref[j] = jnp.int32(0)
        half_ref[j] = jnp.int32(0)
        nxt_ref[j] = jnp.int32(-1)
        nxt2_ref[j] = jnp.int32(-1)
        slot_ref[j] = jnp.int32(0)
        return carry

    lax.fori_loop(start, N_BLOCKS, tail, 0)

    idx = tab_ref[:, 0:TOP_K, :]
    base = jnp.zeros(idx.shape, jnp.int32)
    for e in range(N_EXPERTS):
        base = jnp.where(idx == e, start_s[e] * RB, base)
    dest_ref[...] = base + tab_ref[:, TOP_K:2 * TOP_K, :]


def _plan_call(cnt, tab):
    nb = N_TOK // TB
    smem = pl.BlockSpec(memory_space=pltpu.SMEM)
    blocks = jax.ShapeDtypeStruct((N_BLOCKS,), jnp.int32)
    return pl.pallas_call(
        _plan_kernel,
        in_specs=[smem, pl.BlockSpec(memory_space=pltpu.VMEM)],
        out_specs=(pl.BlockSpec(memory_space=pltpu.VMEM),) + (smem,) * 9,
        out_shape=(jax.ShapeDtypeStruct((nb, TOP_K, TB), jnp.int32),
                   jax.ShapeDtypeStruct((N_EXPERTS,), jnp.int32),
                   jax.ShapeDtypeStruct((N_EXPERTS,), jnp.int32),
                   jax.ShapeDtypeStruct((1,), jnp.int32),
                   blocks, blocks, blocks, blocks, blocks, blocks),
        scratch_shapes=[pltpu.SMEM((N_EXPERTS,), jnp.int32), pltpu.SMEM((N_EXPERTS,), jnp.int32)],
        name="plan_rows",
    )(cnt, tab)


def _dispatch_kernel(cend_ref, pend_ref, nu_ref, dest_ref, h2_ref, xs_ref, zero_ref, sem, zsem):
    i = pl.program_id(0)

    def zero_padding(act):
        for e in range(N_EXPERTS):
            c_end = cend_ref[e]
            aligned = lax.shift_left(lax.shift_right_logical(c_end + 7, 3), 3)
            for r in range(7):

                @pl.when(c_end + r < aligned)
                def _():
                    act(pltpu.make_async_copy(_tile_rows(zero_ref, 0), _tile_rows(xs_ref, c_end + r), zsem))

            n = pend_ref[e] - aligned
            off = aligned
            for size in (128, 64, 32, 16, 8):

                @pl.when((n & size) != 0)
                def _():
                    act(pltpu.make_async_copy(_tile_rows(zero_ref, 0, size), _tile_rows(xs_ref, off, size),
                                              zsem))

                off = off + (n & size)

        def tail(b, carry):
            act(pltpu.make_async_copy(zero_ref, _tile_rows(xs_ref, b * RB, RB), zsem))
            return carry

        lax.fori_loop(nu_ref[0], N_BLOCKS, tail, 0)

    @pl.when(i == 0)
    def _():
        zero_ref[...] = jnp.zeros(zero_ref.shape, F32)
        zero_padding(lambda cp: cp.start())

    def body(t, carry):
        for k in range(TOP_K):
            pltpu.make_async_copy(_tile_rows(h2_ref, t), _tile_rows(xs_ref, dest_ref[0, k * TB + t]),
                                  sem).start(priority=k % 2)
        return carry

    lax.fori_loop(0, TB, body, 0, unroll=8)
    for k in range(TOP_K):
        pltpu.make_async_copy(h2_ref, _tile_rows(xs_ref, 0, TB), sem).wait()

    @pl.when(i == pl.num_programs(0) - 1)
    def _():
        zero_padding(lambda cp: cp.wait())


def _dispatch_call(cend, pend, n_used, dest, h2):
    nb = N_TOK // TB
    grid_spec = pltpu.PrefetchScalarGridSpec(
        num_scalar_prefetch=3,
        grid=(nb,),
        in_specs=[
            pl.BlockSpec((None, 1, TOP_K * TB), lambda i, c, p, n: (i, 0, 0), memory_space=pltpu.SMEM),
            pl.BlockSpec((TB * SUB, LANES), lambda i, c, p, n: (i, 0)),
        ],
        out_specs=pl.BlockSpec(memory_space=pl.ANY),
        scratch_shapes=[pltpu.VMEM((RB * SUB, LANES), F32), pltpu.SemaphoreType.DMA,
                        pltpu.SemaphoreType.DMA],
    )
    return pl.pallas_call(
        _dispatch_kernel,
        grid_spec=grid_spec,
        out_shape=jax.ShapeDtypeStruct((N_ROWS * SUB, LANES), F32),
        compiler_params=pltpu.CompilerParams(dimension_semantics=("arbitrary",),
                                             vmem_limit_bytes=VMEM_LIMIT),
        name="dispatch_rows",
    )(cend, pend, n_used, dest, h2)


def _moe_kernel(be_ref, first_ref, half_ref, nxt_ref, nxt2_ref, slot_ref, nu_ref, xs_ref, wgu_hbm, bgu_ref,
                wdn_hbm, bdn_ref, o_ref, wgu_f, wdn_f, sems, *, layer):
    b = pl.program_id(0)
    used = b < nu_ref[0]

    def fetch(e, s):
        return (pltpu.make_async_copy(wgu_hbm.at[layer, e], wgu_f.at[s], sems.at[0, s]),
                pltpu.make_async_copy(wdn_hbm.at[layer, e], wdn_f.at[s], sems.at[1, s]))

    @pl.when(b == 0)
    def _():
        for cp in fetch(be_ref[0], 0):
            cp.start(priority=1)

        @pl.when(nxt_ref[0] >= 0)
        def _():
            for cp in fetch(nxt_ref[0], 1):
                cp.start(priority=1)

    @pl.when(first_ref[b] == 1)
    def _():
        s = slot_ref[b]
        for cp in fetch(be_ref[b], s):
            cp.wait()

        @pl.when(nxt2_ref[b] >= 0)
        def _():
            for cp in fetch(nxt2_ref[b], (s + 2) % W_SLOTS):
                cp.start(priority=1)

    def experts(rows):
        s = slot_ref[b]
        x = _load_tiled(xs_ref, rows).astype(BF16)
        gu = jnp.dot(x, wgu_f[s].astype(BF16), preferred_element_type=F32) + bgu_ref[...]
        g = jnp.minimum(gu[:, :D_FF], SWIGLU_LIMIT)
        l = jnp.clip(gu[:, D_FF:], -SWIGLU_LIMIT, SWIGLU_LIMIT)
        a = g * jax.nn.sigmoid(SWIGLU_ALPHA * g) * (l + 1.0)
        y = jnp.dot(a.astype(BF16), wdn_f[s].astype(BF16), preferred_element_type=F32) + bdn_ref[...]
        _store_tiled(o_ref, y)
        if rows < RB:
            o_ref[rows * SUB:, :] = jnp.zeros(((RB - rows) * SUB, LANES), o_ref.dtype)

    half = half_ref[b] == 1

    @pl.when(jnp.logical_and(used, jnp.logical_not(half)))
    def _():
        experts(RB)

    @pl.when(jnp.logical_and(used, half))
    def _():
        experts(RB // 2)

    @pl.when(jnp.logical_not(used))
    def _():
        o_ref[...] = jnp.zeros(o_ref.shape, o_ref.dtype)


def _moe_call(layer, tables, xs, w_gu, b_gu, w_dn, b_dn):
    def rows_in(b, be, fi, ha, nx, n2, sl, nu):
        return (jnp.minimum(b, nu[0] - 1), 0)

    def rows_out(b, be, fi, ha, nx, n2, sl, nu):
        return (b, 0)

    def expert(b, be, fi, ha, nx, n2, sl, nu):
        return (layer, be[b], 0, 0)

    grid_spec = pltpu.PrefetchScalarGridSpec(
        num_scalar_prefetch=7,
        grid=(N_BLOCKS,),
        in_specs=[
            pl.BlockSpec((RB * SUB, LANES), rows_in),
            pl.BlockSpec(memory_space=pl.ANY),
            pl.BlockSpec((None, None, 1, 2 * D_FF), expert),
            pl.BlockSpec(memory_space=pl.ANY),
            pl.BlockSpec((None, None, 1, D), expert),
        ],
        out_specs=pl.BlockSpec((RB * SUB, LANES), rows_out),
        scratch_shapes=[pltpu.VMEM((W_SLOTS, D, 2 * D_FF), F32), pltpu.VMEM((W_SLOTS, D_FF, D), F32),
                        pltpu.SemaphoreType.DMA((2, W_SLOTS))],
    )
    return pl.pallas_call(
        functools.partial(_moe_kernel, layer=layer),
        grid_spec=grid_spec,
        out_shape=jax.ShapeDtypeStruct((N_ROWS * SUB, LANES), F32),
        compiler_params=pltpu.CompilerParams(dimension_semantics=("arbitrary",),
                                             vmem_limit_bytes=VMEM_LIMIT),
        name="moe_experts",
    )(*tables, xs, w_gu, b_gu, w_dn, b_dn)


def _final_kernel(*refs):
    _combine_body(pl.program_id(0), *refs)


def _combine_body(i, dcur_ref, dnxt_ref, x1_ref, ys_ref, rw_ref, mod_ref, g_ref, oc_ref, ol_ref, buf, sems):
    nb = pl.num_programs(0)
    slot = i % 2

    def gather(dest_ref, s):
        def body(t, carry):
            for k in range(TOP_K):
                pltpu.make_async_copy(_tile_rows(ys_ref, dest_ref[0, k * TB + t]),
                                      _tile_rows(buf.at[s], k * TB + t), sems.at[s]).start(priority=k % 2)
            return carry

        lax.fori_loop(0, TB, body, 0, unroll=8)

    @pl.when(i == 0)
    def _():
        gather(dcur_ref, 0)

    @pl.when(i + 1 < nb)
    def _():
        gather(dnxt_ref, 1 - slot)

    pltpu.make_async_copy(_tile_rows(ys_ref, 0, TOP_K * TB), buf.at[slot], sems.at[slot]).wait()

    rw = rw_ref[...]
    y = jnp.zeros((TB, D), F32)
    for k in range(TOP_K):
        yk = jnp.concatenate([buf[slot, pl.ds(k * TB * SUB + s, TB, stride=SUB), :] for s in range(SUB)],
                             axis=1)
        y = y + yk * rw[:, k:k + 1]
    x2 = x1_ref[...] + mod_ref[5:6, :] * _rms(y, g_ref[...])

    @pl.when(i < N_CTX // TB)
    def _():
        oc_ref[...] = x2

    @pl.when(i >= N_CTX // TB)
    def _():
        ol_ref[...] = x2


def _final_call(layer, dest_flat, x1, ys, rw, mods, g_post_ffn):
    nb = N_TOK // TB
    return pl.pallas_call(
        _final_kernel,
        grid=(nb,),
        in_specs=[pl.BlockSpec((None, 1, TOP_K * TB), lambda i: (i, 0, 0), memory_space=pltpu.SMEM),
                  pl.BlockSpec((None, 1, TOP_K * TB), lambda i: (jnp.minimum(i + 1, nb - 1), 0, 0),
                               memory_space=pltpu.SMEM),
                  pl.BlockSpec((TB, D), lambda i: (i, 0)),
                  pl.BlockSpec(memory_space=pl.ANY),
                  pl.BlockSpec((TB, LANES), lambda i: (i, 0)),
                  _mod_spec(layer), _layer_spec(layer, (1, D))],
        out_specs=(_ctx_spec(D), _lat_spec(D)),
        out_shape=(jax.ShapeDtypeStruct((N_CTX, D), F32), jax.ShapeDtypeStruct((N_LAT, D), F32)),
        scratch_shapes=[pltpu.VMEM((2, TOP_K * TB * SUB, LANES), F32), pltpu.SemaphoreType.DMA((2,))],
        compiler_params=pltpu.CompilerParams(dimension_semantics=("arbitrary",),
                                             vmem_limit_bytes=VMEM_LIMIT),
        name="combine_residual",
    )(dest_flat, dest_flat, x1, ys, rw, mods, g_post_ffn)


def _dft_tables(t):
    j = np.arange(t, dtype=np.int64)
    ang = 2.0 * np.pi * ((j[:, None] * j[None, :]) % t) / t
    return (np.cos(ang) / math.sqrt(t)).astype(np.float32), (np.sin(ang) / math.sqrt(t)).astype(np.float32)


def _channel_dft():
    c = np.arange(F_GROUP_DIM, dtype=np.int64)
    ang = 2.0 * np.pi * ((c[:, None] * c[None, :]) % F_GROUP_DIM) / F_GROUP_DIM
    eye = np.eye(F_GROUPS)
    bdc = np.kron(eye, np.cos(ang)) / math.sqrt(F_GROUP_DIM)
    bds = np.kron(eye, np.sin(ang)) / math.sqrt(F_GROUP_DIM)
    return bdc.astype(np.float32), bds.astype(np.float32)


def _rope_tables():
    pos = np.arange(LAT_T)
    n = ROPE // 4
    inv_freq = np.power(np.float32(ROPE_BASE), -np.arange(n, dtype=np.float32) / np.float32(n))
    ang_r = (pos // GRID_W).astype(np.float32)[:, None] * inv_freq
    ang_c = (pos % GRID_W).astype(np.float32)[:, None] * inv_freq
    cos = np.concatenate([np.cos(ang_r), np.cos(ang_r), np.cos(ang_c), np.cos(ang_c),
                          np.ones((LAT_T, LANES - ROPE))], axis=1)
    sin = np.concatenate([-np.sin(ang_r), np.sin(ang_r), -np.sin(ang_c), np.sin(ang_c),
                          np.zeros((LAT_T, LANES - ROPE))], axis=1)
    return cos.astype(np.float32), sin.astype(np.float32)


def kernel(x_prompt, x_sample, cache_ckv, cache_krope, c, c_ctx, w_ada, b_ada, g_pre_mix, g_post_mix, g_pre_ffn, g_post_ffn, w_in, g_sgu, w_spatial, b_spatial, g_q, w_uq, g_kv, w_ukv, w_out, w_router, b_router, w_gate_up, b_gate_up, w_down, b_down):
    xc, xl = x_prompt.reshape(N_CTX, D), x_sample.reshape(N_LAT, D)

    cond = jnp.concatenate([c_ctx[None, :], c, jnp.zeros((5, D), F32)], axis=0)
    mods = _mod_call(cond.T, w_ada, b_ada.reshape(DEPTH, 1, 6 * D))
    mods = mods[:, :3].reshape(DEPTH, 3, 6, D)

    gmat = jnp.asarray(np.kron(np.eye(A_HEADS), np.full((A_HEAD_DIM, A_HEAD_DIM), 1.0 / A_HEAD_DIM)),
                       dtype=BF16)
    bdc_np, bds_np = _channel_dft()
    bdc, bds = jnp.asarray(bdc_np).astype(BF16), jnp.asarray(bds_np).astype(BF16)
    dft = {}
    for t in (CTX_T, LAT_T):
        ct_np, st_np = _dft_tables(t)
        dft[t] = (jnp.asarray(ct_np).astype(BF16), jnp.asarray(st_np).astype(BF16))
    cos_np, sin_np = _rope_tables()
    cos_t, sin_t = jnp.asarray(cos_np), jnp.asarray(sin_np)
    tri = jnp.asarray(np.tril(np.ones((TB, TB), np.float32), k=-1)).astype(BF16)

    b_gu = b_gate_up.reshape(DEPTH, N_EXPERTS, 1, 2 * D_FF)
    b_dn = b_down.reshape(DEPTH, N_EXPERTS, 1, D)
    w_in_p = jnp.pad(w_in, ((0, 0), (0, 0), (0, IN_PAD - w_in.shape[-1]))).astype(BF16)
    w_uq_p = jnp.pad(w_uq.reshape(DEPTH, Q_LORA, HEADS, NOPE + ROPE),
                     ((0, 0), (0, 0), (0, 0), (0, HEAD_PAD - NOPE - ROPE)))
    w_uq_p = w_uq_p.reshape(DEPTH, Q_LORA, HEADS * HEAD_PAD).astype(BF16)
    w_ukv_b = w_ukv.astype(BF16)
    ws = w_spatial.astype(BF16)
    bs_full = jnp.repeat(jnp.swapaxes(b_spatial, 1, 2), A_HEAD_DIM, axis=2)
    ckr_p = jnp.pad(cache_krope, ((0, 0), (0, 0), (0, 0), (0, LANES - ROPE))).astype(BF16)
    w_out_b = w_out.astype(BF16)
    w_r = jnp.pad(w_router, ((0, 0), (0, 0), (0, LANES - N_EXPERTS)))
    b_r = jnp.pad(b_router, ((0, 0), (0, LANES - N_EXPERTS)), constant_values=NEG)[:, None, :]
    row = lambda g: g[:, None, :]

    pre_params = (mods, row(g_pre_mix), w_in_p, row(g_sgu), gmat, row(g_q), w_uq_p, row(g_kv), w_ukv_b,
                  cos_t, sin_t)

    ckv_layers, krope_layers = [], []
    for i in range(DEPTH):
        u, vn, zf, q, kv, kr, ckv, zkr = _pre_call(i, xc, xl, pre_params)
        ckv_layers.append(ckv[:N_CTX].reshape(N_CTX_B, CTX_T, KV_LORA))
        krope_layers.append(zkr[:N_CTX, :ROPE].reshape(N_CTX_B, CTX_T, ROPE))

        mix_in = (u, vn, zf, q, kv, kr)
        y_ctx = _mix_call(i, mix_in, (ws, bs_full, bdc, bds) + dft[CTX_T], CTX_T, N_CTX_B, 0)
        y_lat = _mix_call(i, mix_in, (ws, bs_full, bdc, bds) + dft[LAT_T], LAT_T, N_LAT_B, N_CTX,
                          cache=(cache_ckv, ckr_p, w_ukv_b))

        x1, h2, tab, rw, cnt = _post_call(i, y_ctx, y_lat, xc, xl, mods, w_out_b,
                                          row(g_post_mix), row(g_pre_ffn), w_r, b_r, tri)

        dest, cend, pend, n_used, block_e, first, half, nxt, nxt2, slot = _plan_call(cnt, tab)
        dest_flat = dest.reshape(N_TOK // TB, 1, TOP_K * TB)
        xs = _dispatch_call(cend, pend, n_used, dest_flat, h2)
        ys = _moe_call(i, (block_e, first, half, nxt, nxt2, slot, n_used), xs, w_gate_up, b_gu, w_down, b_dn)
        xc, xl = _final_call(i, dest_flat, x1, ys, rw, mods, row(g_post_ffn))

    y_prompt = xc.reshape(N_CTX_B, CTX_T, D)
    y_sample = xl.reshape(N_LAT_B, LAT_T, D)
    return (y_prompt, y_sample, jnp.stack(ckv_layers, axis=1), jnp.stack(krope_layers, axis=1))
```

```python
import functools
import math

import jax
import jax.numpy as jnp
import numpy as np
from jax import lax
from jax.experimental import pallas as pl
from jax.experimental.pallas import tpu as pltpu

F32 = jnp.float32
BF16 = jnp.bfloat16

D = 1024
N_CTX_B, CTX_T = 16, 256
N_LAT_B, LAT_T = 2, 1024
PAST = 512
N_CTX = N_CTX_B * CTX_T
N_LAT = N_LAT_B * LAT_T
N_TOK = N_CTX + N_LAT
DEPTH = 2
GRID_W = 64
EPS = 1e-6
A_HEADS, A_HEAD_DIM, A_WIDTH, CHUNK = 4, 64, 256, 128
F_GROUPS, F_GROUP_DIM, F_WIDTH = 4, 64, 256
HEADS, Q_LORA, KV_LORA, NOPE, ROPE, V_DIM = 4, 256, 128, 128, 64, 128
HEAD_PAD = 256
IN_PAD = 1280
N_EXPERTS, TOP_K, D_FF = 32, 4, 1024
SWIGLU_LIMIT, SWIGLU_ALPHA = 7.0, 1.702
ROPE_BASE = 10000.0

TB = 512
QB = 256
RB = 256
W_SLOTS = 3
LANES = 128
SUB = D // LANES
N_BLOCKS = N_TOK * TOP_K // RB + N_EXPERTS
N_ROWS = N_BLOCKS * RB
NEG = -3.0e38
VMEM_LIMIT = 56 * 1024 * 1024


def _rms(x, g):
    return x * lax.rsqrt(jnp.mean(x * x, axis=-1, keepdims=True) + EPS) * g


def _split_dot(v, m):
    hi = v.astype(BF16)
    lo = (v - hi.astype(F32)).astype(BF16)
    return (jnp.dot(hi, m, preferred_element_type=F32)
            + jnp.dot(lo, m, preferred_element_type=F32))


def _dot_nt(a, b):
    return lax.dot_general(a, b, (((1,), (1,)), ((), ())), preferred_element_type=F32)


def _store_tiled(ref, x):
    rows = x.shape[0]
    for s in range(SUB):
        ref[pl.ds(s, rows, stride=SUB), :] = x[:, s * LANES:(s + 1) * LANES]


def _load_tiled(ref, rows):
    return jnp.concatenate([ref[pl.ds(s, rows, stride=SUB), :] for s in range(SUB)], axis=1)


def _tile_rows(ref, row, n_rows=1):
    return ref.at[pl.ds(pl.multiple_of(row * SUB, SUB), n_rows * SUB)]


def _mod_kernel(ct_ref, w_ref, b_ref, o_ref):
    ct = ct_ref[...]
    s = ct * jax.nn.sigmoid(ct)
    w = w_ref[...]
    o_ref[...] = jnp.zeros(o_ref.shape, F32)
    for r in range(3):
        o_ref[r:r + 1, :] = jnp.sum(w * s[:, r:r + 1], axis=0, keepdims=True) + b_ref[...]


def _mod_call(cond_t, w_ada, b_ada):
    cb = 2048
    return pl.pallas_call(
        _mod_kernel,
        grid=(DEPTH, 6 * D // cb),
        in_specs=[
            pl.BlockSpec((D, 8), lambda l, j: (0, 0)),
            pl.BlockSpec((None, D, cb), lambda l, j: (l, 0, j)),
            pl.BlockSpec((None, 1, cb), lambda l, j: (l, 0, j)),
        ],
        out_specs=pl.BlockSpec((None, 8, cb), lambda l, j: (l, 0, j)),
        out_shape=jax.ShapeDtypeStruct((DEPTH, 8, 6 * D), F32),
        compiler_params=pltpu.CompilerParams(dimension_semantics=("parallel", "parallel")),
        name="modulation",
    )(cond_t, w_ada, b_ada)


def _mod_index(i):
    first_lat = N_CTX // TB
    return jnp.where(i < first_lat, 0, 1 + (i - first_lat) // (LAT_T // TB))


def _ctx_spec(width):
    return pl.BlockSpec((TB, width), lambda i: (jnp.minimum(i, N_CTX // TB - 1), 0))


def _lat_spec(width):
    return pl.BlockSpec((TB, width), lambda i: (jnp.maximum(i - N_CTX // TB, 0), 0))


def _pick(i, ctx_ref, lat_ref):
    return jnp.where(i >= N_CTX // TB, lat_ref[...], ctx_ref[...])


def _layer_spec(layer, shape):
    zeros = (0,) * len(shape)
    return pl.BlockSpec((None,) + tuple(shape), lambda *_: (layer,) + zeros)


def _mod_spec(layer):
    return pl.BlockSpec((None, None, 6, D), lambda i: (layer, _mod_index(i), 0, 0))


def _swap_halves(x, lane):
    w = x.shape[-1]
    fwd = pltpu.roll(x, w - 16, 1)
    bwd = pltpu.roll(x, 16, 1)
    return jnp.where((lane & 31) < 16, fwd, bwd)


def _pre_kernel(xc_ref, xl_ref, *refs):
    i = pl.program_id(0)
    _pre_body(i, _pick(i, xc_ref, xl_ref), *refs)


def _pre_body(i, x, mod_ref, gpre_ref, win_ref, gsgu_ref, gmat_ref, gq_ref, wuq_ref,
              gkv_ref, wukv_ref, cos_ref, sin_ref,
              u_ref, vn_ref, zf_ref, q_ref, kv_ref, kr_ref, ckv_ref, zkr_ref):
    is_lat = i >= N_CTX // TB
    h = _rms(x, gpre_ref[...] * (1.0 + mod_ref[1:2, :])) + mod_ref[0:1, :]
    z = jnp.dot(h.astype(BF16), win_ref[...], preferred_element_type=F32)

    ga = jax.nn.gelu(z[:, :2 * A_WIDTH])
    u_ref[...] = ga[:, :A_WIDTH]
    v = ga[:, A_WIDTH:]
    gmat = gmat_ref[...]
    dv = v - _split_dot(v, gmat)
    var = _split_dot(dv * dv, gmat)
    vn_ref[...] = (dv * lax.rsqrt(var + EPS) * gsgu_ref[...]).astype(BF16)

    zf_ref[...] = z[:, 512:768].astype(BF16)

    cos = jnp.where(is_lat, cos_ref[...], 1.0)
    sin = jnp.where(is_lat, sin_ref[...], 0.0)
    lane = lax.broadcasted_iota(jnp.int32, (TB, LANES), 1)

    qn = _rms(z[:, 768:1024], gq_ref[...])
    scale = (NOPE + ROPE) ** -0.5
    q = jnp.dot(qn.astype(BF16), wuq_ref[...], preferred_element_type=F32) * scale
    for hd in range(HEADS):
        base = hd * HEAD_PAD
        q_ref[:, base:base + NOPE] = q[:, base:base + NOPE].astype(BF16)
        qr = q[:, base + NOPE:base + HEAD_PAD]
        q_ref[:, base + NOPE:base + HEAD_PAD] = (qr * cos + _swap_halves(qr, lane) * sin).astype(BF16)

    ckv = _rms(z[:, 1024:1152], gkv_ref[...])
    ckv_ref[...] = ckv
    kv_ref[...] = jnp.dot(ckv.astype(BF16), wukv_ref[...], preferred_element_type=F32).astype(BF16)

    zkr = z[:, 1152:1280]
    zkr_ref[...] = zkr
    kr_ref[...] = (zkr * cos + _swap_halves(zkr, lane) * sin).astype(BF16)


def _pre_specs(layer):
    lay = functools.partial(_layer_spec, layer)
    first_lat = N_CTX // TB
    pos_blocks = LAT_T // TB

    def tok(width):
        return pl.BlockSpec((TB, width), lambda i: (i, 0))

    def full(shape):
        return pl.BlockSpec(shape, lambda i: (0,) * len(shape))

    def rope_map(i):
        return (jnp.where(i >= first_lat, (i - first_lat) % pos_blocks, 0), 0)

    in_specs = [
        _mod_spec(layer),
        lay((1, D)), lay((D, IN_PAD)), lay((1, A_WIDTH)), full((A_WIDTH, A_WIDTH)),
        lay((1, Q_LORA)), lay((Q_LORA, HEADS * HEAD_PAD)),
        lay((1, KV_LORA)), lay((KV_LORA, HEADS * (NOPE + V_DIM))),
        pl.BlockSpec((TB, LANES), rope_map), pl.BlockSpec((TB, LANES), rope_map),
    ]
    out_specs = (tok(A_WIDTH), tok(A_WIDTH), tok(F_WIDTH), tok(HEADS * HEAD_PAD),
                 tok(HEADS * (NOPE + V_DIM)), tok(LANES), tok(KV_LORA), tok(LANES))
    out_shape = (
        jax.ShapeDtypeStruct((N_TOK, A_WIDTH), F32),
        jax.ShapeDtypeStruct((N_TOK, A_WIDTH), BF16),
        jax.ShapeDtypeStruct((N_TOK, F_WIDTH), BF16),
        jax.ShapeDtypeStruct((N_TOK, HEADS * HEAD_PAD), BF16),
        jax.ShapeDtypeStruct((N_TOK, HEADS * (NOPE + V_DIM)), BF16),
        jax.ShapeDtypeStruct((N_TOK, LANES), BF16),
        jax.ShapeDtypeStruct((N_TOK, KV_LORA), F32),
        jax.ShapeDtypeStruct((N_TOK, LANES), F32),
    )
    return in_specs, out_specs, out_shape


def _pre_call(layer, xc, xl, pre_params):
    in_specs, out_specs, out_shape = _pre_specs(layer)
    return pl.pallas_call(
        _pre_kernel,
        grid=(N_TOK // TB,),
        in_specs=[_ctx_spec(D), _lat_spec(D)] + in_specs,
        out_specs=out_specs,
        out_shape=out_shape,
        compiler_params=pltpu.CompilerParams(dimension_semantics=("parallel",),
                                             vmem_limit_bytes=VMEM_LIMIT),
        name="pre_mix",
    )(xc, xl, *pre_params)


def _mix_kernel(*refs, has_cache):
    if has_cache:
        (u_ref, vn_ref, zf_ref, q_ref, kv_ref, kr_ref, ws_ref, bs_ref, bdc_ref, bds_ref,
         ct_ref, st_ref, cckv_ref, ckr_ref, wukv_ref, o_ref) = refs
    else:
        (u_ref, vn_ref, zf_ref, q_ref, kv_ref, kr_ref, ws_ref, bs_ref, bdc_ref, bds_ref,
         ct_ref, st_ref, o_ref) = refs

    lane = lax.broadcasted_iota(jnp.int32, (CHUNK, A_WIDTH), 1)
    for c in range(QB // CHUNK):
        rows = slice(c * CHUNK, (c + 1) * CHUNK)
        vch = vn_ref[rows, :]
        s = bs_ref[...]
        for g in range(A_HEADS):
            sg = jnp.dot(ws_ref[g], vch, preferred_element_type=F32)
            in_head = (lane >= g * A_HEAD_DIM) & (lane < (g + 1) * A_HEAD_DIM)
            s = s + jnp.where(in_head, sg, 0.0)
        o_ref[rows, 0:A_WIDTH] = (u_ref[rows, :] * s).astype(BF16)

    zf = zf_ref[...]
    zc = jnp.dot(zf, bdc_ref[...], preferred_element_type=F32).astype(BF16)
    zs = jnp.dot(zf, bds_ref[...], preferred_element_type=F32).astype(BF16)
    yf = (jnp.dot(ct_ref[...], zc, preferred_element_type=F32)
          - jnp.dot(st_ref[...], zs, preferred_element_type=F32))
    o_ref[:, A_WIDTH:A_WIDTH + F_WIDTH] = yf.astype(BF16)

    kr = kr_ref[...]
    if has_cache:
        kvc = jnp.dot(cckv_ref[...].astype(BF16), wukv_ref[...],
                      preferred_element_type=F32).astype(BF16)
        krc = ckr_ref[...]
    for hd in range(HEADS):
        qh = q_ref[:, hd * HEAD_PAD:(hd + 1) * HEAD_PAD]
        kb = hd * (NOPE + V_DIM)
        kh = jnp.concatenate([kv_ref[:, kb:kb + NOPE], kr], axis=1)
        vh = kv_ref[:, kb + NOPE:kb + NOPE + V_DIM]
        s = _dot_nt(qh, kh)
        m = jnp.max(s, axis=-1, keepdims=True)
        if has_cache:
            khc = jnp.concatenate([kvc[:, kb:kb + NOPE], krc], axis=1)
            vhc = kvc[:, kb + NOPE:kb + NOPE + V_DIM]
            sc = _dot_nt(qh, khc)
            m = jnp.maximum(m, jnp.max(sc, axis=-1, keepdims=True))
        e = jnp.exp(s - m)
        den = jnp.sum(e, axis=-1, keepdims=True)
        o = jnp.dot(e.astype(BF16), vh, preferred_element_type=F32)
        if has_cache:
            ec = jnp.exp(sc - m)
            den = den + jnp.sum(ec, axis=-1, keepdims=True)
            o = o + jnp.dot(ec.astype(BF16), vhc, preferred_element_type=F32)
        ob = A_WIDTH + F_WIDTH + hd * V_DIM
        o_ref[:, ob:ob + V_DIM] = (o * (1.0 / den)).astype(BF16)


def _mix_call(layer, pre, consts, seq_t, n_batch, tok_off, cache=None):
    u, vn, zf, q, kv, kr = pre
    ws, bs_full, bdc, bds, ct, st = consts
    lay = functools.partial(_layer_spec, layer)
    nq = seq_t // QB
    qoff = tok_off // QB
    soff = tok_off // seq_t

    def qrow(width):
        return pl.BlockSpec((QB, width), lambda b, j: (qoff + b * nq + j, 0))

    def srow(width):
        return pl.BlockSpec((seq_t, width), lambda b, j: (soff + b, 0))

    def full(shape):
        return pl.BlockSpec(shape, lambda b, j: (0,) * len(shape))

    in_specs = [qrow(A_WIDTH), qrow(A_WIDTH), srow(F_WIDTH), qrow(HEADS * HEAD_PAD),
                srow(HEADS * (NOPE + V_DIM)), srow(LANES),
                lay((A_HEADS, CHUNK, CHUNK)), lay((CHUNK, A_WIDTH)),
                full((F_WIDTH, F_WIDTH)), full((F_WIDTH, F_WIDTH)),
                pl.BlockSpec((QB, seq_t), lambda b, j: (j, 0)),
                pl.BlockSpec((QB, seq_t), lambda b, j: (j, 0))]
    args = [u, vn, zf, q, kv, kr, ws, bs_full, bdc, bds, ct, st]
    if cache is not None:
        cckv, ckr, wukv = cache
        in_specs += [pl.BlockSpec((None, None, PAST, KV_LORA), lambda b, j: (b, layer, 0, 0)),
                     pl.BlockSpec((None, None, PAST, LANES), lambda b, j: (b, layer, 0, 0)),
                     lay((KV_LORA, HEADS * (NOPE + V_DIM)))]
        args += [cckv, ckr, wukv]
    return pl.pallas_call(
        functools.partial(_mix_kernel, has_cache=cache is not None),
        grid=(n_batch, nq),
        in_specs=in_specs,
        out_specs=pl.BlockSpec((QB, D), lambda b, j: (b * nq + j, 0)),
        out_shape=jax.ShapeDtypeStruct((n_batch * seq_t, D), BF16),
        compiler_params=pltpu.CompilerParams(dimension_semantics=("parallel", "parallel"),
                                             vmem_limit_bytes=VMEM_LIMIT),
        name="mix_lat" if cache is not None else "mix_ctx",
    )(*args)


def _post_kernel(yc_ref, yl_ref, xc_ref, xl_ref, mod_ref, wout_ref, gpost_ref, gffn_ref, wr_ref, br_ref,
                 tri_ref, x1_ref, h2_ref, tab_ref, rw_ref, cnt_ref, carry_ref):
    i = pl.program_id(0)

    @pl.when(i == 0)
    def _():
        carry_ref[...] = jnp.zeros(carry_ref.shape, F32)

    y = jnp.dot(_pick(i, yc_ref, yl_ref), wout_ref[...], preferred_element_type=F32)
    x1 = _pick(i, xc_ref, xl_ref) + _rms(y, gpost_ref[...] * mod_ref[2:3, :])
    x1_ref[...] = x1
    h2 = _rms(x1, gffn_ref[...] * (1.0 + mod_ref[4:5, :])) + mod_ref[3:4, :]
    _store_tiled(h2_ref, h2)

    wr = wr_ref[...]
    wr_hi = wr.astype(BF16)
    wr_lo = (wr - wr_hi.astype(F32)).astype(BF16)
    h_hi = h2.astype(BF16)
    h_lo = (h2 - h_hi.astype(F32)).astype(BF16)
    hi_terms = jnp.dot(h_hi, jnp.concatenate([wr_hi, wr_lo], axis=1), preferred_element_type=F32)
    logits = (hi_terms[:, :LANES] + jnp.dot(h_lo, wr_hi, preferred_element_type=F32)
              + hi_terms[:, LANES:]) + br_ref[...]

    lane = lax.broadcasted_iota(jnp.int32, (TB, LANES), 1)
    lane_f = lane.astype(F32)
    work = logits
    idx, val = [], []
    for _ in range(TOP_K):
        m = jnp.max(work, axis=-1, keepdims=True)
        ik = jnp.min(jnp.where(work == m, lane_f, float(LANES)), axis=-1, keepdims=True)
        idx.append(ik)
        val.append(m)
        work = jnp.where(lane_f == ik, NEG, work)
    ex = [jnp.exp(v - val[0]) for v in val]
    den = ex[0] + ex[1] + ex[2] + ex[3]

    onehot = jnp.zeros((TB, LANES), F32)
    for k in range(TOP_K):
        onehot = onehot + jnp.where(lane_f == idx[k], 1.0, 0.0)
    before = jnp.dot(tri_ref[...], onehot.astype(BF16), preferred_element_type=F32) + carry_ref[0:1, :]
    ri = jnp.zeros((TB, LANES), F32)
    rw = jnp.zeros((TB, LANES), F32)
    for k in range(TOP_K):
        rank_k = jnp.sum(jnp.where(lane_f == idx[k], before, 0.0), axis=-1, keepdims=True)
        ri = ri + jnp.where(lane == k, idx[k], 0.0) + jnp.where(lane == TOP_K + k, rank_k, 0.0)
        rw = rw + jnp.where(lane == k, ex[k] / den, 0.0)
    tab_ref[...] = ri.T[0:2 * TOP_K, :].astype(jnp.int32)
    rw_ref[...] = rw
    total = carry_ref[0:1, :] + jnp.sum(onehot, axis=0, keepdims=True)
    carry_ref[...] = jnp.broadcast_to(total, carry_ref.shape)
    cnt_ref[...] = jnp.broadcast_to(total, cnt_ref.shape).astype(jnp.int32)


def _post_call(layer, yc, yl, xc, xl, mods, w_out, g_post, g_ffn, w_r, b_r, tri):
    nb = N_TOK // TB
    lay = functools.partial(_layer_spec, layer)

    def tok(width):
        return pl.BlockSpec((TB, width), lambda i: (i, 0))

    def full(shape):
        return pl.BlockSpec(shape, lambda i: (0,) * len(shape))

    return pl.pallas_call(
        _post_kernel,
        grid=(nb,),
        in_specs=[_ctx_spec(D), _lat_spec(D), _ctx_spec(D), _lat_spec(D), _mod_spec(layer),
                  lay((D, D)), lay((1, D)), lay((1, D)), lay((D, LANES)), lay((1, LANES)),
                  full((TB, TB))],
        out_specs=(tok(D), pl.BlockSpec((TB * SUB, LANES), lambda i: (i, 0)),
                   pl.BlockSpec((None, 2 * TOP_K, TB), lambda i: (i, 0, 0)),
                   tok(LANES), full((8, LANES))),
        out_shape=(jax.ShapeDtypeStruct((N_TOK, D), F32),
                   jax.ShapeDtypeStruct((N_TOK * SUB, LANES), F32),
                   jax.ShapeDtypeStruct((nb, 2 * TOP_K, TB), jnp.int32),
                   jax.ShapeDtypeStruct((N_TOK, LANES), F32),
                   jax.ShapeDtypeStruct((8, LANES), jnp.int32)),
        scratch_shapes=[pltpu.VMEM((8, LANES), F32)],
        compiler_params=pltpu.CompilerParams(dimension_semantics=("arbitrary",),
                                             vmem_limit_bytes=VMEM_LIMIT),
        name="post_mix_router",
    )(yc, yl, xc, xl, mods, w_out, g_post, g_ffn, w_r, b_r, tri)


def _plan_kernel(cnt_ref, tab_ref, dest_ref, cend_ref, pend_ref, nu_ref, be_ref, first_ref, half_ref, nxt_ref,
                 nxt2_ref, slot_ref, start_s, next_s):
    nxt = jnp.int32(-1)
    for e in reversed(range(N_EXPERTS)):
        next_s[e] = nxt
        nxt = jnp.where(cnt_ref[0, e] > 0, jnp.int32(e), nxt)

    start = jnp.int32(0)
    run = jnp.int32(0)
    last = jnp.int32(0)
    for e in range(N_EXPERTS):
        c = cnt_ref[0, e]
        n_blk = lax.shift_right_logical(c + (RB - 1), RB.bit_length() - 1)
        start_s[e] = start
        cend_ref[e] = start * RB + c
        pend_ref[e] = (start + n_blk) * RB
        few = (c - (n_blk - 1) * RB) <= RB // 2
        n1 = next_s[e]
        n2 = jnp.where(n1 >= 0, next_s[jnp.maximum(n1, 0)], jnp.int32(-1))
        slot = lax.rem(run, W_SLOTS)

        def fill(j, carry, e=e, n1=n1, n2=n2, slot=slot, start=start, n_blk=n_blk, few=few):
            be_ref[j] = jnp.int32(e)
            first_ref[j] = (j == start).astype(jnp.int32)
            half_ref[j] = jnp.logical_and(j == start + n_blk - 1, few).astype(jnp.int32)
            nxt_ref[j] = n1
            nxt2_ref[j] = n2
            slot_ref[j] = slot
            return carry

        lax.fori_loop(start, start + n_blk, fill, 0)
        owns = n_blk > 0
        last = jnp.where(owns, jnp.int32(e), last)
        run = run + owns.astype(jnp.int32)
        start = start + n_blk
    nu_ref[0] = start

    def tail(j, carry):
        be_ref[j] = last
        first_ref[j] = jnp.int32(0)
        half_ref[j] = jnp.int32(0)
        nxt_ref[j] = jnp.int32(-1)
        nxt2_ref[j] = jnp.int32(-1)
        slot_ref[j] = jnp.int32(0)
        return carry

    lax.fori_loop(start, N_BLOCKS, tail, 0)

    idx = tab_ref[:, 0:TOP_K, :]
    base = jnp.zeros(idx.shape, jnp.int32)
    for e in range(N_EXPERTS):
        base = jnp.where(idx == e, start_s[e] * RB, base)
    dest_ref[...] = base + tab_ref[:, TOP_K:2 * TOP_K, :]


def _plan_call(cnt, tab):
    nb = N_TOK // TB
    smem = pl.BlockSpec(memory_space=pltpu.SMEM)
    blocks = jax.ShapeDtypeStruct((N_BLOCKS,), jnp.int32)
    return pl.pallas_call(
        _plan_kernel,
        in_specs=[smem, pl.BlockSpec(memory_space=pltpu.VMEM)],
        out_specs=(pl.BlockSpec(memory_space=pltpu.VMEM),) + (smem,) * 9,
        out_shape=(jax.ShapeDtypeStruct((nb, TOP_K, TB), jnp.int32),
                   jax.ShapeDtypeStruct((N_EXPERTS,), jnp.int32),
                   jax.ShapeDtypeStruct((N_EXPERTS,), jnp.int32),
                   jax.ShapeDtypeStruct((1,), jnp.int32),
                   blocks, blocks, blocks, blocks, blocks, blocks),
        scratch_shapes=[pltpu.SMEM((N_EXPERTS,), jnp.int32), pltpu.SMEM((N_EXPERTS,), jnp.int32)],
        name="plan_rows",
    )(cnt, tab)


def _dispatch_kernel(cend_ref, pend_ref, nu_ref, dest_ref, h2_ref, xs_ref, zero_ref, sem, zsem):
    i = pl.program_id(0)

    def zero_padding(act):
        for e in range(N_EXPERTS):
            c_end = cend_ref[e]
            aligned = lax.shift_left(lax.shift_right_logical(c_end + 7, 3), 3)
            for r in range(7):

                @pl.when(c_end + r < aligned)
                def _():
                    act(pltpu.make_async_copy(_tile_rows(zero_ref, 0), _tile_rows(xs_ref, c_end + r), zsem))

            n = pend_ref[e] - aligned
            off = aligned
            for size in (128, 64, 32, 16, 8):

                @pl.when((n & size) != 0)
                def _():
                    act(pltpu.make_async_copy(_tile_rows(zero_ref, 0, size), _tile_rows(xs_ref, off, size),
                                              zsem))

                off = off + (n & size)

        def tail(b, carry):
            act(pltpu.make_async_copy(zero_ref, _tile_rows(xs_ref, b * RB, RB), zsem))
            return carry

        lax.fori_loop(nu_ref[0], N_BLOCKS, tail, 0)

    @pl.when(i == 0)
    def _():
        zero_ref[...] = jnp.zeros(zero_ref.shape, F32)
        zero_padding(lambda cp: cp.start())

    def body(t, carry):
        for k in range(TOP_K):
            pltpu.make_async_copy(_tile_rows(h2_ref, t), _tile_rows(xs_ref, dest_ref[0, k * TB + t]),
                                  sem).start(priority=k % 2)
        return carry

    lax.fori_loop(0, TB, body, 0, unroll=8)
    for k in range(TOP_K):
        pltpu.make_async_copy(h2_ref, _tile_rows(xs_ref, 0, TB), sem).wait()

    @pl.when(i == pl.num_programs(0) - 1)
    def _():
        zero_padding(lambda cp: cp.wait())


def _dispatch_call(cend, pend, n_used, dest, h2):
    nb = N_TOK // TB
    grid_spec = pltpu.PrefetchScalarGridSpec(
        num_scalar_prefetch=3,
        grid=(nb,),
        in_specs=[
            pl.BlockSpec((None, 1, TOP_K * TB), lambda i, c, p, n: (i, 0, 0), memory_space=pltpu.SMEM),
            pl.BlockSpec((TB * SUB, LANES), lambda i, c, p, n: (i, 0)),
        ],
        out_specs=pl.BlockSpec(memory_space=pl.ANY),
        scratch_shapes=[pltpu.VMEM((RB * SUB, LANES), F32), pltpu.SemaphoreType.DMA,
                        pltpu.SemaphoreType.DMA],
    )
    return pl.pallas_call(
        _dispatch_kernel,
        grid_spec=grid_spec,
        out_shape=jax.ShapeDtypeStruct((N_ROWS * SUB, LANES), F32),
        compiler_params=pltpu.CompilerParams(dimension_semantics=("arbitrary",),
                                             vmem_limit_bytes=VMEM_LIMIT),
        name="dispatch_rows",
    )(cend, pend, n_used, dest, h2)


def _moe_kernel(be_ref, first_ref, half_ref, nxt_ref, nxt2_ref, slot_ref, nu_ref, xs_ref, wgu_hbm, bgu_ref,
                wdn_hbm, bdn_ref, o_ref, wgu_f, wdn_f, sems, *, layer):
    b = pl.program_id(0)
    used = b < nu_ref[0]

    def fetch(e, s):
        return (pltpu.make_async_copy(wgu_hbm.at[layer, e], wgu_f.at[s], sems.at[0, s]),
                pltpu.make_async_copy(wdn_hbm.at[layer, e], wdn_f.at[s], sems.at[1, s]))

    @pl.when(b == 0)
    def _():
        for cp in fetch(be_ref[0], 0):
            cp.start(priority=1)

        @pl.when(nxt_ref[0] >= 0)
        def _():
            for cp in fetch(nxt_ref[0], 1):
                cp.start(priority=1)

    @pl.when(first_ref[b] == 1)
    def _():
        s = slot_ref[b]
        for cp in fetch(be_ref[b], s):
            cp.wait()

        @pl.when(nxt2_ref[b] >= 0)
        def _():
            for cp in fetch(nxt2_ref[b], (s + 2) % W_SLOTS):
                cp.start(priority=1)

    def experts(rows):
        s = slot_ref[b]
        x = _load_tiled(xs_ref, rows).astype(BF16)
        gu = jnp.dot(x, wgu_f[s].astype(BF16), preferred_element_type=F32) + bgu_ref[...]
        g = jnp.minimum(gu[:, :D_FF], SWIGLU_LIMIT)
        l = jnp.clip(gu[:, D_FF:], -SWIGLU_LIMIT, SWIGLU_LIMIT)
        a = g * jax.nn.sigmoid(SWIGLU_ALPHA * g) * (l + 1.0)
        y = jnp.dot(a.astype(BF16), wdn_f[s].astype(BF16), preferred_element_type=F32) + bdn_ref[...]
        _store_tiled(o_ref, y)
        if rows < RB:
            o_ref[rows * SUB:, :] = jnp.zeros(((RB - rows) * SUB, LANES), o_ref.dtype)

    half = half_ref[b] == 1

    @pl.when(jnp.logical_and(used, jnp.logical_not(half)))
    def _():
        experts(RB)

    @pl.when(jnp.logical_and(used, half))
    def _():
        experts(RB // 2)

    @pl.when(jnp.logical_not(used))
    def _():
        o_ref[...] = jnp.zeros(o_ref.shape, o_ref.dtype)


def _moe_call(layer, tables, xs, w_gu, b_gu, w_dn, b_dn):
    def rows_in(b, be, fi, ha, nx, n2, sl, nu):
        return (jnp.minimum(b, nu[0] - 1), 0)

    def rows_out(b, be, fi, ha, nx, n2, sl, nu):
        return (b, 0)

    def expert(b, be, fi, ha, nx, n2, sl, nu):
        return (layer, be[b], 0, 0)

    grid_spec = pltpu.PrefetchScalarGridSpec(
        num_scalar_prefetch=7,
        grid=(N_BLOCKS,),
        in_specs=[
            pl.BlockSpec((RB * SUB, LANES), rows_in),
            pl.BlockSpec(memory_space=pl.ANY),
            pl.BlockSpec((None, None, 1, 2 * D_FF), expert),
            pl.BlockSpec(memory_space=pl.ANY),
            pl.BlockSpec((None, None, 1, D), expert),
        ],
        out_specs=pl.BlockSpec((RB * SUB, LANES), rows_out),
        scratch_shapes=[pltpu.VMEM((W_SLOTS, D, 2 * D_FF), F32), pltpu.VMEM((W_SLOTS, D_FF, D), F32),
                        pltpu.SemaphoreType.DMA((2, W_SLOTS))],
    )
    return pl.pallas_call(
        functools.partial(_moe_kernel, layer=layer),
        grid_spec=grid_spec,
        out_shape=jax.ShapeDtypeStruct((N_ROWS * SUB, LANES), F32),
        compiler_params=pltpu.CompilerParams(dimension_semantics=("arbitrary",),
                                             vmem_limit_bytes=VMEM_LIMIT),
        name="moe_experts",
    )(*tables, xs, w_gu, b_gu, w_dn, b_dn)


def _final_kernel(*refs):
    _combine_body(pl.program_id(0), *refs)


def _combine_body(i, dcur_ref, dnxt_ref, x1_ref, ys_ref, rw_ref, mod_ref, g_ref, oc_ref, ol_ref, buf, sems):
    nb = pl.num_programs(0)
    slot = i % 2

    def gather(dest_ref, s):
        def body(t, carry):
            for k in range(TOP_K):
                pltpu.make_async_copy(_tile_rows(ys_ref, dest_ref[0, k * TB + t]),
                                      _tile_rows(buf.at[s], k * TB + t), sems.at[s]).start(priority=k % 2)
            return carry

        lax.fori_loop(0, TB, body, 0, unroll=8)

    @pl.when(i == 0)
    def _():
        gather(dcur_ref, 0)

    @pl.when(i + 1 < nb)
    def _():
        gather(dnxt_ref, 1 - slot)

    pltpu.make_async_copy(_tile_rows(ys_ref, 0, TOP_K * TB), buf.at[slot], sems.at[slot]).wait()

    rw = rw_ref[...]
    y = jnp.zeros((TB, D), F32)
    for k in range(TOP_K):
        yk = jnp.concatenate([buf[slot, pl.ds(k * TB * SUB + s, TB, stride=SUB), :] for s in range(SUB)],
                             axis=1)
        y = y + yk * rw[:, k:k + 1]
    x2 = x1_ref[...] + _rms(y, g_ref[...] * mod_ref[5:6, :])

    @pl.when(i < N_CTX // TB)
    def _():
        oc_ref[...] = x2

    @pl.when(i >= N_CTX // TB)
    def _():
        ol_ref[...] = x2


def _final_call(layer, dest_flat, x1, ys, rw, mods, g_post_ffn):
    nb = N_TOK // TB
    return pl.pallas_call(
        _final_kernel,
        grid=(nb,),
        in_specs=[pl.BlockSpec((None, 1, TOP_K * TB), lambda i: (i, 0, 0), memory_space=pltpu.SMEM),
                  pl.BlockSpec((None, 1, TOP_K * TB), lambda i: (jnp.minimum(i + 1, nb - 1), 0, 0),
                               memory_space=pltpu.SMEM),
                  pl.BlockSpec((TB, D), lambda i: (i, 0)),
                  pl.BlockSpec(memory_space=pl.ANY),
                  pl.BlockSpec((TB, LANES), lambda i: (i, 0)),
                  _mod_spec(layer), _layer_spec(layer, (1, D))],
        out_specs=(_ctx_spec(D), _lat_spec(D)),
        out_shape=(jax.ShapeDtypeStruct((N_CTX, D), F32), jax.ShapeDtypeStruct((N_LAT, D), F32)),
        scratch_shapes=[pltpu.VMEM((2, TOP_K * TB * SUB, LANES), F32), pltpu.SemaphoreType.DMA((2,))],
        compiler_params=pltpu.CompilerParams(dimension_semantics=("arbitrary",),
                                             vmem_limit_bytes=VMEM_LIMIT),
        name="combine_residual",
    )(dest_flat, dest_flat, x1, ys, rw, mods, g_post_ffn)


def _dft_tables(t):
    j = np.arange(t, dtype=np.int64)
    ang = 2.0 * np.pi * ((j[:, None] * j[None, :]) % t) / t
    return (np.cos(ang) / math.sqrt(t)).astype(np.float32), (np.sin(ang) / math.sqrt(t)).astype(np.float32)


def _channel_dft():
    c = np.arange(F_GROUP_DIM, dtype=np.int64)
    ang = 2.0 * np.pi * ((c[:, None] * c[None, :]) % F_GROUP_DIM) / F_GROUP_DIM
    eye = np.eye(F_GROUPS)
    bdc = np.kron(eye, np.cos(ang)) / math.sqrt(F_GROUP_DIM)
    bds = np.kron(eye, np.sin(ang)) / math.sqrt(F_GROUP_DIM)
    return bdc.astype(np.float32), bds.astype(np.float32)


def _rope_tables():
    pos = np.arange(LAT_T)
    n = ROPE // 4
    inv_freq = np.power(np.float32(ROPE_BASE), -np.arange(n, dtype=np.float32) / np.float32(n))
    ang_r = (pos // GRID_W).astype(np.float32)[:, None] * inv_freq
    ang_c = (pos % GRID_W).astype(np.float32)[:, None] * inv_freq
    cos = np.concatenate([np.cos(ang_r), np.cos(ang_r), np.cos(ang_c), np.cos(ang_c),
                          np.ones((LAT_T, LANES - ROPE))], axis=1)
    sin = np.concatenate([-np.sin(ang_r), np.sin(ang_r), -np.sin(ang_c), np.sin(ang_c),
                          np.zeros((LAT_T, LANES - ROPE))], axis=1)
    return cos.astype(np.float32), sin.astype(np.float32)


def kernel(x_prompt, x_sample, cache_ckv, cache_krope, c, c_ctx, w_ada, b_ada, g_pre_mix, g_post_mix, g_pre_ffn, g_post_ffn, w_in, g_sgu, w_spatial, b_spatial, g_q, w_uq, g_kv, w_ukv, w_out, w_router, b_router, w_gate_up, b_gate_up, w_down, b_down):
    xc, xl = x_prompt.reshape(N_CTX, D), x_sample.reshape(N_LAT, D)

    cond = jnp.concatenate([c_ctx[None, :], c, jnp.zeros((5, D), F32)], axis=0)
    mods = _mod_call(cond.T, w_ada, b_ada.reshape(DEPTH, 1, 6 * D))
    mods = mods[:, :3].reshape(DEPTH, 3, 6, D)

    gmat = jnp.asarray(np.kron(np.eye(A_HEADS), np.full((A_HEAD_DIM, A_HEAD_DIM), 1.0 / A_HEAD_DIM)),
                       dtype=BF16)
    bdc_np, bds_np = _channel_dft()
    bdc, bds = jnp.asarray(bdc_np).astype(BF16), jnp.asarray(bds_np).astype(BF16)
    dft = {}
    for t in (CTX_T, LAT_T):
        ct_np, st_np = _dft_tables(t)
        dft[t] = (jnp.asarray(ct_np).astype(BF16), jnp.asarray(st_np).astype(BF16))
    cos_np, sin_np = _rope_tables()
    cos_t, sin_t = jnp.asarray(cos_np), jnp.asarray(sin_np)
    tri = jnp.asarray(np.tril(np.ones((TB, TB), np.float32), k=-1)).astype(BF16)

    b_gu = b_gate_up.reshape(DEPTH, N_EXPERTS, 1, 2 * D_FF)
    b_dn = b_down.reshape(DEPTH, N_EXPERTS, 1, D)
    w_in_p = jnp.pad(w_in, ((0, 0), (0, 0), (0, IN_PAD - w_in.shape[-1]))).astype(BF16)
    w_uq_p = jnp.pad(w_uq.reshape(DEPTH, Q_LORA, HEADS, NOPE + ROPE),
                     ((0, 0), (0, 0), (0, 0), (0, HEAD_PAD - NOPE - ROPE)))
    w_uq_p = w_uq_p.reshape(DEPTH, Q_LORA, HEADS * HEAD_PAD).astype(BF16)
    w_ukv_b = w_ukv.astype(BF16)
    ws = w_spatial.astype(BF16)
    bs_full = jnp.repeat(jnp.swapaxes(b_spatial, 1, 2), A_HEAD_DIM, axis=2)
    ckr_p = jnp.pad(cache_krope, ((0, 0), (0, 0), (0, 0), (0, LANES - ROPE))).astype(BF16)
    w_out_b = w_out.astype(BF16)
    w_r = jnp.pad(w_router, ((0, 0), (0, 0), (0, LANES - N_EXPERTS)))
    b_r = jnp.pad(b_router, ((0, 0), (0, LANES - N_EXPERTS)), constant_values=NEG)[:, None, :]
    row = lambda g: g[:, None, :]

    pre_params = (mods, row(g_pre_mix), w_in_p, row(g_sgu), gmat, row(g_q), w_uq_p, row(g_kv), w_ukv_b,
                  cos_t, sin_t)

    ckv_layers, krope_layers = [], []
    for i in range(DEPTH):
        u, vn, zf, q, kv, kr, ckv, zkr = _pre_call(i, xc, xl, pre_params)
        ckv_layers.append(ckv[:N_CTX].reshape(N_CTX_B, CTX_T, KV_LORA))
        krope_layers.append(zkr[:N_CTX, :ROPE].reshape(N_CTX_B, CTX_T, ROPE))

        mix_in = (u, vn, zf, q, kv, kr)
        y_ctx = _mix_call(i, mix_in, (ws, bs_full, bdc, bds) + dft[CTX_T], CTX_T, N_CTX_B, 0)
        y_lat = _mix_call(i, mix_in, (ws, bs_full, bdc, bds) + dft[LAT_T], LAT_T, N_LAT_B, N_CTX,
                          cache=(cache_ckv, ckr_p, w_ukv_b))

        x1, h2, tab, rw, cnt = _post_call(i, y_ctx, y_lat, xc, xl, mods, w_out_b,
                                          row(g_post_mix), row(g_pre_ffn), w_r, b_r, tri)

        dest, cend, pend, n_used, block_e, first, half, nxt, nxt2, slot = _plan_call(cnt, tab)
        dest_flat = dest.reshape(N_TOK // TB, 1, TOP_K * TB)
        xs = _dispatch_call(cend, pend, n_used, dest_flat, h2)
        ys = _moe_call(i, (block_e, first, half, nxt, nxt2, slot, n_used), xs, w_gate_up, b_gu, w_down, b_dn)
        xc, xl = _final_call(i, dest_flat, x1, ys, rw, mods, row(g_post_ffn))

    y_prompt = xc.reshape(N_CTX_B, CTX_T, D)
    y_sample = xl.reshape(N_LAT_B, LAT_T, D)
    return (y_prompt, y_sample, jnp.stack(ckv_layers, axis=1), jnp.stack(krope_layers, axis=1))
```

```python
import functools
import math

import jax
import jax.numpy as jnp
import numpy as np
from jax import lax
from jax.experimental import pallas as pl
from jax.experimental.pallas import tpu as pltpu

F32 = jnp.float32
BF16 = jnp.bfloat16

D = 1024
N_CTX_B, CTX_T = 16, 256
N_LAT_B, LAT_T = 2, 1024
PAST = 512
N_CTX = N_CTX_B * CTX_T
N_LAT = N_LAT_B * LAT_T
N_TOK = N_CTX + N_LAT
DEPTH = 2
GRID_W = 64
EPS = 1e-6
A_HEADS, A_HEAD_DIM, A_WIDTH, CHUNK = 4, 64, 256, 128
F_GROUPS, F_GROUP_DIM, F_WIDTH = 4, 64, 256
HEADS, Q_LORA, KV_LORA, NOPE, ROPE, V_DIM = 4, 256, 128, 128, 64, 128
HEAD_PAD = 256
IN_PAD = 1280
N_EXPERTS, TOP_K, D_FF = 32, 4, 1024
SWIGLU_LIMIT, SWIGLU_ALPHA = 7.0, 1.702
ROPE_BASE = 10000.0

TB = 512
QB_MAX = 512
RB = 256
W_SLOTS = 3
LANES = 128
SUB = D // LANES
N_BLOCKS = N_TOK * TOP_K // RB + N_EXPERTS
N_ROWS = N_BLOCKS * RB
NEG = -3.0e38
VMEM_LIMIT = 56 * 1024 * 1024


def _rms(x, g):
    return x * lax.rsqrt(jnp.mean(x * x, axis=-1, keepdims=True) + EPS) * g


def _split_dot(v, m):
    hi = v.astype(BF16)
    lo = (v - hi.astype(F32)).astype(BF16)
    return (jnp.dot(hi, m, preferred_element_type=F32)
            + jnp.dot(lo, m, preferred_element_type=F32))


def _dot_nt(a, b):
    return lax.dot_general(a, b, (((1,), (1,)), ((), ())), preferred_element_type=F32)


def _store_tiled(ref, x):
    rows = x.shape[0]
    for s in range(SUB):
        ref[pl.ds(s, rows, stride=SUB), :] = x[:, s * LANES:(s + 1) * LANES]


def _load_tiled(ref, rows):
    return jnp.concatenate([ref[pl.ds(s, rows, stride=SUB), :] for s in range(SUB)], axis=1)


def _tile_rows(ref, row, n_rows=1):
    return ref.at[pl.ds(pl.multiple_of(row * SUB, SUB), n_rows * SUB)]


def _mod_kernel(ct_ref, w_ref, b_ref, o_ref):
    ct = ct_ref[...]
    s = ct * jax.nn.sigmoid(ct)
    w = w_ref[...]
    o_ref[...] = jnp.zeros(o_ref.shape, F32)
    for r in range(3):
        o_ref[r:r + 1, :] = jnp.sum(w * s[:, r:r + 1], axis=0, keepdims=True) + b_ref[...]


def _mod_call(cond_t, w_ada, b_ada):
    cb = 2048
    return pl.pallas_call(
        _mod_kernel,
        grid=(DEPTH, 6 * D // cb),
        in_specs=[
            pl.BlockSpec((D, 8), lambda l, j: (0, 0)),
            pl.BlockSpec((None, D, cb), lambda l, j: (l, 0, j)),
            pl.BlockSpec((None, 1, cb), lambda l, j: (l, 0, j)),
        ],
        out_specs=pl.BlockSpec((None, 8, cb), lambda l, j: (l, 0, j)),
        out_shape=jax.ShapeDtypeStruct((DEPTH, 8, 6 * D), F32),
        compiler_params=pltpu.CompilerParams(dimension_semantics=("parallel", "parallel")),
        name="modulation",
    )(cond_t, w_ada, b_ada)


def _mod_index(i):
    first_lat = N_CTX // TB
    return jnp.where(i < first_lat, 0, 1 + (i - first_lat) // (LAT_T // TB))


def _ctx_spec(width):
    return pl.BlockSpec((TB, width), lambda i: (jnp.minimum(i, N_CTX // TB - 1), 0))


def _lat_spec(width):
    return pl.BlockSpec((TB, width), lambda i: (jnp.maximum(i - N_CTX // TB, 0), 0))


def _pick(i, ctx_ref, lat_ref):
    return jnp.where(i >= N_CTX // TB, lat_ref[...], ctx_ref[...])


def _layer_spec(layer, shape):
    zeros = (0,) * len(shape)
    return pl.BlockSpec((None,) + tuple(shape), lambda *_: (layer,) + zeros)


def _mod_spec(layer):
    return pl.BlockSpec((None, None, 6, D), lambda i: (layer, _mod_index(i), 0, 0))


def _swap_halves(x, lane):
    w = x.shape[-1]
    fwd = pltpu.roll(x, w - 16, 1)
    bwd = pltpu.roll(x, 16, 1)
    return jnp.where((lane & 31) < 16, fwd, bwd)


def _pre_kernel(xc_ref, xl_ref, *refs):
    i = pl.program_id(0)
    _pre_body(i, _pick(i, xc_ref, xl_ref), *refs)


def _pre_body(i, x, mod_ref, gpre_ref, win_ref, gsgu_ref, gmat_ref, gq_ref, wuq_ref,
              gkv_ref, wukv_ref, cos_ref, sin_ref,
              u_ref, vn_ref, zf_ref, q_ref, kv_ref, kr_ref, ckv_ref, zkr_ref):
    is_lat = i >= N_CTX // TB
    h = _rms(x, gpre_ref[...] * (1.0 + mod_ref[1:2, :])) + mod_ref[0:1, :]
    z = jnp.dot(h.astype(BF16), win_ref[...], preferred_element_type=F32)

    ga = jax.nn.gelu(z[:, :2 * A_WIDTH])
    u_ref[...] = ga[:, :A_WIDTH]
    v = ga[:, A_WIDTH:]
    gmat = gmat_ref[...]
    dv = v - _split_dot(v, gmat)
    var = _split_dot(dv * dv, gmat)
    vn_ref[...] = (dv * lax.rsqrt(var + EPS) * gsgu_ref[...]).astype(BF16)

    zf_ref[...] = z[:, 512:768].astype(BF16)

    cos = jnp.where(is_lat, cos_ref[...], 1.0)
    sin = jnp.where(is_lat, sin_ref[...], 0.0)
    lane = lax.broadcasted_iota(jnp.int32, (TB, LANES), 1)

    qn = _rms(z[:, 768:1024], gq_ref[...])
    scale = (NOPE + ROPE) ** -0.5
    q = jnp.dot(qn.astype(BF16), wuq_ref[...], preferred_element_type=F32) * scale
    for hd in range(HEADS):
        base = hd * HEAD_PAD
        q_ref[:, base:base + NOPE] = q[:, base:base + NOPE].astype(BF16)
        qr = q[:, base + NOPE:base + HEAD_PAD]
        q_ref[:, base + NOPE:base + HEAD_PAD] = (qr * cos + _swap_halves(qr, lane) * sin).astype(BF16)

    ckv = _rms(z[:, 1024:1152], gkv_ref[...])
    ckv_ref[...] = ckv
    kv_ref[...] = jnp.dot(ckv.astype(BF16), wukv_ref[...], preferred_element_type=F32).astype(BF16)

    zkr = z[:, 1152:1280]
    zkr_ref[...] = zkr
    kr_ref[...] = (zkr * cos + _swap_halves(zkr, lane) * sin).astype(BF16)


def _pre_specs(layer):
    lay = functools.partial(_layer_spec, layer)
    first_lat = N_CTX // TB
    pos_blocks = LAT_T // TB

    def tok(width):
        return pl.BlockSpec((TB, width), lambda i: (i, 0))

    def full(shape):
        return pl.BlockSpec(shape, lambda i: (0,) * len(shape))

    def rope_map(i):
        return (jnp.where(i >= first_lat, (i - first_lat) % pos_blocks, 0), 0)

    in_specs = [
        _mod_spec(layer),
        lay((1, D)), lay((D, IN_PAD)), lay((1, A_WIDTH)), full((A_WIDTH, A_WIDTH)),
        lay((1, Q_LORA)), lay((Q_LORA, HEADS * HEAD_PAD)),
        lay((1, KV_LORA)), lay((KV_LORA, HEADS * (NOPE + V_DIM))),
        pl.BlockSpec((TB, LANES), rope_map), pl.BlockSpec((TB, LANES), rope_map),
    ]
    out_specs = (tok(A_WIDTH), tok(A_WIDTH), tok(F_WIDTH), tok(HEADS * HEAD_PAD),
                 tok(HEADS * (NOPE + V_DIM)), tok(LANES), tok(KV_LORA), tok(LANES))
    out_shape = (
        jax.ShapeDtypeStruct((N_TOK, A_WIDTH), F32),
        jax.ShapeDtypeStruct((N_TOK, A_WIDTH), BF16),
        jax.ShapeDtypeStruct((N_TOK, F_WIDTH), BF16),
        jax.ShapeDtypeStruct((N_TOK, HEADS * HEAD_PAD), BF16),
        jax.ShapeDtypeStruct((N_TOK, HEADS * (NOPE + V_DIM)), BF16),
        jax.ShapeDtypeStruct((N_TOK, LANES), BF16),
        jax.ShapeDtypeStruct((N_TOK, KV_LORA), F32),
        jax.ShapeDtypeStruct((N_TOK, LANES), F32),
    )
    return in_specs, out_specs, out_shape


def _pre_call(layer, xc, xl, pre_params):
    in_specs, out_specs, out_shape = _pre_specs(layer)
    return pl.pallas_call(
        _pre_kernel,
        grid=(N_TOK // TB,),
        in_specs=[_ctx_spec(D), _lat_spec(D)] + in_specs,
        out_specs=out_specs,
        out_shape=out_shape,
        compiler_params=pltpu.CompilerParams(dimension_semantics=("parallel",),
                                             vmem_limit_bytes=VMEM_LIMIT),
        name="pre_mix",
    )(xc, xl, *pre_params)


def _mix_kernel(*refs, has_cache, qb):
    if has_cache:
        (u_ref, vn_ref, zf_ref, q_ref, kv_ref, kr_ref, ws_ref, bs_ref, bdc_ref, bds_ref,
         ct_ref, st_ref, cckv_ref, ckr_ref, wukv_ref, o_ref) = refs
    else:
        (u_ref, vn_ref, zf_ref, q_ref, kv_ref, kr_ref, ws_ref, bs_ref, bdc_ref, bds_ref,
         ct_ref, st_ref, o_ref) = refs

    lane = lax.broadcasted_iota(jnp.int32, (CHUNK, A_WIDTH), 1)
    for c in range(qb // CHUNK):
        rows = slice(c * CHUNK, (c + 1) * CHUNK)
        vch = vn_ref[rows, :]
        s = bs_ref[...]
        for g in range(A_HEADS):
            sg = jnp.dot(ws_ref[g], vch, preferred_element_type=F32)
            in_head = (lane >= g * A_HEAD_DIM) & (lane < (g + 1) * A_HEAD_DIM)
            s = s + jnp.where(in_head, sg, 0.0)
        o_ref[rows, 0:A_WIDTH] = (u_ref[rows, :] * s).astype(BF16)

    zf = zf_ref[...]
    zc = jnp.dot(zf, bdc_ref[...], preferred_element_type=F32).astype(BF16)
    zs = jnp.dot(zf, bds_ref[...], preferred_element_type=F32).astype(BF16)
    yf = (jnp.dot(ct_ref[...], zc, preferred_element_type=F32)
          - jnp.dot(st_ref[...], zs, preferred_element_type=F32))
    o_ref[:, A_WIDTH:A_WIDTH + F_WIDTH] = yf.astype(BF16)

    kr = kr_ref[...]
    if has_cache:
        kvc = jnp.dot(cckv_ref[...].astype(BF16), wukv_ref[...],
                      preferred_element_type=F32).astype(BF16)
        krc = ckr_ref[...]
    for hd in range(HEADS):
        qh = q_ref[:, hd * HEAD_PAD:(hd + 1) * HEAD_PAD]
        kb = hd * (NOPE + V_DIM)
        kh = jnp.concatenate([kv_ref[:, kb:kb + NOPE], kr], axis=1)
        vh = kv_ref[:, kb + NOPE:kb + NOPE + V_DIM]
        s = _dot_nt(qh, kh)
        m = jnp.max(s, axis=-1, keepdims=True)
        if has_cache:
            khc = jnp.concatenate([kvc[:, kb:kb + NOPE], krc], axis=1)
            vhc = kvc[:, kb + NOPE:kb + NOPE + V_DIM]
            sc = _dot_nt(qh, khc)
            m = jnp.maximum(m, jnp.max(sc, axis=-1, keepdims=True))
        e = jnp.exp(s - m)
        den = jnp.sum(e, axis=-1, keepdims=True)
        o = jnp.dot(e.astype(BF16), vh, preferred_element_type=F32)
        if has_cache:
            ec = jnp.exp(sc - m)
            den = den + jnp.sum(ec, axis=-1, keepdims=True)
            o = o + jnp.dot(ec.astype(BF16), vhc, preferred_element_type=F32)
        ob = A_WIDTH + F_WIDTH + hd * V_DIM
        o_ref[:, ob:ob + V_DIM] = (o * (1.0 / den)).astype(BF16)


def _mix_call(layer, pre, consts, seq_t, n_batch, tok_off, cache=None):
    u, vn, zf, q, kv, kr = pre
    ws, bs_full, bdc, bds, ct, st = consts
    lay = functools.partial(_layer_spec, layer)
    qb = min(seq_t, QB_MAX)
    nq = seq_t // qb
    qoff = tok_off // qb
    soff = tok_off // seq_t

    def qrow(width):
        return pl.BlockSpec((qb, width), lambda b, j: (qoff + b * nq + j, 0))

    def srow(width):
        return pl.BlockSpec((seq_t, width), lambda b, j: (soff + b, 0))

    def full(shape):
        return pl.BlockSpec(shape, lambda b, j: (0,) * len(shape))

    in_specs = [qrow(A_WIDTH), qrow(A_WIDTH), srow(F_WIDTH), qrow(HEADS * HEAD_PAD),
                srow(HEADS * (NOPE + V_DIM)), srow(LANES),
                lay((A_HEADS, CHUNK, CHUNK)), lay((CHUNK, A_WIDTH)),
                full((F_WIDTH, F_WIDTH)), full((F_WIDTH, F_WIDTH)),
                pl.BlockSpec((qb, seq_t), lambda b, j: (j, 0)),
                pl.BlockSpec((qb, seq_t), lambda b, j: (j, 0))]
    args = [u, vn, zf, q, kv, kr, ws, bs_full, bdc, bds, ct, st]
    if cache is not None:
        cckv, ckr, wukv = cache
        in_specs += [pl.BlockSpec((None, None, PAST, KV_LORA), lambda b, j: (b, layer, 0, 0)),
                     pl.BlockSpec((None, None, PAST, LANES), lambda b, j: (b, layer, 0, 0)),
                     lay((KV_LORA, HEADS * (NOPE + V_DIM)))]
        args += [cckv, ckr, wukv]
    return pl.pallas_call(
        functools.partial(_mix_kernel, has_cache=cache is not None, qb=qb),
        grid=(n_batch, nq),
        in_specs=in_specs,
        out_specs=pl.BlockSpec((qb, D), lambda b, j: (b * nq + j, 0)),
        out_shape=jax.ShapeDtypeStruct((n_batch * seq_t, D), BF16),
        compiler_params=pltpu.CompilerParams(dimension_semantics=("parallel", "parallel"),
                                             vmem_limit_bytes=VMEM_LIMIT),
        name="mix_lat" if cache is not None else "mix_ctx",
    )(*args)


def _post_kernel(yc_ref, yl_ref, xc_ref, xl_ref, mod_ref, wout_ref, gpost_ref, gffn_ref, wr_ref, br_ref,
                 tri_ref, x1_ref, h2_ref, tab_ref, rw_ref, cnt_ref, carry_ref):
    i = pl.program_id(0)

    @pl.when(i == 0)
    def _():
        carry_ref[...] = jnp.zeros(carry_ref.shape, F32)

    y = jnp.dot(_pick(i, yc_ref, yl_ref), wout_ref[...], preferred_element_type=F32)
    x1 = _pick(i, xc_ref, xl_ref) + _rms(y, gpost_ref[...] * mod_ref[2:3, :])
    x1_ref[...] = x1
    h2 = _rms(x1, gffn_ref[...] * (1.0 + mod_ref[4:5, :])) + mod_ref[3:4, :]
    _store_tiled(h2_ref, h2)

    wr = wr_ref[...]
    wr_hi = wr.astype(BF16)
    wr_lo = (wr - wr_hi.astype(F32)).astype(BF16)
    h_hi = h2.astype(BF16)
    h_lo = (h2 - h_hi.astype(F32)).astype(BF16)
    hi_terms = jnp.dot(h_hi, jnp.concatenate([wr_hi, wr_lo], axis=1), preferred_element_type=F32)
    logits = (hi_terms[:, :LANES] + jnp.dot(h_lo, wr_hi, preferred_element_type=F32)
              + hi_terms[:, LANES:]) + br_ref[...]

    lane = lax.broadcasted_iota(jnp.int32, (TB, LANES), 1)
    lane_f = lane.astype(F32)
    work = logits
    idx, val = [], []
    for _ in range(TOP_K):
        m = jnp.max(work, axis=-1, keepdims=True)
        ik = jnp.min(jnp.where(work == m, lane_f, float(LANES)), axis=-1, keepdims=True)
        idx.append(ik)
        val.append(m)
        work = jnp.where(lane_f == ik, NEG, work)
    ex = [jnp.exp(v - val[0]) for v in val]
    den = ex[0] + ex[1] + ex[2] + ex[3]

    onehot = jnp.zeros((TB, LANES), F32)
    for k in range(TOP_K):
        onehot = onehot + jnp.where(lane_f == idx[k], 1.0, 0.0)
    before = jnp.dot(tri_ref[...], onehot.astype(BF16), preferred_element_type=F32) + carry_ref[0:1, :]
    ri = jnp.zeros((TB, LANES), F32)
    rw = jnp.zeros((TB, LANES), F32)
    for k in range(TOP_K):
        rank_k = jnp.sum(jnp.where(lane_f == idx[k], before, 0.0), axis=-1, keepdims=True)
        ri = ri + jnp.where(lane == k, idx[k], 0.0) + jnp.where(lane == TOP_K + k, rank_k, 0.0)
        rw = rw + jnp.where(lane == k, ex[k] / den, 0.0)
    tab_ref[...] = ri.T[0:2 * TOP_K, :].astype(jnp.int32)
    rw_ref[...] = rw
    total = carry_ref[0:1, :] + jnp.sum(onehot, axis=0, keepdims=True)
    carry_ref[...] = jnp.broadcast_to(total, carry_ref.shape)
    cnt_ref[...] = jnp.broadcast_to(total, cnt_ref.shape).astype(jnp.int32)


def _post_call(layer, yc, yl, xc, xl, mods, w_out, g_post, g_ffn, w_r, b_r, tri):
    nb = N_TOK // TB
    lay = functools.partial(_layer_spec, layer)

    def tok(width):
        return pl.BlockSpec((TB, width), lambda i: (i, 0))

    def full(shape):
        return pl.BlockSpec(shape, lambda i: (0,) * len(shape))

    return pl.pallas_call(
        _post_kernel,
        grid=(nb,),
        in_specs=[_ctx_spec(D), _lat_spec(D), _ctx_spec(D), _lat_spec(D), _mod_spec(layer),
                  lay((D, D)), lay((1, D)), lay((1, D)), lay((D, LANES)), lay((1, LANES)),
                  full((TB, TB))],
        out_specs=(tok(D), pl.BlockSpec((TB * SUB, LANES), lambda i: (i, 0)),
                   pl.BlockSpec((None, 2 * TOP_K, TB), lambda i: (i, 0, 0)),
                   tok(LANES), full((8, LANES))),
        out_shape=(jax.ShapeDtypeStruct((N_TOK, D), F32),
                   jax.ShapeDtypeStruct((N_TOK * SUB, LANES), F32),
                   jax.ShapeDtypeStruct((nb, 2 * TOP_K, TB), jnp.int32),
                   jax.ShapeDtypeStruct((N_TOK, LANES), F32),
                   jax.ShapeDtypeStruct((8, LANES), jnp.int32)),
        scratch_shapes=[pltpu.VMEM((8, LANES), F32)],
        compiler_params=pltpu.CompilerParams(dimension_semantics=("arbitrary",),
                                             vmem_limit_bytes=VMEM_LIMIT),
        name="post_mix_router",
    )(yc, yl, xc, xl, mods, w_out, g_post, g_ffn, w_r, b_r, tri)


def _plan_kernel(cnt_ref, tab_ref, dest_ref, cend_ref, pend_ref, nu_ref, be_ref, first_ref, half_ref, nxt_ref,
                 nxt2_ref, slot_ref, start_s, next_s):
    nxt = jnp.int32(-1)
    for e in reversed(range(N_EXPERTS)):
        next_s[e] = nxt
        nxt = jnp.where(cnt_ref[0, e] > 0, jnp.int32(e), nxt)

    start = jnp.int32(0)
    run = jnp.int32(0)
    last = jnp.int32(0)
    for e in range(N_EXPERTS):
        c = cnt_ref[0, e]
        n_blk = lax.shift_right_logical(c + (RB - 1), RB.bit_length() - 1)
        start_s[e] = start
        cend_ref[e] = start * RB + c
        pend_ref[e] = (start + n_blk) * RB
        few = (c - (n_blk - 1) * RB) <= RB // 2
        n1 = next_s[e]
        n2 = jnp.where(n1 >= 0, next_s[jnp.maximum(n1, 0)], jnp.int32(-1))
        slot = lax.rem(run, W_SLOTS)

        def fill(j, carry, e=e, n1=n1, n2=n2, slot=slot, start=start, n_blk=n_blk, few=few):
            be_ref[j] = jnp.int32(e)
            first_ref[j] = (j == start).astype(jnp.int32)
            half_ref[j] = jnp.logical_and(j == start + n_blk - 1, few).astype(jnp.int32)
            nxt_ref[j] = n1
            nxt2_ref[j] = n2
            slot_ref[j] = slot
            return carry

        lax.fori_loop(start, start + n_blk, fill, 0)
        owns = n_blk > 0
        last = jnp.where(owns, jnp.int32(e), last)
        run = run + owns.astype(jnp.int32)
        start = start + n_blk
    nu_ref[0] = start

    def tail(j, carry):
        be_ref[j] = last
        first_ref[j] = jnp.int32(0)
        half_ref[j] = jnp.int32(0)
        nxt_ref[j] = jnp.int32(-1)
        nxt2_ref[j] = jnp.int32(-1)
        slot_ref[j] = jnp.int32(0)
        return carry

    lax.fori_loop(start, N_BLOCKS, tail, 0)

    idx = tab_ref[:, 0:TOP_K, :]
    base = jnp.zeros(idx.shape, jnp.int32)
    for e in range(N_EXPERTS):
        base = jnp.where(idx == e, start_s[e] * RB, base)
    dest_ref[...] = base + tab_ref[:, TOP_K:2 * TOP_K, :]


def _plan_call(cnt, tab):
    nb = N_TOK // TB
    smem = pl.BlockSpec(memory_space=pltpu.SMEM)
    blocks = jax.ShapeDtypeStruct((N_BLOCKS,), jnp.int32)
    return pl.pallas_call(
        _plan_kernel,
        in_specs=[smem, pl.BlockSpec(memory_space=pltpu.VMEM)],
        out_specs=(pl.BlockSpec(memory_space=pltpu.VMEM),) + (smem,) * 9,
        out_shape=(jax.ShapeDtypeStruct((nb, TOP_K, TB), jnp.int32),
                   jax.ShapeDtypeStruct((N_EXPERTS,), jnp.int32),
                   jax.ShapeDtypeStruct((N_EXPERTS,), jnp.int32),
                   jax.ShapeDtypeStruct((1,), jnp.int32),
                   blocks, blocks, blocks, blocks, blocks, blocks),
        scratch_shapes=[pltpu.SMEM((N_EXPERTS,), jnp.int32), pltpu.SMEM((N_EXPERTS,), jnp.int32)],
        name="plan_rows",
    )(cnt, tab)


def _dispatch_kernel(cend_ref, pend_ref, nu_ref, dest_ref, h2_ref, xs_ref, zero_ref, sem, zsem):
    i = pl.program_id(0)

    def zero_padding(act):
        for e in range(N_EXPERTS):
            c_end = cend_ref[e]
            aligned = lax.shift_left(lax.shift_right_logical(c_end + 7, 3), 3)
            for r in range(7):

                @pl.when(c_end + r < aligned)
                def _():
                    act(pltpu.make_async_copy(_tile_rows(zero_ref, 0), _tile_rows(xs_ref, c_end + r), zsem))

            n = pend_ref[e] - aligned
            off = aligned
            for size in (128, 64, 32, 16, 8):

                @pl.when((n & size) != 0)
                def _():
                    act(pltpu.make_async_copy(_tile_rows(zero_ref, 0, size), _tile_rows(xs_ref, off, size),
                                              zsem))

                off = off + (n & size)

        def tail(b, carry):
            act(pltpu.make_async_copy(zero_ref, _tile_rows(xs_ref, b * RB, RB), zsem))
            return carry

        lax.fori_loop(nu_ref[0], N_BLOCKS, tail, 0)

    @pl.when(i == 0)
    def _():
        zero_ref[...] = jnp.zeros(zero_ref.shape, F32)
        zero_padding(lambda cp: cp.start())

    def body(t, carry):
        for k in range(TOP_K):
            pltpu.make_async_copy(_tile_rows(h2_ref, t), _tile_rows(xs_ref, dest_ref[0, k * TB + t]),
                                  sem).start(priority=k % 2)
        return carry

    lax.fori_loop(0, TB, body, 0, unroll=8)
    for k in range(TOP_K):
        pltpu.make_async_copy(h2_ref, _tile_rows(xs_ref, 0, TB), sem).wait()

    @pl.when(i == pl.num_programs(0) - 1)
    def _():
        zero_padding(lambda cp: cp.wait())


def _dispatch_call(cend, pend, n_used, dest, h2):
    nb = N_TOK // TB
    grid_spec = pltpu.PrefetchScalarGridSpec(
        num_scalar_prefetch=3,
        grid=(nb,),
        in_specs=[
            pl.BlockSpec((None, 1, TOP_K * TB), lambda i, c, p, n: (i, 0, 0), memory_space=pltpu.SMEM),
            pl.BlockSpec((TB * SUB, LANES), lambda i, c, p, n: (i, 0)),
        ],
        out_specs=pl.BlockSpec(memory_space=pl.ANY),
        scratch_shapes=[pltpu.VMEM((RB * SUB, LANES), F32), pltpu.SemaphoreType.DMA,
                        pltpu.SemaphoreType.DMA],
    )
    return pl.pallas_call(
        _dispatch_kernel,
        grid_spec=grid_spec,
        out_shape=jax.ShapeDtypeStruct((N_ROWS * SUB, LANES), F32),
        compiler_params=pltpu.CompilerParams(dimension_semantics=("arbitrary",),
                                             vmem_limit_bytes=VMEM_LIMIT),
        name="dispatch_rows",
    )(cend, pend, n_used, dest, h2)


def _moe_kernel(be_ref, first_ref, half_ref, nxt_ref, nxt2_ref, slot_ref, nu_ref, xs_ref, wgu_hbm, bgu_ref,
                wdn_hbm, bdn_ref, o_ref, wgu_f, wdn_f, sems, *, layer):
    b = pl.program_id(0)
    used = b < nu_ref[0]

    def fetch(e, s):
        return (pltpu.make_async_copy(wgu_hbm.at[layer, e], wgu_f.at[s], sems.at[0, s]),
                pltpu.make_async_copy(wdn_hbm.at[layer, e], wdn_f.at[s], sems.at[1, s]))

    @pl.when(b == 0)
    def _():
        for cp in fetch(be_ref[0], 0):
            cp.start(priority=1)

        @pl.when(nxt_ref[0] >= 0)
        def _():
            for cp in fetch(nxt_ref[0], 1):
                cp.start(priority=1)

    @pl.when(first_ref[b] == 1)
    def _():
        s = slot_ref[b]
        for cp in fetch(be_ref[b], s):
            cp.wait()

        @pl.when(nxt2_ref[b] >= 0)
        def _():
            for cp in fetch(nxt2_ref[b], (s + 2) % W_SLOTS):
                cp.start(priority=1)

    def experts(rows):
        s = slot_ref[b]
        x = _load_tiled(xs_ref, rows).astype(BF16)
        gu = jnp.dot(x, wgu_f[s].astype(BF16), preferred_element_type=F32) + bgu_ref[...]
        g = jnp.minimum(gu[:, :D_FF], SWIGLU_LIMIT)
        l = jnp.clip(gu[:, D_FF:], -SWIGLU_LIMIT, SWIGLU_LIMIT)
        a = g * jax.nn.sigmoid(SWIGLU_ALPHA * g) * (l + 1.0)
        y = jnp.dot(a.astype(BF16), wdn_f[s].astype(BF16), preferred_element_type=F32) + bdn_ref[...]
        _store_tiled(o_ref, y)
        if rows < RB:
            o_ref[rows * SUB:, :] = jnp.zeros(((RB - rows) * SUB, LANES), o_ref.dtype)

    half = half_ref[b] == 1

    @pl.when(jnp.logical_and(used, jnp.logical_not(half)))
    def _():
        experts(RB)

    @pl.when(jnp.logical_and(used, half))
    def _():
        experts(RB // 2)

    @pl.when(jnp.logical_not(used))
    def _():
        o_ref[...] = jnp.zeros(o_ref.shape, o_ref.dtype)


def _moe_call(layer, tables, xs, w_gu, b_gu, w_dn, b_dn):
    def rows_in(b, be, fi, ha, nx, n2, sl, nu):
        return (jnp.minimum(b, nu[0] - 1), 0)

    def rows_out(b, be, fi, ha, nx, n2, sl, nu):
        return (b, 0)

    def expert(b, be, fi, ha, nx, n2, sl, nu):
        return (layer, be[b], 0, 0)

    grid_spec = pltpu.PrefetchScalarGridSpec(
        num_scalar_prefetch=7,
        grid=(N_BLOCKS,),
        in_specs=[
            pl.BlockSpec((RB * SUB, LANES), rows_in),
            pl.BlockSpec(memory_space=pl.ANY),
            pl.BlockSpec((None, None, 1, 2 * D_FF), expert),
            pl.BlockSpec(memory_space=pl.ANY),
            pl.BlockSpec((None, None, 1, D), expert),
        ],
        out_specs=pl.BlockSpec((RB * SUB, LANES), rows_out),
        scratch_shapes=[pltpu.VMEM((W_SLOTS, D, 2 * D_FF), F32), pltpu.VMEM((W_SLOTS, D_FF, D), F32),
                        pltpu.SemaphoreType.DMA((2, W_SLOTS))],
    )
    return pl.pallas_call(
        functools.partial(_moe_kernel, layer=layer),
        grid_spec=grid_spec,
        out_shape=jax.ShapeDtypeStruct((N_ROWS * SUB, LANES), F32),
        compiler_params=pltpu.CompilerParams(dimension_semantics=("arbitrary",),
                                             vmem_limit_bytes=VMEM_LIMIT),
        name="moe_experts",
    )(*tables, xs, w_gu, b_gu, w_dn, b_dn)


def _final_kernel(*refs):
    _combine_body(pl.program_id(0), *refs)


def _combine_body(i, dcur_ref, dnxt_ref, x1_ref, ys_ref, rw_ref, mod_ref, g_ref, oc_ref, ol_ref, buf, sems):
    nb = pl.num_programs(0)
    slot = i % 2

    def gather(dest_ref, s):
        def body(t, carry):
            for k in range(TOP_K):
                pltpu.make_async_copy(_tile_rows(ys_ref, dest_ref[0, k * TB + t]),
                                      _tile_rows(buf.at[s], k * TB + t), sems.at[s]).start(priority=k % 2)
            return carry

        lax.fori_loop(0, TB, body, 0, unroll=8)

    @pl.when(i == 0)
    def _():
        gather(dcur_ref, 0)

    @pl.when(i + 1 < nb)
    def _():
        gather(dnxt_ref, 1 - slot)

    pltpu.make_async_copy(_tile_rows(ys_ref, 0, TOP_K * TB), buf.at[slot], sems.at[slot]).wait()

    rw = rw_ref[...]
    y = jnp.zeros((TB, D), F32)
    for k in range(TOP_K):
        yk = jnp.concatenate([buf[slot, pl.ds(k * TB * SUB + s, TB, stride=SUB), :] for s in range(SUB)],
                             axis=1)
        y = y + yk * rw[:, k:k + 1]
    x2 = x1_ref[...] + _rms(y, g_ref[...] * mod_ref[5:6, :])

    @pl.when(i < N_CTX // TB)
    def _():
        oc_ref[...] = x2

    @pl.when(i >= N_CTX // TB)
    def _():
        ol_ref[...] = x2


def _final_call(layer, dest_flat, x1, ys, rw, mods, g_post_ffn):
    nb = N_TOK // TB
    return pl.pallas_call(
        _final_kernel,
        grid=(nb,),
        in_specs=[pl.BlockSpec((None, 1, TOP_K * TB), lambda i: (i, 0, 0), memory_space=pltpu.SMEM),
                  pl.BlockSpec((None, 1, TOP_K * TB), lambda i: (jnp.minimum(i + 1, nb - 1), 0, 0),
                               memory_space=pltpu.SMEM),
                  pl.BlockSpec((TB, D), lambda i: (i, 0)),
                  pl.BlockSpec(memory_space=pl.ANY),
                  pl.BlockSpec((TB, LANES), lambda i: (i, 0)),
                  _mod_spec(layer), _layer_spec(layer, (1, D))],
        out_specs=(_ctx_spec(D), _lat_spec(D)),
        out_shape=(jax.ShapeDtypeStruct((N_CTX, D), F32), jax.ShapeDtypeStruct((N_LAT, D), F32)),
        scratch_shapes=[pltpu.VMEM((2, TOP_K * TB * SUB, LANES), F32), pltpu.SemaphoreType.DMA((2,))],
        compiler_params=pltpu.CompilerParams(dimension_semantics=("arbitrary",),
                                             vmem_limit_bytes=VMEM_LIMIT),
        name="combine_residual",
    )(dest_flat, dest_flat, x1, ys, rw, mods, g_post_ffn)


def _dft_tables(t):
    j = np.arange(t, dtype=np.int64)
    ang = 2.0 * np.pi * ((j[:, None] * j[None, :]) % t) / t
    return (np.cos(ang) / math.sqrt(t)).astype(np.float32), (np.sin(ang) / math.sqrt(t)).astype(np.float32)


def _channel_dft():
    c = np.arange(F_GROUP_DIM, dtype=np.int64)
    ang = 2.0 * np.pi * ((c[:, None] * c[None, :]) % F_GROUP_DIM) / F_GROUP_DIM
    eye = np.eye(F_GROUPS)
    bdc = np.kron(eye, np.cos(ang)) / math.sqrt(F_GROUP_DIM)
    bds = np.kron(eye, np.sin(ang)) / math.sqrt(F_GROUP_DIM)
    return bdc.astype(np.float32), bds.astype(np.float32)


def _rope_tables():
    pos = np.arange(LAT_T)
    n = ROPE // 4
    inv_freq = np.power(np.float32(ROPE_BASE), -np.arange(n, dtype=np.float32) / np.float32(n))
    ang_r = (pos // GRID_W).astype(np.float32)[:, None] * inv_freq
    ang_c = (pos % GRID_W).astype(np.float32)[:, None] * inv_freq
    cos = np.concatenate([np.cos(ang_r), np.cos(ang_r), np.cos(ang_c), np.cos(ang_c),
                          np.ones((LAT_T, LANES - ROPE))], axis=1)
    sin = np.concatenate([-np.sin(ang_r), np.sin(ang_r), -np.sin(ang_c), np.sin(ang_c),
                          np.zeros((LAT_T, LANES - ROPE))], axis=1)
    return cos.astype(np.float32), sin.astype(np.float32)


def kernel(x_prompt, x_sample, cache_ckv, cache_krope, c, c_ctx, w_ada, b_ada, g_pre_mix, g_post_mix, g_pre_ffn, g_post_ffn, w_in, g_sgu, w_spatial, b_spatial, g_q, w_uq, g_kv, w_ukv, w_out, w_router, b_router, w_gate_up, b_gate_up, w_down, b_down):
    xc, xl = x_prompt.reshape(N_CTX, D), x_sample.reshape(N_LAT, D)

    cond = jnp.concatenate([c_ctx[None, :], c, jnp.zeros((5, D), F32)], axis=0)
    mods = _mod_call(cond.T, w_ada, b_ada.reshape(DEPTH, 1, 6 * D))
    mods = mods[:, :3].reshape(DEPTH, 3, 6, D)

    gmat = jnp.asarray(np.kron(np.eye(A_HEADS), np.full((A_HEAD_DIM, A_HEAD_DIM), 1.0 / A_HEAD_DIM)),
                       dtype=BF16)
    bdc_np, bds_np = _channel_dft()
    bdc, bds = jnp.asarray(bdc_np).astype(BF16), jnp.asarray(bds_np).astype(BF16)
    dft = {}
    for t in (CTX_T, LAT_T):
        ct_np, st_np = _dft_tables(t)
        dft[t] = (jnp.asarray(ct_np).astype(BF16), jnp.asarray(st_np).astype(BF16))
    cos_np, sin_np = _rope_tables()
    cos_t, sin_t = jnp.asarray(cos_np), jnp.asarray(sin_np)
    tri = jnp.asarray(np.tril(np.ones((TB, TB), np.float32), k=-1)).astype(BF16)

    b_gu = b_gate_up.reshape(DEPTH, N_EXPERTS, 1, 2 * D_FF)
    b_dn = b_down.reshape(DEPTH, N_EXPERTS, 1, D)
    w_in_p = jnp.pad(w_in, ((0, 0), (0, 0), (0, IN_PAD - w_in.shape[-1]))).astype(BF16)
    w_uq_p = jnp.pad(w_uq.reshape(DEPTH, Q_LORA, HEADS, NOPE + ROPE),
                     ((0, 0), (0, 0), (0, 0), (0, HEAD_PAD - NOPE - ROPE)))
    w_uq_p = w_uq_p.reshape(DEPTH, Q_LORA, HEADS * HEAD_PAD).astype(BF16)
    w_ukv_b = w_ukv.astype(BF16)
    ws = w_spatial.astype(BF16)
    bs_full = jnp.repeat(jnp.swapaxes(b_spatial, 1, 2), A_HEAD_DIM, axis=2)
    ckr_p = jnp.pad(cache_krope, ((0, 0), (0, 0), (0, 0), (0, LANES - ROPE))).astype(BF16)
    w_out_b = w_out.astype(BF16)
    w_r = jnp.pad(w_router, ((0, 0), (0, 0), (0, LANES - N_EXPERTS)))
    b_r = jnp.pad(b_router, ((0, 0), (0, LANES - N_EXPERTS)), constant_values=NEG)[:, None, :]
    row = lambda g: g[:, None, :]

    pre_params = (mods, row(g_pre_mix), w_in_p, row(g_sgu), gmat, row(g_q), w_uq_p, row(g_kv), w_ukv_b,
                  cos_t, sin_t)

    ckv_layers, krope_layers = [], []
    for i in range(DEPTH):
        u, vn, zf, q, kv, kr, ckv, zkr = _pre_call(i, xc, xl, pre_params)
        ckv_layers.append(ckv[:N_CTX].reshape(N_CTX_B, CTX_T, KV_LORA))
        krope_layers.append(zkr[:N_CTX, :ROPE].reshape(N_CTX_B, CTX_T, ROPE))

        mix_in = (u, vn, zf, q, kv, kr)
        y_ctx = _mix_call(i, mix_in, (ws, bs_full, bdc, bds) + dft[CTX_T], CTX_T, N_CTX_B, 0)
        y_lat = _mix_call(i, mix_in, (ws, bs_full, bdc, bds) + dft[LAT_T], LAT_T, N_LAT_B, N_CTX,
                          cache=(cache_ckv, ckr_p, w_ukv_b))

        x1, h2, tab, rw, cnt = _post_call(i, y_ctx, y_lat, xc, xl, mods, w_out_b,
                                          row(g_post_mix), row(g_pre_ffn), w_r, b_r, tri)

        dest, cend, pend, n_used, block_e, first, half, nxt, nxt2, slot = _plan_call(cnt, tab)
        dest_flat = dest.reshape(N_TOK // TB, 1, TOP_K * TB)
        xs = _dispatch_call(cend, pend, n_used, dest_flat, h2)
        ys = _moe_call(i, (block_e, first, half, nxt, nxt2, slot, n_used), xs, w_gate_up, b_gu, w_down, b_dn)
        xc, xl = _final_call(i, dest_flat, x1, ys, rw, mods, row(g_post_ffn))

    y_prompt = xc.reshape(N_CTX_B, CTX_T, D)
    y_sample = xl.reshape(N_LAT_B, LAT_T, D)
    return (y_prompt, y_sample, jnp.stack(ckv_layers, axis=1), jnp.stack(krope_layers, axis=1))
```

```python
import functools
import math

import jax
import jax.numpy as jnp
import numpy as np
from jax import lax
from jax.experimental import pallas as pl
from jax.experimental.pallas import tpu as pltpu

F32 = jnp.float32
BF16 = jnp.bfloat16

D = 1024
N_CTX_B, CTX_T = 16, 256
N_LAT_B, LAT_T = 2, 1024
PAST = 512
N_CTX = N_CTX_B * CTX_T
N_LAT = N_LAT_B * LAT_T
N_TOK = N_CTX + N_LAT
DEPTH = 2
GRID_W = 64
EPS = 1e-6
A_HEADS, A_HEAD_DIM, A_WIDTH, CHUNK = 4, 64, 256, 128
F_GROUPS, F_GROUP_DIM, F_WIDTH = 4, 64, 256
HEADS, Q_LORA, KV_LORA, NOPE, ROPE, V_DIM = 4, 256, 128, 128, 64, 128
HEAD_PAD = 256
IN_PAD = 1280
N_EXPERTS, TOP_K, D_FF = 32, 4, 1024
SWIGLU_LIMIT, SWIGLU_ALPHA = 7.0, 1.702
ROPE_BASE = 10000.0

TB = 512
QB_MAX = 1024
RB = 256
W_SLOTS = 3
LANES = 128
SUB = D // LANES
N_BLOCKS = N_TOK * TOP_K // RB + N_EXPERTS
N_ROWS = N_BLOCKS * RB
NEG = -3.0e38
VMEM_LIMIT = 56 * 1024 * 1024


def _rms(x, g):
    return x * lax.rsqrt(jnp.mean(x * x, axis=-1, keepdims=True) + EPS) * g


def _split_dot(v, m):
    hi = v.astype(BF16)
    lo = (v - hi.astype(F32)).astype(BF16)
    return (jnp.dot(hi, m, preferred_element_type=F32)
            + jnp.dot(lo, m, preferred_element_type=F32))


def _dot_nt(a, b):
    return lax.dot_general(a, b, (((1,), (1,)), ((), ())), preferred_element_type=F32)


def _store_tiled(ref, x):
    rows = x.shape[0]
    for s in range(SUB):
        ref[pl.ds(s, rows, stride=SUB), :] = x[:, s * LANES:(s + 1) * LANES]


def _load_tiled(ref, rows):
    return jnp.concatenate([ref[pl.ds(s, rows, stride=SUB), :] for s in range(SUB)], axis=1)


def _tile_rows(ref, row, n_rows=1):
    return ref.at[pl.ds(pl.multiple_of(row * SUB, SUB), n_rows * SUB)]


def _mod_kernel(ct_ref, w_ref, b_ref, o_ref):
    ct = ct_ref[...]
    s = ct * jax.nn.sigmoid(ct)
    w = w_ref[...]
    o_ref[...] = jnp.zeros(o_ref.shape, F32)
    for r in range(3):
        o_ref[r:r + 1, :] = jnp.sum(w * s[:, r:r + 1], axis=0, keepdims=True) + b_ref[...]


def _mod_call(cond_t, w_ada, b_ada):
    cb = 2048
    return pl.pallas_call(
        _mod_kernel,
        grid=(DEPTH, 6 * D // cb),
        in_specs=[
            pl.BlockSpec((D, 8), lambda l, j: (0, 0)),
            pl.BlockSpec((None, D, cb), lambda l, j: (l, 0, j)),
            pl.BlockSpec((None, 1, cb), lambda l, j: (l, 0, j)),
        ],
        out_specs=pl.BlockSpec((None, 8, cb), lambda l, j: (l, 0, j)),
        out_shape=jax.ShapeDtypeStruct((DEPTH, 8, 6 * D), F32),
        compiler_params=pltpu.CompilerParams(dimension_semantics=("parallel", "parallel")),
        name="modulation",
    )(cond_t, w_ada, b_ada)


def _mod_index(i):
    first_lat = N_CTX // TB
    return jnp.where(i < first_lat, 0, 1 + (i - first_lat) // (LAT_T // TB))


def _ctx_spec(width):
    return pl.BlockSpec((TB, width), lambda i: (jnp.minimum(i, N_CTX // TB - 1), 0))


def _lat_spec(width):
    return pl.BlockSpec((TB, width), lambda i: (jnp.maximum(i - N_CTX // TB, 0), 0))


def _pick(i, ctx_ref, lat_ref):
    return jnp.where(i >= N_CTX // TB, lat_ref[...], ctx_ref[...])


def _layer_spec(layer, shape):
    zeros = (0,) * len(shape)
    return pl.BlockSpec((None,) + tuple(shape), lambda *_: (layer,) + zeros)


def _mod_spec(layer):
    return pl.BlockSpec((None, None, 6, D), lambda i: (layer, _mod_index(i), 0, 0))


def _swap_halves(x, lane):
    w = x.shape[-1]
    fwd = pltpu.roll(x, w - 16, 1)
    bwd = pltpu.roll(x, 16, 1)
    return jnp.where((lane & 31) < 16, fwd, bwd)


def _pre_kernel(xc_ref, xl_ref, *refs):
    i = pl.program_id(0)
    _pre_body(i, _pick(i, xc_ref, xl_ref), *refs)


def _pre_body(i, x, mod_ref, gpre_ref, win_ref, gsgu_ref, gmat_ref, gq_ref, wuq_ref,
              gkv_ref, wukv_ref, cos_ref, sin_ref,
              u_ref, vn_ref, zf_ref, q_ref, kv_ref, kr_ref, ckv_ref, zkr_ref):
    is_lat = i >= N_CTX // TB
    h = _rms(x, gpre_ref[...] * (1.0 + mod_ref[1:2, :])) + mod_ref[0:1, :]
    z = jnp.dot(h.astype(BF16), win_ref[...], preferred_element_type=F32)

    ga = jax.nn.gelu(z[:, :2 * A_WIDTH])
    u_ref[...] = ga[:, :A_WIDTH]
    v = ga[:, A_WIDTH:]
    gmat = gmat_ref[...]
    dv = v - _split_dot(v, gmat)
    var = _split_dot(dv * dv, gmat)
    vn_ref[...] = (dv * lax.rsqrt(var + EPS) * gsgu_ref[...]).astype(BF16)

    zf_ref[...] = z[:, 512:768].astype(BF16)

    cos = jnp.where(is_lat, cos_ref[...], 1.0)
    sin = jnp.where(is_lat, sin_ref[...], 0.0)
    lane = lax.broadcasted_iota(jnp.int32, (TB, LANES), 1)

    qn = _rms(z[:, 768:1024], gq_ref[...])
    scale = (NOPE + ROPE) ** -0.5
    q = jnp.dot(qn.astype(BF16), wuq_ref[...], preferred_element_type=F32) * scale
    for hd in range(HEADS):
        base = hd * HEAD_PAD
        q_ref[:, base:base + NOPE] = q[:, base:base + NOPE].astype(BF16)
        qr = q[:, base + NOPE:base + HEAD_PAD]
        q_ref[:, base + NOPE:base + HEAD_PAD] = (qr * cos + _swap_halves(qr, lane) * sin).astype(BF16)

    ckv = _rms(z[:, 1024:1152], gkv_ref[...])
    ckv_ref[...] = ckv
    kv_ref[...] = jnp.dot(ckv.astype(BF16), wukv_ref[...], preferred_element_type=F32).astype(BF16)

    zkr = z[:, 1152:1280]
    zkr_ref[...] = zkr
    kr_ref[...] = (zkr * cos + _swap_halves(zkr, lane) * sin).astype(BF16)


def _pre_specs(layer):
    lay = functools.partial(_layer_spec, layer)
    first_lat = N_CTX // TB
    pos_blocks = LAT_T // TB

    def tok(width):
        return pl.BlockSpec((TB, width), lambda i: (i, 0))

    def full(shape):
        return pl.BlockSpec(shape, lambda i: (0,) * len(shape))

    def rope_map(i):
        return (jnp.where(i >= first_lat, (i - first_lat) % pos_blocks, 0), 0)

    in_specs = [
        _mod_spec(layer),
        lay((1, D)), lay((D, IN_PAD)), lay((1, A_WIDTH)), full((A_WIDTH, A_WIDTH)),
        lay((1, Q_LORA)), lay((Q_LORA, HEADS * HEAD_PAD)),
        lay((1, KV_LORA)), lay((KV_LORA, HEADS * (NOPE + V_DIM))),
        pl.BlockSpec((TB, LANES), rope_map), pl.BlockSpec((TB, LANES), rope_map),
    ]
    out_specs = (tok(A_WIDTH), tok(A_WIDTH), tok(F_WIDTH), tok(HEADS * HEAD_PAD),
                 tok(HEADS * (NOPE + V_DIM)), tok(LANES), tok(KV_LORA), tok(LANES))
    out_shape = (
        jax.ShapeDtypeStruct((N_TOK, A_WIDTH), F32),
        jax.ShapeDtypeStruct((N_TOK, A_WIDTH), BF16),
        jax.ShapeDtypeStruct((N_TOK, F_WIDTH), BF16),
        jax.ShapeDtypeStruct((N_TOK, HEADS * HEAD_PAD), BF16),
        jax.ShapeDtypeStruct((N_TOK, HEADS * (NOPE + V_DIM)), BF16),
        jax.ShapeDtypeStruct((N_TOK, LANES), BF16),
        jax.ShapeDtypeStruct((N_TOK, KV_LORA), F32),
        jax.ShapeDtypeStruct((N_TOK, LANES), F32),
    )
    return in_specs, out_specs, out_shape


def _pre_call(layer, xc, xl, pre_params):
    in_specs, out_specs, out_shape = _pre_specs(layer)
    return pl.pallas_call(
        _pre_kernel,
        grid=(N_TOK // TB,),
        in_specs=[_ctx_spec(D), _lat_spec(D)] + in_specs,
        out_specs=out_specs,
        out_shape=out_shape,
        compiler_params=pltpu.CompilerParams(dimension_semantics=("parallel",),
                                             vmem_limit_bytes=VMEM_LIMIT),
        name="pre_mix",
    )(xc, xl, *pre_params)


def _mix_kernel(*refs, has_cache, qb, seq_t, n_seq):
    if has_cache:
        (u_ref, vn_ref, zf_ref, q_ref, kv_ref, kr_ref, ws_ref, bs_ref, bdc_ref, bds_ref,
         ct_ref, st_ref, cckv_ref, ckr_ref, wukv_ref, o_ref) = refs
    else:
        (u_ref, vn_ref, zf_ref, q_ref, kv_ref, kr_ref, ws_ref, bs_ref, bdc_ref, bds_ref,
         ct_ref, st_ref, o_ref) = refs

    lane = lax.broadcasted_iota(jnp.int32, (CHUNK, A_WIDTH), 1)
    for c in range(n_seq * qb // CHUNK):
        rows = slice(c * CHUNK, (c + 1) * CHUNK)
        vch = vn_ref[rows, :]
        s = bs_ref[...]
        for g in range(A_HEADS):
            sg = jnp.dot(ws_ref[g], vch, preferred_element_type=F32)
            in_head = (lane >= g * A_HEAD_DIM) & (lane < (g + 1) * A_HEAD_DIM)
            s = s + jnp.where(in_head, sg, 0.0)
        o_ref[rows, 0:A_WIDTH] = (u_ref[rows, :] * s).astype(BF16)

    if has_cache:
        kvc = jnp.dot(cckv_ref[...].astype(BF16), wukv_ref[...],
                      preferred_element_type=F32).astype(BF16)
        krc = ckr_ref[...]

    for sq in range(n_seq):
        qrows = slice(sq * qb, (sq + 1) * qb)
        srows = slice(sq * seq_t, (sq + 1) * seq_t)

        zf = zf_ref[srows, :]
        zc = jnp.dot(zf, bdc_ref[...], preferred_element_type=F32).astype(BF16)
        zs = jnp.dot(zf, bds_ref[...], preferred_element_type=F32).astype(BF16)
        yf = (jnp.dot(ct_ref[...], zc, preferred_element_type=F32)
              - jnp.dot(st_ref[...], zs, preferred_element_type=F32))
        o_ref[qrows, A_WIDTH:A_WIDTH + F_WIDTH] = yf.astype(BF16)

        kr = kr_ref[srows, :]
        for hd in range(HEADS):
            qh = q_ref[qrows, hd * HEAD_PAD:(hd + 1) * HEAD_PAD]
            kb = hd * (NOPE + V_DIM)
            kh = jnp.concatenate([kv_ref[srows, kb:kb + NOPE], kr], axis=1)
            vh = kv_ref[srows, kb + NOPE:kb + NOPE + V_DIM]
            s = _dot_nt(qh, kh)
            m = jnp.max(s, axis=-1, keepdims=True)
            if has_cache:
                khc = jnp.concatenate([kvc[:, kb:kb + NOPE], krc], axis=1)
                vhc = kvc[:, kb + NOPE:kb + NOPE + V_DIM]
                sc = _dot_nt(qh, khc)
                m = jnp.maximum(m, jnp.max(sc, axis=-1, keepdims=True))
            e = jnp.exp(s - m)
            den = jnp.sum(e, axis=-1, keepdims=True)
            o = jnp.dot(e.astype(BF16), vh, preferred_element_type=F32)
            if has_cache:
                ec = jnp.exp(sc - m)
                den = den + jnp.sum(ec, axis=-1, keepdims=True)
                o = o + jnp.dot(ec.astype(BF16), vhc, preferred_element_type=F32)
            ob = A_WIDTH + F_WIDTH + hd * V_DIM
            o_ref[qrows, ob:ob + V_DIM] = (o * (1.0 / den)).astype(BF16)


def _mix_call(layer, pre, consts, seq_t, n_batch, tok_off, cache=None):
    u, vn, zf, q, kv, kr = pre
    ws, bs_full, bdc, bds, ct, st = consts
    lay = functools.partial(_layer_spec, layer)
    qb = min(seq_t, QB_MAX)
    nq = seq_t // qb
    n_seq = QB_MAX // seq_t if nq == 1 else 1
    q_rows, s_rows = n_seq * qb, n_seq * seq_t
    qoff = tok_off // q_rows
    soff = tok_off // s_rows

    def qrow(width):
        return pl.BlockSpec((q_rows, width), lambda b, j: (qoff + b * nq + j, 0))

    def srow(width):
        return pl.BlockSpec((s_rows, width), lambda b, j: (soff + b, 0))

    def full(shape):
        return pl.BlockSpec(shape, lambda b, j: (0,) * len(shape))

    in_specs = [qrow(A_WIDTH), qrow(A_WIDTH), srow(F_WIDTH), qrow(HEADS * HEAD_PAD),
                srow(HEADS * (NOPE + V_DIM)), srow(LANES),
                lay((A_HEADS, CHUNK, CHUNK)), lay((CHUNK, A_WIDTH)),
                full((F_WIDTH, F_WIDTH)), full((F_WIDTH, F_WIDTH)),
                pl.BlockSpec((qb, seq_t), lambda b, j: (j, 0)),
                pl.BlockSpec((qb, seq_t), lambda b, j: (j, 0))]
    args = [u, vn, zf, q, kv, kr, ws, bs_full, bdc, bds, ct, st]
    if cache is not None:
        assert n_seq == 1
        cckv, ckr, wukv = cache
        in_specs += [pl.BlockSpec((None, None, PAST, KV_LORA), lambda b, j: (b, layer, 0, 0)),
                     pl.BlockSpec((None, None, PAST, LANES), lambda b, j: (b, layer, 0, 0)),
                     lay((KV_LORA, HEADS * (NOPE + V_DIM)))]
        args += [cckv, ckr, wukv]
    return pl.pallas_call(
        functools.partial(_mix_kernel, has_cache=cache is not None, qb=qb, seq_t=seq_t, n_seq=n_seq),
        grid=(n_batch // n_seq, nq),
        in_specs=in_specs,
        out_specs=pl.BlockSpec((q_rows, D), lambda b, j: (b * nq + j, 0)),
        out_shape=jax.ShapeDtypeStruct((n_batch * seq_t, D), BF16),
        compiler_params=pltpu.CompilerParams(dimension_semantics=("parallel", "parallel"),
                                             vmem_limit_bytes=VMEM_LIMIT),
        name="mix_lat" if cache is not None else "mix_ctx",
    )(*args)


def _post_kernel(yc_ref, yl_ref, xc_ref, xl_ref, mod_ref, wout_ref, gpost_ref, gffn_ref, wr_ref, br_ref,
                 tri_ref, x1_ref, h2_ref, tab_ref, rw_ref, cnt_ref, carry_ref):
    i = pl.program_id(0)

    @pl.when(i == 0)
    def _():
        carry_ref[...] = jnp.zeros(carry_ref.shape, F32)

    y = jnp.dot(_pick(i, yc_ref, yl_ref), wout_ref[...], preferred_element_type=F32)
    x1 = _pick(i, xc_ref, xl_ref) + _rms(y, gpost_ref[...] * mod_ref[2:3, :])
    x1_ref[...] = x1
    h2 = _rms(x1, gffn_ref[...] * (1.0 + mod_ref[4:5, :])) + mod_ref[3:4, :]
    _store_tiled(h2_ref, h2)

    wr = wr_ref[...]
    wr_hi = wr.astype(BF16)
    wr_lo = (wr - wr_hi.astype(F32)).astype(BF16)
    h_hi = h2.astype(BF16)
    h_lo = (h2 - h_hi.astype(F32)).astype(BF16)
    hi_terms = jnp.dot(h_hi, jnp.concatenate([wr_hi, wr_lo], axis=1), preferred_element_type=F32)
    logits = (hi_terms[:, :LANES] + jnp.dot(h_lo, wr_hi, preferred_element_type=F32)
              + hi_terms[:, LANES:]) + br_ref[...]

    lane = lax.broadcasted_iota(jnp.int32, (TB, LANES), 1)
    lane_f = lane.astype(F32)
    work = logits
    idx, val = [], []
    for _ in range(TOP_K):
        m = jnp.max(work, axis=-1, keepdims=True)
        ik = jnp.min(jnp.where(work == m, lane_f, float(LANES)), axis=-1, keepdims=True)
        idx.append(ik)
        val.append(m)
        work = jnp.where(lane_f == ik, NEG, work)
    ex = [jnp.exp(v - val[0]) for v in val]
    den = ex[0] + ex[1] + ex[2] + ex[3]

    onehot = jnp.zeros((TB, LANES), F32)
    for k in range(TOP_K):
        onehot = onehot + jnp.where(lane_f == idx[k], 1.0, 0.0)
    before = jnp.dot(tri_ref[...], onehot.astype(BF16), preferred_element_type=F32) + carry_ref[0:1, :]
    ri = jnp.zeros((TB, LANES), F32)
    rw = jnp.zeros((TB, LANES), F32)
    for k in range(TOP_K):
        rank_k = jnp.sum(jnp.where(lane_f == idx[k], before, 0.0), axis=-1, keepdims=True)
        ri = ri + jnp.where(lane == k, idx[k], 0.0) + jnp.where(lane == TOP_K + k, rank_k, 0.0)
        rw = rw + jnp.where(lane == k, ex[k] / den, 0.0)
    tab_ref[...] = ri.T[0:2 * TOP_K, :].astype(jnp.int32)
    rw_ref[...] = rw
    total = carry_ref[0:1, :] + jnp.sum(onehot, axis=0, keepdims=True)
    carry_ref[...] = jnp.broadcast_to(total, carry_ref.shape)
    cnt_ref[...] = jnp.broadcast_to(total, cnt_ref.shape).astype(jnp.int32)


def _post_call(layer, yc, yl, xc, xl, mods, w_out, g_post, g_ffn, w_r, b_r, tri):
    nb = N_TOK // TB
    lay = functools.partial(_layer_spec, layer)

    def tok(width):
        return pl.BlockSpec((TB, width), lambda i: (i, 0))

    def full(shape):
        return pl.BlockSpec(shape, lambda i: (0,) * len(shape))

    return pl.pallas_call(
        _post_kernel,
        grid=(nb,),
        in_specs=[_ctx_spec(D), _lat_spec(D), _ctx_spec(D), _lat_spec(D), _mod_spec(layer),
                  lay((D, D)), lay((1, D)), lay((1, D)), lay((D, LANES)), lay((1, LANES)),
                  full((TB, TB))],
        out_specs=(tok(D), pl.BlockSpec((TB * SUB, LANES), lambda i: (i, 0)),
                   pl.BlockSpec((None, 2 * TOP_K, TB), lambda i: (i, 0, 0)),
                   tok(LANES), full((8, LANES))),
        out_shape=(jax.ShapeDtypeStruct((N_TOK, D), F32),
                   jax.ShapeDtypeStruct((N_TOK * SUB, LANES), F32),
                   jax.ShapeDtypeStruct((nb, 2 * TOP_K, TB), jnp.int32),
                   jax.ShapeDtypeStruct((N_TOK, LANES), F32),
                   jax.ShapeDtypeStruct((8, LANES), jnp.int32)),
        scratch_shapes=[pltpu.VMEM((8, LANES), F32)],
        compiler_params=pltpu.CompilerParams(dimension_semantics=("arbitrary",),
                                             vmem_limit_bytes=VMEM_LIMIT),
        name="post_mix_router",
    )(yc, yl, xc, xl, mods, w_out, g_post, g_ffn, w_r, b_r, tri)


def _plan_kernel(cnt_ref, tab_ref, dest_ref, cend_ref, pend_ref, nu_ref, be_ref, first_ref, half_ref, nxt_ref,
                 nxt2_ref, slot_ref, start_s, next_s):
    nxt = jnp.int32(-1)
    for e in reversed(range(N_EXPERTS)):
        next_s[e] = nxt
        nxt = jnp.where(cnt_ref[0, e] > 0, jnp.int32(e), nxt)

    start = jnp.int32(0)
    run = jnp.int32(0)
    last = jnp.int32(0)
    for e in range(N_EXPERTS):
        c = cnt_ref[0, e]
        n_blk = lax.shift_right_logical(c + (RB - 1), RB.bit_length() - 1)
        start_s[e] = start
        cend_ref[e] = start * RB + c
        pend_ref[e] = (start + n_blk) * RB
        few = (c - (n_blk - 1) * RB) <= RB // 2
        n1 = next_s[e]
        n2 = jnp.where(n1 >= 0, next_s[jnp.maximum(n1, 0)], jnp.int32(-1))
        slot = lax.rem(run, W_SLOTS)

        def fill(j, carry, e=e, n1=n1, n2=n2, slot=slot, start=start, n_blk=n_blk, few=few):
            be_ref[j] = jnp.int32(e)
            first_ref[j] = (j == start).astype(jnp.int32)
            half_ref[j] = jnp.logical_and(j == start + n_blk - 1, few).astype(jnp.int32)
            nxt_ref[j] = n1
            nxt2_ref[j] = n2
            slot_ref[j] = slot
            return carry

        lax.fori_loop(start, start + n_blk, fill, 0)
        owns = n_blk > 0
        last = jnp.where(owns, jnp.int32(e), last)
        run = run + owns.astype(jnp.int32)
        start = start + n_blk
    nu_ref[0] = start

    def tail(j, carry):
        be_ref[j] = last
        first_ref[j] = jnp.int32(0)
        half_ref[j] = jnp.int32(0)
        nxt_ref[j] = jnp.int32(-1)
        nxt2_ref[j] = jnp.int32(-1)
        slot_ref[j] = jnp.int32(0)
        return carry

    lax.fori_loop(start, N_BLOCKS, tail, 0)

    idx = tab_ref[:, 0:TOP_K, :]
    base = jnp.zeros(idx.shape, jnp.int32)
    for e in range(N_EXPERTS):
        base = jnp.where(idx == e, start_s[e] * RB, base)
    dest_ref[...] = base + tab_ref[:, TOP_K:2 * TOP_K, :]


def _plan_call(cnt, tab):
    nb = N_TOK // TB
    smem = pl.BlockSpec(memory_space=pltpu.SMEM)
    blocks = jax.ShapeDtypeStruct((N_BLOCKS,), jnp.int32)
    return pl.pallas_call(
        _plan_kernel,
        in_specs=[smem, pl.BlockSpec(memory_space=pltpu.VMEM)],
        out_specs=(pl.BlockSpec(memory_space=pltpu.VMEM),) + (smem,) * 9,
        out_shape=(jax.ShapeDtypeStruct((nb, TOP_K, TB), jnp.int32),
                   jax.ShapeDtypeStruct((N_EXPERTS,), jnp.int32),
                   jax.ShapeDtypeStruct((N_EXPERTS,), jnp.int32),
                   jax.ShapeDtypeStruct((1,), jnp.int32),
                   blocks, blocks, blocks, blocks, blocks, blocks),
        scratch_shapes=[pltpu.SMEM((N_EXPERTS,), jnp.int32), pltpu.SMEM((N_EXPERTS,), jnp.int32)],
        name="plan_rows",
    )(cnt, tab)


def _dispatch_kernel(cend_ref, pend_ref, nu_ref, dest_ref, h2_ref, xs_ref, zero_ref, sem, zsem):
    i = pl.program_id(0)

    def zero_padding(act):
        for e in range(N_EXPERTS):
            c_end = cend_ref[e]
            aligned = lax.shift_left(lax.shift_right_logical(c_end + 7, 3), 3)
            for r in range(7):

                @pl.when(c_end + r < aligned)
                def _():
                    act(pltpu.make_async_copy(_tile_rows(zero_ref, 0), _tile_rows(xs_ref, c_end + r), zsem))

            n = pend_ref[e] - aligned
            off = aligned
            for size in (128, 64, 32, 16, 8):

                @pl.when((n & size) != 0)
                def _():
                    act(pltpu.make_async_copy(_tile_rows(zero_ref, 0, size), _tile_rows(xs_ref, off, size),
                                              zsem))

                off = off + (n & size)

        def tail(b, carry):
            act(pltpu.make_async_copy(zero_ref, _tile_rows(xs_ref, b * RB, RB), zsem))
            return carry

        lax.fori_loop(nu_ref[0], N_BLOCKS, tail, 0)

    @pl.when(i == 0)
    def _():
        zero_ref[...] = jnp.zeros(zero_ref.shape, F32)
        zero_padding(lambda cp: cp.start())

    def body(t, carry):
        for k in range(TOP_K):
            pltpu.make_async_copy(_tile_rows(h2_ref, t), _tile_rows(xs_ref, dest_ref[0, k * TB + t]),
                                  sem).start(priority=k % 2)
        return carry

    lax.fori_loop(0, TB, body, 0, unroll=8)
    for k in range(TOP_K):
        pltpu.make_async_copy(h2_ref, _tile_rows(xs_ref, 0, TB), sem).wait()

    @pl.when(i == pl.num_programs(0) - 1)
    def _():
        zero_padding(lambda cp: cp.wait())


def _dispatch_call(cend, pend, n_used, dest, h2):
    nb = N_TOK // TB
    grid_spec = pltpu.PrefetchScalarGridSpec(
        num_scalar_prefetch=3,
        grid=(nb,),
        in_specs=[
            pl.BlockSpec((None, 1, TOP_K * TB), lambda i, c, p, n: (i, 0, 0), memory_space=pltpu.SMEM),
            pl.BlockSpec((TB * SUB, LANES), lambda i, c, p, n: (i, 0)),
        ],
        out_specs=pl.BlockSpec(memory_space=pl.ANY),
        scratch_shapes=[pltpu.VMEM((RB * SUB, LANES), F32), pltpu.SemaphoreType.DMA,
                        pltpu.SemaphoreType.DMA],
    )
    return pl.pallas_call(
        _dispatch_kernel,
        grid_spec=grid_spec,
        out_shape=jax.ShapeDtypeStruct((N_ROWS * SUB, LANES), F32),
        compiler_params=pltpu.CompilerParams(dimension_semantics=("arbitrary",),
                                             vmem_limit_bytes=VMEM_LIMIT),
        name="dispatch_rows",
    )(cend, pend, n_used, dest, h2)


def _moe_kernel(be_ref, first_ref, half_ref, nxt_ref, nxt2_ref, slot_ref, nu_ref, xs_ref, wgu_hbm, bgu_ref,
                wdn_hbm, bdn_ref, o_ref, wgu_f, wdn_f, sems, *, layer):
    b = pl.program_id(0)
    used = b < nu_ref[0]

    def fetch(e, s):
        return (pltpu.make_async_copy(wgu_hbm.at[layer, e], wgu_f.at[s], sems.at[0, s]),
                pltpu.make_async_copy(wdn_hbm.at[layer, e], wdn_f.at[s], sems.at[1, s]))

    @pl.when(b == 0)
    def _():
        for cp in fetch(be_ref[0], 0):
            cp.start(priority=1)

        @pl.when(nxt_ref[0] >= 0)
        def _():
            for cp in fetch(nxt_ref[0], 1):
                cp.start(priority=1)

    @pl.when(first_ref[b] == 1)
    def _():
        s = slot_ref[b]
        for cp in fetch(be_ref[b], s):
            cp.wait()

        @pl.when(nxt2_ref[b] >= 0)
        def _():
            for cp in fetch(nxt2_ref[b], (s + 2) % W_SLOTS):
                cp.start(priority=1)

    def experts(rows):
        s = slot_ref[b]
        x = _load_tiled(xs_ref, rows).astype(BF16)
        gu = jnp.dot(x, wgu_f[s].astype(BF16), preferred_element_type=F32) + bgu_ref[...]
        g = jnp.minimum(gu[:, :D_FF], SWIGLU_LIMIT)
        l = jnp.clip(gu[:, D_FF:], -SWIGLU_LIMIT, SWIGLU_LIMIT)
        a = g * jax.nn.sigmoid(SWIGLU_ALPHA * g) * (l + 1.0)
        y = jnp.dot(a.astype(BF16), wdn_f[s].astype(BF16), preferred_element_type=F32) + bdn_ref[...]
        _store_tiled(o_ref, y)
        if rows < RB:
            o_ref[rows * SUB:, :] = jnp.zeros(((RB - rows) * SUB, LANES), o_ref.dtype)

    half = half_ref[b] == 1

    @pl.when(jnp.logical_and(used, jnp.logical_not(half)))
    def _():
        experts(RB)

    @pl.when(jnp.logical_and(used, half))
    def _():
        experts(RB // 2)

    @pl.when(jnp.logical_not(used))
    def _():
        o_ref[...] = jnp.zeros(o_ref.shape, o_ref.dtype)


def _moe_call(layer, tables, xs, w_gu, b_gu, w_dn, b_dn):
    def rows_in(b, be, fi, ha, nx, n2, sl, nu):
        return (jnp.minimum(b, nu[0] - 1), 0)

    def rows_out(b, be, fi, ha, nx, n2, sl, nu):
        return (b, 0)

    def expert(b, be, fi, ha, nx, n2, sl, nu):
        return (layer, be[b], 0, 0)

    grid_spec = pltpu.PrefetchScalarGridSpec(
        num_scalar_prefetch=7,
        grid=(N_BLOCKS,),
        in_specs=[
            pl.BlockSpec((RB * SUB, LANES), rows_in),
            pl.BlockSpec(memory_space=pl.ANY),
            pl.BlockSpec((None, None, 1, 2 * D_FF), expert),
            pl.BlockSpec(memory_space=pl.ANY),
            pl.BlockSpec((None, None, 1, D), expert),
        ],
        out_specs=pl.BlockSpec((RB * SUB, LANES), rows_out),
        scratch_shapes=[pltpu.VMEM((W_SLOTS, D, 2 * D_FF), F32), pltpu.VMEM((W_SLOTS, D_FF, D), F32),
                        pltpu.SemaphoreType.DMA((2, W_SLOTS))],
    )
    return pl.pallas_call(
        functools.partial(_moe_kernel, layer=layer),
        grid_spec=grid_spec,
        out_shape=jax.ShapeDtypeStruct((N_ROWS * SUB, LANES), F32),
        compiler_params=pltpu.CompilerParams(dimension_semantics=("arbitrary",),
                                             vmem_limit_bytes=VMEM_LIMIT),
        name="moe_experts",
    )(*tables, xs, w_gu, b_gu, w_dn, b_dn)


def _final_kernel(*refs):
    _combine_body(pl.program_id(0), *refs)


def _combine_body(i, dcur_ref, dnxt_ref, x1_ref, ys_ref, rw_ref, mod_ref, g_ref, oc_ref, ol_ref, buf, sems):
    nb = pl.num_programs(0)
    slot = i % 2

    def gather(dest_ref, s):
        def body(t, carry):
            for k in range(TOP_K):
                pltpu.make_async_copy(_tile_rows(ys_ref, dest_ref[0, k * TB + t]),
                                      _tile_rows(buf.at[s], k * TB + t), sems.at[s]).start(priority=k % 2)
            return carry

        lax.fori_loop(0, TB, body, 0, unroll=8)

    @pl.when(i == 0)
    def _():
        gather(dcur_ref, 0)

    @pl.when(i + 1 < nb)
    def _():
        gather(dnxt_ref, 1 - slot)

    pltpu.make_async_copy(_tile_rows(ys_ref, 0, TOP_K * TB), buf.at[slot], sems.at[slot]).wait()

    rw = rw_ref[...]
    y = jnp.zeros((TB, D), F32)
    for k in range(TOP_K):
        yk = jnp.concatenate([buf[slot, pl.ds(k * TB * SUB + s, TB, stride=SUB), :] for s in range(SUB)],
                             axis=1)
        y = y + yk * rw[:, k:k + 1]
    x2 = x1_ref[...] + _rms(y, g_ref[...] * mod_ref[5:6, :])

    @pl.when(i < N_CTX // TB)
    def _():
        oc_ref[...] = x2

    @pl.when(i >= N_CTX // TB)
    def _():
        ol_ref[...] = x2


def _final_call(layer, dest_flat, x1, ys, rw, mods, g_post_ffn):
    nb = N_TOK // TB
    return pl.pallas_call(
        _final_kernel,
        grid=(nb,),
        in_specs=[pl.BlockSpec((None, 1, TOP_K * TB), lambda i: (i, 0, 0), memory_space=pltpu.SMEM),
                  pl.BlockSpec((None, 1, TOP_K * TB), lambda i: (jnp.minimum(i + 1, nb - 1), 0, 0),
                               memory_space=pltpu.SMEM),
                  pl.BlockSpec((TB, D), lambda i: (i, 0)),
                  pl.BlockSpec(memory_space=pl.ANY),
                  pl.BlockSpec((TB, LANES), lambda i: (i, 0)),
                  _mod_spec(layer), _layer_spec(layer, (1, D))],
        out_specs=(_ctx_spec(D), _lat_spec(D)),
        out_shape=(jax.ShapeDtypeStruct((N_CTX, D), F32), jax.ShapeDtypeStruct((N_LAT, D), F32)),
        scratch_shapes=[pltpu.VMEM((2, TOP_K * TB * SUB, LANES), F32), pltpu.SemaphoreType.DMA((2,))],
        compiler_params=pltpu.CompilerParams(dimension_semantics=("arbitrary",),
                                             vmem_limit_bytes=VMEM_LIMIT),
        name="combine_residual",
    )(dest_flat, dest_flat, x1, ys, rw, mods, g_post_ffn)


def _dft_tables(t):
    j = np.arange(t, dtype=np.int64)
    ang = 2.0 * np.pi * ((j[:, None] * j[None, :]) % t) / t
    return (np.cos(ang) / math.sqrt(t)).astype(np.float32), (np.sin(ang) / math.sqrt(t)).astype(np.float32)


def _channel_dft():
    c = np.arange(F_GROUP_DIM, dtype=np.int64)
    ang = 2.0 * np.pi * ((c[:, None] * c[None, :]) % F_GROUP_DIM) / F_GROUP_DIM
    eye = np.eye(F_GROUPS)
    bdc = np.kron(eye, np.cos(ang)) / math.sqrt(F_GROUP_DIM)
    bds = np.kron(eye, np.sin(ang)) / math.sqrt(F_GROUP_DIM)
    return bdc.astype(np.float32), bds.astype(np.float32)


def _rope_tables():
    pos = np.arange(LAT_T)
    n = ROPE // 4
    inv_freq = np.power(np.float32(ROPE_BASE), -np.arange(n, dtype=np.float32) / np.float32(n))
    ang_r = (pos // GRID_W).astype(np.float32)[:, None] * inv_freq
    ang_c = (pos % GRID_W).astype(np.float32)[:, None] * inv_freq
    cos = np.concatenate([np.cos(ang_r), np.cos(ang_r), np.cos(ang_c), np.cos(ang_c),
                          np.ones((LAT_T, LANES - ROPE))], axis=1)
    sin = np.concatenate([-np.sin(ang_r), np.sin(ang_r), -np.sin(ang_c), np.sin(ang_c),
                          np.zeros((LAT_T, LANES - ROPE))], axis=1)
    return cos.astype(np.float32), sin.astype(np.float32)


def kernel(x_prompt, x_sample, cache_ckv, cache_krope, c, c_ctx, w_ada, b_ada, g_pre_mix, g_post_mix, g_pre_ffn, g_post_ffn, w_in, g_sgu, w_spatial, b_spatial, g_q, w_uq, g_kv, w_ukv, w_out, w_router, b_router, w_gate_up, b_gate_up, w_down, b_down):
    xc, xl = x_prompt.reshape(N_CTX, D), x_sample.reshape(N_LAT, D)

    cond = jnp.concatenate([c_ctx[None, :], c, jnp.zeros((5, D), F32)], axis=0)
    mods = _mod_call(cond.T, w_ada, b_ada.reshape(DEPTH, 1, 6 * D))
    mods = mods[:, :3].reshape(DEPTH, 3, 6, D)

    gmat = jnp.asarray(np.kron(np.eye(A_HEADS), np.full((A_HEAD_DIM, A_HEAD_DIM), 1.0 / A_HEAD_DIM)),
                       dtype=BF16)
    bdc_np, bds_np = _channel_dft()
    bdc, bds = jnp.asarray(bdc_np).astype(BF16), jnp.asarray(bds_np).astype(BF16)
    dft = {}
    for t in (CTX_T, LAT_T):
        ct_np, st_np = _dft_tables(t)
        dft[t] = (jnp.asarray(ct_np).astype(BF16), jnp.asarray(st_np).astype(BF16))
    cos_np, sin_np = _rope_tables()
    cos_t, sin_t = jnp.asarray(cos_np), jnp.asarray(sin_np)
    tri = jnp.asarray(np.tril(np.ones((TB, TB), np.float32), k=-1)).astype(BF16)

    b_gu = b_gate_up.reshape(DEPTH, N_EXPERTS, 1, 2 * D_FF)
    b_dn = b_down.reshape(DEPTH, N_EXPERTS, 1, D)
    w_in_p = jnp.pad(w_in, ((0, 0), (0, 0), (0, IN_PAD - w_in.shape[-1]))).astype(BF16)
    w_uq_p = jnp.pad(w_uq.reshape(DEPTH, Q_LORA, HEADS, NOPE + ROPE),
                     ((0, 0), (0, 0), (0, 0), (0, HEAD_PAD - NOPE - ROPE)))
    w_uq_p = w_uq_p.reshape(DEPTH, Q_LORA, HEADS * HEAD_PAD).astype(BF16)
    w_ukv_b = w_ukv.astype(BF16)
    ws = w_spatial.astype(BF16)
    bs_full = jnp.repeat(jnp.swapaxes(b_spatial, 1, 2), A_HEAD_DIM, axis=2)
    ckr_p = jnp.pad(cache_krope, ((0, 0), (0, 0), (0, 0), (0, LANES - ROPE))).astype(BF16)
    w_out_b = w_out.astype(BF16)
    w_r = jnp.pad(w_router, ((0, 0), (0, 0), (0, LANES - N_EXPERTS)))
    b_r = jnp.pad(b_router, ((0, 0), (0, LANES - N_EXPERTS)), constant_values=NEG)[:, None, :]
    row = lambda g: g[:, None, :]

    pre_params = (mods, row(g_pre_mix), w_in_p, row(g_sgu), gmat, row(g_q), w_uq_p, row(g_kv), w_ukv_b,
                  cos_t, sin_t)

    ckv_layers, krope_layers = [], []
    for i in range(DEPTH):
        u, vn, zf, q, kv, kr, ckv, zkr = _pre_call(i, xc, xl, pre_params)
        ckv_layers.append(ckv[:N_CTX].reshape(N_CTX_B, CTX_T, KV_LORA))
        krope_layers.append(zkr[:N_CTX, :ROPE].reshape(N_CTX_B, CTX_T, ROPE))

        mix_in = (u, vn, zf, q, kv, kr)
        y_ctx = _mix_call(i, mix_in, (ws, bs_full, bdc, bds) + dft[CTX_T], CTX_T, N_CTX_B, 0)
        y_lat = _mix_call(i, mix_in, (ws, bs_full, bdc, bds) + dft[LAT_T], LAT_T, N_LAT_B, N_CTX,
                          cache=(cache_ckv, ckr_p, w_ukv_b))

        x1, h2, tab, rw, cnt = _post_call(i, y_ctx, y_lat, xc, xl, mods, w_out_b,
                                          row(g_post_mix), row(g_pre_ffn), w_r, b_r, tri)

        dest, cend, pend, n_used, block_e, first, half, nxt, nxt2, slot = _plan_call(cnt, tab)
        dest_flat = dest.reshape(N_TOK // TB, 1, TOP_K * TB)
        xs = _dispatch_call(cend, pend, n_used, dest_flat, h2)
        ys = _moe_call(i, (block_e, first, half, nxt, nxt2, slot, n_used), xs, w_gate_up, b_gu, w_down, b_dn)
        xc, xl = _final_call(i, dest_flat, x1, ys, rw, mods, row(g_post_ffn))

    y_prompt = xc.reshape(N_CTX_B, CTX_T, D)
    y_sample = xl.reshape(N_LAT_B, LAT_T, D)
    return (y_prompt, y_sample, jnp.stack(ckv_layers, axis=1), jnp.stack(krope_layers, axis=1))
```

```python
import functools
import math

import jax
import jax.numpy as jnp
import numpy as np
from jax import lax
from jax.experimental import pallas as pl
from jax.experimental.pallas import tpu as pltpu

F32 = jnp.float32
BF16 = jnp.bfloat16

D = 1024
N_CTX_B, CTX_T = 16, 256
N_LAT_B, LAT_T = 2, 1024
PAST = 512
N_CTX = N_CTX_B * CTX_T
N_LAT = N_LAT_B * LAT_T
N_TOK = N_CTX + N_LAT
DEPTH = 2
GRID_W = 64
EPS = 1e-6
A_HEADS, A_HEAD_DIM, A_WIDTH, CHUNK = 4, 64, 256, 128
F_GROUPS, F_GROUP_DIM, F_WIDTH = 4, 64, 256
HEADS, Q_LORA, KV_LORA, NOPE, ROPE, V_DIM = 4, 256, 128, 128, 64, 128
HEAD_PAD = 256
IN_COLS = 1216
N_EXPERTS, TOP_K, D_FF = 32, 4, 1024
SWIGLU_LIMIT, SWIGLU_ALPHA = 7.0, 1.702
ROPE_BASE = 10000.0

TB = 512
QB_MAX = 1024
RB = 256
W_SLOTS = 3
LANES = 128
SUB = D // LANES
N_BLOCKS = N_TOK * TOP_K // RB + N_EXPERTS
N_ROWS = N_BLOCKS * RB
NEG = -3.0e38
VMEM_LIMIT = 56 * 1024 * 1024


def _rms(x, g):
    return x * lax.rsqrt(jnp.mean(x * x, axis=-1, keepdims=True) + EPS) * g


def _split_dot(v, m):
    hi = v.astype(BF16)
    lo = (v - hi.astype(F32)).astype(BF16)
    return (jnp.dot(hi, m, preferred_element_type=F32)
            + jnp.dot(lo, m, preferred_element_type=F32))


def _dot_nt(a, b):
    return lax.dot_general(a, b, (((1,), (1,)), ((), ())), preferred_element_type=F32)


def _store_tiled(ref, x):
    rows = x.shape[0]
    for s in range(SUB):
        ref[pl.ds(s, rows, stride=SUB), :] = x[:, s * LANES:(s + 1) * LANES]


def _load_tiled(ref, rows):
    return jnp.concatenate([ref[pl.ds(s, rows, stride=SUB), :] for s in range(SUB)], axis=1)


def _tile_rows(ref, row, n_rows=1):
    return ref.at[pl.ds(pl.multiple_of(row * SUB, SUB), n_rows * SUB)]


def _mod_kernel(ct_ref, w_ref, b_ref, o_ref):
    ct = ct_ref[...]
    s = ct * jax.nn.sigmoid(ct)
    w = w_ref[...]
    o_ref[...] = jnp.zeros(o_ref.shape, F32)
    for r in range(3):
        o_ref[r:r + 1, :] = jnp.sum(w * s[:, r:r + 1], axis=0, keepdims=True) + b_ref[...]


def _mod_call(cond_t, w_ada, b_ada):
    cb = 2048
    return pl.pallas_call(
        _mod_kernel,
        grid=(DEPTH, 6 * D // cb),
        in_specs=[
            pl.BlockSpec((D, 8), lambda l, j: (0, 0)),
            pl.BlockSpec((None, D, cb), lambda l, j: (l, 0, j)),
            pl.BlockSpec((None, 1, cb), lambda l, j: (l, 0, j)),
        ],
        out_specs=pl.BlockSpec((None, 8, cb), lambda l, j: (l, 0, j)),
        out_shape=jax.ShapeDtypeStruct((DEPTH, 8, 6 * D), F32),
        compiler_params=pltpu.CompilerParams(dimension_semantics=("parallel", "parallel")),
        name="modulation",
    )(cond_t, w_ada, b_ada)


def _mod_index(i):
    first_lat = N_CTX // TB
    return jnp.where(i < first_lat, 0, 1 + (i - first_lat) // (LAT_T // TB))


def _ctx_spec(width):
    return pl.BlockSpec((TB, width), lambda i: (jnp.minimum(i, N_CTX // TB - 1), 0))


def _lat_spec(width):
    return pl.BlockSpec((TB, width), lambda i: (jnp.maximum(i - N_CTX // TB, 0), 0))


def _pick(i, ctx_ref, lat_ref):
    return jnp.where(i >= N_CTX // TB, lat_ref[...], ctx_ref[...])


def _layer_spec(layer, shape):
    zeros = (0,) * len(shape)
    return pl.BlockSpec((None,) + tuple(shape), lambda *_: (layer,) + zeros)


def _mod_spec(layer):
    return pl.BlockSpec((None, None, 6, D), lambda i: (layer, _mod_index(i), 0, 0))


def _swap_halves(x, lane):
    w = x.shape[-1]
    fwd = pltpu.roll(x, w - 16, 1)
    bwd = pltpu.roll(x, 16, 1)
    return jnp.where((lane & 31) < 16, fwd, bwd)


def _pre_kernel(xc_ref, xl_ref, *refs):
    i = pl.program_id(0)
    _pre_body(i, _pick(i, xc_ref, xl_ref), *refs)


def _pre_body(i, x, mod_ref, gpre_ref, win_ref, gsgu_ref, gmat_ref, gq_ref, wuq_ref,
              gkv_ref, wukv_ref, cos_ref, sin_ref,
              u_ref, vn_ref, zf_ref, q_ref, kv_ref, kr_ref, ckv_ref, zkr_ref):
    is_lat = i >= N_CTX // TB
    h = _rms(x, gpre_ref[...] * (1.0 + mod_ref[1:2, :])) + mod_ref[0:1, :]
    z = jnp.dot(h.astype(BF16), win_ref[...].astype(BF16), preferred_element_type=F32)

    ga = jax.nn.gelu(z[:, :2 * A_WIDTH])
    u_ref[...] = ga[:, :A_WIDTH]
    v = ga[:, A_WIDTH:]
    gmat = gmat_ref[...]
    dv = v - _split_dot(v, gmat)
    var = _split_dot(dv * dv, gmat)
    vn_ref[...] = (dv * lax.rsqrt(var + EPS) * gsgu_ref[...]).astype(BF16)

    zf_ref[...] = z[:, 512:768].astype(BF16)

    cos = jnp.where(is_lat, cos_ref[...], 1.0)
    sin = jnp.where(is_lat, sin_ref[...], 0.0)
    lane = lax.broadcasted_iota(jnp.int32, (TB, LANES), 1)

    qn = _rms(z[:, 768:1024], gq_ref[...])
    scale = (NOPE + ROPE) ** -0.5
    q = jnp.dot(qn.astype(BF16), wuq_ref[...], preferred_element_type=F32) * scale
    for hd in range(HEADS):
        base = hd * HEAD_PAD
        q_ref[:, base:base + NOPE] = q[:, base:base + NOPE].astype(BF16)
        qr = q[:, base + NOPE:base + HEAD_PAD]
        q_ref[:, base + NOPE:base + HEAD_PAD] = (qr * cos + _swap_halves(qr, lane) * sin).astype(BF16)

    ckv = _rms(z[:, 1024:1152], gkv_ref[...])
    ckv_ref[...] = ckv
    kv_ref[...] = jnp.dot(ckv.astype(BF16), wukv_ref[...], preferred_element_type=F32).astype(BF16)

    zkr = jnp.concatenate([z[:, 1152:IN_COLS], jnp.zeros((TB, LANES - ROPE), F32)], axis=1)
    zkr_ref[...] = zkr
    kr_ref[...] = (zkr * cos + _swap_halves(zkr, lane) * sin).astype(BF16)


def _pre_specs(layer):
    lay = functools.partial(_layer_spec, layer)
    first_lat = N_CTX // TB
    pos_blocks = LAT_T // TB

    def tok(width):
        return pl.BlockSpec((TB, width), lambda i: (i, 0))

    def full(shape):
        return pl.BlockSpec(shape, lambda i: (0,) * len(shape))

    def rope_map(i):
        return (jnp.where(i >= first_lat, (i - first_lat) % pos_blocks, 0), 0)

    in_specs = [
        _mod_spec(layer),
        lay((1, D)), lay((D, IN_COLS)), lay((1, A_WIDTH)), full((A_WIDTH, A_WIDTH)),
        lay((1, Q_LORA)), lay((Q_LORA, HEADS * HEAD_PAD)),
        lay((1, KV_LORA)), lay((KV_LORA, HEADS * (NOPE + V_DIM))),
        pl.BlockSpec((TB, LANES), rope_map), pl.BlockSpec((TB, LANES), rope_map),
    ]
    out_specs = (tok(A_WIDTH), tok(A_WIDTH), tok(F_WIDTH), tok(HEADS * HEAD_PAD),
                 tok(HEADS * (NOPE + V_DIM)), tok(LANES), tok(KV_LORA), tok(LANES))
    out_shape = (
        jax.ShapeDtypeStruct((N_TOK, A_WIDTH), F32),
        jax.ShapeDtypeStruct((N_TOK, A_WIDTH), BF16),
        jax.ShapeDtypeStruct((N_TOK, F_WIDTH), BF16),
        jax.ShapeDtypeStruct((N_TOK, HEADS * HEAD_PAD), BF16),
        jax.ShapeDtypeStruct((N_TOK, HEADS * (NOPE + V_DIM)), BF16),
        jax.ShapeDtypeStruct((N_TOK, LANES), BF16),
        jax.ShapeDtypeStruct((N_TOK, KV_LORA), F32),
        jax.ShapeDtypeStruct((N_TOK, LANES), F32),
    )
    return in_specs, out_specs, out_shape


def _pre_call(layer, xc, xl, pre_params):
    in_specs, out_specs, out_shape = _pre_specs(layer)
    return pl.pallas_call(
        _pre_kernel,
        grid=(N_TOK // TB,),
        in_specs=[_ctx_spec(D), _lat_spec(D)] + in_specs,
        out_specs=out_specs,
        out_shape=out_shape,
        compiler_params=pltpu.CompilerParams(dimension_semantics=("parallel",),
                                             vmem_limit_bytes=VMEM_LIMIT),
        name="pre_mix",
    )(xc, xl, *pre_params)


def _mix_kernel(*refs, has_cache, qb, seq_t, n_seq):
    if has_cache:
        (u_ref, vn_ref, zf_ref, q_ref, kv_ref, kr_ref, ws_ref, bs_ref, bdc_ref, bds_ref,
         ct_ref, st_ref, cckv_ref, ckr_ref, wukv_ref, o_ref) = refs
    else:
        (u_ref, vn_ref, zf_ref, q_ref, kv_ref, kr_ref, ws_ref, bs_ref, bdc_ref, bds_ref,
         ct_ref, st_ref, o_ref) = refs

    lane = lax.broadcasted_iota(jnp.int32, (CHUNK, A_WIDTH), 1)
    for c in range(n_seq * qb // CHUNK):
        rows = slice(c * CHUNK, (c + 1) * CHUNK)
        vch = vn_ref[rows, :]
        s = bs_ref[...]
        for g in range(A_HEADS):
            sg = jnp.dot(ws_ref[g], vch, preferred_element_type=F32)
            in_head = (lane >= g * A_HEAD_DIM) & (lane < (g + 1) * A_HEAD_DIM)
            s = s + jnp.where(in_head, sg, 0.0)
        o_ref[rows, 0:A_WIDTH] = (u_ref[rows, :] * s).astype(BF16)

    if has_cache:
        kvc = jnp.dot(cckv_ref[...].astype(BF16), wukv_ref[...],
                      preferred_element_type=F32).astype(BF16)
        krc = ckr_ref[...]

    for sq in range(n_seq):
        qrows = slice(sq * qb, (sq + 1) * qb)
        srows = slice(sq * seq_t, (sq + 1) * seq_t)

        zf = zf_ref[srows, :]
        zc = jnp.dot(zf, bdc_ref[...], preferred_element_type=F32).astype(BF16)
        zs = jnp.dot(zf, bds_ref[...], preferred_element_type=F32).astype(BF16)
        yf = (jnp.dot(ct_ref[...], zc, preferred_element_type=F32)
              - jnp.dot(st_ref[...], zs, preferred_element_type=F32))
        o_ref[qrows, A_WIDTH:A_WIDTH + F_WIDTH] = yf.astype(BF16)

        kr = kr_ref[srows, :]
        for hd in range(HEADS):
            qh = q_ref[qrows, hd * HEAD_PAD:(hd + 1) * HEAD_PAD]
            kb = hd * (NOPE + V_DIM)
            kh = jnp.concatenate([kv_ref[srows, kb:kb + NOPE], kr], axis=1)
            vh = kv_ref[srows, kb + NOPE:kb + NOPE + V_DIM]
            s = _dot_nt(qh, kh)
            m = jnp.max(s, axis=-1, keepdims=True)
            if has_cache:
                khc = jnp.concatenate([kvc[:, kb:kb + NOPE], krc], axis=1)
                vhc = kvc[:, kb + NOPE:kb + NOPE + V_DIM]
                sc = _dot_nt(qh, khc)
                m = jnp.maximum(m, jnp.max(sc, axis=-1, keepdims=True))
            e = jnp.exp(s - m)
            den = jnp.sum(e, axis=-1, keepdims=True)
            o = jnp.dot(e.astype(BF16), vh, preferred_element_type=F32)
            if has_cache:
                ec = jnp.exp(sc - m)
                den = den + jnp.sum(ec, axis=-1, keepdims=True)
                o = o + jnp.dot(ec.astype(BF16), vhc, preferred_element_type=F32)
            ob = A_WIDTH + F_WIDTH + hd * V_DIM
            o_ref[qrows, ob:ob + V_DIM] = (o * (1.0 / den)).astype(BF16)


def _mix_call(layer, pre, consts, seq_t, n_batch, tok_off, cache=None):
    u, vn, zf, q, kv, kr = pre
    ws, bs_full, bdc, bds, ct, st = consts
    lay = functools.partial(_layer_spec, layer)
    qb = min(seq_t, QB_MAX)
    nq = seq_t // qb
    n_seq = QB_MAX // seq_t if nq == 1 else 1
    q_rows, s_rows = n_seq * qb, n_seq * seq_t
    qoff = tok_off // q_rows
    soff = tok_off // s_rows

    def qrow(width):
        return pl.BlockSpec((q_rows, width), lambda b, j: (qoff + b * nq + j, 0))

    def srow(width):
        return pl.BlockSpec((s_rows, width), lambda b, j: (soff + b, 0))

    def full(shape):
        return pl.BlockSpec(shape, lambda b, j: (0,) * len(shape))

    in_specs = [qrow(A_WIDTH), qrow(A_WIDTH), srow(F_WIDTH), qrow(HEADS * HEAD_PAD),
                srow(HEADS * (NOPE + V_DIM)), srow(LANES),
                lay((A_HEADS, CHUNK, CHUNK)), lay((CHUNK, A_WIDTH)),
                full((F_WIDTH, F_WIDTH)), full((F_WIDTH, F_WIDTH)),
                pl.BlockSpec((qb, seq_t), lambda b, j: (j, 0)),
                pl.BlockSpec((qb, seq_t), lambda b, j: (j, 0))]
    args = [u, vn, zf, q, kv, kr, ws, bs_full, bdc, bds, ct, st]
    if cache is not None:
        assert n_seq == 1
        cckv, ckr, wukv = cache
        in_specs += [pl.BlockSpec((None, None, PAST, KV_LORA), lambda b, j: (b, layer, 0, 0)),
                     pl.BlockSpec((None, None, PAST, LANES), lambda b, j: (b, layer, 0, 0)),
                     lay((KV_LORA, HEADS * (NOPE + V_DIM)))]
        args += [cckv, ckr, wukv]
    return pl.pallas_call(
        functools.partial(_mix_kernel, has_cache=cache is not None, qb=qb, seq_t=seq_t, n_seq=n_seq),
        grid=(n_batch // n_seq, nq),
        in_specs=in_specs,
        out_specs=pl.BlockSpec((q_rows, D), lambda b, j: (b * nq + j, 0)),
        out_shape=jax.ShapeDtypeStruct((n_batch * seq_t, D), BF16),
        compiler_params=pltpu.CompilerParams(dimension_semantics=("parallel", "parallel"),
                                             vmem_limit_bytes=VMEM_LIMIT),
        name="mix_lat" if cache is not None else "mix_ctx",
    )(*args)


def _post_kernel(yc_ref, yl_ref, xc_ref, xl_ref, mod_ref, wout_ref, gpost_ref, gffn_ref, wr_ref, br_ref,
                 tri_ref, x1_ref, h2_ref, tab_ref, rw_ref, cnt_ref, carry_ref):
    i = pl.program_id(0)

    @pl.when(i == 0)
    def _():
        carry_ref[...] = jnp.zeros(carry_ref.shape, F32)

    y = jnp.dot(_pick(i, yc_ref, yl_ref), wout_ref[...].astype(BF16), preferred_element_type=F32)
    x1 = _pick(i, xc_ref, xl_ref) + _rms(y, gpost_ref[...] * mod_ref[2:3, :])
    x1_ref[...] = x1
    h2 = _rms(x1, gffn_ref[...] * (1.0 + mod_ref[4:5, :])) + mod_ref[3:4, :]
    _store_tiled(h2_ref, h2)

    wr = wr_ref[...]
    wr_hi = wr.astype(BF16)
    wr_lo = (wr - wr_hi.astype(F32)).astype(BF16)
    h_hi = h2.astype(BF16)
    h_lo = (h2 - h_hi.astype(F32)).astype(BF16)
    hi_terms = jnp.dot(h_hi, jnp.concatenate([wr_hi, wr_lo], axis=1), preferred_element_type=F32)
    logits = (hi_terms[:, :LANES] + jnp.dot(h_lo, wr_hi, preferred_element_type=F32)
              + hi_terms[:, LANES:]) + br_ref[...]

    lane = lax.broadcasted_iota(jnp.int32, (TB, LANES), 1)
    lane_f = lane.astype(F32)
    work = logits
    idx, val = [], []
    for _ in range(TOP_K):
        m = jnp.max(work, axis=-1, keepdims=True)
        ik = jnp.min(jnp.where(work == m, lane_f, float(LANES)), axis=-1, keepdims=True)
        idx.append(ik)
        val.append(m)
        work = jnp.where(lane_f == ik, NEG, work)
    ex = [jnp.exp(v - val[0]) for v in val]
    den = ex[0] + ex[1] + ex[2] + ex[3]

    onehot = jnp.zeros((TB, LANES), F32)
    for k in range(TOP_K):
        onehot = onehot + jnp.where(lane_f == idx[k], 1.0, 0.0)
    before = jnp.dot(tri_ref[...], onehot.astype(BF16), preferred_element_type=F32) + carry_ref[0:1, :]
    ri = jnp.zeros((TB, LANES), F32)
    rw = jnp.zeros((TB, LANES), F32)
    for k in range(TOP_K):
        rank_k = jnp.sum(jnp.where(lane_f == idx[k], before, 0.0), axis=-1, keepdims=True)
        ri = ri + jnp.where(lane == k, idx[k], 0.0) + jnp.where(lane == TOP_K + k, rank_k, 0.0)
        rw = rw + jnp.where(lane == k, ex[k] / den, 0.0)
    tab_ref[...] = ri.T[0:2 * TOP_K, :].astype(jnp.int32)
    rw_ref[...] = rw
    total = carry_ref[0:1, :] + jnp.sum(onehot, axis=0, keepdims=True)
    carry_ref[...] = jnp.broadcast_to(total, carry_ref.shape)
    cnt_ref[...] = jnp.broadcast_to(total, cnt_ref.shape).astype(jnp.int32)


def _post_call(layer, yc, yl, xc, xl, mods, w_out, g_post, g_ffn, w_r, b_r, tri):
    nb = N_TOK // TB
    lay = functools.partial(_layer_spec, layer)

    def tok(width):
        return pl.BlockSpec((TB, width), lambda i: (i, 0))

    def full(shape):
        return pl.BlockSpec(shape, lambda i: (0,) * len(shape))

    return pl.pallas_call(
        _post_kernel,
        grid=(nb,),
        in_specs=[_ctx_spec(D), _lat_spec(D), _ctx_spec(D), _lat_spec(D), _mod_spec(layer),
                  lay((D, D)), lay((1, D)), lay((1, D)), lay((D, LANES)), lay((1, LANES)),
                  full((TB, TB))],
        out_specs=(tok(D), pl.BlockSpec((TB * SUB, LANES), lambda i: (i, 0)),
                   pl.BlockSpec((None, 2 * TOP_K, TB), lambda i: (i, 0, 0)),
                   tok(LANES), full((8, LANES))),
        out_shape=(jax.ShapeDtypeStruct((N_TOK, D), F32),
                   jax.ShapeDtypeStruct((N_TOK * SUB, LANES), F32),
                   jax.ShapeDtypeStruct((nb, 2 * TOP_K, TB), jnp.int32),
                   jax.ShapeDtypeStruct((N_TOK, LANES), F32),
                   jax.ShapeDtypeStruct((8, LANES), jnp.int32)),
        scratch_shapes=[pltpu.VMEM((8, LANES), F32)],
        compiler_params=pltpu.CompilerParams(dimension_semantics=("arbitrary",),
                                             vmem_limit_bytes=VMEM_LIMIT),
        name="post_mix_router",
    )(yc, yl, xc, xl, mods, w_out, g_post, g_ffn, w_r, b_r, tri)


def _plan_kernel(cnt_ref, tab_ref, dest_ref, cend_ref, pend_ref, nu_ref, be_ref, first_ref, half_ref, nxt_ref,
                 nxt2_ref, slot_ref, start_s, next_s):
    nxt = jnp.int32(-1)
    for e in reversed(range(N_EXPERTS)):
        next_s[e] = nxt
        nxt = jnp.where(cnt_ref[0, e] > 0, jnp.int32(e), nxt)

    start = jnp.int32(0)
    run = jnp.int32(0)
    last = jnp.int32(0)
    for e in range(N_EXPERTS):
        c = cnt_ref[0, e]
        n_blk = lax.shift_right_logical(c + (RB - 1), RB.bit_length() - 1)
        start_s[e] = start
        cend_ref[e] = start * RB + c
        pend_ref[e] = (start + n_blk) * RB
        few = (c - (n_blk - 1) * RB) <= RB // 2
        n1 = next_s[e]
        n2 = jnp.where(n1 >= 0, next_s[jnp.maximum(n1, 0)], jnp.int32(-1))
        slot = lax.rem(run, W_SLOTS)

        def fill(j, carry, e=e, n1=n1, n2=n2, slot=slot, start=start, n_blk=n_blk, few=few):
            be_ref[j] = jnp.int32(e)
            first_ref[j] = (j == start).astype(jnp.int32)
            half_ref[j] = jnp.logical_and(j == start + n_blk - 1, few).astype(jnp.int32)
            nxt_ref[j] = n1
            nxt2_ref[j] = n2
            slot_ref[j] = slot
            return carry

        lax.fori_loop(start, start + n_blk, fill, 0)
        owns = n_blk > 0
        last = jnp.where(owns, jnp.int32(e), last)
        run = run + owns.astype(jnp.int32)
        start = start + n_blk
    nu_ref[0] = start

    def tail(j, carry):
        be_ref[j] = last
        first_ref[j] = jnp.int32(0)
        half_ref[j] = jnp.int32(0)
        nxt_ref[j] = jnp.int32(-1)
        nxt2_ref[j] = jnp.int32(-1)
        slot_ref[j] = jnp.int32(0)
        return carry

    lax.fori_loop(start, N_BLOCKS, tail, 0)

    idx = tab_ref[:, 0:TOP_K, :]
    base = jnp.zeros(idx.shape, jnp.int32)
    for e in range(N_EXPERTS):
        base = jnp.where(idx == e, start_s[e] * RB, base)
    dest_ref[...] = base + tab_ref[:, TOP_K:2 * TOP_K, :]


def _plan_call(cnt, tab):
    nb = N_TOK // TB
    smem = pl.BlockSpec(memory_space=pltpu.SMEM)
    blocks = jax.ShapeDtypeStruct((N_BLOCKS,), jnp.int32)
    return pl.pallas_call(
        _plan_kernel,
        in_specs=[smem, pl.BlockSpec(memory_space=pltpu.VMEM)],
        out_specs=(pl.BlockSpec(memory_space=pltpu.VMEM),) + (smem,) * 9,
        out_shape=(jax.ShapeDtypeStruct((nb, TOP_K, TB), jnp.int32),
                   jax.ShapeDtypeStruct((N_EXPERTS,), jnp.int32),
                   jax.ShapeDtypeStruct((N_EXPERTS,), jnp.int32),
                   jax.ShapeDtypeStruct((1,), jnp.int32),
                   blocks, blocks, blocks, blocks, blocks, blocks),
        scratch_shapes=[pltpu.SMEM((N_EXPERTS,), jnp.int32), pltpu.SMEM((N_EXPERTS,), jnp.int32)],
        name="plan_rows",
    )(cnt, tab)


def _dispatch_kernel(cend_ref, pend_ref, nu_ref, dest_ref, h2_ref, xs_ref, zero_ref, sem, zsem):
    i = pl.program_id(0)

    def zero_padding(act):
        for e in range(N_EXPERTS):
            c_end = cend_ref[e]
            aligned = lax.shift_left(lax.shift_right_logical(c_end + 7, 3), 3)
            for r in range(7):

                @pl.when(c_end + r < aligned)
                def _():
                    act(pltpu.make_async_copy(_tile_rows(zero_ref, 0), _tile_rows(xs_ref, c_end + r), zsem))

            n = pend_ref[e] - aligned
            off = aligned
            for size in (128, 64, 32, 16, 8):

                @pl.when((n & size) != 0)
                def _():
                    act(pltpu.make_async_copy(_tile_rows(zero_ref, 0, size), _tile_rows(xs_ref, off, size),
                                              zsem))

                off = off + (n & size)

        def tail(b, carry):
            act(pltpu.make_async_copy(zero_ref, _tile_rows(xs_ref, b * RB, RB), zsem))
            return carry

        lax.fori_loop(nu_ref[0], N_BLOCKS, tail, 0)

    @pl.when(i == 0)
    def _():
        zero_ref[...] = jnp.zeros(zero_ref.shape, F32)
        zero_padding(lambda cp: cp.start())

    def body(t, carry):
        for k in range(TOP_K):
            pltpu.make_async_copy(_tile_rows(h2_ref, t), _tile_rows(xs_ref, dest_ref[0, k * TB + t]),
                                  sem).start(priority=k % 2)
        return carry

    lax.fori_loop(0, TB, body, 0, unroll=8)
    for k in range(TOP_K):
        pltpu.make_async_copy(h2_ref, _tile_rows(xs_ref, 0, TB), sem).wait()

    @pl.when(i == pl.num_programs(0) - 1)
    def _():
        zero_padding(lambda cp: cp.wait())


def _dispatch_call(cend, pend, n_used, dest, h2):
    nb = N_TOK // TB
    grid_spec = pltpu.PrefetchScalarGridSpec(
        num_scalar_prefetch=3,
        grid=(nb,),
        in_specs=[
            pl.BlockSpec((None, 1, TOP_K * TB), lambda i, c, p, n: (i, 0, 0), memory_space=pltpu.SMEM),
            pl.BlockSpec((TB * SUB, LANES), lambda i, c, p, n: (i, 0)),
        ],
        out_specs=pl.BlockSpec(memory_space=pl.ANY),
        scratch_shapes=[pltpu.VMEM((RB * SUB, LANES), F32), pltpu.SemaphoreType.DMA,
                        pltpu.SemaphoreType.DMA],
    )
    return pl.pallas_call(
        _dispatch_kernel,
        grid_spec=grid_spec,
        out_shape=jax.ShapeDtypeStruct((N_ROWS * SUB, LANES), F32),
        compiler_params=pltpu.CompilerParams(dimension_semantics=("arbitrary",),
                                             vmem_limit_bytes=VMEM_LIMIT),
        name="dispatch_rows",
    )(cend, pend, n_used, dest, h2)


def _moe_kernel(be_ref, first_ref, half_ref, nxt_ref, nxt2_ref, slot_ref, nu_ref, xs_ref, wgu_hbm, bgu_ref,
                wdn_hbm, bdn_ref, o_ref, wgu_f, wdn_f, sems, *, layer):
    b = pl.program_id(0)
    used = b < nu_ref[0]

    def fetch(e, s):
        return (pltpu.make_async_copy(wgu_hbm.at[layer, e], wgu_f.at[s], sems.at[0, s]),
                pltpu.make_async_copy(wdn_hbm.at[layer, e], wdn_f.at[s], sems.at[1, s]))

    @pl.when(b == 0)
    def _():
        for cp in fetch(be_ref[0], 0):
            cp.start(priority=1)

        @pl.when(nxt_ref[0] >= 0)
        def _():
            for cp in fetch(nxt_ref[0], 1):
                cp.start(priority=1)

    @pl.when(first_ref[b] == 1)
    def _():
        s = slot_ref[b]
        for cp in fetch(be_ref[b], s):
            cp.wait()

        @pl.when(nxt2_ref[b] >= 0)
        def _():
            for cp in fetch(nxt2_ref[b], (s + 2) % W_SLOTS):
                cp.start(priority=1)

    def experts(rows):
        s = slot_ref[b]
        x = _load_tiled(xs_ref, rows).astype(BF16)
        gu = jnp.dot(x, wgu_f[s].astype(BF16), preferred_element_type=F32) + bgu_ref[...]
        g = jnp.minimum(gu[:, :D_FF], SWIGLU_LIMIT)
        l = jnp.clip(gu[:, D_FF:], -SWIGLU_LIMIT, SWIGLU_LIMIT)
        a = g * jax.nn.sigmoid(SWIGLU_ALPHA * g) * (l + 1.0)
        y = jnp.dot(a.astype(BF16), wdn_f[s].astype(BF16), preferred_element_type=F32) + bdn_ref[...]
        _store_tiled(o_ref, y)
        if rows < RB:
            o_ref[rows * SUB:, :] = jnp.zeros(((RB - rows) * SUB, LANES), o_ref.dtype)

    half = half_ref[b] == 1

    @pl.when(jnp.logical_and(used, jnp.logical_not(half)))
    def _():
        experts(RB)

    @pl.when(jnp.logical_and(used, half))
    def _():
        experts(RB // 2)

    @pl.when(jnp.logical_not(used))
    def _():
        o_ref[...] = jnp.zeros(o_ref.shape, o_ref.dtype)


def _moe_call(layer, tables, xs, w_gu, b_gu, w_dn, b_dn):
    def rows_in(b, be, fi, ha, nx, n2, sl, nu):
        return (jnp.minimum(b, nu[0] - 1), 0)

    def rows_out(b, be, fi, ha, nx, n2, sl, nu):
        return (b, 0)

    def expert(b, be, fi, ha, nx, n2, sl, nu):
        return (layer, be[b], 0, 0)

    grid_spec = pltpu.PrefetchScalarGridSpec(
        num_scalar_prefetch=7,
        grid=(N_BLOCKS,),
        in_specs=[
            pl.BlockSpec((RB * SUB, LANES), rows_in),
            pl.BlockSpec(memory_space=pl.ANY),
            pl.BlockSpec((None, None, 1, 2 * D_FF), expert),
            pl.BlockSpec(memory_space=pl.ANY),
            pl.BlockSpec((None, None, 1, D), expert),
        ],
        out_specs=pl.BlockSpec((RB * SUB, LANES), rows_out),
        scratch_shapes=[pltpu.VMEM((W_SLOTS, D, 2 * D_FF), F32), pltpu.VMEM((W_SLOTS, D_FF, D), F32),
                        pltpu.SemaphoreType.DMA((2, W_SLOTS))],
    )
    return pl.pallas_call(
        functools.partial(_moe_kernel, layer=layer),
        grid_spec=grid_spec,
        out_shape=jax.ShapeDtypeStruct((N_ROWS * SUB, LANES), F32),
        compiler_params=pltpu.CompilerParams(dimension_semantics=("arbitrary",),
                                             vmem_limit_bytes=VMEM_LIMIT),
        name="moe_experts",
    )(*tables, xs, w_gu, b_gu, w_dn, b_dn)


def _final_kernel(*refs):
    _combine_body(pl.program_id(0), *refs)


def _combine_body(i, dcur_ref, dnxt_ref, x1_ref, ys_ref, rw_ref, mod_ref, g_ref, oc_ref, ol_ref, buf, sems):
    nb = pl.num_programs(0)
    slot = i % 2

    def gather(dest_ref, s):
        def body(t, carry):
            for k in range(TOP_K):
                pltpu.make_async_copy(_tile_rows(ys_ref, dest_ref[0, k * TB + t]),
                                      _tile_rows(buf.at[s], k * TB + t), sems.at[s]).start(priority=k % 2)
            return carry

        lax.fori_loop(0, TB, body, 0, unroll=8)

    @pl.when(i == 0)
    def _():
        gather(dcur_ref, 0)

    @pl.when(i + 1 < nb)
    def _():
        gather(dnxt_ref, 1 - slot)

    pltpu.make_async_copy(_tile_rows(ys_ref, 0, TOP_K * TB), buf.at[slot], sems.at[slot]).wait()

    rw = rw_ref[...]
    y = jnp.zeros((TB, D), F32)
    for k in range(TOP_K):
        yk = jnp.concatenate([buf[slot, pl.ds(k * TB * SUB + s, TB, stride=SUB), :] for s in range(SUB)],
                             axis=1)
        y = y + yk * rw[:, k:k + 1]
    x2 = x1_ref[...] + _rms(y, g_ref[...] * mod_ref[5:6, :])

    @pl.when(i < N_CTX // TB)
    def _():
        oc_ref[...] = x2

    @pl.when(i >= N_CTX // TB)
    def _():
        ol_ref[...] = x2


def _final_call(layer, dest_flat, x1, ys, rw, mods, g_post_ffn):
    nb = N_TOK // TB
    return pl.pallas_call(
        _final_kernel,
        grid=(nb,),
        in_specs=[pl.BlockSpec((None, 1, TOP_K * TB), lambda i: (i, 0, 0), memory_space=pltpu.SMEM),
                  pl.BlockSpec((None, 1, TOP_K * TB), lambda i: (jnp.minimum(i + 1, nb - 1), 0, 0),
                               memory_space=pltpu.SMEM),
                  pl.BlockSpec((TB, D), lambda i: (i, 0)),
                  pl.BlockSpec(memory_space=pl.ANY),
                  pl.BlockSpec((TB, LANES), lambda i: (i, 0)),
                  _mod_spec(layer), _layer_spec(layer, (1, D))],
        out_specs=(_ctx_spec(D), _lat_spec(D)),
        out_shape=(jax.ShapeDtypeStruct((N_CTX, D), F32), jax.ShapeDtypeStruct((N_LAT, D), F32)),
        scratch_shapes=[pltpu.VMEM((2, TOP_K * TB * SUB, LANES), F32), pltpu.SemaphoreType.DMA((2,))],
        compiler_params=pltpu.CompilerParams(dimension_semantics=("arbitrary",),
                                             vmem_limit_bytes=VMEM_LIMIT),
        name="combine_residual",
    )(dest_flat, dest_flat, x1, ys, rw, mods, g_post_ffn)


def _dft_tables(t):
    j = np.arange(t, dtype=np.int64)
    ang = 2.0 * np.pi * ((j[:, None] * j[None, :]) % t) / t
    return (np.cos(ang) / math.sqrt(t)).astype(np.float32), (np.sin(ang) / math.sqrt(t)).astype(np.float32)


def _channel_dft():
    c = np.arange(F_GROUP_DIM, dtype=np.int64)
    ang = 2.0 * np.pi * ((c[:, None] * c[None, :]) % F_GROUP_DIM) / F_GROUP_DIM
    eye = np.eye(F_GROUPS)
    bdc = np.kron(eye, np.cos(ang)) / math.sqrt(F_GROUP_DIM)
    bds = np.kron(eye, np.sin(ang)) / math.sqrt(F_GROUP_DIM)
    return bdc.astype(np.float32), bds.astype(np.float32)


def _rope_tables():
    pos = np.arange(LAT_T)
    n = ROPE // 4
    inv_freq = np.power(np.float32(ROPE_BASE), -np.arange(n, dtype=np.float32) / np.float32(n))
    ang_r = (pos // GRID_W).astype(np.float32)[:, None] * inv_freq
    ang_c = (pos % GRID_W).astype(np.float32)[:, None] * inv_freq
    cos = np.concatenate([np.cos(ang_r), np.cos(ang_r), np.cos(ang_c), np.cos(ang_c),
                          np.ones((LAT_T, LANES - ROPE))], axis=1)
    sin = np.concatenate([-np.sin(ang_r), np.sin(ang_r), -np.sin(ang_c), np.sin(ang_c),
                          np.zeros((LAT_T, LANES - ROPE))], axis=1)
    return cos.astype(np.float32), sin.astype(np.float32)


def kernel(x_prompt, x_sample, cache_ckv, cache_krope, c, c_ctx, w_ada, b_ada, g_pre_mix, g_post_mix, g_pre_ffn, g_post_ffn, w_in, g_sgu, w_spatial, b_spatial, g_q, w_uq, g_kv, w_ukv, w_out, w_router, b_router, w_gate_up, b_gate_up, w_down, b_down):
    xc, xl = x_prompt.reshape(N_CTX, D), x_sample.reshape(N_LAT, D)

    cond = jnp.concatenate([c_ctx[None, :], c, jnp.zeros((5, D), F32)], axis=0)
    mods = _mod_call(cond.T, w_ada, b_ada.reshape(DEPTH, 1, 6 * D))
    mods = mods[:, :3].reshape(DEPTH, 3, 6, D)

    gmat = jnp.asarray(np.kron(np.eye(A_HEADS), np.full((A_HEAD_DIM, A_HEAD_DIM), 1.0 / A_HEAD_DIM)),
                       dtype=BF16)
    bdc_np, bds_np = _channel_dft()
    bdc, bds = jnp.asarray(bdc_np).astype(BF16), jnp.asarray(bds_np).astype(BF16)
    dft = {}
    for t in (CTX_T, LAT_T):
        ct_np, st_np = _dft_tables(t)
        dft[t] = (jnp.asarray(ct_np).astype(BF16), jnp.asarray(st_np).astype(BF16))
    cos_np, sin_np = _rope_tables()
    cos_t, sin_t = jnp.asarray(cos_np), jnp.asarray(sin_np)
    tri = jnp.asarray(np.tril(np.ones((TB, TB), np.float32), k=-1)).astype(BF16)

    b_gu = b_gate_up.reshape(DEPTH, N_EXPERTS, 1, 2 * D_FF)
    b_dn = b_down.reshape(DEPTH, N_EXPERTS, 1, D)
    w_uq_p = jnp.pad(w_uq.reshape(DEPTH, Q_LORA, HEADS, NOPE + ROPE),
                     ((0, 0), (0, 0), (0, 0), (0, HEAD_PAD - NOPE - ROPE)))
    w_uq_p = w_uq_p.reshape(DEPTH, Q_LORA, HEADS * HEAD_PAD).astype(BF16)
    w_ukv_b = w_ukv.astype(BF16)
    ws = w_spatial.astype(BF16)
    bs_full = jnp.repeat(jnp.swapaxes(b_spatial, 1, 2), A_HEAD_DIM, axis=2)
    ckr_p = jnp.pad(cache_krope, ((0, 0), (0, 0), (0, 0), (0, LANES - ROPE))).astype(BF16)
    w_r = jnp.pad(w_router, ((0, 0), (0, 0), (0, LANES - N_EXPERTS)))
    b_r = jnp.pad(b_router, ((0, 0), (0, LANES - N_EXPERTS)), constant_values=NEG)[:, None, :]
    row = lambda g: g[:, None, :]

    pre_params = (mods, row(g_pre_mix), w_in, row(g_sgu), gmat, row(g_q), w_uq_p, row(g_kv), w_ukv_b,
                  cos_t, sin_t)

    ckv_layers, krope_layers = [], []
    for i in range(DEPTH):
        u, vn, zf, q, kv, kr, ckv, zkr = _pre_call(i, xc, xl, pre_params)
        ckv_layers.append(ckv[:N_CTX].reshape(N_CTX_B, CTX_T, KV_LORA))
        krope_layers.append(zkr[:N_CTX, :ROPE].reshape(N_CTX_B, CTX_T, ROPE))

        mix_in = (u, vn, zf, q, kv, kr)
        y_ctx = _mix_call(i, mix_in, (ws, bs_full, bdc, bds) + dft[CTX_T], CTX_T, N_CTX_B, 0)
        y_lat = _mix_call(i, mix_in, (ws, bs_full, bdc, bds) + dft[LAT_T], LAT_T, N_LAT_B, N_CTX,
                          cache=(cache_ckv, ckr_p, w_ukv_b))

        x1, h2, tab, rw, cnt = _post_call(i, y_ctx, y_lat, xc, xl, mods, w_out,
                                          row(g_post_mix), row(g_pre_ffn), w_r, b_r, tri)

        dest, cend, pend, n_used, block_e, first, half, nxt, nxt2, slot = _plan_call(cnt, tab)
        dest_flat = dest.reshape(N_TOK // TB, 1, TOP_K * TB)
        xs = _dispatch_call(cend, pend, n_used, dest_flat, h2)
        ys = _moe_call(i, (block_e, first, half, nxt, nxt2, slot, n_used), xs, w_gate_up, b_gu, w_down, b_dn)
        xc, xl = _final_call(i, dest_flat, x1, ys, rw, mods, row(g_post_ffn))

    y_prompt = xc.reshape(N_CTX_B, CTX_T, D)
    y_sample = xl.reshape(N_LAT_B, LAT_T, D)
    return (y_prompt, y_sample, jnp.stack(ckv_layers, axis=1), jnp.stack(krope_layers, axis=1))
```

```python
import functools
import math

import jax
import jax.numpy as jnp
import numpy as np
from jax import lax
from jax.experimental import pallas as pl
from jax.experimental.pallas import tpu as pltpu

F32 = jnp.float32
BF16 = jnp.bfloat16

D = 1024
N_CTX_B, CTX_T = 16, 256
N_LAT_B, LAT_T = 2, 1024
PAST = 512
N_CTX = N_CTX_B * CTX_T
N_LAT = N_LAT_B * LAT_T
N_TOK = N_CTX + N_LAT
DEPTH = 2
GRID_W = 64
EPS = 1e-6
A_HEADS, A_HEAD_DIM, A_WIDTH, CHUNK = 4, 64, 256, 128
F_GROUPS, F_GROUP_DIM, F_WIDTH = 4, 64, 256
HEADS, Q_LORA, KV_LORA, NOPE, ROPE, V_DIM = 4, 256, 128, 128, 64, 128
HEAD_PAD = 256
IN_COLS = 1216
N_EXPERTS, TOP_K, D_FF = 32, 4, 1024
SWIGLU_LIMIT, SWIGLU_ALPHA = 7.0, 1.702
ROPE_BASE = 10000.0

TB = 512
QB_MAX = 1024
RB = 256
W_SLOTS = 3
LANES = 128
SUB = D // LANES
N_BLOCKS = N_TOK * TOP_K // RB + N_EXPERTS
N_ROWS = N_BLOCKS * RB
NEG = -3.0e38
VMEM_LIMIT = 56 * 1024 * 1024


def _rms(x, g):
    return x * lax.rsqrt(jnp.mean(x * x, axis=-1, keepdims=True) + EPS) * g


def _split_dot(v, m):
    hi = v.astype(BF16)
    lo = (v - hi.astype(F32)).astype(BF16)
    return (jnp.dot(hi, m, preferred_element_type=F32)
            + jnp.dot(lo, m, preferred_element_type=F32))


def _dot_nt(a, b):
    return lax.dot_general(a, b, (((1,), (1,)), ((), ())), preferred_element_type=F32)


def _store_tiled(ref, x):
    rows = x.shape[0]
    for s in range(SUB):
        ref[pl.ds(s, rows, stride=SUB), :] = x[:, s * LANES:(s + 1) * LANES]


def _load_tiled(ref, rows):
    return jnp.concatenate([ref[pl.ds(s, rows, stride=SUB), :] for s in range(SUB)], axis=1)


def _tile_rows(ref, row, n_rows=1):
    return ref.at[pl.ds(pl.multiple_of(row * SUB, SUB), n_rows * SUB)]


def _mod_kernel(ct_ref, w_ref, b_ref, o_ref):
    ct = ct_ref[...]
    s = ct * jax.nn.sigmoid(ct)
    w = w_ref[...]
    o_ref[...] = jnp.zeros(o_ref.shape, F32)
    for r in range(3):
        o_ref[r:r + 1, :] = jnp.sum(w * s[:, r:r + 1], axis=0, keepdims=True) + b_ref[...]


def _mod_call(cond_t, w_ada, b_ada):
    cb = 3072
    return pl.pallas_call(
        _mod_kernel,
        grid=(DEPTH, 6 * D // cb),
        in_specs=[
            pl.BlockSpec((D, 8), lambda l, j: (0, 0)),
            pl.BlockSpec((None, D, cb), lambda l, j: (l, 0, j)),
            pl.BlockSpec((None, 1, cb), lambda l, j: (l, 0, j)),
        ],
        out_specs=pl.BlockSpec((None, 8, cb), lambda l, j: (l, 0, j)),
        out_shape=jax.ShapeDtypeStruct((DEPTH, 8, 6 * D), F32),
        compiler_params=pltpu.CompilerParams(dimension_semantics=("parallel", "parallel")),
        name="modulation",
    )(cond_t, w_ada, b_ada)


def _mod_index(i):
    first_lat = N_CTX // TB
    return jnp.where(i < first_lat, 0, 1 + (i - first_lat) // (LAT_T // TB))


def _ctx_spec(width):
    return pl.BlockSpec((TB, width), lambda i: (jnp.minimum(i, N_CTX // TB - 1), 0))


def _lat_spec(width):
    return pl.BlockSpec((TB, width), lambda i: (jnp.maximum(i - N_CTX // TB, 0), 0))


def _pick(i, ctx_ref, lat_ref):
    return jnp.where(i >= N_CTX // TB, lat_ref[...], ctx_ref[...])


def _layer_spec(layer, shape):
    zeros = (0,) * len(shape)
    return pl.BlockSpec((None,) + tuple(shape), lambda *_: (layer,) + zeros)


def _mod_spec(layer):
    return pl.BlockSpec((None, None, 6, D), lambda i: (layer, _mod_index(i), 0, 0))


def _swap_halves(x, lane):
    w = x.shape[-1]
    fwd = pltpu.roll(x, w - 16, 1)
    bwd = pltpu.roll(x, 16, 1)
    return jnp.where((lane & 31) < 16, fwd, bwd)


def _pre_kernel(xc_ref, xl_ref, *refs):
    i = pl.program_id(0)
    _pre_body(i, _pick(i, xc_ref, xl_ref), *refs)


def _pre_body(i, x, mod_ref, gpre_ref, win_ref, gsgu_ref, gmat_ref, gq_ref, wuq_ref,
              gkv_ref, wukv_ref, cos_ref, sin_ref,
              u_ref, vn_ref, zf_ref, q_ref, kv_ref, kr_ref, ckv_ref, zkr_ref):
    is_lat = i >= N_CTX // TB
    h = _rms(x, gpre_ref[...] * (1.0 + mod_ref[1:2, :])) + mod_ref[0:1, :]
    z = jnp.dot(h.astype(BF16), win_ref[...].astype(BF16), preferred_element_type=F32)

    ga = jax.nn.gelu(z[:, :2 * A_WIDTH])
    u_ref[...] = ga[:, :A_WIDTH]
    v = ga[:, A_WIDTH:]
    gmat = gmat_ref[...]
    dv = v - _split_dot(v, gmat)
    var = _split_dot(dv * dv, gmat)
    vn_ref[...] = (dv * lax.rsqrt(var + EPS) * gsgu_ref[...]).astype(BF16)

    zf_ref[...] = z[:, 512:768].astype(BF16)

    cos = jnp.where(is_lat, cos_ref[...], 1.0)
    sin = jnp.where(is_lat, sin_ref[...], 0.0)
    lane = lax.broadcasted_iota(jnp.int32, (TB, LANES), 1)

    qn = _rms(z[:, 768:1024], gq_ref[...])
    scale = (NOPE + ROPE) ** -0.5
    q = jnp.dot(qn.astype(BF16), wuq_ref[...], preferred_element_type=F32) * scale
    for hd in range(HEADS):
        base = hd * HEAD_PAD
        q_ref[:, base:base + NOPE] = q[:, base:base + NOPE].astype(BF16)
        qr = q[:, base + NOPE:base + HEAD_PAD]
        q_ref[:, base + NOPE:base + HEAD_PAD] = (qr * cos + _swap_halves(qr, lane) * sin).astype(BF16)

    ckv = _rms(z[:, 1024:1152], gkv_ref[...])
    ckv_ref[...] = ckv
    kv_ref[...] = jnp.dot(ckv.astype(BF16), wukv_ref[...], preferred_element_type=F32).astype(BF16)

    zkr = jnp.concatenate([z[:, 1152:IN_COLS], jnp.zeros((TB, LANES - ROPE), F32)], axis=1)
    zkr_ref[...] = zkr
    kr_ref[...] = (zkr * cos + _swap_halves(zkr, lane) * sin).astype(BF16)


def _pre_specs(layer):
    lay = functools.partial(_layer_spec, layer)
    first_lat = N_CTX // TB
    pos_blocks = LAT_T // TB

    def tok(width):
        return pl.BlockSpec((TB, width), lambda i: (i, 0))

    def full(shape):
        return pl.BlockSpec(shape, lambda i: (0,) * len(shape))

    def rope_map(i):
        return (jnp.where(i >= first_lat, (i - first_lat) % pos_blocks, 0), 0)

    in_specs = [
        _mod_spec(layer),
        lay((1, D)), lay((D, IN_COLS)), lay((1, A_WIDTH)), full((A_WIDTH, A_WIDTH)),
        lay((1, Q_LORA)), lay((Q_LORA, HEADS * HEAD_PAD)),
        lay((1, KV_LORA)), lay((KV_LORA, HEADS * (NOPE + V_DIM))),
        pl.BlockSpec((TB, LANES), rope_map), pl.BlockSpec((TB, LANES), rope_map),
    ]
    out_specs = (tok(A_WIDTH), tok(A_WIDTH), tok(F_WIDTH), tok(HEADS * HEAD_PAD),
                 tok(HEADS * (NOPE + V_DIM)), tok(LANES), tok(KV_LORA), tok(LANES))
    out_shape = (
        jax.ShapeDtypeStruct((N_TOK, A_WIDTH), F32),
        jax.ShapeDtypeStruct((N_TOK, A_WIDTH), BF16),
        jax.ShapeDtypeStruct((N_TOK, F_WIDTH), BF16),
        jax.ShapeDtypeStruct((N_TOK, HEADS * HEAD_PAD), BF16),
        jax.ShapeDtypeStruct((N_TOK, HEADS * (NOPE + V_DIM)), BF16),
        jax.ShapeDtypeStruct((N_TOK, LANES), BF16),
        jax.ShapeDtypeStruct((N_TOK, KV_LORA), F32),
        jax.ShapeDtypeStruct((N_TOK, LANES), F32),
    )
    return in_specs, out_specs, out_shape


def _pre_call(layer, xc, xl, pre_params):
    in_specs, out_specs, out_shape = _pre_specs(layer)
    return pl.pallas_call(
        _pre_kernel,
        grid=(N_TOK // TB,),
        in_specs=[_ctx_spec(D), _lat_spec(D)] + in_specs,
        out_specs=out_specs,
        out_shape=out_shape,
        compiler_params=pltpu.CompilerParams(dimension_semantics=("parallel",),
                                             vmem_limit_bytes=VMEM_LIMIT),
        name="pre_mix",
    )(xc, xl, *pre_params)


def _mix_kernel(*refs, has_cache, qb, seq_t, n_seq):
    if has_cache:
        (u_ref, vn_ref, zf_ref, q_ref, kv_ref, kr_ref, ws_ref, bs_ref, bdc_ref, bds_ref,
         ct_ref, st_ref, cckv_ref, ckr_ref, wukv_ref, o_ref) = refs
    else:
        (u_ref, vn_ref, zf_ref, q_ref, kv_ref, kr_ref, ws_ref, bs_ref, bdc_ref, bds_ref,
         ct_ref, st_ref, o_ref) = refs

    lane = lax.broadcasted_iota(jnp.int32, (CHUNK, A_WIDTH), 1)
    for c in range(n_seq * qb // CHUNK):
        rows = slice(c * CHUNK, (c + 1) * CHUNK)
        vch = vn_ref[rows, :]
        s = bs_ref[...]
        for g in range(A_HEADS):
            sg = jnp.dot(ws_ref[g], vch, preferred_element_type=F32)
            in_head = (lane >= g * A_HEAD_DIM) & (lane < (g + 1) * A_HEAD_DIM)
            s = s + jnp.where(in_head, sg, 0.0)
        o_ref[rows, 0:A_WIDTH] = (u_ref[rows, :] * s).astype(BF16)

    if has_cache:
        kvc = jnp.dot(cckv_ref[...].astype(BF16), wukv_ref[...],
                      preferred_element_type=F32).astype(BF16)
        krc = ckr_ref[...]

    for sq in range(n_seq):
        qrows = slice(sq * qb, (sq + 1) * qb)
        srows = slice(sq * seq_t, (sq + 1) * seq_t)

        zf = zf_ref[srows, :]
        zc = jnp.dot(zf, bdc_ref[...], preferred_element_type=F32).astype(BF16)
        zs = jnp.dot(zf, bds_ref[...], preferred_element_type=F32).astype(BF16)
        yf = (jnp.dot(ct_ref[...], zc, preferred_element_type=F32)
              - jnp.dot(st_ref[...], zs, preferred_element_type=F32))
        o_ref[qrows, A_WIDTH:A_WIDTH + F_WIDTH] = yf.astype(BF16)

        kr = kr_ref[srows, :]
        for hd in range(HEADS):
            qh = q_ref[qrows, hd * HEAD_PAD:(hd + 1) * HEAD_PAD]
            kb = hd * (NOPE + V_DIM)
            kh = jnp.concatenate([kv_ref[srows, kb:kb + NOPE], kr], axis=1)
            vh = kv_ref[srows, kb + NOPE:kb + NOPE + V_DIM]
            s = _dot_nt(qh, kh)
            m = jnp.max(s, axis=-1, keepdims=True)
            if has_cache:
                khc = jnp.concatenate([kvc[:, kb:kb + NOPE], krc], axis=1)
                vhc = kvc[:, kb + NOPE:kb + NOPE + V_DIM]
                sc = _dot_nt(qh, khc)
                m = jnp.maximum(m, jnp.max(sc, axis=-1, keepdims=True))
            e = jnp.exp(s - m)
            den = jnp.sum(e, axis=-1, keepdims=True)
            o = jnp.dot(e.astype(BF16), vh, preferred_element_type=F32)
            if has_cache:
                ec = jnp.exp(sc - m)
                den = den + jnp.sum(ec, axis=-1, keepdims=True)
                o = o + jnp.dot(ec.astype(BF16), vhc, preferred_element_type=F32)
            ob = A_WIDTH + F_WIDTH + hd * V_DIM
            o_ref[qrows, ob:ob + V_DIM] = (o * (1.0 / den)).astype(BF16)


def _mix_call(layer, pre, consts, seq_t, n_batch, tok_off, cache=None):
    u, vn, zf, q, kv, kr = pre
    ws, bs_full, bdc, bds, ct, st = consts
    lay = functools.partial(_layer_spec, layer)
    qb = min(seq_t, QB_MAX)
    nq = seq_t // qb
    n_seq = QB_MAX // seq_t if nq == 1 else 1
    q_rows, s_rows = n_seq * qb, n_seq * seq_t
    qoff = tok_off // q_rows
    soff = tok_off // s_rows

    def qrow(width):
        return pl.BlockSpec((q_rows, width), lambda b, j: (qoff + b * nq + j, 0))

    def srow(width):
        return pl.BlockSpec((s_rows, width), lambda b, j: (soff + b, 0))

    def full(shape):
        return pl.BlockSpec(shape, lambda b, j: (0,) * len(shape))

    in_specs = [qrow(A_WIDTH), qrow(A_WIDTH), srow(F_WIDTH), qrow(HEADS * HEAD_PAD),
                srow(HEADS * (NOPE + V_DIM)), srow(LANES),
                lay((A_HEADS, CHUNK, CHUNK)), lay((CHUNK, A_WIDTH)),
                full((F_WIDTH, F_WIDTH)), full((F_WIDTH, F_WIDTH)),
                pl.BlockSpec((qb, seq_t), lambda b, j: (j, 0)),
                pl.BlockSpec((qb, seq_t), lambda b, j: (j, 0))]
    args = [u, vn, zf, q, kv, kr, ws, bs_full, bdc, bds, ct, st]
    if cache is not None:
        assert n_seq == 1
        cckv, ckr, wukv = cache
        in_specs += [pl.BlockSpec((None, None, PAST, KV_LORA), lambda b, j: (b, layer, 0, 0)),
                     pl.BlockSpec((None, None, PAST, LANES), lambda b, j: (b, layer, 0, 0)),
                     lay((KV_LORA, HEADS * (NOPE + V_DIM)))]
        args += [cckv, ckr, wukv]
    return pl.pallas_call(
        functools.partial(_mix_kernel, has_cache=cache is not None, qb=qb, seq_t=seq_t, n_seq=n_seq),
        grid=(n_batch // n_seq, nq),
        in_specs=in_specs,
        out_specs=pl.BlockSpec((q_rows, D), lambda b, j: (b * nq + j, 0)),
        out_shape=jax.ShapeDtypeStruct((n_batch * seq_t, D), BF16),
        compiler_params=pltpu.CompilerParams(dimension_semantics=("parallel", "parallel"),
                                             vmem_limit_bytes=VMEM_LIMIT),
        name="mix_lat" if cache is not None else "mix_ctx",
    )(*args)


def _post_kernel(yc_ref, yl_ref, xc_ref, xl_ref, mod_ref, wout_ref, gpost_ref, gffn_ref, wr_ref, br_ref,
                 tri_ref, x1_ref, h2_ref, tab_ref, rw_ref, cnt_ref, carry_ref):
    i = pl.program_id(0)

    @pl.when(i == 0)
    def _():
        carry_ref[...] = jnp.zeros(carry_ref.shape, F32)

    y = jnp.dot(_pick(i, yc_ref, yl_ref), wout_ref[...].astype(BF16), preferred_element_type=F32)
    x1 = _pick(i, xc_ref, xl_ref) + _rms(y, gpost_ref[...] * mod_ref[2:3, :])
    x1_ref[...] = x1
    h2 = _rms(x1, gffn_ref[...] * (1.0 + mod_ref[4:5, :])) + mod_ref[3:4, :]
    _store_tiled(h2_ref, h2)

    wr = wr_ref[...]
    wr_hi = wr.astype(BF16)
    wr_lo = (wr - wr_hi.astype(F32)).astype(BF16)
    h_hi = h2.astype(BF16)
    h_lo = (h2 - h_hi.astype(F32)).astype(BF16)
    hi_terms = jnp.dot(h_hi, jnp.concatenate([wr_hi, wr_lo], axis=1), preferred_element_type=F32)
    logits = (hi_terms[:, :LANES] + jnp.dot(h_lo, wr_hi, preferred_element_type=F32)
              + hi_terms[:, LANES:]) + br_ref[...]

    lane = lax.broadcasted_iota(jnp.int32, (TB, LANES), 1)
    lane_f = lane.astype(F32)
    work = logits
    idx, val = [], []
    for _ in range(TOP_K):
        m = jnp.max(work, axis=-1, keepdims=True)
        ik = jnp.min(jnp.where(work == m, lane_f, float(LANES)), axis=-1, keepdims=True)
        idx.append(ik)
        val.append(m)
        work = jnp.where(lane_f == ik, NEG, work)
    ex = [jnp.exp(v - val[0]) for v in val]
    den = ex[0] + ex[1] + ex[2] + ex[3]

    onehot = jnp.zeros((TB, LANES), F32)
    for k in range(TOP_K):
        onehot = onehot + jnp.where(lane_f == idx[k], 1.0, 0.0)
    before = jnp.dot(tri_ref[...], onehot.astype(BF16), preferred_element_type=F32) + carry_ref[0:1, :]
    ri = jnp.zeros((TB, LANES), F32)
    rw = jnp.zeros((TB, LANES), F32)
    for k in range(TOP_K):
        rank_k = jnp.sum(jnp.where(lane_f == idx[k], before, 0.0), axis=-1, keepdims=True)
        ri = ri + jnp.where(lane == k, idx[k], 0.0) + jnp.where(lane == TOP_K + k, rank_k, 0.0)
        rw = rw + jnp.where(lane == k, ex[k] / den, 0.0)
    tab_ref[...] = ri.T[0:2 * TOP_K, :].astype(jnp.int32)
    rw_ref[...] = rw
    total = carry_ref[0:1, :] + jnp.sum(onehot, axis=0, keepdims=True)
    carry_ref[...] = jnp.broadcast_to(total, carry_ref.shape)
    cnt_ref[...] = jnp.broadcast_to(total, cnt_ref.shape).astype(jnp.int32)


def _post_call(layer, yc, yl, xc, xl, mods, w_out, g_post, g_ffn, w_r, b_r, tri):
    nb = N_TOK // TB
    lay = functools.partial(_layer_spec, layer)

    def tok(width):
        return pl.BlockSpec((TB, width), lambda i: (i, 0))

    def full(shape):
        return pl.BlockSpec(shape, lambda i: (0,) * len(shape))

    return pl.pallas_call(
        _post_kernel,
        grid=(nb,),
        in_specs=[_ctx_spec(D), _lat_spec(D), _ctx_spec(D), _lat_spec(D), _mod_spec(layer),
                  lay((D, D)), lay((1, D)), lay((1, D)), lay((D, LANES)), lay((1, LANES)),
                  full((TB, TB))],
        out_specs=(tok(D), pl.BlockSpec((TB * SUB, LANES), lambda i: (i, 0)),
                   pl.BlockSpec((None, 2 * TOP_K, TB), lambda i: (i, 0, 0)),
                   tok(LANES), full((8, LANES))),
        out_shape=(jax.ShapeDtypeStruct((N_TOK, D), F32),
                   jax.ShapeDtypeStruct((N_TOK * SUB, LANES), F32),
                   jax.ShapeDtypeStruct((nb, 2 * TOP_K, TB), jnp.int32),
                   jax.ShapeDtypeStruct((N_TOK, LANES), F32),
                   jax.ShapeDtypeStruct((8, LANES), jnp.int32)),
        scratch_shapes=[pltpu.VMEM((8, LANES), F32)],
        compiler_params=pltpu.CompilerParams(dimension_semantics=("arbitrary",),
                                             vmem_limit_bytes=VMEM_LIMIT),
        name="post_mix_router",
    )(yc, yl, xc, xl, mods, w_out, g_post, g_ffn, w_r, b_r, tri)


def _plan_kernel(cnt_ref, tab_ref, dest_ref, cend_ref, pend_ref, nu_ref, be_ref, first_ref, half_ref, nxt_ref,
                 nxt2_ref, slot_ref, start_s, next_s):
    nxt = jnp.int32(-1)
    for e in reversed(range(N_EXPERTS)):
        next_s[e] = nxt
        nxt = jnp.where(cnt_ref[0, e] > 0, jnp.int32(e), nxt)

    start = jnp.int32(0)
    run = jnp.int32(0)
    last = jnp.int32(0)
    for e in range(N_EXPERTS):
        c = cnt_ref[0, e]
        n_blk = lax.shift_right_logical(c + (RB - 1), RB.bit_length() - 1)
        start_s[e] = start
        cend_ref[e] = start * RB + c
        pend_ref[e] = (start + n_blk) * RB
        few = (c - (n_blk - 1) * RB) <= RB // 2
        n1 = next_s[e]
        n2 = jnp.where(n1 >= 0, next_s[jnp.maximum(n1, 0)], jnp.int32(-1))
        slot = lax.rem(run, W_SLOTS)

        def fill(j, carry, e=e, n1=n1, n2=n2, slot=slot, start=start, n_blk=n_blk, few=few):
            be_ref[j] = jnp.int32(e)
            first_ref[j] = (j == start).astype(jnp.int32)
            half_ref[j] = jnp.logical_and(j == start + n_blk - 1, few).astype(jnp.int32)
            nxt_ref[j] = n1
            nxt2_ref[j] = n2
            slot_ref[j] = slot
            return carry

        lax.fori_loop(start, start + n_blk, fill, 0)
        owns = n_blk > 0
        last = jnp.where(owns, jnp.int32(e), last)
        run = run + owns.astype(jnp.int32)
        start = start + n_blk
    nu_ref[0] = start

    def tail(j, carry):
        be_ref[j] = last
        first_ref[j] = jnp.int32(0)
        half_ref[j] = jnp.int32(0)
        nxt_ref[j] = jnp.int32(-1)
        nxt2_ref[j] = jnp.int32(-1)
        slot_ref[j] = jnp.int32(0)
        return carry

    lax.fori_loop(start, N_BLOCKS, tail, 0)

    idx = tab_ref[:, 0:TOP_K, :]
    base = jnp.zeros(idx.shape, jnp.int32)
    for e in range(N_EXPERTS):
        base = jnp.where(idx == e, start_s[e] * RB, base)
    dest_ref[...] = base + tab_ref[:, TOP_K:2 * TOP_K, :]


def _plan_call(cnt, tab):
    nb = N_TOK // TB
    smem = pl.BlockSpec(memory_space=pltpu.SMEM)
    blocks = jax.ShapeDtypeStruct((N_BLOCKS,), jnp.int32)
    return pl.pallas_call(
        _plan_kernel,
        in_specs=[smem, pl.BlockSpec(memory_space=pltpu.VMEM)],
        out_specs=(pl.BlockSpec(memory_space=pltpu.VMEM),) + (smem,) * 9,
        out_shape=(jax.ShapeDtypeStruct((nb, TOP_K, TB), jnp.int32),
                   jax.ShapeDtypeStruct((N_EXPERTS,), jnp.int32),
                   jax.ShapeDtypeStruct((N_EXPERTS,), jnp.int32),
                   jax.ShapeDtypeStruct((1,), jnp.int32),
                   blocks, blocks, blocks, blocks, blocks, blocks),
        scratch_shapes=[pltpu.SMEM((N_EXPERTS,), jnp.int32), pltpu.SMEM((N_EXPERTS,), jnp.int32)],
        name="plan_rows",
    )(cnt, tab)


def _dispatch_kernel(cend_ref, pend_ref, nu_ref, dest_ref, h2_ref, xs_ref, zero_ref, sem, zsem):
    i = pl.program_id(0)

    def zero_padding(act):
        for e in range(N_EXPERTS):
            c_end = cend_ref[e]
            aligned = lax.shift_left(lax.shift_right_logical(c_end + 7, 3), 3)
            for r in range(7):

                @pl.when(c_end + r < aligned)
                def _():
                    act(pltpu.make_async_copy(_tile_rows(zero_ref, 0), _tile_rows(xs_ref, c_end + r), zsem))

            n = pend_ref[e] - aligned
            off = aligned
            for size in (128, 64, 32, 16, 8):

                @pl.when((n & size) != 0)
                def _():
                    act(pltpu.make_async_copy(_tile_rows(zero_ref, 0, size), _tile_rows(xs_ref, off, size),
                                              zsem))

                off = off + (n & size)

        def tail(b, carry):
            act(pltpu.make_async_copy(zero_ref, _tile_rows(xs_ref, b * RB, RB), zsem))
            return carry

        lax.fori_loop(nu_ref[0], N_BLOCKS, tail, 0)

    @pl.when(i == 0)
    def _():
        zero_ref[...] = jnp.zeros(zero_ref.shape, F32)
        zero_padding(lambda cp: cp.start())

    def body(t, carry):
        for k in range(TOP_K):
            pltpu.make_async_copy(_tile_rows(h2_ref, t), _tile_rows(xs_ref, dest_ref[0, k * TB + t]),
                                  sem).start(priority=k % 2)
        return carry

    lax.fori_loop(0, TB, body, 0, unroll=16)
    for k in range(TOP_K):
        pltpu.make_async_copy(h2_ref, _tile_rows(xs_ref, 0, TB), sem).wait()

    @pl.when(i == pl.num_programs(0) - 1)
    def _():
        zero_padding(lambda cp: cp.wait())


def _dispatch_call(cend, pend, n_used, dest, h2):
    nb = N_TOK // TB
    grid_spec = pltpu.PrefetchScalarGridSpec(
        num_scalar_prefetch=3,
        grid=(nb,),
        in_specs=[
            pl.BlockSpec((None, 1, TOP_K * TB), lambda i, c, p, n: (i, 0, 0), memory_space=pltpu.SMEM),
            pl.BlockSpec((TB * SUB, LANES), lambda i, c, p, n: (i, 0)),
        ],
        out_specs=pl.BlockSpec(memory_space=pl.ANY),
        scratch_shapes=[pltpu.VMEM((RB * SUB, LANES), F32), pltpu.SemaphoreType.DMA,
                        pltpu.SemaphoreType.DMA],
    )
    return pl.pallas_call(
        _dispatch_kernel,
        grid_spec=grid_spec,
        out_shape=jax.ShapeDtypeStruct((N_ROWS * SUB, LANES), F32),
        compiler_params=pltpu.CompilerParams(dimension_semantics=("arbitrary",),
                                             vmem_limit_bytes=VMEM_LIMIT),
        name="dispatch_rows",
    )(cend, pend, n_used, dest, h2)


def _moe_kernel(be_ref, first_ref, half_ref, nxt_ref, nxt2_ref, slot_ref, nu_ref, xs_ref, wgu_hbm, bgu_ref,
                wdn_hbm, bdn_ref, o_ref, wgu_f, wdn_f, sems, *, layer):
    b = pl.program_id(0)
    used = b < nu_ref[0]

    def fetch(e, s):
        return (pltpu.make_async_copy(wgu_hbm.at[layer, e], wgu_f.at[s], sems.at[0, s]),
                pltpu.make_async_copy(wdn_hbm.at[layer, e], wdn_f.at[s], sems.at[1, s]))

    @pl.when(b == 0)
    def _():
        for cp in fetch(be_ref[0], 0):
            cp.start(priority=1)

        @pl.when(nxt_ref[0] >= 0)
        def _():
            for cp in fetch(nxt_ref[0], 1):
                cp.start(priority=1)

    @pl.when(first_ref[b] == 1)
    def _():
        s = slot_ref[b]
        for cp in fetch(be_ref[b], s):
            cp.wait()

        @pl.when(nxt2_ref[b] >= 0)
        def _():
            for cp in fetch(nxt2_ref[b], (s + 2) % W_SLOTS):
                cp.start(priority=1)

    def experts(rows):
        s = slot_ref[b]
        x = _load_tiled(xs_ref, rows).astype(BF16)
        gu = jnp.dot(x, wgu_f[s].astype(BF16), preferred_element_type=F32) + bgu_ref[...]
        g = jnp.minimum(gu[:, :D_FF], SWIGLU_LIMIT)
        l = jnp.clip(gu[:, D_FF:], -SWIGLU_LIMIT, SWIGLU_LIMIT)
        a = g * jax.nn.sigmoid(SWIGLU_ALPHA * g) * (l + 1.0)
        y = jnp.dot(a.astype(BF16), wdn_f[s].astype(BF16), preferred_element_type=F32) + bdn_ref[...]
        _store_tiled(o_ref, y)
        if rows < RB:
            o_ref[rows * SUB:, :] = jnp.zeros(((RB - rows) * SUB, LANES), o_ref.dtype)

    half = half_ref[b] == 1

    @pl.when(jnp.logical_and(used, jnp.logical_not(half)))
    def _():
        experts(RB)

    @pl.when(jnp.logical_and(used, half))
    def _():
        experts(RB // 2)

    @pl.when(jnp.logical_not(used))
    def _():
        o_ref[...] = jnp.zeros(o_ref.shape, o_ref.dtype)


def _moe_call(layer, tables, xs, w_gu, b_gu, w_dn, b_dn):
    def rows_in(b, be, fi, ha, nx, n2, sl, nu):
        return (jnp.minimum(b, nu[0] - 1), 0)

    def rows_out(b, be, fi, ha, nx, n2, sl, nu):
        return (b, 0)

    def expert(b, be, fi, ha, nx, n2, sl, nu):
        return (layer, be[b], 0, 0)

    grid_spec = pltpu.PrefetchScalarGridSpec(
        num_scalar_prefetch=7,
        grid=(N_BLOCKS,),
        in_specs=[
            pl.BlockSpec((RB * SUB, LANES), rows_in),
            pl.BlockSpec(memory_space=pl.ANY),
            pl.BlockSpec((None, None, 1, 2 * D_FF), expert),
            pl.BlockSpec(memory_space=pl.ANY),
            pl.BlockSpec((None, None, 1, D), expert),
        ],
        out_specs=pl.BlockSpec((RB * SUB, LANES), rows_out),
        scratch_shapes=[pltpu.VMEM((W_SLOTS, D, 2 * D_FF), F32), pltpu.VMEM((W_SLOTS, D_FF, D), F32),
                        pltpu.SemaphoreType.DMA((2, W_SLOTS))],
    )
    return pl.pallas_call(
        functools.partial(_moe_kernel, layer=layer),
        grid_spec=grid_spec,
        out_shape=jax.ShapeDtypeStruct((N_ROWS * SUB, LANES), F32),
        compiler_params=pltpu.CompilerParams(dimension_semantics=("arbitrary",),
                                             vmem_limit_bytes=VMEM_LIMIT),
        name="moe_experts",
    )(*tables, xs, w_gu, b_gu, w_dn, b_dn)


def _final_kernel(*refs):
    _combine_body(pl.program_id(0), *refs)


def _combine_body(i, dcur_ref, dnxt_ref, x1_ref, ys_ref, rw_ref, mod_ref, g_ref, oc_ref, ol_ref, buf, sems):
    nb = pl.num_programs(0)
    slot = i % 2

    def gather(dest_ref, s):
        def body(t, carry):
            for k in range(TOP_K):
                pltpu.make_async_copy(_tile_rows(ys_ref, dest_ref[0, k * TB + t]),
                                      _tile_rows(buf.at[s], k * TB + t), sems.at[s]).start(priority=k % 2)
            return carry

        lax.fori_loop(0, TB, body, 0, unroll=16)

    @pl.when(i == 0)
    def _():
        gather(dcur_ref, 0)

    @pl.when(i + 1 < nb)
    def _():
        gather(dnxt_ref, 1 - slot)

    pltpu.make_async_copy(_tile_rows(ys_ref, 0, TOP_K * TB), buf.at[slot], sems.at[slot]).wait()

    rw = rw_ref[...]
    y = jnp.zeros((TB, D), F32)
    for k in range(TOP_K):
        yk = jnp.concatenate([buf[slot, pl.ds(k * TB * SUB + s, TB, stride=SUB), :] for s in range(SUB)],
                             axis=1)
        y = y + yk * rw[:, k:k + 1]
    x2 = x1_ref[...] + _rms(y, g_ref[...] * mod_ref[5:6, :])

    @pl.when(i < N_CTX // TB)
    def _():
        oc_ref[...] = x2

    @pl.when(i >= N_CTX // TB)
    def _():
        ol_ref[...] = x2


def _final_call(layer, dest_flat, x1, ys, rw, mods, g_post_ffn):
    nb = N_TOK // TB
    return pl.pallas_call(
        _final_kernel,
        grid=(nb,),
        in_specs=[pl.BlockSpec((None, 1, TOP_K * TB), lambda i: (i, 0, 0), memory_space=pltpu.SMEM),
                  pl.BlockSpec((None, 1, TOP_K * TB), lambda i: (jnp.minimum(i + 1, nb - 1), 0, 0),
                               memory_space=pltpu.SMEM),
                  pl.BlockSpec((TB, D), lambda i: (i, 0)),
                  pl.BlockSpec(memory_space=pl.ANY),
                  pl.BlockSpec((TB, LANES), lambda i: (i, 0)),
                  _mod_spec(layer), _layer_spec(layer, (1, D))],
        out_specs=(_ctx_spec(D), _lat_spec(D)),
        out_shape=(jax.ShapeDtypeStruct((N_CTX, D), F32), jax.ShapeDtypeStruct((N_LAT, D), F32)),
        scratch_shapes=[pltpu.VMEM((2, TOP_K * TB * SUB, LANES), F32), pltpu.SemaphoreType.DMA((2,))],
        compiler_params=pltpu.CompilerParams(dimension_semantics=("arbitrary",),
                                             vmem_limit_bytes=VMEM_LIMIT),
        name="combine_residual",
    )(dest_flat, dest_flat, x1, ys, rw, mods, g_post_ffn)


def _dft_tables(t):
    j = np.arange(t, dtype=np.int64)
    ang = 2.0 * np.pi * ((j[:, None] * j[None, :]) % t) / t
    return (np.cos(ang) / math.sqrt(t)).astype(np.float32), (np.sin(ang) / math.sqrt(t)).astype(np.float32)


def _channel_dft():
    c = np.arange(F_GROUP_DIM, dtype=np.int64)
    ang = 2.0 * np.pi * ((c[:, None] * c[None, :]) % F_GROUP_DIM) / F_GROUP_DIM
    eye = np.eye(F_GROUPS)
    bdc = np.kron(eye, np.cos(ang)) / math.sqrt(F_GROUP_DIM)
    bds = np.kron(eye, np.sin(ang)) / math.sqrt(F_GROUP_DIM)
    return bdc.astype(np.float32), bds.astype(np.float32)


def _rope_tables():
    pos = np.arange(LAT_T)
    n = ROPE // 4
    inv_freq = np.power(np.float32(ROPE_BASE), -np.arange(n, dtype=np.float32) / np.float32(n))
    ang_r = (pos // GRID_W).astype(np.float32)[:, None] * inv_freq
    ang_c = (pos % GRID_W).astype(np.float32)[:, None] * inv_freq
    cos = np.concatenate([np.cos(ang_r), np.cos(ang_r), np.cos(ang_c), np.cos(ang_c),
                          np.ones((LAT_T, LANES - ROPE))], axis=1)
    sin = np.concatenate([-np.sin(ang_r), np.sin(ang_r), -np.sin(ang_c), np.sin(ang_c),
                          np.zeros((LAT_T, LANES - ROPE))], axis=1)
    return cos.astype(np.float32), sin.astype(np.float32)


def kernel(x_prompt, x_sample, cache_ckv, cache_krope, c, c_ctx, w_ada, b_ada, g_pre_mix, g_post_mix, g_pre_ffn, g_post_ffn, w_in, g_sgu, w_spatial, b_spatial, g_q, w_uq, g_kv, w_ukv, w_out, w_router, b_router, w_gate_up, b_gate_up, w_down, b_down):
    xc, xl = x_prompt.reshape(N_CTX, D), x_sample.reshape(N_LAT, D)

    cond = jnp.concatenate([c_ctx[None, :], c, jnp.zeros((5, D), F32)], axis=0)
    mods = _mod_call(cond.T, w_ada, b_ada.reshape(DEPTH, 1, 6 * D))
    mods = mods[:, :3].reshape(DEPTH, 3, 6, D)

    gmat = jnp.asarray(np.kron(np.eye(A_HEADS), np.full((A_HEAD_DIM, A_HEAD_DIM), 1.0 / A_HEAD_DIM)),
                       dtype=BF16)
    bdc_np, bds_np = _channel_dft()
    bdc, bds = jnp.asarray(bdc_np).astype(BF16), jnp.asarray(bds_np).astype(BF16)
    dft = {}
    for t in (CTX_T, LAT_T):
        ct_np, st_np = _dft_tables(t)
        dft[t] = (jnp.asarray(ct_np).astype(BF16), jnp.asarray(st_np).astype(BF16))
    cos_np, sin_np = _rope_tables()
    cos_t, sin_t = jnp.asarray(cos_np), jnp.asarray(sin_np)
    tri = jnp.asarray(np.tril(np.ones((TB, TB), np.float32), k=-1)).astype(BF16)

    b_gu = b_gate_up.reshape(DEPTH, N_EXPERTS, 1, 2 * D_FF)
    b_dn = b_down.reshape(DEPTH, N_EXPERTS, 1, D)
    w_uq_p = jnp.pad(w_uq.reshape(DEPTH, Q_LORA, HEADS, NOPE + ROPE),
                     ((0, 0), (0, 0), (0, 0), (0, HEAD_PAD - NOPE - ROPE)))
    w_uq_p = w_uq_p.reshape(DEPTH, Q_LORA, HEADS * HEAD_PAD).astype(BF16)
    w_ukv_b = w_ukv.astype(BF16)
    ws = w_spatial.astype(BF16)
    bs_full = jnp.repeat(jnp.swapaxes(b_spatial, 1, 2), A_HEAD_DIM, axis=2)
    ckr_p = jnp.pad(cache_krope, ((0, 0), (0, 0), (0, 0), (0, LANES - ROPE))).astype(BF16)
    w_r = jnp.pad(w_router, ((0, 0), (0, 0), (0, LANES - N_EXPERTS)))
    b_r = jnp.pad(b_router, ((0, 0), (0, LANES - N_EXPERTS)), constant_values=NEG)[:, None, :]
    row = lambda g: g[:, None, :]

    pre_params = (mods, row(g_pre_mix), w_in, row(g_sgu), gmat, row(g_q), w_uq_p, row(g_kv), w_ukv_b,
                  cos_t, sin_t)

    ckv_layers, krope_layers = [], []
    for i in range(DEPTH):
        u, vn, zf, q, kv, kr, ckv, zkr = _pre_call(i, xc, xl, pre_params)
        ckv_layers.append(ckv[:N_CTX].reshape(N_CTX_B, CTX_T, KV_LORA))
        krope_layers.append(zkr[:N_CTX, :ROPE].reshape(N_CTX_B, CTX_T, ROPE))

        mix_in = (u, vn, zf, q, kv, kr)
        y_ctx = _mix_call(i, mix_in, (ws, bs_full, bdc, bds) + dft[CTX_T], CTX_T, N_CTX_B, 0)
        y_lat = _mix_call(i, mix_in, (ws, bs_full, bdc, bds) + dft[LAT_T], LAT_T, N_LAT_B, N_CTX,
                          cache=(cache_ckv, ckr_p, w_ukv_b))

        x1, h2, tab, rw, cnt = _post_call(i, y_ctx, y_lat, xc, xl, mods, w_out,
                                          row(g_post_mix), row(g_pre_ffn), w_r, b_r, tri)

        dest, cend, pend, n_used, block_e, first, half, nxt, nxt2, slot = _plan_call(cnt, tab)
        dest_flat = dest.reshape(N_TOK // TB, 1, TOP_K * TB)
        xs = _dispatch_call(cend, pend, n_used, dest_flat, h2)
        ys = _moe_call(i, (block_e, first, half, nxt, nxt2, slot, n_used), xs, w_gate_up, b_gu, w_down, b_dn)
        xc, xl = _final_call(i, dest_flat, x1, ys, rw, mods, row(g_post_ffn))

    y_prompt = xc.reshape(N_CTX_B, CTX_T, D)
    y_sample = xl.reshape(N_LAT_B, LAT_T, D)
    return (y_prompt, y_sample, jnp.stack(ckv_layers, axis=1), jnp.stack(krope_layers, axis=1))
```

```python
import functools
import math

import jax
import jax.numpy as jnp
import numpy as np
from jax import lax
from jax.experimental import pallas as pl
from jax.experimental.pallas import tpu as pltpu

F32 = jnp.float32
BF16 = jnp.bfloat16

D = 1024
N_CTX_B, CTX_T = 16, 256
N_LAT_B, LAT_T = 2, 1024
PAST = 512
N_CTX = N_CTX_B * CTX_T
N_LAT = N_LAT_B * LAT_T
N_TOK = N_CTX + N_LAT
DEPTH = 2
GRID_W = 64
EPS = 1e-6
A_HEADS, A_HEAD_DIM, A_WIDTH, CHUNK = 4, 64, 256, 128
F_GROUPS, F_GROUP_DIM, F_WIDTH = 4, 64, 256
HEADS, Q_LORA, KV_LORA, NOPE, ROPE, V_DIM = 4, 256, 128, 128, 64, 128
HEAD_PAD = 256
IN_COLS = 1216
N_EXPERTS, TOP_K, D_FF = 32, 4, 1024
SWIGLU_LIMIT, SWIGLU_ALPHA = 7.0, 1.702
ROPE_BASE = 10000.0

TB = 512
QB_MAX = 1024
RB = 256
W_SLOTS = 3
LANES = 128
SUB = D // LANES
N_BLOCKS = N_TOK * TOP_K // RB + N_EXPERTS
N_ROWS = N_BLOCKS * RB
NEG = -3.0e38
VMEM_LIMIT = 56 * 1024 * 1024


def _rms(x, g):
    return x * lax.rsqrt(jnp.mean(x * x, axis=-1, keepdims=True) + EPS) * g


def _split_dot(v, m):
    hi = v.astype(BF16)
    lo = (v - hi.astype(F32)).astype(BF16)
    return (jnp.dot(hi, m, preferred_element_type=F32)
            + jnp.dot(lo, m, preferred_element_type=F32))


def _dot_nt(a, b):
    return lax.dot_general(a, b, (((1,), (1,)), ((), ())), preferred_element_type=F32)


def _store_tiled(ref, x):
    rows = x.shape[0]
    for s in range(SUB):
        ref[pl.ds(s, rows, stride=SUB), :] = x[:, s * LANES:(s + 1) * LANES]


def _load_tiled(ref, rows):
    return jnp.concatenate([ref[pl.ds(s, rows, stride=SUB), :] for s in range(SUB)], axis=1)


def _tile_rows(ref, row, n_rows=1):
    return ref.at[pl.ds(pl.multiple_of(row * SUB, SUB), n_rows * SUB)]


def _mod_kernel(ct_ref, w_ref, b_ref, o_ref):
    ct = ct_ref[...]
    s = ct * jax.nn.sigmoid(ct)
    w = w_ref[...]
    o_ref[...] = jnp.zeros(o_ref.shape, F32)
    for r in range(3):
        o_ref[r:r + 1, :] = jnp.sum(w * s[:, r:r + 1], axis=0, keepdims=True) + b_ref[...]


def _mod_call(cond_t, w_ada, b_ada):
    cb = 2048
    return pl.pallas_call(
        _mod_kernel,
        grid=(DEPTH, 6 * D // cb),
        in_specs=[
            pl.BlockSpec((D, 8), lambda l, j: (0, 0)),
            pl.BlockSpec((None, D, cb), lambda l, j: (l, 0, j)),
            pl.BlockSpec((None, 1, cb), lambda l, j: (l, 0, j)),
        ],
        out_specs=pl.BlockSpec((None, 8, cb), lambda l, j: (l, 0, j)),
        out_shape=jax.ShapeDtypeStruct((DEPTH, 8, 6 * D), F32),
        compiler_params=pltpu.CompilerParams(dimension_semantics=("parallel", "parallel")),
        name="modulation",
    )(cond_t, w_ada, b_ada)


def _mod_index(i):
    first_lat = N_CTX // TB
    return jnp.where(i < first_lat, 0, 1 + (i - first_lat) // (LAT_T // TB))


def _ctx_spec(width):
    return pl.BlockSpec((TB, width), lambda i: (jnp.minimum(i, N_CTX // TB - 1), 0))


def _lat_spec(width):
    return pl.BlockSpec((TB, width), lambda i: (jnp.maximum(i - N_CTX // TB, 0), 0))


def _pick(i, ctx_ref, lat_ref):
    return jnp.where(i >= N_CTX // TB, lat_ref[...], ctx_ref[...])


def _layer_spec(layer, shape):
    zeros = (0,) * len(shape)
    return pl.BlockSpec((None,) + tuple(shape), lambda *_: (layer,) + zeros)


def _mod_spec(layer):
    return pl.BlockSpec((None, None, 6, D), lambda i: (layer, _mod_index(i), 0, 0))


def _swap_halves(x, lane):
    w = x.shape[-1]
    fwd = pltpu.roll(x, w - 16, 1)
    bwd = pltpu.roll(x, 16, 1)
    return jnp.where((lane & 31) < 16, fwd, bwd)


def _pre_kernel(xc_ref, xl_ref, *refs):
    i = pl.program_id(0)
    _pre_body(i, _pick(i, xc_ref, xl_ref), *refs)


def _pre_body(i, x, mod_ref, gpre_ref, win_ref, gsgu_ref, gmat_ref, gq_ref, wuq_ref,
              gkv_ref, wukv_ref, cos_ref, sin_ref,
              u_ref, vn_ref, zf_ref, q_ref, kv_ref, kr_ref, ckv_ref, zkr_ref):
    is_lat = i >= N_CTX // TB
    h = _rms(x, gpre_ref[...] * (1.0 + mod_ref[1:2, :])) + mod_ref[0:1, :]
    z = _dot_nt(h.astype(BF16), win_ref[...].astype(BF16))

    ga = jax.nn.gelu(z[:, :2 * A_WIDTH])
    u_ref[...] = ga[:, :A_WIDTH]
    v = ga[:, A_WIDTH:]
    gmat = gmat_ref[...]
    dv = v - _split_dot(v, gmat)
    var = _split_dot(dv * dv, gmat)
    vn_ref[...] = (dv * lax.rsqrt(var + EPS) * gsgu_ref[...]).astype(BF16)

    zf_ref[...] = z[:, 512:768].astype(BF16)

    cos = jnp.where(is_lat, cos_ref[...], 1.0)
    sin = jnp.where(is_lat, sin_ref[...], 0.0)
    lane = lax.broadcasted_iota(jnp.int32, (TB, LANES), 1)

    qn = _rms(z[:, 768:1024], gq_ref[...])
    scale = (NOPE + ROPE) ** -0.5
    q = jnp.dot(qn.astype(BF16), wuq_ref[...], preferred_element_type=F32) * scale
    for hd in range(HEADS):
        base = hd * HEAD_PAD
        q_ref[:, base:base + NOPE] = q[:, base:base + NOPE].astype(BF16)
        qr = q[:, base + NOPE:base + HEAD_PAD]
        q_ref[:, base + NOPE:base + HEAD_PAD] = (qr * cos + _swap_halves(qr, lane) * sin).astype(BF16)

    ckv = _rms(z[:, 1024:1152], gkv_ref[...])
    ckv_ref[...] = ckv
    kv_ref[...] = jnp.dot(ckv.astype(BF16), wukv_ref[...], preferred_element_type=F32).astype(BF16)

    zkr = jnp.concatenate([z[:, 1152:IN_COLS], jnp.zeros((TB, LANES - ROPE), F32)], axis=1)
    zkr_ref[...] = zkr
    kr_ref[...] = (zkr * cos + _swap_halves(zkr, lane) * sin).astype(BF16)


def _pre_specs(layer):
    lay = functools.partial(_layer_spec, layer)
    first_lat = N_CTX // TB
    pos_blocks = LAT_T // TB

    def tok(width):
        return pl.BlockSpec((TB, width), lambda i: (i, 0))

    def full(shape):
        return pl.BlockSpec(shape, lambda i: (0,) * len(shape))

    def rope_map(i):
        return (jnp.where(i >= first_lat, (i - first_lat) % pos_blocks, 0), 0)

    in_specs = [
        _mod_spec(layer),
        lay((1, D)), lay((IN_COLS, D)), lay((1, A_WIDTH)), full((A_WIDTH, A_WIDTH)),
        lay((1, Q_LORA)), lay((Q_LORA, HEADS * HEAD_PAD)),
        lay((1, KV_LORA)), lay((KV_LORA, HEADS * (NOPE + V_DIM))),
        pl.BlockSpec((TB, LANES), rope_map), pl.BlockSpec((TB, LANES), rope_map),
    ]
    out_specs = (tok(A_WIDTH), tok(A_WIDTH), tok(F_WIDTH), tok(HEADS * HEAD_PAD),
                 tok(HEADS * (NOPE + V_DIM)), tok(LANES), tok(KV_LORA), tok(LANES))
    out_shape = (
        jax.ShapeDtypeStruct((N_TOK, A_WIDTH), F32),
        jax.ShapeDtypeStruct((N_TOK, A_WIDTH), BF16),
        jax.ShapeDtypeStruct((N_TOK, F_WIDTH), BF16),
        jax.ShapeDtypeStruct((N_TOK, HEADS * HEAD_PAD), BF16),
        jax.ShapeDtypeStruct((N_TOK, HEADS * (NOPE + V_DIM)), BF16),
        jax.ShapeDtypeStruct((N_TOK, LANES), BF16),
        jax.ShapeDtypeStruct((N_TOK, KV_LORA), F32),
        jax.ShapeDtypeStruct((N_TOK, LANES), F32),
    )
    return in_specs, out_specs, out_shape


def _pre_call(layer, xc, xl, pre_params):
    in_specs, out_specs, out_shape = _pre_specs(layer)
    return pl.pallas_call(
        _pre_kernel,
        grid=(N_TOK // TB,),
        in_specs=[_ctx_spec(D), _lat_spec(D)] + in_specs,
        out_specs=out_specs,
        out_shape=out_shape,
        compiler_params=pltpu.CompilerParams(dimension_semantics=("parallel",),
                                             vmem_limit_bytes=VMEM_LIMIT),
        name="pre_mix",
    )(xc, xl, *pre_params)


def _mix_kernel(*refs, has_cache, qb, seq_t, n_seq):
    if has_cache:
        (u_ref, vn_ref, zf_ref, q_ref, kv_ref, kr_ref, ws_ref, bs_ref, bdc_ref, bds_ref,
         ct_ref, st_ref, cckv_ref, ckr_ref, wukv_ref, o_ref) = refs
    else:
        (u_ref, vn_ref, zf_ref, q_ref, kv_ref, kr_ref, ws_ref, bs_ref, bdc_ref, bds_ref,
         ct_ref, st_ref, o_ref) = refs

    lane = lax.broadcasted_iota(jnp.int32, (CHUNK, A_WIDTH), 1)
    for c in range(n_seq * qb // CHUNK):
        rows = slice(c * CHUNK, (c + 1) * CHUNK)
        vch = vn_ref[rows, :]
        s = bs_ref[...]
        for g in range(A_HEADS):
            sg = jnp.dot(ws_ref[g], vch, preferred_element_type=F32)
            in_head = (lane >= g * A_HEAD_DIM) & (lane < (g + 1) * A_HEAD_DIM)
            s = s + jnp.where(in_head, sg, 0.0)
        o_ref[rows, 0:A_WIDTH] = (u_ref[rows, :] * s).astype(BF16)

    if has_cache:
        kvc = jnp.dot(cckv_ref[...].astype(BF16), wukv_ref[...],
                      preferred_element_type=F32).astype(BF16)
        krc = ckr_ref[...]

    for sq in range(n_seq):
        qrows = slice(sq * qb, (sq + 1) * qb)
        srows = slice(sq * seq_t, (sq + 1) * seq_t)

        zf = zf_ref[srows, :]
        zc = jnp.dot(zf, bdc_ref[...], preferred_element_type=F32).astype(BF16)
        zs = jnp.dot(zf, bds_ref[...], preferred_element_type=F32).astype(BF16)
        yf = (jnp.dot(ct_ref[...], zc, preferred_element_type=F32)
              - jnp.dot(st_ref[...], zs, preferred_element_type=F32))
        o_ref[qrows, A_WIDTH:A_WIDTH + F_WIDTH] = yf.astype(BF16)

        kr = kr_ref[srows, :]
        for hd in range(HEADS):
            qh = q_ref[qrows, hd * HEAD_PAD:(hd + 1) * HEAD_PAD]
            kb = hd * (NOPE + V_DIM)
            kh = jnp.concatenate([kv_ref[srows, kb:kb + NOPE], kr], axis=1)
            vh = kv_ref[srows, kb + NOPE:kb + NOPE + V_DIM]
            s = _dot_nt(qh, kh)
            m = jnp.max(s, axis=-1, keepdims=True)
            if has_cache:
                khc = jnp.concatenate([kvc[:, kb:kb + NOPE], krc], axis=1)
                vhc = kvc[:, kb + NOPE:kb + NOPE + V_DIM]
                sc = _dot_nt(qh, khc)
                m = jnp.maximum(m, jnp.max(sc, axis=-1, keepdims=True))
            e = jnp.exp(s - m)
            den = jnp.sum(e, axis=-1, keepdims=True)
            o = jnp.dot(e.astype(BF16), vh, preferred_element_type=F32)
            if has_cache:
                ec = jnp.exp(sc - m)
                den = den + jnp.sum(ec, axis=-1, keepdims=True)
                o = o + jnp.dot(ec.astype(BF16), vhc, preferred_element_type=F32)
            ob = A_WIDTH + F_WIDTH + hd * V_DIM
            o_ref[qrows, ob:ob + V_DIM] = (o * (1.0 / den)).astype(BF16)


def _mix_call(layer, pre, consts, seq_t, n_batch, tok_off, cache=None):
    u, vn, zf, q, kv, kr = pre
    ws, bs_full, bdc, bds, ct, st = consts
    lay = functools.partial(_layer_spec, layer)
    qb = min(seq_t, QB_MAX)
    nq = seq_t // qb
    n_seq = QB_MAX // seq_t if nq == 1 else 1
    q_rows, s_rows = n_seq * qb, n_seq * seq_t
    qoff = tok_off // q_rows
    soff = tok_off // s_rows

    def qrow(width):
        return pl.BlockSpec((q_rows, width), lambda b, j: (qoff + b * nq + j, 0))

    def srow(width):
        return pl.BlockSpec((s_rows, width), lambda b, j: (soff + b, 0))

    def full(shape):
        return pl.BlockSpec(shape, lambda b, j: (0,) * len(shape))

    in_specs = [qrow(A_WIDTH), qrow(A_WIDTH), srow(F_WIDTH), qrow(HEADS * HEAD_PAD),
                srow(HEADS * (NOPE + V_DIM)), srow(LANES),
                lay((A_HEADS, CHUNK, CHUNK)), lay((CHUNK, A_WIDTH)),
                full((F_WIDTH, F_WIDTH)), full((F_WIDTH, F_WIDTH)),
                pl.BlockSpec((qb, seq_t), lambda b, j: (j, 0)),
                pl.BlockSpec((qb, seq_t), lambda b, j: (j, 0))]
    args = [u, vn, zf, q, kv, kr, ws, bs_full, bdc, bds, ct, st]
    if cache is not None:
        assert n_seq == 1
        cckv, ckr, wukv = cache
        in_specs += [pl.BlockSpec((None, None, PAST, KV_LORA), lambda b, j: (b, layer, 0, 0)),
                     pl.BlockSpec((None, None, PAST, LANES), lambda b, j: (b, layer, 0, 0)),
                     lay((KV_LORA, HEADS * (NOPE + V_DIM)))]
        args += [cckv, ckr, wukv]
    return pl.pallas_call(
        functools.partial(_mix_kernel, has_cache=cache is not None, qb=qb, seq_t=seq_t, n_seq=n_seq),
        grid=(n_batch // n_seq, nq),
        in_specs=in_specs,
        out_specs=pl.BlockSpec((q_rows, D), lambda b, j: (b * nq + j, 0)),
        out_shape=jax.ShapeDtypeStruct((n_batch * seq_t, D), BF16),
        compiler_params=pltpu.CompilerParams(dimension_semantics=("parallel", "parallel"),
                                             vmem_limit_bytes=VMEM_LIMIT),
        name="mix_lat" if cache is not None else "mix_ctx",
    )(*args)


def _post_kernel(yc_ref, yl_ref, xc_ref, xl_ref, mod_ref, wout_ref, gpost_ref, gffn_ref, wr_ref, br_ref,
                 tri_ref, x1_ref, h2_ref, tab_ref, rw_ref, cnt_ref, carry_ref):
    i = pl.program_id(0)

    @pl.when(i == 0)
    def _():
        carry_ref[...] = jnp.zeros(carry_ref.shape, F32)

    y = jnp.dot(_pick(i, yc_ref, yl_ref), wout_ref[...].astype(BF16), preferred_element_type=F32)
    x1 = _pick(i, xc_ref, xl_ref) + _rms(y, gpost_ref[...] * mod_ref[2:3, :])
    x1_ref[...] = x1
    h2 = _rms(x1, gffn_ref[...] * (1.0 + mod_ref[4:5, :])) + mod_ref[3:4, :]
    _store_tiled(h2_ref, h2)

    wr = wr_ref[...]
    wr_hi = wr.astype(BF16)
    wr_lo = (wr - wr_hi.astype(F32)).astype(BF16)
    h_hi = h2.astype(BF16)
    h_lo = (h2 - h_hi.astype(F32)).astype(BF16)
    hi_terms = jnp.dot(h_hi, jnp.concatenate([wr_hi, wr_lo], axis=1), preferred_element_type=F32)
    logits = (hi_terms[:, :LANES] + jnp.dot(h_lo, wr_hi, preferred_element_type=F32)
              + hi_terms[:, LANES:]) + br_ref[...]

    lane = lax.broadcasted_iota(jnp.int32, (TB, LANES), 1)
    lane_f = lane.astype(F32)
    work = logits
    idx, val = [], []
    for _ in range(TOP_K):
        m = jnp.max(work, axis=-1, keepdims=True)
        ik = jnp.min(jnp.where(work == m, lane_f, float(LANES)), axis=-1, keepdims=True)
        idx.append(ik)
        val.append(m)
        work = jnp.where(lane_f == ik, NEG, work)
    ex = [jnp.exp(v - val[0]) for v in val]
    den = ex[0] + ex[1] + ex[2] + ex[3]

    onehot = jnp.zeros((TB, LANES), F32)
    for k in range(TOP_K):
        onehot = onehot + jnp.where(lane_f == idx[k], 1.0, 0.0)
    before = jnp.dot(tri_ref[...], onehot.astype(BF16), preferred_element_type=F32) + carry_ref[0:1, :]
    ri = jnp.zeros((TB, LANES), F32)
    rw = jnp.zeros((TB, LANES), F32)
    for k in range(TOP_K):
        rank_k = jnp.sum(jnp.where(lane_f == idx[k], before, 0.0), axis=-1, keepdims=True)
        ri = ri + jnp.where(lane == k, idx[k], 0.0) + jnp.where(lane == TOP_K + k, rank_k, 0.0)
        rw = rw + jnp.where(lane == k, ex[k] / den, 0.0)
    tab_ref[...] = ri.T[0:2 * TOP_K, :].astype(jnp.int32)
    rw_ref[...] = rw
    total = carry_ref[0:1, :] + jnp.sum(onehot, axis=0, keepdims=True)
    carry_ref[...] = jnp.broadcast_to(total, carry_ref.shape)
    cnt_ref[...] = jnp.broadcast_to(total, cnt_ref.shape).astype(jnp.int32)


def _post_call(layer, yc, yl, xc, xl, mods, w_out, g_post, g_ffn, w_r, b_r, tri):
    nb = N_TOK // TB
    lay = functools.partial(_layer_spec, layer)

    def tok(width):
        return pl.BlockSpec((TB, width), lambda i: (i, 0))

    def full(shape):
        return pl.BlockSpec(shape, lambda i: (0,) * len(shape))

    return pl.pallas_call(
        _post_kernel,
        grid=(nb,),
        in_specs=[_ctx_spec(D), _lat_spec(D), _ctx_spec(D), _lat_spec(D), _mod_spec(layer),
                  lay((D, D)), lay((1, D)), lay((1, D)), lay((D, LANES)), lay((1, LANES)),
                  full((TB, TB))],
        out_specs=(tok(D), pl.BlockSpec((TB * SUB, LANES), lambda i: (i, 0)),
                   pl.BlockSpec((None, 2 * TOP_K, TB), lambda i: (i, 0, 0)),
                   tok(LANES), full((8, LANES))),
        out_shape=(jax.ShapeDtypeStruct((N_TOK, D), F32),
                   jax.ShapeDtypeStruct((N_TOK * SUB, LANES), F32),
                   jax.ShapeDtypeStruct((nb, 2 * TOP_K, TB), jnp.int32),
                   jax.ShapeDtypeStruct((N_TOK, LANES), F32),
                   jax.ShapeDtypeStruct((8, LANES), jnp.int32)),
        scratch_shapes=[pltpu.VMEM((8, LANES), F32)],
        compiler_params=pltpu.CompilerParams(dimension_semantics=("arbitrary",),
                                             vmem_limit_bytes=VMEM_LIMIT),
        name="post_mix_router",
    )(yc, yl, xc, xl, mods, w_out, g_post, g_ffn, w_r, b_r, tri)


def _plan_kernel(cnt_ref, tab_ref, dest_ref, cend_ref, pend_ref, nu_ref, be_ref, first_ref, half_ref, nxt_ref,
                 nxt2_ref, slot_ref, start_s, next_s):
    nxt = jnp.int32(-1)
    for e in reversed(range(N_EXPERTS)):
        next_s[e] = nxt
        nxt = jnp.where(cnt_ref[0, e] > 0, jnp.int32(e), nxt)

    start = jnp.int32(0)
    run = jnp.int32(0)
    last = jnp.int32(0)
    for e in range(N_EXPERTS):
        c = cnt_ref[0, e]
        n_blk = lax.shift_right_logical(c + (RB - 1), RB.bit_length() - 1)
        start_s[e] = start
        cend_ref[e] = start * RB + c
        pend_ref[e] = (start + n_blk) * RB
        few = (c - (n_blk - 1) * RB) <= RB // 2
        n1 = next_s[e]
        n2 = jnp.where(n1 >= 0, next_s[jnp.maximum(n1, 0)], jnp.int32(-1))
        slot = lax.rem(run, W_SLOTS)

        def fill(j, carry, e=e, n1=n1, n2=n2, slot=slot, start=start, n_blk=n_blk, few=few):
            be_ref[j] = jnp.int32(e)
            first_ref[j] = (j == start).astype(jnp.int32)
            half_ref[j] = jnp.logical_and(j == start + n_blk - 1, few).astype(jnp.int32)
            nxt_ref[j] = n1
            nxt2_ref[j] = n2
            slot_ref[j] = slot
            return carry

        lax.fori_loop(start, start + n_blk, fill, 0)
        owns = n_blk > 0
        last = jnp.where(owns, jnp.int32(e), last)
        run = run + owns.astype(jnp.int32)
        start = start + n_blk
    nu_ref[0] = start

    def tail(j, carry):
        be_ref[j] = last
        first_ref[j] = jnp.int32(0)
        half_ref[j] = jnp.int32(0)
        nxt_ref[j] = jnp.int32(-1)
        nxt2_ref[j] = jnp.int32(-1)
        slot_ref[j] = jnp.int32(0)
        return carry

    lax.fori_loop(start, N_BLOCKS, tail, 0)

    idx = tab_ref[:, 0:TOP_K, :]
    base = jnp.zeros(idx.shape, jnp.int32)
    for e in range(N_EXPERTS):
        base = jnp.where(idx == e, start_s[e] * RB, base)
    dest_ref[...] = base + tab_ref[:, TOP_K:2 * TOP_K, :]


def _plan_call(cnt, tab):
    nb = N_TOK // TB
    smem = pl.BlockSpec(memory_space=pltpu.SMEM)
    blocks = jax.ShapeDtypeStruct((N_BLOCKS,), jnp.int32)
    return pl.pallas_call(
        _plan_kernel,
        in_specs=[smem, pl.BlockSpec(memory_space=pltpu.VMEM)],
        out_specs=(pl.BlockSpec(memory_space=pltpu.VMEM),) + (smem,) * 9,
        out_shape=(jax.ShapeDtypeStruct((nb, TOP_K, TB), jnp.int32),
                   jax.ShapeDtypeStruct((N_EXPERTS,), jnp.int32),
                   jax.ShapeDtypeStruct((N_EXPERTS,), jnp.int32),
                   jax.ShapeDtypeStruct((1,), jnp.int32),
                   blocks, blocks, blocks, blocks, blocks, blocks),
        scratch_shapes=[pltpu.SMEM((N_EXPERTS,), jnp.int32), pltpu.SMEM((N_EXPERTS,), jnp.int32)],
        name="plan_rows",
    )(cnt, tab)


def _dispatch_kernel(cend_ref, pend_ref, nu_ref, dest_ref, h2_ref, xs_ref, zero_ref, sem, zsem):
    i = pl.program_id(0)

    def zero_padding(act):
        for e in range(N_EXPERTS):
            c_end = cend_ref[e]
            aligned = lax.shift_left(lax.shift_right_logical(c_end + 7, 3), 3)
            for r in range(7):

                @pl.when(c_end + r < aligned)
                def _():
                    act(pltpu.make_async_copy(_tile_rows(zero_ref, 0), _tile_rows(xs_ref, c_end + r), zsem))

            n = pend_ref[e] - aligned
            off = aligned
            for size in (128, 64, 32, 16, 8):

                @pl.when((n & size) != 0)
                def _():
                    act(pltpu.make_async_copy(_tile_rows(zero_ref, 0, size), _tile_rows(xs_ref, off, size),
                                              zsem))

                off = off + (n & size)

        def tail(b, carry):
            act(pltpu.make_async_copy(zero_ref, _tile_rows(xs_ref, b * RB, RB), zsem))
            return carry

        lax.fori_loop(nu_ref[0], N_BLOCKS, tail, 0)

    @pl.when(i == 0)
    def _():
        zero_ref[...] = jnp.zeros(zero_ref.shape, F32)
        zero_padding(lambda cp: cp.start())

    def body(t, carry):
        for k in range(TOP_K):
            pltpu.make_async_copy(_tile_rows(h2_ref, t), _tile_rows(xs_ref, dest_ref[0, k * TB + t]),
                                  sem).start(priority=k % 2)
        return carry

    lax.fori_loop(0, TB, body, 0, unroll=8)
    for k in range(TOP_K):
        pltpu.make_async_copy(h2_ref, _tile_rows(xs_ref, 0, TB), sem).wait()

    @pl.when(i == pl.num_programs(0) - 1)
    def _():
        zero_padding(lambda cp: cp.wait())


def _dispatch_call(cend, pend, n_used, dest, h2):
    nb = N_TOK // TB
    grid_spec = pltpu.PrefetchScalarGridSpec(
        num_scalar_prefetch=3,
        grid=(nb,),
        in_specs=[
            pl.BlockSpec((None, 1, TOP_K * TB), lambda i, c, p, n: (i, 0, 0), memory_space=pltpu.SMEM),
            pl.BlockSpec((TB * SUB, LANES), lambda i, c, p, n: (i, 0)),
        ],
        out_specs=pl.BlockSpec(memory_space=pl.ANY),
        scratch_shapes=[pltpu.VMEM((RB * SUB, LANES), F32), pltpu.SemaphoreType.DMA,
                        pltpu.SemaphoreType.DMA],
    )
    return pl.pallas_call(
        _dispatch_kernel,
        grid_spec=grid_spec,
        out_shape=jax.ShapeDtypeStruct((N_ROWS * SUB, LANES), F32),
        compiler_params=pltpu.CompilerParams(dimension_semantics=("arbitrary",),
                                             vmem_limit_bytes=VMEM_LIMIT),
        name="dispatch_rows",
    )(cend, pend, n_used, dest, h2)


def _moe_kernel(be_ref, first_ref, half_ref, nxt_ref, nxt2_ref, slot_ref, nu_ref, xs_ref, wgu_hbm, bgu_ref,
                wdn_hbm, bdn_ref, o_ref, wgu_f, wdn_f, sems, *, layer):
    b = pl.program_id(0)
    used = b < nu_ref[0]

    def fetch(e, s):
        return (pltpu.make_async_copy(wgu_hbm.at[layer, e], wgu_f.at[s], sems.at[0, s]),
                pltpu.make_async_copy(wdn_hbm.at[layer, e], wdn_f.at[s], sems.at[1, s]))

    @pl.when(b == 0)
    def _():
        for cp in fetch(be_ref[0], 0):
            cp.start(priority=1)

        @pl.when(nxt_ref[0] >= 0)
        def _():
            for cp in fetch(nxt_ref[0], 1):
                cp.start(priority=1)

    @pl.when(first_ref[b] == 1)
    def _():
        s = slot_ref[b]
        for cp in fetch(be_ref[b], s):
            cp.wait()

        @pl.when(nxt2_ref[b] >= 0)
        def _():
            for cp in fetch(nxt2_ref[b], (s + 2) % W_SLOTS):
                cp.start(priority=1)

    def experts(rows):
        s = slot_ref[b]
        x = _load_tiled(xs_ref, rows).astype(BF16)
        gu = jnp.dot(x, wgu_f[s].astype(BF16), preferred_element_type=F32) + bgu_ref[...]
        g = jnp.minimum(gu[:, :D_FF], SWIGLU_LIMIT)
        l = jnp.clip(gu[:, D_FF:], -SWIGLU_LIMIT, SWIGLU_LIMIT)
        a = g * jax.nn.sigmoid(SWIGLU_ALPHA * g) * (l + 1.0)
        y = jnp.dot(a.astype(BF16), wdn_f[s].astype(BF16), preferred_element_type=F32) + bdn_ref[...]
        _store_tiled(o_ref, y)
        if rows < RB:
            o_ref[rows * SUB:, :] = jnp.zeros(((RB - rows) * SUB, LANES), o_ref.dtype)

    half = half_ref[b] == 1

    @pl.when(jnp.logical_and(used, jnp.logical_not(half)))
    def _():
        experts(RB)

    @pl.when(jnp.logical_and(used, half))
    def _():
        experts(RB // 2)

    @pl.when(jnp.logical_not(used))
    def _():
        o_ref[...] = jnp.zeros(o_ref.shape, o_ref.dtype)


def _moe_call(layer, tables, xs, w_gu, b_gu, w_dn, b_dn):
    def rows_in(b, be, fi, ha, nx, n2, sl, nu):
        return (jnp.minimum(b, nu[0] - 1), 0)

    def rows_out(b, be, fi, ha, nx, n2, sl, nu):
        return (b, 0)

    def expert(b, be, fi, ha, nx, n2, sl, nu):
        return (layer, be[b], 0, 0)

    grid_spec = pltpu.PrefetchScalarGridSpec(
        num_scalar_prefetch=7,
        grid=(N_BLOCKS,),
        in_specs=[
            pl.BlockSpec((RB * SUB, LANES), rows_in),
            pl.BlockSpec(memory_space=pl.ANY),
            pl.BlockSpec((None, None, 1, 2 * D_FF), expert),
            pl.BlockSpec(memory_space=pl.ANY),
            pl.BlockSpec((None, None, 1, D), expert),
        ],
        out_specs=pl.BlockSpec((RB * SUB, LANES), rows_out),
        scratch_shapes=[pltpu.VMEM((W_SLOTS, D, 2 * D_FF), F32), pltpu.VMEM((W_SLOTS, D_FF, D), F32),
                        pltpu.SemaphoreType.DMA((2, W_SLOTS))],
    )
    return pl.pallas_call(
        functools.partial(_moe_kernel, layer=layer),
        grid_spec=grid_spec,
        out_shape=jax.ShapeDtypeStruct((N_ROWS * SUB, LANES), F32),
        compiler_params=pltpu.CompilerParams(dimension_semantics=("arbitrary",),
                                             vmem_limit_bytes=VMEM_LIMIT),
        name="moe_experts",
    )(*tables, xs, w_gu, b_gu, w_dn, b_dn)


def _final_kernel(*refs):
    _combine_body(pl.program_id(0), *refs)


def _combine_body(i, dcur_ref, dnxt_ref, x1_ref, ys_ref, rw_ref, mod_ref, g_ref, oc_ref, ol_ref, buf, sems):
    nb = pl.num_programs(0)
    slot = i % 2

    def gather(dest_ref, s):
        def body(t, carry):
            for k in range(TOP_K):
                pltpu.make_async_copy(_tile_rows(ys_ref, dest_ref[0, k * TB + t]),
                                      _tile_rows(buf.at[s], k * TB + t), sems.at[s]).start(priority=k % 2)
            return carry

        lax.fori_loop(0, TB, body, 0, unroll=8)

    @pl.when(i == 0)
    def _():
        gather(dcur_ref, 0)

    @pl.when(i + 1 < nb)
    def _():
        gather(dnxt_ref, 1 - slot)

    pltpu.make_async_copy(_tile_rows(ys_ref, 0, TOP_K * TB), buf.at[slot], sems.at[slot]).wait()

    rw = rw_ref[...]
    y = jnp.zeros((TB, D), F32)
    for k in range(TOP_K):
        yk = jnp.concatenate([buf[slot, pl.ds(k * TB * SUB + s, TB, stride=SUB), :] for s in range(SUB)],
                             axis=1)
        y = y + yk * rw[:, k:k + 1]
    x2 = x1_ref[...] + _rms(y, g_ref[...] * mod_ref[5:6, :])

    @pl.when(i < N_CTX // TB)
    def _():
        oc_ref[...] = x2

    @pl.when(i >= N_CTX // TB)
    def _():
        ol_ref[...] = x2


def _final_call(layer, dest_flat, x1, ys, rw, mods, g_post_ffn):
    nb = N_TOK // TB
    return pl.pallas_call(
        _final_kernel,
        grid=(nb,),
        in_specs=[pl.BlockSpec((None, 1, TOP_K * TB), lambda i: (i, 0, 0), memory_space=pltpu.SMEM),
                  pl.BlockSpec((None, 1, TOP_K * TB), lambda i: (jnp.minimum(i + 1, nb - 1), 0, 0),
                               memory_space=pltpu.SMEM),
                  pl.BlockSpec((TB, D), lambda i: (i, 0)),
                  pl.BlockSpec(memory_space=pl.ANY),
                  pl.BlockSpec((TB, LANES), lambda i: (i, 0)),
                  _mod_spec(layer), _layer_spec(layer, (1, D))],
        out_specs=(_ctx_spec(D), _lat_spec(D)),
        out_shape=(jax.ShapeDtypeStruct((N_CTX, D), F32), jax.ShapeDtypeStruct((N_LAT, D), F32)),
        scratch_shapes=[pltpu.VMEM((2, TOP_K * TB * SUB, LANES), F32), pltpu.SemaphoreType.DMA((2,))],
        compiler_params=pltpu.CompilerParams(dimension_semantics=("arbitrary",),
                                             vmem_limit_bytes=VMEM_LIMIT),
        name="combine_residual",
    )(dest_flat, dest_flat, x1, ys, rw, mods, g_post_ffn)


def _dft_tables(t):
    j = np.arange(t, dtype=np.int64)
    ang = 2.0 * np.pi * ((j[:, None] * j[None, :]) % t) / t
    return (np.cos(ang) / math.sqrt(t)).astype(np.float32), (np.sin(ang) / math.sqrt(t)).astype(np.float32)


def _channel_dft():
    c = np.arange(F_GROUP_DIM, dtype=np.int64)
    ang = 2.0 * np.pi * ((c[:, None] * c[None, :]) % F_GROUP_DIM) / F_GROUP_DIM
    eye = np.eye(F_GROUPS)
    bdc = np.kron(eye, np.cos(ang)) / math.sqrt(F_GROUP_DIM)
    bds = np.kron(eye, np.sin(ang)) / math.sqrt(F_GROUP_DIM)
    return bdc.astype(np.float32), bds.astype(np.float32)


def _rope_tables():
    pos = np.arange(LAT_T)
    n = ROPE // 4
    inv_freq = np.power(np.float32(ROPE_BASE), -np.arange(n, dtype=np.float32) / np.float32(n))
    ang_r = (pos // GRID_W).astype(np.float32)[:, None] * inv_freq
    ang_c = (pos % GRID_W).astype(np.float32)[:, None] * inv_freq
    cos = np.concatenate([np.cos(ang_r), np.cos(ang_r), np.cos(ang_c), np.cos(ang_c),
                          np.ones((LAT_T, LANES - ROPE))], axis=1)
    sin = np.concatenate([-np.sin(ang_r), np.sin(ang_r), -np.sin(ang_c), np.sin(ang_c),
                          np.zeros((LAT_T, LANES - ROPE))], axis=1)
    return cos.astype(np.float32), sin.astype(np.float32)


def kernel(x_prompt, x_sample, cache_ckv, cache_krope, c, c_ctx, w_ada, b_ada, g_pre_mix, g_post_mix, g_pre_ffn, g_post_ffn, w_in, g_sgu, w_spatial, b_spatial, g_q, w_uq, g_kv, w_ukv, w_out, w_router, b_router, w_gate_up, b_gate_up, w_down, b_down):
    xc, xl = x_prompt.reshape(N_CTX, D), x_sample.reshape(N_LAT, D)

    cond = jnp.concatenate([c_ctx[None, :], c, jnp.zeros((5, D), F32)], axis=0)
    mods = _mod_call(cond.T, w_ada, b_ada.reshape(DEPTH, 1, 6 * D))
    mods = mods[:, :3].reshape(DEPTH, 3, 6, D)

    gmat = jnp.asarray(np.kron(np.eye(A_HEADS), np.full((A_HEAD_DIM, A_HEAD_DIM), 1.0 / A_HEAD_DIM)),
                       dtype=BF16)
    bdc_np, bds_np = _channel_dft()
    bdc, bds = jnp.asarray(bdc_np).astype(BF16), jnp.asarray(bds_np).astype(BF16)
    dft = {}
    for t in (CTX_T, LAT_T):
        ct_np, st_np = _dft_tables(t)
        dft[t] = (jnp.asarray(ct_np).astype(BF16), jnp.asarray(st_np).astype(BF16))
    cos_np, sin_np = _rope_tables()
    cos_t, sin_t = jnp.asarray(cos_np), jnp.asarray(sin_np)
    tri = jnp.asarray(np.tril(np.ones((TB, TB), np.float32), k=-1)).astype(BF16)

    b_gu = b_gate_up.reshape(DEPTH, N_EXPERTS, 1, 2 * D_FF)
    b_dn = b_down.reshape(DEPTH, N_EXPERTS, 1, D)
    w_uq_p = jnp.pad(w_uq.reshape(DEPTH, Q_LORA, HEADS, NOPE + ROPE),
                     ((0, 0), (0, 0), (0, 0), (0, HEAD_PAD - NOPE - ROPE)))
    w_uq_p = w_uq_p.reshape(DEPTH, Q_LORA, HEADS * HEAD_PAD).astype(BF16)
    w_ukv_b = w_ukv.astype(BF16)
    ws = w_spatial.astype(BF16)
    bs_full = jnp.repeat(jnp.swapaxes(b_spatial, 1, 2), A_HEAD_DIM, axis=2)
    ckr_p = jnp.pad(cache_krope, ((0, 0), (0, 0), (0, 0), (0, LANES - ROPE))).astype(BF16)
    w_r = jnp.pad(w_router, ((0, 0), (0, 0), (0, LANES - N_EXPERTS)))
    b_r = jnp.pad(b_router, ((0, 0), (0, LANES - N_EXPERTS)), constant_values=NEG)[:, None, :]
    row = lambda g: g[:, None, :]

    w_in_t = jnp.swapaxes(w_in, 1, 2)
    pre_params = (mods, row(g_pre_mix), w_in_t, row(g_sgu), gmat, row(g_q), w_uq_p, row(g_kv), w_ukv_b,
                  cos_t, sin_t)

    ckv_layers, krope_layers = [], []
    for i in range(DEPTH):
        u, vn, zf, q, kv, kr, ckv, zkr = _pre_call(i, xc, xl, pre_params)
        ckv_layers.append(ckv[:N_CTX].reshape(N_CTX_B, CTX_T, KV_LORA))
        krope_layers.append(zkr[:N_CTX, :ROPE].reshape(N_CTX_B, CTX_T, ROPE))

        mix_in = (u, vn, zf, q, kv, kr)
        y_ctx = _mix_call(i, mix_in, (ws, bs_full, bdc, bds) + dft[CTX_T], CTX_T, N_CTX_B, 0)
        y_lat = _mix_call(i, mix_in, (ws, bs_full, bdc, bds) + dft[LAT_T], LAT_T, N_LAT_B, N_CTX,
                          cache=(cache_ckv, ckr_p, w_ukv_b))

        x1, h2, tab, rw, cnt = _post_call(i, y_ctx, y_lat, xc, xl, mods, w_out,
                                          row(g_post_mix), row(g_pre_ffn), w_r, b_r, tri)

        dest, cend, pend, n_used, block_e, first, half, nxt, nxt2, slot = _plan_call(cnt, tab)
        dest_flat = dest.reshape(N_TOK // TB, 1, TOP_K * TB)
        xs = _dispatch_call(cend, pend, n_used, dest_flat, h2)
        ys = _moe_call(i, (block_e, first, half, nxt, nxt2, slot, n_used), xs, w_gate_up, b_gu, w_down, b_dn)
        xc, xl = _final_call(i, dest_flat, x1, ys, rw, mods, row(g_post_ffn))

    y_prompt = xc.reshape(N_CTX_B, CTX_T, D)
    y_sample = xl.reshape(N_LAT_B, LAT_T, D)
    return (y_prompt, y_sample, jnp.stack(ckv_layers, axis=1), jnp.stack(krope_layers, axis=1))
```

```python
import functools
import math

import jax
import jax.numpy as jnp
import numpy as np
from jax import lax
from jax.experimental import pallas as pl
from jax.experimental.pallas import tpu as pltpu

F32 = jnp.float32
BF16 = jnp.bfloat16

D = 1024
N_CTX_B, CTX_T = 16, 256
N_LAT_B, LAT_T = 2, 1024
PAST = 512
N_CTX = N_CTX_B * CTX_T
N_LAT = N_LAT_B * LAT_T
N_TOK = N_CTX + N_LAT
DEPTH = 2
GRID_W = 64
EPS = 1e-6
A_HEADS, A_HEAD_DIM, A_WIDTH, CHUNK = 4, 64, 256, 128
F_GROUPS, F_GROUP_DIM, F_WIDTH = 4, 64, 256
HEADS, Q_LORA, KV_LORA, NOPE, ROPE, V_DIM = 4, 256, 128, 128, 64, 128
HEAD_PAD = 256
IN_COLS = 1216
N_EXPERTS, TOP_K, D_FF = 32, 4, 1024
SWIGLU_LIMIT, SWIGLU_ALPHA = 7.0, 1.702
ROPE_BASE = 10000.0

TB = 512
QB_MAX = 1024
MIX_ROWS = 2048
RB = 256
W_SLOTS = 3
LANES = 128
SUB = D // LANES
N_BLOCKS = N_TOK * TOP_K // RB + N_EXPERTS
N_ROWS = N_BLOCKS * RB
NEG = -3.0e38
VMEM_LIMIT = 56 * 1024 * 1024


def _rms(x, g):
    return x * lax.rsqrt(jnp.mean(x * x, axis=-1, keepdims=True) + EPS) * g


def _split_dot(v, m):
    hi = v.astype(BF16)
    lo = (v - hi.astype(F32)).astype(BF16)
    return (jnp.dot(hi, m, preferred_element_type=F32)
            + jnp.dot(lo, m, preferred_element_type=F32))


def _dot_nt(a, b):
    return lax.dot_general(a, b, (((1,), (1,)), ((), ())), preferred_element_type=F32)


def _store_tiled(ref, x):
    rows = x.shape[0]
    for s in range(SUB):
        ref[pl.ds(s, rows, stride=SUB), :] = x[:, s * LANES:(s + 1) * LANES]


def _load_tiled(ref, rows):
    return jnp.concatenate([ref[pl.ds(s, rows, stride=SUB), :] for s in range(SUB)], axis=1)


def _tile_rows(ref, row, n_rows=1):
    return ref.at[pl.ds(pl.multiple_of(row * SUB, SUB), n_rows * SUB)]


def _mod_kernel(ct_ref, w_ref, b_ref, o_ref):
    ct = ct_ref[...]
    s = ct * jax.nn.sigmoid(ct)
    w = w_ref[...]
    o_ref[...] = jnp.zeros(o_ref.shape, F32)
    for r in range(3):
        o_ref[r:r + 1, :] = jnp.sum(w * s[:, r:r + 1], axis=0, keepdims=True) + b_ref[...]


def _mod_call(cond_t, w_ada, b_ada):
    cb = 2048
    return pl.pallas_call(
        _mod_kernel,
        grid=(DEPTH, 6 * D // cb),
        in_specs=[
            pl.BlockSpec((D, 8), lambda l, j: (0, 0)),
            pl.BlockSpec((None, D, cb), lambda l, j: (l, 0, j)),
            pl.BlockSpec((None, 1, cb), lambda l, j: (l, 0, j)),
        ],
        out_specs=pl.BlockSpec((None, 8, cb), lambda l, j: (l, 0, j)),
        out_shape=jax.ShapeDtypeStruct((DEPTH, 8, 6 * D), F32),
        compiler_params=pltpu.CompilerParams(dimension_semantics=("parallel", "parallel")),
        name="modulation",
    )(cond_t, w_ada, b_ada)


def _mod_index(i):
    first_lat = N_CTX // TB
    return jnp.where(i < first_lat, 0, 1 + (i - first_lat) // (LAT_T // TB))


def _ctx_spec(width):
    return pl.BlockSpec((TB, width), lambda i: (jnp.minimum(i, N_CTX // TB - 1), 0))


def _lat_spec(width):
    return pl.BlockSpec((TB, width), lambda i: (jnp.maximum(i - N_CTX // TB, 0), 0))


def _pick(i, ctx_ref, lat_ref):
    return jnp.where(i >= N_CTX // TB, lat_ref[...], ctx_ref[...])


def _layer_spec(layer, shape):
    zeros = (0,) * len(shape)
    return pl.BlockSpec((None,) + tuple(shape), lambda *_: (layer,) + zeros)


def _mod_spec(layer):
    return pl.BlockSpec((None, None, 6, D), lambda i: (layer, _mod_index(i), 0, 0))


def _swap_halves(x, lane):
    w = x.shape[-1]
    fwd = pltpu.roll(x, w - 16, 1)
    bwd = pltpu.roll(x, 16, 1)
    return jnp.where((lane & 31) < 16, fwd, bwd)


def _pre_kernel(xc_ref, xl_ref, *refs):
    i = pl.program_id(0)
    _pre_body(i, _pick(i, xc_ref, xl_ref), *refs)


def _pre_body(i, x, mod_ref, gpre_ref, win_ref, gsgu_ref, gmat_ref, gq_ref, wuq_ref,
              gkv_ref, wukv_ref, cos_ref, sin_ref,
              u_ref, vn_ref, zf_ref, q_ref, kv_ref, kr_ref, ckv_ref, zkr_ref):
    is_lat = i >= N_CTX // TB
    h = _rms(x, gpre_ref[...] * (1.0 + mod_ref[1:2, :])) + mod_ref[0:1, :]
    z = _dot_nt(h.astype(BF16), win_ref[...].astype(BF16))

    ga = jax.nn.gelu(z[:, :2 * A_WIDTH])
    u_ref[...] = ga[:, :A_WIDTH]
    v = ga[:, A_WIDTH:]
    gmat = gmat_ref[...]
    dv = v - _split_dot(v, gmat)
    var = _split_dot(dv * dv, gmat)
    vn_ref[...] = (dv * lax.rsqrt(var + EPS) * gsgu_ref[...]).astype(BF16)

    zf_ref[...] = z[:, 512:768].astype(BF16)

    cos = jnp.where(is_lat, cos_ref[...], 1.0)
    sin = jnp.where(is_lat, sin_ref[...], 0.0)
    lane = lax.broadcasted_iota(jnp.int32, (TB, LANES), 1)

    qn = _rms(z[:, 768:1024], gq_ref[...])
    scale = (NOPE + ROPE) ** -0.5
    q = jnp.dot(qn.astype(BF16), wuq_ref[...], preferred_element_type=F32) * scale
    for hd in range(HEADS):
        base = hd * HEAD_PAD
        q_ref[:, base:base + NOPE] = q[:, base:base + NOPE].astype(BF16)
        qr = q[:, base + NOPE:base + HEAD_PAD]
        q_ref[:, base + NOPE:base + HEAD_PAD] = (qr * cos + _swap_halves(qr, lane) * sin).astype(BF16)

    ckv = _rms(z[:, 1024:1152], gkv_ref[...])
    ckv_ref[...] = ckv
    kv_ref[...] = jnp.dot(ckv.astype(BF16), wukv_ref[...], preferred_element_type=F32).astype(BF16)

    zkr = jnp.concatenate([z[:, 1152:IN_COLS], jnp.zeros((TB, LANES - ROPE), F32)], axis=1)
    zkr_ref[...] = zkr
    kr_ref[...] = (zkr * cos + _swap_halves(zkr, lane) * sin).astype(BF16)


def _pre_specs(layer):
    lay = functools.partial(_layer_spec, layer)
    first_lat = N_CTX // TB
    pos_blocks = LAT_T // TB

    def tok(width):
        return pl.BlockSpec((TB, width), lambda i: (i, 0))

    def full(shape):
        return pl.BlockSpec(shape, lambda i: (0,) * len(shape))

    def rope_map(i):
        return (jnp.where(i >= first_lat, (i - first_lat) % pos_blocks, 0), 0)

    in_specs = [
        _mod_spec(layer),
        lay((1, D)), lay((IN_COLS, D)), lay((1, A_WIDTH)), full((A_WIDTH, A_WIDTH)),
        lay((1, Q_LORA)), lay((Q_LORA, HEADS * HEAD_PAD)),
        lay((1, KV_LORA)), lay((KV_LORA, HEADS * (NOPE + V_DIM))),
        pl.BlockSpec((TB, LANES), rope_map), pl.BlockSpec((TB, LANES), rope_map),
    ]
    out_specs = (tok(A_WIDTH), tok(A_WIDTH), tok(F_WIDTH), tok(HEADS * HEAD_PAD),
                 tok(HEADS * (NOPE + V_DIM)), tok(LANES), tok(KV_LORA), tok(LANES))
    out_shape = (
        jax.ShapeDtypeStruct((N_TOK, A_WIDTH), F32),
        jax.ShapeDtypeStruct((N_TOK, A_WIDTH), BF16),
        jax.ShapeDtypeStruct((N_TOK, F_WIDTH), BF16),
        jax.ShapeDtypeStruct((N_TOK, HEADS * HEAD_PAD), BF16),
        jax.ShapeDtypeStruct((N_TOK, HEADS * (NOPE + V_DIM)), BF16),
        jax.ShapeDtypeStruct((N_TOK, LANES), BF16),
        jax.ShapeDtypeStruct((N_TOK, KV_LORA), F32),
        jax.ShapeDtypeStruct((N_TOK, LANES), F32),
    )
    return in_specs, out_specs, out_shape


def _pre_call(layer, xc, xl, pre_params):
    in_specs, out_specs, out_shape = _pre_specs(layer)
    return pl.pallas_call(
        _pre_kernel,
        grid=(N_TOK // TB,),
        in_specs=[_ctx_spec(D), _lat_spec(D)] + in_specs,
        out_specs=out_specs,
        out_shape=out_shape,
        compiler_params=pltpu.CompilerParams(dimension_semantics=("parallel",),
                                             vmem_limit_bytes=VMEM_LIMIT),
        name="pre_mix",
    )(xc, xl, *pre_params)


def _mix_kernel(*refs, has_cache, qb, seq_t, n_seq):
    if has_cache:
        (u_ref, vn_ref, zf_ref, q_ref, kv_ref, kr_ref, ws_ref, bs_ref, bdc_ref, bds_ref,
         ct_ref, st_ref, cckv_ref, ckr_ref, wukv_ref, o_ref) = refs
    else:
        (u_ref, vn_ref, zf_ref, q_ref, kv_ref, kr_ref, ws_ref, bs_ref, bdc_ref, bds_ref,
         ct_ref, st_ref, o_ref) = refs

    lane = lax.broadcasted_iota(jnp.int32, (CHUNK, A_WIDTH), 1)
    for c in range(n_seq * qb // CHUNK):
        rows = slice(c * CHUNK, (c + 1) * CHUNK)
        vch = vn_ref[rows, :]
        s = bs_ref[...]
        for g in range(A_HEADS):
            sg = jnp.dot(ws_ref[g], vch, preferred_element_type=F32)
            in_head = (lane >= g * A_HEAD_DIM) & (lane < (g + 1) * A_HEAD_DIM)
            s = s + jnp.where(in_head, sg, 0.0)
        o_ref[rows, 0:A_WIDTH] = (u_ref[rows, :] * s).astype(BF16)

    if has_cache:
        kvc = jnp.dot(cckv_ref[...].astype(BF16), wukv_ref[...],
                      preferred_element_type=F32).astype(BF16)
        krc = ckr_ref[...]

    for sq in range(n_seq):
        qrows = slice(sq * qb, (sq + 1) * qb)
        srows = slice(sq * seq_t, (sq + 1) * seq_t)

        zf = zf_ref[srows, :]
        zc = jnp.dot(zf, bdc_ref[...], preferred_element_type=F32).astype(BF16)
        zs = jnp.dot(zf, bds_ref[...], preferred_element_type=F32).astype(BF16)
        yf = (jnp.dot(ct_ref[...], zc, preferred_element_type=F32)
              - jnp.dot(st_ref[...], zs, preferred_element_type=F32))
        o_ref[qrows, A_WIDTH:A_WIDTH + F_WIDTH] = yf.astype(BF16)

        kr = kr_ref[srows, :]
        for hd in range(HEADS):
            qh = q_ref[qrows, hd * HEAD_PAD:(hd + 1) * HEAD_PAD]
            kb = hd * (NOPE + V_DIM)
            kh = jnp.concatenate([kv_ref[srows, kb:kb + NOPE], kr], axis=1)
            vh = kv_ref[srows, kb + NOPE:kb + NOPE + V_DIM]
            s = _dot_nt(qh, kh)
            m = jnp.max(s, axis=-1, keepdims=True)
            if has_cache:
                khc = jnp.concatenate([kvc[:, kb:kb + NOPE], krc], axis=1)
                vhc = kvc[:, kb + NOPE:kb + NOPE + V_DIM]
                sc = _dot_nt(qh, khc)
                m = jnp.maximum(m, jnp.max(sc, axis=-1, keepdims=True))
            e = jnp.exp(s - m)
            den = jnp.sum(e, axis=-1, keepdims=True)
            o = jnp.dot(e.astype(BF16), vh, preferred_element_type=F32)
            if has_cache:
                ec = jnp.exp(sc - m)
                den = den + jnp.sum(ec, axis=-1, keepdims=True)
                o = o + jnp.dot(ec.astype(BF16), vhc, preferred_element_type=F32)
            ob = A_WIDTH + F_WIDTH + hd * V_DIM
            o_ref[qrows, ob:ob + V_DIM] = (o * (1.0 / den)).astype(BF16)


def _mix_call(layer, pre, consts, seq_t, n_batch, tok_off, cache=None):
    u, vn, zf, q, kv, kr = pre
    ws, bs_full, bdc, bds, ct, st = consts
    lay = functools.partial(_layer_spec, layer)
    qb = min(seq_t, QB_MAX)
    nq = seq_t // qb
    n_seq = max(1, MIX_ROWS // seq_t) if (nq == 1 and cache is None) else 1
    q_rows, s_rows = n_seq * qb, n_seq * seq_t
    qoff = tok_off // q_rows
    soff = tok_off // s_rows

    def qrow(width):
        return pl.BlockSpec((q_rows, width), lambda b, j: (qoff + b * nq + j, 0))

    def srow(width):
        return pl.BlockSpec((s_rows, width), lambda b, j: (soff + b, 0))

    def full(shape):
        return pl.BlockSpec(shape, lambda b, j: (0,) * len(shape))

    in_specs = [qrow(A_WIDTH), qrow(A_WIDTH), srow(F_WIDTH), qrow(HEADS * HEAD_PAD),
                srow(HEADS * (NOPE + V_DIM)), srow(LANES),
                lay((A_HEADS, CHUNK, CHUNK)), lay((CHUNK, A_WIDTH)),
                full((F_WIDTH, F_WIDTH)), full((F_WIDTH, F_WIDTH)),
                pl.BlockSpec((qb, seq_t), lambda b, j: (j, 0)),
                pl.BlockSpec((qb, seq_t), lambda b, j: (j, 0))]
    args = [u, vn, zf, q, kv, kr, ws, bs_full, bdc, bds, ct, st]
    if cache is not None:
        assert n_seq == 1
        cckv, ckr, wukv = cache
        in_specs += [pl.BlockSpec((None, None, PAST, KV_LORA), lambda b, j: (b, layer, 0, 0)),
                     pl.BlockSpec((None, None, PAST, LANES), lambda b, j: (b, layer, 0, 0)),
                     lay((KV_LORA, HEADS * (NOPE + V_DIM)))]
        args += [cckv, ckr, wukv]
    return pl.pallas_call(
        functools.partial(_mix_kernel, has_cache=cache is not None, qb=qb, seq_t=seq_t, n_seq=n_seq),
        grid=(n_batch // n_seq, nq),
        in_specs=in_specs,
        out_specs=pl.BlockSpec((q_rows, D), lambda b, j: (b * nq + j, 0)),
        out_shape=jax.ShapeDtypeStruct((n_batch * seq_t, D), BF16),
        compiler_params=pltpu.CompilerParams(dimension_semantics=("parallel", "parallel"),
                                             vmem_limit_bytes=VMEM_LIMIT),
        name="mix_lat" if cache is not None else "mix_ctx",
    )(*args)


def _post_kernel(yc_ref, yl_ref, xc_ref, xl_ref, mod_ref, wout_ref, gpost_ref, gffn_ref, wr_ref, br_ref,
                 tri_ref, x1_ref, h2_ref, tab_ref, rw_ref, cnt_ref, carry_ref):
    i = pl.program_id(0)

    @pl.when(i == 0)
    def _():
        carry_ref[...] = jnp.zeros(carry_ref.shape, F32)

    y = jnp.dot(_pick(i, yc_ref, yl_ref), wout_ref[...].astype(BF16), preferred_element_type=F32)
    x1 = _pick(i, xc_ref, xl_ref) + _rms(y, gpost_ref[...] * mod_ref[2:3, :])
    x1_ref[...] = x1
    h2 = _rms(x1, gffn_ref[...] * (1.0 + mod_ref[4:5, :])) + mod_ref[3:4, :]
    _store_tiled(h2_ref, h2)

    wr = wr_ref[...]
    wr_hi = wr.astype(BF16)
    wr_lo = (wr - wr_hi.astype(F32)).astype(BF16)
    h_hi = h2.astype(BF16)
    h_lo = (h2 - h_hi.astype(F32)).astype(BF16)
    hi_terms = jnp.dot(h_hi, jnp.concatenate([wr_hi, wr_lo], axis=1), preferred_element_type=F32)
    logits = (hi_terms[:, :LANES] + jnp.dot(h_lo, wr_hi, preferred_element_type=F32)
              + hi_terms[:, LANES:]) + br_ref[...]

    lane = lax.broadcasted_iota(jnp.int32, (TB, LANES), 1)
    lane_f = lane.astype(F32)
    work = logits
    idx, val = [], []
    for _ in range(TOP_K):
        m = jnp.max(work, axis=-1, keepdims=True)
        ik = jnp.min(jnp.where(work == m, lane_f, float(LANES)), axis=-1, keepdims=True)
        idx.append(ik)
        val.append(m)
        work = jnp.where(lane_f == ik, NEG, work)
    ex = [jnp.exp(v - val[0]) for v in val]
    den = ex[0] + ex[1] + ex[2] + ex[3]

    onehot = jnp.zeros((TB, LANES), F32)
    for k in range(TOP_K):
        onehot = onehot + jnp.where(lane_f == idx[k], 1.0, 0.0)
    before = jnp.dot(tri_ref[...], onehot.astype(BF16), preferred_element_type=F32) + carry_ref[0:1, :]
    ri = jnp.zeros((TB, LANES), F32)
    rw = jnp.zeros((TB, LANES), F32)
    for k in range(TOP_K):
        rank_k = jnp.sum(jnp.where(lane_f == idx[k], before, 0.0), axis=-1, keepdims=True)
        ri = ri + jnp.where(lane == k, idx[k], 0.0) + jnp.where(lane == TOP_K + k, rank_k, 0.0)
        rw = rw + jnp.where(lane == k, ex[k] / den, 0.0)
    tab_ref[...] = ri.T[0:2 * TOP_K, :].astype(jnp.int32)
    rw_ref[...] = rw
    total = carry_ref[0:1, :] + jnp.sum(onehot, axis=0, keepdims=True)
    carry_ref[...] = jnp.broadcast_to(total, carry_ref.shape)
    cnt_ref[...] = jnp.broadcast_to(total, cnt_ref.shape).astype(jnp.int32)


def _post_call(layer, yc, yl, xc, xl, mods, w_out, g_post, g_ffn, w_r, b_r, tri):
    nb = N_TOK // TB
    lay = functools.partial(_layer_spec, layer)

    def tok(width):
        return pl.BlockSpec((TB, width), lambda i: (i, 0))

    def full(shape):
        return pl.BlockSpec(shape, lambda i: (0,) * len(shape))

    return pl.pallas_call(
        _post_kernel,
        grid=(nb,),
        in_specs=[_ctx_spec(D), _lat_spec(D), _ctx_spec(D), _lat_spec(D), _mod_spec(layer),
                  lay((D, D)), lay((1, D)), lay((1, D)), lay((D, LANES)), lay((1, LANES)),
                  full((TB, TB))],
        out_specs=(tok(D), pl.BlockSpec((TB * SUB, LANES), lambda i: (i, 0)),
                   pl.BlockSpec((None, 2 * TOP_K, TB), lambda i: (i, 0, 0)),
                   tok(LANES), full((8, LANES))),
        out_shape=(jax.ShapeDtypeStruct((N_TOK, D), F32),
                   jax.ShapeDtypeStruct((N_TOK * SUB, LANES), F32),
                   jax.ShapeDtypeStruct((nb, 2 * TOP_K, TB), jnp.int32),
                   jax.ShapeDtypeStruct((N_TOK, LANES), F32),
                   jax.ShapeDtypeStruct((8, LANES), jnp.int32)),
        scratch_shapes=[pltpu.VMEM((8, LANES), F32)],
        compiler_params=pltpu.CompilerParams(dimension_semantics=("arbitrary",),
                                             vmem_limit_bytes=VMEM_LIMIT),
        name="post_mix_router",
    )(yc, yl, xc, xl, mods, w_out, g_post, g_ffn, w_r, b_r, tri)


def _plan_kernel(cnt_ref, tab_ref, dest_ref, cend_ref, pend_ref, nu_ref, be_ref, first_ref, half_ref, nxt_ref,
                 nxt2_ref, slot_ref, start_s, next_s):
    nxt = jnp.int32(-1)
    for e in reversed(range(N_EXPERTS)):
        next_s[e] = nxt
        nxt = jnp.where(cnt_ref[0, e] > 0, jnp.int32(e), nxt)

    start = jnp.int32(0)
    run = jnp.int32(0)
    last = jnp.int32(0)
    for e in range(N_EXPERTS):
        c = cnt_ref[0, e]
        n_blk = lax.shift_right_logical(c + (RB - 1), RB.bit_length() - 1)
        start_s[e] = start
        cend_ref[e] = start * RB + c
        pend_ref[e] = (start + n_blk) * RB
        few = (c - (n_blk - 1) * RB) <= RB // 2
        n1 = next_s[e]
        n2 = jnp.where(n1 >= 0, next_s[jnp.maximum(n1, 0)], jnp.int32(-1))
        slot = lax.rem(run, W_SLOTS)

        def fill(j, carry, e=e, n1=n1, n2=n2, slot=slot, start=start, n_blk=n_blk, few=few):
            be_ref[j] = jnp.int32(e)
            first_ref[j] = (j == start).astype(jnp.int32)
            half_ref[j] = jnp.logical_and(j == start + n_blk - 1, few).astype(jnp.int32)
            nxt_ref[j] = n1
            nxt2_ref[j] = n2
            slot_ref[j] = slot
            return carry

        lax.fori_loop(start, start + n_blk, fill, 0)
        owns = n_blk > 0
        last = jnp.where(owns, jnp.int32(e), last)
        run = run + owns.astype(jnp.int32)
        start = start + n_blk
    nu_ref[0] = start

    def tail(j, carry):
        be_ref[j] = last
        first_ref[j] = jnp.int32(0)
        half_ref[j] = jnp.int32(0)
        nxt_ref[j] = jnp.int32(-1)
        nxt2_ref[j] = jnp.int32(-1)
        slot_ref[j] = jnp.int32(0)
        return carry

    lax.fori_loop(start, N_BLOCKS, tail, 0)

    idx = tab_ref[:, 0:TOP_K, :]
    base = jnp.zeros(idx.shape, jnp.int32)
    for e in range(N_EXPERTS):
        base = jnp.where(idx == e, start_s[e] * RB, base)
    dest_ref[...] = base + tab_ref[:, TOP_K:2 * TOP_K, :]


def _plan_call(cnt, tab):
    nb = N_TOK // TB
    smem = pl.BlockSpec(memory_space=pltpu.SMEM)
    blocks = jax.ShapeDtypeStruct((N_BLOCKS,), jnp.int32)
    return pl.pallas_call(
        _plan_kernel,
        in_specs=[smem, pl.BlockSpec(memory_space=pltpu.VMEM)],
        out_specs=(pl.BlockSpec(memory_space=pltpu.VMEM),) + (smem,) * 9,
        out_shape=(jax.ShapeDtypeStruct((nb, TOP_K, TB), jnp.int32),
                   jax.ShapeDtypeStruct((N_EXPERTS,), jnp.int32),
                   jax.ShapeDtypeStruct((N_EXPERTS,), jnp.int32),
                   jax.ShapeDtypeStruct((1,), jnp.int32),
                   blocks, blocks, blocks, blocks, blocks, blocks),
        scratch_shapes=[pltpu.SMEM((N_EXPERTS,), jnp.int32), pltpu.SMEM((N_EXPERTS,), jnp.int32)],
        name="plan_rows",
    )(cnt, tab)


def _dispatch_kernel(cend_ref, pend_ref, nu_ref, dest_ref, h2_ref, xs_ref, zero_ref, sem, zsem):
    i = pl.program_id(0)

    def zero_padding(act):
        for e in range(N_EXPERTS):
            c_end = cend_ref[e]
            aligned = lax.shift_left(lax.shift_right_logical(c_end + 7, 3), 3)
            for r in range(7):

                @pl.when(c_end + r < aligned)
                def _():
                    act(pltpu.make_async_copy(_tile_rows(zero_ref, 0), _tile_rows(xs_ref, c_end + r), zsem))

            n = pend_ref[e] - aligned
            off = aligned
            for size in (128, 64, 32, 16, 8):

                @pl.when((n & size) != 0)
                def _():
                    act(pltpu.make_async_copy(_tile_rows(zero_ref, 0, size), _tile_rows(xs_ref, off, size),
                                              zsem))

                off = off + (n & size)

        def tail(b, carry):
            act(pltpu.make_async_copy(zero_ref, _tile_rows(xs_ref, b * RB, RB), zsem))
            return carry

        lax.fori_loop(nu_ref[0], N_BLOCKS, tail, 0)

    @pl.when(i == 0)
    def _():
        zero_ref[...] = jnp.zeros(zero_ref.shape, F32)
        zero_padding(lambda cp: cp.start())

    def body(t, carry):
        for k in range(TOP_K):
            pltpu.make_async_copy(_tile_rows(h2_ref, t), _tile_rows(xs_ref, dest_ref[0, k * TB + t]),
                                  sem).start(priority=k % 2)
        return carry

    lax.fori_loop(0, TB, body, 0, unroll=8)
    for k in range(TOP_K):
        pltpu.make_async_copy(h2_ref, _tile_rows(xs_ref, 0, TB), sem).wait()

    @pl.when(i == pl.num_programs(0) - 1)
    def _():
        zero_padding(lambda cp: cp.wait())


def _dispatch_call(cend, pend, n_used, dest, h2):
    nb = N_TOK // TB
    grid_spec = pltpu.PrefetchScalarGridSpec(
        num_scalar_prefetch=3,
        grid=(nb,),
        in_specs=[
            pl.BlockSpec((None, 1, TOP_K * TB), lambda i, c, p, n: (i, 0, 0), memory_space=pltpu.SMEM),
            pl.BlockSpec((TB * SUB, LANES), lambda i, c, p, n: (i, 0)),
        ],
        out_specs=pl.BlockSpec(memory_space=pl.ANY),
        scratch_shapes=[pltpu.VMEM((RB * SUB, LANES), F32), pltpu.SemaphoreType.DMA,
                        pltpu.SemaphoreType.DMA],
    )
    return pl.pallas_call(
        _dispatch_kernel,
        grid_spec=grid_spec,
        out_shape=jax.ShapeDtypeStruct((N_ROWS * SUB, LANES), F32),
        compiler_params=pltpu.CompilerParams(dimension_semantics=("arbitrary",),
                                             vmem_limit_bytes=VMEM_LIMIT),
        name="dispatch_rows",
    )(cend, pend, n_used, dest, h2)


def _moe_kernel(be_ref, first_ref, half_ref, nxt_ref, nxt2_ref, slot_ref, nu_ref, xs_ref, wgu_hbm, bgu_ref,
                wdn_hbm, bdn_ref, o_ref, wgu_f, wdn_f, sems, *, layer):
    b = pl.program_id(0)
    used = b < nu_ref[0]

    def fetch(e, s):
        return (pltpu.make_async_copy(wgu_hbm.at[layer, e], wgu_f.at[s], sems.at[0, s]),
                pltpu.make_async_copy(wdn_hbm.at[layer, e], wdn_f.at[s], sems.at[1, s]))

    @pl.when(b == 0)
    def _():
        for cp in fetch(be_ref[0], 0):
            cp.start(priority=1)

        @pl.when(nxt_ref[0] >= 0)
        def _():
            for cp in fetch(nxt_ref[0], 1):
                cp.start(priority=1)

    @pl.when(first_ref[b] == 1)
    def _():
        s = slot_ref[b]
        for cp in fetch(be_ref[b], s):
            cp.wait()

        @pl.when(nxt2_ref[b] >= 0)
        def _():
            for cp in fetch(nxt2_ref[b], (s + 2) % W_SLOTS):
                cp.start(priority=1)

    def experts(rows):
        s = slot_ref[b]
        x = _load_tiled(xs_ref, rows).astype(BF16)
        gu = jnp.dot(x, wgu_f[s].astype(BF16), preferred_element_type=F32) + bgu_ref[...]
        g = jnp.minimum(gu[:, :D_FF], SWIGLU_LIMIT)
        l = jnp.clip(gu[:, D_FF:], -SWIGLU_LIMIT, SWIGLU_LIMIT)
        a = g * jax.nn.sigmoid(SWIGLU_ALPHA * g) * (l + 1.0)
        y = jnp.dot(a.astype(BF16), wdn_f[s].astype(BF16), preferred_element_type=F32) + bdn_ref[...]
        _store_tiled(o_ref, y)
        if rows < RB:
            o_ref[rows * SUB:, :] = jnp.zeros(((RB - rows) * SUB, LANES), o_ref.dtype)

    half = half_ref[b] == 1

    @pl.when(jnp.logical_and(used, jnp.logical_not(half)))
    def _():
        experts(RB)

    @pl.when(jnp.logical_and(used, half))
    def _():
        experts(RB // 2)

    @pl.when(jnp.logical_not(used))
    def _():
        o_ref[...] = jnp.zeros(o_ref.shape, o_ref.dtype)


def _moe_call(layer, tables, xs, w_gu, b_gu, w_dn, b_dn):
    def rows_in(b, be, fi, ha, nx, n2, sl, nu):
        return (jnp.minimum(b, nu[0] - 1), 0)

    def rows_out(b, be, fi, ha, nx, n2, sl, nu):
        return (b, 0)

    def expert(b, be, fi, ha, nx, n2, sl, nu):
        return (layer, be[b], 0, 0)

    grid_spec = pltpu.PrefetchScalarGridSpec(
        num_scalar_prefetch=7,
        grid=(N_BLOCKS,),
        in_specs=[
            pl.BlockSpec((RB * SUB, LANES), rows_in),
            pl.BlockSpec(memory_space=pl.ANY),
            pl.BlockSpec((None, None, 1, 2 * D_FF), expert),
            pl.BlockSpec(memory_space=pl.ANY),
            pl.BlockSpec((None, None, 1, D), expert),
        ],
        out_specs=pl.BlockSpec((RB * SUB, LANES), rows_out),
        scratch_shapes=[pltpu.VMEM((W_SLOTS, D, 2 * D_FF), F32), pltpu.VMEM((W_SLOTS, D_FF, D), F32),
                        pltpu.SemaphoreType.DMA((2, W_SLOTS))],
    )
    return pl.pallas_call(
        functools.partial(_moe_kernel, layer=layer),
        grid_spec=grid_spec,
        out_shape=jax.ShapeDtypeStruct((N_ROWS * SUB, LANES), F32),
        compiler_params=pltpu.CompilerParams(dimension_semantics=("arbitrary",),
                                             vmem_limit_bytes=VMEM_LIMIT),
        name="moe_experts",
    )(*tables, xs, w_gu, b_gu, w_dn, b_dn)


def _final_kernel(*refs):
    _combine_body(pl.program_id(0), *refs)


def _combine_body(i, dcur_ref, dnxt_ref, x1_ref, ys_ref, rw_ref, mod_ref, g_ref, oc_ref, ol_ref, buf, sems):
    nb = pl.num_programs(0)
    slot = i % 2

    def gather(dest_ref, s):
        def body(t, carry):
            for k in range(TOP_K):
                pltpu.make_async_copy(_tile_rows(ys_ref, dest_ref[0, k * TB + t]),
                                      _tile_rows(buf.at[s], k * TB + t), sems.at[s]).start(priority=k % 2)
            return carry

        lax.fori_loop(0, TB, body, 0, unroll=8)

    @pl.when(i == 0)
    def _():
        gather(dcur_ref, 0)

    @pl.when(i + 1 < nb)
    def _():
        gather(dnxt_ref, 1 - slot)

    pltpu.make_async_copy(_tile_rows(ys_ref, 0, TOP_K * TB), buf.at[slot], sems.at[slot]).wait()

    rw = rw_ref[...]
    y = jnp.zeros((TB, D), F32)
    for k in range(TOP_K):
        yk = jnp.concatenate([buf[slot, pl.ds(k * TB * SUB + s, TB, stride=SUB), :] for s in range(SUB)],
                             axis=1)
        y = y + yk * rw[:, k:k + 1]
    x2 = x1_ref[...] + _rms(y, g_ref[...] * mod_ref[5:6, :])

    @pl.when(i < N_CTX // TB)
    def _():
        oc_ref[...] = x2

    @pl.when(i >= N_CTX // TB)
    def _():
        ol_ref[...] = x2


def _final_call(layer, dest_flat, x1, ys, rw, mods, g_post_ffn):
    nb = N_TOK // TB
    return pl.pallas_call(
        _final_kernel,
        grid=(nb,),
        in_specs=[pl.BlockSpec((None, 1, TOP_K * TB), lambda i: (i, 0, 0), memory_space=pltpu.SMEM),
                  pl.BlockSpec((None, 1, TOP_K * TB), lambda i: (jnp.minimum(i + 1, nb - 1), 0, 0),
                               memory_space=pltpu.SMEM),
                  pl.BlockSpec((TB, D), lambda i: (i, 0)),
                  pl.BlockSpec(memory_space=pl.ANY),
                  pl.BlockSpec((TB, LANES), lambda i: (i, 0)),
                  _mod_spec(layer), _layer_spec(layer, (1, D))],
        out_specs=(_ctx_spec(D), _lat_spec(D)),
        out_shape=(jax.ShapeDtypeStruct((N_CTX, D), F32), jax.ShapeDtypeStruct((N_LAT, D), F32)),
        scratch_shapes=[pltpu.VMEM((2, TOP_K * TB * SUB, LANES), F32), pltpu.SemaphoreType.DMA((2,))],
        compiler_params=pltpu.CompilerParams(dimension_semantics=("arbitrary",),
                                             vmem_limit_bytes=VMEM_LIMIT),
        name="combine_residual",
    )(dest_flat, dest_flat, x1, ys, rw, mods, g_post_ffn)


def _dft_tables(t):
    j = np.arange(t, dtype=np.int64)
    ang = 2.0 * np.pi * ((j[:, None] * j[None, :]) % t) / t
    return (np.cos(ang) / math.sqrt(t)).astype(np.float32), (np.sin(ang) / math.sqrt(t)).astype(np.float32)


def _channel_dft():
    c = np.arange(F_GROUP_DIM, dtype=np.int64)
    ang = 2.0 * np.pi * ((c[:, None] * c[None, :]) % F_GROUP_DIM) / F_GROUP_DIM
    eye = np.eye(F_GROUPS)
    bdc = np.kron(eye, np.cos(ang)) / math.sqrt(F_GROUP_DIM)
    bds = np.kron(eye, np.sin(ang)) / math.sqrt(F_GROUP_DIM)
    return bdc.astype(np.float32), bds.astype(np.float32)


def _rope_tables():
    pos = np.arange(LAT_T)
    n = ROPE // 4
    inv_freq = np.power(np.float32(ROPE_BASE), -np.arange(n, dtype=np.float32) / np.float32(n))
    ang_r = (pos // GRID_W).astype(np.float32)[:, None] * inv_freq
    ang_c = (pos % GRID_W).astype(np.float32)[:, None] * inv_freq
    cos = np.concatenate([np.cos(ang_r), np.cos(ang_r), np.cos(ang_c), np.cos(ang_c),
                          np.ones((LAT_T, LANES - ROPE))], axis=1)
    sin = np.concatenate([-np.sin(ang_r), np.sin(ang_r), -np.sin(ang_c), np.sin(ang_c),
                          np.zeros((LAT_T, LANES - ROPE))], axis=1)
    return cos.astype(np.float32), sin.astype(np.float32)


def kernel(x_prompt, x_sample, cache_ckv, cache_krope, c, c_ctx, w_ada, b_ada, g_pre_mix, g_post_mix, g_pre_ffn, g_post_ffn, w_in, g_sgu, w_spatial, b_spatial, g_q, w_uq, g_kv, w_ukv, w_out, w_router, b_router, w_gate_up, b_gate_up, w_down, b_down):
    xc, xl = x_prompt.reshape(N_CTX, D), x_sample.reshape(N_LAT, D)

    cond = jnp.concatenate([c_ctx[None, :], c, jnp.zeros((5, D), F32)], axis=0)
    mods = _mod_call(cond.T, w_ada, b_ada.reshape(DEPTH, 1, 6 * D))
    mods = mods[:, :3].reshape(DEPTH, 3, 6, D)

    gmat = jnp.asarray(np.kron(np.eye(A_HEADS), np.full((A_HEAD_DIM, A_HEAD_DIM), 1.0 / A_HEAD_DIM)),
                       dtype=BF16)
    bdc_np, bds_np = _channel_dft()
    bdc, bds = jnp.asarray(bdc_np).astype(BF16), jnp.asarray(bds_np).astype(BF16)
    dft = {}
    for t in (CTX_T, LAT_T):
        ct_np, st_np = _dft_tables(t)
        dft[t] = (jnp.asarray(ct_np).astype(BF16), jnp.asarray(st_np).astype(BF16))
    cos_np, sin_np = _rope_tables()
    cos_t, sin_t = jnp.asarray(cos_np), jnp.asarray(sin_np)
    tri = jnp.asarray(np.tril(np.ones((TB, TB), np.float32), k=-1)).astype(BF16)

    b_gu = b_gate_up.reshape(DEPTH, N_EXPERTS, 1, 2 * D_FF)
    b_dn = b_down.reshape(DEPTH, N_EXPERTS, 1, D)
    w_uq_p = jnp.pad(w_uq.reshape(DEPTH, Q_LORA, HEADS, NOPE + ROPE),
                     ((0, 0), (0, 0), (0, 0), (0, HEAD_PAD - NOPE - ROPE)))
    w_uq_p = w_uq_p.reshape(DEPTH, Q_LORA, HEADS * HEAD_PAD).astype(BF16)
    w_ukv_b = w_ukv.astype(BF16)
    ws = w_spatial.astype(BF16)
    bs_full = jnp.repeat(jnp.swapaxes(b_spatial, 1, 2), A_HEAD_DIM, axis=2)
    ckr_p = jnp.pad(cache_krope, ((0, 0), (0, 0), (0, 0), (0, LANES - ROPE))).astype(BF16)
    w_r = jnp.pad(w_router, ((0, 0), (0, 0), (0, LANES - N_EXPERTS)))
    b_r = jnp.pad(b_router, ((0, 0), (0, LANES - N_EXPERTS)), constant_values=NEG)[:, None, :]
    row = lambda g: g[:, None, :]

    w_in_t = jnp.swapaxes(w_in, 1, 2)
    pre_params = (mods, row(g_pre_mix), w_in_t, row(g_sgu), gmat, row(g_q), w_uq_p, row(g_kv), w_ukv_b,
                  cos_t, sin_t)

    ckv_layers, krope_layers = [], []
    for i in range(DEPTH):
        u, vn, zf, q, kv, kr, ckv, zkr = _pre_call(i, xc, xl, pre_params)
        ckv_layers.append(ckv[:N_CTX].reshape(N_CTX_B, CTX_T, KV_LORA))
        krope_layers.append(zkr[:N_CTX, :ROPE].reshape(N_CTX_B, CTX_T, ROPE))

        mix_in = (u, vn, zf, q, kv, kr)
        y_ctx = _mix_call(i, mix_in, (ws, bs_full, bdc, bds) + dft[CTX_T], CTX_T, N_CTX_B, 0)
        y_lat = _mix_call(i, mix_in, (ws, bs_full, bdc, bds) + dft[LAT_T], LAT_T, N_LAT_B, N_CTX,
                          cache=(cache_ckv, ckr_p, w_ukv_b))

        x1, h2, tab, rw, cnt = _post_call(i, y_ctx, y_lat, xc, xl, mods, w_out,
                                          row(g_post_mix), row(g_pre_ffn), w_r, b_r, tri)

        dest, cend, pend, n_used, block_e, first, half, nxt, nxt2, slot = _plan_call(cnt, tab)
        dest_flat = dest.reshape(N_TOK // TB, 1, TOP_K * TB)
        xs = _dispatch_call(cend, pend, n_used, dest_flat, h2)
        ys = _moe_call(i, (block_e, first, half, nxt, nxt2, slot, n_used), xs, w_gate_up, b_gu, w_down, b_dn)
        xc, xl = _final_call(i, dest_flat, x1, ys, rw, mods, row(g_post_ffn))

    y_prompt = xc.reshape(N_CTX_B, CTX_T, D)
    y_sample = xl.reshape(N_LAT_B, LAT_T, D)
    return (y_prompt, y_sample, jnp.stack(ckv_layers, axis=1), jnp.stack(krope_layers, axis=1))
```

```python
import functools
import math

import jax
import jax.numpy as jnp
import numpy as np
from jax import lax
from jax.experimental import pallas as pl
from jax.experimental.pallas import tpu as pltpu

F32 = jnp.float32
BF16 = jnp.bfloat16

D = 1024
N_CTX_B, CTX_T = 16, 256
N_LAT_B, LAT_T = 2, 1024
PAST = 512
N_CTX = N_CTX_B * CTX_T
N_LAT = N_LAT_B * LAT_T
N_TOK = N_CTX + N_LAT
DEPTH = 2
GRID_W = 64
EPS = 1e-6
A_HEADS, A_HEAD_DIM, A_WIDTH, CHUNK = 4, 64, 256, 128
F_GROUPS, F_GROUP_DIM, F_WIDTH = 4, 64, 256
HEADS, Q_LORA, KV_LORA, NOPE, ROPE, V_DIM = 4, 256, 128, 128, 64, 128
HEAD_PAD = 256
IN_COLS = 1216
N_EXPERTS, TOP_K, D_FF = 32, 4, 1024
SWIGLU_LIMIT, SWIGLU_ALPHA = 7.0, 1.702
ROPE_BASE = 10000.0

TB = 512
QB_MAX = 1024
RB = 256
W_SLOTS = 3
LANES = 128
SUB = D // LANES
N_BLOCKS = N_TOK * TOP_K // RB + N_EXPERTS
N_ROWS = N_BLOCKS * RB
NEG = -3.0e38
VMEM_LIMIT = 56 * 1024 * 1024


def _rms(x, g):
    return x * lax.rsqrt(jnp.mean(x * x, axis=-1, keepdims=True) + EPS) * g


def _split_dot(v, m):
    hi = v.astype(BF16)
    lo = (v - hi.astype(F32)).astype(BF16)
    return (jnp.dot(hi, m, preferred_element_type=F32)
            + jnp.dot(lo, m, preferred_element_type=F32))


def _dot_nt(a, b):
    return lax.dot_general(a, b, (((1,), (1,)), ((), ())), preferred_element_type=F32)


def _store_tiled(ref, x):
    rows = x.shape[0]
    for s in range(SUB):
        ref[pl.ds(s, rows, stride=SUB), :] = x[:, s * LANES:(s + 1) * LANES]


def _load_tiled(ref, rows):
    return jnp.concatenate([ref[pl.ds(s, rows, stride=SUB), :] for s in range(SUB)], axis=1)


def _tile_rows(ref, row, n_rows=1):
    return ref.at[pl.ds(pl.multiple_of(row * SUB, SUB), n_rows * SUB)]


def _mod_kernel(ct_ref, w_ref, b_ref, o_ref):
    ct = ct_ref[...]
    s = ct * jax.nn.sigmoid(ct)
    w = w_ref[...]
    o_ref[...] = jnp.zeros(o_ref.shape, F32)
    for r in range(3):
        o_ref[r:r + 1, :] = jnp.sum(w * s[:, r:r + 1], axis=0, keepdims=True) + b_ref[...]


def _mod_call(cond_t, w_ada, b_ada):
    cb = 1536
    return pl.pallas_call(
        _mod_kernel,
        grid=(DEPTH, 6 * D // cb),
        in_specs=[
            pl.BlockSpec((D, 8), lambda l, j: (0, 0)),
            pl.BlockSpec((None, D, cb), lambda l, j: (l, 0, j)),
            pl.BlockSpec((None, 1, cb), lambda l, j: (l, 0, j)),
        ],
        out_specs=pl.BlockSpec((None, 8, cb), lambda l, j: (l, 0, j)),
        out_shape=jax.ShapeDtypeStruct((DEPTH, 8, 6 * D), F32),
        compiler_params=pltpu.CompilerParams(dimension_semantics=("parallel", "parallel")),
        name="modulation",
    )(cond_t, w_ada, b_ada)


def _mod_index(i):
    first_lat = N_CTX // TB
    return jnp.where(i < first_lat, 0, 1 + (i - first_lat) // (LAT_T // TB))


def _ctx_spec(width):
    return pl.BlockSpec((TB, width), lambda i: (jnp.minimum(i, N_CTX // TB - 1), 0))


def _lat_spec(width):
    return pl.BlockSpec((TB, width), lambda i: (jnp.maximum(i - N_CTX // TB, 0), 0))


def _pick(i, ctx_ref, lat_ref):
    return jnp.where(i >= N_CTX // TB, lat_ref[...], ctx_ref[...])


def _layer_spec(layer, shape):
    zeros = (0,) * len(shape)
    return pl.BlockSpec((None,) + tuple(shape), lambda *_: (layer,) + zeros)


def _mod_spec(layer):
    return pl.BlockSpec((None, None, 6, D), lambda i: (layer, _mod_index(i), 0, 0))


def _swap_halves(x, lane):
    w = x.shape[-1]
    fwd = pltpu.roll(x, w - 16, 1)
    bwd = pltpu.roll(x, 16, 1)
    return jnp.where((lane & 31) < 16, fwd, bwd)


def _pre_kernel(xc_ref, xl_ref, *refs):
    i = pl.program_id(0)
    _pre_body(i, _pick(i, xc_ref, xl_ref), *refs)


def _pre_body(i, x, mod_ref, gpre_ref, win_ref, gsgu_ref, gmat_ref, gq_ref, wuq_ref,
              gkv_ref, wukv_ref, cos_ref, sin_ref,
              u_ref, vn_ref, zf_ref, q_ref, kv_ref, kr_ref, ckv_ref, zkr_ref):
    is_lat = i >= N_CTX // TB
    h = _rms(x, gpre_ref[...] * (1.0 + mod_ref[1:2, :])) + mod_ref[0:1, :]
    z = _dot_nt(h.astype(BF16), win_ref[...].astype(BF16))

    ga = jax.nn.gelu(z[:, :2 * A_WIDTH])
    u_ref[...] = ga[:, :A_WIDTH]
    v = ga[:, A_WIDTH:]
    gmat = gmat_ref[...]
    dv = v - _split_dot(v, gmat)
    var = _split_dot(dv * dv, gmat)
    vn_ref[...] = (dv * lax.rsqrt(var + EPS) * gsgu_ref[...]).astype(BF16)

    zf_ref[...] = z[:, 512:768].astype(BF16)

    cos = jnp.where(is_lat, cos_ref[...], 1.0)
    sin = jnp.where(is_lat, sin_ref[...], 0.0)
    lane = lax.broadcasted_iota(jnp.int32, (TB, LANES), 1)

    qn = _rms(z[:, 768:1024], gq_ref[...])
    scale = (NOPE + ROPE) ** -0.5
    q = jnp.dot(qn.astype(BF16), wuq_ref[...], preferred_element_type=F32) * scale
    for hd in range(HEADS):
        base = hd * HEAD_PAD
        q_ref[:, base:base + NOPE] = q[:, base:base + NOPE].astype(BF16)
        qr = q[:, base + NOPE:base + HEAD_PAD]
        q_ref[:, base + NOPE:base + HEAD_PAD] = (qr * cos + _swap_halves(qr, lane) * sin).astype(BF16)

    ckv = _rms(z[:, 1024:1152], gkv_ref[...])
    ckv_ref[...] = ckv
    kv_ref[...] = jnp.dot(ckv.astype(BF16), wukv_ref[...], preferred_element_type=F32).astype(BF16)

    zkr = jnp.concatenate([z[:, 1152:IN_COLS], jnp.zeros((TB, LANES - ROPE), F32)], axis=1)
    zkr_ref[...] = zkr
    kr_ref[...] = (zkr * cos + _swap_halves(zkr, lane) * sin).astype(BF16)


def _pre_specs(layer):
    lay = functools.partial(_layer_spec, layer)
    first_lat = N_CTX // TB
    pos_blocks = LAT_T // TB

    def tok(width):
        return pl.BlockSpec((TB, width), lambda i: (i, 0))

    def full(shape):
        return pl.BlockSpec(shape, lambda i: (0,) * len(shape))

    def rope_map(i):
        return (jnp.where(i >= first_lat, (i - first_lat) % pos_blocks, 0), 0)

    in_specs = [
        _mod_spec(layer),
        lay((1, D)), lay((IN_COLS, D)), lay((1, A_WIDTH)), full((A_WIDTH, A_WIDTH)),
        lay((1, Q_LORA)), lay((Q_LORA, HEADS * HEAD_PAD)),
        lay((1, KV_LORA)), lay((KV_LORA, HEADS * (NOPE + V_DIM))),
        pl.BlockSpec((TB, LANES), rope_map), pl.BlockSpec((TB, LANES), rope_map),
    ]
    out_specs = (tok(A_WIDTH), tok(A_WIDTH), tok(F_WIDTH), tok(HEADS * HEAD_PAD),
                 tok(HEADS * (NOPE + V_DIM)), tok(LANES), tok(KV_LORA), tok(LANES))
    out_shape = (
        jax.ShapeDtypeStruct((N_TOK, A_WIDTH), F32),
        jax.ShapeDtypeStruct((N_TOK, A_WIDTH), BF16),
        jax.ShapeDtypeStruct((N_TOK, F_WIDTH), BF16),
        jax.ShapeDtypeStruct((N_TOK, HEADS * HEAD_PAD), BF16),
        jax.ShapeDtypeStruct((N_TOK, HEADS * (NOPE + V_DIM)), BF16),
        jax.ShapeDtypeStruct((N_TOK, LANES), BF16),
        jax.ShapeDtypeStruct((N_TOK, KV_LORA), F32),
        jax.ShapeDtypeStruct((N_TOK, LANES), F32),
    )
    return in_specs, out_specs, out_shape


def _pre_call(layer, xc, xl, pre_params):
    in_specs, out_specs, out_shape = _pre_specs(layer)
    return pl.pallas_call(
        _pre_kernel,
        grid=(N_TOK // TB,),
        in_specs=[_ctx_spec(D), _lat_spec(D)] + in_specs,
        out_specs=out_specs,
        out_shape=out_shape,
        compiler_params=pltpu.CompilerParams(dimension_semantics=("parallel",),
                                             vmem_limit_bytes=VMEM_LIMIT),
        name="pre_mix",
    )(xc, xl, *pre_params)


def _mix_kernel(*refs, has_cache, qb, seq_t, n_seq):
    if has_cache:
        (u_ref, vn_ref, zf_ref, q_ref, kv_ref, kr_ref, ws_ref, bs_ref, bdc_ref, bds_ref,
         ct_ref, st_ref, cckv_ref, ckr_ref, wukv_ref, o_ref) = refs
    else:
        (u_ref, vn_ref, zf_ref, q_ref, kv_ref, kr_ref, ws_ref, bs_ref, bdc_ref, bds_ref,
         ct_ref, st_ref, o_ref) = refs

    lane = lax.broadcasted_iota(jnp.int32, (CHUNK, A_WIDTH), 1)
    for c in range(n_seq * qb // CHUNK):
        rows = slice(c * CHUNK, (c + 1) * CHUNK)
        vch = vn_ref[rows, :]
        s = bs_ref[...]
        for g in range(A_HEADS):
            sg = jnp.dot(ws_ref[g], vch, preferred_element_type=F32)
            in_head = (lane >= g * A_HEAD_DIM) & (lane < (g + 1) * A_HEAD_DIM)
            s = s + jnp.where(in_head, sg, 0.0)
        o_ref[rows, 0:A_WIDTH] = (u_ref[rows, :] * s).astype(BF16)

    if has_cache:
        kvc = jnp.dot(cckv_ref[...].astype(BF16), wukv_ref[...],
                      preferred_element_type=F32).astype(BF16)
        krc = ckr_ref[...]

    for sq in range(n_seq):
        qrows = slice(sq * qb, (sq + 1) * qb)
        srows = slice(sq * seq_t, (sq + 1) * seq_t)

        zf = zf_ref[srows, :]
        zc = jnp.dot(zf, bdc_ref[...], preferred_element_type=F32).astype(BF16)
        zs = jnp.dot(zf, bds_ref[...], preferred_element_type=F32).astype(BF16)
        yf = (jnp.dot(ct_ref[...], zc, preferred_element_type=F32)
              - jnp.dot(st_ref[...], zs, preferred_element_type=F32))
        o_ref[qrows, A_WIDTH:A_WIDTH + F_WIDTH] = yf.astype(BF16)

        kr = kr_ref[srows, :]
        for hd in range(HEADS):
            qh = q_ref[qrows, hd * HEAD_PAD:(hd + 1) * HEAD_PAD]
            kb = hd * (NOPE + V_DIM)
            kh = jnp.concatenate([kv_ref[srows, kb:kb + NOPE], kr], axis=1)
            vh = kv_ref[srows, kb + NOPE:kb + NOPE + V_DIM]
            s = _dot_nt(qh, kh)
            m = jnp.max(s, axis=-1, keepdims=True)
            if has_cache:
                khc = jnp.concatenate([kvc[:, kb:kb + NOPE], krc], axis=1)
                vhc = kvc[:, kb + NOPE:kb + NOPE + V_DIM]
                sc = _dot_nt(qh, khc)
                m = jnp.maximum(m, jnp.max(sc, axis=-1, keepdims=True))
            e = jnp.exp(s - m)
            den = jnp.sum(e, axis=-1, keepdims=True)
            o = jnp.dot(e.astype(BF16), vh, preferred_element_type=F32)
            if has_cache:
                ec = jnp.exp(sc - m)
                den = den + jnp.sum(ec, axis=-1, keepdims=True)
                o = o + jnp.dot(ec.astype(BF16), vhc, preferred_element_type=F32)
            ob = A_WIDTH + F_WIDTH + hd * V_DIM
            o_ref[qrows, ob:ob + V_DIM] = (o * (1.0 / den)).astype(BF16)


def _mix_call(layer, pre, consts, seq_t, n_batch, tok_off, cache=None):
    u, vn, zf, q, kv, kr = pre
    ws, bs_full, bdc, bds, ct, st = consts
    lay = functools.partial(_layer_spec, layer)
    qb = min(seq_t, QB_MAX)
    nq = seq_t // qb
    n_seq = QB_MAX // seq_t if nq == 1 else 1
    q_rows, s_rows = n_seq * qb, n_seq * seq_t
    qoff = tok_off // q_rows
    soff = tok_off // s_rows

    def qrow(width):
        return pl.BlockSpec((q_rows, width), lambda b, j: (qoff + b * nq + j, 0))

    def srow(width):
        return pl.BlockSpec((s_rows, width), lambda b, j: (soff + b, 0))

    def full(shape):
        return pl.BlockSpec(shape, lambda b, j: (0,) * len(shape))

    in_specs = [qrow(A_WIDTH), qrow(A_WIDTH), srow(F_WIDTH), qrow(HEADS * HEAD_PAD),
                srow(HEADS * (NOPE + V_DIM)), srow(LANES),
                lay((A_HEADS, CHUNK, CHUNK)), lay((CHUNK, A_WIDTH)),
                full((F_WIDTH, F_WIDTH)), full((F_WIDTH, F_WIDTH)),
                pl.BlockSpec((qb, seq_t), lambda b, j: (j, 0)),
                pl.BlockSpec((qb, seq_t), lambda b, j: (j, 0))]
    args = [u, vn, zf, q, kv, kr, ws, bs_full, bdc, bds, ct, st]
    if cache is not None:
        assert n_seq == 1
        cckv, ckr, wukv = cache
        in_specs += [pl.BlockSpec((None, None, PAST, KV_LORA), lambda b, j: (b, layer, 0, 0)),
                     pl.BlockSpec((None, None, PAST, LANES), lambda b, j: (b, layer, 0, 0)),
                     lay((KV_LORA, HEADS * (NOPE + V_DIM)))]
        args += [cckv, ckr, wukv]
    return pl.pallas_call(
        functools.partial(_mix_kernel, has_cache=cache is not None, qb=qb, seq_t=seq_t, n_seq=n_seq),
        grid=(n_batch // n_seq, nq),
        in_specs=in_specs,
        out_specs=pl.BlockSpec((q_rows, D), lambda b, j: (b * nq + j, 0)),
        out_shape=jax.ShapeDtypeStruct((n_batch * seq_t, D), BF16),
        compiler_params=pltpu.CompilerParams(dimension_semantics=("parallel", "parallel"),
                                             vmem_limit_bytes=VMEM_LIMIT),
        name="mix_lat" if cache is not None else "mix_ctx",
    )(*args)


def _post_kernel(yc_ref, yl_ref, xc_ref, xl_ref, mod_ref, wout_ref, gpost_ref, gffn_ref, wr_ref, br_ref,
                 tri_ref, x1_ref, h2_ref, tab_ref, rw_ref, cnt_ref, carry_ref):
    i = pl.program_id(0)

    @pl.when(i == 0)
    def _():
        carry_ref[...] = jnp.zeros(carry_ref.shape, F32)

    y = jnp.dot(_pick(i, yc_ref, yl_ref), wout_ref[...].astype(BF16), preferred_element_type=F32)
    x1 = _pick(i, xc_ref, xl_ref) + _rms(y, gpost_ref[...] * mod_ref[2:3, :])
    x1_ref[...] = x1
    h2 = _rms(x1, gffn_ref[...] * (1.0 + mod_ref[4:5, :])) + mod_ref[3:4, :]
    _store_tiled(h2_ref, h2)

    wr = wr_ref[...]
    wr_hi = wr.astype(BF16)
    wr_lo = (wr - wr_hi.astype(F32)).astype(BF16)
    h_hi = h2.astype(BF16)
    h_lo = (h2 - h_hi.astype(F32)).astype(BF16)
    hi_terms = jnp.dot(h_hi, jnp.concatenate([wr_hi, wr_lo], axis=1), preferred_element_type=F32)
    logits = (hi_terms[:, :LANES] + jnp.dot(h_lo, wr_hi, preferred_element_type=F32)
              + hi_terms[:, LANES:]) + br_ref[...]

    lane = lax.broadcasted_iota(jnp.int32, (TB, LANES), 1)
    lane_f = lane.astype(F32)
    work = logits
    idx, val = [], []
    for _ in range(TOP_K):
        m = jnp.max(work, axis=-1, keepdims=True)
        ik = jnp.min(jnp.where(work == m, lane_f, float(LANES)), axis=-1, keepdims=True)
        idx.append(ik)
        val.append(m)
        work = jnp.where(lane_f == ik, NEG, work)
    ex = [jnp.exp(v - val[0]) for v in val]
    den = ex[0] + ex[1] + ex[2] + ex[3]

    onehot = jnp.zeros((TB, LANES), F32)
    for k in range(TOP_K):
        onehot = onehot + jnp.where(lane_f == idx[k], 1.0, 0.0)
    before = jnp.dot(tri_ref[...], onehot.astype(BF16), preferred_element_type=F32) + carry_ref[0:1, :]
    ri = jnp.zeros((TB, LANES), F32)
    rw = jnp.zeros((TB, LANES), F32)
    for k in range(TOP_K):
        rank_k = jnp.sum(jnp.where(lane_f == idx[k], before, 0.0), axis=-1, keepdims=True)
        ri = ri + jnp.where(lane == k, idx[k], 0.0) + jnp.where(lane == TOP_K + k, rank_k, 0.0)
        rw = rw + jnp.where(lane == k, ex[k] / den, 0.0)
    tab_ref[...] = ri.T[0:2 * TOP_K, :].astype(jnp.int32)
    rw_ref[...] = rw
    total = carry_ref[0:1, :] + jnp.sum(onehot, axis=0, keepdims=True)
    carry_ref[...] = jnp.broadcast_to(total, carry_ref.shape)
    cnt_ref[...] = jnp.broadcast_to(total, cnt_ref.shape).astype(jnp.int32)


def _post_call(layer, yc, yl, xc, xl, mods, w_out, g_post, g_ffn, w_r, b_r, tri):
    nb = N_TOK // TB
    lay = functools.partial(_layer_spec, layer)

    def tok(width):
        return pl.BlockSpec((TB, width), lambda i: (i, 0))

    def full(shape):
        return pl.BlockSpec(shape, lambda i: (0,) * len(shape))

    return pl.pallas_call(
        _post_kernel,
        grid=(nb,),
        in_specs=[_ctx_spec(D), _lat_spec(D), _ctx_spec(D), _lat_spec(D), _mod_spec(layer),
                  lay((D, D)), lay((1, D)), lay((1, D)), lay((D, LANES)), lay((1, LANES)),
                  full((TB, TB))],
        out_specs=(tok(D), pl.BlockSpec((TB * SUB, LANES), lambda i: (i, 0)),
                   pl.BlockSpec((None, 2 * TOP_K, TB), lambda i: (i, 0, 0)),
                   tok(LANES), full((8, LANES))),
        out_shape=(jax.ShapeDtypeStruct((N_TOK, D), F32),
                   jax.ShapeDtypeStruct((N_TOK * SUB, LANES), F32),
                   jax.ShapeDtypeStruct((nb, 2 * TOP_K, TB), jnp.int32),
                   jax.ShapeDtypeStruct((N_TOK, LANES), F32),
                   jax.ShapeDtypeStruct((8, LANES), jnp.int32)),
        scratch_shapes=[pltpu.VMEM((8, LANES), F32)],
        compiler_params=pltpu.CompilerParams(dimension_semantics=("arbitrary",),
                                             vmem_limit_bytes=VMEM_LIMIT),
        name="post_mix_router",
    )(yc, yl, xc, xl, mods, w_out, g_post, g_ffn, w_r, b_r, tri)


def _plan_kernel(cnt_ref, tab_ref, dest_ref, cend_ref, pend_ref, nu_ref, be_ref, first_ref, half_ref, nxt_ref,
                 nxt2_ref, slot_ref, start_s, next_s):
    nxt = jnp.int32(-1)
    for e in reversed(range(N_EXPERTS)):
        next_s[e] = nxt
        nxt = jnp.where(cnt_ref[0, e] > 0, jnp.int32(e), nxt)

    start = jnp.int32(0)
    run = jnp.int32(0)
    last = jnp.int32(0)
    for e in range(N_EXPERTS):
        c = cnt_ref[0, e]
        n_blk = lax.shift_right_logical(c + (RB - 1), RB.bit_length() - 1)
        start_s[e] = start
        cend_ref[e] = start * RB + c
        pend_ref[e] = (start + n_blk) * RB
        few = (c - (n_blk - 1) * RB) <= RB // 2
        n1 = next_s[e]
        n2 = jnp.where(n1 >= 0, next_s[jnp.maximum(n1, 0)], jnp.int32(-1))
        slot = lax.rem(run, W_SLOTS)

        def fill(j, carry, e=e, n1=n1, n2=n2, slot=slot, start=start, n_blk=n_blk, few=few):
            be_ref[j] = jnp.int32(e)
            first_ref[j] = (j == start).astype(jnp.int32)
            half_ref[j] = jnp.logical_and(j == start + n_blk - 1, few).astype(jnp.int32)
            nxt_ref[j] = n1
            nxt2_ref[j] = n2
            slot_ref[j] = slot
            return carry

        lax.fori_loop(start, start + n_blk, fill, 0)
        owns = n_blk > 0
        last = jnp.where(owns, jnp.int32(e), last)
        run = run + owns.astype(jnp.int32)
        start = start + n_blk
    nu_ref[0] = start

    def tail(j, carry):
        be_ref[j] = last
        first_ref[j] = jnp.int32(0)
        half_ref[j] = jnp.int32(0)
        nxt_ref[j] = jnp.int32(-1)
        nxt2_ref[j] = jnp.int32(-1)
        slot_ref[j] = jnp.int32(0)
        return carry

    lax.fori_loop(start, N_BLOCKS, tail, 0)

    idx = tab_ref[:, 0:TOP_K, :]
    base = jnp.zeros(idx.shape, jnp.int32)
    for e in range(N_EXPERTS):
        base = jnp.where(idx == e, start_s[e] * RB, base)
    dest_ref[...] = base + tab_ref[:, TOP_K:2 * TOP_K, :]


def _plan_call(cnt, tab):
    nb = N_TOK // TB
    smem = pl.BlockSpec(memory_space=pltpu.SMEM)
    blocks = jax.ShapeDtypeStruct((N_BLOCKS,), jnp.int32)
    return pl.pallas_call(
        _plan_kernel,
        in_specs=[smem, pl.BlockSpec(memory_space=pltpu.VMEM)],
        out_specs=(pl.BlockSpec(memory_space=pltpu.VMEM),) + (smem,) * 9,
        out_shape=(jax.ShapeDtypeStruct((nb, TOP_K, TB), jnp.int32),
                   jax.ShapeDtypeStruct((N_EXPERTS,), jnp.int32),
                   jax.ShapeDtypeStruct((N_EXPERTS,), jnp.int32),
                   jax.ShapeDtypeStruct((1,), jnp.int32),
                   blocks, blocks, blocks, blocks, blocks, blocks),
        scratch_shapes=[pltpu.SMEM((N_EXPERTS,), jnp.int32), pltpu.SMEM((N_EXPERTS,), jnp.int32)],
        name="plan_rows",
    )(cnt, tab)


def _dispatch_kernel(cend_ref, pend_ref, nu_ref, dest_ref, h2_ref, xs_ref, zero_ref, sem, zsem):
    i = pl.program_id(0)

    def zero_padding(act):
        for e in range(N_EXPERTS):
            c_end = cend_ref[e]
            aligned = lax.shift_left(lax.shift_right_logical(c_end + 7, 3), 3)
            for r in range(7):

                @pl.when(c_end + r < aligned)
                def _():
                    act(pltpu.make_async_copy(_tile_rows(zero_ref, 0), _tile_rows(xs_ref, c_end + r), zsem))

            n = pend_ref[e] - aligned
            off = aligned
            for size in (128, 64, 32, 16, 8):

                @pl.when((n & size) != 0)
                def _():
                    act(pltpu.make_async_copy(_tile_rows(zero_ref, 0, size), _tile_rows(xs_ref, off, size),
                                              zsem))

                off = off + (n & size)

        def tail(b, carry):
            act(pltpu.make_async_copy(zero_ref, _tile_rows(xs_ref, b * RB, RB), zsem))
            return carry

        lax.fori_loop(nu_ref[0], N_BLOCKS, tail, 0)

    @pl.when(i == 0)
    def _():
        zero_ref[...] = jnp.zeros(zero_ref.shape, F32)
        zero_padding(lambda cp: cp.start())

    def body(t, carry):
        for k in range(TOP_K):
            pltpu.make_async_copy(_tile_rows(h2_ref, t), _tile_rows(xs_ref, dest_ref[0, k * TB + t]),
                                  sem).start(priority=k % 2)
        return carry

    lax.fori_loop(0, TB, body, 0, unroll=8)
    for k in range(TOP_K):
        pltpu.make_async_copy(h2_ref, _tile_rows(xs_ref, 0, TB), sem).wait()

    @pl.when(i == pl.num_programs(0) - 1)
    def _():
        zero_padding(lambda cp: cp.wait())


def _dispatch_call(cend, pend, n_used, dest, h2):
    nb = N_TOK // TB
    grid_spec = pltpu.PrefetchScalarGridSpec(
        num_scalar_prefetch=3,
        grid=(nb,),
        in_specs=[
            pl.BlockSpec((None, 1, TOP_K * TB), lambda i, c, p, n: (i, 0, 0), memory_space=pltpu.SMEM),
            pl.BlockSpec((TB * SUB, LANES), lambda i, c, p, n: (i, 0)),
        ],
        out_specs=pl.BlockSpec(memory_space=pl.ANY),
        scratch_shapes=[pltpu.VMEM((RB * SUB, LANES), F32), pltpu.SemaphoreType.DMA,
                        pltpu.SemaphoreType.DMA],
    )
    return pl.pallas_call(
        _dispatch_kernel,
        grid_spec=grid_spec,
        out_shape=jax.ShapeDtypeStruct((N_ROWS * SUB, LANES), F32),
        compiler_params=pltpu.CompilerParams(dimension_semantics=("arbitrary",),
                                             vmem_limit_bytes=VMEM_LIMIT),
        name="dispatch_rows",
    )(cend, pend, n_used, dest, h2)


def _moe_kernel(be_ref, first_ref, half_ref, nxt_ref, nxt2_ref, slot_ref, nu_ref, xs_ref, wgu_hbm, bgu_ref,
                wdn_hbm, bdn_ref, o_ref, wgu_f, wdn_f, sems, *, layer):
    b = pl.program_id(0)
    used = b < nu_ref[0]

    def fetch(e, s):
        return (pltpu.make_async_copy(wgu_hbm.at[layer, e], wgu_f.at[s], sems.at[0, s]),
                pltpu.make_async_copy(wdn_hbm.at[layer, e], wdn_f.at[s], sems.at[1, s]))

    @pl.when(b == 0)
    def _():
        for cp in fetch(be_ref[0], 0):
            cp.start(priority=1)

        @pl.when(nxt_ref[0] >= 0)
        def _():
            for cp in fetch(nxt_ref[0], 1):
                cp.start(priority=1)

    @pl.when(first_ref[b] == 1)
    def _():
        s = slot_ref[b]
        for cp in fetch(be_ref[b], s):
            cp.wait()

        @pl.when(nxt2_ref[b] >= 0)
        def _():
            for cp in fetch(nxt2_ref[b], (s + 2) % W_SLOTS):
                cp.start(priority=1)

    def experts(rows):
        s = slot_ref[b]
        x = _load_tiled(xs_ref, rows).astype(BF16)
        gu = jnp.dot(x, wgu_f[s].astype(BF16), preferred_element_type=F32) + bgu_ref[...]
        g = jnp.minimum(gu[:, :D_FF], SWIGLU_LIMIT)
        l = jnp.clip(gu[:, D_FF:], -SWIGLU_LIMIT, SWIGLU_LIMIT)
        a = g * jax.nn.sigmoid(SWIGLU_ALPHA * g) * (l + 1.0)
        y = jnp.dot(a.astype(BF16), wdn_f[s].astype(BF16), preferred_element_type=F32) + bdn_ref[...]
        _store_tiled(o_ref, y)
        if rows < RB:
            o_ref[rows * SUB:, :] = jnp.zeros(((RB - rows) * SUB, LANES), o_ref.dtype)

    half = half_ref[b] == 1

    @pl.when(jnp.logical_and(used, jnp.logical_not(half)))
    def _():
        experts(RB)

    @pl.when(jnp.logical_and(used, half))
    def _():
        experts(RB // 2)

    @pl.when(jnp.logical_not(used))
    def _():
        o_ref[...] = jnp.zeros(o_ref.shape, o_ref.dtype)


def _moe_call(layer, tables, xs, w_gu, b_gu, w_dn, b_dn):
    def rows_in(b, be, fi, ha, nx, n2, sl, nu):
        return (jnp.minimum(b, nu[0] - 1), 0)

    def rows_out(b, be, fi, ha, nx, n2, sl, nu):
        return (b, 0)

    def expert(b, be, fi, ha, nx, n2, sl, nu):
        return (layer, be[b], 0, 0)

    grid_spec = pltpu.PrefetchScalarGridSpec(
        num_scalar_prefetch=7,
        grid=(N_BLOCKS,),
        in_specs=[
            pl.BlockSpec((RB * SUB, LANES), rows_in),
            pl.BlockSpec(memory_space=pl.ANY),
            pl.BlockSpec((None, None, 1, 2 * D_FF), expert),
            pl.BlockSpec(memory_space=pl.ANY),
            pl.BlockSpec((None, None, 1, D), expert),
        ],
        out_specs=pl.BlockSpec((RB * SUB, LANES), rows_out),
        scratch_shapes=[pltpu.VMEM((W_SLOTS, D, 2 * D_FF), F32), pltpu.VMEM((W_SLOTS, D_FF, D), F32),
                        pltpu.SemaphoreType.DMA((2, W_SLOTS))],
    )
    return pl.pallas_call(
        functools.partial(_moe_kernel, layer=layer),
        grid_spec=grid_spec,
        out_shape=jax.ShapeDtypeStruct((N_ROWS * SUB, LANES), F32),
        compiler_params=pltpu.CompilerParams(dimension_semantics=("arbitrary",),
                                             vmem_limit_bytes=VMEM_LIMIT),
        name="moe_experts",
    )(*tables, xs, w_gu, b_gu, w_dn, b_dn)


def _final_kernel(*refs):
    _combine_body(pl.program_id(0), *refs)


def _combine_body(i, dcur_ref, dnxt_ref, x1_ref, ys_ref, rw_ref, mod_ref, g_ref, oc_ref, ol_ref, buf, sems):
    nb = pl.num_programs(0)
    slot = i % 2

    def gather(dest_ref, s):
        def body(t, carry):
            for k in range(TOP_K):
                pltpu.make_async_copy(_tile_rows(ys_ref, dest_ref[0, k * TB + t]),
                                      _tile_rows(buf.at[s], k * TB + t), sems.at[s]).start(priority=k % 2)
            return carry

        lax.fori_loop(0, TB, body, 0, unroll=8)

    @pl.when(i == 0)
    def _():
        gather(dcur_ref, 0)

    @pl.when(i + 1 < nb)
    def _():
        gather(dnxt_ref, 1 - slot)

    pltpu.make_async_copy(_tile_rows(ys_ref, 0, TOP_K * TB), buf.at[slot], sems.at[slot]).wait()

    rw = rw_ref[...]
    y = jnp.zeros((TB, D), F32)
    for k in range(TOP_K):
        yk = jnp.concatenate([buf[slot, pl.ds(k * TB * SUB + s, TB, stride=SUB), :] for s in range(SUB)],
                             axis=1)
        y = y + yk * rw[:, k:k + 1]
    x2 = x1_ref[...] + _rms(y, g_ref[...] * mod_ref[5:6, :])

    @pl.when(i < N_CTX // TB)
    def _():
        oc_ref[...] = x2

    @pl.when(i >= N_CTX // TB)
    def _():
        ol_ref[...] = x2


def _final_call(layer, dest_flat, x1, ys, rw, mods, g_post_ffn):
    nb = N_TOK // TB
    return pl.pallas_call(
        _final_kernel,
        grid=(nb,),
        in_specs=[pl.BlockSpec((None, 1, TOP_K * TB), lambda i: (i, 0, 0), memory_space=pltpu.SMEM),
                  pl.BlockSpec((None, 1, TOP_K * TB), lambda i: (jnp.minimum(i + 1, nb - 1), 0, 0),
                               memory_space=pltpu.SMEM),
                  pl.BlockSpec((TB, D), lambda i: (i, 0)),
                  pl.BlockSpec(memory_space=pl.ANY),
                  pl.BlockSpec((TB, LANES), lambda i: (i, 0)),
                  _mod_spec(layer), _layer_spec(layer, (1, D))],
        out_specs=(_ctx_spec(D), _lat_spec(D)),
        out_shape=(jax.ShapeDtypeStruct((N_CTX, D), F32), jax.ShapeDtypeStruct((N_LAT, D), F32)),
        scratch_shapes=[pltpu.VMEM((2, TOP_K * TB * SUB, LANES), F32), pltpu.SemaphoreType.DMA((2,))],
        compiler_params=pltpu.CompilerParams(dimension_semantics=("arbitrary",),
                                             vmem_limit_bytes=VMEM_LIMIT),
        name="combine_residual",
    )(dest_flat, dest_flat, x1, ys, rw, mods, g_post_ffn)


def _dft_tables(t):
    j = np.arange(t, dtype=np.int64)
    ang = 2.0 * np.pi * ((j[:, None] * j[None, :]) % t) / t
    return (np.cos(ang) / math.sqrt(t)).astype(np.float32), (np.sin(ang) / math.sqrt(t)).astype(np.float32)


def _channel_dft():
    c = np.arange(F_GROUP_DIM, dtype=np.int64)
    ang = 2.0 * np.pi * ((c[:, None] * c[None, :]) % F_GROUP_DIM) / F_GROUP_DIM
    eye = np.eye(F_GROUPS)
    bdc = np.kron(eye, np.cos(ang)) / math.sqrt(F_GROUP_DIM)
    bds = np.kron(eye, np.sin(ang)) / math.sqrt(F_GROUP_DIM)
    return bdc.astype(np.float32), bds.astype(np.float32)


def _rope_tables():
    pos = np.arange(LAT_T)
    n = ROPE // 4
    inv_freq = np.power(np.float32(ROPE_BASE), -np.arange(n, dtype=np.float32) / np.float32(n))
    ang_r = (pos // GRID_W).astype(np.float32)[:, None] * inv_freq
    ang_c = (pos % GRID_W).astype(np.float32)[:, None] * inv_freq
    cos = np.concatenate([np.cos(ang_r), np.cos(ang_r), np.cos(ang_c), np.cos(ang_c),
                          np.ones((LAT_T, LANES - ROPE))], axis=1)
    sin = np.concatenate([-np.sin(ang_r), np.sin(ang_r), -np.sin(ang_c), np.sin(ang_c),
                          np.zeros((LAT_T, LANES - ROPE))], axis=1)
    return cos.astype(np.float32), sin.astype(np.float32)


def kernel(x_prompt, x_sample, cache_ckv, cache_krope, c, c_ctx, w_ada, b_ada, g_pre_mix, g_post_mix, g_pre_ffn, g_post_ffn, w_in, g_sgu, w_spatial, b_spatial, g_q, w_uq, g_kv, w_ukv, w_out, w_router, b_router, w_gate_up, b_gate_up, w_down, b_down):
    xc, xl = x_prompt.reshape(N_CTX, D), x_sample.reshape(N_LAT, D)

    cond = jnp.concatenate([c_ctx[None, :], c, jnp.zeros((5, D), F32)], axis=0)
    mods = _mod_call(cond.T, w_ada, b_ada.reshape(DEPTH, 1, 6 * D))
    mods = mods[:, :3].reshape(DEPTH, 3, 6, D)

    gmat = jnp.asarray(np.kron(np.eye(A_HEADS), np.full((A_HEAD_DIM, A_HEAD_DIM), 1.0 / A_HEAD_DIM)),
                       dtype=BF16)
    bdc_np, bds_np = _channel_dft()
    bdc, bds = jnp.asarray(bdc_np).astype(BF16), jnp.asarray(bds_np).astype(BF16)
    dft = {}
    for t in (CTX_T, LAT_T):
        ct_np, st_np = _dft_tables(t)
        dft[t] = (jnp.asarray(ct_np).astype(BF16), jnp.asarray(st_np).astype(BF16))
    cos_np, sin_np = _rope_tables()
    cos_t, sin_t = jnp.asarray(cos_np), jnp.asarray(sin_np)
    tri = jnp.asarray(np.tril(np.ones((TB, TB), np.float32), k=-1)).astype(BF16)

    b_gu = b_gate_up.reshape(DEPTH, N_EXPERTS, 1, 2 * D_FF)
    b_dn = b_down.reshape(DEPTH, N_EXPERTS, 1, D)
    w_uq_p = jnp.pad(w_uq.reshape(DEPTH, Q_LORA, HEADS, NOPE + ROPE),
                     ((0, 0), (0, 0), (0, 0), (0, HEAD_PAD - NOPE - ROPE)))
    w_uq_p = w_uq_p.reshape(DEPTH, Q_LORA, HEADS * HEAD_PAD).astype(BF16)
    w_ukv_b = w_ukv.astype(BF16)
    ws = w_spatial.astype(BF16)
    bs_full = jnp.repeat(jnp.swapaxes(b_spatial, 1, 2), A_HEAD_DIM, axis=2)
    ckr_p = jnp.pad(cache_krope, ((0, 0), (0, 0), (0, 0), (0, LANES - ROPE))).astype(BF16)
    w_r = jnp.pad(w_router, ((0, 0), (0, 0), (0, LANES - N_EXPERTS)))
    b_r = jnp.pad(b_router, ((0, 0), (0, LANES - N_EXPERTS)), constant_values=NEG)[:, None, :]
    row = lambda g: g[:, None, :]

    w_in_t = jnp.swapaxes(w_in, 1, 2)
    pre_params = (mods, row(g_pre_mix), w_in_t, row(g_sgu), gmat, row(g_q), w_uq_p, row(g_kv), w_ukv_b,
                  cos_t, sin_t)

    ckv_layers, krope_layers = [], []
    for i in range(DEPTH):
        u, vn, zf, q, kv, kr, ckv, zkr = _pre_call(i, xc, xl, pre_params)
        ckv_layers.append(ckv[:N_CTX].reshape(N_CTX_B, CTX_T, KV_LORA))
        krope_layers.append(zkr[:N_CTX, :ROPE].reshape(N_CTX_B, CTX_T, ROPE))

        mix_in = (u, vn, zf, q, kv, kr)
        y_ctx = _mix_call(i, mix_in, (ws, bs_full, bdc, bds) + dft[CTX_T], CTX_T, N_CTX_B, 0)
        y_lat = _mix_call(i, mix_in, (ws, bs_full, bdc, bds) + dft[LAT_T], LAT_T, N_LAT_B, N_CTX,
                          cache=(cache_ckv, ckr_p, w_ukv_b))

        x1, h2, tab, rw, cnt = _post_call(i, y_ctx, y_lat, xc, xl, mods, w_out,
                                          row(g_post_mix), row(g_pre_ffn), w_r, b_r, tri)

        dest, cend, pend, n_used, block_e, first, half, nxt, nxt2, slot = _plan_call(cnt, tab)
        dest_flat = dest.reshape(N_TOK // TB, 1, TOP_K * TB)
        xs = _dispatch_call(cend, pend, n_used, dest_flat, h2)
        ys = _moe_call(i, (block_e, first, half, nxt, nxt2, slot, n_used), xs, w_gate_up, b_gu, w_down, b_dn)
        xc, xl = _final_call(i, dest_flat, x1, ys, rw, mods, row(g_post_ffn))

    y_prompt = xc.reshape(N_CTX_B, CTX_T, D)
    y_sample = xl.reshape(N_LAT_B, LAT_T, D)
    return (y_prompt, y_sample, jnp.stack(ckv_layers, axis=1), jnp.stack(krope_layers, axis=1))
```
